```python
import math
import jax, jax.numpy as jnp
from jax import lax
import numpy as np

D_MODEL = 1024
BATCH = 32
SEQ = 2048
DEPTH = 1

D_FF = 2816
CONV_WIDTH = 1024
CONV_HEADS = 16
CONV_K = 3
SSM_WIDTH = 512
SSM_GROUP = 16
SSM_GROUPS = SSM_WIDTH // SSM_GROUP
SSM_STATE = 64
DT_MIN = 0.001
DT_MAX = 0.1
N_MOD = 9
EPS = 1e-6
IN_COLS = 3 * CONV_WIDTH + SSM_WIDTH + 2 * D_MODEL

kernel_name = "hybrid_shortconv_s5_macaron_block"


def rms_norm(x, g):
    xf = x.astype(jnp.float32)
    y = xf * lax.rsqrt(jnp.mean(xf * xf, axis=-1, keepdims=True) + EPS)
    return (y * g.astype(jnp.float32)).astype(x.dtype)


def modulate(h, shift, scale):
    return h * (1.0 + scale[:, None, :]) + shift[:, None, :]


def swiglu(h, w1, w3, w2):
    return (jax.nn.silu(h @ w1) * (h @ w3)) @ w2


def short_conv(v, w):
    return lax.conv_general_dilated(
        v, w[:, None, :].astype(v.dtype), window_strides=(1,),
        padding=((CONV_K - 1, 0),),
        dimension_numbers=("NWC", "WIO", "NWC"),
        feature_group_count=v.shape[-1])


def s5_ssm(u, a_re, a_im, b_re, b_im, c_re, c_im, log_dt):
    bsz, seq, _ = u.shape
    f32 = jnp.float32
    uf = u.astype(f32).reshape(bsz, seq, SSM_GROUPS, SSM_GROUP).transpose(1, 0, 2, 3)
    lam = lax.complex(a_re.astype(f32), a_im.astype(f32))
    dt = jnp.exp(log_dt.astype(f32))[:, None]
    a_bar = jnp.exp(lam * dt)
    b_mat = lax.complex(b_re.astype(f32), b_im.astype(f32))
    b_bar = ((a_bar - 1.0) / lam)[..., None] * b_mat
    c_mat = lax.complex(c_re.astype(f32), c_im.astype(f32))
    bu = jnp.einsum("sbgh,gph->sbgp", uf.astype(jnp.complex64), b_bar)
    a_seq = jnp.broadcast_to(a_bar, (seq, 1) + a_bar.shape)

    def combine(left, right):
        a_l, b_l = left
        a_r, b_r = right
        return (a_r * a_l, a_r * b_l + b_r)

    _, states = lax.associative_scan(combine, (a_seq, bu), axis=0)
    y = jnp.einsum("sbgp,ghp->sbgh", states, c_mat).real
    return y.transpose(1, 0, 2, 3).reshape(bsz, seq, SSM_WIDTH)


def hybrid_mixer(h, w_in, conv_w, w_conv_out, a_re, a_im, b_re, b_im, c_re, c_im,
                 log_dt, d_skip, w_glu, w_ssm_out, w_out):
    proj = h @ w_in
    cuts = [CONV_WIDTH, 2 * CONV_WIDTH, 3 * CONV_WIDTH,
            3 * CONV_WIDTH + SSM_WIDTH, 3 * CONV_WIDTH + SSM_WIDTH + D_MODEL]
    gate_b, gate_c, v, u, glog_a, glog_b = jnp.split(proj, cuts, axis=-1)
    y_a = (gate_b * short_conv(gate_c * v, conv_w)) @ w_conv_out
    s = s5_ssm(u, a_re, a_im, b_re, b_im, c_re, c_im, log_dt).astype(u.dtype) + d_skip * u
    s = jax.nn.gelu(s)
    s = s * jax.nn.sigmoid(s @ w_glu)
    y_b = s @ w_ssm_out
    merged = jax.nn.sigmoid(glog_a) * y_a + jax.nn.sigmoid(glog_b) * y_b
    return merged @ w_out


def _fwd_setup_inputs(seed: int = 0) -> dict:
    key = jax.random.key(seed)
    ks = jax.random.split(key, 32)
    L, D, G, P, H = DEPTH, D_MODEL, SSM_GROUPS, SSM_STATE, SSM_GROUP
    nrm = lambda k, shape, s: jax.random.normal(k, shape, jnp.float32) * s
    gain = lambda k: 1.0 + 0.02 * jax.random.normal(k, (L, D), jnp.float32)
    a_im_base = jnp.pi * jnp.arange(P, dtype=jnp.float32)
    return {
        "x": nrm(ks[0], (BATCH, SEQ, D), 1.0),
        "c": nrm(ks[1], (BATCH, D), 1.0),
        "w_ada": nrm(ks[2], (L, D, N_MOD * D), 0.5 * D ** -0.5),
        "b_ada": nrm(ks[3], (L, N_MOD * D), 0.01),
        "g_ffn1": gain(ks[4]),
        "w1_a": nrm(ks[5], (L, D, D_FF), D ** -0.5),
        "w3_a": nrm(ks[6], (L, D, D_FF), D ** -0.5),
        "w2_a": nrm(ks[7], (L, D_FF, D), D_FF ** -0.5),
        "g_mix": gain(ks[8]),
        "w_in": nrm(ks[9], (L, D, IN_COLS), D ** -0.5),
        "conv_w": nrm(ks[10], (L, CONV_K, CONV_WIDTH), CONV_K ** -0.5),
        "w_conv_out": nrm(ks[11], (L, CONV_WIDTH, D), CONV_WIDTH ** -0.5),
        "a_re": -0.5 + nrm(ks[12], (L, G, P), 0.01),
        "a_im": a_im_base + nrm(ks[13], (L, G, P), 0.01),
        "b_re": nrm(ks[14], (L, G, P, H), (2.0 * H) ** -0.5),
        "b_im": nrm(ks[15], (L, G, P, H), (2.0 * H) ** -0.5),
        "c_re": nrm(ks[16], (L, G, H, P), P ** -0.5),
        "c_im": nrm(ks[17], (L, G, H, P), P ** -0.5),
        "log_dt": jax.random.uniform(ks[18], (L, G), jnp.float32,
                                     math.log(DT_MIN), math.log(DT_MAX)),
        "d_skip": nrm(ks[19], (L, SSM_WIDTH), 1.0),
        "w_glu": nrm(ks[20], (L, SSM_WIDTH, SSM_WIDTH), SSM_WIDTH ** -0.5),
        "w_ssm_out": nrm(ks[21], (L, SSM_WIDTH, D), SSM_WIDTH ** -0.5),
        "w_out": nrm(ks[22], (L, D, D), D ** -0.5),
        "g_ffn2": gain(ks[23]),
        "w1_b": nrm(ks[24], (L, D, D_FF), D ** -0.5),
        "w3_b": nrm(ks[25], (L, D, D_FF), D ** -0.5),
        "w2_b": nrm(ks[26], (L, D_FF, D), D_FF ** -0.5),
        "g_final": 1.0 + 0.02 * jax.random.normal(ks[27], (D,), jnp.float32),
    }


def _fwd_reference(x, c, w_ada, b_ada, g_ffn1, w1_a, w3_a, w2_a, g_mix, w_in, conv_w,
              w_conv_out, a_re, a_im, b_re, b_im, c_re, c_im, log_dt, d_skip, w_glu,
              w_ssm_out, w_out, g_ffn2, w1_b, w3_b, w2_b, g_final):
    cond = jax.nn.silu(c)
    for l in range(DEPTH):
        mod = cond @ w_ada[l] + b_ada[l]
        sh1, sc1, gt1, sh2, sc2, gt2, sh3, sc3, gt3 = jnp.split(mod, N_MOD, axis=-1)
        h = modulate(rms_norm(x, g_ffn1[l]), sh1, sc1)
        x = x + 0.5 * gt1[:, None, :] * swiglu(h, w1_a[l], w3_a[l], w2_a[l])
        h = modulate(rms_norm(x, g_mix[l]), sh2, sc2)
        x = x + gt2[:, None, :] * hybrid_mixer(
            h, w_in[l], conv_w[l], w_conv_out[l], a_re[l], a_im[l], b_re[l], b_im[l],
            c_re[l], c_im[l], log_dt[l], d_skip[l], w_glu[l], w_ssm_out[l], w_out[l])
        h = modulate(rms_norm(x, g_ffn2[l]), sh3, sc3)
        x = x + 0.5 * gt3[:, None, :] * swiglu(h, w1_b[l], w3_b[l], w2_b[l])
    return rms_norm(x, g_final)


import jax as _jax
import jax.numpy as _jnp

TWIN_FORMAT = 'train_step'
FWD_PARAMS = ['x', 'c', 'w_ada', 'b_ada', 'g_ffn1', 'w1_a', 'w3_a', 'w2_a', 'g_mix', 'w_in', 'conv_w', 'w_conv_out', 'a_re', 'a_im', 'b_re', 'b_im', 'c_re', 'c_im', 'log_dt', 'd_skip', 'w_glu', 'w_ssm_out', 'w_out', 'g_ffn2', 'w1_b', 'w3_b', 'w2_b', 'g_final']
TWIN_WEIGHTS = ['w_ada', 'b_ada', 'g_ffn1', 'w1_a', 'w3_a', 'w2_a', 'g_mix', 'w_in', 'conv_w', 'w_conv_out', 'a_re', 'a_im', 'b_re', 'b_im', 'c_re', 'c_im', 'log_dt', 'd_skip', 'w_glu', 'w_ssm_out', 'w_out', 'g_ffn2', 'w1_b', 'w3_b', 'w2_b', 'g_final']
TWIN_DIFF_INPUT = 'x'
TWIN_INPUTS = ['x', 'c', 'w_ada', 'b_ada', 'g_ffn1', 'w1_a', 'w3_a', 'w2_a', 'g_mix', 'w_in', 'conv_w', 'w_conv_out', 'a_re', 'a_im', 'b_re', 'b_im', 'c_re', 'c_im', 'log_dt', 'd_skip', 'w_glu', 'w_ssm_out', 'w_out', 'g_ffn2', 'w1_b', 'w3_b', 'w2_b', 'g_final', 'loss_target', 'm_w_ada', 'm_b_ada', 'm_g_ffn1', 'm_w1_a', 'm_w3_a', 'm_w2_a', 'm_g_mix', 'm_w_in', 'm_conv_w', 'm_w_conv_out', 'm_a_re', 'm_a_im', 'm_b_re', 'm_b_im', 'm_c_re', 'm_c_im', 'm_log_dt', 'm_d_skip', 'm_w_glu', 'm_w_ssm_out', 'm_w_out', 'm_g_ffn2', 'm_w1_b', 'm_w3_b', 'm_w2_b', 'm_g_final', 'v_w_ada', 'v_b_ada', 'v_g_ffn1', 'v_w1_a', 'v_w3_a', 'v_w2_a', 'v_g_mix', 'v_w_in', 'v_conv_w', 'v_w_conv_out', 'v_a_re', 'v_a_im', 'v_b_re', 'v_b_im', 'v_c_re', 'v_c_im', 'v_log_dt', 'v_d_skip', 'v_w_glu', 'v_w_ssm_out', 'v_w_out', 'v_g_ffn2', 'v_w1_b', 'v_w3_b', 'v_w2_b', 'v_g_final']
TWIN_OUTPUTS = ['loss', 'grad_x', 'grad_w_ada', 'grad_b_ada', 'grad_g_ffn1', 'grad_w1_a', 'grad_w3_a', 'grad_w2_a', 'grad_g_mix', 'grad_w_in', 'grad_conv_w', 'grad_w_conv_out', 'grad_a_re', 'grad_a_im', 'grad_b_re', 'grad_b_im', 'grad_c_re', 'grad_c_im', 'grad_log_dt', 'grad_d_skip', 'grad_w_glu', 'grad_w_ssm_out', 'grad_w_out', 'grad_g_ffn2', 'grad_w1_b', 'grad_w3_b', 'grad_w2_b', 'grad_g_final', 'delta_w_ada', 'delta_b_ada', 'delta_g_ffn1', 'delta_w1_a', 'delta_w3_a', 'delta_w2_a', 'delta_g_mix', 'delta_w_in', 'delta_conv_w', 'delta_w_conv_out', 'delta_a_re', 'delta_a_im', 'delta_b_re', 'delta_b_im', 'delta_c_re', 'delta_c_im', 'delta_log_dt', 'delta_d_skip', 'delta_w_glu', 'delta_w_ssm_out', 'delta_w_out', 'delta_g_ffn2', 'delta_w1_b', 'delta_w3_b', 'delta_w2_b', 'delta_g_final', 'new_m_w_ada', 'new_m_b_ada', 'new_m_g_ffn1', 'new_m_w1_a', 'new_m_w3_a', 'new_m_w2_a', 'new_m_g_mix', 'new_m_w_in', 'new_m_conv_w', 'new_m_w_conv_out', 'new_m_a_re', 'new_m_a_im', 'new_m_b_re', 'new_m_b_im', 'new_m_c_re', 'new_m_c_im', 'new_m_log_dt', 'new_m_d_skip', 'new_m_w_glu', 'new_m_w_ssm_out', 'new_m_w_out', 'new_m_g_ffn2', 'new_m_w1_b', 'new_m_w3_b', 'new_m_w2_b', 'new_m_g_final', 'new_v_w_ada', 'new_v_b_ada', 'new_v_g_ffn1', 'new_v_w1_a', 'new_v_w3_a', 'new_v_w2_a', 'new_v_g_mix', 'new_v_w_in', 'new_v_conv_w', 'new_v_w_conv_out', 'new_v_a_re', 'new_v_a_im', 'new_v_b_re', 'new_v_b_im', 'new_v_c_re', 'new_v_c_im', 'new_v_log_dt', 'new_v_d_skip', 'new_v_w_glu', 'new_v_w_ssm_out', 'new_v_w_out', 'new_v_g_ffn2', 'new_v_w1_b', 'new_v_w3_b', 'new_v_w2_b', 'new_v_g_final']
TWIN_LEAF_KINDS = {'loss': 'loss', 'grad_x': 'grad_x', 'grad_w_ada': 'grad_w', 'grad_b_ada': 'grad_w', 'grad_g_ffn1': 'grad_w', 'grad_w1_a': 'grad_w', 'grad_w3_a': 'grad_w', 'grad_w2_a': 'grad_w', 'grad_g_mix': 'grad_w', 'grad_w_in': 'grad_w', 'grad_conv_w': 'grad_w', 'grad_w_conv_out': 'grad_w', 'grad_a_re': 'grad_w', 'grad_a_im': 'grad_w', 'grad_b_re': 'grad_w', 'grad_b_im': 'grad_w', 'grad_c_re': 'grad_w', 'grad_c_im': 'grad_w', 'grad_log_dt': 'grad_w', 'grad_d_skip': 'grad_w', 'grad_w_glu': 'grad_w', 'grad_w_ssm_out': 'grad_w', 'grad_w_out': 'grad_w', 'grad_g_ffn2': 'grad_w', 'grad_w1_b': 'grad_w', 'grad_w3_b': 'grad_w', 'grad_w2_b': 'grad_w', 'grad_g_final': 'grad_w', 'delta_w_ada': 'delta_w', 'delta_b_ada': 'delta_w', 'delta_g_ffn1': 'delta_w', 'delta_w1_a': 'delta_w', 'delta_w3_a': 'delta_w', 'delta_w2_a': 'delta_w', 'delta_g_mix': 'delta_w', 'delta_w_in': 'delta_w', 'delta_conv_w': 'delta_w', 'delta_w_conv_out': 'delta_w', 'delta_a_re': 'delta_w', 'delta_a_im': 'delta_w', 'delta_b_re': 'delta_w', 'delta_b_im': 'delta_w', 'delta_c_re': 'delta_w', 'delta_c_im': 'delta_w', 'delta_log_dt': 'delta_w', 'delta_d_skip': 'delta_w', 'delta_w_glu': 'delta_w', 'delta_w_ssm_out': 'delta_w', 'delta_w_out': 'delta_w', 'delta_g_ffn2': 'delta_w', 'delta_w1_b': 'delta_w', 'delta_w3_b': 'delta_w', 'delta_w2_b': 'delta_w', 'delta_g_final': 'delta_w', 'new_m_w_ada': 'new_m', 'new_m_b_ada': 'new_m', 'new_m_g_ffn1': 'new_m', 'new_m_w1_a': 'new_m', 'new_m_w3_a': 'new_m', 'new_m_w2_a': 'new_m', 'new_m_g_mix': 'new_m', 'new_m_w_in': 'new_m', 'new_m_conv_w': 'new_m', 'new_m_w_conv_out': 'new_m', 'new_m_a_re': 'new_m', 'new_m_a_im': 'new_m', 'new_m_b_re': 'new_m', 'new_m_b_im': 'new_m', 'new_m_c_re': 'new_m', 'new_m_c_im': 'new_m', 'new_m_log_dt': 'new_m', 'new_m_d_skip': 'new_m', 'new_m_w_glu': 'new_m', 'new_m_w_ssm_out': 'new_m', 'new_m_w_out': 'new_m', 'new_m_g_ffn2': 'new_m', 'new_m_w1_b': 'new_m', 'new_m_w3_b': 'new_m', 'new_m_w2_b': 'new_m', 'new_m_g_final': 'new_m', 'new_v_w_ada': 'new_v', 'new_v_b_ada': 'new_v', 'new_v_g_ffn1': 'new_v', 'new_v_w1_a': 'new_v', 'new_v_w3_a': 'new_v', 'new_v_w2_a': 'new_v', 'new_v_g_mix': 'new_v', 'new_v_w_in': 'new_v', 'new_v_conv_w': 'new_v', 'new_v_w_conv_out': 'new_v', 'new_v_a_re': 'new_v', 'new_v_a_im': 'new_v', 'new_v_b_re': 'new_v', 'new_v_b_im': 'new_v', 'new_v_c_re': 'new_v', 'new_v_c_im': 'new_v', 'new_v_log_dt': 'new_v', 'new_v_d_skip': 'new_v', 'new_v_w_glu': 'new_v', 'new_v_w_ssm_out': 'new_v', 'new_v_w_out': 'new_v', 'new_v_g_ffn2': 'new_v', 'new_v_w1_b': 'new_v', 'new_v_w3_b': 'new_v', 'new_v_w2_b': 'new_v', 'new_v_g_final': 'new_v'}


def _forward(args):
    return _fwd_reference(*[args[k] for k in FWD_PARAMS])


def _output_shape():
    out = _jax.eval_shape(lambda: _forward(_fwd_setup_inputs(0)))
    return out.shape, out.dtype

N_MICROBATCH = 1
ADAM_LR = 0.001
ADAM_B1 = 0.9
ADAM_B2 = 0.999
ADAM_EPS = 1e-08
ADAM_WD = 0.01
ADAM_STEP = 10
PER_EXAMPLE_BATCH_AXIS = {'x': 0, 'c': 0, 'loss_target': 0}
SHARED_INPUTS = []
_WEIGHT_DTYPES = {'w_ada': _jnp.float32, 'b_ada': _jnp.float32, 'g_ffn1': _jnp.float32, 'w1_a': _jnp.float32, 'w3_a': _jnp.float32, 'w2_a': _jnp.float32, 'g_mix': _jnp.float32, 'w_in': _jnp.float32, 'conv_w': _jnp.float32, 'w_conv_out': _jnp.float32, 'a_re': _jnp.float32, 'a_im': _jnp.float32, 'b_re': _jnp.float32, 'b_im': _jnp.float32, 'c_re': _jnp.float32, 'c_im': _jnp.float32, 'log_dt': _jnp.float32, 'd_skip': _jnp.float32, 'w_glu': _jnp.float32, 'w_ssm_out': _jnp.float32, 'w_out': _jnp.float32, 'g_ffn2': _jnp.float32, 'w1_b': _jnp.float32, 'w3_b': _jnp.float32, 'w2_b': _jnp.float32, 'g_final': _jnp.float32}
MOMENT_SCALE = {'w_ada': 5.532355e-02, 'b_ada': 9.669099e-02, 'g_ffn1': 3.991673e-02, 'w1_a': 1.774775e-02, 'w3_a': 1.713916e-02, 'w2_a': 2.846330e-02, 'g_mix': 9.466806e-02, 'w_in': 4.176324e-02, 'conv_w': 5.373278e-02, 'w_conv_out': 5.347745e-02, 'a_re': 2.509799e-03, 'a_im': 3.609838e-03, 'b_re': 1.534291e-03, 'b_im': 1.512671e-03, 'c_re': 2.067714e-03, 'c_im': 2.172946e-03, 'log_dt': 7.374715e-01, 'd_skip': 2.709403e-02, 'w_glu': 8.006086e-03, 'w_ssm_out': 1.583873e-02, 'w_out': 5.607005e-02, 'g_ffn2': 4.204337e-02, 'w1_b': 1.674707e-02, 'w3_b': 1.608665e-02, 'w2_b': 2.670388e-02, 'g_final': 6.393377e+01}


def _to_microbatches(a, axis):
    t = _jnp.moveaxis(a, axis, 0)
    t = t.reshape((N_MICROBATCH, t.shape[0] // N_MICROBATCH) + t.shape[1:])
    return _jnp.moveaxis(t, 1, axis + 1)


def setup_inputs(seed: int = 0) -> dict:
    inp = _fwd_setup_inputs(seed)
    key = _jax.random.fold_in(_jax.random.key(seed), 7919)
    shape, _ = _output_shape()
    out = dict(inp)
    out["loss_target"] = _jax.random.normal(_jax.random.fold_in(key, 0), shape, _jnp.float32)
    for i, name in enumerate(TWIN_WEIGHTS):
        w = inp[name].astype(_jnp.float32)
        if MOMENT_SCALE is None:
            s = _jnp.sqrt(_jnp.mean(_jnp.square(w)) + 1e-30)
        else:
            s = MOMENT_SCALE[name]
        km, kv = _jax.random.split(_jax.random.fold_in(key, i + 1))
        out[name] = w
        out["m_" + name] = s * _jax.random.normal(km, w.shape, _jnp.float32)
        out["v_" + name] = (s * s) * _jax.random.uniform(kv, w.shape, _jnp.float32, 0.5, 1.5)
    if N_MICROBATCH > 1:
        for name, axis in PER_EXAMPLE_BATCH_AXIS.items():
            out[name] = _to_microbatches(out[name], axis)
    return {'x': out['x'], 'c': out['c'], 'w_ada': out['w_ada'], 'b_ada': out['b_ada'], 'g_ffn1': out['g_ffn1'], 'w1_a': out['w1_a'], 'w3_a': out['w3_a'], 'w2_a': out['w2_a'], 'g_mix': out['g_mix'], 'w_in': out['w_in'], 'conv_w': out['conv_w'], 'w_conv_out': out['w_conv_out'], 'a_re': out['a_re'], 'a_im': out['a_im'], 'b_re': out['b_re'], 'b_im': out['b_im'], 'c_re': out['c_re'], 'c_im': out['c_im'], 'log_dt': out['log_dt'], 'd_skip': out['d_skip'], 'w_glu': out['w_glu'], 'w_ssm_out': out['w_ssm_out'], 'w_out': out['w_out'], 'g_ffn2': out['g_ffn2'], 'w1_b': out['w1_b'], 'w3_b': out['w3_b'], 'w2_b': out['w2_b'], 'g_final': out['g_final'], 'loss_target': out['loss_target'], 'm_w_ada': out['m_w_ada'], 'm_b_ada': out['m_b_ada'], 'm_g_ffn1': out['m_g_ffn1'], 'm_w1_a': out['m_w1_a'], 'm_w3_a': out['m_w3_a'], 'm_w2_a': out['m_w2_a'], 'm_g_mix': out['m_g_mix'], 'm_w_in': out['m_w_in'], 'm_conv_w': out['m_conv_w'], 'm_w_conv_out': out['m_w_conv_out'], 'm_a_re': out['m_a_re'], 'm_a_im': out['m_a_im'], 'm_b_re': out['m_b_re'], 'm_b_im': out['m_b_im'], 'm_c_re': out['m_c_re'], 'm_c_im': out['m_c_im'], 'm_log_dt': out['m_log_dt'], 'm_d_skip': out['m_d_skip'], 'm_w_glu': out['m_w_glu'], 'm_w_ssm_out': out['m_w_ssm_out'], 'm_w_out': out['m_w_out'], 'm_g_ffn2': out['m_g_ffn2'], 'm_w1_b': out['m_w1_b'], 'm_w3_b': out['m_w3_b'], 'm_w2_b': out['m_w2_b'], 'm_g_final': out['m_g_final'], 'v_w_ada': out['v_w_ada'], 'v_b_ada': out['v_b_ada'], 'v_g_ffn1': out['v_g_ffn1'], 'v_w1_a': out['v_w1_a'], 'v_w3_a': out['v_w3_a'], 'v_w2_a': out['v_w2_a'], 'v_g_mix': out['v_g_mix'], 'v_w_in': out['v_w_in'], 'v_conv_w': out['v_conv_w'], 'v_w_conv_out': out['v_w_conv_out'], 'v_a_re': out['v_a_re'], 'v_a_im': out['v_a_im'], 'v_b_re': out['v_b_re'], 'v_b_im': out['v_b_im'], 'v_c_re': out['v_c_re'], 'v_c_im': out['v_c_im'], 'v_log_dt': out['v_log_dt'], 'v_d_skip': out['v_d_skip'], 'v_w_glu': out['v_w_glu'], 'v_w_ssm_out': out['v_w_ssm_out'], 'v_w_out': out['v_w_out'], 'v_g_ffn2': out['v_g_ffn2'], 'v_w1_b': out['v_w1_b'], 'v_w3_b': out['v_w3_b'], 'v_w2_b': out['v_w2_b'], 'v_g_final': out['v_g_final']}


def _loss(weights, diff, rest, loss_target):
    with _jax.named_scope("forward"):
        args = {**rest, TWIN_DIFF_INPUT: diff, **{k: w.astype(_WEIGHT_DTYPES[k]) for k, w in weights.items()}}
        y = _forward(args)
    with _jax.named_scope("loss_head"):
        err = _jnp.square(y.astype(_jnp.float32) - loss_target)
        return 0.5 * _jnp.sum(_jnp.mean(err, axis=-1)) if err.ndim else 0.5 * err


def _adamw(w, g, m, v):
    m = ADAM_B1 * m + (1.0 - ADAM_B1) * g
    v = ADAM_B2 * v + (1.0 - ADAM_B2) * _jnp.square(g)
    m_hat = m / (1.0 - ADAM_B1 ** ADAM_STEP)
    v_hat = v / (1.0 - ADAM_B2 ** ADAM_STEP)
    delta = -ADAM_LR * (m_hat / (_jnp.sqrt(v_hat) + ADAM_EPS) + ADAM_WD * w)
    return delta, m, v


def reference(x, c, w_ada, b_ada, g_ffn1, w1_a, w3_a, w2_a, g_mix, w_in, conv_w, w_conv_out, a_re, a_im, b_re, b_im, c_re, c_im, log_dt, d_skip, w_glu, w_ssm_out, w_out, g_ffn2, w1_b, w3_b, w2_b, g_final, loss_target, m_w_ada, m_b_ada, m_g_ffn1, m_w1_a, m_w3_a, m_w2_a, m_g_mix, m_w_in, m_conv_w, m_w_conv_out, m_a_re, m_a_im, m_b_re, m_b_im, m_c_re, m_c_im, m_log_dt, m_d_skip, m_w_glu, m_w_ssm_out, m_w_out, m_g_ffn2, m_w1_b, m_w3_b, m_w2_b, m_g_final, v_w_ada, v_b_ada, v_g_ffn1, v_w1_a, v_w3_a, v_w2_a, v_g_mix, v_w_in, v_conv_w, v_w_conv_out, v_a_re, v_a_im, v_b_re, v_b_im, v_c_re, v_c_im, v_log_dt, v_d_skip, v_w_glu, v_w_ssm_out, v_w_out, v_g_ffn2, v_w1_b, v_w3_b, v_w2_b, v_g_final):
    given = dict(x=x, c=c, w_ada=w_ada, b_ada=b_ada, g_ffn1=g_ffn1, w1_a=w1_a, w3_a=w3_a, w2_a=w2_a, g_mix=g_mix, w_in=w_in, conv_w=conv_w, w_conv_out=w_conv_out, a_re=a_re, a_im=a_im, b_re=b_re, b_im=b_im, c_re=c_re, c_im=c_im, log_dt=log_dt, d_skip=d_skip, w_glu=w_glu, w_ssm_out=w_ssm_out, w_out=w_out, g_ffn2=g_ffn2, w1_b=w1_b, w3_b=w3_b, w2_b=w2_b, g_final=g_final, loss_target=loss_target, m_w_ada=m_w_ada, m_b_ada=m_b_ada, m_g_ffn1=m_g_ffn1, m_w1_a=m_w1_a, m_w3_a=m_w3_a, m_w2_a=m_w2_a, m_g_mix=m_g_mix, m_w_in=m_w_in, m_conv_w=m_conv_w, m_w_conv_out=m_w_conv_out, m_a_re=m_a_re, m_a_im=m_a_im, m_b_re=m_b_re, m_b_im=m_b_im, m_c_re=m_c_re, m_c_im=m_c_im, m_log_dt=m_log_dt, m_d_skip=m_d_skip, m_w_glu=m_w_glu, m_w_ssm_out=m_w_ssm_out, m_w_out=m_w_out, m_g_ffn2=m_g_ffn2, m_w1_b=m_w1_b, m_w3_b=m_w3_b, m_w2_b=m_w2_b, m_g_final=m_g_final, v_w_ada=v_w_ada, v_b_ada=v_b_ada, v_g_ffn1=v_g_ffn1, v_w1_a=v_w1_a, v_w3_a=v_w3_a, v_w2_a=v_w2_a, v_g_mix=v_g_mix, v_w_in=v_w_in, v_conv_w=v_conv_w, v_w_conv_out=v_w_conv_out, v_a_re=v_a_re, v_a_im=v_a_im, v_b_re=v_b_re, v_b_im=v_b_im, v_c_re=v_c_re, v_c_im=v_c_im, v_log_dt=v_log_dt, v_d_skip=v_d_skip, v_w_glu=v_w_glu, v_w_ssm_out=v_w_ssm_out, v_w_out=v_w_out, v_g_ffn2=v_g_ffn2, v_w1_b=v_w1_b, v_w3_b=v_w3_b, v_w2_b=v_w2_b, v_g_final=v_g_final)
    weights = {n: given[n] for n in TWIN_WEIGHTS}
    shared = {n: given[n] for n in SHARED_INPUTS}
    per_example = {n: given[n] for n in ['x', 'c']}
    grad_fn = _jax.value_and_grad(_loss, argnums=(0, 1))

    def one_microbatch(ex, loss_target):
        ex = dict(ex)
        diff = ex.pop(TWIN_DIFF_INPUT)
        return grad_fn(weights, diff, {**shared, **ex}, loss_target)

    if N_MICROBATCH == 1:
        loss, (grad_w, grad_x) = one_microbatch(per_example, given["loss_target"])
    else:
        def body(carry, xs):
            loss_sum, grad_sum = carry
            l_k, (gw_k, gx_k) = one_microbatch(xs[0], xs[1])
            with _jax.named_scope("update"):
                return (loss_sum + l_k, _jax.tree.map(_jnp.add, grad_sum, gw_k)), gx_k

        init = (_jnp.zeros((), _jnp.float32), _jax.tree.map(_jnp.zeros_like, weights))
        (loss, grad_w), grad_x = _jax.lax.scan(body, init, (per_example, given["loss_target"]))
    with _jax.named_scope("update"):
        delta_w, new_m, new_v = {}, {}, {}
        for n in TWIN_WEIGHTS:
            delta_w[n], new_m[n], new_v[n] = _adamw(weights[n], grad_w[n], given["m_" + n], given["v_" + n])
    return (loss, grad_x, *[grad_w[n] for n in TWIN_WEIGHTS], *[delta_w[n] for n in TWIN_WEIGHTS],
            *[new_m[n] for n in TWIN_WEIGHTS], *[new_v[n] for n in TWIN_WEIGHTS])
```

```python
import functools
import math

import jax
import jax.numpy as jnp
from jax import lax
from jax.experimental import pallas as pl
from jax.experimental.pallas import tpu as pltpu

F32 = jnp.float32
BF16 = jnp.bfloat16
SDS = jax.ShapeDtypeStruct
MESH = pl.DeviceIdType.MESH

D = 1024
DFF = 2816
CW = 1024
SW = 512
NG, NP, NH = 32, 64, 16
GP = NG * NP
NMOD = 9
EPS = 1e-6
N_CHIPS = 4
N_DEV = 8
SUB = 8
LANE = 128
SSM_SUPER = 4
VMEM_LIMIT = 50 * 1024 * 1024

LR, B1, B2, AEPS, WD, STEP = 0.001, 0.9, 0.999, 1e-08, 0.01, 10
BC1 = 1.0 - B1 ** STEP
BC2 = 1.0 - B2 ** STEP


def _cp(*sem):
    return pltpu.CompilerParams(dimension_semantics=sem or None, vmem_limit_bytes=VMEM_LIMIT)


def _pick_tile(n, cands):
    for t in cands:
        if t <= n and n % t == 0:
            return t
    return n


def _dot(a, b):
    return lax.dot_general(a, b, (((1,), (0,)), ((), ())), preferred_element_type=F32)


def _dot_nt(a, b):
    return lax.dot_general(a, b, (((1,), (1,)), ((), ())), preferred_element_type=F32)


def _dot_tn(a, b):
    return lax.dot_general(a, b, (((0,), (0,)), ((), ())), preferred_element_type=F32)


def _row(tm, width, col=0):
    return pl.BlockSpec((1, tm, width), lambda b, i, *_: (b, i, col))


def _seqvec(width):
    return pl.BlockSpec((1, 1, width), lambda b, *_: (b, 0, 0))


def _full2(shape):
    return pl.BlockSpec(shape, lambda *_: (0, 0))


def _sigmoid(x):
    return jax.nn.sigmoid(x)


def _mm(a, b, *, ta=False, tb=False, out_dtype=F32, name):
    if ta:
        kdim, m = a.shape
    else:
        m, kdim = a.shape
    n = b.shape[0] if tb else b.shape[1]
    tm = _pick_tile(m, (1024, 512, 256, 128))
    tn = _pick_tile(n, (1408, 1024, 512, 256, 128))
    tk = _pick_tile(kdim, (512, 256, 128))
    nk = kdim // tk

    def body(a_ref, b_ref, o_ref, acc_ref):
        k = pl.program_id(2)

        @pl.when(k == 0)
        def _():
            acc_ref[...] = jnp.zeros_like(acc_ref)

        av = a_ref[...].astype(BF16)
        bv = b_ref[...].astype(BF16)
        dn = (((0 if ta else 1,), (1 if tb else 0,)), ((), ()))
        acc_ref[...] += lax.dot_general(av, bv, dn, preferred_element_type=F32)

        @pl.when(k == nk - 1)
        def _():
            o_ref[...] = acc_ref[...].astype(out_dtype)

    a_spec = pl.BlockSpec((tk, tm), lambda i, j, k: (k, i)) if ta else pl.BlockSpec((tm, tk), lambda i, j, k: (i, k))
    b_spec = pl.BlockSpec((tn, tk), lambda i, j, k: (j, k)) if tb else pl.BlockSpec((tk, tn), lambda i, j, k: (k, j))
    return pl.pallas_call(
        body, name=name, grid=(m // tm, n // tn, nk),
        in_specs=[a_spec, b_spec],
        out_specs=pl.BlockSpec((tm, tn), lambda i, j, k: (i, j)),
        out_shape=SDS((m, n), out_dtype),
        scratch_shapes=[pltpu.VMEM((tm, tn), F32)],
        compiler_params=_cp("parallel", "parallel", "arbitrary"),
    )(a, b)


def _flat(a):
    return a.reshape(-1, a.shape[-1])


def _norm_mod(x, g, sh, sc, name):
    bsz, seq, dm = x.shape
    tm = _pick_tile(seq, (512, 256, 128))

    def body(x_ref, g_ref, sh_ref, sc_ref, o_ref):
        xf = x_ref[0]
        r = lax.rsqrt(jnp.mean(xf * xf, axis=-1, keepdims=True) + EPS)
        hn = xf * r * g_ref[...]
        o_ref[0] = (hn * (1.0 + sc_ref[0]) + sh_ref[0]).astype(BF16)

    return pl.pallas_call(
        body, name=name, grid=(bsz, seq // tm),
        in_specs=[_row(tm, dm), _full2((1, dm)), _seqvec(dm), _seqvec(dm)],
        out_specs=_row(tm, dm), out_shape=SDS((bsz, seq, dm), BF16),
        compiler_params=_cp("parallel", "parallel"),
    )(x, g, sh, sc)


def _swiglu_up(h, w1, w3, name):
    bsz, seq, dm = h.shape
    nf = w1.shape[1]
    tm = _pick_tile(seq, (512, 256, 128))
    tn = _pick_tile(nf, (1408, 512, 256, 128))

    def body(h_ref, w1_ref, w3_ref, a_ref, b_ref, hid_ref):
        hv = h_ref[0]
        a = _dot(hv, w1_ref[...])
        b = _dot(hv, w3_ref[...])
        a_ref[0] = a.astype(BF16)
        b_ref[0] = b.astype(BF16)
        hid_ref[0] = (a * _sigmoid(a) * b).astype(BF16)

    wspec = pl.BlockSpec((dm, tn), lambda n, b, i: (0, n))
    ospec = pl.BlockSpec((1, tm, tn), lambda n, b, i: (b, i, n))
    shp = SDS((bsz, seq, nf), BF16)
    return pl.pallas_call(
        body, name=name, grid=(nf // tn, bsz, seq // tm),
        in_specs=[pl.BlockSpec((1, tm, dm), lambda n, b, i: (b, i, 0)), wspec, wspec],
        out_specs=[ospec, ospec, ospec], out_shape=[shp, shp, shp],
        compiler_params=_cp("parallel", "parallel", "parallel"),
    )(h, w1, w3)


def _ffn_down(hid, w2, x, gt, name):
    bsz, seq, nf = hid.shape
    dm = w2.shape[1]
    tm = _pick_tile(seq, (512, 256, 128))
    tk = _pick_tile(nf, (1408, 512, 256, 128))
    nk = nf // tk

    def body(hid_ref, w2_ref, x_ref, gt_ref, f_ref, xo_ref, acc_ref):
        k = pl.program_id(2)

        @pl.when(k == 0)
        def _():
            acc_ref[...] = jnp.zeros_like(acc_ref)

        acc_ref[...] += _dot(hid_ref[0], w2_ref[...])

        @pl.when(k == nk - 1)
        def _():
            f = acc_ref[...]
            f_ref[0] = f
            xo_ref[0] = x_ref[0] + 0.5 * gt_ref[0] * f

    shp = SDS((bsz, seq, dm), F32)
    return pl.pallas_call(
        body, name=name, grid=(bsz, seq // tm, nk),
        in_specs=[pl.BlockSpec((1, tm, tk), lambda b, i, k: (b, i, k)),
                  pl.BlockSpec((tk, dm), lambda b, i, k: (k, 0)), _row(tm, dm), _seqvec(dm)],
        out_specs=[_row(tm, dm), _row(tm, dm)], out_shape=[shp, shp],
        scratch_shapes=[pltpu.VMEM((tm, dm), F32)],
        compiler_params=_cp("parallel", "parallel", "arbitrary"),
    )(hid, w2, x, gt)


def _ffn_bwd_hid(dxo, gt, f, a, b, w2, name):
    bsz, seq, dm = dxo.shape
    nf = a.shape[2]
    tm = _pick_tile(seq, (512, 256, 128))
    tn = _pick_tile(nf, (1408, 512, 256, 128))

    def body(dxo_ref, gt_ref, f_ref, a_ref, b_ref, w2_ref, dfs_ref, da_ref, db_ref, dgt_ref):
        i = pl.program_id(1)
        n = pl.program_id(2)

        @pl.when(n == 0)
        def _():
            dxo = dxo_ref[0]
            dfs_ref[0] = (0.5 * gt_ref[0] * dxo).astype(BF16)
            part = jnp.sum(0.5 * dxo * f_ref[0], axis=0, keepdims=True)

            @pl.when(i == 0)
            def _():
                dgt_ref[0] = part

            @pl.when(i > 0)
            def _():
                dgt_ref[0] += part

        dhid = _dot_nt(dfs_ref[0], w2_ref[...])
        av = a_ref[0].astype(F32)
        bv = b_ref[0].astype(F32)
        sg = _sigmoid(av)
        da_ref[0] = (dhid * bv * (sg * (1.0 + av * (1.0 - sg)))).astype(BF16)
        db_ref[0] = (dhid * (av * sg)).astype(BF16)

    hspec = pl.BlockSpec((1, tm, tn), lambda b, i, n: (b, i, n))
    return pl.pallas_call(
        body, name=name, grid=(bsz, seq // tm, nf // tn),
        in_specs=[_row(tm, dm), _seqvec(dm), _row(tm, dm), hspec, hspec,
                  pl.BlockSpec((tn, dm), lambda b, i, n: (n, 0))],
        out_specs=[_row(tm, dm), hspec, hspec, _seqvec(dm)],
        out_shape=[SDS((bsz, seq, dm), BF16), SDS((bsz, seq, nf), BF16), SDS((bsz, seq, nf), BF16),
                   SDS((bsz, 1, dm), F32)],
        compiler_params=_cp("arbitrary", "arbitrary", "arbitrary"),
    )(dxo, gt, f, a, b, w2)


def _dh_norm_bwd(pieces, w, tk, x, g, sc, dxo, name):
    bsz, seq, dm = x.shape
    tm = _pick_tile(seq, (512, 256, 128))
    nblk = [p.shape[2] // tk for p in pieces]
    offs = [sum(nblk[:j]) for j in range(len(pieces))]
    nk = sum(nblk)
    npc = len(pieces)

    def body(*refs):
        p_refs = refs[:npc]
        w_ref, x_ref, g_ref, sc_ref, dxo_ref, dx_ref, dsh_ref, dsc_ref, dg_ref, acc_ref = refs[npc:]
        b = pl.program_id(0)
        i = pl.program_id(1)
        k = pl.program_id(2)

        @pl.when(k == 0)
        def _():
            acc_ref[...] = jnp.zeros_like(acc_ref)

        for j in range(npc):
            @pl.when((k >= offs[j]) & (k < offs[j] + nblk[j]))
            def _(j=j):
                acc_ref[...] += _dot_nt(p_refs[j][0], w_ref[...])

        @pl.when(k == nk - 1)
        def _():
            dh = acc_ref[...]
            xf = x_ref[0]
            gv = g_ref[...]
            r = lax.rsqrt(jnp.mean(xf * xf, axis=-1, keepdims=True) + EPS)
            xhat = xf * r
            dhn = dh * (1.0 + sc_ref[0])
            p_sh = jnp.sum(dh, axis=0, keepdims=True)
            p_sc = jnp.sum(dh * (xhat * gv), axis=0, keepdims=True)
            p_g = jnp.sum(dhn * xhat, axis=0, keepdims=True)
            dxh = dhn * gv
            dx_ref[0] = dxo_ref[0] + r * (dxh - xhat * jnp.mean(dxh * xhat, axis=-1, keepdims=True))

            @pl.when(i == 0)
            def _():
                dsh_ref[0] = p_sh
                dsc_ref[0] = p_sc

            @pl.when(i > 0)
            def _():
                dsh_ref[0] += p_sh
                dsc_ref[0] += p_sc

            @pl.when((i == 0) & (b == 0))
            def _():
                dg_ref[...] = p_g

            @pl.when((i > 0) | (b > 0))
            def _():
                dg_ref[...] += p_g

    def pspec(j):
        return pl.BlockSpec((1, tm, tk), lambda b, i, k: (b, i, jnp.clip(k - offs[j], 0, nblk[j] - 1)))

    return pl.pallas_call(
        body, name=name, grid=(bsz, seq // tm, nk),
        in_specs=[pspec(j) for j in range(npc)] + [
            pl.BlockSpec((dm, tk), lambda b, i, k: (0, k)), _row(tm, dm), _full2((1, dm)), _seqvec(dm), _row(tm, dm)],
        out_specs=[_row(tm, dm), _seqvec(dm), _seqvec(dm), _full2((1, dm))],
        out_shape=[SDS((bsz, seq, dm), F32), SDS((bsz, 1, dm), F32), SDS((bsz, 1, dm), F32), SDS((1, dm), F32)],
        scratch_shapes=[pltpu.VMEM((tm, dm), F32)],
        compiler_params=_cp("arbitrary", "arbitrary", "arbitrary"),
    )(*pieces, w, x, g, sc, dxo)


HALO = 16


def _conv_core(gc, v, gch, vh, w, first):
    cv = gc * v
    halo = jnp.where(first, 0.0, gch * vh)
    ext = jnp.concatenate([halo, cv], axis=0)
    cv1 = pltpu.roll(ext, 1, 0)[HALO:]
    cv2 = pltpu.roll(ext, 2, 0)[HALO:]
    conv = w[0:1] * cv2 + w[1:2] * cv1 + w[2:3] * cv
    return cv, cv1, cv2, conv


def _prev_halo(tm, col):
    return pl.BlockSpec((1, HALO, CW), lambda b, i, *_: (b, jnp.maximum(i * (tm // HALO) - 1, 0), col))


def _next_halo(tm, seq, col):
    return pl.BlockSpec((1, HALO, CW), lambda b, i, *_: (b, jnp.minimum((i + 1) * (tm // HALO), seq // HALO - 1), col))


def _conv_fwd(p1, convw8, name):
    bsz, seq, _ = p1.shape
    tm = _pick_tile(seq, (512, 256, 128))

    def body(gb_ref, gc_ref, v_ref, gch_ref, vh_ref, w_ref, o_ref):
        first = pl.program_id(1) == 0
        _, _, _, conv = _conv_core(gc_ref[0], v_ref[0], gch_ref[0], vh_ref[0], w_ref[...], first)
        o_ref[0] = (gb_ref[0] * conv).astype(BF16)

    return pl.pallas_call(
        body, name=name, grid=(bsz, seq // tm),
        in_specs=[_row(tm, CW, 0), _row(tm, CW, 1), _row(tm, CW, 2), _prev_halo(tm, 1), _prev_halo(tm, 2),
                  _full2((8, CW))],
        out_specs=_row(tm, CW), out_shape=SDS((bsz, seq, CW), BF16),
        compiler_params=_cp("parallel", "parallel"),
    )(p1, p1, p1, p1, p1, convw8)


def _conv_bwd(dya, wco, p1, convw8, name):
    bsz, seq, _ = p1.shape
    tm = _pick_tile(seq, (512, 256, 128))
    nt = seq // tm
    ext_rows = tm + HALO

    def body(dya_ref, dyan_ref, wco_ref, gb_ref, gbn_ref, gc_ref, v_ref, gch_ref, vh_ref, w_ref, dp_ref, dw_ref):
        b = pl.program_id(0)
        i = pl.program_id(1)
        w = w_ref[...]
        cv, cv1, cv2, conv = _conv_core(gc_ref[0], v_ref[0], gch_ref[0], vh_ref[0], w, i == 0)
        dya_ext = jnp.concatenate([dya_ref[0], dyan_ref[0]], axis=0)
        dyain_ext = _dot_nt(dya_ext, wco_ref[...])
        gb = gb_ref[0]
        gb_ext = jnp.concatenate([gb, gbn_ref[0]], axis=0)
        rows = lax.broadcasted_iota(jnp.int32, (ext_rows, 1), 0)
        dconv_ext = jnp.where((rows < tm) | (i < nt - 1), dyain_ext * gb_ext, 0.0)
        dconv = dconv_ext[:tm]
        dconv1 = pltpu.roll(dconv_ext, ext_rows - 1, 0)[:tm]
        dconv2 = pltpu.roll(dconv_ext, ext_rows - 2, 0)[:tm]
        dcv = w[2:3] * dconv + w[1:2] * dconv1 + w[0:1] * dconv2
        dp_ref[0, :, 0:CW] = (dyain_ext[:tm] * conv).astype(BF16)
        dp_ref[0, :, CW:2 * CW] = (dcv * v_ref[0]).astype(BF16)
        dp_ref[0, :, 2 * CW:3 * CW] = (dcv * gc_ref[0]).astype(BF16)
        g0 = jnp.sum(dconv * cv2, axis=0, keepdims=True)
        g1 = jnp.sum(dconv * cv1, axis=0, keepdims=True)
        g2 = jnp.sum(dconv * cv, axis=0, keepdims=True)
        upd = jnp.concatenate([g0, g1, g2, jnp.zeros((5, CW), F32)], axis=0)

        @pl.when((i == 0) & (b == 0))
        def _():
            dw_ref[...] = upd

        @pl.when((i > 0) | (b > 0))
        def _():
            dw_ref[...] += upd

    return pl.pallas_call(
        body, name=name, grid=(bsz, seq // tm),
        in_specs=[_row(tm, CW), _next_halo(tm, seq, 0), _full2((CW, D)),
                  _row(tm, CW, 0), _next_halo(tm, seq, 0), _row(tm, CW, 1), _row(tm, CW, 2),
                  _prev_halo(tm, 1), _prev_halo(tm, 2), _full2((8, CW))],
        out_specs=[_row(tm, 3 * CW), _full2((8, CW))],
        out_shape=[SDS((bsz, seq, 3 * CW), BF16), SDS((8, CW), F32)],
        compiler_params=_cp("arbitrary", "arbitrary"),
    )(dya, dya, wco, p1, p1, p1, p1, p1, p1, convw8)


def _disc(are, aim, ldt, bre, bim):
    dt = jnp.exp(ldt)
    mag = jnp.exp(are * dt)
    ang = aim * dt
    abr = mag * jnp.cos(ang)
    abi = mag * jnp.sin(ang)
    nr = abr - 1.0
    den = are * are + aim * aim
    cr = (nr * are + abi * aim) / den
    ci = (abi * are - nr * aim) / den
    return abr, abi, cr * bre - ci * bim, cr * bim + ci * bre


def _ssm_disc(are, aim, ldt, bre_t, bim_t):
    def body(are_ref, aim_ref, ldt_ref, bre_ref, bim_ref, abr_ref, abi_ref, bbr_ref, bbi_ref):
        abr, abi, bbr, bbi = _disc(are_ref[...], aim_ref[...], ldt_ref[...], bre_ref[...], bim_ref[...])
        abr_ref[...] = abr
        abi_ref[...] = abi
        bbr_ref[...] = bbr
        bbi_ref[...] = bbi

    v1, vh = SDS((1, GP), F32), SDS((NH, GP), F32)
    return pl.pallas_call(body, name="ssm_disc", out_shape=[v1, v1, vh, vh], compiler_params=_cp())(
        are, aim, ldt, bre_t, bim_t)


def _ssm_disc_bwd(are, aim, ldt, bre_t, bim_t, dabr, dabi, dbbr, dbbi):
    def body(are_ref, aim_ref, ldt_ref, bre_ref, bim_ref, g0, g1, g2, g3, o0, o1, o2, o3, o4):
        prim = (are_ref[...], aim_ref[...], ldt_ref[...], bre_ref[...], bim_ref[...])
        _, vjp = jax.vjp(_disc, *prim)
        d_are, d_aim, d_ldt, d_bre, d_bim = vjp((g0[...], g1[...], g2[...], g3[...]))
        o0[...] = d_are
        o1[...] = d_aim
        o2[...] = d_ldt
        o3[...] = d_bre
        o4[...] = d_bim

    v1, vh = SDS((1, GP), F32), SDS((NH, GP), F32)
    return pl.pallas_call(body, name="ssm_disc_bwd", out_shape=[v1, v1, v1, vh, vh], compiler_params=_cp())(
        are, aim, ldt, bre_t, bim_t, dabr, dabi, dbbr, dbbi)


def _scan_chunk(buf_ref, nt, bsz, ar, ai, init_r, init_i, reverse):
    nsub = SUB // bsz
    row = lax.broadcasted_iota(jnp.int32, (SUB, GP), 0)
    shift = ((SUB - bsz) if reverse else bsz) % SUB
    order = list(range(nsub - 1, -1, -1)) if reverse else list(range(nsub))

    def step(j, carry):
        pr, pi = carry
        jj = (nt - 1 - j) if reverse else j
        off = pl.multiple_of(jj * SUB, SUB)
        br = buf_ref[pl.ds(off, SUB), 0:GP]
        bi = buf_ref[pl.ds(off, SUB), GP:2 * GP]
        nr, ni = pr, pi
        for s in order:
            sr, si = nr, ni
            if shift:
                sr = pltpu.roll(sr, shift, 0)
                si = pltpu.roll(si, shift, 0)
            cr = ar * sr - ai * si + br
            ci = ar * si + ai * sr + bi
            if nsub == 1:
                nr, ni = cr, ci
            else:
                m = (row >= s * bsz) & (row < (s + 1) * bsz)
                nr = jnp.where(m, cr, nr)
                ni = jnp.where(m, ci, ni)
        buf_ref[pl.ds(off, SUB), 0:GP] = nr
        buf_ref[pl.ds(off, SUB), GP:2 * GP] = ni
        return nr, ni

    return lax.fori_loop(0, nt, step, (init_r, init_i))


def _ssm_chunk_rows(total_rows, bsz):
    return min(total_rows, 64 * bsz)


def _ssm_fwd(u_tm, wb, wct, ar8, ai8, bsz, name):
    rt = u_tm.shape[0]
    r = _ssm_chunk_rows(rt, bsz)
    nt = r // SUB

    def body(u_ref, wb_hbm, wct_hbm, ar_ref, ai_ref, x_ref, y_ref, wb_ref, wct_ref, st_ref):
        @pl.when(pl.program_id(0) == 0)
        def _():
            pltpu.sync_copy(wb_hbm, wb_ref)
            pltpu.sync_copy(wct_hbm, wct_ref)
            st_ref[...] = jnp.zeros_like(st_ref)

        x_ref[...] = _dot(u_ref[...].astype(BF16), wb_ref[...])
        fr, fi = _scan_chunk(x_ref, nt, bsz, ar_ref[...], ai_ref[...], st_ref[:, 0:GP], st_ref[:, GP:2 * GP], False)
        st_ref[:, 0:GP] = fr
        st_ref[:, GP:2 * GP] = fi
        y_ref[...] = _dot_nt(x_ref[...].astype(BF16), wct_ref[...])

    anyspec = pl.BlockSpec(memory_space=pl.ANY)
    return pl.pallas_call(
        body, name=name, grid=(rt // r,),
        in_specs=[pl.BlockSpec((r, SW), lambda i: (i, 0)), anyspec, anyspec, _full2((SUB, GP)), _full2((SUB, GP))],
        out_specs=[pl.BlockSpec((r, 2 * GP), lambda i: (i, 0)), pl.BlockSpec((r, SW), lambda i: (i, 0))],
        out_shape=[SDS((rt, 2 * GP), F32), SDS((rt, SW), F32)],
        scratch_shapes=[pltpu.VMEM((SW, 2 * GP), BF16), pltpu.VMEM((SW, 2 * GP), BF16), pltpu.VMEM((SUB, 2 * GP), F32)],
        compiler_params=_cp("arbitrary"),
    )(u_tm, wb, wct, ar8, ai8)


def _ssm_bwd(dy_tm, u_tm, xs, wb, wct, ar8, ai8, bsz, name):
    rt = u_tm.shape[0]
    r = _ssm_chunk_rows(rt, bsz)
    nt = r // SUB
    nc = rt // r
    ub = SW // SSM_SUPER
    sb = GP // SSM_SUPER

    def body(dy_ref, u_ref, x_ref, xh_ref, wb_hbm, wct_hbm, ar_ref, ai_ref,
             du_ref, dwb_hbm, dwct_hbm, dar_ref, dai_ref, wb_ref, wct_ref, g_ref, st_ref, awb_ref, awct_ref):
        i = pl.program_id(0)

        @pl.when(i == 0)
        def _():
            pltpu.sync_copy(wb_hbm, wb_ref)
            pltpu.sync_copy(wct_hbm, wct_ref)
            st_ref[...] = jnp.zeros_like(st_ref)
            awb_ref[...] = jnp.zeros_like(awb_ref)
            awct_ref[...] = jnp.zeros_like(awct_ref)
            dar_ref[...] = jnp.zeros_like(dar_ref)
            dai_ref[...] = jnp.zeros_like(dai_ref)

        dyb = dy_ref[...].astype(BF16)
        g_ref[...] = _dot(dyb, wct_ref[...])
        ar = ar_ref[...]
        ai = ai_ref[...]
        fr, fi = _scan_chunk(g_ref, nt, bsz, ar, -ai, st_ref[:, 0:GP], st_ref[:, GP:2 * GP], True)
        st_ref[:, 0:GP] = fr
        st_ref[:, GP:2 * GP] = fi

        gb = g_ref[...].astype(BF16)
        du_ref[...] = _dot_nt(gb, wb_ref[...])
        ub16 = u_ref[...].astype(BF16)
        xb16 = x_ref[...].astype(BF16)
        for s in range(SSM_SUPER):
            us = ub16[:, s * ub:(s + 1) * ub]
            ds = dyb[:, s * ub:(s + 1) * ub]
            for half in range(2):
                cols = slice(half * GP + s * sb, half * GP + (s + 1) * sb)
                ocols = slice(half * sb, (half + 1) * sb)
                awb_ref[s * ub:(s + 1) * ub, ocols] += _dot_tn(us, gb[:, cols])
                awct_ref[s * ub:(s + 1) * ub, ocols] += _dot_tn(ds, xb16[:, cols])

        gr = g_ref[:, 0:GP]
        gi = g_ref[:, GP:2 * GP]
        xsr = pltpu.roll(x_ref[:, 0:GP], bsz, 0)
        xsi = pltpu.roll(x_ref[:, GP:2 * GP], bsz, 0)
        inner = lax.broadcasted_iota(jnp.int32, (r, 1), 0) >= bsz
        t_r = jnp.where(inner, gr * xsr + gi * xsi, 0.0)
        t_i = jnp.where(inner, gi * xsr - gr * xsi, 0.0)
        acc_r = jnp.sum(t_r.reshape(nt, SUB, GP), axis=0)
        acc_i = jnp.sum(t_i.reshape(nt, SUB, GP), axis=0)
        hr = xh_ref[:, 0:GP]
        hi = xh_ref[:, GP:2 * GP]
        if bsz % SUB:
            hr = pltpu.roll(hr, bsz, 0)
            hi = pltpu.roll(hi, bsz, 0)
        edge = (lax.broadcasted_iota(jnp.int32, (SUB, 1), 0) < bsz) & (i < nc - 1)
        g0r = g_ref[0:SUB, 0:GP]
        g0i = g_ref[0:SUB, GP:2 * GP]
        dar_ref[...] += acc_r + jnp.where(edge, g0r * hr + g0i * hi, 0.0)
        dai_ref[...] += acc_i + jnp.where(edge, g0i * hr - g0r * hi, 0.0)

        @pl.when(i == nc - 1)
        def _():
            pltpu.sync_copy(awb_ref, dwb_hbm)
            pltpu.sync_copy(awct_ref, dwct_hbm)

    anyspec = pl.BlockSpec(memory_space=pl.ANY)
    rev = lambda i: (nc - 1 - i, 0)
    wshape = (SW, 2 * sb)
    return pl.pallas_call(
        body, name=name, grid=(nc,),
        in_specs=[pl.BlockSpec((r, SW), rev), pl.BlockSpec((r, SW), rev), pl.BlockSpec((r, 2 * GP), rev),
                  pl.BlockSpec((SUB, 2 * GP), lambda i: (jnp.maximum((nc - 1 - i) * nt - 1, 0), 0)),
                  anyspec, anyspec, _full2((SUB, GP)), _full2((SUB, GP))],
        out_specs=[pl.BlockSpec((r, SW), rev), anyspec, anyspec, _full2((SUB, GP)), _full2((SUB, GP))],
        out_shape=[SDS((rt, SW), F32), SDS(wshape, F32), SDS(wshape, F32), SDS((SUB, GP), F32), SDS((SUB, GP), F32)],
        scratch_shapes=[pltpu.VMEM((SW, 2 * GP), BF16), pltpu.VMEM((SW, 2 * GP), BF16),
                        pltpu.VMEM((r, 2 * GP), F32), pltpu.VMEM((SUB, 2 * GP), F32),
                        pltpu.VMEM(wshape, F32), pltpu.VMEM(wshape, F32)],
        compiler_params=_cp("arbitrary"),
    )(dy_tm, u_tm, xs, xs, wb, wct, ar8, ai8)


GELU_C = math.sqrt(2.0 / math.pi)


def _gelu(x):
    return 0.5 * x * (1.0 + jnp.tanh(GELU_C * (x + 0.044715 * x * x * x)))


def _gelu_grad(x):
    th = jnp.tanh(GELU_C * (x + 0.044715 * x * x * x))
    return 0.5 * (1.0 + th) + 0.5 * x * (1.0 - th * th) * GELU_C * (1.0 + 3.0 * 0.044715 * x * x)


def _ssm_post(ys, u, dskip, wglu, wso, name):
    bsz, seq, _ = ys.shape
    tm = _pick_tile(seq, (512, 256, 128))

    def body(ys_ref, u_ref, d_ref, wg_ref, wo_ref, s0_ref, z_ref, s1_ref, s2_ref, yb_ref):
        s0 = ys_ref[0] + d_ref[...] * u_ref[0]
        s1 = _gelu(s0)
        s1b = s1.astype(BF16)
        z = _dot(s1b, wg_ref[...])
        s2b = (s1 * _sigmoid(z)).astype(BF16)
        s0_ref[0] = s0
        z_ref[0] = z
        s1_ref[0] = s1b
        s2_ref[0] = s2b
        yb_ref[0] = _dot(s2b, wo_ref[...])

    return pl.pallas_call(
        body, name=name, grid=(bsz, seq // tm),
        in_specs=[_row(tm, SW), _row(tm, SW), _full2((1, SW)), _full2((SW, SW)), _full2((SW, D))],
        out_specs=[_row(tm, SW), _row(tm, SW), _row(tm, SW), _row(tm, SW), _row(tm, D)],
        out_shape=[SDS((bsz, seq, SW), F32), SDS((bsz, seq, SW), F32), SDS((bsz, seq, SW), BF16),
                   SDS((bsz, seq, SW), BF16), SDS((bsz, seq, D), F32)],
        compiler_params=_cp("parallel", "parallel"),
    )(ys, u, dskip, wglu, wso)


def _ssm_post_bwd(dyb, s0, z, u, dskip, wglu, wso, name):
    bsz, seq, _ = s0.shape
    tm = _pick_tile(seq, (512, 256, 128))

    def body(dyb_ref, s0_ref, z_ref, u_ref, wg_ref, wo_ref, ds0_ref, dz_ref, dd_ref):
        b = pl.program_id(0)
        i = pl.program_id(1)
        ds2 = _dot_nt(dyb_ref[0], wo_ref[...])
        s0 = s0_ref[0]
        s1 = _gelu(s0)
        sg = _sigmoid(z_ref[0])
        dz = ds2 * s1 * sg * (1.0 - sg)
        dzb = dz.astype(BF16)
        ds1 = ds2 * sg + _dot_nt(dzb, wg_ref[...])
        ds0 = ds1 * _gelu_grad(s0)
        ds0_ref[0] = ds0
        dz_ref[0] = dzb
        part = jnp.sum(ds0 * u_ref[0], axis=0, keepdims=True)

        @pl.when((i == 0) & (b == 0))
        def _():
            dd_ref[...] = part

        @pl.when((i > 0) | (b > 0))
        def _():
            dd_ref[...] += part

    del dskip
    return pl.pallas_call(
        body, name=name, grid=(bsz, seq // tm),
        in_specs=[_row(tm, D), _row(tm, SW), _row(tm, SW), _row(tm, SW), _full2((SW, SW)), _full2((SW, D))],
        out_specs=[_row(tm, SW), _row(tm, SW), _full2((1, SW))],
        out_shape=[SDS((bsz, seq, SW), F32), SDS((bsz, seq, SW), BF16), SDS((1, SW), F32)],
        compiler_params=_cp("arbitrary", "arbitrary"),
    )(dyb, s0, z, u, wglu, wso)


def _du_combine(du_ssm, ds0, dskip, name):
    bsz, seq, _ = ds0.shape
    tm = _pick_tile(seq, (512, 256, 128))

    def body(a_ref, b_ref, d_ref, o_ref):
        o_ref[0] = (a_ref[0] + b_ref[0] * d_ref[...]).astype(BF16)

    return pl.pallas_call(
        body, name=name, grid=(bsz, seq // tm),
        in_specs=[_row(tm, SW), _row(tm, SW), _full2((1, SW))],
        out_specs=_row(tm, SW), out_shape=SDS((bsz, seq, SW), BF16),
        compiler_params=_cp("parallel", "parallel"),
    )(du_ssm, ds0, dskip)


def _merge_out(ya, yb, p3, wout, x1, gt, name):
    bsz, seq, _ = ya.shape
    tm = _pick_tile(seq, (512, 256, 128))

    def body(ya_ref, yb_ref, ga_ref, gbb_ref, w_ref, x_ref, gt_ref, mg_ref, mix_ref, xo_ref):
        merged = (_sigmoid(ga_ref[0]) * ya_ref[0] + _sigmoid(gbb_ref[0]) * yb_ref[0]).astype(BF16)
        mix = _dot(merged, w_ref[...])
        mg_ref[0] = merged
        mix_ref[0] = mix
        xo_ref[0] = x_ref[0] + gt_ref[0] * mix

    return pl.pallas_call(
        body, name=name, grid=(bsz, seq // tm),
        in_specs=[_row(tm, D), _row(tm, D), _row(tm, D, 0), _row(tm, D, 1), _full2((D, D)), _row(tm, D), _seqvec(D)],
        out_specs=[_row(tm, D), _row(tm, D), _row(tm, D)],
        out_shape=[SDS((bsz, seq, D), BF16), SDS((bsz, seq, D), F32), SDS((bsz, seq, D), F32)],
        compiler_params=_cp("parallel", "parallel"),
    )(ya, yb, p3, p3, wout, x1, gt)


def _merge_bwd(dx2, gt, mix, ya, yb, p3, wout, name):
    bsz, seq, _ = ya.shape
    tm = _pick_tile(seq, (512, 256, 128))

    def body(dx_ref, gt_ref, mix_ref, ya_ref, yb_ref, ga_ref, gbb_ref, w_ref, dmix_ref, dya_ref, dyb_ref, dp_ref, dgt_ref):
        i = pl.program_id(1)
        dx = dx_ref[0]
        dmix = (gt_ref[0] * dx).astype(BF16)
        dmix_ref[0] = dmix
        part = jnp.sum(dx * mix_ref[0], axis=0, keepdims=True)

        @pl.when(i == 0)
        def _():
            dgt_ref[0] = part

        @pl.when(i > 0)
        def _():
            dgt_ref[0] += part

        dmg = _dot_nt(dmix, w_ref[...])
        sa = _sigmoid(ga_ref[0])
        sb = _sigmoid(gbb_ref[0])
        dya_ref[0] = (dmg * sa).astype(BF16)
        dyb_ref[0] = (dmg * sb).astype(BF16)
        dp_ref[0, :, 0:D] = (dmg * ya_ref[0] * sa * (1.0 - sa)).astype(BF16)
        dp_ref[0, :, D:2 * D] = (dmg * yb_ref[0] * sb * (1.0 - sb)).astype(BF16)

    bshape = SDS((bsz, seq, D), BF16)
    return pl.pallas_call(
        body, name=name, grid=(bsz, seq // tm),
        in_specs=[_row(tm, D), _seqvec(D), _row(tm, D), _row(tm, D), _row(tm, D), _row(tm, D, 0), _row(tm, D, 1),
                  _full2((D, D))],
        out_specs=[_row(tm, D), _row(tm, D), _row(tm, D), _row(tm, 2 * D), _seqvec(D)],
        out_shape=[bshape, bshape, bshape, SDS((bsz, seq, 2 * D), BF16), SDS((bsz, 1, D), F32)],
        compiler_params=_cp("arbitrary", "arbitrary"),
    )(dx2, gt, mix, ya, yb, p3, p3, wout)


def _final_loss(x3, gfin, target, name):
    bsz, seq, dm = x3.shape
    tm = _pick_tile(seq, (512, 256, 128))

    def body(x_ref, g_ref, t_ref, dx_ref, loss_ref, dg_ref):
        b = pl.program_id(0)
        i = pl.program_id(1)
        xf = x_ref[0]
        gv = g_ref[...]
        r = lax.rsqrt(jnp.mean(xf * xf, axis=-1, keepdims=True) + EPS)
        xhat = xf * r
        e = xhat * gv - t_ref[0]
        dy = e * (1.0 / dm)
        dxh = dy * gv
        dx_ref[0] = r * (dxh - xhat * jnp.mean(dxh * xhat, axis=-1, keepdims=True))
        p_l = jnp.sum(e * e, axis=0, keepdims=True) * (0.5 / dm)
        p_g = jnp.sum(dy * xhat, axis=0, keepdims=True)

        @pl.when((i == 0) & (b == 0))
        def _():
            loss_ref[...] = p_l
            dg_ref[...] = p_g

        @pl.when((i > 0) | (b > 0))
        def _():
            loss_ref[...] += p_l
            dg_ref[...] += p_g

    return pl.pallas_call(
        body, name=name, grid=(bsz, seq // tm),
        in_specs=[_row(tm, dm), _full2((1, dm)), _row(tm, dm)],
        out_specs=[_row(tm, dm), _full2((1, dm)), _full2((1, dm))],
        out_shape=[SDS((bsz, seq, dm), F32), SDS((1, dm), F32), SDS((1, dm), F32)],
        compiler_params=_cp("arbitrary", "arbitrary"),
    )(x3, gfin, target)


def _ada_fwd(c_all, w_shard, b_shard):
    nb = c_all.shape[0]
    n = w_shard.shape[1]

    def body(c_ref, w_ref, b_ref, o_ref):
        cv = c_ref[...]
        cond = (cv * _sigmoid(cv)).astype(BF16)
        o_ref[...] = _dot(cond, w_ref[...].astype(BF16)) + b_ref[...]

    return pl.pallas_call(body, name="ada_fwd", out_shape=SDS((nb, n), F32), compiler_params=_cp())(
        c_all, w_shard, b_shard)


def _ada_bwd(c_all, dmod_shard, dmod_all):
    n = dmod_shard.shape[1]

    def body(c_ref, ds_ref, da_ref, gw_ref, gb_ref):
        cv = c_ref[...]
        cond = (cv * _sigmoid(cv)).astype(BF16)
        gw_ref[...] = _dot_tn(cond, ds_ref[...].astype(BF16))
        gb_ref[...] = jnp.sum(da_ref[...], axis=0, keepdims=True)

    return pl.pallas_call(
        body, name="ada_bwd", out_shape=[SDS((D, n), F32), SDS((1, dmod_all.shape[1]), F32)], compiler_params=_cp(),
    )(c_all, dmod_shard, dmod_all)


def _adamw_math(w, g, m, v):
    m = B1 * m + (1.0 - B1) * g
    v = B2 * v + (1.0 - B2) * (g * g)
    delta = -LR * ((m / BC1) / (jnp.sqrt(v / BC2) + AEPS) + WD * w)
    return delta, m, v


def _adamw_big(w, m, v, p_own, p_sib, row0, name):
    rows = w.shape[0]
    tr = 64
    blk0 = row0 // tr

    def body(w_ref, m_ref, v_ref, a_ref, b_ref, g_ref, d_ref, mo_ref, vo_ref):
        g = a_ref[...] + b_ref[...]
        delta, mn, vn = _adamw_math(w_ref[...], g, m_ref[...], v_ref[...])
        g_ref[...] = g
        d_ref[...] = delta
        mo_ref[...] = mn
        vo_ref[...] = vn

    own = pl.BlockSpec((tr, 1024), lambda i: (i, 0))
    packed = pl.BlockSpec((tr, 1024), lambda i: (blk0 + i, 0))
    shp = SDS(w.shape, F32)
    return pl.pallas_call(
        body, name=name, grid=(rows // tr,),
        in_specs=[own, own, own, packed, packed], out_specs=[own, own, own, own], out_shape=[shp, shp, shp, shp],
        compiler_params=_cp("parallel"),
    )(w, m, v, p_own, p_sib)


def _adamw_plain(w, m, v, g, name):
    def body(w_ref, m_ref, v_ref, g_ref, d_ref, mo_ref, vo_ref):
        delta, mn, vn = _adamw_math(w_ref[...], g_ref[...], m_ref[...], v_ref[...])
        d_ref[...] = delta
        mo_ref[...] = mn
        vo_ref[...] = vn

    shp = SDS(w.shape, F32)
    return pl.pallas_call(body, name=name, out_shape=[shp, shp, shp], compiler_params=_cp())(w, m, v, g)


def _adamw_rows(w, m, v, g, name):
    rows, cols = w.shape
    tr = _pick_tile(rows, (128, 64, 32, 16, 8))

    def body(w_ref, m_ref, v_ref, g_ref, d_ref, mo_ref, vo_ref):
        delta, mn, vn = _adamw_math(w_ref[...], g_ref[...], m_ref[...], v_ref[...])
        d_ref[...] = delta
        mo_ref[...] = mn
        vo_ref[...] = vn

    spec = pl.BlockSpec((tr, cols), lambda i: (i, 0))
    shp = SDS(w.shape, F32)
    return pl.pallas_call(
        body, name=name, grid=(rows // tr,), in_specs=[spec] * 4, out_specs=[spec] * 3, out_shape=[shp] * 3,
        compiler_params=_cp("parallel"),
    )(w, m, v, g)


def _sum_slabs(r, name):
    n, rows, cols = r.shape
    tr = _pick_tile(rows, (256, 128, 64))

    def body(r_ref, o_ref):
        acc = r_ref[0].astype(F32)
        for j in range(1, n):
            acc = acc + r_ref[j].astype(F32)
        o_ref[...] = acc

    return pl.pallas_call(
        body, name=name, grid=(rows // tr,),
        in_specs=[pl.BlockSpec((n, tr, cols), lambda i: (0, i, 0))],
        out_specs=pl.BlockSpec((tr, cols), lambda i: (i, 0)), out_shape=SDS((rows, cols), F32),
        compiler_params=_cp("parallel"),
    )(r)


def _place():
    return lax.axis_index("x"), lax.axis_index("y"), lax.axis_index("c")


def _all_gather8(blk, name):
    m_per, n = blk.shape

    def body(x_ref, out_ref, send_sems, recv_sems, local_sem):
        x, y, c = _place()
        me, sibling = (x, y, c), (x, y, 1 - c)
        chips = [(1 - x, y), (x, 1 - y), (1 - x, 1 - y)]

        def rows(px, py, pc):
            return out_ref.at[pl.ds((4 * px + 2 * py + pc) * m_per, m_per), :]

        def copy(k, block, to, src=None):
            return pltpu.make_async_remote_copy(
                src_ref=rows(*block) if src is None else src, dst_ref=rows(*block),
                send_sem=send_sems.at[k], recv_sem=recv_sems.at[k], device_id=to, device_id_type=MESH)

        mine = pltpu.make_async_copy(x_ref, rows(*me), local_sem)
        mine.start()
        first = [copy(0, me, sibling, src=x_ref)]
        first += [copy(1 + j, me, (*chip, c), src=x_ref) for j, chip in enumerate(chips)]
        for cp in first:
            cp.start()
        passed = [copy(4 + j, (*chip, c), sibling) for j, chip in enumerate(chips)]
        for j, chip in enumerate(chips):
            copy(1 + j, (*chip, c), me).wait_recv()
            passed[j].start()
        copy(0, sibling, me).wait_recv()
        for j, chip in enumerate(chips):
            copy(4 + j, (*chip, 1 - c), me).wait_recv()
        for cp in first + passed:
            cp.wait_send()
        mine.wait()

    return pl.pallas_call(
        body, name=name, out_shape=SDS((N_DEV * m_per, n), blk.dtype),
        in_specs=[pl.BlockSpec(memory_space=pltpu.VMEM)], out_specs=pl.BlockSpec(memory_space=pltpu.VMEM),
        scratch_shapes=[pltpu.SemaphoreType.DMA((7,)), pltpu.SemaphoreType.DMA((7,)), pltpu.SemaphoreType.DMA],
        compiler_params=pltpu.CompilerParams(vmem_limit_bytes=VMEM_LIMIT),
    )(blk)


def _chip_peers(x, y):
    return [(1 - x, y), (x, 1 - y), (1 - x, 1 - y)]


def _gather_chips(shard, name):
    def body(in_ref, out_ref, send_sems, recv_sems, local_sem):
        x, y, c = _place()
        mine = pltpu.make_async_copy(in_ref, out_ref.at[2 * x + y], local_sem)
        mine.start()
        cps = []
        for j, (px, py) in enumerate(_chip_peers(x, y)):
            cp = pltpu.make_async_remote_copy(
                src_ref=in_ref, dst_ref=out_ref.at[2 * x + y], send_sem=send_sems.at[j], recv_sem=recv_sems.at[j],
                device_id=(px, py, c), device_id_type=MESH)
            cp.start()
            cps.append(cp)
        for j, (px, py) in enumerate(_chip_peers(x, y)):
            pltpu.make_async_remote_copy(
                src_ref=in_ref, dst_ref=out_ref.at[2 * px + py], send_sem=send_sems.at[j], recv_sem=recv_sems.at[j],
                device_id=(px, py, c), device_id_type=MESH).wait_recv()
        for cp in cps:
            cp.wait_send()
        mine.wait()

    anyspec = pl.BlockSpec(memory_space=pl.ANY)
    return pl.pallas_call(
        body, name=name, out_shape=SDS((N_CHIPS,) + shard.shape, shard.dtype),
        in_specs=[anyspec], out_specs=anyspec,
        scratch_shapes=[pltpu.SemaphoreType.DMA((3,)), pltpu.SemaphoreType.DMA((3,)), pltpu.SemaphoreType.DMA],
        compiler_params=pltpu.CompilerParams(vmem_limit_bytes=VMEM_LIMIT),
    )(shard)


def _scatter_chips(stacked, name):
    def body(in_ref, out_ref, send_sems, recv_sems, local_sem):
        x, y, c = _place()
        me = 2 * x + y
        mine = pltpu.make_async_copy(in_ref.at[me], out_ref.at[me], local_sem)
        mine.start()
        cps = []
        for j, (px, py) in enumerate(_chip_peers(x, y)):
            cp = pltpu.make_async_remote_copy(
                src_ref=in_ref.at[2 * px + py], dst_ref=out_ref.at[me], send_sem=send_sems.at[j],
                recv_sem=recv_sems.at[j], device_id=(px, py, c), device_id_type=MESH)
            cp.start()
            cps.append(cp)
        for j, (px, py) in enumerate(_chip_peers(x, y)):
            pltpu.make_async_remote_copy(
                src_ref=in_ref.at[me], dst_ref=out_ref.at[2 * px + py], send_sem=send_sems.at[j],
                recv_sem=recv_sems.at[j], device_id=(px, py, c), device_id_type=MESH).wait_recv()
        for cp in cps:
            cp.wait_send()
        mine.wait()

    anyspec = pl.BlockSpec(memory_space=pl.ANY)
    return pl.pallas_call(
        body, name=name, out_shape=SDS(stacked.shape, stacked.dtype),
        in_specs=[anyspec], out_specs=anyspec,
        scratch_shapes=[pltpu.SemaphoreType.DMA((3,)), pltpu.SemaphoreType.DMA((3,)), pltpu.SemaphoreType.DMA],
        compiler_params=pltpu.CompilerParams(vmem_limit_bytes=VMEM_LIMIT),
    )(stacked)


def _swap_sibling(v, name):
    def body(in_ref, out_ref, send_sem, recv_sem):
        x, y, c = _place()
        cp = pltpu.make_async_remote_copy(
            src_ref=in_ref, dst_ref=out_ref, send_sem=send_sem, recv_sem=recv_sem,
            device_id=(x, y, 1 - c), device_id_type=MESH)
        cp.start()
        cp.wait()

    anyspec = pl.BlockSpec(memory_space=pl.ANY)
    return pl.pallas_call(
        body, name=name, out_shape=SDS(v.shape, v.dtype), in_specs=[anyspec], out_specs=anyspec,
        scratch_shapes=[pltpu.SemaphoreType.DMA, pltpu.SemaphoreType.DMA],
        compiler_params=pltpu.CompilerParams(vmem_limit_bytes=VMEM_LIMIT),
    )(v)


def _select(stacked, idx):
    out = stacked[0]
    for j in range(1, stacked.shape[0]):
        out = jnp.where(idx == j, stacked[j], out)
    return out


BIG = (
    ("w1_a", D, DFF // 4, True), ("w3_a", D, DFF // 4, True), ("w2_a", DFF // 4, D, False),
    ("w_in", D, 5632 // 4, True), ("w_conv_out", CW // 4, D, False), ("w_glu", SW // 4, SW, False),
    ("w_ssm_out", SW, D // 4, True), ("w_out", D // 4, D, False),
    ("w1_b", D, DFF // 4, True), ("w3_b", D, DFF // 4, True), ("w2_b", DFF // 4, D, False),
)
PACK_COLS = 1024


def _pack_rows():
    offs, r = {}, 0
    for name, rows, cols, _ in BIG:
        offs[name] = r
        r += rows * cols // PACK_COLS
    return offs, r


def _full_from_stacked(st, rows, cols, split_cols):
    st = st.reshape(N_CHIPS, rows, cols)
    if split_cols:
        return st.transpose(1, 0, 2).reshape(rows, N_CHIPS * cols)
    return st.reshape(N_CHIPS * rows, cols)


def _stacked_from_full(full, rows, cols, split_cols):
    if split_cols:
        st = full.reshape(rows, N_CHIPS, cols).transpose(1, 0, 2)
    else:
        st = full.reshape(N_CHIPS, rows, cols)
    return st.reshape(N_CHIPS, rows * cols // PACK_COLS, PACK_COLS)


def _blockdiag(t):
    r = lax.broadcasted_iota(jnp.int32, (SW, GP), 0) // NH
    cidx = lax.broadcasted_iota(jnp.int32, (SW, GP), 1) // NP
    return jnp.where(r == cidx, jnp.tile(t, (NG, 1)), 0.0)


def _blockdiag_extract(acc):
    gs = NG // SSM_SUPER
    a = acc.reshape(NG, NH, gs, NP)
    sel = (lax.broadcasted_iota(jnp.int32, (NG, 1, gs, 1), 0) % gs) == lax.broadcasted_iota(jnp.int32, (NG, 1, gs, 1), 2)
    a = jnp.sum(jnp.where(sel, a, 0.0), axis=2)
    return a.transpose(1, 0, 2).reshape(NH, GP)


def _to_t(p):
    return p.transpose(2, 0, 1).reshape(NH, GP)


def _from_t(t):
    return t.reshape(NH, NG, NP).transpose(1, 2, 0)


def _c_to_t(p):
    return p.transpose(1, 0, 2).reshape(NH, GP)


def _c_from_t(t):
    return t.reshape(NH, NG, NP).transpose(1, 0, 2)


def _ffn_forward(x, g, sh, sc, gt, w1, w3, w2, tag):
    h = _norm_mod(x, g, sh, sc, f"{tag}_norm")
    a, b, hid = _swiglu_up(h, w1, w3, f"{tag}_up")
    f, xo = _ffn_down(hid, w2, x, gt, f"{tag}_down")
    return xo, (x, h, a, b, hid, f)


def _ffn_backward(dxo, saved, g, sc, gt, w1, w3, w2, tag):
    x, h, a, b, hid, f = saved
    dfs, da, db, dgt = _ffn_bwd_hid(dxo, gt, f, a, b, w2, f"{tag}_bwd_hid")
    w13 = jnp.concatenate([w1, w3], axis=1)
    tk = _pick_tile(a.shape[2], (1408, 512, 256, 128))
    dx, dsh, dsc, dg = _dh_norm_bwd([da, db], w13, tk, x, g, sc, dxo, f"{tag}_bwd_dh")
    h2, da2, db2 = _flat(h), _flat(da), _flat(db)
    gw1 = _mm(h2, da2, ta=True, name=f"{tag}_gw1")
    gw3 = _mm(h2, db2, ta=True, name=f"{tag}_gw3")
    gw2 = _mm(_flat(hid), _flat(dfs), ta=True, name=f"{tag}_gw2")
    return dx, (dsh, dsc, dgt, dg), (gw1, gw3, gw2)


def kernel(x, c, w_ada, b_ada, g_ffn1, w1_a, w3_a, w2_a, g_mix, w_in, conv_w, w_conv_out, a_re, a_im, b_re, b_im, c_re, c_im, log_dt, d_skip, w_glu, w_ssm_out, w_out, g_ffn2, w1_b, w3_b, w2_b, g_final, loss_target, m_w_ada, m_b_ada, m_g_ffn1, m_w1_a, m_w3_a, m_w2_a, m_g_mix, m_w_in, m_conv_w, m_w_conv_out, m_a_re, m_a_im, m_b_re, m_b_im, m_c_re, m_c_im, m_log_dt, m_d_skip, m_w_glu, m_w_ssm_out, m_w_out, m_g_ffn2, m_w1_b, m_w3_b, m_w2_b, m_g_final, v_w_ada, v_b_ada, v_g_ffn1, v_w1_a, v_w3_a, v_w2_a, v_g_mix, v_w_in, v_conv_w, v_w_conv_out, v_a_re, v_a_im, v_b_re, v_b_im, v_c_re, v_c_im, v_log_dt, v_d_skip, v_w_glu, v_w_ssm_out, v_w_out, v_g_ffn2, v_w1_b, v_w3_b, v_w2_b, v_g_final):
    args = dict(locals())
    names = ["w_ada", "b_ada", "g_ffn1", "w1_a", "w3_a", "w2_a", "g_mix", "w_in", "conv_w", "w_conv_out", "a_re",
             "a_im", "b_re", "b_im", "c_re", "c_im", "log_dt", "d_skip", "w_glu", "w_ssm_out", "w_out", "g_ffn2",
             "w1_b", "w3_b", "w2_b", "g_final"]
    bsz, seq, _ = x.shape
    mx, my, mc = _place()
    chip = 2 * mx + my
    dev = 4 * mx + 2 * my + mc

    offs, pack_rows = _pack_rows()
    packed = jnp.concatenate([args[n][0].astype(BF16).reshape(-1, PACK_COLS) for n, _, _, _ in BIG], axis=0)
    gathered = _gather_chips(packed, "gather_weights")
    wfull = {}
    for n, rows, cols, split in BIG:
        nr = rows * cols // PACK_COLS
        wfull[n] = _full_from_stacked(gathered[:, offs[n]:offs[n] + nr], rows, cols, split)

    nmod_shard = NMOD * D // N_CHIPS
    c_all = _all_gather8(c.reshape(SUB, -1), "gather_c").reshape(N_DEV * bsz, D)
    b_shard = _select(b_ada.reshape(N_CHIPS, 1, nmod_shard), chip)
    mod_shard = _ada_fwd(c_all, w_ada[0], b_shard)
    nb = N_DEV * bsz
    cw_pad = jnp.pad(conv_w[0], ((0, SUB - 3), (0, nmod_shard - CW // N_CHIPS)))
    mod_st = _gather_chips(jnp.concatenate([mod_shard, cw_pad], axis=0), "gather_mod")
    mod_all = mod_st[:, :nb].transpose(1, 0, 2).reshape(N_DEV, bsz, NMOD * D)
    mod = _select(mod_all, dev)
    sh1, sc1, gt1, sh2, sc2, gt2, sh3, sc3, gt3 = [mod[:, None, j * D:(j + 1) * D] for j in range(NMOD)]
    convw = mod_st[:, nb:nb + 3, :CW // N_CHIPS].transpose(1, 0, 2).reshape(3, CW)
    convw8 = jnp.pad(convw, ((0, SUB - 3), (0, 0)))

    are, aim = a_re.reshape(1, GP), a_im.reshape(1, GP)
    ldt = jnp.broadcast_to(log_dt.reshape(NG, 1), (NG, NP)).reshape(1, GP)
    bre_t, bim_t = _to_t(b_re[0]), _to_t(b_im[0])
    abr, abi, bbr_t, bbi_t = _ssm_disc(are, aim, ldt, bre_t, bim_t)
    wb = jnp.concatenate([_blockdiag(bbr_t), _blockdiag(bbi_t)], axis=1).astype(BF16)
    wct = jnp.concatenate([_blockdiag(_c_to_t(c_re[0])), -_blockdiag(_c_to_t(c_im[0]))], axis=1).astype(BF16)
    ar8 = jnp.broadcast_to(abr, (SUB, GP))
    ai8 = jnp.broadcast_to(abi, (SUB, GP))

    x1, ffn1_saved = _ffn_forward(x, g_ffn1, sh1, sc1, gt1, wfull["w1_a"], wfull["w3_a"], wfull["w2_a"], "ffn1")

    h2 = _norm_mod(x1, g_mix, sh2, sc2, "mix_norm")
    h2f = _flat(h2)
    win = wfull["w_in"]
    win1, winu, win3 = win[:, :3 * CW], win[:, 3 * CW:3 * CW + SW], win[:, 3 * CW + SW:]
    p1 = _mm(h2f, win1, name="mix_in1").reshape(bsz, seq, 3 * CW)
    u = _mm(h2f, winu, name="mix_inu").reshape(bsz, seq, SW)
    p3 = _mm(h2f, win3, name="mix_in3").reshape(bsz, seq, 2 * D)

    ya_in = _conv_fwd(p1, convw8, "conv_fwd")
    ya = _mm(_flat(ya_in), wfull["w_conv_out"], name="conv_out").reshape(bsz, seq, D)

    u_tm = u.transpose(1, 0, 2).reshape(seq * bsz, SW)
    xs, y_tm = _ssm_fwd(u_tm, wb, wct, ar8, ai8, bsz, "ssm_fwd")
    ys = y_tm.reshape(seq, bsz, SW).transpose(1, 0, 2)
    s0, z, s1, s2, yb = _ssm_post(ys, u, d_skip, wfull["w_glu"], wfull["w_ssm_out"], "ssm_post")

    merged, mix, x2 = _merge_out(ya, yb, p3, wfull["w_out"], x1, gt2, "merge_out")
    x3, ffn2_saved = _ffn_forward(x2, g_ffn2, sh3, sc3, gt3, wfull["w1_b"], wfull["w3_b"], wfull["w2_b"], "ffn2")

    dx3, lossvec, dgfin = _final_loss(x3, g_final.reshape(1, D), loss_target, "final_loss")
    loss = lax.psum(jnp.sum(lossvec), ("x", "y", "c"))

    gfull = {}
    dx2, (dsh3, dsc3, dgt3, dg3), (gfull["w1_b"], gfull["w3_b"], gfull["w2_b"]) = _ffn_backward(
        dx3, ffn2_saved, g_ffn2, sc3, gt3, wfull["w1_b"], wfull["w3_b"], wfull["w2_b"], "ffn2")

    dmix, dya, dyb, dp3, dgt2 = _merge_bwd(dx2, gt2, mix, ya, yb, p3, wfull["w_out"], "merge_bwd")
    gfull["w_out"] = _mm(_flat(merged), _flat(dmix), ta=True, name="gw_out")
    dp1, dconvw8 = _conv_bwd(dya, wfull["w_conv_out"], p1, convw8, "conv_bwd")
    gfull["w_conv_out"] = _mm(_flat(ya_in), _flat(dya), ta=True, name="gw_conv_out")
    ds0, dz, ddskip = _ssm_post_bwd(dyb, s0, z, u, d_skip, wfull["w_glu"], wfull["w_ssm_out"], "ssm_post_bwd")
    gfull["w_ssm_out"] = _mm(_flat(s2), _flat(dyb), ta=True, name="gw_ssm_out")
    gfull["w_glu"] = _mm(_flat(s1), _flat(dz), ta=True, name="gw_glu")
    dy_tm = ds0.transpose(1, 0, 2).reshape(seq * bsz, SW)
    du_tm, dwb, dwct, dar8, dai8 = _ssm_bwd(dy_tm, u_tm, xs, wb, wct, ar8, ai8, bsz, "ssm_bwd")
    du = _du_combine(du_tm.reshape(seq, bsz, SW).transpose(1, 0, 2), ds0, d_skip, "du_combine")
    dx1, dsh2, dsc2, dgmix = _dh_norm_bwd([dp1, du, dp3], win, SW, x1, g_mix, sc2, dx2, "mix_bwd_dh")
    gfull["w_in"] = jnp.concatenate([
        _mm(h2f, _flat(dp1), ta=True, name="gw_in1"), _mm(h2f, _flat(du), ta=True, name="gw_inu"),
        _mm(h2f, _flat(dp3), ta=True, name="gw_in3")], axis=1)

    grad_x, (dsh1, dsc1, dgt1, dg1), (gfull["w1_a"], gfull["w3_a"], gfull["w2_a"]) = _ffn_backward(
        dx1, ffn1_saved, g_ffn1, sc1, gt1, wfull["w1_a"], wfull["w3_a"], wfull["w2_a"], "ffn1")

    sbw = GP // SSM_SUPER
    d_are, d_aim, d_ldt, d_bre_t, d_bim_t = _ssm_disc_bwd(
        are, aim, ldt, bre_t, bim_t, jnp.sum(dar8, axis=0, keepdims=True), jnp.sum(dai8, axis=0, keepdims=True),
        _blockdiag_extract(dwb[:, :sbw]), _blockdiag_extract(dwb[:, sbw:]))
    d_cre = _c_from_t(_blockdiag_extract(dwct[:, :sbw]))
    d_cim = -_c_from_t(_blockdiag_extract(dwct[:, sbw:]))

    small_parts = [dg1, dgmix, dg3, dgfin, dconvw8[:3], d_are, d_aim, _from_t(d_bre_t), _from_t(d_bim_t), d_cre, d_cim,
                   jnp.sum(d_ldt.reshape(NG, NP), axis=1), ddskip]
    small_sizes = [int(p.size) for p in small_parts]
    n_small = sum(small_sizes)
    n_small_pad = -(-n_small // (SUB * PACK_COLS)) * (SUB * PACK_COLS)
    dmod = jnp.concatenate([dsh1, dsc1, dgt1, dsh2, dsc2, dgt2, dsh3, dsc3, dgt3], axis=2).reshape(bsz * NMOD * D)
    flat = jnp.concatenate([p.reshape(-1) for p in small_parts] + [jnp.zeros((n_small_pad - n_small,), F32), dmod])
    allg = _all_gather8(flat.reshape(SUB, -1), "gather_small").reshape(N_DEV, -1)
    small = _sum_slabs(allg[:, :n_small_pad].reshape(N_DEV, -1, PACK_COLS), "sum_small").reshape(-1)
    sg, o = [], 0
    for p, sz in zip(small_parts, small_sizes):
        sg.append(small[o:o + sz].reshape(p.shape))
        o += sz
    (g_g1, g_gmix, g_g3, g_gfin, g_convw, g_are, g_aim, g_bre, g_bim, g_cre, g_cim, g_ldt, g_dskip) = sg

    dmod_all = allg[:, n_small_pad:].reshape(nb, NMOD * D)
    dmod_shard = _select(dmod_all.reshape(nb, N_CHIPS, nmod_shard).transpose(1, 0, 2), chip)
    g_wada, g_bada = _ada_bwd(c_all, dmod_shard, dmod_all)

    stacked = jnp.concatenate(
        [_stacked_from_full(gfull[n], rows, cols, split) for n, rows, cols, split in BIG], axis=1).astype(BF16)
    recv = _scatter_chips(stacked, "scatter_grads")
    p_own = _sum_slabs(recv, "sum_chips")
    p_sib = _swap_sibling(p_own, "swap_sibling")

    grads, deltas, new_m, new_v = {}, {}, {}, {}
    for n, rows, cols, _ in BIG:
        shp = args[n].shape
        w2d, m2d, v2d = [args[k + n][0].reshape(-1, PACK_COLS) for k in ("", "m_", "v_")]
        res = _adamw_big(w2d, m2d, v2d, p_own, p_sib, offs[n], f"adamw_{n}")
        grads[n], deltas[n], new_m[n], new_v[n] = [r.reshape(shp) for r in res]
    res = _adamw_rows(w_ada[0], m_w_ada[0], v_w_ada[0], g_wada, "adamw_w_ada")
    grads["w_ada"] = g_wada[None]
    deltas["w_ada"], new_m["w_ada"], new_v["w_ada"] = [r[None] for r in res]

    g_convw_shard = _select(g_convw.reshape(3, N_CHIPS, CW // N_CHIPS).transpose(1, 0, 2), chip)
    small_g = {"b_ada": g_bada, "g_ffn1": g_g1, "g_mix": g_gmix, "g_ffn2": g_g3, "g_final": g_gfin,
               "conv_w": g_convw_shard, "a_re": g_are, "a_im": g_aim, "b_re": g_bre, "b_im": g_bim,
               "c_re": g_cre, "c_im": g_cim, "log_dt": g_ldt, "d_skip": g_dskip}
    small_names = list(small_g)
    sizes = [int(args[n].size) for n in small_names]
    tot = sum(sizes)
    tot_pad = -(-tot // (SUB * PACK_COLS)) * (SUB * PACK_COLS)

    def pack(get):
        return jnp.concatenate([get(n).reshape(-1) for n in small_names] + [jnp.zeros((tot_pad - tot,), F32)]).reshape(
            -1, PACK_COLS)

    res = _adamw_plain(pack(lambda n: args[n]), pack(lambda n: args["m_" + n]), pack(lambda n: args["v_" + n]),
                       pack(lambda n: small_g[n]), "adamw_small")
    o = 0
    for n, sz in zip(small_names, sizes):
        shp = args[n].shape
        grads[n] = small_g[n].reshape(shp)
        deltas[n], new_m[n], new_v[n] = [r.reshape(-1)[o:o + sz].reshape(shp) for r in res]
        o += sz

    return (loss, grad_x, *[grads[n] for n in names], *[deltas[n] for n in names],
            *[new_m[n] for n in names], *[new_v[n] for n in names])
```

```python
import functools
import math

import jax
import jax.numpy as jnp
from jax import lax
from jax.experimental import pallas as pl
from jax.experimental.pallas import tpu as pltpu

F32 = jnp.float32
BF16 = jnp.bfloat16
SDS = jax.ShapeDtypeStruct
MESH = pl.DeviceIdType.MESH

D = 1024
DFF = 2816
CW = 1024
SW = 512
NG, NP, NH = 32, 64, 16
GP = NG * NP
NMOD = 9
EPS = 1e-6
N_CHIPS = 4
N_DEV = 8
SUB = 8
LANE = 128
SSM_SUPER = 4
VMEM_LIMIT = 50 * 1024 * 1024

LR, B1, B2, AEPS, WD, STEP = 0.001, 0.9, 0.999, 1e-08, 0.01, 10
BC1 = 1.0 - B1 ** STEP
BC2 = 1.0 - B2 ** STEP


def _cp(*sem):
    return pltpu.CompilerParams(dimension_semantics=sem or None, vmem_limit_bytes=VMEM_LIMIT)


def _pick_tile(n, cands):
    for t in cands:
        if t <= n and n % t == 0:
            return t
    return n


def _dot(a, b):
    return lax.dot_general(a, b, (((1,), (0,)), ((), ())), preferred_element_type=F32)


def _dot_nt(a, b):
    return lax.dot_general(a, b, (((1,), (1,)), ((), ())), preferred_element_type=F32)


def _dot_tn(a, b):
    return lax.dot_general(a, b, (((0,), (0,)), ((), ())), preferred_element_type=F32)


def _row(tm, width, col=0):
    return pl.BlockSpec((1, tm, width), lambda b, i, *_: (b, i, col))


def _seqvec(width):
    return pl.BlockSpec((1, 1, width), lambda b, *_: (b, 0, 0))


def _full2(shape):
    return pl.BlockSpec(shape, lambda *_: (0, 0))


def _sigmoid(x):
    return jax.nn.sigmoid(x)


def _mm(a, b, *, ta=False, tb=False, out_dtype=F32, name):
    if ta:
        kdim, m = a.shape
    else:
        m, kdim = a.shape
    n = b.shape[0] if tb else b.shape[1]
    tm = _pick_tile(m, (1408, 1024, 512, 256, 128))
    tn = _pick_tile(n, (1408, 1024, 512, 256, 128))
    tk = _pick_tile(kdim, (512, 256, 128))
    nk = kdim // tk

    def body(a_ref, b_ref, o_ref, acc_ref):
        k = pl.program_id(2)

        @pl.when(k == 0)
        def _():
            acc_ref[...] = jnp.zeros_like(acc_ref)

        av = a_ref[...].astype(BF16)
        bv = b_ref[...].astype(BF16)
        dn = (((0 if ta else 1,), (1 if tb else 0,)), ((), ()))
        acc_ref[...] += lax.dot_general(av, bv, dn, preferred_element_type=F32)

        @pl.when(k == nk - 1)
        def _():
            o_ref[...] = acc_ref[...].astype(out_dtype)

    a_spec = pl.BlockSpec((tk, tm), lambda i, j, k: (k, i)) if ta else pl.BlockSpec((tm, tk), lambda i, j, k: (i, k))
    b_spec = pl.BlockSpec((tn, tk), lambda i, j, k: (j, k)) if tb else pl.BlockSpec((tk, tn), lambda i, j, k: (k, j))
    return pl.pallas_call(
        body, name=name, grid=(m // tm, n // tn, nk),
        in_specs=[a_spec, b_spec],
        out_specs=pl.BlockSpec((tm, tn), lambda i, j, k: (i, j)),
        out_shape=SDS((m, n), out_dtype),
        scratch_shapes=[pltpu.VMEM((tm, tn), F32)],
        compiler_params=_cp("parallel", "parallel", "arbitrary"),
    )(a, b)


def _flat(a):
    return a.reshape(-1, a.shape[-1])


def _norm_mod(x, g, sh, sc, name):
    bsz, seq, dm = x.shape
    tm = _pick_tile(seq, (512, 256, 128))

    def body(x_ref, g_ref, sh_ref, sc_ref, o_ref):
        xf = x_ref[0]
        r = lax.rsqrt(jnp.mean(xf * xf, axis=-1, keepdims=True) + EPS)
        hn = xf * r * g_ref[...]
        o_ref[0] = (hn * (1.0 + sc_ref[0]) + sh_ref[0]).astype(BF16)

    return pl.pallas_call(
        body, name=name, grid=(bsz, seq // tm),
        in_specs=[_row(tm, dm), _full2((1, dm)), _seqvec(dm), _seqvec(dm)],
        out_specs=_row(tm, dm), out_shape=SDS((bsz, seq, dm), BF16),
        compiler_params=_cp("parallel", "parallel"),
    )(x, g, sh, sc)


def _swiglu_up(h, w1, w3, name):
    bsz, seq, dm = h.shape
    nf = w1.shape[1]
    tm = _pick_tile(seq, (512, 256, 128))
    tn = _pick_tile(nf, (1408, 512, 256, 128))

    def body(h_ref, w1_ref, w3_ref, a_ref, b_ref, hid_ref):
        hv = h_ref[0]
        a = _dot(hv, w1_ref[...])
        b = _dot(hv, w3_ref[...])
        a_ref[0] = a.astype(BF16)
        b_ref[0] = b.astype(BF16)
        hid_ref[0] = (a * _sigmoid(a) * b).astype(BF16)

    wspec = pl.BlockSpec((dm, tn), lambda n, b, i: (0, n))
    ospec = pl.BlockSpec((1, tm, tn), lambda n, b, i: (b, i, n))
    shp = SDS((bsz, seq, nf), BF16)
    return pl.pallas_call(
        body, name=name, grid=(nf // tn, bsz, seq // tm),
        in_specs=[pl.BlockSpec((1, tm, dm), lambda n, b, i: (b, i, 0)), wspec, wspec],
        out_specs=[ospec, ospec, ospec], out_shape=[shp, shp, shp],
        compiler_params=_cp("parallel", "parallel", "parallel"),
    )(h, w1, w3)


def _ffn_down(hid, w2, x, gt, name):
    bsz, seq, nf = hid.shape
    dm = w2.shape[1]
    tm = _pick_tile(seq, (512, 256, 128))
    tk = _pick_tile(nf, (1408, 512, 256, 128))
    nk = nf // tk

    def body(hid_ref, w2_ref, x_ref, gt_ref, f_ref, xo_ref, acc_ref):
        k = pl.program_id(2)

        @pl.when(k == 0)
        def _():
            acc_ref[...] = jnp.zeros_like(acc_ref)

        acc_ref[...] += _dot(hid_ref[0], w2_ref[...])

        @pl.when(k == nk - 1)
        def _():
            f = acc_ref[...]
            f_ref[0] = f
            xo_ref[0] = x_ref[0] + 0.5 * gt_ref[0] * f

    shp = SDS((bsz, seq, dm), F32)
    return pl.pallas_call(
        body, name=name, grid=(bsz, seq // tm, nk),
        in_specs=[pl.BlockSpec((1, tm, tk), lambda b, i, k: (b, i, k)),
                  pl.BlockSpec((tk, dm), lambda b, i, k: (k, 0)), _row(tm, dm), _seqvec(dm)],
        out_specs=[_row(tm, dm), _row(tm, dm)], out_shape=[shp, shp],
        scratch_shapes=[pltpu.VMEM((tm, dm), F32)],
        compiler_params=_cp("parallel", "parallel", "arbitrary"),
    )(hid, w2, x, gt)


def _ffn_bwd_hid(dxo, gt, f, a, b, w2, name):
    bsz, seq, dm = dxo.shape
    nf = a.shape[2]
    tm = _pick_tile(seq, (512, 256, 128))
    tn = _pick_tile(nf, (1408, 512, 256, 128))

    def body(dxo_ref, gt_ref, f_ref, a_ref, b_ref, w2_ref, dfs_ref, da_ref, db_ref, dgt_ref):
        i = pl.program_id(1)
        n = pl.program_id(2)

        @pl.when(n == 0)
        def _():
            dxo = dxo_ref[0]
            dfs_ref[0] = (0.5 * gt_ref[0] * dxo).astype(BF16)
            part = jnp.sum(0.5 * dxo * f_ref[0], axis=0, keepdims=True)

            @pl.when(i == 0)
            def _():
                dgt_ref[0] = part

            @pl.when(i > 0)
            def _():
                dgt_ref[0] += part

        dhid = _dot_nt(dfs_ref[0], w2_ref[...])
        av = a_ref[0].astype(F32)
        bv = b_ref[0].astype(F32)
        sg = _sigmoid(av)
        da_ref[0] = (dhid * bv * (sg * (1.0 + av * (1.0 - sg)))).astype(BF16)
        db_ref[0] = (dhid * (av * sg)).astype(BF16)

    hspec = pl.BlockSpec((1, tm, tn), lambda b, i, n: (b, i, n))
    return pl.pallas_call(
        body, name=name, grid=(bsz, seq // tm, nf // tn),
        in_specs=[_row(tm, dm), _seqvec(dm), _row(tm, dm), hspec, hspec,
                  pl.BlockSpec((tn, dm), lambda b, i, n: (n, 0))],
        out_specs=[_row(tm, dm), hspec, hspec, _seqvec(dm)],
        out_shape=[SDS((bsz, seq, dm), BF16), SDS((bsz, seq, nf), BF16), SDS((bsz, seq, nf), BF16),
                   SDS((bsz, 1, dm), F32)],
        compiler_params=_cp("arbitrary", "arbitrary", "arbitrary"),
    )(dxo, gt, f, a, b, w2)


def _dh_norm_bwd(pieces, w, tk, x, g, sc, dxo, name):
    bsz, seq, dm = x.shape
    tm = _pick_tile(seq, (512, 256, 128))
    nblk = [p.shape[2] // tk for p in pieces]
    offs = [sum(nblk[:j]) for j in range(len(pieces))]
    nk = sum(nblk)
    npc = len(pieces)

    def body(*refs):
        p_refs = refs[:npc]
        w_ref, x_ref, g_ref, sc_ref, dxo_ref, dx_ref, dsh_ref, dsc_ref, dg_ref, acc_ref = refs[npc:]
        b = pl.program_id(0)
        i = pl.program_id(1)
        k = pl.program_id(2)

        @pl.when(k == 0)
        def _():
            acc_ref[...] = jnp.zeros_like(acc_ref)

        for j in range(npc):
            @pl.when((k >= offs[j]) & (k < offs[j] + nblk[j]))
            def _(j=j):
                acc_ref[...] += _dot_nt(p_refs[j][0], w_ref[...])

        @pl.when(k == nk - 1)
        def _():
            dh = acc_ref[...]
            xf = x_ref[0]
            gv = g_ref[...]
            r = lax.rsqrt(jnp.mean(xf * xf, axis=-1, keepdims=True) + EPS)
            xhat = xf * r
            dhn = dh * (1.0 + sc_ref[0])
            p_sh = jnp.sum(dh, axis=0, keepdims=True)
            p_sc = jnp.sum(dh * (xhat * gv), axis=0, keepdims=True)
            p_g = jnp.sum(dhn * xhat, axis=0, keepdims=True)
            dxh = dhn * gv
            dx_ref[0] = dxo_ref[0] + r * (dxh - xhat * jnp.mean(dxh * xhat, axis=-1, keepdims=True))

            @pl.when(i == 0)
            def _():
                dsh_ref[0] = p_sh
                dsc_ref[0] = p_sc

            @pl.when(i > 0)
            def _():
                dsh_ref[0] += p_sh
                dsc_ref[0] += p_sc

            @pl.when((i == 0) & (b == 0))
            def _():
                dg_ref[...] = p_g

            @pl.when((i > 0) | (b > 0))
            def _():
                dg_ref[...] += p_g

    def pspec(j):
        return pl.BlockSpec((1, tm, tk), lambda b, i, k: (b, i, jnp.clip(k - offs[j], 0, nblk[j] - 1)))

    return pl.pallas_call(
        body, name=name, grid=(bsz, seq // tm, nk),
        in_specs=[pspec(j) for j in range(npc)] + [
            pl.BlockSpec((dm, tk), lambda b, i, k: (0, k)), _row(tm, dm), _full2((1, dm)), _seqvec(dm), _row(tm, dm)],
        out_specs=[_row(tm, dm), _seqvec(dm), _seqvec(dm), _full2((1, dm))],
        out_shape=[SDS((bsz, seq, dm), F32), SDS((bsz, 1, dm), F32), SDS((bsz, 1, dm), F32), SDS((1, dm), F32)],
        scratch_shapes=[pltpu.VMEM((tm, dm), F32)],
        compiler_params=_cp("arbitrary", "arbitrary", "arbitrary"),
    )(*pieces, w, x, g, sc, dxo)


HALO = 16


def _conv_core(gc, v, gch, vh, w, first):
    cv = gc * v
    halo = jnp.where(first, 0.0, gch * vh)
    ext = jnp.concatenate([halo, cv], axis=0)
    cv1 = pltpu.roll(ext, 1, 0)[HALO:]
    cv2 = pltpu.roll(ext, 2, 0)[HALO:]
    conv = w[0:1] * cv2 + w[1:2] * cv1 + w[2:3] * cv
    return cv, cv1, cv2, conv


def _prev_halo(tm, col):
    return pl.BlockSpec((1, HALO, CW), lambda b, i, *_: (b, jnp.maximum(i * (tm // HALO) - 1, 0), col))


def _next_halo(tm, seq, col):
    return pl.BlockSpec((1, HALO, CW), lambda b, i, *_: (b, jnp.minimum((i + 1) * (tm // HALO), seq // HALO - 1), col))


def _conv_fwd(p1, convw8, name):
    bsz, seq, _ = p1.shape
    tm = _pick_tile(seq, (512, 256, 128))

    def body(gb_ref, gc_ref, v_ref, gch_ref, vh_ref, w_ref, o_ref):
        first = pl.program_id(1) == 0
        _, _, _, conv = _conv_core(gc_ref[0], v_ref[0], gch_ref[0], vh_ref[0], w_ref[...], first)
        o_ref[0] = (gb_ref[0] * conv).astype(BF16)

    return pl.pallas_call(
        body, name=name, grid=(bsz, seq // tm),
        in_specs=[_row(tm, CW, 0), _row(tm, CW, 1), _row(tm, CW, 2), _prev_halo(tm, 1), _prev_halo(tm, 2),
                  _full2((8, CW))],
        out_specs=_row(tm, CW), out_shape=SDS((bsz, seq, CW), BF16),
        compiler_params=_cp("parallel", "parallel"),
    )(p1, p1, p1, p1, p1, convw8)


def _conv_bwd(dya, wco, p1, convw8, name):
    bsz, seq, _ = p1.shape
    tm = _pick_tile(seq, (512, 256, 128))
    nt = seq // tm
    ext_rows = tm + HALO

    def body(dya_ref, dyan_ref, wco_ref, gb_ref, gbn_ref, gc_ref, v_ref, gch_ref, vh_ref, w_ref, dp_ref, dw_ref):
        b = pl.program_id(0)
        i = pl.program_id(1)
        w = w_ref[...]
        cv, cv1, cv2, conv = _conv_core(gc_ref[0], v_ref[0], gch_ref[0], vh_ref[0], w, i == 0)
        dya_ext = jnp.concatenate([dya_ref[0], dyan_ref[0]], axis=0)
        dyain_ext = _dot_nt(dya_ext, wco_ref[...])
        gb = gb_ref[0]
        gb_ext = jnp.concatenate([gb, gbn_ref[0]], axis=0)
        rows = lax.broadcasted_iota(jnp.int32, (ext_rows, 1), 0)
        dconv_ext = jnp.where((rows < tm) | (i < nt - 1), dyain_ext * gb_ext, 0.0)
        dconv = dconv_ext[:tm]
        dconv1 = pltpu.roll(dconv_ext, ext_rows - 1, 0)[:tm]
        dconv2 = pltpu.roll(dconv_ext, ext_rows - 2, 0)[:tm]
        dcv = w[2:3] * dconv + w[1:2] * dconv1 + w[0:1] * dconv2
        dp_ref[0, :, 0:CW] = (dyain_ext[:tm] * conv).astype(BF16)
        dp_ref[0, :, CW:2 * CW] = (dcv * v_ref[0]).astype(BF16)
        dp_ref[0, :, 2 * CW:3 * CW] = (dcv * gc_ref[0]).astype(BF16)
        g0 = jnp.sum(dconv * cv2, axis=0, keepdims=True)
        g1 = jnp.sum(dconv * cv1, axis=0, keepdims=True)
        g2 = jnp.sum(dconv * cv, axis=0, keepdims=True)
        upd = jnp.concatenate([g0, g1, g2, jnp.zeros((5, CW), F32)], axis=0)

        @pl.when((i == 0) & (b == 0))
        def _():
            dw_ref[...] = upd

        @pl.when((i > 0) | (b > 0))
        def _():
            dw_ref[...] += upd

    return pl.pallas_call(
        body, name=name, grid=(bsz, seq // tm),
        in_specs=[_row(tm, CW), _next_halo(tm, seq, 0), _full2((CW, D)),
                  _row(tm, CW, 0), _next_halo(tm, seq, 0), _row(tm, CW, 1), _row(tm, CW, 2),
                  _prev_halo(tm, 1), _prev_halo(tm, 2), _full2((8, CW))],
        out_specs=[_row(tm, 3 * CW), _full2((8, CW))],
        out_shape=[SDS((bsz, seq, 3 * CW), BF16), SDS((8, CW), F32)],
        compiler_params=_cp("arbitrary", "arbitrary"),
    )(dya, dya, wco, p1, p1, p1, p1, p1, p1, convw8)


def _disc(are, aim, ldt, bre, bim):
    dt = jnp.exp(ldt)
    mag = jnp.exp(are * dt)
    ang = aim * dt
    abr = mag * jnp.cos(ang)
    abi = mag * jnp.sin(ang)
    nr = abr - 1.0
    den = are * are + aim * aim
    cr = (nr * are + abi * aim) / den
    ci = (abi * are - nr * aim) / den
    return abr, abi, cr * bre - ci * bim, cr * bim + ci * bre


def _ssm_disc(are, aim, ldt, bre_t, bim_t):
    def body(are_ref, aim_ref, ldt_ref, bre_ref, bim_ref, abr_ref, abi_ref, bbr_ref, bbi_ref):
        abr, abi, bbr, bbi = _disc(are_ref[...], aim_ref[...], ldt_ref[...], bre_ref[...], bim_ref[...])
        abr_ref[...] = abr
        abi_ref[...] = abi
        bbr_ref[...] = bbr
        bbi_ref[...] = bbi

    v1, vh = SDS((1, GP), F32), SDS((NH, GP), F32)
    return pl.pallas_call(body, name="ssm_disc", out_shape=[v1, v1, vh, vh], compiler_params=_cp())(
        are, aim, ldt, bre_t, bim_t)


def _ssm_disc_bwd(are, aim, ldt, bre_t, bim_t, dabr, dabi, dbbr, dbbi):
    def body(are_ref, aim_ref, ldt_ref, bre_ref, bim_ref, g0, g1, g2, g3, o0, o1, o2, o3, o4):
        prim = (are_ref[...], aim_ref[...], ldt_ref[...], bre_ref[...], bim_ref[...])
        _, vjp = jax.vjp(_disc, *prim)
        d_are, d_aim, d_ldt, d_bre, d_bim = vjp((g0[...], g1[...], g2[...], g3[...]))
        o0[...] = d_are
        o1[...] = d_aim
        o2[...] = d_ldt
        o3[...] = d_bre
        o4[...] = d_bim

    v1, vh = SDS((1, GP), F32), SDS((NH, GP), F32)
    return pl.pallas_call(body, name="ssm_disc_bwd", out_shape=[v1, v1, v1, vh, vh], compiler_params=_cp())(
        are, aim, ldt, bre_t, bim_t, dabr, dabi, dbbr, dbbi)


def _scan_chunk(buf_ref, nt, bsz, ar, ai, init_r, init_i, reverse):
    nsub = SUB // bsz
    row = lax.broadcasted_iota(jnp.int32, (SUB, GP), 0)
    shift = ((SUB - bsz) if reverse else bsz) % SUB
    order = list(range(nsub - 1, -1, -1)) if reverse else list(range(nsub))

    def step(j, carry):
        pr, pi = carry
        jj = (nt - 1 - j) if reverse else j
        off = pl.multiple_of(jj * SUB, SUB)
        br = buf_ref[pl.ds(off, SUB), 0:GP]
        bi = buf_ref[pl.ds(off, SUB), GP:2 * GP]
        nr, ni = pr, pi
        for s in order:
            sr, si = nr, ni
            if shift:
                sr = pltpu.roll(sr, shift, 0)
                si = pltpu.roll(si, shift, 0)
            cr = ar * sr - ai * si + br
            ci = ar * si + ai * sr + bi
            if nsub == 1:
                nr, ni = cr, ci
            else:
                m = (row >= s * bsz) & (row < (s + 1) * bsz)
                nr = jnp.where(m, cr, nr)
                ni = jnp.where(m, ci, ni)
        buf_ref[pl.ds(off, SUB), 0:GP] = nr
        buf_ref[pl.ds(off, SUB), GP:2 * GP] = ni
        return nr, ni

    return lax.fori_loop(0, nt, step, (init_r, init_i))


def _ssm_chunk_rows(total_rows, bsz):
    return min(total_rows, 64 * bsz)


def _ssm_fwd(u_tm, wb, wct, ar8, ai8, bsz, name):
    rt = u_tm.shape[0]
    r = _ssm_chunk_rows(rt, bsz)
    nt = r // SUB

    def body(u_ref, wb_hbm, wct_hbm, ar_ref, ai_ref, x_ref, y_ref, wb_ref, wct_ref, st_ref):
        @pl.when(pl.program_id(0) == 0)
        def _():
            pltpu.sync_copy(wb_hbm, wb_ref)
            pltpu.sync_copy(wct_hbm, wct_ref)
            st_ref[...] = jnp.zeros_like(st_ref)

        x_ref[...] = _dot(u_ref[...].astype(BF16), wb_ref[...])
        fr, fi = _scan_chunk(x_ref, nt, bsz, ar_ref[...], ai_ref[...], st_ref[:, 0:GP], st_ref[:, GP:2 * GP], False)
        st_ref[:, 0:GP] = fr
        st_ref[:, GP:2 * GP] = fi
        y_ref[...] = _dot_nt(x_ref[...].astype(BF16), wct_ref[...])

    anyspec = pl.BlockSpec(memory_space=pl.ANY)
    return pl.pallas_call(
        body, name=name, grid=(rt // r,),
        in_specs=[pl.BlockSpec((r, SW), lambda i: (i, 0)), anyspec, anyspec, _full2((SUB, GP)), _full2((SUB, GP))],
        out_specs=[pl.BlockSpec((r, 2 * GP), lambda i: (i, 0)), pl.BlockSpec((r, SW), lambda i: (i, 0))],
        out_shape=[SDS((rt, 2 * GP), F32), SDS((rt, SW), F32)],
        scratch_shapes=[pltpu.VMEM((SW, 2 * GP), BF16), pltpu.VMEM((SW, 2 * GP), BF16), pltpu.VMEM((SUB, 2 * GP), F32)],
        compiler_params=_cp("arbitrary"),
    )(u_tm, wb, wct, ar8, ai8)


def _ssm_bwd(dy_tm, u_tm, xs, wb, wct, ar8, ai8, bsz, name):
    rt = u_tm.shape[0]
    r = _ssm_chunk_rows(rt, bsz)
    nt = r // SUB
    nc = rt // r
    ub = SW // SSM_SUPER
    sb = GP // SSM_SUPER

    def body(dy_ref, u_ref, x_ref, xh_ref, wb_hbm, wct_hbm, ar_ref, ai_ref,
             du_ref, dwb_hbm, dwct_hbm, dar_ref, dai_ref, wb_ref, wct_ref, g_ref, st_ref, awb_ref, awct_ref):
        i = pl.program_id(0)

        @pl.when(i == 0)
        def _():
            pltpu.sync_copy(wb_hbm, wb_ref)
            pltpu.sync_copy(wct_hbm, wct_ref)
            st_ref[...] = jnp.zeros_like(st_ref)
            awb_ref[...] = jnp.zeros_like(awb_ref)
            awct_ref[...] = jnp.zeros_like(awct_ref)
            dar_ref[...] = jnp.zeros_like(dar_ref)
            dai_ref[...] = jnp.zeros_like(dai_ref)

        dyb = dy_ref[...].astype(BF16)
        g_ref[...] = _dot(dyb, wct_ref[...])
        ar = ar_ref[...]
        ai = ai_ref[...]
        fr, fi = _scan_chunk(g_ref, nt, bsz, ar, -ai, st_ref[:, 0:GP], st_ref[:, GP:2 * GP], True)
        st_ref[:, 0:GP] = fr
        st_ref[:, GP:2 * GP] = fi

        gb = g_ref[...].astype(BF16)
        du_ref[...] = _dot_nt(gb, wb_ref[...])
        ub16 = u_ref[...].astype(BF16)
        xb16 = x_ref[...].astype(BF16)
        for s in range(SSM_SUPER):
            us = ub16[:, s * ub:(s + 1) * ub]
            ds = dyb[:, s * ub:(s + 1) * ub]
            for half in range(2):
                cols = slice(half * GP + s * sb, half * GP + (s + 1) * sb)
                ocols = slice(half * sb, (half + 1) * sb)
                awb_ref[s * ub:(s + 1) * ub, ocols] += _dot_tn(us, gb[:, cols])
                awct_ref[s * ub:(s + 1) * ub, ocols] += _dot_tn(ds, xb16[:, cols])

        gr = g_ref[:, 0:GP]
        gi = g_ref[:, GP:2 * GP]
        xsr = pltpu.roll(x_ref[:, 0:GP], bsz, 0)
        xsi = pltpu.roll(x_ref[:, GP:2 * GP], bsz, 0)
        inner = lax.broadcasted_iota(jnp.int32, (r, 1), 0) >= bsz
        t_r = jnp.where(inner, gr * xsr + gi * xsi, 0.0)
        t_i = jnp.where(inner, gi * xsr - gr * xsi, 0.0)
        acc_r = jnp.sum(t_r.reshape(nt, SUB, GP), axis=0)
        acc_i = jnp.sum(t_i.reshape(nt, SUB, GP), axis=0)
        hr = xh_ref[:, 0:GP]
        hi = xh_ref[:, GP:2 * GP]
        if bsz % SUB:
            hr = pltpu.roll(hr, bsz, 0)
            hi = pltpu.roll(hi, bsz, 0)
        edge = (lax.broadcasted_iota(jnp.int32, (SUB, 1), 0) < bsz) & (i < nc - 1)
        g0r = g_ref[0:SUB, 0:GP]
        g0i = g_ref[0:SUB, GP:2 * GP]
        dar_ref[...] += acc_r + jnp.where(edge, g0r * hr + g0i * hi, 0.0)
        dai_ref[...] += acc_i + jnp.where(edge, g0i * hr - g0r * hi, 0.0)

        @pl.when(i == nc - 1)
        def _():
            pltpu.sync_copy(awb_ref, dwb_hbm)
            pltpu.sync_copy(awct_ref, dwct_hbm)

    anyspec = pl.BlockSpec(memory_space=pl.ANY)
    rev = lambda i: (nc - 1 - i, 0)
    wshape = (SW, 2 * sb)
    return pl.pallas_call(
        body, name=name, grid=(nc,),
        in_specs=[pl.BlockSpec((r, SW), rev), pl.BlockSpec((r, SW), rev), pl.BlockSpec((r, 2 * GP), rev),
                  pl.BlockSpec((SUB, 2 * GP), lambda i: (jnp.maximum((nc - 1 - i) * nt - 1, 0), 0)),
                  anyspec, anyspec, _full2((SUB, GP)), _full2((SUB, GP))],
        out_specs=[pl.BlockSpec((r, SW), rev), anyspec, anyspec, _full2((SUB, GP)), _full2((SUB, GP))],
        out_shape=[SDS((rt, SW), F32), SDS(wshape, F32), SDS(wshape, F32), SDS((SUB, GP), F32), SDS((SUB, GP), F32)],
        scratch_shapes=[pltpu.VMEM((SW, 2 * GP), BF16), pltpu.VMEM((SW, 2 * GP), BF16),
                        pltpu.VMEM((r, 2 * GP), F32), pltpu.VMEM((SUB, 2 * GP), F32),
                        pltpu.VMEM(wshape, F32), pltpu.VMEM(wshape, F32)],
        compiler_params=_cp("arbitrary"),
    )(dy_tm, u_tm, xs, xs, wb, wct, ar8, ai8)


GELU_C = math.sqrt(2.0 / math.pi)


def _gelu(x):
    return 0.5 * x * (1.0 + jnp.tanh(GELU_C * (x + 0.044715 * x * x * x)))


def _gelu_grad(x):
    th = jnp.tanh(GELU_C * (x + 0.044715 * x * x * x))
    return 0.5 * (1.0 + th) + 0.5 * x * (1.0 - th * th) * GELU_C * (1.0 + 3.0 * 0.044715 * x * x)


def _ssm_post(ys, u, dskip, wglu, wso, name):
    bsz, seq, _ = ys.shape
    tm = _pick_tile(seq, (512, 256, 128))

    def body(ys_ref, u_ref, d_ref, wg_ref, wo_ref, s0_ref, z_ref, s1_ref, s2_ref, yb_ref):
        s0 = ys_ref[0] + d_ref[...] * u_ref[0]
        s1 = _gelu(s0)
        s1b = s1.astype(BF16)
        z = _dot(s1b, wg_ref[...])
        s2b = (s1 * _sigmoid(z)).astype(BF16)
        s0_ref[0] = s0
        z_ref[0] = z
        s1_ref[0] = s1b
        s2_ref[0] = s2b
        yb_ref[0] = _dot(s2b, wo_ref[...])

    return pl.pallas_call(
        body, name=name, grid=(bsz, seq // tm),
        in_specs=[_row(tm, SW), _row(tm, SW), _full2((1, SW)), _full2((SW, SW)), _full2((SW, D))],
        out_specs=[_row(tm, SW), _row(tm, SW), _row(tm, SW), _row(tm, SW), _row(tm, D)],
        out_shape=[SDS((bsz, seq, SW), F32), SDS((bsz, seq, SW), F32), SDS((bsz, seq, SW), BF16),
                   SDS((bsz, seq, SW), BF16), SDS((bsz, seq, D), F32)],
        compiler_params=_cp("parallel", "parallel"),
    )(ys, u, dskip, wglu, wso)


def _ssm_post_bwd(dyb, s0, z, u, dskip, wglu, wso, name):
    bsz, seq, _ = s0.shape
    tm = _pick_tile(seq, (512, 256, 128))

    def body(dyb_ref, s0_ref, z_ref, u_ref, wg_ref, wo_ref, ds0_ref, dz_ref, dd_ref):
        b = pl.program_id(0)
        i = pl.program_id(1)
        ds2 = _dot_nt(dyb_ref[0], wo_ref[...])
        s0 = s0_ref[0]
        s1 = _gelu(s0)
        sg = _sigmoid(z_ref[0])
        dz = ds2 * s1 * sg * (1.0 - sg)
        dzb = dz.astype(BF16)
        ds1 = ds2 * sg + _dot_nt(dzb, wg_ref[...])
        ds0 = ds1 * _gelu_grad(s0)
        ds0_ref[0] = ds0
        dz_ref[0] = dzb
        part = jnp.sum(ds0 * u_ref[0], axis=0, keepdims=True)

        @pl.when((i == 0) & (b == 0))
        def _():
            dd_ref[...] = part

        @pl.when((i > 0) | (b > 0))
        def _():
            dd_ref[...] += part

    del dskip
    return pl.pallas_call(
        body, name=name, grid=(bsz, seq // tm),
        in_specs=[_row(tm, D), _row(tm, SW), _row(tm, SW), _row(tm, SW), _full2((SW, SW)), _full2((SW, D))],
        out_specs=[_row(tm, SW), _row(tm, SW), _full2((1, SW))],
        out_shape=[SDS((bsz, seq, SW), F32), SDS((bsz, seq, SW), BF16), SDS((1, SW), F32)],
        compiler_params=_cp("arbitrary", "arbitrary"),
    )(dyb, s0, z, u, wglu, wso)


def _du_combine(du_ssm, ds0, dskip, name):
    bsz, seq, _ = ds0.shape
    tm = _pick_tile(seq, (512, 256, 128))

    def body(a_ref, b_ref, d_ref, o_ref):
        o_ref[0] = (a_ref[0] + b_ref[0] * d_ref[...]).astype(BF16)

    return pl.pallas_call(
        body, name=name, grid=(bsz, seq // tm),
        in_specs=[_row(tm, SW), _row(tm, SW), _full2((1, SW))],
        out_specs=_row(tm, SW), out_shape=SDS((bsz, seq, SW), BF16),
        compiler_params=_cp("parallel", "parallel"),
    )(du_ssm, ds0, dskip)


def _merge_out(ya, yb, p3, wout, x1, gt, name):
    bsz, seq, _ = ya.shape
    tm = _pick_tile(seq, (512, 256, 128))

    def body(ya_ref, yb_ref, ga_ref, gbb_ref, w_ref, x_ref, gt_ref, mg_ref, mix_ref, xo_ref):
        merged = (_sigmoid(ga_ref[0]) * ya_ref[0] + _sigmoid(gbb_ref[0]) * yb_ref[0]).astype(BF16)
        mix = _dot(merged, w_ref[...])
        mg_ref[0] = merged
        mix_ref[0] = mix
        xo_ref[0] = x_ref[0] + gt_ref[0] * mix

    return pl.pallas_call(
        body, name=name, grid=(bsz, seq // tm),
        in_specs=[_row(tm, D), _row(tm, D), _row(tm, D, 0), _row(tm, D, 1), _full2((D, D)), _row(tm, D), _seqvec(D)],
        out_specs=[_row(tm, D), _row(tm, D), _row(tm, D)],
        out_shape=[SDS((bsz, seq, D), BF16), SDS((bsz, seq, D), F32), SDS((bsz, seq, D), F32)],
        compiler_params=_cp("parallel", "parallel"),
    )(ya, yb, p3, p3, wout, x1, gt)


def _merge_bwd(dx2, gt, mix, ya, yb, p3, wout, name):
    bsz, seq, _ = ya.shape
    tm = _pick_tile(seq, (512, 256, 128))

    def body(dx_ref, gt_ref, mix_ref, ya_ref, yb_ref, ga_ref, gbb_ref, w_ref, dmix_ref, dya_ref, dyb_ref, dp_ref, dgt_ref):
        i = pl.program_id(1)
        dx = dx_ref[0]
        dmix = (gt_ref[0] * dx).astype(BF16)
        dmix_ref[0] = dmix
        part = jnp.sum(dx * mix_ref[0], axis=0, keepdims=True)

        @pl.when(i == 0)
        def _():
            dgt_ref[0] = part

        @pl.when(i > 0)
        def _():
            dgt_ref[0] += part

        dmg = _dot_nt(dmix, w_ref[...])
        sa = _sigmoid(ga_ref[0])
        sb = _sigmoid(gbb_ref[0])
        dya_ref[0] = (dmg * sa).astype(BF16)
        dyb_ref[0] = (dmg * sb).astype(BF16)
        dp_ref[0, :, 0:D] = (dmg * ya_ref[0] * sa * (1.0 - sa)).astype(BF16)
        dp_ref[0, :, D:2 * D] = (dmg * yb_ref[0] * sb * (1.0 - sb)).astype(BF16)

    bshape = SDS((bsz, seq, D), BF16)
    return pl.pallas_call(
        body, name=name, grid=(bsz, seq // tm),
        in_specs=[_row(tm, D), _seqvec(D), _row(tm, D), _row(tm, D), _row(tm, D), _row(tm, D, 0), _row(tm, D, 1),
                  _full2((D, D))],
        out_specs=[_row(tm, D), _row(tm, D), _row(tm, D), _row(tm, 2 * D), _seqvec(D)],
        out_shape=[bshape, bshape, bshape, SDS((bsz, seq, 2 * D), BF16), SDS((bsz, 1, D), F32)],
        compiler_params=_cp("arbitrary", "arbitrary"),
    )(dx2, gt, mix, ya, yb, p3, p3, wout)


def _final_loss(x3, gfin, target, name):
    bsz, seq, dm = x3.shape
    tm = _pick_tile(seq, (512, 256, 128))

    def body(x_ref, g_ref, t_ref, dx_ref, loss_ref, dg_ref):
        b = pl.program_id(0)
        i = pl.program_id(1)
        xf = x_ref[0]
        gv = g_ref[...]
        r = lax.rsqrt(jnp.mean(xf * xf, axis=-1, keepdims=True) + EPS)
        xhat = xf * r
        e = xhat * gv - t_ref[0]
        dy = e * (1.0 / dm)
        dxh = dy * gv
        dx_ref[0] = r * (dxh - xhat * jnp.mean(dxh * xhat, axis=-1, keepdims=True))
        p_l = jnp.sum(e * e, axis=0, keepdims=True) * (0.5 / dm)
        p_g = jnp.sum(dy * xhat, axis=0, keepdims=True)

        @pl.when((i == 0) & (b == 0))
        def _():
            loss_ref[...] = p_l
            dg_ref[...] = p_g

        @pl.when((i > 0) | (b > 0))
        def _():
            loss_ref[...] += p_l
            dg_ref[...] += p_g

    return pl.pallas_call(
        body, name=name, grid=(bsz, seq // tm),
        in_specs=[_row(tm, dm), _full2((1, dm)), _row(tm, dm)],
        out_specs=[_row(tm, dm), _full2((1, dm)), _full2((1, dm))],
        out_shape=[SDS((bsz, seq, dm), F32), SDS((1, dm), F32), SDS((1, dm), F32)],
        compiler_params=_cp("arbitrary", "arbitrary"),
    )(x3, gfin, target)


def _ada_fwd(c_all, w_shard, b_shard):
    nb = c_all.shape[0]
    n = w_shard.shape[1]

    def body(c_ref, w_ref, b_ref, o_ref):
        cv = c_ref[...]
        cond = (cv * _sigmoid(cv)).astype(BF16)
        o_ref[...] = _dot(cond, w_ref[...].astype(BF16)) + b_ref[...]

    return pl.pallas_call(body, name="ada_fwd", out_shape=SDS((nb, n), F32), compiler_params=_cp())(
        c_all, w_shard, b_shard)


def _ada_bwd(c_all, dmod_shard, dmod_all):
    n = dmod_shard.shape[1]

    def body(c_ref, ds_ref, da_ref, gw_ref, gb_ref):
        cv = c_ref[...]
        cond = (cv * _sigmoid(cv)).astype(BF16)
        gw_ref[...] = _dot_tn(cond, ds_ref[...].astype(BF16))
        gb_ref[...] = jnp.sum(da_ref[...], axis=0, keepdims=True)

    return pl.pallas_call(
        body, name="ada_bwd", out_shape=[SDS((D, n), F32), SDS((1, dmod_all.shape[1]), F32)], compiler_params=_cp(),
    )(c_all, dmod_shard, dmod_all)


def _adamw_math(w, g, m, v):
    m = B1 * m + (1.0 - B1) * g
    v = B2 * v + (1.0 - B2) * (g * g)
    delta = -LR * ((m / BC1) / (jnp.sqrt(v / BC2) + AEPS) + WD * w)
    return delta, m, v


def _adamw_big(w, m, v, p_own, p_sib, row0, name):
    rows = w.shape[0]
    tr = 64
    blk0 = row0 // tr

    def body(w_ref, m_ref, v_ref, a_ref, b_ref, g_ref, d_ref, mo_ref, vo_ref):
        g = a_ref[...] + b_ref[...]
        delta, mn, vn = _adamw_math(w_ref[...], g, m_ref[...], v_ref[...])
        g_ref[...] = g
        d_ref[...] = delta
        mo_ref[...] = mn
        vo_ref[...] = vn

    own = pl.BlockSpec((tr, 1024), lambda i: (i, 0))
    packed = pl.BlockSpec((tr, 1024), lambda i: (blk0 + i, 0))
    shp = SDS(w.shape, F32)
    return pl.pallas_call(
        body, name=name, grid=(rows // tr,),
        in_specs=[own, own, own, packed, packed], out_specs=[own, own, own, own], out_shape=[shp, shp, shp, shp],
        compiler_params=_cp("parallel"),
    )(w, m, v, p_own, p_sib)


def _adamw_plain(w, m, v, g, name):
    def body(w_ref, m_ref, v_ref, g_ref, d_ref, mo_ref, vo_ref):
        delta, mn, vn = _adamw_math(w_ref[...], g_ref[...], m_ref[...], v_ref[...])
        d_ref[...] = delta
        mo_ref[...] = mn
        vo_ref[...] = vn

    shp = SDS(w.shape, F32)
    return pl.pallas_call(body, name=name, out_shape=[shp, shp, shp], compiler_params=_cp())(w, m, v, g)


def _adamw_rows(w, m, v, g, name):
    rows, cols = w.shape
    tr = _pick_tile(rows, (128, 64, 32, 16, 8))

    def body(w_ref, m_ref, v_ref, g_ref, d_ref, mo_ref, vo_ref):
        delta, mn, vn = _adamw_math(w_ref[...], g_ref[...], m_ref[...], v_ref[...])
        d_ref[...] = delta
        mo_ref[...] = mn
        vo_ref[...] = vn

    spec = pl.BlockSpec((tr, cols), lambda i: (i, 0))
    shp = SDS(w.shape, F32)
    return pl.pallas_call(
        body, name=name, grid=(rows // tr,), in_specs=[spec] * 4, out_specs=[spec] * 3, out_shape=[shp] * 3,
        compiler_params=_cp("parallel"),
    )(w, m, v, g)


def _sum_slabs(r, name):
    n, rows, cols = r.shape
    tr = _pick_tile(rows, (256, 128, 64))

    def body(r_ref, o_ref):
        acc = r_ref[0].astype(F32)
        for j in range(1, n):
            acc = acc + r_ref[j].astype(F32)
        o_ref[...] = acc

    return pl.pallas_call(
        body, name=name, grid=(rows // tr,),
        in_specs=[pl.BlockSpec((n, tr, cols), lambda i: (0, i, 0))],
        out_specs=pl.BlockSpec((tr, cols), lambda i: (i, 0)), out_shape=SDS((rows, cols), F32),
        compiler_params=_cp("parallel"),
    )(r)


def _sum_groups(recvs, name):
    ng = len(recvs)
    n, rows, cols = recvs[0].shape
    tr = _pick_tile(rows, (256, 128, 64))
    nblk = rows // tr

    def body(*refs):
        o_ref = refs[ng]
        gid = pl.program_id(0)
        for g in range(ng):
            @pl.when(gid == g)
            def _(g=g):
                acc = refs[g][0].astype(F32)
                for j in range(1, n):
                    acc = acc + refs[g][j].astype(F32)
                o_ref[...] = acc

    def spec(g):
        return pl.BlockSpec((n, tr, cols), lambda gid, i: (0, jnp.where(gid == g, i, 0), 0))

    return pl.pallas_call(
        body, name=name, grid=(ng, nblk), in_specs=[spec(g) for g in range(ng)],
        out_specs=pl.BlockSpec((tr, cols), lambda gid, i: (gid * nblk + i, 0)),
        out_shape=SDS((ng * rows, cols), F32), compiler_params=_cp("arbitrary", "arbitrary"),
    )(*recvs)


def _place():
    return lax.axis_index("x"), lax.axis_index("y"), lax.axis_index("c")


def _all_gather8(blk, name):
    m_per, n = blk.shape

    def body(x_ref, out_ref, send_sems, recv_sems, local_sem):
        x, y, c = _place()
        me, sibling = (x, y, c), (x, y, 1 - c)
        chips = [(1 - x, y), (x, 1 - y), (1 - x, 1 - y)]

        def rows(px, py, pc):
            return out_ref.at[pl.ds((4 * px + 2 * py + pc) * m_per, m_per), :]

        def copy(k, block, to, src=None):
            return pltpu.make_async_remote_copy(
                src_ref=rows(*block) if src is None else src, dst_ref=rows(*block),
                send_sem=send_sems.at[k], recv_sem=recv_sems.at[k], device_id=to, device_id_type=MESH)

        mine = pltpu.make_async_copy(x_ref, rows(*me), local_sem)
        mine.start()
        first = [copy(0, me, sibling, src=x_ref)]
        first += [copy(1 + j, me, (*chip, c), src=x_ref) for j, chip in enumerate(chips)]
        for cp in first:
            cp.start()
        passed = [copy(4 + j, (*chip, c), sibling) for j, chip in enumerate(chips)]
        for j, chip in enumerate(chips):
            copy(1 + j, (*chip, c), me).wait_recv()
            passed[j].start()
        copy(0, sibling, me).wait_recv()
        for j, chip in enumerate(chips):
            copy(4 + j, (*chip, 1 - c), me).wait_recv()
        for cp in first + passed:
            cp.wait_send()
        mine.wait()

    return pl.pallas_call(
        body, name=name, out_shape=SDS((N_DEV * m_per, n), blk.dtype),
        in_specs=[pl.BlockSpec(memory_space=pltpu.VMEM)], out_specs=pl.BlockSpec(memory_space=pltpu.VMEM),
        scratch_shapes=[pltpu.SemaphoreType.DMA((7,)), pltpu.SemaphoreType.DMA((7,)), pltpu.SemaphoreType.DMA],
        compiler_params=pltpu.CompilerParams(vmem_limit_bytes=VMEM_LIMIT),
    )(blk)


def _chip_peers(x, y):
    return [(1 - x, y), (x, 1 - y), (1 - x, 1 - y)]


def _gather_chips(shard, name):
    def body(in_ref, out_ref, send_sems, recv_sems, local_sem):
        x, y, c = _place()
        mine = pltpu.make_async_copy(in_ref, out_ref.at[2 * x + y], local_sem)
        mine.start()
        cps = []
        for j, (px, py) in enumerate(_chip_peers(x, y)):
            cp = pltpu.make_async_remote_copy(
                src_ref=in_ref, dst_ref=out_ref.at[2 * x + y], send_sem=send_sems.at[j], recv_sem=recv_sems.at[j],
                device_id=(px, py, c), device_id_type=MESH)
            cp.start()
            cps.append(cp)
        for j, (px, py) in enumerate(_chip_peers(x, y)):
            pltpu.make_async_remote_copy(
                src_ref=in_ref, dst_ref=out_ref.at[2 * px + py], send_sem=send_sems.at[j], recv_sem=recv_sems.at[j],
                device_id=(px, py, c), device_id_type=MESH).wait_recv()
        for cp in cps:
            cp.wait_send()
        mine.wait()

    anyspec = pl.BlockSpec(memory_space=pl.ANY)
    return pl.pallas_call(
        body, name=name, out_shape=SDS((N_CHIPS,) + shard.shape, shard.dtype),
        in_specs=[anyspec], out_specs=anyspec,
        scratch_shapes=[pltpu.SemaphoreType.DMA((3,)), pltpu.SemaphoreType.DMA((3,)), pltpu.SemaphoreType.DMA],
        compiler_params=pltpu.CompilerParams(vmem_limit_bytes=VMEM_LIMIT),
    )(shard)


def _scatter_chips(stacked, name):
    def body(in_ref, out_ref, send_sems, recv_sems, local_sem):
        x, y, c = _place()
        me = 2 * x + y
        mine = pltpu.make_async_copy(in_ref.at[me], out_ref.at[me], local_sem)
        mine.start()
        cps = []
        for j, (px, py) in enumerate(_chip_peers(x, y)):
            cp = pltpu.make_async_remote_copy(
                src_ref=in_ref.at[2 * px + py], dst_ref=out_ref.at[me], send_sem=send_sems.at[j],
                recv_sem=recv_sems.at[j], device_id=(px, py, c), device_id_type=MESH)
            cp.start()
            cps.append(cp)
        for j, (px, py) in enumerate(_chip_peers(x, y)):
            pltpu.make_async_remote_copy(
                src_ref=in_ref.at[me], dst_ref=out_ref.at[2 * px + py], send_sem=send_sems.at[j],
                recv_sem=recv_sems.at[j], device_id=(px, py, c), device_id_type=MESH).wait_recv()
        for cp in cps:
            cp.wait_send()
        mine.wait()

    anyspec = pl.BlockSpec(memory_space=pl.ANY)
    return pl.pallas_call(
        body, name=name, out_shape=SDS(stacked.shape, stacked.dtype),
        in_specs=[anyspec], out_specs=anyspec,
        scratch_shapes=[pltpu.SemaphoreType.DMA((3,)), pltpu.SemaphoreType.DMA((3,)), pltpu.SemaphoreType.DMA],
        compiler_params=pltpu.CompilerParams(vmem_limit_bytes=VMEM_LIMIT),
    )(stacked)


_HBM = pl.BlockSpec(memory_space=pltpu.HBM)
_SEM = pl.BlockSpec(memory_space=pltpu.SEMAPHORE)
_EFFECT = pltpu.SideEffectType.DATAFLOW_SIDE_EFFECTING


def _peer_copies(src_ref, land_ref, send_sems, recv_sems, scatter, landed):
    x, y, c = _place()
    cps = []
    for j, (px, py) in enumerate(_chip_peers(x, y)):
        src = src_ref.at[2 * px + py] if scatter else src_ref
        dst = land_ref.at[2 * px + py] if landed else land_ref.at[2 * x + y]
        cps.append(pltpu.make_async_remote_copy(
            src_ref=src, dst_ref=dst, send_sem=send_sems.at[j], recv_sem=recv_sems.at[j],
            device_id=(px, py, c), device_id_type=MESH))
    return cps


def _exchange_begin(src, land, scatter, name):
    def body(src_ref, land_ref, send_sems, recv_sems, src_thru, land_thru, token):
        del src_thru, land_thru
        for cp in _peer_copies(src_ref, land_ref, send_sems, recv_sems, scatter, False):
            cp.start()
        token[...] = jnp.zeros_like(token)

    return pl.pallas_call(
        body, name=name,
        out_shape=(pltpu.SemaphoreType.DMA((3,)), pltpu.SemaphoreType.DMA((3,)), pltpu.HBM(src.shape, src.dtype),
                   pltpu.HBM(land.shape, land.dtype), SDS((SUB, LANE), F32)),
        in_specs=(_HBM, _HBM), out_specs=(_SEM, _SEM, _HBM, _HBM, pl.BlockSpec(memory_space=pltpu.VMEM)),
        input_output_aliases={0: 2, 1: 3},
        compiler_params=pltpu.CompilerParams(has_side_effects=_EFFECT),
    )(pltpu.with_memory_space_constraint(src, pltpu.HBM), pltpu.with_memory_space_constraint(land, pltpu.HBM))


def _exchange_end(handle, after, scatter, name):
    send_sems, recv_sems, src_thru, land_thru, _ = handle

    def body(src_ref, land_ref, send_sems, recv_sems, after_ref, src_dead, land_out):
        del after_ref, src_dead, land_out
        for cp in _peer_copies(src_ref, land_ref, send_sems, recv_sems, scatter, True):
            cp.wait_send()
            cp.wait_recv()

    return pl.pallas_call(
        body, name=name,
        out_shape=(pltpu.HBM(src_thru.shape, src_thru.dtype), pltpu.HBM(land_thru.shape, land_thru.dtype)),
        in_specs=(_HBM, _HBM, _SEM, _SEM, pl.BlockSpec(memory_space=pl.ANY)), out_specs=(_HBM, _HBM),
        input_output_aliases={0: 0, 1: 1},
        compiler_params=pltpu.CompilerParams(has_side_effects=_EFFECT),
    )(src_thru, land_thru, send_sems, recv_sems, after)[1]


def _own_slab(stack4, chip):
    idx = lax.broadcasted_iota(jnp.int32, (N_CHIPS,) + (1,) * (stack4.ndim - 1), 0)
    return jnp.where(idx == chip, stack4, jnp.zeros((), stack4.dtype))


def _swap_sibling(v, name):
    def body(in_ref, out_ref, send_sem, recv_sem):
        x, y, c = _place()
        cp = pltpu.make_async_remote_copy(
            src_ref=in_ref, dst_ref=out_ref, send_sem=send_sem, recv_sem=recv_sem,
            device_id=(x, y, 1 - c), device_id_type=MESH)
        cp.start()
        cp.wait()

    anyspec = pl.BlockSpec(memory_space=pl.ANY)
    return pl.pallas_call(
        body, name=name, out_shape=SDS(v.shape, v.dtype), in_specs=[anyspec], out_specs=anyspec,
        scratch_shapes=[pltpu.SemaphoreType.DMA, pltpu.SemaphoreType.DMA],
        compiler_params=pltpu.CompilerParams(vmem_limit_bytes=VMEM_LIMIT),
    )(v)


def _select(stacked, idx):
    out = stacked[0]
    for j in range(1, stacked.shape[0]):
        out = jnp.where(idx == j, stacked[j], out)
    return out


BIG = (
    ("w1_a", D, DFF // 4, True), ("w3_a", D, DFF // 4, True), ("w2_a", DFF // 4, D, False),
    ("w_in", D, 5632 // 4, True), ("w_conv_out", CW // 4, D, False), ("w_glu", SW // 4, SW, False),
    ("w_ssm_out", SW, D // 4, True), ("w_out", D // 4, D, False),
    ("w1_b", D, DFF // 4, True), ("w3_b", D, DFF // 4, True), ("w2_b", DFF // 4, D, False),
)
PACK_COLS = 1024


def _pack_rows():
    offs, r = {}, 0
    for name, rows, cols, _ in BIG:
        offs[name] = r
        r += rows * cols // PACK_COLS
    return offs, r


def _full_from_stacked(st, rows, cols, split_cols):
    st = st.reshape(N_CHIPS, rows, cols)
    if split_cols:
        return st.transpose(1, 0, 2).reshape(rows, N_CHIPS * cols)
    return st.reshape(N_CHIPS * rows, cols)


def _stacked_from_full(full, rows, cols, split_cols):
    if split_cols:
        st = full.reshape(rows, N_CHIPS, cols).transpose(1, 0, 2)
    else:
        st = full.reshape(N_CHIPS, rows, cols)
    return st.reshape(N_CHIPS, rows * cols // PACK_COLS, PACK_COLS)


def _blockdiag(t):
    r = lax.broadcasted_iota(jnp.int32, (SW, GP), 0) // NH
    cidx = lax.broadcasted_iota(jnp.int32, (SW, GP), 1) // NP
    return jnp.where(r == cidx, jnp.tile(t, (NG, 1)), 0.0)


def _blockdiag_extract(acc):
    gs = NG // SSM_SUPER
    a = acc.reshape(NG, NH, gs, NP)
    sel = (lax.broadcasted_iota(jnp.int32, (NG, 1, gs, 1), 0) % gs) == lax.broadcasted_iota(jnp.int32, (NG, 1, gs, 1), 2)
    a = jnp.sum(jnp.where(sel, a, 0.0), axis=2)
    return a.transpose(1, 0, 2).reshape(NH, GP)


def _to_t(p):
    return p.transpose(2, 0, 1).reshape(NH, GP)


def _from_t(t):
    return t.reshape(NH, NG, NP).transpose(1, 2, 0)


def _c_to_t(p):
    return p.transpose(1, 0, 2).reshape(NH, GP)


def _c_from_t(t):
    return t.reshape(NH, NG, NP).transpose(1, 0, 2)


def _ffn_forward(x, g, sh, sc, gt, w1, w3, w2, tag):
    h = _norm_mod(x, g, sh, sc, f"{tag}_norm")
    a, b, hid = _swiglu_up(h, w1, w3, f"{tag}_up")
    f, xo = _ffn_down(hid, w2, x, gt, f"{tag}_down")
    return xo, (x, h, a, b, hid, f)


def _ffn_backward(dxo, saved, g, sc, gt, w1, w3, w2, tag):
    x, h, a, b, hid, f = saved
    dfs, da, db, dgt = _ffn_bwd_hid(dxo, gt, f, a, b, w2, f"{tag}_bwd_hid")
    w13 = jnp.concatenate([w1, w3], axis=1)
    tk = _pick_tile(a.shape[2], (1408, 512, 256, 128))
    dx, dsh, dsc, dg = _dh_norm_bwd([da, db], w13, tk, x, g, sc, dxo, f"{tag}_bwd_dh")
    h2, da2, db2 = _flat(h), _flat(da), _flat(db)
    gw1 = _mm(h2, da2, ta=True, name=f"{tag}_gw1")
    gw3 = _mm(h2, db2, ta=True, name=f"{tag}_gw3")
    gw2 = _mm(_flat(hid), _flat(dfs), ta=True, name=f"{tag}_gw2")
    return dx, (dsh, dsc, dgt, dg), (gw1, gw3, gw2)


def kernel(x, c, w_ada, b_ada, g_ffn1, w1_a, w3_a, w2_a, g_mix, w_in, conv_w, w_conv_out, a_re, a_im, b_re, b_im, c_re, c_im, log_dt, d_skip, w_glu, w_ssm_out, w_out, g_ffn2, w1_b, w3_b, w2_b, g_final, loss_target, m_w_ada, m_b_ada, m_g_ffn1, m_w1_a, m_w3_a, m_w2_a, m_g_mix, m_w_in, m_conv_w, m_w_conv_out, m_a_re, m_a_im, m_b_re, m_b_im, m_c_re, m_c_im, m_log_dt, m_d_skip, m_w_glu, m_w_ssm_out, m_w_out, m_g_ffn2, m_w1_b, m_w3_b, m_w2_b, m_g_final, v_w_ada, v_b_ada, v_g_ffn1, v_w1_a, v_w3_a, v_w2_a, v_g_mix, v_w_in, v_conv_w, v_w_conv_out, v_a_re, v_a_im, v_b_re, v_b_im, v_c_re, v_c_im, v_log_dt, v_d_skip, v_w_glu, v_w_ssm_out, v_w_out, v_g_ffn2, v_w1_b, v_w3_b, v_w2_b, v_g_final):
    args = dict(locals())
    names = ["w_ada", "b_ada", "g_ffn1", "w1_a", "w3_a", "w2_a", "g_mix", "w_in", "conv_w", "w_conv_out", "a_re",
             "a_im", "b_re", "b_im", "c_re", "c_im", "log_dt", "d_skip", "w_glu", "w_ssm_out", "w_out", "g_ffn2",
             "w1_b", "w3_b", "w2_b", "g_final"]
    bsz, seq, _ = x.shape
    mx, my, mc = _place()
    chip = 2 * mx + my
    dev = 4 * mx + 2 * my + mc

    offs, pack_rows = _pack_rows()
    groups = (BIG[:3], BIG[3:8], BIG[8:])

    def pack_group(grp):
        return jnp.concatenate([args[n][0].astype(BF16).reshape(-1, PACK_COLS) for n, _, _, _ in grp], axis=0)

    wfull = {}

    def unpack_group(gathered, grp):
        base = offs[grp[0][0]]
        for n, rows, cols, split in grp:
            nr = rows * cols // PACK_COLS
            wfull[n] = _full_from_stacked(gathered[:, offs[n] - base:offs[n] - base + nr], rows, cols, split)

    unpack_group(_gather_chips(pack_group(groups[0]), "gather_w_ffn1"), groups[0])
    packed_rest = pack_group(groups[1] + groups[2])
    rest_land = _own_slab(jnp.broadcast_to(packed_rest[None], (N_CHIPS,) + packed_rest.shape), chip)
    rest_handle = _exchange_begin(packed_rest, rest_land, False, "gather_w_rest_start")

    nmod_shard = NMOD * D // N_CHIPS
    c_all = _all_gather8(c.reshape(SUB, -1), "gather_c").reshape(N_DEV * bsz, D)
    b_shard = _select(b_ada.reshape(N_CHIPS, 1, nmod_shard), chip)
    mod_shard = _ada_fwd(c_all, w_ada[0], b_shard)
    nb = N_DEV * bsz
    cw_pad = jnp.pad(conv_w[0], ((0, SUB - 3), (0, nmod_shard - CW // N_CHIPS)))
    mod_st = _gather_chips(jnp.concatenate([mod_shard, cw_pad], axis=0), "gather_mod")
    mod_all = mod_st[:, :nb].transpose(1, 0, 2).reshape(N_DEV, bsz, NMOD * D)
    mod = _select(mod_all, dev)
    sh1, sc1, gt1, sh2, sc2, gt2, sh3, sc3, gt3 = [mod[:, None, j * D:(j + 1) * D] for j in range(NMOD)]
    convw = mod_st[:, nb:nb + 3, :CW // N_CHIPS].transpose(1, 0, 2).reshape(3, CW)
    convw8 = jnp.pad(convw, ((0, SUB - 3), (0, 0)))

    are, aim = a_re.reshape(1, GP), a_im.reshape(1, GP)
    ldt = jnp.broadcast_to(log_dt.reshape(NG, 1), (NG, NP)).reshape(1, GP)
    bre_t, bim_t = _to_t(b_re[0]), _to_t(b_im[0])
    abr, abi, bbr_t, bbi_t = _ssm_disc(are, aim, ldt, bre_t, bim_t)
    wb = jnp.concatenate([_blockdiag(bbr_t), _blockdiag(bbi_t)], axis=1).astype(BF16)
    wct = jnp.concatenate([_blockdiag(_c_to_t(c_re[0])), -_blockdiag(_c_to_t(c_im[0]))], axis=1).astype(BF16)
    ar8 = jnp.broadcast_to(abr, (SUB, GP))
    ai8 = jnp.broadcast_to(abi, (SUB, GP))

    x1, ffn1_saved = _ffn_forward(x, g_ffn1 + rest_handle[4][0:1, 0:1], sh1, sc1, gt1,
                                  wfull["w1_a"], wfull["w3_a"], wfull["w2_a"], "ffn1")
    rest = _exchange_end(rest_handle, x1, False, "gather_w_rest_wait")
    unpack_group(rest, groups[1] + groups[2])

    h2 = _norm_mod(x1, g_mix, sh2, sc2, "mix_norm")
    h2f = _flat(h2)
    win = wfull["w_in"]
    win1, winu, win3 = win[:, :3 * CW], win[:, 3 * CW:3 * CW + SW], win[:, 3 * CW + SW:]
    p1 = _mm(h2f, win1, name="mix_in1").reshape(bsz, seq, 3 * CW)
    u = _mm(h2f, winu, name="mix_inu").reshape(bsz, seq, SW)
    p3 = _mm(h2f, win3, name="mix_in3").reshape(bsz, seq, 2 * D)

    ya_in = _conv_fwd(p1, convw8, "conv_fwd")
    ya = _mm(_flat(ya_in), wfull["w_conv_out"], name="conv_out").reshape(bsz, seq, D)

    u_tm = u.transpose(1, 0, 2).reshape(seq * bsz, SW)
    xs, y_tm = _ssm_fwd(u_tm, wb, wct, ar8, ai8, bsz, "ssm_fwd")
    ys = y_tm.reshape(seq, bsz, SW).transpose(1, 0, 2)
    s0, z, s1, s2, yb = _ssm_post(ys, u, d_skip, wfull["w_glu"], wfull["w_ssm_out"], "ssm_post")

    merged, mix, x2 = _merge_out(ya, yb, p3, wfull["w_out"], x1, gt2, "merge_out")
    x3, ffn2_saved = _ffn_forward(x2, g_ffn2, sh3, sc3, gt3, wfull["w1_b"], wfull["w3_b"], wfull["w2_b"], "ffn2")

    dx3, lossvec, dgfin = _final_loss(x3, g_final.reshape(1, D), loss_target, "final_loss")
    loss = lax.psum(jnp.sum(lossvec), ("x", "y", "c"))

    gfull = {}
    dx2, (dsh3, dsc3, dgt3, dg3), (gfull["w1_b"], gfull["w3_b"], gfull["w2_b"]) = _ffn_backward(
        dx3, ffn2_saved, g_ffn2, sc3, gt3, wfull["w1_b"], wfull["w3_b"], wfull["w2_b"], "ffn2")

    def stack_group(grp):
        return jnp.concatenate(
            [_stacked_from_full(gfull[n], rows, cols, split) for n, rows, cols, split in grp], axis=1).astype(BF16)

    st_ffn2 = stack_group(groups[2])
    h_ffn2 = _exchange_begin(st_ffn2, _own_slab(st_ffn2, chip), True, "scatter_ffn2_start")

    dmix, dya, dyb, dp3, dgt2 = _merge_bwd(dx2, gt2 + h_ffn2[4][0, 0], mix, ya, yb, p3, wfull["w_out"], "merge_bwd")
    gfull["w_out"] = _mm(_flat(merged), _flat(dmix), ta=True, name="gw_out")
    dp1, dconvw8 = _conv_bwd(dya, wfull["w_conv_out"], p1, convw8, "conv_bwd")
    gfull["w_conv_out"] = _mm(_flat(ya_in), _flat(dya), ta=True, name="gw_conv_out")
    ds0, dz, ddskip = _ssm_post_bwd(dyb, s0, z, u, d_skip, wfull["w_glu"], wfull["w_ssm_out"], "ssm_post_bwd")
    gfull["w_ssm_out"] = _mm(_flat(s2), _flat(dyb), ta=True, name="gw_ssm_out")
    gfull["w_glu"] = _mm(_flat(s1), _flat(dz), ta=True, name="gw_glu")
    dy_tm = ds0.transpose(1, 0, 2).reshape(seq * bsz, SW)
    du_tm, dwb, dwct, dar8, dai8 = _ssm_bwd(dy_tm, u_tm, xs, wb, wct, ar8, ai8, bsz, "ssm_bwd")
    du = _du_combine(du_tm.reshape(seq, bsz, SW).transpose(1, 0, 2), ds0, d_skip, "du_combine")
    dx1, dsh2, dsc2, dgmix = _dh_norm_bwd([dp1, du, dp3], win, SW, x1, g_mix, sc2, dx2, "mix_bwd_dh")
    gfull["w_in"] = jnp.concatenate([
        _mm(h2f, _flat(dp1), ta=True, name="gw_in1"), _mm(h2f, _flat(du), ta=True, name="gw_inu"),
        _mm(h2f, _flat(dp3), ta=True, name="gw_in3")], axis=1)

    st_mix = stack_group(groups[1])
    h_mix = _exchange_begin(st_mix, _own_slab(st_mix, chip), True, "scatter_mix_start")

    grad_x, (dsh1, dsc1, dgt1, dg1), (gfull["w1_a"], gfull["w3_a"], gfull["w2_a"]) = _ffn_backward(
        dx1, ffn1_saved, g_ffn1, sc1, gt1 + h_mix[4][0, 0], wfull["w1_a"], wfull["w3_a"], wfull["w2_a"], "ffn1")

    sbw = GP // SSM_SUPER
    d_are, d_aim, d_ldt, d_bre_t, d_bim_t = _ssm_disc_bwd(
        are, aim, ldt, bre_t, bim_t, jnp.sum(dar8, axis=0, keepdims=True), jnp.sum(dai8, axis=0, keepdims=True),
        _blockdiag_extract(dwb[:, :sbw]), _blockdiag_extract(dwb[:, sbw:]))
    d_cre = _c_from_t(_blockdiag_extract(dwct[:, :sbw]))
    d_cim = -_c_from_t(_blockdiag_extract(dwct[:, sbw:]))

    small_parts = [dg1, dgmix, dg3, dgfin, dconvw8[:3], d_are, d_aim, _from_t(d_bre_t), _from_t(d_bim_t), d_cre, d_cim,
                   jnp.sum(d_ldt.reshape(NG, NP), axis=1), ddskip]
    small_sizes = [int(p.size) for p in small_parts]
    n_small = sum(small_sizes)
    n_small_pad = -(-n_small // (SUB * PACK_COLS)) * (SUB * PACK_COLS)
    dmod = jnp.concatenate([dsh1, dsc1, dgt1, dsh2, dsc2, dgt2, dsh3, dsc3, dgt3], axis=2).reshape(bsz * NMOD * D)
    flat = jnp.concatenate([p.reshape(-1) for p in small_parts] + [jnp.zeros((n_small_pad - n_small,), F32), dmod])
    allg = _all_gather8(flat.reshape(SUB, -1), "gather_small").reshape(N_DEV, -1)
    small = _sum_slabs(allg[:, :n_small_pad].reshape(N_DEV, -1, PACK_COLS), "sum_small").reshape(-1)
    sg, o = [], 0
    for p, sz in zip(small_parts, small_sizes):
        sg.append(small[o:o + sz].reshape(p.shape))
        o += sz
    (g_g1, g_gmix, g_g3, g_gfin, g_convw, g_are, g_aim, g_bre, g_bim, g_cre, g_cim, g_ldt, g_dskip) = sg

    dmod_all = allg[:, n_small_pad:].reshape(nb, NMOD * D)
    dmod_shard = _select(dmod_all.reshape(nb, N_CHIPS, nmod_shard).transpose(1, 0, 2), chip)
    g_wada, g_bada = _ada_bwd(c_all, dmod_shard, dmod_all)

    recv_ffn1 = _scatter_chips(stack_group(groups[0]), "scatter_ffn1")
    recv_mix = _exchange_end(h_mix, recv_ffn1, True, "scatter_mix_wait")
    recv_ffn2 = _exchange_end(h_ffn2, recv_ffn1, True, "scatter_ffn2_wait")
    p_own = _sum_groups([recv_ffn1, recv_mix, recv_ffn2], "sum_chips")
    p_sib = _swap_sibling(p_own, "swap_sibling")

    grads, deltas, new_m, new_v = {}, {}, {}, {}
    for n, rows, cols, _ in BIG:
        shp = args[n].shape
        w2d, m2d, v2d = [args[k + n][0].reshape(-1, PACK_COLS) for k in ("", "m_", "v_")]
        res = _adamw_big(w2d, m2d, v2d, p_own, p_sib, offs[n], f"adamw_{n}")
        grads[n], deltas[n], new_m[n], new_v[n] = [r.reshape(shp) for r in res]
    res = _adamw_rows(w_ada[0], m_w_ada[0], v_w_ada[0], g_wada, "adamw_w_ada")
    grads["w_ada"] = g_wada[None]
    deltas["w_ada"], new_m["w_ada"], new_v["w_ada"] = [r[None] for r in res]

    g_convw_shard = _select(g_convw.reshape(3, N_CHIPS, CW // N_CHIPS).transpose(1, 0, 2), chip)
    small_g = {"b_ada": g_bada, "g_ffn1": g_g1, "g_mix": g_gmix, "g_ffn2": g_g3, "g_final": g_gfin,
               "conv_w": g_convw_shard, "a_re": g_are, "a_im": g_aim, "b_re": g_bre, "b_im": g_bim,
               "c_re": g_cre, "c_im": g_cim, "log_dt": g_ldt, "d_skip": g_dskip}
    small_names = list(small_g)
    sizes = [int(args[n].size) for n in small_names]
    tot = sum(sizes)
    tot_pad = -(-tot // (SUB * PACK_COLS)) * (SUB * PACK_COLS)

    def pack(get):
        return jnp.concatenate([get(n).reshape(-1) for n in small_names] + [jnp.zeros((tot_pad - tot,), F32)]).reshape(
            -1, PACK_COLS)

    res = _adamw_plain(pack(lambda n: args[n]), pack(lambda n: args["m_" + n]), pack(lambda n: args["v_" + n]),
                       pack(lambda n: small_g[n]), "adamw_small")
    o = 0
    for n, sz in zip(small_names, sizes):
        shp = args[n].shape
        grads[n] = small_g[n].reshape(shp)
        deltas[n], new_m[n], new_v[n] = [r.reshape(-1)[o:o + sz].reshape(shp) for r in res]
        o += sz

    return (loss, grad_x, *[grads[n] for n in names], *[deltas[n] for n in names],
            *[new_m[n] for n in names], *[new_v[n] for n in names])
```

```python
import functools
import math

import jax
import jax.numpy as jnp
from jax import lax
from jax.experimental import pallas as pl
from jax.experimental.pallas import tpu as pltpu

F32 = jnp.float32
BF16 = jnp.bfloat16
SDS = jax.ShapeDtypeStruct
MESH = pl.DeviceIdType.MESH

D = 1024
DFF = 2816
CW = 1024
SW = 512
NG, NP, NH = 32, 64, 16
GP = NG * NP
NMOD = 9
EPS = 1e-6
N_CHIPS = 4
N_DEV = 8
SUB = 8
LANE = 128
SSM_SUPER = 4
VMEM_LIMIT = 50 * 1024 * 1024

LR, B1, B2, AEPS, WD, STEP = 0.001, 0.9, 0.999, 1e-08, 0.01, 10
BC1 = 1.0 - B1 ** STEP
BC2 = 1.0 - B2 ** STEP


def _cp(*sem):
    return pltpu.CompilerParams(dimension_semantics=sem or None, vmem_limit_bytes=VMEM_LIMIT)


def _pick_tile(n, cands):
    for t in cands:
        if t <= n and n % t == 0:
            return t
    return n


def _dot(a, b):
    return lax.dot_general(a, b, (((1,), (0,)), ((), ())), preferred_element_type=F32)


def _dot_nt(a, b):
    return lax.dot_general(a, b, (((1,), (1,)), ((), ())), preferred_element_type=F32)


def _dot_tn(a, b):
    return lax.dot_general(a, b, (((0,), (0,)), ((), ())), preferred_element_type=F32)


def _row(tm, width, col=0):
    return pl.BlockSpec((1, tm, width), lambda b, i, *_: (b, i, col))


def _seqvec(width):
    return pl.BlockSpec((1, 1, width), lambda b, *_: (b, 0, 0))


def _full2(shape):
    return pl.BlockSpec(shape, lambda *_: (0, 0))


def _sigmoid(x):
    return jax.nn.sigmoid(x)


def _mm(a, b, *, ta=False, tb=False, out_dtype=F32, name):
    if ta:
        kdim, m = a.shape
    else:
        m, kdim = a.shape
    n = b.shape[0] if tb else b.shape[1]
    tm = _pick_tile(m, (1408, 1024, 512, 256, 128))
    tn = _pick_tile(n, (1408, 1024, 512, 256, 128))
    tk = _pick_tile(kdim, (512, 256, 128))
    nk = kdim // tk

    def body(a_ref, b_ref, o_ref, acc_ref):
        k = pl.program_id(2)

        @pl.when(k == 0)
        def _():
            acc_ref[...] = jnp.zeros_like(acc_ref)

        av = a_ref[...].astype(BF16)
        bv = b_ref[...].astype(BF16)
        dn = (((0 if ta else 1,), (1 if tb else 0,)), ((), ()))
        acc_ref[...] += lax.dot_general(av, bv, dn, preferred_element_type=F32)

        @pl.when(k == nk - 1)
        def _():
            o_ref[...] = acc_ref[...].astype(out_dtype)

    a_spec = pl.BlockSpec((tk, tm), lambda i, j, k: (k, i)) if ta else pl.BlockSpec((tm, tk), lambda i, j, k: (i, k))
    b_spec = pl.BlockSpec((tn, tk), lambda i, j, k: (j, k)) if tb else pl.BlockSpec((tk, tn), lambda i, j, k: (k, j))
    return pl.pallas_call(
        body, name=name, grid=(m // tm, n // tn, nk),
        in_specs=[a_spec, b_spec],
        out_specs=pl.BlockSpec((tm, tn), lambda i, j, k: (i, j)),
        out_shape=SDS((m, n), out_dtype),
        scratch_shapes=[pltpu.VMEM((tm, tn), F32)],
        compiler_params=_cp("parallel", "parallel", "arbitrary"),
    )(a, b)


def _flat(a):
    return a.reshape(-1, a.shape[-1])


def _norm_mod(x, g, sh, sc, name):
    bsz, seq, dm = x.shape
    tm = _pick_tile(seq, (512, 256, 128))

    def body(x_ref, g_ref, sh_ref, sc_ref, o_ref):
        xf = x_ref[0]
        r = lax.rsqrt(jnp.mean(xf * xf, axis=-1, keepdims=True) + EPS)
        hn = xf * r * g_ref[...]
        o_ref[0] = (hn * (1.0 + sc_ref[0]) + sh_ref[0]).astype(BF16)

    return pl.pallas_call(
        body, name=name, grid=(bsz, seq // tm),
        in_specs=[_row(tm, dm), _full2((1, dm)), _seqvec(dm), _seqvec(dm)],
        out_specs=_row(tm, dm), out_shape=SDS((bsz, seq, dm), BF16),
        compiler_params=_cp("parallel", "parallel"),
    )(x, g, sh, sc)


def _swiglu_up(h, w1, w3, name):
    bsz, seq, dm = h.shape
    nf = w1.shape[1]
    tm = _pick_tile(seq, (512, 256, 128))
    tn = _pick_tile(nf, (1408, 512, 256, 128))

    def body(h_ref, w1_ref, w3_ref, a_ref, b_ref, hid_ref):
        hv = h_ref[0]
        a = _dot(hv, w1_ref[...])
        b = _dot(hv, w3_ref[...])
        sg = _sigmoid(a)
        sa = a * sg
        a_ref[0] = (b * (sg * (1.0 + a * (1.0 - sg)))).astype(BF16)
        b_ref[0] = sa.astype(BF16)
        hid_ref[0] = (sa * b).astype(BF16)

    wspec = pl.BlockSpec((dm, tn), lambda n, b, i: (0, n))
    ospec = pl.BlockSpec((1, tm, tn), lambda n, b, i: (b, i, n))
    shp = SDS((bsz, seq, nf), BF16)
    return pl.pallas_call(
        body, name=name, grid=(nf // tn, bsz, seq // tm),
        in_specs=[pl.BlockSpec((1, tm, dm), lambda n, b, i: (b, i, 0)), wspec, wspec],
        out_specs=[ospec, ospec, ospec], out_shape=[shp, shp, shp],
        compiler_params=_cp("parallel", "parallel", "parallel"),
    )(h, w1, w3)


def _ffn_down(hid, w2, x, gt, name):
    bsz, seq, nf = hid.shape
    dm = w2.shape[1]
    tm = _pick_tile(seq, (512, 256, 128))
    tk = _pick_tile(nf, (1408, 512, 256, 128))
    nk = nf // tk

    def body(hid_ref, w2_ref, x_ref, gt_ref, f_ref, xo_ref, acc_ref):
        k = pl.program_id(2)

        @pl.when(k == 0)
        def _():
            acc_ref[...] = jnp.zeros_like(acc_ref)

        acc_ref[...] += _dot(hid_ref[0], w2_ref[...])

        @pl.when(k == nk - 1)
        def _():
            f = acc_ref[...]
            f_ref[0] = f
            xo_ref[0] = x_ref[0] + 0.5 * gt_ref[0] * f

    shp = SDS((bsz, seq, dm), F32)
    return pl.pallas_call(
        body, name=name, grid=(bsz, seq // tm, nk),
        in_specs=[pl.BlockSpec((1, tm, tk), lambda b, i, k: (b, i, k)),
                  pl.BlockSpec((tk, dm), lambda b, i, k: (k, 0)), _row(tm, dm), _seqvec(dm)],
        out_specs=[_row(tm, dm), _row(tm, dm)], out_shape=[shp, shp],
        scratch_shapes=[pltpu.VMEM((tm, dm), F32)],
        compiler_params=_cp("parallel", "parallel", "arbitrary"),
    )(hid, w2, x, gt)


def _ffn_bwd_hid(dxo, gt, f, a, b, w2, name):
    bsz, seq, dm = dxo.shape
    nf = a.shape[2]
    tm = _pick_tile(seq, (512, 256, 128))
    tn = _pick_tile(nf, (1408, 512, 256, 128))

    def body(dxo_ref, gt_ref, f_ref, a_ref, b_ref, w2_ref, dfs_ref, da_ref, db_ref, dgt_ref):
        i = pl.program_id(1)
        n = pl.program_id(2)

        @pl.when(n == 0)
        def _():
            dxo = dxo_ref[0]
            dfs_ref[0] = (0.5 * gt_ref[0] * dxo).astype(BF16)
            part = jnp.sum(0.5 * dxo * f_ref[0], axis=0, keepdims=True)

            @pl.when(i == 0)
            def _():
                dgt_ref[0] = part

            @pl.when(i > 0)
            def _():
                dgt_ref[0] += part

        dhid = _dot_nt(dfs_ref[0], w2_ref[...])
        da_ref[0] = (dhid * a_ref[0].astype(F32)).astype(BF16)
        db_ref[0] = (dhid * b_ref[0].astype(F32)).astype(BF16)

    hspec = pl.BlockSpec((1, tm, tn), lambda b, i, n: (b, i, n))
    return pl.pallas_call(
        body, name=name, grid=(bsz, seq // tm, nf // tn),
        in_specs=[_row(tm, dm), _seqvec(dm), _row(tm, dm), hspec, hspec,
                  pl.BlockSpec((tn, dm), lambda b, i, n: (n, 0))],
        out_specs=[_row(tm, dm), hspec, hspec, _seqvec(dm)],
        out_shape=[SDS((bsz, seq, dm), BF16), SDS((bsz, seq, nf), BF16), SDS((bsz, seq, nf), BF16),
                   SDS((bsz, 1, dm), F32)],
        compiler_params=_cp("arbitrary", "arbitrary", "arbitrary"),
    )(dxo, gt, f, a, b, w2)


def _dh_norm_bwd(pieces, weights, tk, x, g, sc, dxo, name):
    bsz, seq, dm = x.shape
    tm = _pick_tile(seq, (512, 256, 128))
    nblk = [p.shape[2] // tk for p in pieces]
    offs = [sum(nblk[:j]) for j in range(len(pieces))]
    nk = sum(nblk)
    npc = len(pieces)

    def body(*refs):
        p_refs = refs[:npc]
        w_refs = refs[npc:2 * npc]
        x_ref, g_ref, sc_ref, dxo_ref, dx_ref, dsh_ref, dsc_ref, dg_ref, acc_ref = refs[2 * npc:]
        b = pl.program_id(0)
        i = pl.program_id(1)
        k = pl.program_id(2)

        @pl.when(k == 0)
        def _():
            acc_ref[...] = jnp.zeros_like(acc_ref)

        for j in range(npc):
            @pl.when((k >= offs[j]) & (k < offs[j] + nblk[j]))
            def _(j=j):
                acc_ref[...] += _dot_nt(p_refs[j][0], w_refs[j][...])

        @pl.when(k == nk - 1)
        def _():
            dh = acc_ref[...]
            xf = x_ref[0]
            gv = g_ref[...]
            r = lax.rsqrt(jnp.mean(xf * xf, axis=-1, keepdims=True) + EPS)
            xhat = xf * r
            dhn = dh * (1.0 + sc_ref[0])
            p_sh = jnp.sum(dh, axis=0, keepdims=True)
            p_sc = jnp.sum(dh * (xhat * gv), axis=0, keepdims=True)
            p_g = jnp.sum(dhn * xhat, axis=0, keepdims=True)
            dxh = dhn * gv
            dx_ref[0] = dxo_ref[0] + r * (dxh - xhat * jnp.mean(dxh * xhat, axis=-1, keepdims=True))

            @pl.when(i == 0)
            def _():
                dsh_ref[0] = p_sh
                dsc_ref[0] = p_sc

            @pl.when(i > 0)
            def _():
                dsh_ref[0] += p_sh
                dsc_ref[0] += p_sc

            @pl.when((i == 0) & (b == 0))
            def _():
                dg_ref[...] = p_g

            @pl.when((i > 0) | (b > 0))
            def _():
                dg_ref[...] += p_g

    def pspec(j):
        return pl.BlockSpec((1, tm, tk), lambda b, i, k: (b, i, jnp.clip(k - offs[j], 0, nblk[j] - 1)))

    def wspec(j):
        return pl.BlockSpec((dm, tk), lambda b, i, k: (0, jnp.clip(k - offs[j], 0, nblk[j] - 1)))

    return pl.pallas_call(
        body, name=name, grid=(bsz, seq // tm, nk),
        in_specs=[pspec(j) for j in range(npc)] + [wspec(j) for j in range(npc)] + [
            _row(tm, dm), _full2((1, dm)), _seqvec(dm), _row(tm, dm)],
        out_specs=[_row(tm, dm), _seqvec(dm), _seqvec(dm), _full2((1, dm))],
        out_shape=[SDS((bsz, seq, dm), F32), SDS((bsz, 1, dm), F32), SDS((bsz, 1, dm), F32), SDS((1, dm), F32)],
        scratch_shapes=[pltpu.VMEM((tm, dm), F32)],
        compiler_params=_cp("arbitrary", "arbitrary", "arbitrary"),
    )(*pieces, *weights, x, g, sc, dxo)


HALO = 16


def _conv_core(gc, v, gch, vh, w, first):
    cv = gc * v
    halo = jnp.where(first, 0.0, gch * vh)
    ext = jnp.concatenate([halo, cv], axis=0)
    cv1 = pltpu.roll(ext, 1, 0)[HALO:]
    cv2 = pltpu.roll(ext, 2, 0)[HALO:]
    conv = w[0:1] * cv2 + w[1:2] * cv1 + w[2:3] * cv
    return cv, cv1, cv2, conv


def _prev_halo(tm, col):
    return pl.BlockSpec((1, HALO, CW), lambda b, i, *_: (b, jnp.maximum(i * (tm // HALO) - 1, 0), col))


def _next_halo(tm, seq, col):
    return pl.BlockSpec((1, HALO, CW), lambda b, i, *_: (b, jnp.minimum((i + 1) * (tm // HALO), seq // HALO - 1), col))


def _conv_fwd(p1, convw8, name):
    bsz, seq, _ = p1.shape
    tm = _pick_tile(seq, (512, 256, 128))

    def body(gb_ref, gc_ref, v_ref, gch_ref, vh_ref, w_ref, o_ref):
        first = pl.program_id(1) == 0
        _, _, _, conv = _conv_core(gc_ref[0], v_ref[0], gch_ref[0], vh_ref[0], w_ref[...], first)
        o_ref[0] = (gb_ref[0] * conv).astype(BF16)

    return pl.pallas_call(
        body, name=name, grid=(bsz, seq // tm),
        in_specs=[_row(tm, CW, 0), _row(tm, CW, 1), _row(tm, CW, 2), _prev_halo(tm, 1), _prev_halo(tm, 2),
                  _full2((8, CW))],
        out_specs=_row(tm, CW), out_shape=SDS((bsz, seq, CW), BF16),
        compiler_params=_cp("parallel", "parallel"),
    )(p1, p1, p1, p1, p1, convw8)


def _conv_bwd(dya, wco, p1, convw8, name):
    bsz, seq, _ = p1.shape
    tm = _pick_tile(seq, (512, 256, 128))
    nt = seq // tm
    ext_rows = tm + HALO

    def body(dya_ref, dyan_ref, wco_ref, gb_ref, gbn_ref, gc_ref, v_ref, gch_ref, vh_ref, w_ref, dp_ref, dw_ref):
        b = pl.program_id(0)
        i = pl.program_id(1)
        w = w_ref[...]
        cv, cv1, cv2, conv = _conv_core(gc_ref[0], v_ref[0], gch_ref[0], vh_ref[0], w, i == 0)
        dya_ext = jnp.concatenate([dya_ref[0], dyan_ref[0]], axis=0)
        dyain_ext = _dot_nt(dya_ext, wco_ref[...])
        gb = gb_ref[0]
        gb_ext = jnp.concatenate([gb, gbn_ref[0]], axis=0)
        rows = lax.broadcasted_iota(jnp.int32, (ext_rows, 1), 0)
        dconv_ext = jnp.where((rows < tm) | (i < nt - 1), dyain_ext * gb_ext, 0.0)
        dconv = dconv_ext[:tm]
        dconv1 = pltpu.roll(dconv_ext, ext_rows - 1, 0)[:tm]
        dconv2 = pltpu.roll(dconv_ext, ext_rows - 2, 0)[:tm]
        dcv = w[2:3] * dconv + w[1:2] * dconv1 + w[0:1] * dconv2
        dp_ref[0, :, 0:CW] = (dyain_ext[:tm] * conv).astype(BF16)
        dp_ref[0, :, CW:2 * CW] = (dcv * v_ref[0]).astype(BF16)
        dp_ref[0, :, 2 * CW:3 * CW] = (dcv * gc_ref[0]).astype(BF16)
        g0 = jnp.sum(dconv * cv2, axis=0, keepdims=True)
        g1 = jnp.sum(dconv * cv1, axis=0, keepdims=True)
        g2 = jnp.sum(dconv * cv, axis=0, keepdims=True)
        upd = jnp.concatenate([g0, g1, g2, jnp.zeros((5, CW), F32)], axis=0)

        @pl.when((i == 0) & (b == 0))
        def _():
            dw_ref[...] = upd

        @pl.when((i > 0) | (b > 0))
        def _():
            dw_ref[...] += upd

    return pl.pallas_call(
        body, name=name, grid=(bsz, seq // tm),
        in_specs=[_row(tm, CW), _next_halo(tm, seq, 0), _full2((CW, D)),
                  _row(tm, CW, 0), _next_halo(tm, seq, 0), _row(tm, CW, 1), _row(tm, CW, 2),
                  _prev_halo(tm, 1), _prev_halo(tm, 2), _full2((8, CW))],
        out_specs=[_row(tm, 3 * CW), _full2((8, CW))],
        out_shape=[SDS((bsz, seq, 3 * CW), BF16), SDS((8, CW), F32)],
        compiler_params=_cp("arbitrary", "arbitrary"),
    )(dya, dya, wco, p1, p1, p1, p1, p1, p1, convw8)


def _disc(are, aim, ldt, bre, bim):
    dt = jnp.exp(ldt)
    mag = jnp.exp(are * dt)
    ang = aim * dt
    abr = mag * jnp.cos(ang)
    abi = mag * jnp.sin(ang)
    nr = abr - 1.0
    den = are * are + aim * aim
    cr = (nr * are + abi * aim) / den
    ci = (abi * are - nr * aim) / den
    return abr, abi, cr * bre - ci * bim, cr * bim + ci * bre


def _ssm_disc(are, aim, ldt, bre_t, bim_t):
    def body(are_ref, aim_ref, ldt_ref, bre_ref, bim_ref, abr_ref, abi_ref, bbr_ref, bbi_ref):
        abr, abi, bbr, bbi = _disc(are_ref[...], aim_ref[...], ldt_ref[...], bre_ref[...], bim_ref[...])
        abr_ref[...] = abr
        abi_ref[...] = abi
        bbr_ref[...] = bbr
        bbi_ref[...] = bbi

    v1, vh = SDS((1, GP), F32), SDS((NH, GP), F32)
    return pl.pallas_call(body, name="ssm_disc", out_shape=[v1, v1, vh, vh], compiler_params=_cp())(
        are, aim, ldt, bre_t, bim_t)


def _ssm_disc_bwd(are, aim, ldt, bre_t, bim_t, dabr, dabi, dbbr, dbbi):
    def body(are_ref, aim_ref, ldt_ref, bre_ref, bim_ref, g0, g1, g2, g3, o0, o1, o2, o3, o4):
        prim = (are_ref[...], aim_ref[...], ldt_ref[...], bre_ref[...], bim_ref[...])
        _, vjp = jax.vjp(_disc, *prim)
        d_are, d_aim, d_ldt, d_bre, d_bim = vjp((g0[...], g1[...], g2[...], g3[...]))
        o0[...] = d_are
        o1[...] = d_aim
        o2[...] = d_ldt
        o3[...] = d_bre
        o4[...] = d_bim

    v1, vh = SDS((1, GP), F32), SDS((NH, GP), F32)
    return pl.pallas_call(body, name="ssm_disc_bwd", out_shape=[v1, v1, v1, vh, vh], compiler_params=_cp())(
        are, aim, ldt, bre_t, bim_t, dabr, dabi, dbbr, dbbi)


def _scan_chunk(buf_ref, nt, bsz, ar, ai, init_r, init_i, reverse):
    nsub = SUB // bsz
    row = lax.broadcasted_iota(jnp.int32, (SUB, GP), 0)
    shift = ((SUB - bsz) if reverse else bsz) % SUB
    order = list(range(nsub - 1, -1, -1)) if reverse else list(range(nsub))

    def step(j, carry):
        pr, pi = carry
        jj = (nt - 1 - j) if reverse else j
        off = pl.multiple_of(jj * SUB, SUB)
        br = buf_ref[pl.ds(off, SUB), 0:GP]
        bi = buf_ref[pl.ds(off, SUB), GP:2 * GP]
        nr, ni = pr, pi
        for s in order:
            sr, si = nr, ni
            if shift:
                sr = pltpu.roll(sr, shift, 0)
                si = pltpu.roll(si, shift, 0)
            cr = ar * sr - ai * si + br
            ci = ar * si + ai * sr + bi
            if nsub == 1:
                nr, ni = cr, ci
            else:
                m = (row >= s * bsz) & (row < (s + 1) * bsz)
                nr = jnp.where(m, cr, nr)
                ni = jnp.where(m, ci, ni)
        buf_ref[pl.ds(off, SUB), 0:GP] = nr
        buf_ref[pl.ds(off, SUB), GP:2 * GP] = ni
        return nr, ni

    return lax.fori_loop(0, nt, step, (init_r, init_i))


def _ssm_chunk_rows(total_rows, bsz):
    return min(total_rows, 64 * bsz)


def _ssm_fwd(u_tm, wb, wct, ar8, ai8, bsz, name):
    rt = u_tm.shape[0]
    r = _ssm_chunk_rows(rt, bsz)
    nt = r // SUB

    def body(u_ref, wb_hbm, wct_hbm, ar_ref, ai_ref, x_ref, y_ref, wb_ref, wct_ref, st_ref):
        @pl.when(pl.program_id(0) == 0)
        def _():
            pltpu.sync_copy(wb_hbm, wb_ref)
            pltpu.sync_copy(wct_hbm, wct_ref)
            st_ref[...] = jnp.zeros_like(st_ref)

        x_ref[...] = _dot(u_ref[...].astype(BF16), wb_ref[...])
        fr, fi = _scan_chunk(x_ref, nt, bsz, ar_ref[...], ai_ref[...], st_ref[:, 0:GP], st_ref[:, GP:2 * GP], False)
        st_ref[:, 0:GP] = fr
        st_ref[:, GP:2 * GP] = fi
        y_ref[...] = _dot_nt(x_ref[...].astype(BF16), wct_ref[...])

    anyspec = pl.BlockSpec(memory_space=pl.ANY)
    return pl.pallas_call(
        body, name=name, grid=(rt // r,),
        in_specs=[pl.BlockSpec((r, SW), lambda i: (i, 0)), anyspec, anyspec, _full2((SUB, GP)), _full2((SUB, GP))],
        out_specs=[pl.BlockSpec((r, 2 * GP), lambda i: (i, 0)), pl.BlockSpec((r, SW), lambda i: (i, 0))],
        out_shape=[SDS((rt, 2 * GP), F32), SDS((rt, SW), F32)],
        scratch_shapes=[pltpu.VMEM((SW, 2 * GP), BF16), pltpu.VMEM((SW, 2 * GP), BF16), pltpu.VMEM((SUB, 2 * GP), F32)],
        compiler_params=_cp("arbitrary"),
    )(u_tm, wb, wct, ar8, ai8)


def _ssm_bwd(dy_tm, u_tm, xs, wb, wct, ar8, ai8, bsz, name):
    rt = u_tm.shape[0]
    r = _ssm_chunk_rows(rt, bsz)
    nt = r // SUB
    nc = rt // r
    ub = SW // SSM_SUPER
    sb = GP // SSM_SUPER

    def body(dy_ref, u_ref, x_ref, xh_ref, wb_hbm, wct_hbm, ar_ref, ai_ref,
             du_ref, dwb_hbm, dwct_hbm, dar_ref, dai_ref, wb_ref, wct_ref, g_ref, st_ref, awb_ref, awct_ref):
        i = pl.program_id(0)

        @pl.when(i == 0)
        def _():
            pltpu.sync_copy(wb_hbm, wb_ref)
            pltpu.sync_copy(wct_hbm, wct_ref)
            st_ref[...] = jnp.zeros_like(st_ref)
            awb_ref[...] = jnp.zeros_like(awb_ref)
            awct_ref[...] = jnp.zeros_like(awct_ref)
            dar_ref[...] = jnp.zeros_like(dar_ref)
            dai_ref[...] = jnp.zeros_like(dai_ref)

        dyb = dy_ref[...].astype(BF16)
        g_ref[...] = _dot(dyb, wct_ref[...])
        ar = ar_ref[...]
        ai = ai_ref[...]
        fr, fi = _scan_chunk(g_ref, nt, bsz, ar, -ai, st_ref[:, 0:GP], st_ref[:, GP:2 * GP], True)
        st_ref[:, 0:GP] = fr
        st_ref[:, GP:2 * GP] = fi

        gb = g_ref[...].astype(BF16)
        du_ref[...] = _dot_nt(gb, wb_ref[...])
        ub16 = u_ref[...].astype(BF16)
        xb16 = x_ref[...].astype(BF16)
        for s in range(SSM_SUPER):
            us = ub16[:, s * ub:(s + 1) * ub]
            ds = dyb[:, s * ub:(s + 1) * ub]
            for half in range(2):
                cols = slice(half * GP + s * sb, half * GP + (s + 1) * sb)
                ocols = slice(half * sb, (half + 1) * sb)
                awb_ref[s * ub:(s + 1) * ub, ocols] += _dot_tn(us, gb[:, cols])
                awct_ref[s * ub:(s + 1) * ub, ocols] += _dot_tn(ds, xb16[:, cols])

        gr = g_ref[:, 0:GP]
        gi = g_ref[:, GP:2 * GP]
        xsr = pltpu.roll(x_ref[:, 0:GP], bsz, 0)
        xsi = pltpu.roll(x_ref[:, GP:2 * GP], bsz, 0)
        inner = lax.broadcasted_iota(jnp.int32, (r, 1), 0) >= bsz
        t_r = jnp.where(inner, gr * xsr + gi * xsi, 0.0)
        t_i = jnp.where(inner, gi * xsr - gr * xsi, 0.0)
        acc_r = jnp.sum(t_r.reshape(nt, SUB, GP), axis=0)
        acc_i = jnp.sum(t_i.reshape(nt, SUB, GP), axis=0)
        hr = xh_ref[:, 0:GP]
        hi = xh_ref[:, GP:2 * GP]
        if bsz % SUB:
            hr = pltpu.roll(hr, bsz, 0)
            hi = pltpu.roll(hi, bsz, 0)
        edge = (lax.broadcasted_iota(jnp.int32, (SUB, 1), 0) < bsz) & (i < nc - 1)
        g0r = g_ref[0:SUB, 0:GP]
        g0i = g_ref[0:SUB, GP:2 * GP]
        dar_ref[...] += acc_r + jnp.where(edge, g0r * hr + g0i * hi, 0.0)
        dai_ref[...] += acc_i + jnp.where(edge, g0i * hr - g0r * hi, 0.0)

        @pl.when(i == nc - 1)
        def _():
            pltpu.sync_copy(awb_ref, dwb_hbm)
            pltpu.sync_copy(awct_ref, dwct_hbm)

    anyspec = pl.BlockSpec(memory_space=pl.ANY)
    rev = lambda i: (nc - 1 - i, 0)
    wshape = (SW, 2 * sb)
    return pl.pallas_call(
        body, name=name, grid=(nc,),
        in_specs=[pl.BlockSpec((r, SW), rev), pl.BlockSpec((r, SW), rev), pl.BlockSpec((r, 2 * GP), rev),
                  pl.BlockSpec((SUB, 2 * GP), lambda i: (jnp.maximum((nc - 1 - i) * nt - 1, 0), 0)),
                  anyspec, anyspec, _full2((SUB, GP)), _full2((SUB, GP))],
        out_specs=[pl.BlockSpec((r, SW), rev), anyspec, anyspec, _full2((SUB, GP)), _full2((SUB, GP))],
        out_shape=[SDS((rt, SW), F32), SDS(wshape, F32), SDS(wshape, F32), SDS((SUB, GP), F32), SDS((SUB, GP), F32)],
        scratch_shapes=[pltpu.VMEM((SW, 2 * GP), BF16), pltpu.VMEM((SW, 2 * GP), BF16),
                        pltpu.VMEM((r, 2 * GP), F32), pltpu.VMEM((SUB, 2 * GP), F32),
                        pltpu.VMEM(wshape, F32), pltpu.VMEM(wshape, F32)],
        compiler_params=_cp("arbitrary"),
    )(dy_tm, u_tm, xs, xs, wb, wct, ar8, ai8)


GELU_C = math.sqrt(2.0 / math.pi)


def _gelu(x):
    return 0.5 * x * (1.0 + jnp.tanh(GELU_C * (x + 0.044715 * x * x * x)))


def _gelu_grad(x):
    th = jnp.tanh(GELU_C * (x + 0.044715 * x * x * x))
    return 0.5 * (1.0 + th) + 0.5 * x * (1.0 - th * th) * GELU_C * (1.0 + 3.0 * 0.044715 * x * x)


def _ssm_post(ys, u, dskip, wglu, wso, name):
    bsz, seq, _ = ys.shape
    tm = _pick_tile(seq, (512, 256, 128))

    def body(ys_ref, u_ref, d_ref, wg_ref, wo_ref, s0_ref, z_ref, s1_ref, s2_ref, yb_ref):
        s0 = ys_ref[0] + d_ref[...] * u_ref[0]
        s1 = _gelu(s0)
        s1b = s1.astype(BF16)
        z = _dot(s1b, wg_ref[...])
        s2b = (s1 * _sigmoid(z)).astype(BF16)
        s0_ref[0] = s0
        z_ref[0] = z
        s1_ref[0] = s1b
        s2_ref[0] = s2b
        yb_ref[0] = _dot(s2b, wo_ref[...])

    return pl.pallas_call(
        body, name=name, grid=(bsz, seq // tm),
        in_specs=[_row(tm, SW), _row(tm, SW), _full2((1, SW)), _full2((SW, SW)), _full2((SW, D))],
        out_specs=[_row(tm, SW), _row(tm, SW), _row(tm, SW), _row(tm, SW), _row(tm, D)],
        out_shape=[SDS((bsz, seq, SW), F32), SDS((bsz, seq, SW), F32), SDS((bsz, seq, SW), BF16),
                   SDS((bsz, seq, SW), BF16), SDS((bsz, seq, D), F32)],
        compiler_params=_cp("parallel", "parallel"),
    )(ys, u, dskip, wglu, wso)


def _ssm_post_bwd(dyb, s0, z, u, dskip, wglu, wso, name):
    bsz, seq, _ = s0.shape
    tm = _pick_tile(seq, (512, 256, 128))

    def body(dyb_ref, s0_ref, z_ref, u_ref, wg_ref, wo_ref, ds0_ref, dz_ref, dd_ref):
        b = pl.program_id(0)
        i = pl.program_id(1)
        ds2 = _dot_nt(dyb_ref[0], wo_ref[...])
        s0 = s0_ref[0]
        s1 = _gelu(s0)
        sg = _sigmoid(z_ref[0])
        dz = ds2 * s1 * sg * (1.0 - sg)
        dzb = dz.astype(BF16)
        ds1 = ds2 * sg + _dot_nt(dzb, wg_ref[...])
        ds0 = ds1 * _gelu_grad(s0)
        ds0_ref[0] = ds0
        dz_ref[0] = dzb
        part = jnp.sum(ds0 * u_ref[0], axis=0, keepdims=True)

        @pl.when((i == 0) & (b == 0))
        def _():
            dd_ref[...] = part

        @pl.when((i > 0) | (b > 0))
        def _():
            dd_ref[...] += part

    del dskip
    return pl.pallas_call(
        body, name=name, grid=(bsz, seq // tm),
        in_specs=[_row(tm, D), _row(tm, SW), _row(tm, SW), _row(tm, SW), _full2((SW, SW)), _full2((SW, D))],
        out_specs=[_row(tm, SW), _row(tm, SW), _full2((1, SW))],
        out_shape=[SDS((bsz, seq, SW), F32), SDS((bsz, seq, SW), BF16), SDS((1, SW), F32)],
        compiler_params=_cp("arbitrary", "arbitrary"),
    )(dyb, s0, z, u, wglu, wso)


def _du_combine(du_ssm, ds0, dskip, name):
    bsz, seq, _ = ds0.shape
    tm = _pick_tile(seq, (512, 256, 128))

    def body(a_ref, b_ref, d_ref, o_ref):
        o_ref[0] = (a_ref[0] + b_ref[0] * d_ref[...]).astype(BF16)

    return pl.pallas_call(
        body, name=name, grid=(bsz, seq // tm),
        in_specs=[_row(tm, SW), _row(tm, SW), _full2((1, SW))],
        out_specs=_row(tm, SW), out_shape=SDS((bsz, seq, SW), BF16),
        compiler_params=_cp("parallel", "parallel"),
    )(du_ssm, ds0, dskip)


def _merge_out(ya, yb, p3, wout, x1, gt, name):
    bsz, seq, _ = ya.shape
    tm = _pick_tile(seq, (512, 256, 128))

    def body(ya_ref, yb_ref, ga_ref, gbb_ref, w_ref, x_ref, gt_ref, mg_ref, mix_ref, xo_ref):
        merged = (_sigmoid(ga_ref[0]) * ya_ref[0] + _sigmoid(gbb_ref[0]) * yb_ref[0]).astype(BF16)
        mix = _dot(merged, w_ref[...])
        mg_ref[0] = merged
        mix_ref[0] = mix
        xo_ref[0] = x_ref[0] + gt_ref[0] * mix

    return pl.pallas_call(
        body, name=name, grid=(bsz, seq // tm),
        in_specs=[_row(tm, D), _row(tm, D), _row(tm, D, 0), _row(tm, D, 1), _full2((D, D)), _row(tm, D), _seqvec(D)],
        out_specs=[_row(tm, D), _row(tm, D), _row(tm, D)],
        out_shape=[SDS((bsz, seq, D), BF16), SDS((bsz, seq, D), F32), SDS((bsz, seq, D), F32)],
        compiler_params=_cp("parallel", "parallel"),
    )(ya, yb, p3, p3, wout, x1, gt)


def _merge_bwd(dx2, gt, mix, ya, yb, p3, wout, name):
    bsz, seq, _ = ya.shape
    tm = _pick_tile(seq, (512, 256, 128))

    def body(dx_ref, gt_ref, mix_ref, ya_ref, yb_ref, ga_ref, gbb_ref, w_ref, dmix_ref, dya_ref, dyb_ref, dp_ref, dgt_ref):
        i = pl.program_id(1)
        dx = dx_ref[0]
        dmix = (gt_ref[0] * dx).astype(BF16)
        dmix_ref[0] = dmix
        part = jnp.sum(dx * mix_ref[0], axis=0, keepdims=True)

        @pl.when(i == 0)
        def _():
            dgt_ref[0] = part

        @pl.when(i > 0)
        def _():
            dgt_ref[0] += part

        dmg = _dot_nt(dmix, w_ref[...])
        sa = _sigmoid(ga_ref[0])
        sb = _sigmoid(gbb_ref[0])
        dya_ref[0] = (dmg * sa).astype(BF16)
        dyb_ref[0] = (dmg * sb).astype(BF16)
        dp_ref[0, :, 0:D] = (dmg * ya_ref[0] * sa * (1.0 - sa)).astype(BF16)
        dp_ref[0, :, D:2 * D] = (dmg * yb_ref[0] * sb * (1.0 - sb)).astype(BF16)

    bshape = SDS((bsz, seq, D), BF16)
    return pl.pallas_call(
        body, name=name, grid=(bsz, seq // tm),
        in_specs=[_row(tm, D), _seqvec(D), _row(tm, D), _row(tm, D), _row(tm, D), _row(tm, D, 0), _row(tm, D, 1),
                  _full2((D, D))],
        out_specs=[_row(tm, D), _row(tm, D), _row(tm, D), _row(tm, 2 * D), _seqvec(D)],
        out_shape=[bshape, bshape, bshape, SDS((bsz, seq, 2 * D), BF16), SDS((bsz, 1, D), F32)],
        compiler_params=_cp("arbitrary", "arbitrary"),
    )(dx2, gt, mix, ya, yb, p3, p3, wout)


def _final_loss(x3, gfin, target, name):
    bsz, seq, dm = x3.shape
    tm = _pick_tile(seq, (512, 256, 128))

    def body(x_ref, g_ref, t_ref, dx_ref, loss_ref, dg_ref):
        b = pl.program_id(0)
        i = pl.program_id(1)
        xf = x_ref[0]
        gv = g_ref[...]
        r = lax.rsqrt(jnp.mean(xf * xf, axis=-1, keepdims=True) + EPS)
        xhat = xf * r
        e = xhat * gv - t_ref[0]
        dy = e * (1.0 / dm)
        dxh = dy * gv
        dx_ref[0] = r * (dxh - xhat * jnp.mean(dxh * xhat, axis=-1, keepdims=True))
        p_l = jnp.sum(e * e, axis=0, keepdims=True) * (0.5 / dm)
        p_g = jnp.sum(dy * xhat, axis=0, keepdims=True)

        @pl.when((i == 0) & (b == 0))
        def _():
            loss_ref[...] = p_l
            dg_ref[...] = p_g

        @pl.when((i > 0) | (b > 0))
        def _():
            loss_ref[...] += p_l
            dg_ref[...] += p_g

    return pl.pallas_call(
        body, name=name, grid=(bsz, seq // tm),
        in_specs=[_row(tm, dm), _full2((1, dm)), _row(tm, dm)],
        out_specs=[_row(tm, dm), _full2((1, dm)), _full2((1, dm))],
        out_shape=[SDS((bsz, seq, dm), F32), SDS((1, dm), F32), SDS((1, dm), F32)],
        compiler_params=_cp("arbitrary", "arbitrary"),
    )(x3, gfin, target)


def _ada_fwd(c_all, w_shard, b_shard):
    nb = c_all.shape[0]
    n = w_shard.shape[2]

    def body(c_ref, w_ref, b_ref, o_ref):
        cv = c_ref[...]
        cond = (cv * _sigmoid(cv)).astype(BF16)
        o_ref[...] = _dot(cond, w_ref[0].astype(BF16)) + b_ref[...]

    return pl.pallas_call(body, name="ada_fwd", out_shape=SDS((nb, n), F32), compiler_params=_cp())(
        c_all, w_shard, b_shard)


def _ada_bwd(c_all, dmod_shard, dmod_all):
    n = dmod_shard.shape[1]

    def body(c_ref, ds_ref, da_ref, gw_ref, gb_ref):
        cv = c_ref[...]
        cond = (cv * _sigmoid(cv)).astype(BF16)
        gw_ref[...] = _dot_tn(cond, ds_ref[...].astype(BF16))
        gb_ref[...] = jnp.sum(da_ref[...], axis=0, keepdims=True)

    return pl.pallas_call(
        body, name="ada_bwd", out_shape=[SDS((D, n), F32), SDS((1, dmod_all.shape[1]), F32)], compiler_params=_cp(),
    )(c_all, dmod_shard, dmod_all)


def _adamw_math(w, g, m, v):
    m = B1 * m + (1.0 - B1) * g
    v = B2 * v + (1.0 - B2) * (g * g)
    delta = -LR * ((m / BC1) / (jnp.sqrt(v / BC2) + AEPS) + WD * w)
    return delta, m, v


def _adamw_big(w, m, v, recv_own, recv_sib, name):
    _, rows, cols = w.shape
    tr = _pick_tile(rows, tuple(t for t in (512, 256, 128, 64, 32, 16, 8) if t * cols <= 96 * 1024))

    def body(w_ref, m_ref, v_ref, a_ref, b_ref, g_ref, d_ref, mo_ref, vo_ref):
        def chip_sum(r):
            acc = r[0].astype(F32)
            for k in range(1, N_CHIPS):
                acc = acc + r[k].astype(F32)
            return acc

        g = chip_sum(a_ref) + chip_sum(b_ref)
        delta, mn, vn = _adamw_math(w_ref[0], g, m_ref[0], v_ref[0])
        g_ref[0] = g
        d_ref[0] = delta
        mo_ref[0] = mn
        vo_ref[0] = vn

    own = pl.BlockSpec((1, tr, cols), lambda i: (0, i, 0))
    rspec = pl.BlockSpec((N_CHIPS, tr, cols), lambda i: (0, i, 0))
    shp = SDS(w.shape, F32)
    return pl.pallas_call(
        body, name=name, grid=(rows // tr,),
        in_specs=[own, own, own, rspec, rspec], out_specs=[own, own, own, own], out_shape=[shp, shp, shp, shp],
        compiler_params=_cp("parallel"),
    )(w, m, v, recv_own, recv_sib)


def _adamw_plain(w, m, v, g, name):
    def body(w_ref, m_ref, v_ref, g_ref, d_ref, mo_ref, vo_ref):
        delta, mn, vn = _adamw_math(w_ref[...], g_ref[...], m_ref[...], v_ref[...])
        d_ref[...] = delta
        mo_ref[...] = mn
        vo_ref[...] = vn

    shp = SDS(w.shape, F32)
    return pl.pallas_call(body, name=name, out_shape=[shp, shp, shp], compiler_params=_cp())(w, m, v, g)


def _adamw_rows(w, m, v, g, name):
    _, rows, cols = w.shape
    tr = _pick_tile(rows, (128, 64, 32, 16, 8))

    def body(w_ref, m_ref, v_ref, g_ref, d_ref, mo_ref, vo_ref):
        delta, mn, vn = _adamw_math(w_ref[0], g_ref[...], m_ref[0], v_ref[0])
        d_ref[0] = delta
        mo_ref[0] = mn
        vo_ref[0] = vn

    spec = pl.BlockSpec((1, tr, cols), lambda i: (0, i, 0))
    shp = SDS(w.shape, F32)
    return pl.pallas_call(
        body, name=name, grid=(rows // tr,), in_specs=[spec] * 3 + [pl.BlockSpec((tr, cols), lambda i: (i, 0))],
        out_specs=[spec] * 3, out_shape=[shp] * 3, compiler_params=_cp("parallel"),
    )(w, m, v, g)


def _sum_slabs(r, name):
    n, rows, cols = r.shape
    tr = _pick_tile(rows, (256, 128, 64))

    def body(r_ref, o_ref):
        acc = r_ref[0].astype(F32)
        for j in range(1, n):
            acc = acc + r_ref[j].astype(F32)
        o_ref[...] = acc

    return pl.pallas_call(
        body, name=name, grid=(rows // tr,),
        in_specs=[pl.BlockSpec((n, tr, cols), lambda i: (0, i, 0))],
        out_specs=pl.BlockSpec((tr, cols), lambda i: (i, 0)), out_shape=SDS((rows, cols), F32),
        compiler_params=_cp("parallel"),
    )(r)


def _place():
    return lax.axis_index("x"), lax.axis_index("y"), lax.axis_index("c")


def _all_gather8(blk, name):
    m_per, n = blk.shape

    def body(x_ref, out_ref, send_sems, recv_sems, local_sem):
        x, y, c = _place()
        me, sibling = (x, y, c), (x, y, 1 - c)
        chips = [(1 - x, y), (x, 1 - y), (1 - x, 1 - y)]

        def rows(px, py, pc):
            return out_ref.at[pl.ds((4 * px + 2 * py + pc) * m_per, m_per), :]

        def copy(k, block, to, src=None):
            return pltpu.make_async_remote_copy(
                src_ref=rows(*block) if src is None else src, dst_ref=rows(*block),
                send_sem=send_sems.at[k], recv_sem=recv_sems.at[k], device_id=to, device_id_type=MESH)

        mine = pltpu.make_async_copy(x_ref, rows(*me), local_sem)
        mine.start()
        first = [copy(0, me, sibling, src=x_ref)]
        first += [copy(1 + j, me, (*chip, c), src=x_ref) for j, chip in enumerate(chips)]
        for cp in first:
            cp.start()
        passed = [copy(4 + j, (*chip, c), sibling) for j, chip in enumerate(chips)]
        for j, chip in enumerate(chips):
            copy(1 + j, (*chip, c), me).wait_recv()
            passed[j].start()
        copy(0, sibling, me).wait_recv()
        for j, chip in enumerate(chips):
            copy(4 + j, (*chip, 1 - c), me).wait_recv()
        for cp in first + passed:
            cp.wait_send()
        mine.wait()

    return pl.pallas_call(
        body, name=name, out_shape=SDS((N_DEV * m_per, n), blk.dtype),
        in_specs=[pl.BlockSpec(memory_space=pltpu.VMEM)], out_specs=pl.BlockSpec(memory_space=pltpu.VMEM),
        scratch_shapes=[pltpu.SemaphoreType.DMA((7,)), pltpu.SemaphoreType.DMA((7,)), pltpu.SemaphoreType.DMA],
        compiler_params=pltpu.CompilerParams(vmem_limit_bytes=VMEM_LIMIT),
    )(blk)


def _chip_peers(x, y):
    return [(1 - x, y), (x, 1 - y), (1 - x, 1 - y)]


def _peer_copies(src_refs, land_refs, send_sems, recv_sems, scatter, landed):
    x, y, c = _place()
    cps = []
    for a, (src_ref, land_ref) in enumerate(zip(src_refs, land_refs)):
        for j, (px, py) in enumerate(_chip_peers(x, y)):
            src = src_ref.at[2 * px + py] if scatter else src_ref
            dst = land_ref.at[2 * px + py] if landed else land_ref.at[2 * x + y]
            cps.append(pltpu.make_async_remote_copy(
                src_ref=src, dst_ref=dst, send_sem=send_sems.at[3 * a + j], recv_sem=recv_sems.at[3 * a + j],
                device_id=(px, py, c), device_id_type=MESH))
    return cps


def _exchange_chips(srcs, scatter, name):
    n = len(srcs)

    def body(*refs):
        src_refs, land_refs = refs[:n], refs[n:2 * n]
        send_sems, recv_sems, local_sems = refs[2 * n:]
        x, y, _ = _place()
        me = 2 * x + y
        mine = [pltpu.make_async_copy(s.at[me] if scatter else s, l.at[me], local_sems.at[a])
                for a, (s, l) in enumerate(zip(src_refs, land_refs))]
        for cp in mine:
            cp.start()
        out = _peer_copies(src_refs, land_refs, send_sems, recv_sems, scatter, False)
        for cp in out:
            cp.start()
        for cp in _peer_copies(src_refs, land_refs, send_sems, recv_sems, scatter, True):
            cp.wait_recv()
        for cp in out:
            cp.wait_send()
        for cp in mine:
            cp.wait()

    anyspec = pl.BlockSpec(memory_space=pl.ANY)
    shapes = [SDS(s.shape if scatter else (N_CHIPS,) + s.shape, s.dtype) for s in srcs]
    return pl.pallas_call(
        body, name=name, out_shape=shapes, in_specs=[anyspec] * n, out_specs=[anyspec] * n,
        scratch_shapes=[pltpu.SemaphoreType.DMA((3 * n,)), pltpu.SemaphoreType.DMA((3 * n,)),
                        pltpu.SemaphoreType.DMA((n,))],
        compiler_params=pltpu.CompilerParams(vmem_limit_bytes=VMEM_LIMIT),
    )(*srcs)


_HBM = pl.BlockSpec(memory_space=pltpu.HBM)
_SEM = pl.BlockSpec(memory_space=pltpu.SEMAPHORE)
_EFFECT = pltpu.SideEffectType.DATAFLOW_SIDE_EFFECTING


def _exchange_begin(srcs, lands, scatter, name):
    n = len(srcs)

    def body(*refs):
        src_refs, land_refs = refs[:n], refs[n:2 * n]
        send_sems, recv_sems = refs[2 * n:2 * n + 2]
        token = refs[-1]
        for cp in _peer_copies(src_refs, land_refs, send_sems, recv_sems, scatter, False):
            cp.start()
        token[...] = jnp.zeros_like(token)

    res = pl.pallas_call(
        body, name=name,
        out_shape=(pltpu.SemaphoreType.DMA((3 * n,)), pltpu.SemaphoreType.DMA((3 * n,)),
                   *[pltpu.HBM(s.shape, s.dtype) for s in srcs], *[pltpu.HBM(l.shape, l.dtype) for l in lands],
                   SDS((SUB, LANE), F32)),
        in_specs=[_HBM] * (2 * n), out_specs=(_SEM, _SEM, *[_HBM] * (2 * n), pl.BlockSpec(memory_space=pltpu.VMEM)),
        input_output_aliases={i: 2 + i for i in range(2 * n)},
        compiler_params=pltpu.CompilerParams(has_side_effects=_EFFECT),
    )(*[pltpu.with_memory_space_constraint(a, pltpu.HBM) for a in (*srcs, *lands)])
    return res[0], res[1], res[2:2 + n], res[2 + n:2 + 2 * n], res[-1]


def _exchange_end(handle, after, scatter, name):
    send_sems, recv_sems, srcs, lands, _ = handle
    n = len(srcs)

    def body(*refs):
        src_refs, land_refs = refs[:n], refs[n:2 * n]
        send_sems, recv_sems = refs[2 * n:2 * n + 2]
        for cp in _peer_copies(src_refs, land_refs, send_sems, recv_sems, scatter, True):
            cp.wait_send()
            cp.wait_recv()

    res = pl.pallas_call(
        body, name=name,
        out_shape=tuple(pltpu.HBM(a.shape, a.dtype) for a in (*srcs, *lands)),
        in_specs=[_HBM] * (2 * n) + [_SEM, _SEM, pl.BlockSpec(memory_space=pl.ANY)], out_specs=tuple([_HBM] * (2 * n)),
        input_output_aliases={i: i for i in range(2 * n)},
        compiler_params=pltpu.CompilerParams(has_side_effects=_EFFECT),
    )(*srcs, *lands, send_sems, recv_sems, after)
    return list(res[n:])


def _own_slab(stack4, chip):
    idx = lax.broadcasted_iota(jnp.int32, (N_CHIPS,) + (1,) * (stack4.ndim - 1), 0)
    return jnp.where(idx == chip, stack4, jnp.zeros((), stack4.dtype))


def _swap_sibling(vs, name):
    n = len(vs)

    def body(*refs):
        in_refs, out_refs = refs[:n], refs[n:2 * n]
        send_sems, recv_sems = refs[2 * n:]
        x, y, c = _place()
        cps = [pltpu.make_async_remote_copy(
            src_ref=i, dst_ref=o, send_sem=send_sems.at[a], recv_sem=recv_sems.at[a],
            device_id=(x, y, 1 - c), device_id_type=MESH) for a, (i, o) in enumerate(zip(in_refs, out_refs))]
        for cp in cps:
            cp.start()
        for cp in cps:
            cp.wait()

    anyspec = pl.BlockSpec(memory_space=pl.ANY)
    return pl.pallas_call(
        body, name=name, out_shape=[SDS(v.shape, v.dtype) for v in vs], in_specs=[anyspec] * n, out_specs=[anyspec] * n,
        scratch_shapes=[pltpu.SemaphoreType.DMA((n,)), pltpu.SemaphoreType.DMA((n,))],
        compiler_params=pltpu.CompilerParams(vmem_limit_bytes=VMEM_LIMIT),
    )(*vs)


def _select(stacked, idx):
    out = stacked[0]
    for j in range(1, stacked.shape[0]):
        out = jnp.where(idx == j, stacked[j], out)
    return out


BIG = (
    ("w1_a", D, DFF // 4, True), ("w3_a", D, DFF // 4, True), ("w2_a", DFF // 4, D, False),
    ("w_in", D, 5632 // 4, True), ("w_conv_out", CW // 4, D, False), ("w_glu", SW // 4, SW, False),
    ("w_ssm_out", SW, D // 4, True), ("w_out", D // 4, D, False),
    ("w1_b", D, DFF // 4, True), ("w3_b", D, DFF // 4, True), ("w2_b", DFF // 4, D, False),
)
PACK_COLS = 1024


def _full_from_stacked(st, split_cols):
    _, rows, cols = st.shape
    if split_cols:
        return st.transpose(1, 0, 2).reshape(rows, N_CHIPS * cols)
    return st.reshape(N_CHIPS * rows, cols)


def _stacked_from_full(full, rows, cols, split_cols):
    if split_cols:
        return full.reshape(rows, N_CHIPS, cols).transpose(1, 0, 2)
    return full.reshape(N_CHIPS, rows, cols)


def _blockdiag(t):
    r = lax.broadcasted_iota(jnp.int32, (SW, GP), 0) // NH
    cidx = lax.broadcasted_iota(jnp.int32, (SW, GP), 1) // NP
    return jnp.where(r == cidx, jnp.tile(t, (NG, 1)), 0.0)


def _blockdiag_extract(acc):
    gs = NG // SSM_SUPER
    a = acc.reshape(NG, NH, gs, NP)
    sel = (lax.broadcasted_iota(jnp.int32, (NG, 1, gs, 1), 0) % gs) == lax.broadcasted_iota(jnp.int32, (NG, 1, gs, 1), 2)
    a = jnp.sum(jnp.where(sel, a, 0.0), axis=2)
    return a.transpose(1, 0, 2).reshape(NH, GP)


def _to_t(p):
    return p.transpose(2, 0, 1).reshape(NH, GP)


def _from_t(t):
    return t.reshape(NH, NG, NP).transpose(1, 2, 0)


def _c_to_t(p):
    return p.transpose(1, 0, 2).reshape(NH, GP)


def _c_from_t(t):
    return t.reshape(NH, NG, NP).transpose(1, 0, 2)


def _ffn_forward(x, g, sh, sc, gt, w1, w3, w2, tag):
    h = _norm_mod(x, g, sh, sc, f"{tag}_norm")
    a, b, hid = _swiglu_up(h, w1, w3, f"{tag}_up")
    f, xo = _ffn_down(hid, w2, x, gt, f"{tag}_down")
    return xo, (x, h, a, b, hid, f)


def _ffn_backward(dxo, saved, g, sc, gt, w1, w3, w2, tag):
    x, h, a, b, hid, f = saved
    dfs, da, db, dgt = _ffn_bwd_hid(dxo, gt, f, a, b, w2, f"{tag}_bwd_hid")
    tk = _pick_tile(a.shape[2], (1408, 512, 256, 128))
    dx, dsh, dsc, dg = _dh_norm_bwd([da, db], [w1, w3], tk, x, g, sc, dxo, f"{tag}_bwd_dh")
    h2, da2, db2 = _flat(h), _flat(da), _flat(db)
    gw1 = _mm(h2, da2, ta=True, name=f"{tag}_gw1")
    gw3 = _mm(h2, db2, ta=True, name=f"{tag}_gw3")
    gw2 = _mm(_flat(hid), _flat(dfs), ta=True, name=f"{tag}_gw2")
    return dx, (dsh, dsc, dgt, dg), (gw1, gw3, gw2)


def kernel(x, c, w_ada, b_ada, g_ffn1, w1_a, w3_a, w2_a, g_mix, w_in, conv_w, w_conv_out, a_re, a_im, b_re, b_im, c_re, c_im, log_dt, d_skip, w_glu, w_ssm_out, w_out, g_ffn2, w1_b, w3_b, w2_b, g_final, loss_target, m_w_ada, m_b_ada, m_g_ffn1, m_w1_a, m_w3_a, m_w2_a, m_g_mix, m_w_in, m_conv_w, m_w_conv_out, m_a_re, m_a_im, m_b_re, m_b_im, m_c_re, m_c_im, m_log_dt, m_d_skip, m_w_glu, m_w_ssm_out, m_w_out, m_g_ffn2, m_w1_b, m_w3_b, m_w2_b, m_g_final, v_w_ada, v_b_ada, v_g_ffn1, v_w1_a, v_w3_a, v_w2_a, v_g_mix, v_w_in, v_conv_w, v_w_conv_out, v_a_re, v_a_im, v_b_re, v_b_im, v_c_re, v_c_im, v_log_dt, v_d_skip, v_w_glu, v_w_ssm_out, v_w_out, v_g_ffn2, v_w1_b, v_w3_b, v_w2_b, v_g_final):
    args = dict(locals())
    names = ["w_ada", "b_ada", "g_ffn1", "w1_a", "w3_a", "w2_a", "g_mix", "w_in", "conv_w", "w_conv_out", "a_re",
             "a_im", "b_re", "b_im", "c_re", "c_im", "log_dt", "d_skip", "w_glu", "w_ssm_out", "w_out", "g_ffn2",
             "w1_b", "w3_b", "w2_b", "g_final"]
    bsz, seq, _ = x.shape
    mx, my, mc = _place()
    chip = 2 * mx + my
    dev = 4 * mx + 2 * my + mc

    groups = (BIG[:3], BIG[3:8], BIG[8:])
    wfull = {}

    def shards_bf16(grp):
        return [args[n][0].astype(BF16) for n, _, _, _ in grp]

    def unpack_group(gathered, grp):
        for (n, _, _, split), st in zip(grp, gathered):
            wfull[n] = _full_from_stacked(st, split)

    unpack_group(_exchange_chips(shards_bf16(groups[0]), False, "gather_w_ffn1"), groups[0])

    nmod_shard = NMOD * D // N_CHIPS
    c_all = _all_gather8(c.reshape(SUB, -1), "gather_c").reshape(N_DEV * bsz, D)
    b_shard = _select(b_ada.reshape(N_CHIPS, 1, nmod_shard), chip)
    mod_shard = _ada_fwd(c_all, w_ada, b_shard)
    nb = N_DEV * bsz
    cw_pad = jnp.pad(conv_w[0], ((0, SUB - 3), (0, nmod_shard - CW // N_CHIPS)))
    mod_st = _exchange_chips([jnp.concatenate([mod_shard, cw_pad], axis=0)], False, "gather_mod")[0]
    mod_all = mod_st[:, :nb].transpose(1, 0, 2).reshape(N_DEV, bsz, NMOD * D)
    mod = _select(mod_all, dev)

    rest_grp = groups[1] + groups[2]
    rest_src, mod = lax.optimization_barrier((shards_bf16(rest_grp), mod))
    rest_land = [_own_slab(jnp.broadcast_to(s[None], (N_CHIPS,) + s.shape), chip) for s in rest_src]
    rest_handle = _exchange_begin(rest_src, rest_land, False, "gather_w_rest_start")

    sh1, sc1, gt1, sh2, sc2, gt2, sh3, sc3, gt3 = [mod[:, None, j * D:(j + 1) * D] for j in range(NMOD)]
    convw = mod_st[:, nb:nb + 3, :CW // N_CHIPS].transpose(1, 0, 2).reshape(3, CW)
    convw8 = jnp.pad(convw, ((0, SUB - 3), (0, 0)))

    are, aim = a_re.reshape(1, GP), a_im.reshape(1, GP)
    ldt = jnp.broadcast_to(log_dt.reshape(NG, 1), (NG, NP)).reshape(1, GP)
    bre_t, bim_t = _to_t(b_re[0]), _to_t(b_im[0])
    abr, abi, bbr_t, bbi_t = _ssm_disc(are, aim, ldt, bre_t, bim_t)
    wb = jnp.concatenate([_blockdiag(bbr_t), _blockdiag(bbi_t)], axis=1).astype(BF16)
    wct = jnp.concatenate([_blockdiag(_c_to_t(c_re[0])), -_blockdiag(_c_to_t(c_im[0]))], axis=1).astype(BF16)
    ar8 = jnp.broadcast_to(abr, (SUB, GP))
    ai8 = jnp.broadcast_to(abi, (SUB, GP))

    x1, ffn1_saved = _ffn_forward(x, g_ffn1 + rest_handle[4][0:1, 0:1], sh1, sc1, gt1,
                                  wfull["w1_a"], wfull["w3_a"], wfull["w2_a"], "ffn1")
    unpack_group(_exchange_end(rest_handle, x1, False, "gather_w_rest_wait"), rest_grp)

    h2 = _norm_mod(x1, g_mix, sh2, sc2, "mix_norm")
    h2f = _flat(h2)
    win = wfull["w_in"]
    win1, winu, win3 = win[:, :3 * CW], win[:, 3 * CW:3 * CW + SW], win[:, 3 * CW + SW:]
    p1 = _mm(h2f, win1, name="mix_in1").reshape(bsz, seq, 3 * CW)
    u = _mm(h2f, winu, name="mix_inu").reshape(bsz, seq, SW)
    p3 = _mm(h2f, win3, name="mix_in3").reshape(bsz, seq, 2 * D)

    ya_in = _conv_fwd(p1, convw8, "conv_fwd")
    ya = _mm(_flat(ya_in), wfull["w_conv_out"], name="conv_out").reshape(bsz, seq, D)

    u_tm = u.transpose(1, 0, 2).reshape(seq * bsz, SW)
    xs, y_tm = _ssm_fwd(u_tm, wb, wct, ar8, ai8, bsz, "ssm_fwd")
    ys = y_tm.reshape(seq, bsz, SW).transpose(1, 0, 2)
    s0, z, s1, s2, yb = _ssm_post(ys, u, d_skip, wfull["w_glu"], wfull["w_ssm_out"], "ssm_post")

    merged, mix, x2 = _merge_out(ya, yb, p3, wfull["w_out"], x1, gt2, "merge_out")
    x3, ffn2_saved = _ffn_forward(x2, g_ffn2, sh3, sc3, gt3, wfull["w1_b"], wfull["w3_b"], wfull["w2_b"], "ffn2")

    dx3, lossvec, dgfin = _final_loss(x3, g_final.reshape(1, D), loss_target, "final_loss")
    loss = lax.psum(jnp.sum(lossvec), ("x", "y", "c"))

    gfull = {}
    dx2, (dsh3, dsc3, dgt3, dg3), (gfull["w1_b"], gfull["w3_b"], gfull["w2_b"]) = _ffn_backward(
        dx3, ffn2_saved, g_ffn2, sc3, gt3, wfull["w1_b"], wfull["w3_b"], wfull["w2_b"], "ffn2")

    def stack_group(grp):
        return [_stacked_from_full(gfull[n], rows, cols, split).astype(BF16) for n, rows, cols, split in grp]

    st_ffn2 = stack_group(groups[2])
    h_ffn2 = _exchange_begin(st_ffn2, [_own_slab(s, chip) for s in st_ffn2], True, "scatter_ffn2_start")

    dmix, dya, dyb, dp3, dgt2 = _merge_bwd(dx2, gt2 + h_ffn2[4][0, 0], mix, ya, yb, p3, wfull["w_out"], "merge_bwd")
    gfull["w_out"] = _mm(_flat(merged), _flat(dmix), ta=True, name="gw_out")
    dp1, dconvw8 = _conv_bwd(dya, wfull["w_conv_out"], p1, convw8, "conv_bwd")
    gfull["w_conv_out"] = _mm(_flat(ya_in), _flat(dya), ta=True, name="gw_conv_out")
    ds0, dz, ddskip = _ssm_post_bwd(dyb, s0, z, u, d_skip, wfull["w_glu"], wfull["w_ssm_out"], "ssm_post_bwd")
    gfull["w_ssm_out"] = _mm(_flat(s2), _flat(dyb), ta=True, name="gw_ssm_out")
    gfull["w_glu"] = _mm(_flat(s1), _flat(dz), ta=True, name="gw_glu")
    dy_tm = ds0.transpose(1, 0, 2).reshape(seq * bsz, SW)
    du_tm, dwb, dwct, dar8, dai8 = _ssm_bwd(dy_tm, u_tm, xs, wb, wct, ar8, ai8, bsz, "ssm_bwd")
    du = _du_combine(du_tm.reshape(seq, bsz, SW).transpose(1, 0, 2), ds0, d_skip, "du_combine")
    dx1, dsh2, dsc2, dgmix = _dh_norm_bwd([dp1, du, dp3], [win1, winu, win3], SW, x1, g_mix, sc2, dx2, "mix_bwd_dh")
    gfull["w_in"] = jnp.concatenate([
        _mm(h2f, _flat(dp1), ta=True, name="gw_in1"), _mm(h2f, _flat(du), ta=True, name="gw_inu"),
        _mm(h2f, _flat(dp3), ta=True, name="gw_in3")], axis=1)

    st_mix = stack_group(groups[1])
    h_mix = _exchange_begin(st_mix, [_own_slab(s, chip) for s in st_mix], True, "scatter_mix_start")

    grad_x, (dsh1, dsc1, dgt1, dg1), (gfull["w1_a"], gfull["w3_a"], gfull["w2_a"]) = _ffn_backward(
        dx1, ffn1_saved, g_ffn1, sc1, gt1 + h_mix[4][0, 0], wfull["w1_a"], wfull["w3_a"], wfull["w2_a"], "ffn1")

    sbw = GP // SSM_SUPER
    d_are, d_aim, d_ldt, d_bre_t, d_bim_t = _ssm_disc_bwd(
        are, aim, ldt, bre_t, bim_t, jnp.sum(dar8, axis=0, keepdims=True), jnp.sum(dai8, axis=0, keepdims=True),
        _blockdiag_extract(dwb[:, :sbw]), _blockdiag_extract(dwb[:, sbw:]))
    d_cre = _c_from_t(_blockdiag_extract(dwct[:, :sbw]))
    d_cim = -_c_from_t(_blockdiag_extract(dwct[:, sbw:]))

    small_parts = [dg1, dgmix, dg3, dgfin, dconvw8[:3], d_are, d_aim, _from_t(d_bre_t), _from_t(d_bim_t), d_cre, d_cim,
                   jnp.sum(d_ldt.reshape(NG, NP), axis=1), ddskip]
    small_sizes = [int(p.size) for p in small_parts]
    n_small = sum(small_sizes)
    n_small_pad = -(-n_small // (SUB * PACK_COLS)) * (SUB * PACK_COLS)
    dmod = jnp.concatenate([dsh1, dsc1, dgt1, dsh2, dsc2, dgt2, dsh3, dsc3, dgt3], axis=2).reshape(bsz * NMOD * D)
    flat = jnp.concatenate([p.reshape(-1) for p in small_parts] + [jnp.zeros((n_small_pad - n_small,), F32), dmod])
    allg = _all_gather8(flat.reshape(SUB, -1), "gather_small").reshape(N_DEV, -1)
    small = _sum_slabs(allg[:, :n_small_pad].reshape(N_DEV, -1, PACK_COLS), "sum_small").reshape(-1)
    sg, o = [], 0
    for p, sz in zip(small_parts, small_sizes):
        sg.append(small[o:o + sz].reshape(p.shape))
        o += sz
    (g_g1, g_gmix, g_g3, g_gfin, g_convw, g_are, g_aim, g_bre, g_bim, g_cre, g_cim, g_ldt, g_dskip) = sg

    dmod_all = allg[:, n_small_pad:].reshape(nb, NMOD * D)
    dmod_shard = _select(dmod_all.reshape(nb, N_CHIPS, nmod_shard).transpose(1, 0, 2), chip)
    g_wada, g_bada = _ada_bwd(c_all, dmod_shard, dmod_all)

    recv_ffn1 = _exchange_chips(stack_group(groups[0]), True, "scatter_ffn1")
    recv_mix = _exchange_end(h_mix, recv_ffn1[0], True, "scatter_mix_wait")
    recv_ffn2 = _exchange_end(h_ffn2, recv_ffn1[0], True, "scatter_ffn2_wait")
    recv = [*recv_ffn1, *recv_mix, *recv_ffn2]
    recv_sib = _swap_sibling(recv, "swap_sibling")

    grads, deltas, new_m, new_v = {}, {}, {}, {}
    for (n, _, _, _), r_own, r_sib in zip(BIG, recv, recv_sib):
        grads[n], deltas[n], new_m[n], new_v[n] = _adamw_big(
            args[n], args["m_" + n], args["v_" + n], r_own, r_sib, f"adamw_{n}")
    grads["w_ada"] = g_wada[None]
    deltas["w_ada"], new_m["w_ada"], new_v["w_ada"] = _adamw_rows(w_ada, m_w_ada, v_w_ada, g_wada, "adamw_w_ada")

    g_convw_shard = _select(g_convw.reshape(3, N_CHIPS, CW // N_CHIPS).transpose(1, 0, 2), chip)
    small_g = {"b_ada": g_bada, "g_ffn1": g_g1, "g_mix": g_gmix, "g_ffn2": g_g3, "g_final": g_gfin,
               "conv_w": g_convw_shard, "a_re": g_are, "a_im": g_aim, "b_re": g_bre, "b_im": g_bim,
               "c_re": g_cre, "c_im": g_cim, "log_dt": g_ldt, "d_skip": g_dskip}
    small_names = list(small_g)
    sizes = [int(args[n].size) for n in small_names]
    tot = sum(sizes)
    tot_pad = -(-tot // (SUB * PACK_COLS)) * (SUB * PACK_COLS)

    def pack(get):
        return jnp.concatenate([get(n).reshape(-1) for n in small_names] + [jnp.zeros((tot_pad - tot,), F32)]).reshape(
            -1, PACK_COLS)

    res = _adamw_plain(pack(lambda n: args[n]), pack(lambda n: args["m_" + n]), pack(lambda n: args["v_" + n]),
                       pack(lambda n: small_g[n]), "adamw_small")
    o = 0
    for n, sz in zip(small_names, sizes):
        shp = args[n].shape
        grads[n] = small_g[n].reshape(shp)
        deltas[n], new_m[n], new_v[n] = [r.reshape(-1)[o:o + sz].reshape(shp) for r in res]
        o += sz

    return (loss, grad_x, *[grads[n] for n in names], *[deltas[n] for n in names],
            *[new_m[n] for n in names], *[new_v[n] for n in names])
```

```python
import functools
import math

import jax
import jax.numpy as jnp
from jax import lax
from jax.experimental import pallas as pl
from jax.experimental.pallas import tpu as pltpu

F32 = jnp.float32
BF16 = jnp.bfloat16
SDS = jax.ShapeDtypeStruct
MESH = pl.DeviceIdType.MESH

D = 1024
DFF = 2816
CW = 1024
SW = 512
NG, NP, NH = 32, 64, 16
GP = NG * NP
NMOD = 9
EPS = 1e-6
N_CHIPS = 4
N_DEV = 8
SUB = 8
LANE = 128
SSM_SUPER = 4
VMEM_LIMIT = 50 * 1024 * 1024

LR, B1, B2, AEPS, WD, STEP = 0.001, 0.9, 0.999, 1e-08, 0.01, 10
BC1 = 1.0 - B1 ** STEP
BC2 = 1.0 - B2 ** STEP


def _cp(*sem):
    return pltpu.CompilerParams(dimension_semantics=sem or None, vmem_limit_bytes=VMEM_LIMIT)


def _pick_tile(n, cands):
    for t in cands:
        if t <= n and n % t == 0:
            return t
    return n


def _dot(a, b):
    return lax.dot_general(a, b, (((1,), (0,)), ((), ())), preferred_element_type=F32)


def _dot_nt(a, b):
    return lax.dot_general(a, b, (((1,), (1,)), ((), ())), preferred_element_type=F32)


def _dot_tn(a, b):
    return lax.dot_general(a, b, (((0,), (0,)), ((), ())), preferred_element_type=F32)


def _row(tm, width, col=0):
    return pl.BlockSpec((1, tm, width), lambda b, i, *_: (b, i, col))


def _seqvec(width):
    return pl.BlockSpec((1, 1, width), lambda b, *_: (b, 0, 0))


def _full2(shape):
    return pl.BlockSpec(shape, lambda *_: (0, 0))


def _sigmoid(x):
    return jax.nn.sigmoid(x)


def _mm(a, b, *, ta=False, tb=False, out_dtype=F32, name):
    if ta:
        kdim, m = a.shape
    else:
        m, kdim = a.shape
    n = b.shape[0] if tb else b.shape[1]
    tm = _pick_tile(m, (1408, 1024, 512, 256, 128))
    tn = _pick_tile(n, (1408, 1024, 512, 256, 128))
    tk = _pick_tile(kdim, (512, 256, 128))
    nk = kdim // tk

    def body(a_ref, b_ref, o_ref, acc_ref):
        k = pl.program_id(2)

        @pl.when(k == 0)
        def _():
            acc_ref[...] = jnp.zeros_like(acc_ref)

        av = a_ref[...].astype(BF16)
        bv = b_ref[...].astype(BF16)
        dn = (((0 if ta else 1,), (1 if tb else 0,)), ((), ()))
        acc_ref[...] += lax.dot_general(av, bv, dn, preferred_element_type=F32)

        @pl.when(k == nk - 1)
        def _():
            o_ref[...] = acc_ref[...].astype(out_dtype)

    a_spec = pl.BlockSpec((tk, tm), lambda i, j, k: (k, i)) if ta else pl.BlockSpec((tm, tk), lambda i, j, k: (i, k))
    b_spec = pl.BlockSpec((tn, tk), lambda i, j, k: (j, k)) if tb else pl.BlockSpec((tk, tn), lambda i, j, k: (k, j))
    return pl.pallas_call(
        body, name=name, grid=(m // tm, n // tn, nk),
        in_specs=[a_spec, b_spec],
        out_specs=pl.BlockSpec((tm, tn), lambda i, j, k: (i, j)),
        out_shape=SDS((m, n), out_dtype),
        scratch_shapes=[pltpu.VMEM((tm, tn), F32)],
        compiler_params=_cp("parallel", "parallel", "arbitrary"),
    )(a, b)


def _flat(a):
    return a.reshape(-1, a.shape[-1])


def _norm_mod(x, g, sh, sc, name):
    bsz, seq, dm = x.shape
    tm = _pick_tile(seq, (512, 256, 128))

    def body(x_ref, g_ref, sh_ref, sc_ref, o_ref):
        xf = x_ref[0]
        r = lax.rsqrt(jnp.mean(xf * xf, axis=-1, keepdims=True) + EPS)
        hn = xf * r * g_ref[...]
        o_ref[0] = (hn * (1.0 + sc_ref[0]) + sh_ref[0]).astype(BF16)

    return pl.pallas_call(
        body, name=name, grid=(bsz, seq // tm),
        in_specs=[_row(tm, dm), _full2((1, dm)), _seqvec(dm), _seqvec(dm)],
        out_specs=_row(tm, dm), out_shape=SDS((bsz, seq, dm), BF16),
        compiler_params=_cp("parallel", "parallel"),
    )(x, g, sh, sc)


def _swiglu_up(h, w1, w3, name):
    bsz, seq, dm = h.shape
    nf = w1.shape[1]
    tm = _pick_tile(seq, (512, 256, 128))
    tn = _pick_tile(nf, (1408, 512, 256, 128))

    def body(h_ref, w1_ref, w3_ref, a_ref, b_ref, hid_ref):
        hv = h_ref[0]
        a = _dot(hv, w1_ref[...])
        b = _dot(hv, w3_ref[...])
        sg = _sigmoid(a)
        sa = a * sg
        a_ref[0] = (b * (sg * (1.0 + a * (1.0 - sg)))).astype(BF16)
        b_ref[0] = sa.astype(BF16)
        hid_ref[0] = (sa * b).astype(BF16)

    wspec = pl.BlockSpec((dm, tn), lambda n, b, i: (0, n))
    ospec = pl.BlockSpec((1, tm, tn), lambda n, b, i: (b, i, n))
    shp = SDS((bsz, seq, nf), BF16)
    return pl.pallas_call(
        body, name=name, grid=(nf // tn, bsz, seq // tm),
        in_specs=[pl.BlockSpec((1, tm, dm), lambda n, b, i: (b, i, 0)), wspec, wspec],
        out_specs=[ospec, ospec, ospec], out_shape=[shp, shp, shp],
        compiler_params=_cp("parallel", "parallel", "parallel"),
    )(h, w1, w3)


def _ffn_down(hid, w2, x, gt, name):
    bsz, seq, nf = hid.shape
    dm = w2.shape[1]
    tm = _pick_tile(seq, (512, 256, 128))

    def body(hid_ref, w2_ref, x_ref, gt_ref, f_ref, xo_ref):
        f = _dot(hid_ref[0], w2_ref[...])
        f_ref[0] = f
        xo_ref[0] = x_ref[0] + 0.5 * gt_ref[0] * f

    shp = SDS((bsz, seq, dm), F32)
    return pl.pallas_call(
        body, name=name, grid=(bsz, seq // tm),
        in_specs=[_row(tm, nf), _full2((nf, dm)), _row(tm, dm), _seqvec(dm)],
        out_specs=[_row(tm, dm), _row(tm, dm)], out_shape=[shp, shp],
        compiler_params=_cp("parallel", "parallel"),
    )(hid, w2, x, gt)


def _ffn_bwd_hid(dxo, gt, f, a, b, w2, name):
    bsz, seq, dm = dxo.shape
    nf = a.shape[2]
    tm = _pick_tile(seq, (512, 256, 128))
    tn = _pick_tile(nf, (1408, 512, 256, 128))

    def body(dxo_ref, gt_ref, f_ref, a_ref, b_ref, w2_ref, dfs_ref, da_ref, db_ref, dgt_ref):
        i = pl.program_id(1)
        n = pl.program_id(2)

        @pl.when(n == 0)
        def _():
            dxo = dxo_ref[0]
            dfs_ref[0] = (0.5 * gt_ref[0] * dxo).astype(BF16)
            part = jnp.sum(0.5 * dxo * f_ref[0], axis=0, keepdims=True)

            @pl.when(i == 0)
            def _():
                dgt_ref[0] = part

            @pl.when(i > 0)
            def _():
                dgt_ref[0] += part

        dhid = _dot_nt(dfs_ref[0], w2_ref[pl.ds(pl.multiple_of(n * tn, tn), tn), :])
        da_ref[0] = (dhid * a_ref[0].astype(F32)).astype(BF16)
        db_ref[0] = (dhid * b_ref[0].astype(F32)).astype(BF16)

    hspec = pl.BlockSpec((1, tm, tn), lambda b, i, n: (b, i, n))
    return pl.pallas_call(
        body, name=name, grid=(bsz, seq // tm, nf // tn),
        in_specs=[_row(tm, dm), _seqvec(dm), _row(tm, dm), hspec, hspec, _full2((nf, dm))],
        out_specs=[_row(tm, dm), hspec, hspec, _seqvec(dm)],
        out_shape=[SDS((bsz, seq, dm), BF16), SDS((bsz, seq, nf), BF16), SDS((bsz, seq, nf), BF16),
                   SDS((bsz, 1, dm), F32)],
        compiler_params=_cp("arbitrary", "arbitrary", "arbitrary"),
    )(dxo, gt, f, a, b, w2)


def _dh_norm_bwd(pieces, weights, x, g, sc, dxo, name):
    bsz, seq, dm = x.shape
    tm = _pick_tile(seq, (512, 256, 128))
    npc = len(pieces)

    def body(*refs):
        p_refs = refs[:npc]
        w_hbm = refs[npc:2 * npc]
        x_ref, g_ref, sc_ref, dxo_ref, dx_ref, dsh_ref, dsc_ref, dg_ref = refs[2 * npc:2 * npc + 8]
        w_refs = refs[2 * npc + 8:]
        b = pl.program_id(0)
        i = pl.program_id(1)

        @pl.when((i == 0) & (b == 0))
        def _():
            for src, dst in zip(w_hbm, w_refs):
                pltpu.sync_copy(src, dst)

        dh = _dot_nt(p_refs[0][0], w_refs[0][...])
        for j in range(1, npc):
            dh = dh + _dot_nt(p_refs[j][0], w_refs[j][...])
        xf = x_ref[0]
        gv = g_ref[...]
        r = lax.rsqrt(jnp.mean(xf * xf, axis=-1, keepdims=True) + EPS)
        xhat = xf * r
        dhn = dh * (1.0 + sc_ref[0])
        p_sh = jnp.sum(dh, axis=0, keepdims=True)
        p_sc = jnp.sum(dh * (xhat * gv), axis=0, keepdims=True)
        p_g = jnp.sum(dhn * xhat, axis=0, keepdims=True)
        dxh = dhn * gv
        dx_ref[0] = dxo_ref[0] + r * (dxh - xhat * jnp.mean(dxh * xhat, axis=-1, keepdims=True))

        @pl.when(i == 0)
        def _():
            dsh_ref[0] = p_sh
            dsc_ref[0] = p_sc

        @pl.when(i > 0)
        def _():
            dsh_ref[0] += p_sh
            dsc_ref[0] += p_sc

        @pl.when((i == 0) & (b == 0))
        def _():
            dg_ref[...] = p_g

        @pl.when((i > 0) | (b > 0))
        def _():
            dg_ref[...] += p_g

    return pl.pallas_call(
        body, name=name, grid=(bsz, seq // tm),
        in_specs=[_row(tm, p.shape[2]) for p in pieces] + [pl.BlockSpec(memory_space=pl.ANY)] * npc + [
            _row(tm, dm), _full2((1, dm)), _seqvec(dm), _row(tm, dm)],
        out_specs=[_row(tm, dm), _seqvec(dm), _seqvec(dm), _full2((1, dm))],
        out_shape=[SDS((bsz, seq, dm), F32), SDS((bsz, 1, dm), F32), SDS((bsz, 1, dm), F32), SDS((1, dm), F32)],
        scratch_shapes=[pltpu.VMEM(w.shape, w.dtype) for w in weights],
        compiler_params=_cp("arbitrary", "arbitrary"),
    )(*pieces, *weights, x, g, sc, dxo)


HALO = 16


def _conv_core(gc, v, gch, vh, w, first):
    cv = gc * v
    halo = jnp.where(first, 0.0, gch * vh)
    ext = jnp.concatenate([halo, cv], axis=0)
    cv1 = pltpu.roll(ext, 1, 0)[HALO:]
    cv2 = pltpu.roll(ext, 2, 0)[HALO:]
    conv = w[0:1] * cv2 + w[1:2] * cv1 + w[2:3] * cv
    return cv, cv1, cv2, conv


def _prev_halo(tm, col):
    return pl.BlockSpec((1, HALO, CW), lambda b, i, *_: (b, jnp.maximum(i * (tm // HALO) - 1, 0), col))


def _next_halo(tm, seq, col):
    return pl.BlockSpec((1, HALO, CW), lambda b, i, *_: (b, jnp.minimum((i + 1) * (tm // HALO), seq // HALO - 1), col))


def _conv_fwd(p1, convw8, name):
    bsz, seq, _ = p1.shape
    tm = _pick_tile(seq, (512, 256, 128))

    def body(gb_ref, gc_ref, v_ref, gch_ref, vh_ref, w_ref, o_ref):
        first = pl.program_id(1) == 0
        _, _, _, conv = _conv_core(gc_ref[0], v_ref[0], gch_ref[0], vh_ref[0], w_ref[...], first)
        o_ref[0] = (gb_ref[0] * conv).astype(BF16)

    return pl.pallas_call(
        body, name=name, grid=(bsz, seq // tm),
        in_specs=[_row(tm, CW, 0), _row(tm, CW, 1), _row(tm, CW, 2), _prev_halo(tm, 1), _prev_halo(tm, 2),
                  _full2((8, CW))],
        out_specs=_row(tm, CW), out_shape=SDS((bsz, seq, CW), BF16),
        compiler_params=_cp("parallel", "parallel"),
    )(p1, p1, p1, p1, p1, convw8)


def _conv_bwd(dya, wco, p1, convw8, name):
    bsz, seq, _ = p1.shape
    tm = _pick_tile(seq, (512, 256, 128))
    nt = seq // tm
    ext_rows = tm + HALO

    def body(dya_ref, dyan_ref, wco_ref, gb_ref, gbn_ref, gc_ref, v_ref, gch_ref, vh_ref, w_ref, dp_ref, dw_ref):
        b = pl.program_id(0)
        i = pl.program_id(1)
        w = w_ref[...]
        cv, cv1, cv2, conv = _conv_core(gc_ref[0], v_ref[0], gch_ref[0], vh_ref[0], w, i == 0)
        dya_ext = jnp.concatenate([dya_ref[0], dyan_ref[0]], axis=0)
        dyain_ext = _dot_nt(dya_ext, wco_ref[...])
        gb = gb_ref[0]
        gb_ext = jnp.concatenate([gb, gbn_ref[0]], axis=0)
        rows = lax.broadcasted_iota(jnp.int32, (ext_rows, 1), 0)
        dconv_ext = jnp.where((rows < tm) | (i < nt - 1), dyain_ext * gb_ext, 0.0)
        dconv = dconv_ext[:tm]
        dconv1 = pltpu.roll(dconv_ext, ext_rows - 1, 0)[:tm]
        dconv2 = pltpu.roll(dconv_ext, ext_rows - 2, 0)[:tm]
        dcv = w[2:3] * dconv + w[1:2] * dconv1 + w[0:1] * dconv2
        dp_ref[0, :, 0:CW] = (dyain_ext[:tm] * conv).astype(BF16)
        dp_ref[0, :, CW:2 * CW] = (dcv * v_ref[0]).astype(BF16)
        dp_ref[0, :, 2 * CW:3 * CW] = (dcv * gc_ref[0]).astype(BF16)
        g0 = jnp.sum(dconv * cv2, axis=0, keepdims=True)
        g1 = jnp.sum(dconv * cv1, axis=0, keepdims=True)
        g2 = jnp.sum(dconv * cv, axis=0, keepdims=True)
        upd = jnp.concatenate([g0, g1, g2, jnp.zeros((5, CW), F32)], axis=0)

        @pl.when((i == 0) & (b == 0))
        def _():
            dw_ref[...] = upd

        @pl.when((i > 0) | (b > 0))
        def _():
            dw_ref[...] += upd

    return pl.pallas_call(
        body, name=name, grid=(bsz, seq // tm),
        in_specs=[_row(tm, CW), _next_halo(tm, seq, 0), _full2((CW, D)),
                  _row(tm, CW, 0), _next_halo(tm, seq, 0), _row(tm, CW, 1), _row(tm, CW, 2),
                  _prev_halo(tm, 1), _prev_halo(tm, 2), _full2((8, CW))],
        out_specs=[_row(tm, 3 * CW), _full2((8, CW))],
        out_shape=[SDS((bsz, seq, 3 * CW), BF16), SDS((8, CW), F32)],
        compiler_params=_cp("arbitrary", "arbitrary"),
    )(dya, dya, wco, p1, p1, p1, p1, p1, p1, convw8)


def _disc(are, aim, ldt, bre, bim):
    dt = jnp.exp(ldt)
    mag = jnp.exp(are * dt)
    ang = aim * dt
    abr = mag * jnp.cos(ang)
    abi = mag * jnp.sin(ang)
    nr = abr - 1.0
    den = are * are + aim * aim
    cr = (nr * are + abi * aim) / den
    ci = (abi * are - nr * aim) / den
    return abr, abi, cr * bre - ci * bim, cr * bim + ci * bre


def _ssm_disc(are, aim, ldt, bre_t, bim_t):
    def body(are_ref, aim_ref, ldt_ref, bre_ref, bim_ref, abr_ref, abi_ref, bbr_ref, bbi_ref):
        abr, abi, bbr, bbi = _disc(are_ref[...], aim_ref[...], ldt_ref[...], bre_ref[...], bim_ref[...])
        abr_ref[...] = abr
        abi_ref[...] = abi
        bbr_ref[...] = bbr
        bbi_ref[...] = bbi

    v1, vh = SDS((1, GP), F32), SDS((NH, GP), F32)
    return pl.pallas_call(body, name="ssm_disc", out_shape=[v1, v1, vh, vh], compiler_params=_cp())(
        are, aim, ldt, bre_t, bim_t)


def _ssm_disc_bwd(are, aim, ldt, bre_t, bim_t, dabr, dabi, dbbr, dbbi):
    def body(are_ref, aim_ref, ldt_ref, bre_ref, bim_ref, g0, g1, g2, g3, o0, o1, o2, o3, o4):
        prim = (are_ref[...], aim_ref[...], ldt_ref[...], bre_ref[...], bim_ref[...])
        _, vjp = jax.vjp(_disc, *prim)
        d_are, d_aim, d_ldt, d_bre, d_bim = vjp((g0[...], g1[...], g2[...], g3[...]))
        o0[...] = d_are
        o1[...] = d_aim
        o2[...] = d_ldt
        o3[...] = d_bre
        o4[...] = d_bim

    v1, vh = SDS((1, GP), F32), SDS((NH, GP), F32)
    return pl.pallas_call(body, name="ssm_disc_bwd", out_shape=[v1, v1, v1, vh, vh], compiler_params=_cp())(
        are, aim, ldt, bre_t, bim_t, dabr, dabi, dbbr, dbbi)


def _scan_chunk(buf_ref, nt, bsz, ar, ai, init_r, init_i, reverse):
    nsub = SUB // bsz
    row = lax.broadcasted_iota(jnp.int32, (SUB, GP), 0)
    shift = ((SUB - bsz) if reverse else bsz) % SUB
    order = list(range(nsub - 1, -1, -1)) if reverse else list(range(nsub))

    def step(j, carry):
        pr, pi = carry
        jj = (nt - 1 - j) if reverse else j
        off = pl.multiple_of(jj * SUB, SUB)
        br = buf_ref[pl.ds(off, SUB), 0:GP]
        bi = buf_ref[pl.ds(off, SUB), GP:2 * GP]
        nr, ni = pr, pi
        for s in order:
            sr, si = nr, ni
            if shift:
                sr = pltpu.roll(sr, shift, 0)
                si = pltpu.roll(si, shift, 0)
            cr = ar * sr - ai * si + br
            ci = ar * si + ai * sr + bi
            if nsub == 1:
                nr, ni = cr, ci
            else:
                m = (row >= s * bsz) & (row < (s + 1) * bsz)
                nr = jnp.where(m, cr, nr)
                ni = jnp.where(m, ci, ni)
        buf_ref[pl.ds(off, SUB), 0:GP] = nr
        buf_ref[pl.ds(off, SUB), GP:2 * GP] = ni
        return nr, ni

    return lax.fori_loop(0, nt, step, (init_r, init_i))


def _ssm_chunk_rows(total_rows, bsz):
    return min(total_rows, 64 * bsz)


def _ssm_fwd(u_tm, wb, wct, ar8, ai8, bsz, name):
    rt = u_tm.shape[0]
    r = _ssm_chunk_rows(rt, bsz)
    nt = r // SUB

    def body(u_ref, wb_hbm, wct_hbm, ar_ref, ai_ref, x_ref, y_ref, wb_ref, wct_ref, st_ref):
        @pl.when(pl.program_id(0) == 0)
        def _():
            pltpu.sync_copy(wb_hbm, wb_ref)
            pltpu.sync_copy(wct_hbm, wct_ref)
            st_ref[...] = jnp.zeros_like(st_ref)

        x_ref[...] = _dot(u_ref[...].astype(BF16), wb_ref[...])
        fr, fi = _scan_chunk(x_ref, nt, bsz, ar_ref[...], ai_ref[...], st_ref[:, 0:GP], st_ref[:, GP:2 * GP], False)
        st_ref[:, 0:GP] = fr
        st_ref[:, GP:2 * GP] = fi
        y_ref[...] = _dot_nt(x_ref[...].astype(BF16), wct_ref[...])

    anyspec = pl.BlockSpec(memory_space=pl.ANY)
    return pl.pallas_call(
        body, name=name, grid=(rt // r,),
        in_specs=[pl.BlockSpec((r, SW), lambda i: (i, 0)), anyspec, anyspec, _full2((SUB, GP)), _full2((SUB, GP))],
        out_specs=[pl.BlockSpec((r, 2 * GP), lambda i: (i, 0)), pl.BlockSpec((r, SW), lambda i: (i, 0))],
        out_shape=[SDS((rt, 2 * GP), F32), SDS((rt, SW), F32)],
        scratch_shapes=[pltpu.VMEM((SW, 2 * GP), BF16), pltpu.VMEM((SW, 2 * GP), BF16), pltpu.VMEM((SUB, 2 * GP), F32)],
        compiler_params=_cp("arbitrary"),
    )(u_tm, wb, wct, ar8, ai8)


def _ssm_bwd(dy_tm, u_tm, xs, wb, wct, ar8, ai8, bsz, name):
    rt = u_tm.shape[0]
    r = _ssm_chunk_rows(rt, bsz)
    nt = r // SUB
    nc = rt // r
    ub = SW // SSM_SUPER
    sb = GP // SSM_SUPER

    def body(dy_ref, u_ref, x_ref, xh_ref, wb_hbm, wct_hbm, ar_ref, ai_ref,
             du_ref, dwb_hbm, dwct_hbm, dar_ref, dai_ref, wb_ref, wct_ref, g_ref, st_ref, awb_ref, awct_ref):
        i = pl.program_id(0)

        @pl.when(i == 0)
        def _():
            pltpu.sync_copy(wb_hbm, wb_ref)
            pltpu.sync_copy(wct_hbm, wct_ref)
            st_ref[...] = jnp.zeros_like(st_ref)
            awb_ref[...] = jnp.zeros_like(awb_ref)
            awct_ref[...] = jnp.zeros_like(awct_ref)
            dar_ref[...] = jnp.zeros_like(dar_ref)
            dai_ref[...] = jnp.zeros_like(dai_ref)

        dyb = dy_ref[...].astype(BF16)
        g_ref[...] = _dot(dyb, wct_ref[...])
        ar = ar_ref[...]
        ai = ai_ref[...]
        fr, fi = _scan_chunk(g_ref, nt, bsz, ar, -ai, st_ref[:, 0:GP], st_ref[:, GP:2 * GP], True)
        st_ref[:, 0:GP] = fr
        st_ref[:, GP:2 * GP] = fi

        gb = g_ref[...].astype(BF16)
        du_ref[...] = _dot_nt(gb, wb_ref[...])
        ub16 = u_ref[...].astype(BF16)
        xb16 = x_ref[...].astype(BF16)
        for s in range(SSM_SUPER):
            us = ub16[:, s * ub:(s + 1) * ub]
            ds = dyb[:, s * ub:(s + 1) * ub]
            for half in range(2):
                cols = slice(half * GP + s * sb, half * GP + (s + 1) * sb)
                ocols = slice(half * sb, (half + 1) * sb)
                awb_ref[s * ub:(s + 1) * ub, ocols] += _dot_tn(us, gb[:, cols])
                awct_ref[s * ub:(s + 1) * ub, ocols] += _dot_tn(ds, xb16[:, cols])

        gr = g_ref[:, 0:GP]
        gi = g_ref[:, GP:2 * GP]
        xsr = pltpu.roll(x_ref[:, 0:GP], bsz, 0)
        xsi = pltpu.roll(x_ref[:, GP:2 * GP], bsz, 0)
        inner = lax.broadcasted_iota(jnp.int32, (r, 1), 0) >= bsz
        t_r = jnp.where(inner, gr * xsr + gi * xsi, 0.0)
        t_i = jnp.where(inner, gi * xsr - gr * xsi, 0.0)
        acc_r = jnp.sum(t_r.reshape(nt, SUB, GP), axis=0)
        acc_i = jnp.sum(t_i.reshape(nt, SUB, GP), axis=0)
        hr = xh_ref[:, 0:GP]
        hi = xh_ref[:, GP:2 * GP]
        if bsz % SUB:
            hr = pltpu.roll(hr, bsz, 0)
            hi = pltpu.roll(hi, bsz, 0)
        edge = (lax.broadcasted_iota(jnp.int32, (SUB, 1), 0) < bsz) & (i < nc - 1)
        g0r = g_ref[0:SUB, 0:GP]
        g0i = g_ref[0:SUB, GP:2 * GP]
        dar_ref[...] += acc_r + jnp.where(edge, g0r * hr + g0i * hi, 0.0)
        dai_ref[...] += acc_i + jnp.where(edge, g0i * hr - g0r * hi, 0.0)

        @pl.when(i == nc - 1)
        def _():
            pltpu.sync_copy(awb_ref, dwb_hbm)
            pltpu.sync_copy(awct_ref, dwct_hbm)

    anyspec = pl.BlockSpec(memory_space=pl.ANY)
    rev = lambda i: (nc - 1 - i, 0)
    wshape = (SW, 2 * sb)
    return pl.pallas_call(
        body, name=name, grid=(nc,),
        in_specs=[pl.BlockSpec((r, SW), rev), pl.BlockSpec((r, SW), rev), pl.BlockSpec((r, 2 * GP), rev),
                  pl.BlockSpec((SUB, 2 * GP), lambda i: (jnp.maximum((nc - 1 - i) * nt - 1, 0), 0)),
                  anyspec, anyspec, _full2((SUB, GP)), _full2((SUB, GP))],
        out_specs=[pl.BlockSpec((r, SW), rev), anyspec, anyspec, _full2((SUB, GP)), _full2((SUB, GP))],
        out_shape=[SDS((rt, SW), F32), SDS(wshape, F32), SDS(wshape, F32), SDS((SUB, GP), F32), SDS((SUB, GP), F32)],
        scratch_shapes=[pltpu.VMEM((SW, 2 * GP), BF16), pltpu.VMEM((SW, 2 * GP), BF16),
                        pltpu.VMEM((r, 2 * GP), F32), pltpu.VMEM((SUB, 2 * GP), F32),
                        pltpu.VMEM(wshape, F32), pltpu.VMEM(wshape, F32)],
        compiler_params=_cp("arbitrary"),
    )(dy_tm, u_tm, xs, xs, wb, wct, ar8, ai8)


GELU_C = math.sqrt(2.0 / math.pi)


def _gelu(x):
    return 0.5 * x * (1.0 + jnp.tanh(GELU_C * (x + 0.044715 * x * x * x)))


def _gelu_grad(x):
    th = jnp.tanh(GELU_C * (x + 0.044715 * x * x * x))
    return 0.5 * (1.0 + th) + 0.5 * x * (1.0 - th * th) * GELU_C * (1.0 + 3.0 * 0.044715 * x * x)


def _ssm_post(ys, u, dskip, wglu, wso, name):
    bsz, seq, _ = ys.shape
    tm = _pick_tile(seq, (512, 256, 128))

    def body(ys_ref, u_ref, d_ref, wg_ref, wo_ref, s0_ref, z_ref, s1_ref, s2_ref, yb_ref):
        s0 = ys_ref[0] + d_ref[...] * u_ref[0]
        s1 = _gelu(s0)
        s1b = s1.astype(BF16)
        z = _dot(s1b, wg_ref[...])
        s2b = (s1 * _sigmoid(z)).astype(BF16)
        s0_ref[0] = s0
        z_ref[0] = z
        s1_ref[0] = s1b
        s2_ref[0] = s2b
        yb_ref[0] = _dot(s2b, wo_ref[...])

    return pl.pallas_call(
        body, name=name, grid=(bsz, seq // tm),
        in_specs=[_row(tm, SW), _row(tm, SW), _full2((1, SW)), _full2((SW, SW)), _full2((SW, D))],
        out_specs=[_row(tm, SW), _row(tm, SW), _row(tm, SW), _row(tm, SW), _row(tm, D)],
        out_shape=[SDS((bsz, seq, SW), F32), SDS((bsz, seq, SW), F32), SDS((bsz, seq, SW), BF16),
                   SDS((bsz, seq, SW), BF16), SDS((bsz, seq, D), F32)],
        compiler_params=_cp("parallel", "parallel"),
    )(ys, u, dskip, wglu, wso)


def _ssm_post_bwd(dyb, s0, z, u, dskip, wglu, wso, name):
    bsz, seq, _ = s0.shape
    tm = _pick_tile(seq, (512, 256, 128))

    def body(dyb_ref, s0_ref, z_ref, u_ref, wg_ref, wo_ref, ds0_ref, dz_ref, dd_ref):
        b = pl.program_id(0)
        i = pl.program_id(1)
        ds2 = _dot_nt(dyb_ref[0], wo_ref[...])
        s0 = s0_ref[0]
        s1 = _gelu(s0)
        sg = _sigmoid(z_ref[0])
        dz = ds2 * s1 * sg * (1.0 - sg)
        dzb = dz.astype(BF16)
        ds1 = ds2 * sg + _dot_nt(dzb, wg_ref[...])
        ds0 = ds1 * _gelu_grad(s0)
        ds0_ref[0] = ds0
        dz_ref[0] = dzb
        part = jnp.sum(ds0 * u_ref[0], axis=0, keepdims=True)

        @pl.when((i == 0) & (b == 0))
        def _():
            dd_ref[...] = part

        @pl.when((i > 0) | (b > 0))
        def _():
            dd_ref[...] += part

    del dskip
    return pl.pallas_call(
        body, name=name, grid=(bsz, seq // tm),
        in_specs=[_row(tm, D), _row(tm, SW), _row(tm, SW), _row(tm, SW), _full2((SW, SW)), _full2((SW, D))],
        out_specs=[_row(tm, SW), _row(tm, SW), _full2((1, SW))],
        out_shape=[SDS((bsz, seq, SW), F32), SDS((bsz, seq, SW), BF16), SDS((1, SW), F32)],
        compiler_params=_cp("arbitrary", "arbitrary"),
    )(dyb, s0, z, u, wglu, wso)


def _du_combine(du_ssm, ds0, dskip, name):
    bsz, seq, _ = ds0.shape
    tm = _pick_tile(seq, (512, 256, 128))

    def body(a_ref, b_ref, d_ref, o_ref):
        o_ref[0] = (a_ref[0] + b_ref[0] * d_ref[...]).astype(BF16)

    return pl.pallas_call(
        body, name=name, grid=(bsz, seq // tm),
        in_specs=[_row(tm, SW), _row(tm, SW), _full2((1, SW))],
        out_specs=_row(tm, SW), out_shape=SDS((bsz, seq, SW), BF16),
        compiler_params=_cp("parallel", "parallel"),
    )(du_ssm, ds0, dskip)


def _merge_out(ya, yb, p3, wout, x1, gt, name):
    bsz, seq, _ = ya.shape
    tm = _pick_tile(seq, (512, 256, 128))

    def body(ya_ref, yb_ref, ga_ref, gbb_ref, w_ref, x_ref, gt_ref, mg_ref, mix_ref, xo_ref):
        merged = (_sigmoid(ga_ref[0]) * ya_ref[0] + _sigmoid(gbb_ref[0]) * yb_ref[0]).astype(BF16)
        mix = _dot(merged, w_ref[...])
        mg_ref[0] = merged
        mix_ref[0] = mix
        xo_ref[0] = x_ref[0] + gt_ref[0] * mix

    return pl.pallas_call(
        body, name=name, grid=(bsz, seq // tm),
        in_specs=[_row(tm, D), _row(tm, D), _row(tm, D, 0), _row(tm, D, 1), _full2((D, D)), _row(tm, D), _seqvec(D)],
        out_specs=[_row(tm, D), _row(tm, D), _row(tm, D)],
        out_shape=[SDS((bsz, seq, D), BF16), SDS((bsz, seq, D), F32), SDS((bsz, seq, D), F32)],
        compiler_params=_cp("parallel", "parallel"),
    )(ya, yb, p3, p3, wout, x1, gt)


def _merge_bwd(dx2, gt, mix, ya, yb, p3, wout, name):
    bsz, seq, _ = ya.shape
    tm = _pick_tile(seq, (512, 256, 128))

    def body(dx_ref, gt_ref, mix_ref, ya_ref, yb_ref, ga_ref, gbb_ref, w_ref, dmix_ref, dya_ref, dyb_ref, dp_ref, dgt_ref):
        i = pl.program_id(1)
        dx = dx_ref[0]
        dmix = (gt_ref[0] * dx).astype(BF16)
        dmix_ref[0] = dmix
        part = jnp.sum(dx * mix_ref[0], axis=0, keepdims=True)

        @pl.when(i == 0)
        def _():
            dgt_ref[0] = part

        @pl.when(i > 0)
        def _():
            dgt_ref[0] += part

        dmg = _dot_nt(dmix, w_ref[...])
        sa = _sigmoid(ga_ref[0])
        sb = _sigmoid(gbb_ref[0])
        dya_ref[0] = (dmg * sa).astype(BF16)
        dyb_ref[0] = (dmg * sb).astype(BF16)
        dp_ref[0, :, 0:D] = (dmg * ya_ref[0] * sa * (1.0 - sa)).astype(BF16)
        dp_ref[0, :, D:2 * D] = (dmg * yb_ref[0] * sb * (1.0 - sb)).astype(BF16)

    bshape = SDS((bsz, seq, D), BF16)
    return pl.pallas_call(
        body, name=name, grid=(bsz, seq // tm),
        in_specs=[_row(tm, D), _seqvec(D), _row(tm, D), _row(tm, D), _row(tm, D), _row(tm, D, 0), _row(tm, D, 1),
                  _full2((D, D))],
        out_specs=[_row(tm, D), _row(tm, D), _row(tm, D), _row(tm, 2 * D), _seqvec(D)],
        out_shape=[bshape, bshape, bshape, SDS((bsz, seq, 2 * D), BF16), SDS((bsz, 1, D), F32)],
        compiler_params=_cp("arbitrary", "arbitrary"),
    )(dx2, gt, mix, ya, yb, p3, p3, wout)


def _final_loss(x3, gfin, target, name):
    bsz, seq, dm = x3.shape
    tm = _pick_tile(seq, (512, 256, 128))

    def body(x_ref, g_ref, t_ref, dx_ref, loss_ref, dg_ref):
        b = pl.program_id(0)
        i = pl.program_id(1)
        xf = x_ref[0]
        gv = g_ref[...]
        r = lax.rsqrt(jnp.mean(xf * xf, axis=-1, keepdims=True) + EPS)
        xhat = xf * r
        e = xhat * gv - t_ref[0]
        dy = e * (1.0 / dm)
        dxh = dy * gv
        dx_ref[0] = r * (dxh - xhat * jnp.mean(dxh * xhat, axis=-1, keepdims=True))
        p_l = jnp.sum(e * e, axis=0, keepdims=True) * (0.5 / dm)
        p_g = jnp.sum(dy * xhat, axis=0, keepdims=True)

        @pl.when((i == 0) & (b == 0))
        def _():
            loss_ref[...] = p_l
            dg_ref[...] = p_g

        @pl.when((i > 0) | (b > 0))
        def _():
            loss_ref[...] += p_l
            dg_ref[...] += p_g

    return pl.pallas_call(
        body, name=name, grid=(bsz, seq // tm),
        in_specs=[_row(tm, dm), _full2((1, dm)), _row(tm, dm)],
        out_specs=[_row(tm, dm), _full2((1, dm)), _full2((1, dm))],
        out_shape=[SDS((bsz, seq, dm), F32), SDS((1, dm), F32), SDS((1, dm), F32)],
        compiler_params=_cp("arbitrary", "arbitrary"),
    )(x3, gfin, target)


def _ada_fwd(c_all, w_shard, b_shard):
    nb = c_all.shape[0]
    n = w_shard.shape[2]

    def body(c_ref, w_ref, b_ref, o_ref):
        cv = c_ref[...]
        cond = (cv * _sigmoid(cv)).astype(BF16)
        o_ref[...] = _dot(cond, w_ref[0].astype(BF16)) + b_ref[...]

    return pl.pallas_call(body, name="ada_fwd", out_shape=SDS((nb, n), F32), compiler_params=_cp())(
        c_all, w_shard, b_shard)


def _ada_bwd(c_all, dmod_shard, dmod_all):
    n = dmod_shard.shape[1]

    def body(c_ref, ds_ref, da_ref, gw_ref, gb_ref):
        cv = c_ref[...]
        cond = (cv * _sigmoid(cv)).astype(BF16)
        gw_ref[...] = _dot_tn(cond, ds_ref[...].astype(BF16))
        gb_ref[...] = jnp.sum(da_ref[...], axis=0, keepdims=True)

    return pl.pallas_call(
        body, name="ada_bwd", out_shape=[SDS((D, n), F32), SDS((1, dmod_all.shape[1]), F32)], compiler_params=_cp(),
    )(c_all, dmod_shard, dmod_all)


def _adamw_math(w, g, m, v):
    m = B1 * m + (1.0 - B1) * g
    v = B2 * v + (1.0 - B2) * (g * g)
    delta = -LR * ((m / BC1) / (jnp.sqrt(v / BC2) + AEPS) + WD * w)
    return delta, m, v


def _adamw_big(w, m, v, recv_own, recv_sib, name):
    _, rows, cols = w.shape
    tr = _pick_tile(rows, tuple(t for t in (512, 256, 128, 64, 32, 16, 8) if t * cols <= 96 * 1024))

    def body(w_ref, m_ref, v_ref, a_ref, b_ref, g_ref, d_ref, mo_ref, vo_ref):
        def chip_sum(r):
            acc = r[0].astype(F32)
            for k in range(1, N_CHIPS):
                acc = acc + r[k].astype(F32)
            return acc

        g = chip_sum(a_ref) + chip_sum(b_ref)
        delta, mn, vn = _adamw_math(w_ref[0], g, m_ref[0], v_ref[0])
        g_ref[0] = g
        d_ref[0] = delta
        mo_ref[0] = mn
        vo_ref[0] = vn

    own = pl.BlockSpec((1, tr, cols), lambda i: (0, i, 0))
    rspec = pl.BlockSpec((N_CHIPS, tr, cols), lambda i: (0, i, 0))
    shp = SDS(w.shape, F32)
    return pl.pallas_call(
        body, name=name, grid=(rows // tr,),
        in_specs=[own, own, own, rspec, rspec], out_specs=[own, own, own, own], out_shape=[shp, shp, shp, shp],
        compiler_params=_cp("parallel"),
    )(w, m, v, recv_own, recv_sib)


def _adamw_plain(w, m, v, g, name):
    def body(w_ref, m_ref, v_ref, g_ref, d_ref, mo_ref, vo_ref):
        delta, mn, vn = _adamw_math(w_ref[...], g_ref[...], m_ref[...], v_ref[...])
        d_ref[...] = delta
        mo_ref[...] = mn
        vo_ref[...] = vn

    shp = SDS(w.shape, F32)
    return pl.pallas_call(body, name=name, out_shape=[shp, shp, shp], compiler_params=_cp())(w, m, v, g)


def _adamw_rows(w, m, v, g, name):
    _, rows, cols = w.shape
    tr = _pick_tile(rows, (128, 64, 32, 16, 8))

    def body(w_ref, m_ref, v_ref, g_ref, d_ref, mo_ref, vo_ref):
        delta, mn, vn = _adamw_math(w_ref[0], g_ref[...], m_ref[0], v_ref[0])
        d_ref[0] = delta
        mo_ref[0] = mn
        vo_ref[0] = vn

    spec = pl.BlockSpec((1, tr, cols), lambda i: (0, i, 0))
    shp = SDS(w.shape, F32)
    return pl.pallas_call(
        body, name=name, grid=(rows // tr,), in_specs=[spec] * 3 + [pl.BlockSpec((tr, cols), lambda i: (i, 0))],
        out_specs=[spec] * 3, out_shape=[shp] * 3, compiler_params=_cp("parallel"),
    )(w, m, v, g)


def _sum_slabs(r, name):
    n, rows, cols = r.shape
    tr = _pick_tile(rows, (256, 128, 64))

    def body(r_ref, o_ref):
        acc = r_ref[0].astype(F32)
        for j in range(1, n):
            acc = acc + r_ref[j].astype(F32)
        o_ref[...] = acc

    return pl.pallas_call(
        body, name=name, grid=(rows // tr,),
        in_specs=[pl.BlockSpec((n, tr, cols), lambda i: (0, i, 0))],
        out_specs=pl.BlockSpec((tr, cols), lambda i: (i, 0)), out_shape=SDS((rows, cols), F32),
        compiler_params=_cp("parallel"),
    )(r)


def _place():
    return lax.axis_index("x"), lax.axis_index("y"), lax.axis_index("c")


def _all_gather8(blk, name):
    m_per, n = blk.shape

    def body(x_ref, out_ref, send_sems, recv_sems, local_sem):
        x, y, c = _place()
        me, sibling = (x, y, c), (x, y, 1 - c)
        chips = [(1 - x, y), (x, 1 - y), (1 - x, 1 - y)]

        def rows(px, py, pc):
            return out_ref.at[pl.ds((4 * px + 2 * py + pc) * m_per, m_per), :]

        def copy(k, block, to, src=None):
            return pltpu.make_async_remote_copy(
                src_ref=rows(*block) if src is None else src, dst_ref=rows(*block),
                send_sem=send_sems.at[k], recv_sem=recv_sems.at[k], device_id=to, device_id_type=MESH)

        mine = pltpu.make_async_copy(x_ref, rows(*me), local_sem)
        mine.start()
        first = [copy(0, me, sibling, src=x_ref)]
        first += [copy(1 + j, me, (*chip, c), src=x_ref) for j, chip in enumerate(chips)]
        for cp in first:
            cp.start()
        passed = [copy(4 + j, (*chip, c), sibling) for j, chip in enumerate(chips)]
        for j, chip in enumerate(chips):
            copy(1 + j, (*chip, c), me).wait_recv()
            passed[j].start()
        copy(0, sibling, me).wait_recv()
        for j, chip in enumerate(chips):
            copy(4 + j, (*chip, 1 - c), me).wait_recv()
        for cp in first + passed:
            cp.wait_send()
        mine.wait()

    return pl.pallas_call(
        body, name=name, out_shape=SDS((N_DEV * m_per, n), blk.dtype),
        in_specs=[pl.BlockSpec(memory_space=pltpu.VMEM)], out_specs=pl.BlockSpec(memory_space=pltpu.VMEM),
        scratch_shapes=[pltpu.SemaphoreType.DMA((7,)), pltpu.SemaphoreType.DMA((7,)), pltpu.SemaphoreType.DMA],
        compiler_params=pltpu.CompilerParams(vmem_limit_bytes=VMEM_LIMIT),
    )(blk)


def _chip_peers(x, y):
    return [(1 - x, y), (x, 1 - y), (1 - x, 1 - y)]


def _peer_copies(src_refs, land_refs, send_sems, recv_sems, scatter, landed):
    x, y, c = _place()
    cps = []
    for a, (src_ref, land_ref) in enumerate(zip(src_refs, land_refs)):
        for j, (px, py) in enumerate(_chip_peers(x, y)):
            src = src_ref.at[2 * px + py] if scatter else src_ref
            dst = land_ref.at[2 * px + py] if landed else land_ref.at[2 * x + y]
            cps.append(pltpu.make_async_remote_copy(
                src_ref=src, dst_ref=dst, send_sem=send_sems.at[3 * a + j], recv_sem=recv_sems.at[3 * a + j],
                device_id=(px, py, c), device_id_type=MESH))
    return cps


def _exchange_chips(srcs, scatter, name):
    n = len(srcs)

    def body(*refs):
        src_refs, land_refs = refs[:n], refs[n:2 * n]
        send_sems, recv_sems, local_sems = refs[2 * n:]
        x, y, _ = _place()
        me = 2 * x + y
        mine = [pltpu.make_async_copy(s.at[me] if scatter else s, l.at[me], local_sems.at[a])
                for a, (s, l) in enumerate(zip(src_refs, land_refs))]
        for cp in mine:
            cp.start()
        out = _peer_copies(src_refs, land_refs, send_sems, recv_sems, scatter, False)
        for cp in out:
            cp.start()
        for cp in _peer_copies(src_refs, land_refs, send_sems, recv_sems, scatter, True):
            cp.wait_recv()
        for cp in out:
            cp.wait_send()
        for cp in mine:
            cp.wait()

    anyspec = pl.BlockSpec(memory_space=pl.ANY)
    shapes = [SDS(s.shape if scatter else (N_CHIPS,) + s.shape, s.dtype) for s in srcs]
    return pl.pallas_call(
        body, name=name, out_shape=shapes, in_specs=[anyspec] * n, out_specs=[anyspec] * n,
        scratch_shapes=[pltpu.SemaphoreType.DMA((3 * n,)), pltpu.SemaphoreType.DMA((3 * n,)),
                        pltpu.SemaphoreType.DMA((n,))],
        compiler_params=pltpu.CompilerParams(vmem_limit_bytes=VMEM_LIMIT),
    )(*srcs)


_HBM = pl.BlockSpec(memory_space=pltpu.HBM)
_SEM = pl.BlockSpec(memory_space=pltpu.SEMAPHORE)
_EFFECT = pltpu.SideEffectType.DATAFLOW_SIDE_EFFECTING


def _exchange_begin(srcs, lands, scatter, name):
    n = len(srcs)

    def body(*refs):
        src_refs, land_refs = refs[:n], refs[n:2 * n]
        send_sems, recv_sems = refs[2 * n:2 * n + 2]
        token = refs[-1]
        for cp in _peer_copies(src_refs, land_refs, send_sems, recv_sems, scatter, False):
            cp.start()
        token[...] = jnp.zeros_like(token)

    res = pl.pallas_call(
        body, name=name,
        out_shape=(pltpu.SemaphoreType.DMA((3 * n,)), pltpu.SemaphoreType.DMA((3 * n,)),
                   *[pltpu.HBM(s.shape, s.dtype) for s in srcs], *[pltpu.HBM(l.shape, l.dtype) for l in lands],
                   SDS((SUB, LANE), F32)),
        in_specs=[_HBM] * (2 * n), out_specs=(_SEM, _SEM, *[_HBM] * (2 * n), pl.BlockSpec(memory_space=pltpu.VMEM)),
        input_output_aliases={i: 2 + i for i in range(2 * n)},
        compiler_params=pltpu.CompilerParams(has_side_effects=_EFFECT),
    )(*[pltpu.with_memory_space_constraint(a, pltpu.HBM) for a in (*srcs, *lands)])
    return res[0], res[1], res[2:2 + n], res[2 + n:2 + 2 * n], res[-1]


def _exchange_end(handle, after, scatter, name):
    send_sems, recv_sems, srcs, lands, _ = handle
    n = len(srcs)

    def body(*refs):
        src_refs, land_refs = refs[:n], refs[n:2 * n]
        send_sems, recv_sems = refs[2 * n:2 * n + 2]
        for cp in _peer_copies(src_refs, land_refs, send_sems, recv_sems, scatter, True):
            cp.wait_send()
            cp.wait_recv()

    res = pl.pallas_call(
        body, name=name,
        out_shape=tuple(pltpu.HBM(a.shape, a.dtype) for a in (*srcs, *lands)),
        in_specs=[_HBM] * (2 * n) + [_SEM, _SEM, pl.BlockSpec(memory_space=pl.ANY)], out_specs=tuple([_HBM] * (2 * n)),
        input_output_aliases={i: i for i in range(2 * n)},
        compiler_params=pltpu.CompilerParams(has_side_effects=_EFFECT),
    )(*srcs, *lands, send_sems, recv_sems, after)
    return list(res[n:])


def _own_slab(stack4, chip):
    idx = lax.broadcasted_iota(jnp.int32, (N_CHIPS,) + (1,) * (stack4.ndim - 1), 0)
    return jnp.where(idx == chip, stack4, jnp.zeros((), stack4.dtype))


def _swap_sibling(vs, name):
    n = len(vs)

    def body(*refs):
        in_refs, out_refs = refs[:n], refs[n:2 * n]
        send_sems, recv_sems = refs[2 * n:]
        x, y, c = _place()
        cps = [pltpu.make_async_remote_copy(
            src_ref=i, dst_ref=o, send_sem=send_sems.at[a], recv_sem=recv_sems.at[a],
            device_id=(x, y, 1 - c), device_id_type=MESH) for a, (i, o) in enumerate(zip(in_refs, out_refs))]
        for cp in cps:
            cp.start()
        for cp in cps:
            cp.wait()

    anyspec = pl.BlockSpec(memory_space=pl.ANY)
    return pl.pallas_call(
        body, name=name, out_shape=[SDS(v.shape, v.dtype) for v in vs], in_specs=[anyspec] * n, out_specs=[anyspec] * n,
        scratch_shapes=[pltpu.SemaphoreType.DMA((n,)), pltpu.SemaphoreType.DMA((n,))],
        compiler_params=pltpu.CompilerParams(vmem_limit_bytes=VMEM_LIMIT),
    )(*vs)


def _select(stacked, idx):
    out = stacked[0]
    for j in range(1, stacked.shape[0]):
        out = jnp.where(idx == j, stacked[j], out)
    return out


BIG = (
    ("w1_a", D, DFF // 4, True), ("w3_a", D, DFF // 4, True), ("w2_a", DFF // 4, D, False),
    ("w_in", D, 5632 // 4, True), ("w_conv_out", CW // 4, D, False), ("w_glu", SW // 4, SW, False),
    ("w_ssm_out", SW, D // 4, True), ("w_out", D // 4, D, False),
    ("w1_b", D, DFF // 4, True), ("w3_b", D, DFF // 4, True), ("w2_b", DFF // 4, D, False),
)
PACK_COLS = 1024


def _full_from_stacked(st, split_cols):
    _, rows, cols = st.shape
    if split_cols:
        return st.transpose(1, 0, 2).reshape(rows, N_CHIPS * cols)
    return st.reshape(N_CHIPS * rows, cols)


def _stacked_from_full(full, rows, cols, split_cols):
    if split_cols:
        return full.reshape(rows, N_CHIPS, cols).transpose(1, 0, 2)
    return full.reshape(N_CHIPS, rows, cols)


def _blockdiag(t):
    r = lax.broadcasted_iota(jnp.int32, (SW, GP), 0) // NH
    cidx = lax.broadcasted_iota(jnp.int32, (SW, GP), 1) // NP
    return jnp.where(r == cidx, jnp.tile(t, (NG, 1)), 0.0)


def _blockdiag_extract(acc):
    gs = NG // SSM_SUPER
    a = acc.reshape(NG, NH, gs, NP)
    sel = (lax.broadcasted_iota(jnp.int32, (NG, 1, gs, 1), 0) % gs) == lax.broadcasted_iota(jnp.int32, (NG, 1, gs, 1), 2)
    a = jnp.sum(jnp.where(sel, a, 0.0), axis=2)
    return a.transpose(1, 0, 2).reshape(NH, GP)


def _to_t(p):
    return p.transpose(2, 0, 1).reshape(NH, GP)


def _from_t(t):
    return t.reshape(NH, NG, NP).transpose(1, 2, 0)


def _c_to_t(p):
    return p.transpose(1, 0, 2).reshape(NH, GP)


def _c_from_t(t):
    return t.reshape(NH, NG, NP).transpose(1, 0, 2)


def _ffn_forward(x, g, sh, sc, gt, w1, w3, w2, tag):
    h = _norm_mod(x, g, sh, sc, f"{tag}_norm")
    a, b, hid = _swiglu_up(h, w1, w3, f"{tag}_up")
    w2 = w2(hid) if callable(w2) else w2
    f, xo = _ffn_down(hid, w2, x, gt, f"{tag}_down")
    return xo, (x, h, a, b, hid, f), w2


def _ffn_backward(dxo, saved, g, sc, gt, w1, w3, w2, tag, emit=lambda key, gw: 0.0):
    x, h, a, b, hid, f = saved
    dfs, da, db, dgt = _ffn_bwd_hid(dxo, gt, f, a, b, w2, f"{tag}_bwd_hid")
    gw2 = _mm(_flat(hid), _flat(dfs), ta=True, name=f"{tag}_gw2")
    tok = emit("w2", gw2)
    dx, dsh, dsc, dg = _dh_norm_bwd([da, db], [w1, w3], x, g, sc + tok, dxo, f"{tag}_bwd_dh")
    h2 = _flat(h)
    gw1 = _mm(h2, _flat(da), ta=True, name=f"{tag}_gw1")
    emit("w1", gw1)
    gw3 = _mm(h2, _flat(db), ta=True, name=f"{tag}_gw3")
    emit("w3", gw3)
    return dx, (dsh, dsc, dgt, dg), (gw1, gw3, gw2)


def kernel(x, c, w_ada, b_ada, g_ffn1, w1_a, w3_a, w2_a, g_mix, w_in, conv_w, w_conv_out, a_re, a_im, b_re, b_im, c_re, c_im, log_dt, d_skip, w_glu, w_ssm_out, w_out, g_ffn2, w1_b, w3_b, w2_b, g_final, loss_target, m_w_ada, m_b_ada, m_g_ffn1, m_w1_a, m_w3_a, m_w2_a, m_g_mix, m_w_in, m_conv_w, m_w_conv_out, m_a_re, m_a_im, m_b_re, m_b_im, m_c_re, m_c_im, m_log_dt, m_d_skip, m_w_glu, m_w_ssm_out, m_w_out, m_g_ffn2, m_w1_b, m_w3_b, m_w2_b, m_g_final, v_w_ada, v_b_ada, v_g_ffn1, v_w1_a, v_w3_a, v_w2_a, v_g_mix, v_w_in, v_conv_w, v_w_conv_out, v_a_re, v_a_im, v_b_re, v_b_im, v_c_re, v_c_im, v_log_dt, v_d_skip, v_w_glu, v_w_ssm_out, v_w_out, v_g_ffn2, v_w1_b, v_w3_b, v_w2_b, v_g_final):
    args = dict(locals())
    names = ["w_ada", "b_ada", "g_ffn1", "w1_a", "w3_a", "w2_a", "g_mix", "w_in", "conv_w", "w_conv_out", "a_re",
             "a_im", "b_re", "b_im", "c_re", "c_im", "log_dt", "d_skip", "w_glu", "w_ssm_out", "w_out", "g_ffn2",
             "w1_b", "w3_b", "w2_b", "g_final"]
    bsz, seq, _ = x.shape
    mx, my, mc = _place()
    chip = 2 * mx + my
    dev = 4 * mx + 2 * my + mc

    groups = (BIG[:3], BIG[3:8], BIG[8:])
    wfull = {}

    def shards_bf16(grp):
        return [args[n][0].astype(BF16) for n, _, _, _ in grp]

    def unpack_group(gathered, grp):
        for (n, _, _, split), st in zip(grp, gathered):
            wfull[n] = _full_from_stacked(st, split)

    up_grp, down_grp = groups[0][:2], groups[0][2:]
    unpack_group(_exchange_chips(shards_bf16(up_grp), False, "gather_w_ffn1_up"), up_grp)

    nmod_shard = NMOD * D // N_CHIPS
    c_all = _all_gather8(c.reshape(SUB, -1), "gather_c").reshape(N_DEV * bsz, D)
    b_shard = _select(b_ada.reshape(N_CHIPS, 1, nmod_shard), chip)
    mod_shard = _ada_fwd(c_all, w_ada, b_shard)
    nb = N_DEV * bsz
    cw_pad = jnp.pad(conv_w[0], ((0, SUB - 3), (0, nmod_shard - CW // N_CHIPS)))
    mod_st = _exchange_chips([jnp.concatenate([mod_shard, cw_pad], axis=0)], False, "gather_mod")[0]
    mod_all = mod_st[:, :nb].transpose(1, 0, 2).reshape(N_DEV, bsz, NMOD * D)
    mod = _select(mod_all, dev)

    rest_grp = groups[1] + groups[2]
    down_src, rest_src, mod = lax.optimization_barrier((shards_bf16(down_grp), shards_bf16(rest_grp), mod))

    def gather_begin(srcs, name):
        lands = [_own_slab(jnp.broadcast_to(s[None], (N_CHIPS,) + s.shape), chip) for s in srcs]
        return _exchange_begin(srcs, lands, False, name)

    down_handle = gather_begin(down_src, "gather_w_ffn1_down_start")
    rest_handle = gather_begin(rest_src, "gather_w_rest_start")

    sh1, sc1, gt1, sh2, sc2, gt2, sh3, sc3, gt3 = [mod[:, None, j * D:(j + 1) * D] for j in range(NMOD)]
    convw = mod_st[:, nb:nb + 3, :CW // N_CHIPS].transpose(1, 0, 2).reshape(3, CW)
    convw8 = jnp.pad(convw, ((0, SUB - 3), (0, 0)))

    are, aim = a_re.reshape(1, GP), a_im.reshape(1, GP)
    ldt = jnp.broadcast_to(log_dt.reshape(NG, 1), (NG, NP)).reshape(1, GP)
    bre_t, bim_t = _to_t(b_re[0]), _to_t(b_im[0])
    abr, abi, bbr_t, bbi_t = _ssm_disc(are, aim, ldt, bre_t, bim_t)
    wb = jnp.concatenate([_blockdiag(bbr_t), _blockdiag(bbi_t)], axis=1).astype(BF16)
    wct = jnp.concatenate([_blockdiag(_c_to_t(c_re[0])), -_blockdiag(_c_to_t(c_im[0]))], axis=1).astype(BF16)
    ar8 = jnp.broadcast_to(abr, (SUB, GP))
    ai8 = jnp.broadcast_to(abi, (SUB, GP))

    def late_w2a(hid):
        unpack_group(_exchange_end(down_handle, hid, False, "gather_w_ffn1_down_wait"), down_grp)
        return wfull["w2_a"]

    x1, ffn1_saved, _ = _ffn_forward(
        x, g_ffn1 + (down_handle[4][0:1, 0:1] + rest_handle[4][0:1, 0:1]), sh1, sc1, gt1,
        wfull["w1_a"], wfull["w3_a"], late_w2a, "ffn1")
    unpack_group(_exchange_end(rest_handle, x1, False, "gather_w_rest_wait"), rest_grp)

    h2 = _norm_mod(x1, g_mix, sh2, sc2, "mix_norm")
    h2f = _flat(h2)
    win = wfull["w_in"]
    win1, winu, win3 = win[:, :3 * CW], win[:, 3 * CW:3 * CW + SW], win[:, 3 * CW + SW:]
    p1 = _mm(h2f, win1, name="mix_in1").reshape(bsz, seq, 3 * CW)
    u = _mm(h2f, winu, name="mix_inu").reshape(bsz, seq, SW)
    p3 = _mm(h2f, win3, name="mix_in3").reshape(bsz, seq, 2 * D)

    ya_in = _conv_fwd(p1, convw8, "conv_fwd")
    ya = _mm(_flat(ya_in), wfull["w_conv_out"], name="conv_out").reshape(bsz, seq, D)

    u_tm = u.transpose(1, 0, 2).reshape(seq * bsz, SW)
    xs, y_tm = _ssm_fwd(u_tm, wb, wct, ar8, ai8, bsz, "ssm_fwd")
    ys = y_tm.reshape(seq, bsz, SW).transpose(1, 0, 2)
    s0, z, s1, s2, yb = _ssm_post(ys, u, d_skip, wfull["w_glu"], wfull["w_ssm_out"], "ssm_post")

    merged, mix, x2 = _merge_out(ya, yb, p3, wfull["w_out"], x1, gt2, "merge_out")
    x3, ffn2_saved, _ = _ffn_forward(x2, g_ffn2, sh3, sc3, gt3, wfull["w1_b"], wfull["w3_b"], wfull["w2_b"], "ffn2")

    dx3, lossvec, dgfin = _final_loss(x3, g_final.reshape(1, D), loss_target, "final_loss")
    loss = lax.psum(jnp.sum(lossvec), ("x", "y", "c"))

    gfull = {}
    dx2, (dsh3, dsc3, dgt3, dg3), (gfull["w1_b"], gfull["w3_b"], gfull["w2_b"]) = _ffn_backward(
        dx3, ffn2_saved, g_ffn2, sc3, gt3, wfull["w1_b"], wfull["w3_b"], wfull["w2_b"], "ffn2")

    def stack_group(grp):
        return [_stacked_from_full(gfull[n], rows, cols, split).astype(BF16) for n, rows, cols, split in grp]

    st_ffn2 = stack_group(groups[2])
    h_ffn2 = _exchange_begin(st_ffn2, [_own_slab(s, chip) for s in st_ffn2], True, "scatter_ffn2_start")

    dmix, dya, dyb, dp3, dgt2 = _merge_bwd(dx2, gt2 + h_ffn2[4][0, 0], mix, ya, yb, p3, wfull["w_out"], "merge_bwd")
    gfull["w_out"] = _mm(_flat(merged), _flat(dmix), ta=True, name="gw_out")
    dp1, dconvw8 = _conv_bwd(dya, wfull["w_conv_out"], p1, convw8, "conv_bwd")
    gfull["w_conv_out"] = _mm(_flat(ya_in), _flat(dya), ta=True, name="gw_conv_out")
    ds0, dz, ddskip = _ssm_post_bwd(dyb, s0, z, u, d_skip, wfull["w_glu"], wfull["w_ssm_out"], "ssm_post_bwd")
    gfull["w_ssm_out"] = _mm(_flat(s2), _flat(dyb), ta=True, name="gw_ssm_out")
    gfull["w_glu"] = _mm(_flat(s1), _flat(dz), ta=True, name="gw_glu")
    dy_tm = ds0.transpose(1, 0, 2).reshape(seq * bsz, SW)
    du_tm, dwb, dwct, dar8, dai8 = _ssm_bwd(dy_tm, u_tm, xs, wb, wct, ar8, ai8, bsz, "ssm_bwd")
    du = _du_combine(du_tm.reshape(seq, bsz, SW).transpose(1, 0, 2), ds0, d_skip, "du_combine")
    dx1, dsh2, dsc2, dgmix = _dh_norm_bwd([dp1, du, dp3], [win1, winu, win3], x1, g_mix, sc2, dx2, "mix_bwd_dh")
    gfull["w_in"] = jnp.concatenate([
        _mm(h2f, _flat(dp1), ta=True, name="gw_in1"), _mm(h2f, _flat(du), ta=True, name="gw_inu"),
        _mm(h2f, _flat(dp3), ta=True, name="gw_in3")], axis=1)

    st_mix = stack_group(groups[1])
    h_mix = _exchange_begin(st_mix, [_own_slab(s, chip) for s in st_mix], True, "scatter_mix_start")

    ffn1_names = {"w1": BIG[0], "w3": BIG[1], "w2": BIG[2]}
    ffn1_handles, ffn1_recv = {}, {}

    def emit_ffn1(key, gw):
        n, rows, cols, split = ffn1_names[key]
        st = _stacked_from_full(gw, rows, cols, split).astype(BF16)
        if key == "w3":
            ffn1_recv[n] = _exchange_chips([st], True, "scatter_ffn1_w3")[0]
            return 0.0
        ffn1_handles[n] = _exchange_begin([st], [_own_slab(st, chip)], True, f"scatter_ffn1_{key}_start")
        return ffn1_handles[n][4][0, 0]

    grad_x, (dsh1, dsc1, dgt1, dg1), _ = _ffn_backward(
        dx1, ffn1_saved, g_ffn1, sc1, gt1 + h_mix[4][0, 0], wfull["w1_a"], wfull["w3_a"], wfull["w2_a"], "ffn1",
        emit=emit_ffn1)

    sbw = GP // SSM_SUPER
    d_are, d_aim, d_ldt, d_bre_t, d_bim_t = _ssm_disc_bwd(
        are, aim, ldt, bre_t, bim_t, jnp.sum(dar8, axis=0, keepdims=True), jnp.sum(dai8, axis=0, keepdims=True),
        _blockdiag_extract(dwb[:, :sbw]), _blockdiag_extract(dwb[:, sbw:]))
    d_cre = _c_from_t(_blockdiag_extract(dwct[:, :sbw]))
    d_cim = -_c_from_t(_blockdiag_extract(dwct[:, sbw:]))

    small_parts = [dg1, dgmix, dg3, dgfin, dconvw8[:3], d_are, d_aim, _from_t(d_bre_t), _from_t(d_bim_t), d_cre, d_cim,
                   jnp.sum(d_ldt.reshape(NG, NP), axis=1), ddskip]
    small_sizes = [int(p.size) for p in small_parts]
    n_small = sum(small_sizes)
    n_small_pad = -(-n_small // (SUB * PACK_COLS)) * (SUB * PACK_COLS)
    dmod = jnp.concatenate([dsh1, dsc1, dgt1, dsh2, dsc2, dgt2, dsh3, dsc3, dgt3], axis=2).reshape(bsz * NMOD * D)
    flat = jnp.concatenate([p.reshape(-1) for p in small_parts] + [jnp.zeros((n_small_pad - n_small,), F32), dmod])
    allg = _all_gather8(flat.reshape(SUB, -1), "gather_small").reshape(N_DEV, -1)
    small = _sum_slabs(allg[:, :n_small_pad].reshape(N_DEV, -1, PACK_COLS), "sum_small").reshape(-1)
    sg, o = [], 0
    for p, sz in zip(small_parts, small_sizes):
        sg.append(small[o:o + sz].reshape(p.shape))
        o += sz
    (g_g1, g_gmix, g_g3, g_gfin, g_convw, g_are, g_aim, g_bre, g_bim, g_cre, g_cim, g_ldt, g_dskip) = sg

    dmod_all = allg[:, n_small_pad:].reshape(nb, NMOD * D)
    dmod_shard = _select(dmod_all.reshape(nb, N_CHIPS, nmod_shard).transpose(1, 0, 2), chip)
    g_wada, g_bada = _ada_bwd(c_all, dmod_shard, dmod_all)

    last = ffn1_recv["w3_a"]
    for key in ("w1", "w2"):
        n = ffn1_names[key][0]
        ffn1_recv[n] = _exchange_end(ffn1_handles[n], last, True, f"scatter_ffn1_{key}_wait")[0]
    recv_mix = _exchange_end(h_mix, last, True, "scatter_mix_wait")
    recv_ffn2 = _exchange_end(h_ffn2, last, True, "scatter_ffn2_wait")
    recv = [*[ffn1_recv[n] for n, _, _, _ in groups[0]], *recv_mix, *recv_ffn2]
    recv_sib = _swap_sibling(recv, "swap_sibling")

    grads, deltas, new_m, new_v = {}, {}, {}, {}
    for (n, _, _, _), r_own, r_sib in zip(BIG, recv, recv_sib):
        grads[n], deltas[n], new_m[n], new_v[n] = _adamw_big(
            args[n], args["m_" + n], args["v_" + n], r_own, r_sib, f"adamw_{n}")
    grads["w_ada"] = g_wada[None]
    deltas["w_ada"], new_m["w_ada"], new_v["w_ada"] = _adamw_rows(w_ada, m_w_ada, v_w_ada, g_wada, "adamw_w_ada")

    g_convw_shard = _select(g_convw.reshape(3, N_CHIPS, CW // N_CHIPS).transpose(1, 0, 2), chip)
    small_g = {"b_ada": g_bada, "g_ffn1": g_g1, "g_mix": g_gmix, "g_ffn2": g_g3, "g_final": g_gfin,
               "conv_w": g_convw_shard, "a_re": g_are, "a_im": g_aim, "b_re": g_bre, "b_im": g_bim,
               "c_re": g_cre, "c_im": g_cim, "log_dt": g_ldt, "d_skip": g_dskip}
    small_names = list(small_g)
    sizes = [int(args[n].size) for n in small_names]
    tot = sum(sizes)
    tot_pad = -(-tot // (SUB * PACK_COLS)) * (SUB * PACK_COLS)

    def pack(get):
        return jnp.concatenate([get(n).reshape(-1) for n in small_names] + [jnp.zeros((tot_pad - tot,), F32)]).reshape(
            -1, PACK_COLS)

    res = _adamw_plain(pack(lambda n: args[n]), pack(lambda n: args["m_" + n]), pack(lambda n: args["v_" + n]),
                       pack(lambda n: small_g[n]), "adamw_small")
    o = 0
    for n, sz in zip(small_names, sizes):
        shp = args[n].shape
        grads[n] = small_g[n].reshape(shp)
        deltas[n], new_m[n], new_v[n] = [r.reshape(-1)[o:o + sz].reshape(shp) for r in res]
        o += sz

    return (loss, grad_x, *[grads[n] for n in names], *[deltas[n] for n in names],
            *[new_m[n] for n in names], *[new_v[n] for n in names])
```

```python
import functools
import math

import jax
import jax.numpy as jnp
from jax import lax
from jax.experimental import pallas as pl
from jax.experimental.pallas import tpu as pltpu

F32 = jnp.float32
BF16 = jnp.bfloat16
SDS = jax.ShapeDtypeStruct
MESH = pl.DeviceIdType.MESH

D = 1024
DFF = 2816
CW = 1024
SW = 512
NG, NP, NH = 32, 64, 16
GP = NG * NP
NMOD = 9
EPS = 1e-6
N_CHIPS = 4
N_DEV = 8
SUB = 8
LANE = 128
SSM_SUPER = 4
VMEM_LIMIT = 50 * 1024 * 1024

LR, B1, B2, AEPS, WD, STEP = 0.001, 0.9, 0.999, 1e-08, 0.01, 10
BC1 = 1.0 - B1 ** STEP
BC2 = 1.0 - B2 ** STEP


def _cp(*sem):
    return pltpu.CompilerParams(dimension_semantics=sem or None, vmem_limit_bytes=VMEM_LIMIT)


def _pick_tile(n, cands):
    for t in cands:
        if t <= n and n % t == 0:
            return t
    return n


def _dot(a, b):
    return lax.dot_general(a, b, (((1,), (0,)), ((), ())), preferred_element_type=F32)


def _dot_nt(a, b):
    return lax.dot_general(a, b, (((1,), (1,)), ((), ())), preferred_element_type=F32)


def _dot_tn(a, b):
    return lax.dot_general(a, b, (((0,), (0,)), ((), ())), preferred_element_type=F32)


def _row(tm, width, col=0):
    return pl.BlockSpec((1, tm, width), lambda b, i, *_: (b, i, col))


def _seqvec(width):
    return pl.BlockSpec((1, 1, width), lambda b, *_: (b, 0, 0))


def _full2(shape):
    return pl.BlockSpec(shape, lambda *_: (0, 0))


def _sigmoid(x):
    return jax.nn.sigmoid(x)


def _mm(a, b, *, ta=False, tb=False, out_dtype=F32, name):
    if ta:
        kdim, m = a.shape
    else:
        m, kdim = a.shape
    n = b.shape[0] if tb else b.shape[1]
    tm = _pick_tile(m, (1408, 1024, 512, 256, 128))
    tn = _pick_tile(n, (1408, 1024, 512, 256, 128))
    tk = _pick_tile(kdim, (512, 256, 128))
    nk = kdim // tk

    def body(a_ref, b_ref, o_ref, acc_ref):
        k = pl.program_id(2)

        @pl.when(k == 0)
        def _():
            acc_ref[...] = jnp.zeros_like(acc_ref)

        av = a_ref[...].astype(BF16)
        bv = b_ref[...].astype(BF16)
        dn = (((0 if ta else 1,), (1 if tb else 0,)), ((), ()))
        acc_ref[...] += lax.dot_general(av, bv, dn, preferred_element_type=F32)

        @pl.when(k == nk - 1)
        def _():
            o_ref[...] = acc_ref[...].astype(out_dtype)

    a_spec = pl.BlockSpec((tk, tm), lambda i, j, k: (k, i)) if ta else pl.BlockSpec((tm, tk), lambda i, j, k: (i, k))
    b_spec = pl.BlockSpec((tn, tk), lambda i, j, k: (j, k)) if tb else pl.BlockSpec((tk, tn), lambda i, j, k: (k, j))
    return pl.pallas_call(
        body, name=name, grid=(m // tm, n // tn, nk),
        in_specs=[a_spec, b_spec],
        out_specs=pl.BlockSpec((tm, tn), lambda i, j, k: (i, j)),
        out_shape=SDS((m, n), out_dtype),
        scratch_shapes=[pltpu.VMEM((tm, tn), F32)],
        compiler_params=_cp("parallel", "parallel", "arbitrary"),
    )(a, b)


def _flat(a):
    return a.reshape(-1, a.shape[-1])


def _norm_mod(x, g, sh, sc, name):
    bsz, seq, dm = x.shape
    tm = _pick_tile(seq, (512, 256, 128))

    def body(x_ref, g_ref, sh_ref, sc_ref, o_ref):
        xf = x_ref[0]
        r = lax.rsqrt(jnp.mean(xf * xf, axis=-1, keepdims=True) + EPS)
        hn = xf * r * g_ref[...]
        o_ref[0] = (hn * (1.0 + sc_ref[0]) + sh_ref[0]).astype(BF16)

    return pl.pallas_call(
        body, name=name, grid=(bsz, seq // tm),
        in_specs=[_row(tm, dm), _full2((1, dm)), _seqvec(dm), _seqvec(dm)],
        out_specs=_row(tm, dm), out_shape=SDS((bsz, seq, dm), BF16),
        compiler_params=_cp("parallel", "parallel"),
    )(x, g, sh, sc)


def _swiglu_up(h, w1, w3, name):
    bsz, seq, dm = h.shape
    nf = w1.shape[0]
    tm = _pick_tile(seq, (512, 256, 128))
    tn = _pick_tile(nf, (1408, 512, 256, 128))

    def body(h_ref, w1_ref, w3_ref, a_ref, b_ref, hid_ref):
        hv = h_ref[0]
        a = _dot_nt(hv, w1_ref[...])
        b = _dot_nt(hv, w3_ref[...])
        sg = _sigmoid(a)
        sa = a * sg
        a_ref[0] = (b * (sg * (1.0 + a * (1.0 - sg)))).astype(BF16)
        b_ref[0] = sa.astype(BF16)
        hid_ref[0] = (sa * b).astype(BF16)

    wspec = pl.BlockSpec((tn, dm), lambda n, b, i: (n, 0))
    ospec = pl.BlockSpec((1, tm, tn), lambda n, b, i: (b, i, n))
    shp = SDS((bsz, seq, nf), BF16)
    return pl.pallas_call(
        body, name=name, grid=(nf // tn, bsz, seq // tm),
        in_specs=[pl.BlockSpec((1, tm, dm), lambda n, b, i: (b, i, 0)), wspec, wspec],
        out_specs=[ospec, ospec, ospec], out_shape=[shp, shp, shp],
        compiler_params=_cp("parallel", "parallel", "parallel"),
    )(h, w1, w3)


def _ffn_down(hid, w2, x, gt, name):
    bsz, seq, nf = hid.shape
    dm = w2.shape[1]
    tm = _pick_tile(seq, (512, 256, 128))

    def body(hid_ref, w2_ref, x_ref, gt_ref, f_ref, xo_ref):
        f = _dot(hid_ref[0], w2_ref[...])
        f_ref[0] = f
        xo_ref[0] = x_ref[0] + 0.5 * gt_ref[0] * f

    shp = SDS((bsz, seq, dm), F32)
    return pl.pallas_call(
        body, name=name, grid=(bsz, seq // tm),
        in_specs=[_row(tm, nf), _full2((nf, dm)), _row(tm, dm), _seqvec(dm)],
        out_specs=[_row(tm, dm), _row(tm, dm)], out_shape=[shp, shp],
        compiler_params=_cp("parallel", "parallel"),
    )(hid, w2, x, gt)


def _ffn_bwd_hid(dxo, gt, f, a, b, w2, name):
    bsz, seq, dm = dxo.shape
    nf = a.shape[2]
    tm = _pick_tile(seq, (512, 256, 128))
    tn = _pick_tile(nf, (1408, 512, 256, 128))

    def body(dxo_ref, gt_ref, f_ref, a_ref, b_ref, w2_ref, dfs_ref, da_ref, db_ref, dgt_ref):
        i = pl.program_id(1)
        n = pl.program_id(2)

        @pl.when(n == 0)
        def _():
            dxo = dxo_ref[0]
            dfs_ref[0] = (0.5 * gt_ref[0] * dxo).astype(BF16)
            part = jnp.sum(0.5 * dxo * f_ref[0], axis=0, keepdims=True)

            @pl.when(i == 0)
            def _():
                dgt_ref[0] = part

            @pl.when(i > 0)
            def _():
                dgt_ref[0] += part

        dhid = _dot_nt(dfs_ref[0], w2_ref[pl.ds(pl.multiple_of(n * tn, tn), tn), :])
        da_ref[0] = (dhid * a_ref[0].astype(F32)).astype(BF16)
        db_ref[0] = (dhid * b_ref[0].astype(F32)).astype(BF16)

    hspec = pl.BlockSpec((1, tm, tn), lambda b, i, n: (b, i, n))
    return pl.pallas_call(
        body, name=name, grid=(bsz, seq // tm, nf // tn),
        in_specs=[_row(tm, dm), _seqvec(dm), _row(tm, dm), hspec, hspec, _full2((nf, dm))],
        out_specs=[_row(tm, dm), hspec, hspec, _seqvec(dm)],
        out_shape=[SDS((bsz, seq, dm), BF16), SDS((bsz, seq, nf), BF16), SDS((bsz, seq, nf), BF16),
                   SDS((bsz, 1, dm), F32)],
        compiler_params=_cp("arbitrary", "arbitrary", "arbitrary"),
    )(dxo, gt, f, a, b, w2)


def _dh_norm_bwd(pieces, weights, x, g, sc, dxo, name, transposed=False):
    bsz, seq, dm = x.shape
    tm = _pick_tile(seq, (512, 256, 128))
    npc = len(pieces)
    dot = _dot if transposed else _dot_nt

    def body(*refs):
        p_refs = refs[:npc]
        w_hbm = refs[npc:2 * npc]
        x_ref, g_ref, sc_ref, dxo_ref, dx_ref, dsh_ref, dsc_ref, dg_ref = refs[2 * npc:2 * npc + 8]
        w_refs = refs[2 * npc + 8:]
        b = pl.program_id(0)
        i = pl.program_id(1)

        @pl.when((i == 0) & (b == 0))
        def _():
            for src, dst in zip(w_hbm, w_refs):
                pltpu.sync_copy(src, dst)

        dh = dot(p_refs[0][0], w_refs[0][...])
        for j in range(1, npc):
            dh = dh + dot(p_refs[j][0], w_refs[j][...])
        xf = x_ref[0]
        gv = g_ref[...]
        r = lax.rsqrt(jnp.mean(xf * xf, axis=-1, keepdims=True) + EPS)
        xhat = xf * r
        dhn = dh * (1.0 + sc_ref[0])
        p_sh = jnp.sum(dh, axis=0, keepdims=True)
        p_sc = jnp.sum(dh * (xhat * gv), axis=0, keepdims=True)
        p_g = jnp.sum(dhn * xhat, axis=0, keepdims=True)
        dxh = dhn * gv
        dx_ref[0] = dxo_ref[0] + r * (dxh - xhat * jnp.mean(dxh * xhat, axis=-1, keepdims=True))

        @pl.when(i == 0)
        def _():
            dsh_ref[0] = p_sh
            dsc_ref[0] = p_sc

        @pl.when(i > 0)
        def _():
            dsh_ref[0] += p_sh
            dsc_ref[0] += p_sc

        @pl.when((i == 0) & (b == 0))
        def _():
            dg_ref[...] = p_g

        @pl.when((i > 0) | (b > 0))
        def _():
            dg_ref[...] += p_g

    return pl.pallas_call(
        body, name=name, grid=(bsz, seq // tm),
        in_specs=[_row(tm, p.shape[2]) for p in pieces] + [pl.BlockSpec(memory_space=pl.ANY)] * npc + [
            _row(tm, dm), _full2((1, dm)), _seqvec(dm), _row(tm, dm)],
        out_specs=[_row(tm, dm), _seqvec(dm), _seqvec(dm), _full2((1, dm))],
        out_shape=[SDS((bsz, seq, dm), F32), SDS((bsz, 1, dm), F32), SDS((bsz, 1, dm), F32), SDS((1, dm), F32)],
        scratch_shapes=[pltpu.VMEM(w.shape, w.dtype) for w in weights],
        compiler_params=_cp("arbitrary", "arbitrary"),
    )(*pieces, *weights, x, g, sc, dxo)


HALO = 16


def _conv_core(gc, v, gch, vh, w, first):
    cv = gc * v
    halo = jnp.where(first, 0.0, gch * vh)
    ext = jnp.concatenate([halo, cv], axis=0)
    cv1 = pltpu.roll(ext, 1, 0)[HALO:]
    cv2 = pltpu.roll(ext, 2, 0)[HALO:]
    conv = w[0:1] * cv2 + w[1:2] * cv1 + w[2:3] * cv
    return cv, cv1, cv2, conv


def _prev_halo(tm, col):
    return pl.BlockSpec((1, HALO, CW), lambda b, i, *_: (b, jnp.maximum(i * (tm // HALO) - 1, 0), col))


def _next_halo(tm, seq, col):
    return pl.BlockSpec((1, HALO, CW), lambda b, i, *_: (b, jnp.minimum((i + 1) * (tm // HALO), seq // HALO - 1), col))


def _conv_fwd(p1, convw8, name):
    bsz, seq, _ = p1.shape
    tm = _pick_tile(seq, (512, 256, 128))

    def body(gb_ref, gc_ref, v_ref, gch_ref, vh_ref, w_ref, o_ref):
        first = pl.program_id(1) == 0
        _, _, _, conv = _conv_core(gc_ref[0], v_ref[0], gch_ref[0], vh_ref[0], w_ref[...], first)
        o_ref[0] = (gb_ref[0] * conv).astype(BF16)

    return pl.pallas_call(
        body, name=name, grid=(bsz, seq // tm),
        in_specs=[_row(tm, CW, 0), _row(tm, CW, 1), _row(tm, CW, 2), _prev_halo(tm, 1), _prev_halo(tm, 2),
                  _full2((8, CW))],
        out_specs=_row(tm, CW), out_shape=SDS((bsz, seq, CW), BF16),
        compiler_params=_cp("parallel", "parallel"),
    )(p1, p1, p1, p1, p1, convw8)


def _conv_bwd(dya, wco, p1, convw8, name):
    bsz, seq, _ = p1.shape
    tm = _pick_tile(seq, (512, 256, 128))
    nt = seq // tm
    ext_rows = tm + HALO

    def body(dya_ref, dyan_ref, wco_ref, gb_ref, gbn_ref, gc_ref, v_ref, gch_ref, vh_ref, w_ref, dp_ref, dw_ref):
        b = pl.program_id(0)
        i = pl.program_id(1)
        w = w_ref[...]
        cv, cv1, cv2, conv = _conv_core(gc_ref[0], v_ref[0], gch_ref[0], vh_ref[0], w, i == 0)
        dya_ext = jnp.concatenate([dya_ref[0], dyan_ref[0]], axis=0)
        dyain_ext = _dot_nt(dya_ext, wco_ref[...])
        gb = gb_ref[0]
        gb_ext = jnp.concatenate([gb, gbn_ref[0]], axis=0)
        rows = lax.broadcasted_iota(jnp.int32, (ext_rows, 1), 0)
        dconv_ext = jnp.where((rows < tm) | (i < nt - 1), dyain_ext * gb_ext, 0.0)
        dconv = dconv_ext[:tm]
        dconv1 = pltpu.roll(dconv_ext, ext_rows - 1, 0)[:tm]
        dconv2 = pltpu.roll(dconv_ext, ext_rows - 2, 0)[:tm]
        dcv = w[2:3] * dconv + w[1:2] * dconv1 + w[0:1] * dconv2
        dp_ref[0, :, 0:CW] = (dyain_ext[:tm] * conv).astype(BF16)
        dp_ref[0, :, CW:2 * CW] = (dcv * v_ref[0]).astype(BF16)
        dp_ref[0, :, 2 * CW:3 * CW] = (dcv * gc_ref[0]).astype(BF16)
        g0 = jnp.sum(dconv * cv2, axis=0, keepdims=True)
        g1 = jnp.sum(dconv * cv1, axis=0, keepdims=True)
        g2 = jnp.sum(dconv * cv, axis=0, keepdims=True)
        upd = jnp.concatenate([g0, g1, g2, jnp.zeros((5, CW), F32)], axis=0)

        @pl.when((i == 0) & (b == 0))
        def _():
            dw_ref[...] = upd

        @pl.when((i > 0) | (b > 0))
        def _():
            dw_ref[...] += upd

    return pl.pallas_call(
        body, name=name, grid=(bsz, seq // tm),
        in_specs=[_row(tm, CW), _next_halo(tm, seq, 0), _full2((CW, D)),
                  _row(tm, CW, 0), _next_halo(tm, seq, 0), _row(tm, CW, 1), _row(tm, CW, 2),
                  _prev_halo(tm, 1), _prev_halo(tm, 2), _full2((8, CW))],
        out_specs=[_row(tm, 3 * CW), _full2((8, CW))],
        out_shape=[SDS((bsz, seq, 3 * CW), BF16), SDS((8, CW), F32)],
        compiler_params=_cp("arbitrary", "arbitrary"),
    )(dya, dya, wco, p1, p1, p1, p1, p1, p1, convw8)


def _disc(are, aim, ldt, bre, bim):
    dt = jnp.exp(ldt)
    mag = jnp.exp(are * dt)
    ang = aim * dt
    abr = mag * jnp.cos(ang)
    abi = mag * jnp.sin(ang)
    nr = abr - 1.0
    den = are * are + aim * aim
    cr = (nr * are + abi * aim) / den
    ci = (abi * are - nr * aim) / den
    return abr, abi, cr * bre - ci * bim, cr * bim + ci * bre


def _ssm_disc(are, aim, ldt, bre_t, bim_t):
    def body(are_ref, aim_ref, ldt_ref, bre_ref, bim_ref, abr_ref, abi_ref, bbr_ref, bbi_ref):
        abr, abi, bbr, bbi = _disc(are_ref[...], aim_ref[...], ldt_ref[...], bre_ref[...], bim_ref[...])
        abr_ref[...] = abr
        abi_ref[...] = abi
        bbr_ref[...] = bbr
        bbi_ref[...] = bbi

    v1, vh = SDS((1, GP), F32), SDS((NH, GP), F32)
    return pl.pallas_call(body, name="ssm_disc", out_shape=[v1, v1, vh, vh], compiler_params=_cp())(
        are, aim, ldt, bre_t, bim_t)


def _ssm_disc_bwd(are, aim, ldt, bre_t, bim_t, dabr, dabi, dbbr, dbbi):
    def body(are_ref, aim_ref, ldt_ref, bre_ref, bim_ref, g0, g1, g2, g3, o0, o1, o2, o3, o4):
        prim = (are_ref[...], aim_ref[...], ldt_ref[...], bre_ref[...], bim_ref[...])
        _, vjp = jax.vjp(_disc, *prim)
        d_are, d_aim, d_ldt, d_bre, d_bim = vjp((g0[...], g1[...], g2[...], g3[...]))
        o0[...] = d_are
        o1[...] = d_aim
        o2[...] = d_ldt
        o3[...] = d_bre
        o4[...] = d_bim

    v1, vh = SDS((1, GP), F32), SDS((NH, GP), F32)
    return pl.pallas_call(body, name="ssm_disc_bwd", out_shape=[v1, v1, v1, vh, vh], compiler_params=_cp())(
        are, aim, ldt, bre_t, bim_t, dabr, dabi, dbbr, dbbi)


def _scan_chunk(buf_ref, nt, bsz, ar, ai, init_r, init_i, reverse):
    nsub = SUB // bsz
    row = lax.broadcasted_iota(jnp.int32, (SUB, GP), 0)
    shift = ((SUB - bsz) if reverse else bsz) % SUB
    order = list(range(nsub - 1, -1, -1)) if reverse else list(range(nsub))

    def step(j, carry):
        pr, pi = carry
        jj = (nt - 1 - j) if reverse else j
        off = pl.multiple_of(jj * SUB, SUB)
        br = buf_ref[pl.ds(off, SUB), 0:GP]
        bi = buf_ref[pl.ds(off, SUB), GP:2 * GP]
        nr, ni = pr, pi
        for s in order:
            sr, si = nr, ni
            if shift:
                sr = pltpu.roll(sr, shift, 0)
                si = pltpu.roll(si, shift, 0)
            cr = ar * sr - ai * si + br
            ci = ar * si + ai * sr + bi
            if nsub == 1:
                nr, ni = cr, ci
            else:
                m = (row >= s * bsz) & (row < (s + 1) * bsz)
                nr = jnp.where(m, cr, nr)
                ni = jnp.where(m, ci, ni)
        buf_ref[pl.ds(off, SUB), 0:GP] = nr
        buf_ref[pl.ds(off, SUB), GP:2 * GP] = ni
        return nr, ni

    return lax.fori_loop(0, nt, step, (init_r, init_i))


def _ssm_chunk_rows(total_rows, bsz):
    return min(total_rows, 64 * bsz)


def _ssm_fwd(u_tm, wb, wct, ar8, ai8, bsz, name):
    rt = u_tm.shape[0]
    r = _ssm_chunk_rows(rt, bsz)
    nt = r // SUB

    def body(u_ref, wb_hbm, wct_hbm, ar_ref, ai_ref, x_ref, y_ref, wb_ref, wct_ref, st_ref):
        @pl.when(pl.program_id(0) == 0)
        def _():
            pltpu.sync_copy(wb_hbm, wb_ref)
            pltpu.sync_copy(wct_hbm, wct_ref)
            st_ref[...] = jnp.zeros_like(st_ref)

        x_ref[...] = _dot(u_ref[...].astype(BF16), wb_ref[...])
        fr, fi = _scan_chunk(x_ref, nt, bsz, ar_ref[...], ai_ref[...], st_ref[:, 0:GP], st_ref[:, GP:2 * GP], False)
        st_ref[:, 0:GP] = fr
        st_ref[:, GP:2 * GP] = fi
        y_ref[...] = _dot_nt(x_ref[...].astype(BF16), wct_ref[...])

    anyspec = pl.BlockSpec(memory_space=pl.ANY)
    return pl.pallas_call(
        body, name=name, grid=(rt // r,),
        in_specs=[pl.BlockSpec((r, SW), lambda i: (i, 0)), anyspec, anyspec, _full2((SUB, GP)), _full2((SUB, GP))],
        out_specs=[pl.BlockSpec((r, 2 * GP), lambda i: (i, 0)), pl.BlockSpec((r, SW), lambda i: (i, 0))],
        out_shape=[SDS((rt, 2 * GP), F32), SDS((rt, SW), F32)],
        scratch_shapes=[pltpu.VMEM((SW, 2 * GP), BF16), pltpu.VMEM((SW, 2 * GP), BF16), pltpu.VMEM((SUB, 2 * GP), F32)],
        compiler_params=_cp("arbitrary"),
    )(u_tm, wb, wct, ar8, ai8)


def _ssm_bwd(dy_tm, u_tm, xs, wb, wct, ar8, ai8, bsz, name):
    rt = u_tm.shape[0]
    r = _ssm_chunk_rows(rt, bsz)
    nt = r // SUB
    nc = rt // r
    ub = SW // SSM_SUPER
    sb = GP // SSM_SUPER

    def body(dy_ref, u_ref, x_ref, xh_ref, wb_hbm, wct_hbm, ar_ref, ai_ref,
             du_ref, dwb_hbm, dwct_hbm, dar_ref, dai_ref, wb_ref, wct_ref, g_ref, st_ref, awb_ref, awct_ref):
        i = pl.program_id(0)

        @pl.when(i == 0)
        def _():
            pltpu.sync_copy(wb_hbm, wb_ref)
            pltpu.sync_copy(wct_hbm, wct_ref)
            st_ref[...] = jnp.zeros_like(st_ref)
            awb_ref[...] = jnp.zeros_like(awb_ref)
            awct_ref[...] = jnp.zeros_like(awct_ref)
            dar_ref[...] = jnp.zeros_like(dar_ref)
            dai_ref[...] = jnp.zeros_like(dai_ref)

        dyb = dy_ref[...].astype(BF16)
        g_ref[...] = _dot(dyb, wct_ref[...])
        ar = ar_ref[...]
        ai = ai_ref[...]
        fr, fi = _scan_chunk(g_ref, nt, bsz, ar, -ai, st_ref[:, 0:GP], st_ref[:, GP:2 * GP], True)
        st_ref[:, 0:GP] = fr
        st_ref[:, GP:2 * GP] = fi

        gb = g_ref[...].astype(BF16)
        du_ref[...] = _dot_nt(gb, wb_ref[...])
        ub16 = u_ref[...].astype(BF16)
        xb16 = x_ref[...].astype(BF16)
        for s in range(SSM_SUPER):
            us = ub16[:, s * ub:(s + 1) * ub]
            ds = dyb[:, s * ub:(s + 1) * ub]
            for half in range(2):
                cols = slice(half * GP + s * sb, half * GP + (s + 1) * sb)
                ocols = slice(half * sb, (half + 1) * sb)
                awb_ref[s * ub:(s + 1) * ub, ocols] += _dot_tn(us, gb[:, cols])
                awct_ref[s * ub:(s + 1) * ub, ocols] += _dot_tn(ds, xb16[:, cols])

        gr = g_ref[:, 0:GP]
        gi = g_ref[:, GP:2 * GP]
        xsr = pltpu.roll(x_ref[:, 0:GP], bsz, 0)
        xsi = pltpu.roll(x_ref[:, GP:2 * GP], bsz, 0)
        inner = lax.broadcasted_iota(jnp.int32, (r, 1), 0) >= bsz
        t_r = jnp.where(inner, gr * xsr + gi * xsi, 0.0)
        t_i = jnp.where(inner, gi * xsr - gr * xsi, 0.0)
        acc_r = jnp.sum(t_r.reshape(nt, SUB, GP), axis=0)
        acc_i = jnp.sum(t_i.reshape(nt, SUB, GP), axis=0)
        hr = xh_ref[:, 0:GP]
        hi = xh_ref[:, GP:2 * GP]
        if bsz % SUB:
            hr = pltpu.roll(hr, bsz, 0)
            hi = pltpu.roll(hi, bsz, 0)
        edge = (lax.broadcasted_iota(jnp.int32, (SUB, 1), 0) < bsz) & (i < nc - 1)
        g0r = g_ref[0:SUB, 0:GP]
        g0i = g_ref[0:SUB, GP:2 * GP]
        dar_ref[...] += acc_r + jnp.where(edge, g0r * hr + g0i * hi, 0.0)
        dai_ref[...] += acc_i + jnp.where(edge, g0i * hr - g0r * hi, 0.0)

        @pl.when(i == nc - 1)
        def _():
            pltpu.sync_copy(awb_ref, dwb_hbm)
            pltpu.sync_copy(awct_ref, dwct_hbm)

    anyspec = pl.BlockSpec(memory_space=pl.ANY)
    rev = lambda i: (nc - 1 - i, 0)
    wshape = (SW, 2 * sb)
    return pl.pallas_call(
        body, name=name, grid=(nc,),
        in_specs=[pl.BlockSpec((r, SW), rev), pl.BlockSpec((r, SW), rev), pl.BlockSpec((r, 2 * GP), rev),
                  pl.BlockSpec((SUB, 2 * GP), lambda i: (jnp.maximum((nc - 1 - i) * nt - 1, 0), 0)),
                  anyspec, anyspec, _full2((SUB, GP)), _full2((SUB, GP))],
        out_specs=[pl.BlockSpec((r, SW), rev), anyspec, anyspec, _full2((SUB, GP)), _full2((SUB, GP))],
        out_shape=[SDS((rt, SW), F32), SDS(wshape, F32), SDS(wshape, F32), SDS((SUB, GP), F32), SDS((SUB, GP), F32)],
        scratch_shapes=[pltpu.VMEM((SW, 2 * GP), BF16), pltpu.VMEM((SW, 2 * GP), BF16),
                        pltpu.VMEM((r, 2 * GP), F32), pltpu.VMEM((SUB, 2 * GP), F32),
                        pltpu.VMEM(wshape, F32), pltpu.VMEM(wshape, F32)],
        compiler_params=_cp("arbitrary"),
    )(dy_tm, u_tm, xs, xs, wb, wct, ar8, ai8)


GELU_C = math.sqrt(2.0 / math.pi)


def _gelu(x):
    return 0.5 * x * (1.0 + jnp.tanh(GELU_C * (x + 0.044715 * x * x * x)))


def _gelu_grad(x):
    th = jnp.tanh(GELU_C * (x + 0.044715 * x * x * x))
    return 0.5 * (1.0 + th) + 0.5 * x * (1.0 - th * th) * GELU_C * (1.0 + 3.0 * 0.044715 * x * x)


def _ssm_post(ys, u, dskip, wglu, wso, name):
    bsz, seq, _ = ys.shape
    tm = _pick_tile(seq, (512, 256, 128))

    def body(ys_ref, u_ref, d_ref, wg_ref, wo_ref, s0_ref, z_ref, s1_ref, s2_ref, yb_ref):
        s0 = ys_ref[0] + d_ref[...] * u_ref[0]
        s1 = _gelu(s0)
        s1b = s1.astype(BF16)
        z = _dot(s1b, wg_ref[...])
        s2b = (s1 * _sigmoid(z)).astype(BF16)
        s0_ref[0] = s0
        z_ref[0] = z
        s1_ref[0] = s1b
        s2_ref[0] = s2b
        yb_ref[0] = _dot(s2b, wo_ref[...])

    return pl.pallas_call(
        body, name=name, grid=(bsz, seq // tm),
        in_specs=[_row(tm, SW), _row(tm, SW), _full2((1, SW)), _full2((SW, SW)), _full2((SW, D))],
        out_specs=[_row(tm, SW), _row(tm, SW), _row(tm, SW), _row(tm, SW), _row(tm, D)],
        out_shape=[SDS((bsz, seq, SW), F32), SDS((bsz, seq, SW), F32), SDS((bsz, seq, SW), BF16),
                   SDS((bsz, seq, SW), BF16), SDS((bsz, seq, D), F32)],
        compiler_params=_cp("parallel", "parallel"),
    )(ys, u, dskip, wglu, wso)


def _ssm_post_bwd(dyb, s0, z, u, dskip, wglu, wso, name):
    bsz, seq, _ = s0.shape
    tm = _pick_tile(seq, (512, 256, 128))

    def body(dyb_ref, s0_ref, z_ref, u_ref, wg_ref, wo_ref, ds0_ref, dz_ref, dd_ref):
        b = pl.program_id(0)
        i = pl.program_id(1)
        ds2 = _dot_nt(dyb_ref[0], wo_ref[...])
        s0 = s0_ref[0]
        s1 = _gelu(s0)
        sg = _sigmoid(z_ref[0])
        dz = ds2 * s1 * sg * (1.0 - sg)
        dzb = dz.astype(BF16)
        ds1 = ds2 * sg + _dot_nt(dzb, wg_ref[...])
        ds0 = ds1 * _gelu_grad(s0)
        ds0_ref[0] = ds0
        dz_ref[0] = dzb
        part = jnp.sum(ds0 * u_ref[0], axis=0, keepdims=True)

        @pl.when((i == 0) & (b == 0))
        def _():
            dd_ref[...] = part

        @pl.when((i > 0) | (b > 0))
        def _():
            dd_ref[...] += part

    del dskip
    return pl.pallas_call(
        body, name=name, grid=(bsz, seq // tm),
        in_specs=[_row(tm, D), _row(tm, SW), _row(tm, SW), _row(tm, SW), _full2((SW, SW)), _full2((SW, D))],
        out_specs=[_row(tm, SW), _row(tm, SW), _full2((1, SW))],
        out_shape=[SDS((bsz, seq, SW), F32), SDS((bsz, seq, SW), BF16), SDS((1, SW), F32)],
        compiler_params=_cp("arbitrary", "arbitrary"),
    )(dyb, s0, z, u, wglu, wso)


def _du_combine(du_ssm, ds0, dskip, name):
    bsz, seq, _ = ds0.shape
    tm = _pick_tile(seq, (512, 256, 128))

    def body(a_ref, b_ref, d_ref, o_ref):
        o_ref[0] = (a_ref[0] + b_ref[0] * d_ref[...]).astype(BF16)

    return pl.pallas_call(
        body, name=name, grid=(bsz, seq // tm),
        in_specs=[_row(tm, SW), _row(tm, SW), _full2((1, SW))],
        out_specs=_row(tm, SW), out_shape=SDS((bsz, seq, SW), BF16),
        compiler_params=_cp("parallel", "parallel"),
    )(du_ssm, ds0, dskip)


def _merge_out(ya, yb, p3, wout, x1, gt, name):
    bsz, seq, _ = ya.shape
    tm = _pick_tile(seq, (512, 256, 128))

    def body(ya_ref, yb_ref, ga_ref, gbb_ref, w_ref, x_ref, gt_ref, mg_ref, mix_ref, xo_ref):
        merged = (_sigmoid(ga_ref[0]) * ya_ref[0] + _sigmoid(gbb_ref[0]) * yb_ref[0]).astype(BF16)
        mix = _dot(merged, w_ref[...])
        mg_ref[0] = merged
        mix_ref[0] = mix
        xo_ref[0] = x_ref[0] + gt_ref[0] * mix

    return pl.pallas_call(
        body, name=name, grid=(bsz, seq // tm),
        in_specs=[_row(tm, D), _row(tm, D), _row(tm, D, 0), _row(tm, D, 1), _full2((D, D)), _row(tm, D), _seqvec(D)],
        out_specs=[_row(tm, D), _row(tm, D), _row(tm, D)],
        out_shape=[SDS((bsz, seq, D), BF16), SDS((bsz, seq, D), F32), SDS((bsz, seq, D), F32)],
        compiler_params=_cp("parallel", "parallel"),
    )(ya, yb, p3, p3, wout, x1, gt)


def _merge_bwd(dx2, gt, mix, ya, yb, p3, wout, name):
    bsz, seq, _ = ya.shape
    tm = _pick_tile(seq, (512, 256, 128))

    def body(dx_ref, gt_ref, mix_ref, ya_ref, yb_ref, ga_ref, gbb_ref, w_ref, dmix_ref, dya_ref, dyb_ref, dp_ref, dgt_ref):
        i = pl.program_id(1)
        dx = dx_ref[0]
        dmix = (gt_ref[0] * dx).astype(BF16)
        dmix_ref[0] = dmix
        part = jnp.sum(dx * mix_ref[0], axis=0, keepdims=True)

        @pl.when(i == 0)
        def _():
            dgt_ref[0] = part

        @pl.when(i > 0)
        def _():
            dgt_ref[0] += part

        dmg = _dot_nt(dmix, w_ref[...])
        sa = _sigmoid(ga_ref[0])
        sb = _sigmoid(gbb_ref[0])
        dya_ref[0] = (dmg * sa).astype(BF16)
        dyb_ref[0] = (dmg * sb).astype(BF16)
        dp_ref[0, :, 0:D] = (dmg * ya_ref[0] * sa * (1.0 - sa)).astype(BF16)
        dp_ref[0, :, D:2 * D] = (dmg * yb_ref[0] * sb * (1.0 - sb)).astype(BF16)

    bshape = SDS((bsz, seq, D), BF16)
    return pl.pallas_call(
        body, name=name, grid=(bsz, seq // tm),
        in_specs=[_row(tm, D), _seqvec(D), _row(tm, D), _row(tm, D), _row(tm, D), _row(tm, D, 0), _row(tm, D, 1),
                  _full2((D, D))],
        out_specs=[_row(tm, D), _row(tm, D), _row(tm, D), _row(tm, 2 * D), _seqvec(D)],
        out_shape=[bshape, bshape, bshape, SDS((bsz, seq, 2 * D), BF16), SDS((bsz, 1, D), F32)],
        compiler_params=_cp("arbitrary", "arbitrary"),
    )(dx2, gt, mix, ya, yb, p3, p3, wout)


def _final_loss(x3, gfin, target, name):
    bsz, seq, dm = x3.shape
    tm = _pick_tile(seq, (512, 256, 128))

    def body(x_ref, g_ref, t_ref, dx_ref, loss_ref, dg_ref):
        b = pl.program_id(0)
        i = pl.program_id(1)
        xf = x_ref[0]
        gv = g_ref[...]
        r = lax.rsqrt(jnp.mean(xf * xf, axis=-1, keepdims=True) + EPS)
        xhat = xf * r
        e = xhat * gv - t_ref[0]
        dy = e * (1.0 / dm)
        dxh = dy * gv
        dx_ref[0] = r * (dxh - xhat * jnp.mean(dxh * xhat, axis=-1, keepdims=True))
        p_l = jnp.sum(e * e, axis=0, keepdims=True) * (0.5 / dm)
        p_g = jnp.sum(dy * xhat, axis=0, keepdims=True)

        @pl.when((i == 0) & (b == 0))
        def _():
            loss_ref[...] = p_l
            dg_ref[...] = p_g

        @pl.when((i > 0) | (b > 0))
        def _():
            loss_ref[...] += p_l
            dg_ref[...] += p_g

    return pl.pallas_call(
        body, name=name, grid=(bsz, seq // tm),
        in_specs=[_row(tm, dm), _full2((1, dm)), _row(tm, dm)],
        out_specs=[_row(tm, dm), _full2((1, dm)), _full2((1, dm))],
        out_shape=[SDS((bsz, seq, dm), F32), SDS((1, dm), F32), SDS((1, dm), F32)],
        compiler_params=_cp("arbitrary", "arbitrary"),
    )(x3, gfin, target)


def _ada_fwd(c_all, w_shard, b_shard):
    nb = c_all.shape[0]
    n = w_shard.shape[2]

    def body(c_ref, w_ref, b_ref, o_ref):
        cv = c_ref[...]
        cond = (cv * _sigmoid(cv)).astype(BF16)
        o_ref[...] = _dot(cond, w_ref[0].astype(BF16)) + b_ref[...]

    return pl.pallas_call(body, name="ada_fwd", out_shape=SDS((nb, n), F32), compiler_params=_cp())(
        c_all, w_shard, b_shard)


def _ada_bwd(c_all, dmod_shard, dmod_all):
    n = dmod_shard.shape[1]

    def body(c_ref, ds_ref, da_ref, gw_ref, gb_ref):
        cv = c_ref[...]
        cond = (cv * _sigmoid(cv)).astype(BF16)
        gw_ref[...] = _dot_tn(cond, ds_ref[...].astype(BF16))
        gb_ref[...] = jnp.sum(da_ref[...], axis=0, keepdims=True)

    return pl.pallas_call(
        body, name="ada_bwd", out_shape=[SDS((D, n), F32), SDS((1, dmod_all.shape[1]), F32)], compiler_params=_cp(),
    )(c_all, dmod_shard, dmod_all)


def _adamw_math(w, g, m, v):
    m = B1 * m + (1.0 - B1) * g
    v = B2 * v + (1.0 - B2) * (g * g)
    delta = -LR * ((m / BC1) / (jnp.sqrt(v / BC2) + AEPS) + WD * w)
    return delta, m, v


def _adamw_big(w, m, v, recv_own, recv_sib, name):
    _, rows, cols = w.shape
    tr = _pick_tile(rows, tuple(t for t in (512, 256, 128, 64, 32, 16, 8) if t * cols <= 96 * 1024))

    def body(w_ref, m_ref, v_ref, a_ref, b_ref, g_ref, d_ref, mo_ref, vo_ref):
        def chip_sum(r):
            acc = r[0].astype(F32)
            for k in range(1, N_CHIPS):
                acc = acc + r[k].astype(F32)
            return acc

        g = chip_sum(a_ref) + chip_sum(b_ref)
        delta, mn, vn = _adamw_math(w_ref[0], g, m_ref[0], v_ref[0])
        g_ref[0] = g
        d_ref[0] = delta
        mo_ref[0] = mn
        vo_ref[0] = vn

    own = pl.BlockSpec((1, tr, cols), lambda i: (0, i, 0))
    rspec = pl.BlockSpec((N_CHIPS, tr, cols), lambda i: (0, i, 0))
    shp = SDS(w.shape, F32)
    return pl.pallas_call(
        body, name=name, grid=(rows // tr,),
        in_specs=[own, own, own, rspec, rspec], out_specs=[own, own, own, own], out_shape=[shp, shp, shp, shp],
        compiler_params=_cp("parallel"),
    )(w, m, v, recv_own, recv_sib)


def _adamw_plain(w, m, v, g, name):
    def body(w_ref, m_ref, v_ref, g_ref, d_ref, mo_ref, vo_ref):
        delta, mn, vn = _adamw_math(w_ref[...], g_ref[...], m_ref[...], v_ref[...])
        d_ref[...] = delta
        mo_ref[...] = mn
        vo_ref[...] = vn

    shp = SDS(w.shape, F32)
    return pl.pallas_call(body, name=name, out_shape=[shp, shp, shp], compiler_params=_cp())(w, m, v, g)


def _adamw_rows(w, m, v, g, name):
    _, rows, cols = w.shape
    tr = _pick_tile(rows, (128, 64, 32, 16, 8))

    def body(w_ref, m_ref, v_ref, g_ref, d_ref, mo_ref, vo_ref):
        delta, mn, vn = _adamw_math(w_ref[0], g_ref[...], m_ref[0], v_ref[0])
        d_ref[0] = delta
        mo_ref[0] = mn
        vo_ref[0] = vn

    spec = pl.BlockSpec((1, tr, cols), lambda i: (0, i, 0))
    shp = SDS(w.shape, F32)
    return pl.pallas_call(
        body, name=name, grid=(rows // tr,), in_specs=[spec] * 3 + [pl.BlockSpec((tr, cols), lambda i: (i, 0))],
        out_specs=[spec] * 3, out_shape=[shp] * 3, compiler_params=_cp("parallel"),
    )(w, m, v, g)


def _sum_slabs(r, name):
    n, rows, cols = r.shape
    tr = _pick_tile(rows, (256, 128, 64))

    def body(r_ref, o_ref):
        acc = r_ref[0].astype(F32)
        for j in range(1, n):
            acc = acc + r_ref[j].astype(F32)
        o_ref[...] = acc

    return pl.pallas_call(
        body, name=name, grid=(rows // tr,),
        in_specs=[pl.BlockSpec((n, tr, cols), lambda i: (0, i, 0))],
        out_specs=pl.BlockSpec((tr, cols), lambda i: (i, 0)), out_shape=SDS((rows, cols), F32),
        compiler_params=_cp("parallel"),
    )(r)


def _place():
    return lax.axis_index("x"), lax.axis_index("y"), lax.axis_index("c")


def _all_gather8(blk, name):
    m_per, n = blk.shape

    def body(x_ref, out_ref, send_sems, recv_sems, local_sem):
        x, y, c = _place()
        me, sibling = (x, y, c), (x, y, 1 - c)
        chips = [(1 - x, y), (x, 1 - y), (1 - x, 1 - y)]

        def rows(px, py, pc):
            return out_ref.at[pl.ds((4 * px + 2 * py + pc) * m_per, m_per), :]

        def copy(k, block, to, src=None):
            return pltpu.make_async_remote_copy(
                src_ref=rows(*block) if src is None else src, dst_ref=rows(*block),
                send_sem=send_sems.at[k], recv_sem=recv_sems.at[k], device_id=to, device_id_type=MESH)

        mine = pltpu.make_async_copy(x_ref, rows(*me), local_sem)
        mine.start()
        first = [copy(0, me, sibling, src=x_ref)]
        first += [copy(1 + j, me, (*chip, c), src=x_ref) for j, chip in enumerate(chips)]
        for cp in first:
            cp.start()
        passed = [copy(4 + j, (*chip, c), sibling) for j, chip in enumerate(chips)]
        for j, chip in enumerate(chips):
            copy(1 + j, (*chip, c), me).wait_recv()
            passed[j].start()
        copy(0, sibling, me).wait_recv()
        for j, chip in enumerate(chips):
            copy(4 + j, (*chip, 1 - c), me).wait_recv()
        for cp in first + passed:
            cp.wait_send()
        mine.wait()

    return pl.pallas_call(
        body, name=name, out_shape=SDS((N_DEV * m_per, n), blk.dtype),
        in_specs=[pl.BlockSpec(memory_space=pltpu.VMEM)], out_specs=pl.BlockSpec(memory_space=pltpu.VMEM),
        scratch_shapes=[pltpu.SemaphoreType.DMA((7,)), pltpu.SemaphoreType.DMA((7,)), pltpu.SemaphoreType.DMA],
        compiler_params=pltpu.CompilerParams(vmem_limit_bytes=VMEM_LIMIT),
    )(blk)


def _chip_peers(x, y):
    return [(1 - x, y), (x, 1 - y), (1 - x, 1 - y)]


def _peer_copies(src_refs, land_refs, send_sems, recv_sems, scatter, landed):
    x, y, c = _place()
    cps = []
    for a, (src_ref, land_ref) in enumerate(zip(src_refs, land_refs)):
        for j, (px, py) in enumerate(_chip_peers(x, y)):
            src = src_ref.at[2 * px + py] if scatter else src_ref
            dst = land_ref.at[2 * px + py] if landed else land_ref.at[2 * x + y]
            cps.append(pltpu.make_async_remote_copy(
                src_ref=src, dst_ref=dst, send_sem=send_sems.at[3 * a + j], recv_sem=recv_sems.at[3 * a + j],
                device_id=(px, py, c), device_id_type=MESH))
    return cps


def _exchange_chips(srcs, scatter, name):
    n = len(srcs)

    def body(*refs):
        src_refs, land_refs = refs[:n], refs[n:2 * n]
        send_sems, recv_sems, local_sems = refs[2 * n:]
        x, y, _ = _place()
        me = 2 * x + y
        mine = [pltpu.make_async_copy(s.at[me] if scatter else s, l.at[me], local_sems.at[a])
                for a, (s, l) in enumerate(zip(src_refs, land_refs))]
        for cp in mine:
            cp.start()
        out = _peer_copies(src_refs, land_refs, send_sems, recv_sems, scatter, False)
        for cp in out:
            cp.start()
        for cp in _peer_copies(src_refs, land_refs, send_sems, recv_sems, scatter, True):
            cp.wait_recv()
        for cp in out:
            cp.wait_send()
        for cp in mine:
            cp.wait()

    anyspec = pl.BlockSpec(memory_space=pl.ANY)
    shapes = [SDS(s.shape if scatter else (N_CHIPS,) + s.shape, s.dtype) for s in srcs]
    return pl.pallas_call(
        body, name=name, out_shape=shapes, in_specs=[anyspec] * n, out_specs=[anyspec] * n,
        scratch_shapes=[pltpu.SemaphoreType.DMA((3 * n,)), pltpu.SemaphoreType.DMA((3 * n,)),
                        pltpu.SemaphoreType.DMA((n,))],
        compiler_params=pltpu.CompilerParams(vmem_limit_bytes=VMEM_LIMIT),
    )(*srcs)


_HBM = pl.BlockSpec(memory_space=pltpu.HBM)
_SEM = pl.BlockSpec(memory_space=pltpu.SEMAPHORE)
_EFFECT = pltpu.SideEffectType.DATAFLOW_SIDE_EFFECTING


def _exchange_begin(srcs, lands, scatter, name):
    n = len(srcs)

    def body(*refs):
        src_refs, land_refs = refs[:n], refs[n:2 * n]
        send_sems, recv_sems = refs[2 * n:2 * n + 2]
        token = refs[-1]
        for cp in _peer_copies(src_refs, land_refs, send_sems, recv_sems, scatter, False):
            cp.start()
        token[...] = jnp.zeros_like(token)

    res = pl.pallas_call(
        body, name=name,
        out_shape=(pltpu.SemaphoreType.DMA((3 * n,)), pltpu.SemaphoreType.DMA((3 * n,)),
                   *[pltpu.HBM(s.shape, s.dtype) for s in srcs], *[pltpu.HBM(l.shape, l.dtype) for l in lands],
                   SDS((SUB, LANE), F32)),
        in_specs=[_HBM] * (2 * n), out_specs=(_SEM, _SEM, *[_HBM] * (2 * n), pl.BlockSpec(memory_space=pltpu.VMEM)),
        input_output_aliases={i: 2 + i for i in range(2 * n)},
        compiler_params=pltpu.CompilerParams(has_side_effects=_EFFECT),
    )(*[pltpu.with_memory_space_constraint(a, pltpu.HBM) for a in (*srcs, *lands)])
    return res[0], res[1], res[2:2 + n], res[2 + n:2 + 2 * n], res[-1]


def _exchange_end(handle, after, scatter, name):
    send_sems, recv_sems, srcs, lands, _ = handle
    n = len(srcs)

    def body(*refs):
        src_refs, land_refs = refs[:n], refs[n:2 * n]
        send_sems, recv_sems = refs[2 * n:2 * n + 2]
        for cp in _peer_copies(src_refs, land_refs, send_sems, recv_sems, scatter, True):
            cp.wait_send()
            cp.wait_recv()

    res = pl.pallas_call(
        body, name=name,
        out_shape=tuple(pltpu.HBM(a.shape, a.dtype) for a in (*srcs, *lands)),
        in_specs=[_HBM] * (2 * n) + [_SEM, _SEM, pl.BlockSpec(memory_space=pl.ANY)], out_specs=tuple([_HBM] * (2 * n)),
        input_output_aliases={i: i for i in range(2 * n)},
        compiler_params=pltpu.CompilerParams(has_side_effects=_EFFECT),
    )(*srcs, *lands, send_sems, recv_sems, after)
    return list(res[n:])


def _own_slab(stack4, chip):
    idx = lax.broadcasted_iota(jnp.int32, (N_CHIPS,) + (1,) * (stack4.ndim - 1), 0)
    return jnp.where(idx == chip, stack4, jnp.zeros((), stack4.dtype))


def _swap_sibling(vs, name):
    n = len(vs)

    def body(*refs):
        in_refs, out_refs = refs[:n], refs[n:2 * n]
        send_sems, recv_sems = refs[2 * n:]
        x, y, c = _place()
        cps = [pltpu.make_async_remote_copy(
            src_ref=i, dst_ref=o, send_sem=send_sems.at[a], recv_sem=recv_sems.at[a],
            device_id=(x, y, 1 - c), device_id_type=MESH) for a, (i, o) in enumerate(zip(in_refs, out_refs))]
        for cp in cps:
            cp.start()
        for cp in cps:
            cp.wait()

    anyspec = pl.BlockSpec(memory_space=pl.ANY)
    return pl.pallas_call(
        body, name=name, out_shape=[SDS(v.shape, v.dtype) for v in vs], in_specs=[anyspec] * n, out_specs=[anyspec] * n,
        scratch_shapes=[pltpu.SemaphoreType.DMA((n,)), pltpu.SemaphoreType.DMA((n,))],
        compiler_params=pltpu.CompilerParams(vmem_limit_bytes=VMEM_LIMIT),
    )(*vs)


def _select(stacked, idx):
    out = stacked[0]
    for j in range(1, stacked.shape[0]):
        out = jnp.where(idx == j, stacked[j], out)
    return out


BIG = (
    ("w1_a", DFF // 4, D, False), ("w3_a", DFF // 4, D, False), ("w2_a", DFF // 4, D, False),
    ("w_in", D, 5632 // 4, True), ("w_conv_out", CW // 4, D, False), ("w_glu", SW // 4, SW, False),
    ("w_ssm_out", SW, D // 4, True), ("w_out", D // 4, D, False),
    ("w1_b", DFF // 4, D, False), ("w3_b", DFF // 4, D, False), ("w2_b", DFF // 4, D, False),
)
TRANSPOSED = frozenset(("w1_a", "w3_a", "w1_b", "w3_b"))
PACK_COLS = 1024


def _view(a, name):
    return jnp.transpose(a, (0, 2, 1)) if name in TRANSPOSED else a


def _full_from_stacked(st, split_cols):
    _, rows, cols = st.shape
    if split_cols:
        return st.transpose(1, 0, 2).reshape(rows, N_CHIPS * cols)
    return st.reshape(N_CHIPS * rows, cols)


def _stacked_from_full(full, rows, cols, split_cols):
    if split_cols:
        return full.reshape(rows, N_CHIPS, cols).transpose(1, 0, 2)
    return full.reshape(N_CHIPS, rows, cols)


def _blockdiag(t):
    r = lax.broadcasted_iota(jnp.int32, (SW, GP), 0) // NH
    cidx = lax.broadcasted_iota(jnp.int32, (SW, GP), 1) // NP
    return jnp.where(r == cidx, jnp.tile(t, (NG, 1)), 0.0)


def _blockdiag_extract(acc):
    gs = NG // SSM_SUPER
    a = acc.reshape(NG, NH, gs, NP)
    sel = (lax.broadcasted_iota(jnp.int32, (NG, 1, gs, 1), 0) % gs) == lax.broadcasted_iota(jnp.int32, (NG, 1, gs, 1), 2)
    a = jnp.sum(jnp.where(sel, a, 0.0), axis=2)
    return a.transpose(1, 0, 2).reshape(NH, GP)


def _to_t(p):
    return p.transpose(2, 0, 1).reshape(NH, GP)


def _from_t(t):
    return t.reshape(NH, NG, NP).transpose(1, 2, 0)


def _c_to_t(p):
    return p.transpose(1, 0, 2).reshape(NH, GP)


def _c_from_t(t):
    return t.reshape(NH, NG, NP).transpose(1, 0, 2)


def _ffn_forward(x, g, sh, sc, gt, w1, w3, w2, tag):
    h = _norm_mod(x, g, sh, sc, f"{tag}_norm")
    a, b, hid = _swiglu_up(h, w1, w3, f"{tag}_up")
    w2 = w2(hid) if callable(w2) else w2
    f, xo = _ffn_down(hid, w2, x, gt, f"{tag}_down")
    return xo, (x, h, a, b, hid, f), w2


def _ffn_backward(dxo, saved, g, sc, gt, w1, w3, w2, tag, emit=lambda key, gw: 0.0):
    x, h, a, b, hid, f = saved
    dfs, da, db, dgt = _ffn_bwd_hid(dxo, gt, f, a, b, w2, f"{tag}_bwd_hid")
    h2 = _flat(h)
    gw2 = _mm(_flat(hid), _flat(dfs), ta=True, name=f"{tag}_gw2")
    tok = emit("w2", gw2)
    gw1 = _mm(_flat(da), h2, ta=True, name=f"{tag}_gw1")
    tok = tok + emit("w1", gw1)
    gw3 = _mm(_flat(db), h2, ta=True, name=f"{tag}_gw3")
    tok = tok + emit("w3", gw3)
    dx, dsh, dsc, dg = _dh_norm_bwd([da, db], [w1, w3], x, g, sc + tok, dxo, f"{tag}_bwd_dh", transposed=True)
    return dx, (dsh, dsc, dgt, dg), (gw1, gw3, gw2)


def kernel(x, c, w_ada, b_ada, g_ffn1, w1_a, w3_a, w2_a, g_mix, w_in, conv_w, w_conv_out, a_re, a_im, b_re, b_im, c_re, c_im, log_dt, d_skip, w_glu, w_ssm_out, w_out, g_ffn2, w1_b, w3_b, w2_b, g_final, loss_target, m_w_ada, m_b_ada, m_g_ffn1, m_w1_a, m_w3_a, m_w2_a, m_g_mix, m_w_in, m_conv_w, m_w_conv_out, m_a_re, m_a_im, m_b_re, m_b_im, m_c_re, m_c_im, m_log_dt, m_d_skip, m_w_glu, m_w_ssm_out, m_w_out, m_g_ffn2, m_w1_b, m_w3_b, m_w2_b, m_g_final, v_w_ada, v_b_ada, v_g_ffn1, v_w1_a, v_w3_a, v_w2_a, v_g_mix, v_w_in, v_conv_w, v_w_conv_out, v_a_re, v_a_im, v_b_re, v_b_im, v_c_re, v_c_im, v_log_dt, v_d_skip, v_w_glu, v_w_ssm_out, v_w_out, v_g_ffn2, v_w1_b, v_w3_b, v_w2_b, v_g_final):
    args = dict(locals())
    names = ["w_ada", "b_ada", "g_ffn1", "w1_a", "w3_a", "w2_a", "g_mix", "w_in", "conv_w", "w_conv_out", "a_re",
             "a_im", "b_re", "b_im", "c_re", "c_im", "log_dt", "d_skip", "w_glu", "w_ssm_out", "w_out", "g_ffn2",
             "w1_b", "w3_b", "w2_b", "g_final"]
    bsz, seq, _ = x.shape
    mx, my, mc = _place()
    chip = 2 * mx + my
    dev = 4 * mx + 2 * my + mc

    groups = (BIG[:3], BIG[3:8], BIG[8:])
    wfull = {}

    def shards_bf16(grp):
        return [_view(args[n], n)[0].astype(BF16) for n, _, _, _ in grp]

    def unpack_group(gathered, grp):
        for (n, _, _, split), st in zip(grp, gathered):
            wfull[n] = _full_from_stacked(st, split)

    up_grp, down_grp = groups[0][:2], groups[0][2:]
    unpack_group(_exchange_chips(shards_bf16(up_grp), False, "gather_w_ffn1_up"), up_grp)

    nmod_shard = NMOD * D // N_CHIPS
    c_all = _all_gather8(c.reshape(SUB, -1), "gather_c").reshape(N_DEV * bsz, D)
    b_shard = _select(b_ada.reshape(N_CHIPS, 1, nmod_shard), chip)
    mod_shard = _ada_fwd(c_all, w_ada, b_shard)
    nb = N_DEV * bsz
    cw_pad = jnp.pad(conv_w[0], ((0, SUB - 3), (0, nmod_shard - CW // N_CHIPS)))
    mod_st = _exchange_chips([jnp.concatenate([mod_shard, cw_pad], axis=0)], False, "gather_mod")[0]
    mod_all = mod_st[:, :nb].transpose(1, 0, 2).reshape(N_DEV, bsz, NMOD * D)
    mod = _select(mod_all, dev)

    down_src, mix_src, ffn2_src, mod = lax.optimization_barrier(
        (shards_bf16(down_grp), shards_bf16(groups[1]), shards_bf16(groups[2]), mod))

    def gather_begin(srcs, name):
        lands = [_own_slab(jnp.broadcast_to(s[None], (N_CHIPS,) + s.shape), chip) for s in srcs]
        return _exchange_begin(srcs, lands, False, name)

    down_handle = gather_begin(down_src, "gather_w_ffn1_down_start")
    mix_handle = gather_begin(mix_src, "gather_w_mix_start")
    ffn2_handle = gather_begin(ffn2_src, "gather_w_ffn2_start")
    start_tokens = down_handle[4][0:1, 0:1] + mix_handle[4][0:1, 0:1] + ffn2_handle[4][0:1, 0:1]

    sh1, sc1, gt1, sh2, sc2, gt2, sh3, sc3, gt3 = [mod[:, None, j * D:(j + 1) * D] for j in range(NMOD)]
    convw = mod_st[:, nb:nb + 3, :CW // N_CHIPS].transpose(1, 0, 2).reshape(3, CW)
    convw8 = jnp.pad(convw, ((0, SUB - 3), (0, 0)))

    are, aim = a_re.reshape(1, GP), a_im.reshape(1, GP)
    ldt = jnp.broadcast_to(log_dt.reshape(NG, 1), (NG, NP)).reshape(1, GP)
    bre_t, bim_t = _to_t(b_re[0]), _to_t(b_im[0])
    abr, abi, bbr_t, bbi_t = _ssm_disc(are, aim, ldt, bre_t, bim_t)
    wb = jnp.concatenate([_blockdiag(bbr_t), _blockdiag(bbi_t)], axis=1).astype(BF16)
    wct = jnp.concatenate([_blockdiag(_c_to_t(c_re[0])), -_blockdiag(_c_to_t(c_im[0]))], axis=1).astype(BF16)
    ar8 = jnp.broadcast_to(abr, (SUB, GP))
    ai8 = jnp.broadcast_to(abi, (SUB, GP))

    def late_w2a(hid):
        unpack_group(_exchange_end(down_handle, hid, False, "gather_w_ffn1_down_wait"), down_grp)
        return wfull["w2_a"]

    x1, ffn1_saved, _ = _ffn_forward(
        x, g_ffn1 + start_tokens, sh1, sc1, gt1, wfull["w1_a"], wfull["w3_a"], late_w2a, "ffn1")
    unpack_group(_exchange_end(mix_handle, x1, False, "gather_w_mix_wait"), groups[1])

    h2 = _norm_mod(x1, g_mix, sh2, sc2, "mix_norm")
    h2f = _flat(h2)
    win = wfull["w_in"]
    win1, winu, win3 = win[:, :3 * CW], win[:, 3 * CW:3 * CW + SW], win[:, 3 * CW + SW:]
    p1 = _mm(h2f, win1, name="mix_in1").reshape(bsz, seq, 3 * CW)
    u = _mm(h2f, winu, name="mix_inu").reshape(bsz, seq, SW)
    p3 = _mm(h2f, win3, name="mix_in3").reshape(bsz, seq, 2 * D)

    ya_in = _conv_fwd(p1, convw8, "conv_fwd")
    ya = _mm(_flat(ya_in), wfull["w_conv_out"], name="conv_out").reshape(bsz, seq, D)

    u_tm = u.transpose(1, 0, 2).reshape(seq * bsz, SW)
    xs, y_tm = _ssm_fwd(u_tm, wb, wct, ar8, ai8, bsz, "ssm_fwd")
    ys = y_tm.reshape(seq, bsz, SW).transpose(1, 0, 2)
    s0, z, s1, s2, yb = _ssm_post(ys, u, d_skip, wfull["w_glu"], wfull["w_ssm_out"], "ssm_post")

    merged, mix, x2 = _merge_out(ya, yb, p3, wfull["w_out"], x1, gt2, "merge_out")
    unpack_group(_exchange_end(ffn2_handle, x2, False, "gather_w_ffn2_wait"), groups[2])
    x3, ffn2_saved, _ = _ffn_forward(x2, g_ffn2, sh3, sc3, gt3, wfull["w1_b"], wfull["w3_b"], wfull["w2_b"], "ffn2")

    dx3, lossvec, dgfin = _final_loss(x3, g_final.reshape(1, D), loss_target, "final_loss")
    loss = lax.psum(jnp.sum(lossvec), ("x", "y", "c"))

    gfull = {}
    dx2, (dsh3, dsc3, dgt3, dg3), (gfull["w1_b"], gfull["w3_b"], gfull["w2_b"]) = _ffn_backward(
        dx3, ffn2_saved, g_ffn2, sc3, gt3, wfull["w1_b"], wfull["w3_b"], wfull["w2_b"], "ffn2")

    def stack_group(grp):
        return [_stacked_from_full(gfull[n], rows, cols, split).astype(BF16) for n, rows, cols, split in grp]

    st_ffn2 = stack_group(groups[2])
    h_ffn2 = _exchange_begin(st_ffn2, [_own_slab(s, chip) for s in st_ffn2], True, "scatter_ffn2_start")

    dmix, dya, dyb, dp3, dgt2 = _merge_bwd(dx2, gt2 + h_ffn2[4][0, 0], mix, ya, yb, p3, wfull["w_out"], "merge_bwd")
    gfull["w_out"] = _mm(_flat(merged), _flat(dmix), ta=True, name="gw_out")
    dp1, dconvw8 = _conv_bwd(dya, wfull["w_conv_out"], p1, convw8, "conv_bwd")
    gfull["w_conv_out"] = _mm(_flat(ya_in), _flat(dya), ta=True, name="gw_conv_out")
    ds0, dz, ddskip = _ssm_post_bwd(dyb, s0, z, u, d_skip, wfull["w_glu"], wfull["w_ssm_out"], "ssm_post_bwd")
    gfull["w_ssm_out"] = _mm(_flat(s2), _flat(dyb), ta=True, name="gw_ssm_out")
    gfull["w_glu"] = _mm(_flat(s1), _flat(dz), ta=True, name="gw_glu")
    dy_tm = ds0.transpose(1, 0, 2).reshape(seq * bsz, SW)
    du_tm, dwb, dwct, dar8, dai8 = _ssm_bwd(dy_tm, u_tm, xs, wb, wct, ar8, ai8, bsz, "ssm_bwd")
    du = _du_combine(du_tm.reshape(seq, bsz, SW).transpose(1, 0, 2), ds0, d_skip, "du_combine")
    dx1, dsh2, dsc2, dgmix = _dh_norm_bwd([dp1, du, dp3], [win1, winu, win3], x1, g_mix, sc2, dx2, "mix_bwd_dh")
    gfull["w_in"] = jnp.concatenate([
        _mm(h2f, _flat(dp1), ta=True, name="gw_in1"), _mm(h2f, _flat(du), ta=True, name="gw_inu"),
        _mm(h2f, _flat(dp3), ta=True, name="gw_in3")], axis=1)

    st_mix = stack_group(groups[1])
    h_mix = _exchange_begin(st_mix, [_own_slab(s, chip) for s in st_mix], True, "scatter_mix_start")

    ffn1_names = {"w1": BIG[0], "w3": BIG[1], "w2": BIG[2]}
    ffn1_handles = {}

    def emit_ffn1(key, gw):
        n, rows, cols, split = ffn1_names[key]
        st = _stacked_from_full(gw, rows, cols, split).astype(BF16)
        ffn1_handles[n] = _exchange_begin([st], [_own_slab(st, chip)], True, f"scatter_ffn1_{key}_start")
        return ffn1_handles[n][4][0, 0]

    grad_x, (dsh1, dsc1, dgt1, dg1), _ = _ffn_backward(
        dx1, ffn1_saved, g_ffn1, sc1, gt1 + h_mix[4][0, 0], wfull["w1_a"], wfull["w3_a"], wfull["w2_a"], "ffn1",
        emit=emit_ffn1)

    sbw = GP // SSM_SUPER
    d_are, d_aim, d_ldt, d_bre_t, d_bim_t = _ssm_disc_bwd(
        are, aim, ldt, bre_t, bim_t, jnp.sum(dar8, axis=0, keepdims=True), jnp.sum(dai8, axis=0, keepdims=True),
        _blockdiag_extract(dwb[:, :sbw]), _blockdiag_extract(dwb[:, sbw:]))
    d_cre = _c_from_t(_blockdiag_extract(dwct[:, :sbw]))
    d_cim = -_c_from_t(_blockdiag_extract(dwct[:, sbw:]))

    small_parts = [dg1, dgmix, dg3, dgfin, dconvw8[:3], d_are, d_aim, _from_t(d_bre_t), _from_t(d_bim_t), d_cre, d_cim,
                   jnp.sum(d_ldt.reshape(NG, NP), axis=1), ddskip]
    small_sizes = [int(p.size) for p in small_parts]
    n_small = sum(small_sizes)
    n_small_pad = -(-n_small // (SUB * PACK_COLS)) * (SUB * PACK_COLS)
    dmod = jnp.concatenate([dsh1, dsc1, dgt1, dsh2, dsc2, dgt2, dsh3, dsc3, dgt3], axis=2).reshape(bsz * NMOD * D)
    flat = jnp.concatenate([p.reshape(-1) for p in small_parts] + [jnp.zeros((n_small_pad - n_small,), F32), dmod])
    allg = _all_gather8(flat.reshape(SUB, -1), "gather_small").reshape(N_DEV, -1)
    small = _sum_slabs(allg[:, :n_small_pad].reshape(N_DEV, -1, PACK_COLS), "sum_small").reshape(-1)
    sg, o = [], 0
    for p, sz in zip(small_parts, small_sizes):
        sg.append(small[o:o + sz].reshape(p.shape))
        o += sz
    (g_g1, g_gmix, g_g3, g_gfin, g_convw, g_are, g_aim, g_bre, g_bim, g_cre, g_cim, g_ldt, g_dskip) = sg

    dmod_all = allg[:, n_small_pad:].reshape(nb, NMOD * D)
    dmod_shard = _select(dmod_all.reshape(nb, N_CHIPS, nmod_shard).transpose(1, 0, 2), chip)
    g_wada, g_bada = _ada_bwd(c_all, dmod_shard, dmod_all)

    recv_ffn2 = _exchange_end(h_ffn2, g_wada, True, "scatter_ffn2_wait")
    recv_mix = _exchange_end(h_mix, g_wada, True, "scatter_mix_wait")
    recv_ffn1 = [_exchange_end(ffn1_handles[n], g_wada, True, f"scatter_ffn1_{n}_wait")[0] for n, _, _, _ in groups[0]]
    recv = [*recv_ffn1, *recv_mix, *recv_ffn2]
    recv_sib = _swap_sibling(recv, "swap_sibling")

    grads, deltas, new_m, new_v = {}, {}, {}, {}
    for (n, _, _, _), r_own, r_sib in zip(BIG, recv, recv_sib):
        res = _adamw_big(_view(args[n], n), _view(args["m_" + n], n), _view(args["v_" + n], n), r_own, r_sib,
                         f"adamw_{n}")
        grads[n], deltas[n], new_m[n], new_v[n] = [_view(r, n) for r in res]
    grads["w_ada"] = g_wada[None]
    deltas["w_ada"], new_m["w_ada"], new_v["w_ada"] = _adamw_rows(w_ada, m_w_ada, v_w_ada, g_wada, "adamw_w_ada")

    g_convw_shard = _select(g_convw.reshape(3, N_CHIPS, CW // N_CHIPS).transpose(1, 0, 2), chip)
    small_g = {"b_ada": g_bada, "g_ffn1": g_g1, "g_mix": g_gmix, "g_ffn2": g_g3, "g_final": g_gfin,
               "conv_w": g_convw_shard, "a_re": g_are, "a_im": g_aim, "b_re": g_bre, "b_im": g_bim,
               "c_re": g_cre, "c_im": g_cim, "log_dt": g_ldt, "d_skip": g_dskip}
    small_names = list(small_g)
    sizes = [int(args[n].size) for n in small_names]
    tot = sum(sizes)
    tot_pad = -(-tot // (SUB * PACK_COLS)) * (SUB * PACK_COLS)

    def pack(get):
        return jnp.concatenate([get(n).reshape(-1) for n in small_names] + [jnp.zeros((tot_pad - tot,), F32)]).reshape(
            -1, PACK_COLS)

    res = _adamw_plain(pack(lambda n: args[n]), pack(lambda n: args["m_" + n]), pack(lambda n: args["v_" + n]),
                       pack(lambda n: small_g[n]), "adamw_small")
    o = 0
    for n, sz in zip(small_names, sizes):
        shp = args[n].shape
        grads[n] = small_g[n].reshape(shp)
        deltas[n], new_m[n], new_v[n] = [r.reshape(-1)[o:o + sz].reshape(shp) for r in res]
        o += sz

    return (loss, grad_x, *[grads[n] for n in names], *[deltas[n] for n in names],
            *[new_m[n] for n in names], *[new_v[n] for n in names])
```

```python
import functools
import math

import jax
import jax.numpy as jnp
from jax import lax
from jax.experimental import pallas as pl
from jax.experimental.pallas import tpu as pltpu

F32 = jnp.float32
BF16 = jnp.bfloat16
SDS = jax.ShapeDtypeStruct
MESH = pl.DeviceIdType.MESH

D = 1024
DFF = 2816
CW = 1024
SW = 512
NG, NP, NH = 32, 64, 16
GP = NG * NP
NMOD = 9
EPS = 1e-6
N_CHIPS = 4
N_DEV = 8
SUB = 8
LANE = 128
SSM_SUPER = 4
VMEM_LIMIT = 50 * 1024 * 1024

LR, B1, B2, AEPS, WD, STEP = 0.001, 0.9, 0.999, 1e-08, 0.01, 10
BC1 = 1.0 - B1 ** STEP
BC2 = 1.0 - B2 ** STEP


def _cp(*sem):
    return pltpu.CompilerParams(dimension_semantics=sem or None, vmem_limit_bytes=VMEM_LIMIT)


def _pick_tile(n, cands):
    for t in cands:
        if t <= n and n % t == 0:
            return t
    return n


def _dot(a, b):
    return lax.dot_general(a, b, (((1,), (0,)), ((), ())), preferred_element_type=F32)


def _dot_nt(a, b):
    return lax.dot_general(a, b, (((1,), (1,)), ((), ())), preferred_element_type=F32)


def _dot_tn(a, b):
    return lax.dot_general(a, b, (((0,), (0,)), ((), ())), preferred_element_type=F32)


def _row(tm, width, col=0):
    return pl.BlockSpec((1, tm, width), lambda b, i, *_: (b, i, col))


def _seqvec(width):
    return pl.BlockSpec((1, 1, width), lambda b, *_: (b, 0, 0))


def _full2(shape):
    return pl.BlockSpec(shape, lambda *_: (0, 0))


def _sigmoid(x):
    return jax.nn.sigmoid(x)


def _mm(a, b, *, ta=False, tb=False, out_dtype=F32, name):
    if ta:
        kdim, m = a.shape
    else:
        m, kdim = a.shape
    n = b.shape[0] if tb else b.shape[1]
    tm = _pick_tile(m, (1408, 1024, 512, 256, 128))
    tn = _pick_tile(n, (1408, 1024, 512, 256, 128))
    tk = _pick_tile(kdim, (1024, 512, 256, 128))
    nk = kdim // tk

    def body(a_ref, b_ref, o_ref, acc_ref):
        k = pl.program_id(2)

        @pl.when(k == 0)
        def _():
            acc_ref[...] = jnp.zeros_like(acc_ref)

        av = a_ref[...].astype(BF16)
        bv = b_ref[...].astype(BF16)
        dn = (((0 if ta else 1,), (1 if tb else 0,)), ((), ()))
        acc_ref[...] += lax.dot_general(av, bv, dn, preferred_element_type=F32)

        @pl.when(k == nk - 1)
        def _():
            o_ref[...] = acc_ref[...].astype(out_dtype)

    a_spec = pl.BlockSpec((tk, tm), lambda i, j, k: (k, i)) if ta else pl.BlockSpec((tm, tk), lambda i, j, k: (i, k))
    b_spec = pl.BlockSpec((tn, tk), lambda i, j, k: (j, k)) if tb else pl.BlockSpec((tk, tn), lambda i, j, k: (k, j))
    return pl.pallas_call(
        body, name=name, grid=(m // tm, n // tn, nk),
        in_specs=[a_spec, b_spec],
        out_specs=pl.BlockSpec((tm, tn), lambda i, j, k: (i, j)),
        out_shape=SDS((m, n), out_dtype),
        scratch_shapes=[pltpu.VMEM((tm, tn), F32)],
        compiler_params=_cp("parallel", "parallel", "arbitrary"),
    )(a, b)


def _flat(a):
    return a.reshape(-1, a.shape[-1])


def _norm_mod(x, g, sh, sc, name):
    bsz, seq, dm = x.shape
    tm = _pick_tile(seq, (512, 256, 128))

    def body(x_ref, g_ref, sh_ref, sc_ref, o_ref):
        xf = x_ref[0]
        r = lax.rsqrt(jnp.mean(xf * xf, axis=-1, keepdims=True) + EPS)
        hn = xf * r * g_ref[...]
        o_ref[0] = (hn * (1.0 + sc_ref[0]) + sh_ref[0]).astype(BF16)

    return pl.pallas_call(
        body, name=name, grid=(bsz, seq // tm),
        in_specs=[_row(tm, dm), _full2((1, dm)), _seqvec(dm), _seqvec(dm)],
        out_specs=_row(tm, dm), out_shape=SDS((bsz, seq, dm), BF16),
        compiler_params=_cp("parallel", "parallel"),
    )(x, g, sh, sc)


def _swiglu_up(h, w1, w3, name):
    bsz, seq, dm = h.shape
    nf = w1.shape[0]
    tm = _pick_tile(seq, (512, 256, 128))
    tn = _pick_tile(nf, (1408, 512, 256, 128))

    def body(h_ref, w1_ref, w3_ref, a_ref, b_ref, hid_ref):
        hv = h_ref[0]
        a = _dot_nt(hv, w1_ref[...])
        b = _dot_nt(hv, w3_ref[...])
        sg = _sigmoid(a)
        sa = a * sg
        a_ref[0] = (b * (sg * (1.0 + a * (1.0 - sg)))).astype(BF16)
        b_ref[0] = sa.astype(BF16)
        hid_ref[0] = (sa * b).astype(BF16)

    wspec = pl.BlockSpec((tn, dm), lambda n, b, i: (n, 0))
    ospec = pl.BlockSpec((1, tm, tn), lambda n, b, i: (b, i, n))
    shp = SDS((bsz, seq, nf), BF16)
    return pl.pallas_call(
        body, name=name, grid=(nf // tn, bsz, seq // tm),
        in_specs=[pl.BlockSpec((1, tm, dm), lambda n, b, i: (b, i, 0)), wspec, wspec],
        out_specs=[ospec, ospec, ospec], out_shape=[shp, shp, shp],
        compiler_params=_cp("parallel", "parallel", "parallel"),
    )(h, w1, w3)


def _ffn_down(hid, w2, x, gt, name):
    bsz, seq, nf = hid.shape
    dm = w2.shape[1]
    tm = _pick_tile(seq, (512, 256, 128))

    def body(hid_ref, w2_ref, x_ref, gt_ref, f_ref, xo_ref):
        f = _dot(hid_ref[0], w2_ref[...])
        f_ref[0] = f
        xo_ref[0] = x_ref[0] + 0.5 * gt_ref[0] * f

    shp = SDS((bsz, seq, dm), F32)
    return pl.pallas_call(
        body, name=name, grid=(bsz, seq // tm),
        in_specs=[_row(tm, nf), _full2((nf, dm)), _row(tm, dm), _seqvec(dm)],
        out_specs=[_row(tm, dm), _row(tm, dm)], out_shape=[shp, shp],
        compiler_params=_cp("parallel", "parallel"),
    )(hid, w2, x, gt)


def _ffn_bwd_hid(dxo, gt, f, a, b, w2, name):
    bsz, seq, dm = dxo.shape
    nf = a.shape[2]
    tm = _pick_tile(seq, (512, 256, 128))
    tn = _pick_tile(nf, (1408, 512, 256, 128))

    def body(dxo_ref, gt_ref, f_ref, a_ref, b_ref, w2_ref, dfs_ref, da_ref, db_ref, dgt_ref):
        i = pl.program_id(1)
        n = pl.program_id(2)

        @pl.when(n == 0)
        def _():
            dxo = dxo_ref[0]
            dfs_ref[0] = (0.5 * gt_ref[0] * dxo).astype(BF16)
            part = jnp.sum(0.5 * dxo * f_ref[0], axis=0, keepdims=True)

            @pl.when(i == 0)
            def _():
                dgt_ref[0] = part

            @pl.when(i > 0)
            def _():
                dgt_ref[0] += part

        dhid = _dot_nt(dfs_ref[0], w2_ref[pl.ds(pl.multiple_of(n * tn, tn), tn), :])
        da_ref[0] = (dhid * a_ref[0].astype(F32)).astype(BF16)
        db_ref[0] = (dhid * b_ref[0].astype(F32)).astype(BF16)

    hspec = pl.BlockSpec((1, tm, tn), lambda b, i, n: (b, i, n))
    return pl.pallas_call(
        body, name=name, grid=(bsz, seq // tm, nf // tn),
        in_specs=[_row(tm, dm), _seqvec(dm), _row(tm, dm), hspec, hspec, _full2((nf, dm))],
        out_specs=[_row(tm, dm), hspec, hspec, _seqvec(dm)],
        out_shape=[SDS((bsz, seq, dm), BF16), SDS((bsz, seq, nf), BF16), SDS((bsz, seq, nf), BF16),
                   SDS((bsz, 1, dm), F32)],
        compiler_params=_cp("arbitrary", "arbitrary", "arbitrary"),
    )(dxo, gt, f, a, b, w2)


def _dh_norm_bwd(pieces, weights, x, g, sc, dxo, name, transposed=False):
    bsz, seq, dm = x.shape
    tm = _pick_tile(seq, (512, 256, 128))
    npc = len(pieces)
    dot = _dot if transposed else _dot_nt

    def body(*refs):
        p_refs = refs[:npc]
        w_hbm = refs[npc:2 * npc]
        x_ref, g_ref, sc_ref, dxo_ref, dx_ref, dsh_ref, dsc_ref, dg_ref = refs[2 * npc:2 * npc + 8]
        w_refs = refs[2 * npc + 8:]
        b = pl.program_id(0)
        i = pl.program_id(1)

        @pl.when((i == 0) & (b == 0))
        def _():
            for src, dst in zip(w_hbm, w_refs):
                pltpu.sync_copy(src, dst)

        dh = dot(p_refs[0][0], w_refs[0][...])
        for j in range(1, npc):
            dh = dh + dot(p_refs[j][0], w_refs[j][...])
        xf = x_ref[0]
        gv = g_ref[...]
        r = lax.rsqrt(jnp.mean(xf * xf, axis=-1, keepdims=True) + EPS)
        xhat = xf * r
        dhn = dh * (1.0 + sc_ref[0])
        p_sh = jnp.sum(dh, axis=0, keepdims=True)
        p_sc = jnp.sum(dh * (xhat * gv), axis=0, keepdims=True)
        p_g = jnp.sum(dhn * xhat, axis=0, keepdims=True)
        dxh = dhn * gv
        dx_ref[0] = dxo_ref[0] + r * (dxh - xhat * jnp.mean(dxh * xhat, axis=-1, keepdims=True))

        @pl.when(i == 0)
        def _():
            dsh_ref[0] = p_sh
            dsc_ref[0] = p_sc

        @pl.when(i > 0)
        def _():
            dsh_ref[0] += p_sh
            dsc_ref[0] += p_sc

        @pl.when((i == 0) & (b == 0))
        def _():
            dg_ref[...] = p_g

        @pl.when((i > 0) | (b > 0))
        def _():
            dg_ref[...] += p_g

    return pl.pallas_call(
        body, name=name, grid=(bsz, seq // tm),
        in_specs=[_row(tm, p.shape[2]) for p in pieces] + [pl.BlockSpec(memory_space=pl.ANY)] * npc + [
            _row(tm, dm), _full2((1, dm)), _seqvec(dm), _row(tm, dm)],
        out_specs=[_row(tm, dm), _seqvec(dm), _seqvec(dm), _full2((1, dm))],
        out_shape=[SDS((bsz, seq, dm), F32), SDS((bsz, 1, dm), F32), SDS((bsz, 1, dm), F32), SDS((1, dm), F32)],
        scratch_shapes=[pltpu.VMEM(w.shape, w.dtype) for w in weights],
        compiler_params=_cp("arbitrary", "arbitrary"),
    )(*pieces, *weights, x, g, sc, dxo)


HALO = 16


def _conv_core(gc, v, gch, vh, w, first):
    cv = gc * v
    halo = jnp.where(first, 0.0, gch * vh)
    ext = jnp.concatenate([halo, cv], axis=0)
    cv1 = pltpu.roll(ext, 1, 0)[HALO:]
    cv2 = pltpu.roll(ext, 2, 0)[HALO:]
    conv = w[0:1] * cv2 + w[1:2] * cv1 + w[2:3] * cv
    return cv, cv1, cv2, conv


def _prev_halo(tm, col):
    return pl.BlockSpec((1, HALO, CW), lambda b, i, *_: (b, jnp.maximum(i * (tm // HALO) - 1, 0), col))


def _next_halo(tm, seq, col):
    return pl.BlockSpec((1, HALO, CW), lambda b, i, *_: (b, jnp.minimum((i + 1) * (tm // HALO), seq // HALO - 1), col))


def _conv_fwd(p1, convw8, name):
    bsz, seq, _ = p1.shape
    tm = _pick_tile(seq, (512, 256, 128))

    def body(gb_ref, gc_ref, v_ref, gch_ref, vh_ref, w_ref, o_ref):
        first = pl.program_id(1) == 0
        _, _, _, conv = _conv_core(gc_ref[0].astype(F32), v_ref[0].astype(F32), gch_ref[0].astype(F32),
                                   vh_ref[0].astype(F32), w_ref[...], first)
        o_ref[0] = (gb_ref[0].astype(F32) * conv).astype(BF16)

    return pl.pallas_call(
        body, name=name, grid=(bsz, seq // tm),
        in_specs=[_row(tm, CW, 0), _row(tm, CW, 1), _row(tm, CW, 2), _prev_halo(tm, 1), _prev_halo(tm, 2),
                  _full2((8, CW))],
        out_specs=_row(tm, CW), out_shape=SDS((bsz, seq, CW), BF16),
        compiler_params=_cp("parallel", "parallel"),
    )(p1, p1, p1, p1, p1, convw8)


def _conv_bwd(dya, wco, p1, convw8, name):
    bsz, seq, _ = p1.shape
    tm = _pick_tile(seq, (512, 256, 128))
    nt = seq // tm
    ext_rows = tm + HALO

    def body(dya_ref, dyan_ref, wco_ref, gb_ref, gbn_ref, gc_ref, v_ref, gch_ref, vh_ref, w_ref, dp_ref, dw_ref):
        b = pl.program_id(0)
        i = pl.program_id(1)
        w = w_ref[...]
        gc = gc_ref[0].astype(F32)
        vv = v_ref[0].astype(F32)
        cv, cv1, cv2, conv = _conv_core(gc, vv, gch_ref[0].astype(F32), vh_ref[0].astype(F32), w, i == 0)
        dya_ext = jnp.concatenate([dya_ref[0], dyan_ref[0]], axis=0)
        dyain_ext = _dot_nt(dya_ext, wco_ref[...])
        gb_ext = jnp.concatenate([gb_ref[0], gbn_ref[0]], axis=0).astype(F32)
        rows = lax.broadcasted_iota(jnp.int32, (ext_rows, 1), 0)
        dconv_ext = jnp.where((rows < tm) | (i < nt - 1), dyain_ext * gb_ext, 0.0)
        dconv = dconv_ext[:tm]
        dconv1 = pltpu.roll(dconv_ext, ext_rows - 1, 0)[:tm]
        dconv2 = pltpu.roll(dconv_ext, ext_rows - 2, 0)[:tm]
        dcv = w[2:3] * dconv + w[1:2] * dconv1 + w[0:1] * dconv2
        dp_ref[0, :, 0:CW] = (dyain_ext[:tm] * conv).astype(BF16)
        dp_ref[0, :, CW:2 * CW] = (dcv * vv).astype(BF16)
        dp_ref[0, :, 2 * CW:3 * CW] = (dcv * gc).astype(BF16)
        g0 = jnp.sum(dconv * cv2, axis=0, keepdims=True)
        g1 = jnp.sum(dconv * cv1, axis=0, keepdims=True)
        g2 = jnp.sum(dconv * cv, axis=0, keepdims=True)
        upd = jnp.concatenate([g0, g1, g2, jnp.zeros((5, CW), F32)], axis=0)

        @pl.when((i == 0) & (b == 0))
        def _():
            dw_ref[...] = upd

        @pl.when((i > 0) | (b > 0))
        def _():
            dw_ref[...] += upd

    return pl.pallas_call(
        body, name=name, grid=(bsz, seq // tm),
        in_specs=[_row(tm, CW), _next_halo(tm, seq, 0), _full2((CW, D)),
                  _row(tm, CW, 0), _next_halo(tm, seq, 0), _row(tm, CW, 1), _row(tm, CW, 2),
                  _prev_halo(tm, 1), _prev_halo(tm, 2), _full2((8, CW))],
        out_specs=[_row(tm, 3 * CW), _full2((8, CW))],
        out_shape=[SDS((bsz, seq, 3 * CW), BF16), SDS((8, CW), F32)],
        compiler_params=_cp("arbitrary", "arbitrary"),
    )(dya, dya, wco, p1, p1, p1, p1, p1, p1, convw8)


def _disc(are, aim, ldt, bre, bim):
    dt = jnp.exp(ldt)
    mag = jnp.exp(are * dt)
    ang = aim * dt
    abr = mag * jnp.cos(ang)
    abi = mag * jnp.sin(ang)
    nr = abr - 1.0
    den = are * are + aim * aim
    cr = (nr * are + abi * aim) / den
    ci = (abi * are - nr * aim) / den
    return abr, abi, cr * bre - ci * bim, cr * bim + ci * bre


def _ssm_disc(are, aim, ldt, bre_t, bim_t):
    def body(are_ref, aim_ref, ldt_ref, bre_ref, bim_ref, abr_ref, abi_ref, bbr_ref, bbi_ref):
        abr, abi, bbr, bbi = _disc(are_ref[...], aim_ref[...], ldt_ref[...], bre_ref[...], bim_ref[...])
        abr_ref[...] = abr
        abi_ref[...] = abi
        bbr_ref[...] = bbr
        bbi_ref[...] = bbi

    v1, vh = SDS((1, GP), F32), SDS((NH, GP), F32)
    return pl.pallas_call(body, name="ssm_disc", out_shape=[v1, v1, vh, vh], compiler_params=_cp())(
        are, aim, ldt, bre_t, bim_t)


def _ssm_disc_bwd(are, aim, ldt, bre_t, bim_t, dabr, dabi, dbbr, dbbi):
    def body(are_ref, aim_ref, ldt_ref, bre_ref, bim_ref, g0, g1, g2, g3, o0, o1, o2, o3, o4):
        prim = (are_ref[...], aim_ref[...], ldt_ref[...], bre_ref[...], bim_ref[...])
        _, vjp = jax.vjp(_disc, *prim)
        d_are, d_aim, d_ldt, d_bre, d_bim = vjp((g0[...], g1[...], g2[...], g3[...]))
        o0[...] = d_are
        o1[...] = d_aim
        o2[...] = d_ldt
        o3[...] = d_bre
        o4[...] = d_bim

    v1, vh = SDS((1, GP), F32), SDS((NH, GP), F32)
    return pl.pallas_call(body, name="ssm_disc_bwd", out_shape=[v1, v1, v1, vh, vh], compiler_params=_cp())(
        are, aim, ldt, bre_t, bim_t, dabr, dabi, dbbr, dbbi)


def _scan_chunk(buf_ref, nt, bsz, ar, ai, init_r, init_i, reverse):
    nsub = SUB // bsz
    row = lax.broadcasted_iota(jnp.int32, (SUB, GP), 0)
    shift = ((SUB - bsz) if reverse else bsz) % SUB
    order = list(range(nsub - 1, -1, -1)) if reverse else list(range(nsub))

    def step(j, carry):
        pr, pi = carry
        jj = (nt - 1 - j) if reverse else j
        off = pl.multiple_of(jj * SUB, SUB)
        br = buf_ref[pl.ds(off, SUB), 0:GP]
        bi = buf_ref[pl.ds(off, SUB), GP:2 * GP]
        nr, ni = pr, pi
        for s in order:
            sr, si = nr, ni
            if shift:
                sr = pltpu.roll(sr, shift, 0)
                si = pltpu.roll(si, shift, 0)
            cr = ar * sr - ai * si + br
            ci = ar * si + ai * sr + bi
            if nsub == 1:
                nr, ni = cr, ci
            else:
                m = (row >= s * bsz) & (row < (s + 1) * bsz)
                nr = jnp.where(m, cr, nr)
                ni = jnp.where(m, ci, ni)
        buf_ref[pl.ds(off, SUB), 0:GP] = nr
        buf_ref[pl.ds(off, SUB), GP:2 * GP] = ni
        return nr, ni

    return lax.fori_loop(0, nt, step, (init_r, init_i))


def _ssm_chunk_rows(total_rows, bsz):
    return min(total_rows, 64 * bsz)


def _interleave(src_ref, tmp_ref, bsz, steps):
    nl = tmp_ref.shape[0]
    for b in range(bsz):
        for j in range(nl):
            tmp_ref.at[j][pl.ds(b, steps, stride=bsz), :] = src_ref[b, :, j * LANE:(j + 1) * LANE]
    return jnp.concatenate([tmp_ref[j] for j in range(nl)], axis=1)


def _deinterleave(val, tmp_ref, dst_ref, bsz, steps):
    nl = tmp_ref.shape[0]
    for j in range(nl):
        tmp_ref[j] = val[:, j * LANE:(j + 1) * LANE]
    for b in range(bsz):
        for j in range(nl):
            dst_ref[b, :, j * LANE:(j + 1) * LANE] = tmp_ref.at[j][pl.ds(b, steps, stride=bsz), :]


def _ssm_fwd(u, wb, wct, ar8, ai8, name):
    bsz, seq, _ = u.shape
    rt = seq * bsz
    r = _ssm_chunk_rows(rt, bsz)
    nt = r // SUB
    steps = r // bsz

    def body(u_ref, wb_hbm, wct_hbm, ar_ref, ai_ref, x_ref, y_ref, wb_ref, wct_ref, st_ref, tmp_ref):
        @pl.when(pl.program_id(0) == 0)
        def _():
            pltpu.sync_copy(wb_hbm, wb_ref)
            pltpu.sync_copy(wct_hbm, wct_ref)
            st_ref[...] = jnp.zeros_like(st_ref)

        x_ref[...] = _dot(_interleave(u_ref, tmp_ref, bsz, steps).astype(BF16), wb_ref[...])
        fr, fi = _scan_chunk(x_ref, nt, bsz, ar_ref[...], ai_ref[...], st_ref[:, 0:GP], st_ref[:, GP:2 * GP], False)
        st_ref[:, 0:GP] = fr
        st_ref[:, GP:2 * GP] = fi
        _deinterleave(_dot_nt(x_ref[...].astype(BF16), wct_ref[...]), tmp_ref, y_ref, bsz, steps)

    anyspec = pl.BlockSpec(memory_space=pl.ANY)
    seqs = pl.BlockSpec((bsz, steps, SW), lambda i: (0, i, 0))
    return pl.pallas_call(
        body, name=name, grid=(rt // r,),
        in_specs=[seqs, anyspec, anyspec, _full2((SUB, GP)), _full2((SUB, GP))],
        out_specs=[pl.BlockSpec((r, 2 * GP), lambda i: (i, 0)), seqs],
        out_shape=[SDS((rt, 2 * GP), F32), SDS((bsz, seq, SW), F32)],
        scratch_shapes=[pltpu.VMEM((SW, 2 * GP), BF16), pltpu.VMEM((SW, 2 * GP), BF16), pltpu.VMEM((SUB, 2 * GP), F32),
                        pltpu.VMEM((SW // LANE, r, LANE), F32)],
        compiler_params=_cp("arbitrary"),
    )(u, wb, wct, ar8, ai8)


def _ssm_bwd(dy, u, xs, wb, wct, ar8, ai8, name):
    bsz, seq, _ = u.shape
    rt = seq * bsz
    r = _ssm_chunk_rows(rt, bsz)
    nt = r // SUB
    nc = rt // r
    steps = r // bsz
    ub = SW // SSM_SUPER
    sb = GP // SSM_SUPER

    def body(dys_ref, us_ref, x_ref, xh_ref, wb_hbm, wct_hbm, ar_ref, ai_ref,
             dus_ref, dwb_hbm, dwct_hbm, dar_ref, dai_ref, wb_ref, wct_ref, g_ref, st_ref, awb_ref, awct_ref,
             tmp_ref):
        i = pl.program_id(0)
        dyb = _interleave(dys_ref, tmp_ref, bsz, steps).astype(BF16)
        ub16 = _interleave(us_ref, tmp_ref, bsz, steps).astype(BF16)

        @pl.when(i == 0)
        def _():
            pltpu.sync_copy(wb_hbm, wb_ref)
            pltpu.sync_copy(wct_hbm, wct_ref)
            st_ref[...] = jnp.zeros_like(st_ref)
            awb_ref[...] = jnp.zeros_like(awb_ref)
            awct_ref[...] = jnp.zeros_like(awct_ref)
            dar_ref[...] = jnp.zeros_like(dar_ref)
            dai_ref[...] = jnp.zeros_like(dai_ref)

        g_ref[...] = _dot(dyb, wct_ref[...])
        ar = ar_ref[...]
        ai = ai_ref[...]
        fr, fi = _scan_chunk(g_ref, nt, bsz, ar, -ai, st_ref[:, 0:GP], st_ref[:, GP:2 * GP], True)
        st_ref[:, 0:GP] = fr
        st_ref[:, GP:2 * GP] = fi

        gb = g_ref[...].astype(BF16)
        _deinterleave(_dot_nt(gb, wb_ref[...]), tmp_ref, dus_ref, bsz, steps)
        xb16 = x_ref[...].astype(BF16)
        for s in range(SSM_SUPER):
            us = ub16[:, s * ub:(s + 1) * ub]
            ds = dyb[:, s * ub:(s + 1) * ub]
            for half in range(2):
                cols = slice(half * GP + s * sb, half * GP + (s + 1) * sb)
                ocols = slice(half * sb, (half + 1) * sb)
                awb_ref[s * ub:(s + 1) * ub, ocols] += _dot_tn(us, gb[:, cols])
                awct_ref[s * ub:(s + 1) * ub, ocols] += _dot_tn(ds, xb16[:, cols])

        gr = g_ref[:, 0:GP]
        gi = g_ref[:, GP:2 * GP]
        xsr = pltpu.roll(x_ref[:, 0:GP], bsz, 0)
        xsi = pltpu.roll(x_ref[:, GP:2 * GP], bsz, 0)
        inner = lax.broadcasted_iota(jnp.int32, (r, 1), 0) >= bsz
        t_r = jnp.where(inner, gr * xsr + gi * xsi, 0.0)
        t_i = jnp.where(inner, gi * xsr - gr * xsi, 0.0)
        acc_r = jnp.sum(t_r.reshape(nt, SUB, GP), axis=0)
        acc_i = jnp.sum(t_i.reshape(nt, SUB, GP), axis=0)
        hr = xh_ref[:, 0:GP]
        hi = xh_ref[:, GP:2 * GP]
        if bsz % SUB:
            hr = pltpu.roll(hr, bsz, 0)
            hi = pltpu.roll(hi, bsz, 0)
        edge = (lax.broadcasted_iota(jnp.int32, (SUB, 1), 0) < bsz) & (i < nc - 1)
        g0r = g_ref[0:SUB, 0:GP]
        g0i = g_ref[0:SUB, GP:2 * GP]
        dar_ref[...] += acc_r + jnp.where(edge, g0r * hr + g0i * hi, 0.0)
        dai_ref[...] += acc_i + jnp.where(edge, g0i * hr - g0r * hi, 0.0)

        @pl.when(i == nc - 1)
        def _():
            pltpu.sync_copy(awb_ref, dwb_hbm)
            pltpu.sync_copy(awct_ref, dwct_hbm)

    anyspec = pl.BlockSpec(memory_space=pl.ANY)
    rev = lambda i: (nc - 1 - i, 0)
    seqs = pl.BlockSpec((bsz, steps, SW), lambda i: (0, nc - 1 - i, 0))
    wshape = (SW, 2 * sb)
    return pl.pallas_call(
        body, name=name, grid=(nc,),
        in_specs=[seqs, seqs, pl.BlockSpec((r, 2 * GP), rev),
                  pl.BlockSpec((SUB, 2 * GP), lambda i: (jnp.maximum((nc - 1 - i) * nt - 1, 0), 0)),
                  anyspec, anyspec, _full2((SUB, GP)), _full2((SUB, GP))],
        out_specs=[seqs, anyspec, anyspec, _full2((SUB, GP)), _full2((SUB, GP))],
        out_shape=[SDS((bsz, seq, SW), F32), SDS(wshape, F32), SDS(wshape, F32), SDS((SUB, GP), F32),
                   SDS((SUB, GP), F32)],
        scratch_shapes=[pltpu.VMEM((SW, 2 * GP), BF16), pltpu.VMEM((SW, 2 * GP), BF16),
                        pltpu.VMEM((r, 2 * GP), F32), pltpu.VMEM((SUB, 2 * GP), F32),
                        pltpu.VMEM(wshape, F32), pltpu.VMEM(wshape, F32),
                        pltpu.VMEM((SW // LANE, r, LANE), F32)],
        compiler_params=_cp("arbitrary"),
    )(dy, u, xs, xs, wb, wct, ar8, ai8)


GELU_C = math.sqrt(2.0 / math.pi)


def _gelu(x):
    return 0.5 * x * (1.0 + jnp.tanh(GELU_C * (x + 0.044715 * x * x * x)))


def _gelu_grad(x):
    th = jnp.tanh(GELU_C * (x + 0.044715 * x * x * x))
    return 0.5 * (1.0 + th) + 0.5 * x * (1.0 - th * th) * GELU_C * (1.0 + 3.0 * 0.044715 * x * x)


def _ssm_post(ys, u, dskip, wglu, wso, name):
    bsz, seq, _ = ys.shape
    tm = _pick_tile(seq, (512, 256, 128))

    def body(ys_ref, u_ref, d_ref, wg_ref, wo_ref, s0_ref, z_ref, s1_ref, s2_ref, yb_ref):
        s0 = ys_ref[0] + d_ref[...] * u_ref[0]
        s1 = _gelu(s0)
        s1b = s1.astype(BF16)
        z = _dot(s1b, wg_ref[...])
        s2b = (s1 * _sigmoid(z)).astype(BF16)
        s0_ref[0] = s0
        z_ref[0] = z
        s1_ref[0] = s1b
        s2_ref[0] = s2b
        yb_ref[0] = _dot(s2b, wo_ref[...]).astype(BF16)

    return pl.pallas_call(
        body, name=name, grid=(bsz, seq // tm),
        in_specs=[_row(tm, SW), _row(tm, SW), _full2((1, SW)), _full2((SW, SW)), _full2((SW, D))],
        out_specs=[_row(tm, SW), _row(tm, SW), _row(tm, SW), _row(tm, SW), _row(tm, D)],
        out_shape=[SDS((bsz, seq, SW), F32), SDS((bsz, seq, SW), F32), SDS((bsz, seq, SW), BF16),
                   SDS((bsz, seq, SW), BF16), SDS((bsz, seq, D), BF16)],
        compiler_params=_cp("parallel", "parallel"),
    )(ys, u, dskip, wglu, wso)


def _ssm_post_bwd(dyb, s0, z, u, dskip, wglu, wso, name):
    bsz, seq, _ = s0.shape
    tm = _pick_tile(seq, (512, 256, 128))

    def body(dyb_ref, s0_ref, z_ref, u_ref, wg_ref, wo_ref, ds0_ref, dz_ref, dd_ref):
        b = pl.program_id(0)
        i = pl.program_id(1)
        ds2 = _dot_nt(dyb_ref[0], wo_ref[...])
        s0 = s0_ref[0]
        s1 = _gelu(s0)
        sg = _sigmoid(z_ref[0])
        dz = ds2 * s1 * sg * (1.0 - sg)
        dzb = dz.astype(BF16)
        ds1 = ds2 * sg + _dot_nt(dzb, wg_ref[...])
        ds0 = ds1 * _gelu_grad(s0)
        ds0_ref[0] = ds0
        dz_ref[0] = dzb
        part = jnp.sum(ds0 * u_ref[0], axis=0, keepdims=True)

        @pl.when((i == 0) & (b == 0))
        def _():
            dd_ref[...] = part

        @pl.when((i > 0) | (b > 0))
        def _():
            dd_ref[...] += part

    del dskip
    return pl.pallas_call(
        body, name=name, grid=(bsz, seq // tm),
        in_specs=[_row(tm, D), _row(tm, SW), _row(tm, SW), _row(tm, SW), _full2((SW, SW)), _full2((SW, D))],
        out_specs=[_row(tm, SW), _row(tm, SW), _full2((1, SW))],
        out_shape=[SDS((bsz, seq, SW), F32), SDS((bsz, seq, SW), BF16), SDS((1, SW), F32)],
        compiler_params=_cp("arbitrary", "arbitrary"),
    )(dyb, s0, z, u, wglu, wso)


def _du_combine(du_ssm, ds0, dskip, name):
    bsz, seq, _ = ds0.shape
    tm = _pick_tile(seq, (512, 256, 128))

    def body(a_ref, b_ref, d_ref, o_ref):
        o_ref[0] = (a_ref[0] + b_ref[0] * d_ref[...]).astype(BF16)

    return pl.pallas_call(
        body, name=name, grid=(bsz, seq // tm),
        in_specs=[_row(tm, SW), _row(tm, SW), _full2((1, SW))],
        out_specs=_row(tm, SW), out_shape=SDS((bsz, seq, SW), BF16),
        compiler_params=_cp("parallel", "parallel"),
    )(du_ssm, ds0, dskip)


def _merge_out(ya, yb, p3, wout, x1, gt, name):
    bsz, seq, _ = ya.shape
    tm = _pick_tile(seq, (512, 256, 128))

    def body(ya_ref, yb_ref, ga_ref, gbb_ref, w_ref, x_ref, gt_ref, mg_ref, mix_ref, xo_ref):
        merged = (_sigmoid(ga_ref[0].astype(F32)) * ya_ref[0].astype(F32)
                  + _sigmoid(gbb_ref[0].astype(F32)) * yb_ref[0].astype(F32)).astype(BF16)
        mix = _dot(merged, w_ref[...])
        mg_ref[0] = merged
        mix_ref[0] = mix
        xo_ref[0] = x_ref[0] + gt_ref[0] * mix

    return pl.pallas_call(
        body, name=name, grid=(bsz, seq // tm),
        in_specs=[_row(tm, D), _row(tm, D), _row(tm, D, 0), _row(tm, D, 1), _full2((D, D)), _row(tm, D), _seqvec(D)],
        out_specs=[_row(tm, D), _row(tm, D), _row(tm, D)],
        out_shape=[SDS((bsz, seq, D), BF16), SDS((bsz, seq, D), F32), SDS((bsz, seq, D), F32)],
        compiler_params=_cp("parallel", "parallel"),
    )(ya, yb, p3, p3, wout, x1, gt)


def _merge_bwd(dx2, gt, mix, ya, yb, p3, wout, name):
    bsz, seq, _ = ya.shape
    tm = _pick_tile(seq, (512, 256, 128))

    def body(dx_ref, gt_ref, mix_ref, ya_ref, yb_ref, ga_ref, gbb_ref, w_ref, dmix_ref, dya_ref, dyb_ref, dp_ref, dgt_ref):
        i = pl.program_id(1)
        dx = dx_ref[0]
        dmix = (gt_ref[0] * dx).astype(BF16)
        dmix_ref[0] = dmix
        part = jnp.sum(dx * mix_ref[0], axis=0, keepdims=True)

        @pl.when(i == 0)
        def _():
            dgt_ref[0] = part

        @pl.when(i > 0)
        def _():
            dgt_ref[0] += part

        dmg = _dot_nt(dmix, w_ref[...])
        sa = _sigmoid(ga_ref[0].astype(F32))
        sb = _sigmoid(gbb_ref[0].astype(F32))
        dya_ref[0] = (dmg * sa).astype(BF16)
        dyb_ref[0] = (dmg * sb).astype(BF16)
        dp_ref[0, :, 0:D] = (dmg * ya_ref[0].astype(F32) * sa * (1.0 - sa)).astype(BF16)
        dp_ref[0, :, D:2 * D] = (dmg * yb_ref[0].astype(F32) * sb * (1.0 - sb)).astype(BF16)

    bshape = SDS((bsz, seq, D), BF16)
    return pl.pallas_call(
        body, name=name, grid=(bsz, seq // tm),
        in_specs=[_row(tm, D), _seqvec(D), _row(tm, D), _row(tm, D), _row(tm, D), _row(tm, D, 0), _row(tm, D, 1),
                  _full2((D, D))],
        out_specs=[_row(tm, D), _row(tm, D), _row(tm, D), _row(tm, 2 * D), _seqvec(D)],
        out_shape=[bshape, bshape, bshape, SDS((bsz, seq, 2 * D), BF16), SDS((bsz, 1, D), F32)],
        compiler_params=_cp("arbitrary", "arbitrary"),
    )(dx2, gt, mix, ya, yb, p3, p3, wout)


def _final_loss(x3, gfin, target, name):
    bsz, seq, dm = x3.shape
    tm = _pick_tile(seq, (512, 256, 128))

    def body(x_ref, g_ref, t_ref, dx_ref, loss_ref, dg_ref):
        b = pl.program_id(0)
        i = pl.program_id(1)
        xf = x_ref[0]
        gv = g_ref[...]
        r = lax.rsqrt(jnp.mean(xf * xf, axis=-1, keepdims=True) + EPS)
        xhat = xf * r
        e = xhat * gv - t_ref[0]
        dy = e * (1.0 / dm)
        dxh = dy * gv
        dx_ref[0] = r * (dxh - xhat * jnp.mean(dxh * xhat, axis=-1, keepdims=True))
        p_l = jnp.sum(e * e, axis=0, keepdims=True) * (0.5 / dm)
        p_g = jnp.sum(dy * xhat, axis=0, keepdims=True)

        @pl.when((i == 0) & (b == 0))
        def _():
            loss_ref[...] = p_l
            dg_ref[...] = p_g

        @pl.when((i > 0) | (b > 0))
        def _():
            loss_ref[...] += p_l
            dg_ref[...] += p_g

    return pl.pallas_call(
        body, name=name, grid=(bsz, seq // tm),
        in_specs=[_row(tm, dm), _full2((1, dm)), _row(tm, dm)],
        out_specs=[_row(tm, dm), _full2((1, dm)), _full2((1, dm))],
        out_shape=[SDS((bsz, seq, dm), F32), SDS((1, dm), F32), SDS((1, dm), F32)],
        compiler_params=_cp("arbitrary", "arbitrary"),
    )(x3, gfin, target)


def _ada_fwd(c_all, w_shard, b_shard):
    nb = c_all.shape[0]
    n = w_shard.shape[2]

    def body(c_ref, w_ref, b_ref, o_ref):
        cv = c_ref[...]
        cond = (cv * _sigmoid(cv)).astype(BF16)
        o_ref[...] = _dot(cond, w_ref[0].astype(BF16)) + b_ref[...]

    return pl.pallas_call(body, name="ada_fwd", out_shape=SDS((nb, n), F32), compiler_params=_cp())(
        c_all, w_shard, b_shard)


def _ada_bwd(c_all, dmod_shard, dmod_all):
    n = dmod_shard.shape[1]

    def body(c_ref, ds_ref, da_ref, gw_ref, gb_ref):
        cv = c_ref[...]
        cond = (cv * _sigmoid(cv)).astype(BF16)
        gw_ref[...] = _dot_tn(cond, ds_ref[...].astype(BF16))
        gb_ref[...] = jnp.sum(da_ref[...], axis=0, keepdims=True)

    return pl.pallas_call(
        body, name="ada_bwd", out_shape=[SDS((D, n), F32), SDS((1, dmod_all.shape[1]), F32)], compiler_params=_cp(),
    )(c_all, dmod_shard, dmod_all)


def _adamw_math(w, g, m, v):
    m = B1 * m + (1.0 - B1) * g
    v = B2 * v + (1.0 - B2) * (g * g)
    delta = -LR * ((m / BC1) / (jnp.sqrt(v / BC2) + AEPS) + WD * w)
    return delta, m, v


def _adamw_big(w, m, v, recv_own, recv_sib, name):
    _, rows, cols = w.shape
    tr = _pick_tile(rows, tuple(t for t in (512, 256, 128, 64, 32, 16, 8) if t * cols <= 96 * 1024))

    def body(w_ref, m_ref, v_ref, a_ref, b_ref, g_ref, d_ref, mo_ref, vo_ref):
        def chip_sum(r):
            acc = r[0].astype(F32)
            for k in range(1, N_CHIPS):
                acc = acc + r[k].astype(F32)
            return acc

        g = chip_sum(a_ref) + chip_sum(b_ref)
        delta, mn, vn = _adamw_math(w_ref[0], g, m_ref[0], v_ref[0])
        g_ref[0] = g
        d_ref[0] = delta
        mo_ref[0] = mn
        vo_ref[0] = vn

    own = pl.BlockSpec((1, tr, cols), lambda i: (0, i, 0))
    rspec = pl.BlockSpec((N_CHIPS, tr, cols), lambda i: (0, i, 0))
    shp = SDS(w.shape, F32)
    return pl.pallas_call(
        body, name=name, grid=(rows // tr,),
        in_specs=[own, own, own, rspec, rspec], out_specs=[own, own, own, own], out_shape=[shp, shp, shp, shp],
        compiler_params=_cp("parallel"),
    )(w, m, v, recv_own, recv_sib)


def _adamw_plain(w, m, v, g, name):
    def body(w_ref, m_ref, v_ref, g_ref, d_ref, mo_ref, vo_ref):
        delta, mn, vn = _adamw_math(w_ref[...], g_ref[...], m_ref[...], v_ref[...])
        d_ref[...] = delta
        mo_ref[...] = mn
        vo_ref[...] = vn

    shp = SDS(w.shape, F32)
    return pl.pallas_call(body, name=name, out_shape=[shp, shp, shp], compiler_params=_cp())(w, m, v, g)


def _adamw_rows(w, m, v, g, name):
    _, rows, cols = w.shape
    tr = _pick_tile(rows, (128, 64, 32, 16, 8))

    def body(w_ref, m_ref, v_ref, g_ref, d_ref, mo_ref, vo_ref):
        delta, mn, vn = _adamw_math(w_ref[0], g_ref[...], m_ref[0], v_ref[0])
        d_ref[0] = delta
        mo_ref[0] = mn
        vo_ref[0] = vn

    spec = pl.BlockSpec((1, tr, cols), lambda i: (0, i, 0))
    shp = SDS(w.shape, F32)
    return pl.pallas_call(
        body, name=name, grid=(rows // tr,), in_specs=[spec] * 3 + [pl.BlockSpec((tr, cols), lambda i: (i, 0))],
        out_specs=[spec] * 3, out_shape=[shp] * 3, compiler_params=_cp("parallel"),
    )(w, m, v, g)


def _sum_slabs(r, name):
    n, rows, cols = r.shape
    tr = _pick_tile(rows, (256, 128, 64))

    def body(r_ref, o_ref):
        acc = r_ref[0].astype(F32)
        for j in range(1, n):
            acc = acc + r_ref[j].astype(F32)
        o_ref[...] = acc

    return pl.pallas_call(
        body, name=name, grid=(rows // tr,),
        in_specs=[pl.BlockSpec((n, tr, cols), lambda i: (0, i, 0))],
        out_specs=pl.BlockSpec((tr, cols), lambda i: (i, 0)), out_shape=SDS((rows, cols), F32),
        compiler_params=_cp("parallel"),
    )(r)


def _place():
    return lax.axis_index("x"), lax.axis_index("y"), lax.axis_index("c")


def _all_gather8(blk, name):
    m_per, n = blk.shape

    def body(x_ref, out_ref, send_sems, recv_sems, local_sem):
        x, y, c = _place()
        me, sibling = (x, y, c), (x, y, 1 - c)
        chips = [(1 - x, y), (x, 1 - y), (1 - x, 1 - y)]

        def rows(px, py, pc):
            return out_ref.at[pl.ds((4 * px + 2 * py + pc) * m_per, m_per), :]

        def copy(k, block, to, src=None):
            return pltpu.make_async_remote_copy(
                src_ref=rows(*block) if src is None else src, dst_ref=rows(*block),
                send_sem=send_sems.at[k], recv_sem=recv_sems.at[k], device_id=to, device_id_type=MESH)

        mine = pltpu.make_async_copy(x_ref, rows(*me), local_sem)
        mine.start()
        first = [copy(0, me, sibling, src=x_ref)]
        first += [copy(1 + j, me, (*chip, c), src=x_ref) for j, chip in enumerate(chips)]
        for cp in first:
            cp.start()
        passed = [copy(4 + j, (*chip, c), sibling) for j, chip in enumerate(chips)]
        for j, chip in enumerate(chips):
            copy(1 + j, (*chip, c), me).wait_recv()
            passed[j].start()
        copy(0, sibling, me).wait_recv()
        for j, chip in enumerate(chips):
            copy(4 + j, (*chip, 1 - c), me).wait_recv()
        for cp in first + passed:
            cp.wait_send()
        mine.wait()

    return pl.pallas_call(
        body, name=name, out_shape=SDS((N_DEV * m_per, n), blk.dtype),
        in_specs=[pl.BlockSpec(memory_space=pltpu.VMEM)], out_specs=pl.BlockSpec(memory_space=pltpu.VMEM),
        scratch_shapes=[pltpu.SemaphoreType.DMA((7,)), pltpu.SemaphoreType.DMA((7,)), pltpu.SemaphoreType.DMA],
        compiler_params=pltpu.CompilerParams(vmem_limit_bytes=VMEM_LIMIT),
    )(blk)


def _chip_peers(x, y):
    return [(1 - x, y), (x, 1 - y), (1 - x, 1 - y)]


def _peer_copies(src_refs, land_refs, send_sems, recv_sems, scatter, landed):
    x, y, c = _place()
    cps = []
    for a, (src_ref, land_ref) in enumerate(zip(src_refs, land_refs)):
        for j, (px, py) in enumerate(_chip_peers(x, y)):
            src = src_ref.at[2 * px + py] if scatter else src_ref
            dst = land_ref.at[2 * px + py] if landed else land_ref.at[2 * x + y]
            cps.append(pltpu.make_async_remote_copy(
                src_ref=src, dst_ref=dst, send_sem=send_sems.at[3 * a + j], recv_sem=recv_sems.at[3 * a + j],
                device_id=(px, py, c), device_id_type=MESH))
    return cps


def _exchange_chips(srcs, scatter, name):
    n = len(srcs)

    def body(*refs):
        src_refs, land_refs = refs[:n], refs[n:2 * n]
        send_sems, recv_sems, local_sems = refs[2 * n:]
        x, y, _ = _place()
        me = 2 * x + y
        mine = [pltpu.make_async_copy(s.at[me] if scatter else s, l.at[me], local_sems.at[a])
                for a, (s, l) in enumerate(zip(src_refs, land_refs))]
        for cp in mine:
            cp.start()
        out = _peer_copies(src_refs, land_refs, send_sems, recv_sems, scatter, False)
        for cp in out:
            cp.start()
        for cp in _peer_copies(src_refs, land_refs, send_sems, recv_sems, scatter, True):
            cp.wait_recv()
        for cp in out:
            cp.wait_send()
        for cp in mine:
            cp.wait()

    anyspec = pl.BlockSpec(memory_space=pl.ANY)
    shapes = [SDS(s.shape if scatter else (N_CHIPS,) + s.shape, s.dtype) for s in srcs]
    return pl.pallas_call(
        body, name=name, out_shape=shapes, in_specs=[anyspec] * n, out_specs=[anyspec] * n,
        scratch_shapes=[pltpu.SemaphoreType.DMA((3 * n,)), pltpu.SemaphoreType.DMA((3 * n,)),
                        pltpu.SemaphoreType.DMA((n,))],
        compiler_params=pltpu.CompilerParams(vmem_limit_bytes=VMEM_LIMIT),
    )(*srcs)


_HBM = pl.BlockSpec(memory_space=pltpu.HBM)
_SEM = pl.BlockSpec(memory_space=pltpu.SEMAPHORE)
_EFFECT = pltpu.SideEffectType.DATAFLOW_SIDE_EFFECTING


def _exchange_begin(srcs, lands, scatter, name):
    n = len(srcs)

    def body(*refs):
        src_refs, land_refs = refs[:n], refs[n:2 * n]
        send_sems, recv_sems = refs[2 * n:2 * n + 2]
        token = refs[-1]
        for cp in _peer_copies(src_refs, land_refs, send_sems, recv_sems, scatter, False):
            cp.start()
        token[...] = jnp.zeros_like(token)

    res = pl.pallas_call(
        body, name=name,
        out_shape=(pltpu.SemaphoreType.DMA((3 * n,)), pltpu.SemaphoreType.DMA((3 * n,)),
                   *[pltpu.HBM(s.shape, s.dtype) for s in srcs], *[pltpu.HBM(l.shape, l.dtype) for l in lands],
                   SDS((SUB, LANE), F32)),
        in_specs=[_HBM] * (2 * n), out_specs=(_SEM, _SEM, *[_HBM] * (2 * n), pl.BlockSpec(memory_space=pltpu.VMEM)),
        input_output_aliases={i: 2 + i for i in range(2 * n)},
        compiler_params=pltpu.CompilerParams(has_side_effects=_EFFECT),
    )(*[pltpu.with_memory_space_constraint(a, pltpu.HBM) for a in (*srcs, *lands)])
    return res[0], res[1], res[2:2 + n], res[2 + n:2 + 2 * n], res[-1]


def _exchange_end(handle, after, scatter, name):
    send_sems, recv_sems, srcs, lands, _ = handle
    n = len(srcs)

    def body(*refs):
        src_refs, land_refs = refs[:n], refs[n:2 * n]
        send_sems, recv_sems = refs[2 * n:2 * n + 2]
        for cp in _peer_copies(src_refs, land_refs, send_sems, recv_sems, scatter, True):
            cp.wait_send()
            cp.wait_recv()

    res = pl.pallas_call(
        body, name=name,
        out_shape=tuple(pltpu.HBM(a.shape, a.dtype) for a in (*srcs, *lands)),
        in_specs=[_HBM] * (2 * n) + [_SEM, _SEM, pl.BlockSpec(memory_space=pl.ANY)], out_specs=tuple([_HBM] * (2 * n)),
        input_output_aliases={i: i for i in range(2 * n)},
        compiler_params=pltpu.CompilerParams(has_side_effects=_EFFECT),
    )(*srcs, *lands, send_sems, recv_sems, after)
    return list(res[n:])


def _own_slab(stack4, chip):
    idx = lax.broadcasted_iota(jnp.int32, (N_CHIPS,) + (1,) * (stack4.ndim - 1), 0)
    return jnp.where(idx == chip, stack4, jnp.zeros((), stack4.dtype))


def _swap_sibling(vs, name):
    n = len(vs)

    def body(*refs):
        in_refs, out_refs = refs[:n], refs[n:2 * n]
        send_sems, recv_sems = refs[2 * n:]
        x, y, c = _place()
        cps = [pltpu.make_async_remote_copy(
            src_ref=i, dst_ref=o, send_sem=send_sems.at[a], recv_sem=recv_sems.at[a],
            device_id=(x, y, 1 - c), device_id_type=MESH) for a, (i, o) in enumerate(zip(in_refs, out_refs))]
        for cp in cps:
            cp.start()
        for cp in cps:
            cp.wait()

    anyspec = pl.BlockSpec(memory_space=pl.ANY)
    return pl.pallas_call(
        body, name=name, out_shape=[SDS(v.shape, v.dtype) for v in vs], in_specs=[anyspec] * n, out_specs=[anyspec] * n,
        scratch_shapes=[pltpu.SemaphoreType.DMA((n,)), pltpu.SemaphoreType.DMA((n,))],
        compiler_params=pltpu.CompilerParams(vmem_limit_bytes=VMEM_LIMIT),
    )(*vs)


def _select(stacked, idx):
    out = stacked[0]
    for j in range(1, stacked.shape[0]):
        out = jnp.where(idx == j, stacked[j], out)
    return out


BIG = (
    ("w1_a", DFF // 4, D, False), ("w3_a", DFF // 4, D, False), ("w2_a", DFF // 4, D, False),
    ("w_in", D, 5632 // 4, True), ("w_conv_out", CW // 4, D, False), ("w_glu", SW // 4, SW, False),
    ("w_ssm_out", SW, D // 4, True), ("w_out", D // 4, D, False),
    ("w1_b", DFF // 4, D, False), ("w3_b", DFF // 4, D, False), ("w2_b", DFF // 4, D, False),
)
TRANSPOSED = frozenset(("w1_a", "w3_a", "w1_b", "w3_b"))
PACK_COLS = 1024


def _view(a, name):
    return jnp.transpose(a, (0, 2, 1)) if name in TRANSPOSED else a


def _full_from_stacked(st, split_cols):
    _, rows, cols = st.shape
    if split_cols:
        return st.transpose(1, 0, 2).reshape(rows, N_CHIPS * cols)
    return st.reshape(N_CHIPS * rows, cols)


def _stacked_from_full(full, rows, cols, split_cols):
    if split_cols:
        return full.reshape(rows, N_CHIPS, cols).transpose(1, 0, 2)
    return full.reshape(N_CHIPS, rows, cols)


def _blockdiag(t):
    r = lax.broadcasted_iota(jnp.int32, (SW, GP), 0) // NH
    cidx = lax.broadcasted_iota(jnp.int32, (SW, GP), 1) // NP
    return jnp.where(r == cidx, jnp.tile(t, (NG, 1)), 0.0)


def _blockdiag_extract(acc):
    gs = NG // SSM_SUPER
    a = acc.reshape(NG, NH, gs, NP)
    sel = (lax.broadcasted_iota(jnp.int32, (NG, 1, gs, 1), 0) % gs) == lax.broadcasted_iota(jnp.int32, (NG, 1, gs, 1), 2)
    a = jnp.sum(jnp.where(sel, a, 0.0), axis=2)
    return a.transpose(1, 0, 2).reshape(NH, GP)


def _to_t(p):
    return p.transpose(2, 0, 1).reshape(NH, GP)


def _from_t(t):
    return t.reshape(NH, NG, NP).transpose(1, 2, 0)


def _c_to_t(p):
    return p.transpose(1, 0, 2).reshape(NH, GP)


def _c_from_t(t):
    return t.reshape(NH, NG, NP).transpose(1, 0, 2)


def _ffn_forward(x, g, sh, sc, gt, w1, w3, w2, tag):
    h = _norm_mod(x, g, sh, sc, f"{tag}_norm")
    a, b, hid = _swiglu_up(h, w1, w3, f"{tag}_up")
    w2 = w2(hid) if callable(w2) else w2
    f, xo = _ffn_down(hid, w2, x, gt, f"{tag}_down")
    return xo, (x, h, a, b, hid, f), w2


def _ffn_backward(dxo, saved, g, sc, gt, w1, w3, w2, tag, emit=lambda key, gw: 0.0):
    x, h, a, b, hid, f = saved
    dfs, da, db, dgt = _ffn_bwd_hid(dxo, gt, f, a, b, w2, f"{tag}_bwd_hid")
    h2 = _flat(h)
    gw2 = _mm(_flat(hid), _flat(dfs), ta=True, name=f"{tag}_gw2")
    tok = emit("w2", gw2)
    gw1 = _mm(_flat(da), h2, ta=True, name=f"{tag}_gw1")
    tok = tok + emit("w1", gw1)
    gw3 = _mm(_flat(db), h2, ta=True, name=f"{tag}_gw3")
    tok = tok + emit("w3", gw3)
    dx, dsh, dsc, dg = _dh_norm_bwd([da, db], [w1, w3], x, g, sc + tok, dxo, f"{tag}_bwd_dh", transposed=True)
    return dx, (dsh, dsc, dgt, dg), (gw1, gw3, gw2)


def kernel(x, c, w_ada, b_ada, g_ffn1, w1_a, w3_a, w2_a, g_mix, w_in, conv_w, w_conv_out, a_re, a_im, b_re, b_im, c_re, c_im, log_dt, d_skip, w_glu, w_ssm_out, w_out, g_ffn2, w1_b, w3_b, w2_b, g_final, loss_target, m_w_ada, m_b_ada, m_g_ffn1, m_w1_a, m_w3_a, m_w2_a, m_g_mix, m_w_in, m_conv_w, m_w_conv_out, m_a_re, m_a_im, m_b_re, m_b_im, m_c_re, m_c_im, m_log_dt, m_d_skip, m_w_glu, m_w_ssm_out, m_w_out, m_g_ffn2, m_w1_b, m_w3_b, m_w2_b, m_g_final, v_w_ada, v_b_ada, v_g_ffn1, v_w1_a, v_w3_a, v_w2_a, v_g_mix, v_w_in, v_conv_w, v_w_conv_out, v_a_re, v_a_im, v_b_re, v_b_im, v_c_re, v_c_im, v_log_dt, v_d_skip, v_w_glu, v_w_ssm_out, v_w_out, v_g_ffn2, v_w1_b, v_w3_b, v_w2_b, v_g_final):
    args = dict(locals())
    names = ["w_ada", "b_ada", "g_ffn1", "w1_a", "w3_a", "w2_a", "g_mix", "w_in", "conv_w", "w_conv_out", "a_re",
             "a_im", "b_re", "b_im", "c_re", "c_im", "log_dt", "d_skip", "w_glu", "w_ssm_out", "w_out", "g_ffn2",
             "w1_b", "w3_b", "w2_b", "g_final"]
    bsz, seq, _ = x.shape
    mx, my, mc = _place()
    chip = 2 * mx + my
    dev = 4 * mx + 2 * my + mc

    groups = (BIG[:3], BIG[3:8], BIG[8:])
    wfull = {}

    def shards_bf16(grp):
        return [_view(args[n], n)[0].astype(BF16) for n, _, _, _ in grp]

    def unpack_group(gathered, grp):
        for (n, _, _, split), st in zip(grp, gathered):
            wfull[n] = _full_from_stacked(st, split)

    up_grp, down_grp = groups[0][:2], groups[0][2:]
    up_gathered = _exchange_chips(shards_bf16(up_grp), False, "gather_w_ffn1_up")

    nmod_shard = NMOD * D // N_CHIPS
    c_all = _all_gather8(c.reshape(SUB, -1), "gather_c").reshape(N_DEV * bsz, D)
    b_shard = _select(b_ada.reshape(N_CHIPS, 1, nmod_shard), chip)
    mod_shard = _ada_fwd(c_all, w_ada, b_shard)
    nb = N_DEV * bsz
    cw_pad = jnp.pad(conv_w[0], ((0, SUB - 3), (0, nmod_shard - CW // N_CHIPS)))
    mod_st = _exchange_chips([jnp.concatenate([mod_shard, cw_pad], axis=0)], False, "gather_mod")[0]
    mod_all = mod_st[:, :nb].transpose(1, 0, 2).reshape(N_DEV, bsz, NMOD * D)
    mod = _select(mod_all, dev)

    down_src, mix_src, ffn2_src, mod, up_gathered = lax.optimization_barrier(
        (shards_bf16(down_grp), shards_bf16(groups[1]), shards_bf16(groups[2]), mod, up_gathered))
    unpack_group(up_gathered, up_grp)

    def gather_begin(srcs, name):
        lands = [_own_slab(jnp.broadcast_to(s[None], (N_CHIPS,) + s.shape), chip) for s in srcs]
        return _exchange_begin(srcs, lands, False, name)

    down_handle = gather_begin(down_src, "gather_w_ffn1_down_start")
    mix_handle = gather_begin(mix_src, "gather_w_mix_start")
    ffn2_handle = gather_begin(ffn2_src, "gather_w_ffn2_start")
    start_tokens = down_handle[4][0:1, 0:1] + mix_handle[4][0:1, 0:1] + ffn2_handle[4][0:1, 0:1]

    sh1, sc1, gt1, sh2, sc2, gt2, sh3, sc3, gt3 = [mod[:, None, j * D:(j + 1) * D] for j in range(NMOD)]
    convw = mod_st[:, nb:nb + 3, :CW // N_CHIPS].transpose(1, 0, 2).reshape(3, CW)
    convw8 = jnp.pad(convw, ((0, SUB - 3), (0, 0)))

    are, aim = a_re.reshape(1, GP), a_im.reshape(1, GP)
    ldt = jnp.broadcast_to(log_dt.reshape(NG, 1), (NG, NP)).reshape(1, GP)
    bre_t, bim_t = _to_t(b_re[0]), _to_t(b_im[0])
    abr, abi, bbr_t, bbi_t = _ssm_disc(are, aim, ldt, bre_t, bim_t)
    wb = jnp.concatenate([_blockdiag(bbr_t), _blockdiag(bbi_t)], axis=1).astype(BF16)
    wct = jnp.concatenate([_blockdiag(_c_to_t(c_re[0])), -_blockdiag(_c_to_t(c_im[0]))], axis=1).astype(BF16)
    ar8 = jnp.broadcast_to(abr, (SUB, GP))
    ai8 = jnp.broadcast_to(abi, (SUB, GP))

    def late_w2a(hid):
        unpack_group(_exchange_end(down_handle, hid, False, "gather_w_ffn1_down_wait"), down_grp)
        return wfull["w2_a"]

    x1, ffn1_saved, _ = _ffn_forward(
        x, g_ffn1 + start_tokens, sh1, sc1, gt1, wfull["w1_a"], wfull["w3_a"], late_w2a, "ffn1")
    unpack_group(_exchange_end(mix_handle, x1, False, "gather_w_mix_wait"), groups[1])

    h2 = _norm_mod(x1, g_mix, sh2, sc2, "mix_norm")
    h2f = _flat(h2)
    win = wfull["w_in"]
    win1, winu, win3 = win[:, :3 * CW], win[:, 3 * CW:3 * CW + SW], win[:, 3 * CW + SW:]
    p1 = _mm(h2f, win1, out_dtype=BF16, name="mix_in1").reshape(bsz, seq, 3 * CW)
    u = _mm(h2f, winu, name="mix_inu").reshape(bsz, seq, SW)
    p3 = _mm(h2f, win3, out_dtype=BF16, name="mix_in3").reshape(bsz, seq, 2 * D)

    ya_in = _conv_fwd(p1, convw8, "conv_fwd")
    ya = _mm(_flat(ya_in), wfull["w_conv_out"], out_dtype=BF16, name="conv_out").reshape(bsz, seq, D)

    xs, ys = _ssm_fwd(u, wb, wct, ar8, ai8, "ssm_fwd")
    s0, z, s1, s2, yb = _ssm_post(ys, u, d_skip, wfull["w_glu"], wfull["w_ssm_out"], "ssm_post")

    merged, mix, x2 = _merge_out(ya, yb, p3, wfull["w_out"], x1, gt2, "merge_out")
    unpack_group(_exchange_end(ffn2_handle, x2, False, "gather_w_ffn2_wait"), groups[2])
    x3, ffn2_saved, _ = _ffn_forward(x2, g_ffn2, sh3, sc3, gt3, wfull["w1_b"], wfull["w3_b"], wfull["w2_b"], "ffn2")

    dx3, lossvec, dgfin = _final_loss(x3, g_final.reshape(1, D), loss_target, "final_loss")
    loss = lax.psum(jnp.sum(lossvec), ("x", "y", "c"))

    gfull = {}
    dx2, (dsh3, dsc3, dgt3, dg3), (gfull["w1_b"], gfull["w3_b"], gfull["w2_b"]) = _ffn_backward(
        dx3, ffn2_saved, g_ffn2, sc3, gt3, wfull["w1_b"], wfull["w3_b"], wfull["w2_b"], "ffn2")

    def stack_group(grp):
        return [_stacked_from_full(gfull[n], rows, cols, split).astype(BF16) for n, rows, cols, split in grp]

    st_ffn2 = stack_group(groups[2])
    h_ffn2 = _exchange_begin(st_ffn2, [_own_slab(s, chip) for s in st_ffn2], True, "scatter_ffn2_start")

    dmix, dya, dyb, dp3, dgt2 = _merge_bwd(dx2, gt2 + h_ffn2[4][0, 0], mix, ya, yb, p3, wfull["w_out"], "merge_bwd")
    gfull["w_out"] = _mm(_flat(merged), _flat(dmix), ta=True, name="gw_out")
    dp1, dconvw8 = _conv_bwd(dya, wfull["w_conv_out"], p1, convw8, "conv_bwd")
    gfull["w_conv_out"] = _mm(_flat(ya_in), _flat(dya), ta=True, name="gw_conv_out")
    ds0, dz, ddskip = _ssm_post_bwd(dyb, s0, z, u, d_skip, wfull["w_glu"], wfull["w_ssm_out"], "ssm_post_bwd")
    gfull["w_ssm_out"] = _mm(_flat(s2), _flat(dyb), ta=True, name="gw_ssm_out")
    gfull["w_glu"] = _mm(_flat(s1), _flat(dz), ta=True, name="gw_glu")
    du_ssm, dwb, dwct, dar8, dai8 = _ssm_bwd(ds0, u, xs, wb, wct, ar8, ai8, "ssm_bwd")
    du = _du_combine(du_ssm, ds0, d_skip, "du_combine")
    dx1, dsh2, dsc2, dgmix = _dh_norm_bwd([dp1, du, dp3], [win1, winu, win3], x1, g_mix, sc2, dx2, "mix_bwd_dh")
    gfull["w_in"] = jnp.concatenate([
        _mm(h2f, _flat(dp1), ta=True, name="gw_in1"), _mm(h2f, _flat(du), ta=True, name="gw_inu"),
        _mm(h2f, _flat(dp3), ta=True, name="gw_in3")], axis=1)

    st_mix = stack_group(groups[1])
    h_mix = _exchange_begin(st_mix, [_own_slab(s, chip) for s in st_mix], True, "scatter_mix_start")

    ffn1_names = {"w1": BIG[0], "w3": BIG[1], "w2": BIG[2]}
    ffn1_handles = {}

    def emit_ffn1(key, gw):
        n, rows, cols, split = ffn1_names[key]
        st = _stacked_from_full(gw, rows, cols, split).astype(BF16)
        ffn1_handles[n] = _exchange_begin([st], [_own_slab(st, chip)], True, f"scatter_ffn1_{key}_start")
        return ffn1_handles[n][4][0, 0]

    grad_x, (dsh1, dsc1, dgt1, dg1), _ = _ffn_backward(
        dx1, ffn1_saved, g_ffn1, sc1, gt1 + h_mix[4][0, 0], wfull["w1_a"], wfull["w3_a"], wfull["w2_a"], "ffn1",
        emit=emit_ffn1)

    sbw = GP // SSM_SUPER
    d_are, d_aim, d_ldt, d_bre_t, d_bim_t = _ssm_disc_bwd(
        are, aim, ldt, bre_t, bim_t, jnp.sum(dar8, axis=0, keepdims=True), jnp.sum(dai8, axis=0, keepdims=True),
        _blockdiag_extract(dwb[:, :sbw]), _blockdiag_extract(dwb[:, sbw:]))
    d_cre = _c_from_t(_blockdiag_extract(dwct[:, :sbw]))
    d_cim = -_c_from_t(_blockdiag_extract(dwct[:, sbw:]))

    small_parts = [dg1, dgmix, dg3, dgfin, dconvw8[:3], d_are, d_aim, _from_t(d_bre_t), _from_t(d_bim_t), d_cre, d_cim,
                   jnp.sum(d_ldt.reshape(NG, NP), axis=1), ddskip]
    small_sizes = [int(p.size) for p in small_parts]
    n_small = sum(small_sizes)
    n_small_pad = -(-n_small // (SUB * PACK_COLS)) * (SUB * PACK_COLS)
    dmod = jnp.concatenate([dsh1, dsc1, dgt1, dsh2, dsc2, dgt2, dsh3, dsc3, dgt3], axis=2).reshape(bsz * NMOD * D)
    flat = jnp.concatenate([p.reshape(-1) for p in small_parts] + [jnp.zeros((n_small_pad - n_small,), F32), dmod])
    allg = _all_gather8(flat.reshape(SUB, -1), "gather_small").reshape(N_DEV, -1)
    small = _sum_slabs(allg[:, :n_small_pad].reshape(N_DEV, -1, PACK_COLS), "sum_small").reshape(-1)
    sg, o = [], 0
    for p, sz in zip(small_parts, small_sizes):
        sg.append(small[o:o + sz].reshape(p.shape))
        o += sz
    (g_g1, g_gmix, g_g3, g_gfin, g_convw, g_are, g_aim, g_bre, g_bim, g_cre, g_cim, g_ldt, g_dskip) = sg

    dmod_all = allg[:, n_small_pad:].reshape(nb, NMOD * D)
    dmod_shard = _select(dmod_all.reshape(nb, N_CHIPS, nmod_shard).transpose(1, 0, 2), chip)
    g_wada, g_bada = _ada_bwd(c_all, dmod_shard, dmod_all)

    recv_ffn2 = _exchange_end(h_ffn2, g_wada, True, "scatter_ffn2_wait")
    recv_mix = _exchange_end(h_mix, g_wada, True, "scatter_mix_wait")
    recv_ffn1 = [_exchange_end(ffn1_handles[n], g_wada, True, f"scatter_ffn1_{n}_wait")[0] for n, _, _, _ in groups[0]]
    recv = [*recv_ffn1, *recv_mix, *recv_ffn2]
    recv_sib = _swap_sibling(recv, "swap_sibling")

    grads, deltas, new_m, new_v = {}, {}, {}, {}
    for (n, _, _, _), r_own, r_sib in zip(BIG, recv, recv_sib):
        res = _adamw_big(_view(args[n], n), _view(args["m_" + n], n), _view(args["v_" + n], n), r_own, r_sib,
                         f"adamw_{n}")
        grads[n], deltas[n], new_m[n], new_v[n] = [_view(r, n) for r in res]
    grads["w_ada"] = g_wada[None]
    deltas["w_ada"], new_m["w_ada"], new_v["w_ada"] = _adamw_rows(w_ada, m_w_ada, v_w_ada, g_wada, "adamw_w_ada")

    g_convw_shard = _select(g_convw.reshape(3, N_CHIPS, CW // N_CHIPS).transpose(1, 0, 2), chip)
    small_g = {"b_ada": g_bada, "g_ffn1": g_g1, "g_mix": g_gmix, "g_ffn2": g_g3, "g_final": g_gfin,
               "conv_w": g_convw_shard, "a_re": g_are, "a_im": g_aim, "b_re": g_bre, "b_im": g_bim,
               "c_re": g_cre, "c_im": g_cim, "log_dt": g_ldt, "d_skip": g_dskip}
    small_names = list(small_g)
    sizes = [int(args[n].size) for n in small_names]
    tot = sum(sizes)
    tot_pad = -(-tot // (SUB * PACK_COLS)) * (SUB * PACK_COLS)

    def pack(get):
        return jnp.concatenate([get(n).reshape(-1) for n in small_names] + [jnp.zeros((tot_pad - tot,), F32)]).reshape(
            -1, PACK_COLS)

    res = _adamw_plain(pack(lambda n: args[n]), pack(lambda n: args["m_" + n]), pack(lambda n: args["v_" + n]),
                       pack(lambda n: small_g[n]), "adamw_small")
    o = 0
    for n, sz in zip(small_names, sizes):
        shp = args[n].shape
        grads[n] = small_g[n].reshape(shp)
        deltas[n], new_m[n], new_v[n] = [r.reshape(-1)[o:o + sz].reshape(shp) for r in res]
        o += sz

    return (loss, grad_x, *[grads[n] for n in names], *[deltas[n] for n in names],
            *[new_m[n] for n in names], *[new_v[n] for n in names])
```

```python
import functools
import math

import jax
import jax.numpy as jnp
from jax import lax
from jax.experimental import pallas as pl
from jax.experimental.pallas import tpu as pltpu

F32 = jnp.float32
BF16 = jnp.bfloat16
SDS = jax.ShapeDtypeStruct
MESH = pl.DeviceIdType.MESH

D = 1024
DFF = 2816
CW = 1024
SW = 512
NG, NP, NH = 32, 64, 16
GP = NG * NP
NMOD = 9
EPS = 1e-6
N_CHIPS = 4
N_DEV = 8
SUB = 8
LANE = 128
SSM_SUPER = 4
VMEM_LIMIT = 50 * 1024 * 1024

LR, B1, B2, AEPS, WD, STEP = 0.001, 0.9, 0.999, 1e-08, 0.01, 10
BC1 = 1.0 - B1 ** STEP
BC2 = 1.0 - B2 ** STEP


def _cp(*sem):
    return pltpu.CompilerParams(dimension_semantics=sem or None, vmem_limit_bytes=VMEM_LIMIT)


def _pick_tile(n, cands):
    for t in cands:
        if t <= n and n % t == 0:
            return t
    return n


def _dot(a, b):
    return lax.dot_general(a, b, (((1,), (0,)), ((), ())), preferred_element_type=F32)


def _dot_nt(a, b):
    return lax.dot_general(a, b, (((1,), (1,)), ((), ())), preferred_element_type=F32)


def _dot_tn(a, b):
    return lax.dot_general(a, b, (((0,), (0,)), ((), ())), preferred_element_type=F32)


def _row(tm, width, col=0):
    return pl.BlockSpec((1, tm, width), lambda b, i, *_: (b, i, col))


def _seqvec(width):
    return pl.BlockSpec((1, 1, width), lambda b, *_: (b, 0, 0))


def _full2(shape):
    return pl.BlockSpec(shape, lambda *_: (0, 0))


def _sigmoid(x):
    return jax.nn.sigmoid(x)


def _mm(a, b, *, ta=False, tb=False, out_dtype=F32, name):
    if ta:
        kdim, m = a.shape
    else:
        m, kdim = a.shape
    n = b.shape[0] if tb else b.shape[1]
    tm = _pick_tile(m, (1408, 1024, 512, 256, 128))
    tn = _pick_tile(n, (1408, 1024, 512, 256, 128))
    tk = _pick_tile(kdim, (1024, 512, 256, 128))
    nk = kdim // tk

    def body(a_ref, b_ref, o_ref, acc_ref):
        k = pl.program_id(2)

        @pl.when(k == 0)
        def _():
            acc_ref[...] = jnp.zeros_like(acc_ref)

        av = a_ref[...].astype(BF16)
        bv = b_ref[...].astype(BF16)
        dn = (((0 if ta else 1,), (1 if tb else 0,)), ((), ()))
        acc_ref[...] += lax.dot_general(av, bv, dn, preferred_element_type=F32)

        @pl.when(k == nk - 1)
        def _():
            o_ref[...] = acc_ref[...].astype(out_dtype)

    a_spec = pl.BlockSpec((tk, tm), lambda i, j, k: (k, i)) if ta else pl.BlockSpec((tm, tk), lambda i, j, k: (i, k))
    b_spec = pl.BlockSpec((tn, tk), lambda i, j, k: (j, k)) if tb else pl.BlockSpec((tk, tn), lambda i, j, k: (k, j))
    return pl.pallas_call(
        body, name=name, grid=(m // tm, n // tn, nk),
        in_specs=[a_spec, b_spec],
        out_specs=pl.BlockSpec((tm, tn), lambda i, j, k: (i, j)),
        out_shape=SDS((m, n), out_dtype),
        scratch_shapes=[pltpu.VMEM((tm, tn), F32)],
        compiler_params=_cp("parallel", "parallel", "arbitrary"),
    )(a, b)


def _flat(a):
    return a.reshape(-1, a.shape[-1])


def _norm_mod(x, g, sh, sc, name):
    bsz, seq, dm = x.shape
    tm = _pick_tile(seq, (512, 256, 128))

    def body(x_ref, g_ref, sh_ref, sc_ref, o_ref):
        xf = x_ref[0]
        r = lax.rsqrt(jnp.mean(xf * xf, axis=-1, keepdims=True) + EPS)
        hn = xf * r * g_ref[...]
        o_ref[0] = (hn * (1.0 + sc_ref[0]) + sh_ref[0]).astype(BF16)

    return pl.pallas_call(
        body, name=name, grid=(bsz, seq // tm),
        in_specs=[_row(tm, dm), _full2((1, dm)), _seqvec(dm), _seqvec(dm)],
        out_specs=_row(tm, dm), out_shape=SDS((bsz, seq, dm), BF16),
        compiler_params=_cp("parallel", "parallel"),
    )(x, g, sh, sc)


def _swiglu_up(h, w1, w3, name):
    bsz, seq, dm = h.shape
    nf = w1.shape[0]
    tm = _pick_tile(seq, (512, 256, 128))
    tn = _pick_tile(nf, (1408, 512, 256, 128))

    def body(h_ref, w1_ref, w3_ref, a_ref, b_ref, hid_ref):
        hv = h_ref[0]
        a = _dot_nt(hv, w1_ref[...])
        b = _dot_nt(hv, w3_ref[...])
        sg = _sigmoid(a)
        sa = a * sg
        a_ref[0] = (b * (sg * (1.0 + a * (1.0 - sg)))).astype(BF16)
        b_ref[0] = sa.astype(BF16)
        hid_ref[0] = (sa * b).astype(BF16)

    wspec = pl.BlockSpec((tn, dm), lambda n, b, i: (n, 0))
    ospec = pl.BlockSpec((1, tm, tn), lambda n, b, i: (b, i, n))
    shp = SDS((bsz, seq, nf), BF16)
    return pl.pallas_call(
        body, name=name, grid=(nf // tn, bsz, seq // tm),
        in_specs=[pl.BlockSpec((1, tm, dm), lambda n, b, i: (b, i, 0)), wspec, wspec],
        out_specs=[ospec, ospec, ospec], out_shape=[shp, shp, shp],
        compiler_params=_cp("parallel", "parallel", "parallel"),
    )(h, w1, w3)


def _ffn_down(hid, w2, x, gt, name):
    bsz, seq, nf = hid.shape
    dm = w2.shape[1]
    tm = _pick_tile(seq, (512, 256, 128))

    def body(hid_ref, w2_ref, x_ref, gt_ref, f_ref, xo_ref):
        f = _dot(hid_ref[0], w2_ref[...])
        f_ref[0] = f
        xo_ref[0] = x_ref[0] + 0.5 * gt_ref[0] * f

    shp = SDS((bsz, seq, dm), F32)
    return pl.pallas_call(
        body, name=name, grid=(bsz, seq // tm),
        in_specs=[_row(tm, nf), _full2((nf, dm)), _row(tm, dm), _seqvec(dm)],
        out_specs=[_row(tm, dm), _row(tm, dm)], out_shape=[shp, shp],
        compiler_params=_cp("parallel", "parallel"),
    )(hid, w2, x, gt)


def _ffn_bwd_hid(dxo, gt, f, a, b, w2, name):
    bsz, seq, dm = dxo.shape
    nf = a.shape[2]
    tm = _pick_tile(seq, (512, 256, 128))
    tn = _pick_tile(nf, (1408, 512, 256, 128))

    def body(dxo_ref, gt_ref, f_ref, a_ref, b_ref, w2_ref, dfs_ref, da_ref, db_ref, dgt_ref):
        i = pl.program_id(1)
        n = pl.program_id(2)

        @pl.when(n == 0)
        def _():
            dxo = dxo_ref[0]
            dfs_ref[0] = (0.5 * gt_ref[0] * dxo).astype(BF16)
            part = jnp.sum(0.5 * dxo * f_ref[0], axis=0, keepdims=True)

            @pl.when(i == 0)
            def _():
                dgt_ref[0] = part

            @pl.when(i > 0)
            def _():
                dgt_ref[0] += part

        dhid = _dot_nt(dfs_ref[0], w2_ref[pl.ds(pl.multiple_of(n * tn, tn), tn), :])
        da_ref[0] = (dhid * a_ref[0].astype(F32)).astype(BF16)
        db_ref[0] = (dhid * b_ref[0].astype(F32)).astype(BF16)

    hspec = pl.BlockSpec((1, tm, tn), lambda b, i, n: (b, i, n))
    return pl.pallas_call(
        body, name=name, grid=(bsz, seq // tm, nf // tn),
        in_specs=[_row(tm, dm), _seqvec(dm), _row(tm, dm), hspec, hspec, _full2((nf, dm))],
        out_specs=[_row(tm, dm), hspec, hspec, _seqvec(dm)],
        out_shape=[SDS((bsz, seq, dm), BF16), SDS((bsz, seq, nf), BF16), SDS((bsz, seq, nf), BF16),
                   SDS((bsz, 1, dm), F32)],
        compiler_params=_cp("arbitrary", "arbitrary", "arbitrary"),
    )(dxo, gt, f, a, b, w2)


def _dh_norm_bwd(pieces, weights, x, g, sc, dxo, name, transposed=False):
    bsz, seq, dm = x.shape
    tm = _pick_tile(seq, (512, 256, 128))
    npc = len(pieces)
    dot = _dot if transposed else _dot_nt

    def body(*refs):
        p_refs = refs[:npc]
        w_hbm = refs[npc:2 * npc]
        x_ref, g_ref, sc_ref, dxo_ref, dx_ref, dsh_ref, dsc_ref, dg_ref = refs[2 * npc:2 * npc + 8]
        w_refs = refs[2 * npc + 8:]
        b = pl.program_id(0)
        i = pl.program_id(1)

        @pl.when((i == 0) & (b == 0))
        def _():
            for src, dst in zip(w_hbm, w_refs):
                pltpu.sync_copy(src, dst)

        dh = dot(p_refs[0][0], w_refs[0][...])
        for j in range(1, npc):
            dh = dh + dot(p_refs[j][0], w_refs[j][...])
        xf = x_ref[0]
        gv = g_ref[...]
        r = lax.rsqrt(jnp.mean(xf * xf, axis=-1, keepdims=True) + EPS)
        xhat = xf * r
        dhn = dh * (1.0 + sc_ref[0])
        p_sh = jnp.sum(dh, axis=0, keepdims=True)
        p_sc = jnp.sum(dh * (xhat * gv), axis=0, keepdims=True)
        p_g = jnp.sum(dhn * xhat, axis=0, keepdims=True)
        dxh = dhn * gv
        dx_ref[0] = dxo_ref[0] + r * (dxh - xhat * jnp.mean(dxh * xhat, axis=-1, keepdims=True))

        @pl.when(i == 0)
        def _():
            dsh_ref[0] = p_sh
            dsc_ref[0] = p_sc

        @pl.when(i > 0)
        def _():
            dsh_ref[0] += p_sh
            dsc_ref[0] += p_sc

        @pl.when((i == 0) & (b == 0))
        def _():
            dg_ref[...] = p_g

        @pl.when((i > 0) | (b > 0))
        def _():
            dg_ref[...] += p_g

    return pl.pallas_call(
        body, name=name, grid=(bsz, seq // tm),
        in_specs=[_row(tm, p.shape[2]) for p in pieces] + [pl.BlockSpec(memory_space=pl.ANY)] * npc + [
            _row(tm, dm), _full2((1, dm)), _seqvec(dm), _row(tm, dm)],
        out_specs=[_row(tm, dm), _seqvec(dm), _seqvec(dm), _full2((1, dm))],
        out_shape=[SDS((bsz, seq, dm), F32), SDS((bsz, 1, dm), F32), SDS((bsz, 1, dm), F32), SDS((1, dm), F32)],
        scratch_shapes=[pltpu.VMEM(w.shape, w.dtype) for w in weights],
        compiler_params=_cp("arbitrary", "arbitrary"),
    )(*pieces, *weights, x, g, sc, dxo)


HALO = 16


def _conv_core(gc, v, gch, vh, w, first):
    cv = gc * v
    halo = jnp.where(first, 0.0, gch * vh)
    ext = jnp.concatenate([halo, cv], axis=0)
    cv1 = pltpu.roll(ext, 1, 0)[HALO:]
    cv2 = pltpu.roll(ext, 2, 0)[HALO:]
    conv = w[0:1] * cv2 + w[1:2] * cv1 + w[2:3] * cv
    return cv, cv1, cv2, conv


def _prev_halo(tm, col):
    return pl.BlockSpec((1, HALO, CW), lambda b, i, *_: (b, jnp.maximum(i * (tm // HALO) - 1, 0), col))


def _next_halo(tm, seq, col):
    return pl.BlockSpec((1, HALO, CW), lambda b, i, *_: (b, jnp.minimum((i + 1) * (tm // HALO), seq // HALO - 1), col))


def _conv_fwd(p1, convw8, name):
    bsz, seq, _ = p1.shape
    tm = _pick_tile(seq, (512, 256, 128))

    def body(gb_ref, gc_ref, v_ref, gch_ref, vh_ref, w_ref, o_ref):
        first = pl.program_id(1) == 0
        _, _, _, conv = _conv_core(gc_ref[0].astype(F32), v_ref[0].astype(F32), gch_ref[0].astype(F32),
                                   vh_ref[0].astype(F32), w_ref[...], first)
        o_ref[0] = (gb_ref[0].astype(F32) * conv).astype(BF16)

    return pl.pallas_call(
        body, name=name, grid=(bsz, seq // tm),
        in_specs=[_row(tm, CW, 0), _row(tm, CW, 1), _row(tm, CW, 2), _prev_halo(tm, 1), _prev_halo(tm, 2),
                  _full2((8, CW))],
        out_specs=_row(tm, CW), out_shape=SDS((bsz, seq, CW), BF16),
        compiler_params=_cp("parallel", "parallel"),
    )(p1, p1, p1, p1, p1, convw8)


def _conv_bwd(dya, wco, p1, convw8, name):
    bsz, seq, _ = p1.shape
    tm = _pick_tile(seq, (512, 256, 128))
    nt = seq // tm
    ext_rows = tm + HALO

    def body(dya_ref, dyan_ref, wco_ref, gb_ref, gbn_ref, gc_ref, v_ref, gch_ref, vh_ref, w_ref, dp_ref, dw_ref):
        b = pl.program_id(0)
        i = pl.program_id(1)
        w = w_ref[...]
        gc = gc_ref[0].astype(F32)
        vv = v_ref[0].astype(F32)
        cv, cv1, cv2, conv = _conv_core(gc, vv, gch_ref[0].astype(F32), vh_ref[0].astype(F32), w, i == 0)
        dya_ext = jnp.concatenate([dya_ref[0], dyan_ref[0]], axis=0)
        dyain_ext = _dot_nt(dya_ext, wco_ref[...])
        gb_ext = jnp.concatenate([gb_ref[0], gbn_ref[0]], axis=0).astype(F32)
        rows = lax.broadcasted_iota(jnp.int32, (ext_rows, 1), 0)
        dconv_ext = jnp.where((rows < tm) | (i < nt - 1), dyain_ext * gb_ext, 0.0)
        dconv = dconv_ext[:tm]
        dconv1 = pltpu.roll(dconv_ext, ext_rows - 1, 0)[:tm]
        dconv2 = pltpu.roll(dconv_ext, ext_rows - 2, 0)[:tm]
        dcv = w[2:3] * dconv + w[1:2] * dconv1 + w[0:1] * dconv2
        dp_ref[0, :, 0:CW] = (dyain_ext[:tm] * conv).astype(BF16)
        dp_ref[0, :, CW:2 * CW] = (dcv * vv).astype(BF16)
        dp_ref[0, :, 2 * CW:3 * CW] = (dcv * gc).astype(BF16)
        g0 = jnp.sum(dconv * cv2, axis=0, keepdims=True)
        g1 = jnp.sum(dconv * cv1, axis=0, keepdims=True)
        g2 = jnp.sum(dconv * cv, axis=0, keepdims=True)
        upd = jnp.concatenate([g0, g1, g2, jnp.zeros((5, CW), F32)], axis=0)

        @pl.when((i == 0) & (b == 0))
        def _():
            dw_ref[...] = upd

        @pl.when((i > 0) | (b > 0))
        def _():
            dw_ref[...] += upd

    return pl.pallas_call(
        body, name=name, grid=(bsz, seq // tm),
        in_specs=[_row(tm, CW), _next_halo(tm, seq, 0), _full2((CW, D)),
                  _row(tm, CW, 0), _next_halo(tm, seq, 0), _row(tm, CW, 1), _row(tm, CW, 2),
                  _prev_halo(tm, 1), _prev_halo(tm, 2), _full2((8, CW))],
        out_specs=[_row(tm, 3 * CW), _full2((8, CW))],
        out_shape=[SDS((bsz, seq, 3 * CW), BF16), SDS((8, CW), F32)],
        compiler_params=_cp("arbitrary", "arbitrary"),
    )(dya, dya, wco, p1, p1, p1, p1, p1, p1, convw8)


def _disc(are, aim, ldt, bre, bim):
    dt = jnp.exp(ldt)
    mag = jnp.exp(are * dt)
    ang = aim * dt
    abr = mag * jnp.cos(ang)
    abi = mag * jnp.sin(ang)
    nr = abr - 1.0
    den = are * are + aim * aim
    cr = (nr * are + abi * aim) / den
    ci = (abi * are - nr * aim) / den
    return abr, abi, cr * bre - ci * bim, cr * bim + ci * bre


def _ssm_disc(are, aim, ldt, bre_t, bim_t):
    def body(are_ref, aim_ref, ldt_ref, bre_ref, bim_ref, abr_ref, abi_ref, bbr_ref, bbi_ref):
        abr, abi, bbr, bbi = _disc(are_ref[...], aim_ref[...], ldt_ref[...], bre_ref[...], bim_ref[...])
        abr_ref[...] = abr
        abi_ref[...] = abi
        bbr_ref[...] = bbr
        bbi_ref[...] = bbi

    v1, vh = SDS((1, GP), F32), SDS((NH, GP), F32)
    return pl.pallas_call(body, name="ssm_disc", out_shape=[v1, v1, vh, vh], compiler_params=_cp())(
        are, aim, ldt, bre_t, bim_t)


def _ssm_disc_bwd(are, aim, ldt, bre_t, bim_t, dabr, dabi, dbbr, dbbi):
    def body(are_ref, aim_ref, ldt_ref, bre_ref, bim_ref, g0, g1, g2, g3, o0, o1, o2, o3, o4):
        prim = (are_ref[...], aim_ref[...], ldt_ref[...], bre_ref[...], bim_ref[...])
        _, vjp = jax.vjp(_disc, *prim)
        d_are, d_aim, d_ldt, d_bre, d_bim = vjp((g0[...], g1[...], g2[...], g3[...]))
        o0[...] = d_are
        o1[...] = d_aim
        o2[...] = d_ldt
        o3[...] = d_bre
        o4[...] = d_bim

    v1, vh = SDS((1, GP), F32), SDS((NH, GP), F32)
    return pl.pallas_call(body, name="ssm_disc_bwd", out_shape=[v1, v1, v1, vh, vh], compiler_params=_cp())(
        are, aim, ldt, bre_t, bim_t, dabr, dabi, dbbr, dbbi)


def _scan_chunk(buf_ref, nt, bsz, ar, ai, init_r, init_i, reverse):
    nsub = SUB // bsz
    row = lax.broadcasted_iota(jnp.int32, (SUB, GP), 0)
    shift = ((SUB - bsz) if reverse else bsz) % SUB
    order = list(range(nsub - 1, -1, -1)) if reverse else list(range(nsub))

    def step(j, carry):
        pr, pi = carry
        jj = (nt - 1 - j) if reverse else j
        off = pl.multiple_of(jj * SUB, SUB)
        br = buf_ref[pl.ds(off, SUB), 0:GP]
        bi = buf_ref[pl.ds(off, SUB), GP:2 * GP]
        nr, ni = pr, pi
        for s in order:
            sr, si = nr, ni
            if shift:
                sr = pltpu.roll(sr, shift, 0)
                si = pltpu.roll(si, shift, 0)
            cr = ar * sr - ai * si + br
            ci = ar * si + ai * sr + bi
            if nsub == 1:
                nr, ni = cr, ci
            else:
                m = (row >= s * bsz) & (row < (s + 1) * bsz)
                nr = jnp.where(m, cr, nr)
                ni = jnp.where(m, ci, ni)
        buf_ref[pl.ds(off, SUB), 0:GP] = nr
        buf_ref[pl.ds(off, SUB), GP:2 * GP] = ni
        return nr, ni

    return lax.fori_loop(0, nt, step, (init_r, init_i))


def _ssm_chunk_rows(total_rows, bsz):
    return min(total_rows, 64 * bsz)


def _interleave(src_ref, tmp_ref, bsz, steps):
    nl = tmp_ref.shape[0]
    for b in range(bsz):
        for j in range(nl):
            tmp_ref.at[j][pl.ds(b, steps, stride=bsz), :] = src_ref[b, :, j * LANE:(j + 1) * LANE]
    return jnp.concatenate([tmp_ref[j] for j in range(nl)], axis=1)


def _deinterleave(val, tmp_ref, dst_ref, bsz, steps):
    nl = tmp_ref.shape[0]
    for j in range(nl):
        tmp_ref[j] = val[:, j * LANE:(j + 1) * LANE]
    for b in range(bsz):
        for j in range(nl):
            dst_ref[b, :, j * LANE:(j + 1) * LANE] = tmp_ref.at[j][pl.ds(b, steps, stride=bsz), :]


SSM_UB = SW // SSM_SUPER
SSM_SB = GP // SSM_SUPER


def _sb_cols(s, half):
    return slice(half * GP + s * SSM_SB, half * GP + (s + 1) * SSM_SB)


def _ssm_in(v16, w_ref, x_ref):
    for s in range(SSM_SUPER):
        vs = v16[:, s * SSM_UB:(s + 1) * SSM_UB]
        for half in range(2):
            x_ref[:, _sb_cols(s, half)] = _dot(vs, w_ref[s * SSM_UB:(s + 1) * SSM_UB, half * SSM_SB:(half + 1) * SSM_SB])


def _ssm_out(x16, w_ref):
    outs = []
    for s in range(SSM_SUPER):
        rows = slice(s * SSM_UB, (s + 1) * SSM_UB)
        outs.append(_dot_nt(x16[:, _sb_cols(s, 0)], w_ref[rows, 0:SSM_SB])
                    + _dot_nt(x16[:, _sb_cols(s, 1)], w_ref[rows, SSM_SB:2 * SSM_SB]))
    return jnp.concatenate(outs, axis=1)


def _ssm_fwd(u, wb, wct, ar8, ai8, name):
    bsz, seq, _ = u.shape
    rt = seq * bsz
    r = _ssm_chunk_rows(rt, bsz)
    nt = r // SUB
    steps = r // bsz

    def body(u_ref, wb_hbm, wct_hbm, ar_ref, ai_ref, x_ref, y_ref, wb_ref, wct_ref, st_ref, tmp_ref):
        @pl.when(pl.program_id(0) == 0)
        def _():
            pltpu.sync_copy(wb_hbm, wb_ref)
            pltpu.sync_copy(wct_hbm, wct_ref)
            st_ref[...] = jnp.zeros_like(st_ref)

        _ssm_in(_interleave(u_ref, tmp_ref, bsz, steps).astype(BF16), wb_ref, x_ref)
        fr, fi = _scan_chunk(x_ref, nt, bsz, ar_ref[...], ai_ref[...], st_ref[:, 0:GP], st_ref[:, GP:2 * GP], False)
        st_ref[:, 0:GP] = fr
        st_ref[:, GP:2 * GP] = fi
        _deinterleave(_ssm_out(x_ref[...].astype(BF16), wct_ref), tmp_ref, y_ref, bsz, steps)

    anyspec = pl.BlockSpec(memory_space=pl.ANY)
    seqs = pl.BlockSpec((bsz, steps, SW), lambda i: (0, i, 0))
    return pl.pallas_call(
        body, name=name, grid=(rt // r,),
        in_specs=[seqs, anyspec, anyspec, _full2((SUB, GP)), _full2((SUB, GP))],
        out_specs=[pl.BlockSpec((r, 2 * GP), lambda i: (i, 0)), seqs],
        out_shape=[SDS((rt, 2 * GP), F32), SDS((bsz, seq, SW), F32)],
        scratch_shapes=[pltpu.VMEM(wb.shape, BF16), pltpu.VMEM(wct.shape, BF16), pltpu.VMEM((SUB, 2 * GP), F32),
                        pltpu.VMEM((SW // LANE, r, LANE), F32)],
        compiler_params=_cp("arbitrary"),
    )(u, wb, wct, ar8, ai8)


def _ssm_bwd(dy, u, xs, wb, wct, ar8, ai8, name):
    bsz, seq, _ = u.shape
    rt = seq * bsz
    r = _ssm_chunk_rows(rt, bsz)
    nt = r // SUB
    nc = rt // r
    steps = r // bsz
    ub = SW // SSM_SUPER
    sb = GP // SSM_SUPER

    def body(dys_ref, us_ref, x_ref, xh_ref, wb_hbm, wct_hbm, ar_ref, ai_ref,
             dus_ref, dwb_hbm, dwct_hbm, dar_ref, dai_ref, wb_ref, wct_ref, g_ref, st_ref, awb_ref, awct_ref,
             tmp_ref):
        i = pl.program_id(0)
        dyb = _interleave(dys_ref, tmp_ref, bsz, steps).astype(BF16)
        ub16 = _interleave(us_ref, tmp_ref, bsz, steps).astype(BF16)

        @pl.when(i == 0)
        def _():
            pltpu.sync_copy(wb_hbm, wb_ref)
            pltpu.sync_copy(wct_hbm, wct_ref)
            st_ref[...] = jnp.zeros_like(st_ref)
            awb_ref[...] = jnp.zeros_like(awb_ref)
            awct_ref[...] = jnp.zeros_like(awct_ref)
            dar_ref[...] = jnp.zeros_like(dar_ref)
            dai_ref[...] = jnp.zeros_like(dai_ref)

        _ssm_in(dyb, wct_ref, g_ref)
        ar = ar_ref[...]
        ai = ai_ref[...]
        fr, fi = _scan_chunk(g_ref, nt, bsz, ar, -ai, st_ref[:, 0:GP], st_ref[:, GP:2 * GP], True)
        st_ref[:, 0:GP] = fr
        st_ref[:, GP:2 * GP] = fi

        gb = g_ref[...].astype(BF16)
        _deinterleave(_ssm_out(gb, wb_ref), tmp_ref, dus_ref, bsz, steps)
        xb16 = x_ref[...].astype(BF16)
        for s in range(SSM_SUPER):
            us = ub16[:, s * ub:(s + 1) * ub]
            ds = dyb[:, s * ub:(s + 1) * ub]
            for half in range(2):
                cols = slice(half * GP + s * sb, half * GP + (s + 1) * sb)
                ocols = slice(half * sb, (half + 1) * sb)
                awb_ref[s * ub:(s + 1) * ub, ocols] += _dot_tn(us, gb[:, cols])
                awct_ref[s * ub:(s + 1) * ub, ocols] += _dot_tn(ds, xb16[:, cols])

        gr = g_ref[:, 0:GP]
        gi = g_ref[:, GP:2 * GP]
        xsr = pltpu.roll(x_ref[:, 0:GP], bsz, 0)
        xsi = pltpu.roll(x_ref[:, GP:2 * GP], bsz, 0)
        inner = lax.broadcasted_iota(jnp.int32, (r, 1), 0) >= bsz
        t_r = jnp.where(inner, gr * xsr + gi * xsi, 0.0)
        t_i = jnp.where(inner, gi * xsr - gr * xsi, 0.0)
        acc_r = jnp.sum(t_r.reshape(nt, SUB, GP), axis=0)
        acc_i = jnp.sum(t_i.reshape(nt, SUB, GP), axis=0)
        hr = xh_ref[:, 0:GP]
        hi = xh_ref[:, GP:2 * GP]
        if bsz % SUB:
            hr = pltpu.roll(hr, bsz, 0)
            hi = pltpu.roll(hi, bsz, 0)
        edge = (lax.broadcasted_iota(jnp.int32, (SUB, 1), 0) < bsz) & (i < nc - 1)
        g0r = g_ref[0:SUB, 0:GP]
        g0i = g_ref[0:SUB, GP:2 * GP]
        dar_ref[...] += acc_r + jnp.where(edge, g0r * hr + g0i * hi, 0.0)
        dai_ref[...] += acc_i + jnp.where(edge, g0i * hr - g0r * hi, 0.0)

        @pl.when(i == nc - 1)
        def _():
            pltpu.sync_copy(awb_ref, dwb_hbm)
            pltpu.sync_copy(awct_ref, dwct_hbm)

    anyspec = pl.BlockSpec(memory_space=pl.ANY)
    rev = lambda i: (nc - 1 - i, 0)
    seqs = pl.BlockSpec((bsz, steps, SW), lambda i: (0, nc - 1 - i, 0))
    wshape = (SW, 2 * sb)
    return pl.pallas_call(
        body, name=name, grid=(nc,),
        in_specs=[seqs, seqs, pl.BlockSpec((r, 2 * GP), rev),
                  pl.BlockSpec((SUB, 2 * GP), lambda i: (jnp.maximum((nc - 1 - i) * nt - 1, 0), 0)),
                  anyspec, anyspec, _full2((SUB, GP)), _full2((SUB, GP))],
        out_specs=[seqs, anyspec, anyspec, _full2((SUB, GP)), _full2((SUB, GP))],
        out_shape=[SDS((bsz, seq, SW), F32), SDS(wshape, F32), SDS(wshape, F32), SDS((SUB, GP), F32),
                   SDS((SUB, GP), F32)],
        scratch_shapes=[pltpu.VMEM(wb.shape, BF16), pltpu.VMEM(wct.shape, BF16),
                        pltpu.VMEM((r, 2 * GP), F32), pltpu.VMEM((SUB, 2 * GP), F32),
                        pltpu.VMEM(wshape, F32), pltpu.VMEM(wshape, F32),
                        pltpu.VMEM((SW // LANE, r, LANE), F32)],
        compiler_params=_cp("arbitrary"),
    )(dy, u, xs, xs, wb, wct, ar8, ai8)


GELU_C = math.sqrt(2.0 / math.pi)


def _gelu(x):
    return 0.5 * x * (1.0 + jnp.tanh(GELU_C * (x + 0.044715 * x * x * x)))


def _gelu_grad(x):
    th = jnp.tanh(GELU_C * (x + 0.044715 * x * x * x))
    return 0.5 * (1.0 + th) + 0.5 * x * (1.0 - th * th) * GELU_C * (1.0 + 3.0 * 0.044715 * x * x)


def _ssm_post(ys, u, dskip, wglu, wso, name):
    bsz, seq, _ = ys.shape
    tm = _pick_tile(seq, (512, 256, 128))

    def body(ys_ref, u_ref, d_ref, wg_ref, wo_ref, s0_ref, z_ref, s1_ref, s2_ref, yb_ref):
        s0 = ys_ref[0] + d_ref[...] * u_ref[0]
        s1 = _gelu(s0)
        s1b = s1.astype(BF16)
        z = _dot(s1b, wg_ref[...])
        s2b = (s1 * _sigmoid(z)).astype(BF16)
        s0_ref[0] = s0
        z_ref[0] = z
        s1_ref[0] = s1b
        s2_ref[0] = s2b
        yb_ref[0] = _dot(s2b, wo_ref[...]).astype(BF16)

    return pl.pallas_call(
        body, name=name, grid=(bsz, seq // tm),
        in_specs=[_row(tm, SW), _row(tm, SW), _full2((1, SW)), _full2((SW, SW)), _full2((SW, D))],
        out_specs=[_row(tm, SW), _row(tm, SW), _row(tm, SW), _row(tm, SW), _row(tm, D)],
        out_shape=[SDS((bsz, seq, SW), F32), SDS((bsz, seq, SW), F32), SDS((bsz, seq, SW), BF16),
                   SDS((bsz, seq, SW), BF16), SDS((bsz, seq, D), BF16)],
        compiler_params=_cp("parallel", "parallel"),
    )(ys, u, dskip, wglu, wso)


def _ssm_post_bwd(dyb, s0, z, u, dskip, wglu, wso, name):
    bsz, seq, _ = s0.shape
    tm = _pick_tile(seq, (512, 256, 128))

    def body(dyb_ref, s0_ref, z_ref, u_ref, wg_ref, wo_ref, ds0_ref, dz_ref, dd_ref):
        b = pl.program_id(0)
        i = pl.program_id(1)
        ds2 = _dot_nt(dyb_ref[0], wo_ref[...])
        s0 = s0_ref[0]
        s1 = _gelu(s0)
        sg = _sigmoid(z_ref[0])
        dz = ds2 * s1 * sg * (1.0 - sg)
        dzb = dz.astype(BF16)
        ds1 = ds2 * sg + _dot_nt(dzb, wg_ref[...])
        ds0 = ds1 * _gelu_grad(s0)
        ds0_ref[0] = ds0
        dz_ref[0] = dzb
        part = jnp.sum(ds0 * u_ref[0], axis=0, keepdims=True)

        @pl.when((i == 0) & (b == 0))
        def _():
            dd_ref[...] = part

        @pl.when((i > 0) | (b > 0))
        def _():
            dd_ref[...] += part

    del dskip
    return pl.pallas_call(
        body, name=name, grid=(bsz, seq // tm),
        in_specs=[_row(tm, D), _row(tm, SW), _row(tm, SW), _row(tm, SW), _full2((SW, SW)), _full2((SW, D))],
        out_specs=[_row(tm, SW), _row(tm, SW), _full2((1, SW))],
        out_shape=[SDS((bsz, seq, SW), F32), SDS((bsz, seq, SW), BF16), SDS((1, SW), F32)],
        compiler_params=_cp("arbitrary", "arbitrary"),
    )(dyb, s0, z, u, wglu, wso)


def _du_combine(du_ssm, ds0, dskip, name):
    bsz, seq, _ = ds0.shape
    tm = _pick_tile(seq, (512, 256, 128))

    def body(a_ref, b_ref, d_ref, o_ref):
        o_ref[0] = (a_ref[0] + b_ref[0] * d_ref[...]).astype(BF16)

    return pl.pallas_call(
        body, name=name, grid=(bsz, seq // tm),
        in_specs=[_row(tm, SW), _row(tm, SW), _full2((1, SW))],
        out_specs=_row(tm, SW), out_shape=SDS((bsz, seq, SW), BF16),
        compiler_params=_cp("parallel", "parallel"),
    )(du_ssm, ds0, dskip)


def _merge_out(ya, yb, p3, wout, x1, gt, name):
    bsz, seq, _ = ya.shape
    tm = _pick_tile(seq, (512, 256, 128))

    def body(ya_ref, yb_ref, ga_ref, gbb_ref, w_ref, x_ref, gt_ref, mg_ref, mix_ref, xo_ref):
        merged = (_sigmoid(ga_ref[0].astype(F32)) * ya_ref[0].astype(F32)
                  + _sigmoid(gbb_ref[0].astype(F32)) * yb_ref[0].astype(F32)).astype(BF16)
        mix = _dot(merged, w_ref[...])
        mg_ref[0] = merged
        mix_ref[0] = mix
        xo_ref[0] = x_ref[0] + gt_ref[0] * mix

    return pl.pallas_call(
        body, name=name, grid=(bsz, seq // tm),
        in_specs=[_row(tm, D), _row(tm, D), _row(tm, D, 0), _row(tm, D, 1), _full2((D, D)), _row(tm, D), _seqvec(D)],
        out_specs=[_row(tm, D), _row(tm, D), _row(tm, D)],
        out_shape=[SDS((bsz, seq, D), BF16), SDS((bsz, seq, D), F32), SDS((bsz, seq, D), F32)],
        compiler_params=_cp("parallel", "parallel"),
    )(ya, yb, p3, p3, wout, x1, gt)


def _merge_bwd(dx2, gt, mix, ya, yb, p3, wout, name):
    bsz, seq, _ = ya.shape
    tm = _pick_tile(seq, (512, 256, 128))

    def body(dx_ref, gt_ref, mix_ref, ya_ref, yb_ref, ga_ref, gbb_ref, w_ref, dmix_ref, dya_ref, dyb_ref, dp_ref, dgt_ref):
        i = pl.program_id(1)
        dx = dx_ref[0]
        dmix = (gt_ref[0] * dx).astype(BF16)
        dmix_ref[0] = dmix
        part = jnp.sum(dx * mix_ref[0], axis=0, keepdims=True)

        @pl.when(i == 0)
        def _():
            dgt_ref[0] = part

        @pl.when(i > 0)
        def _():
            dgt_ref[0] += part

        dmg = _dot_nt(dmix, w_ref[...])
        sa = _sigmoid(ga_ref[0].astype(F32))
        sb = _sigmoid(gbb_ref[0].astype(F32))
        dya_ref[0] = (dmg * sa).astype(BF16)
        dyb_ref[0] = (dmg * sb).astype(BF16)
        dp_ref[0, :, 0:D] = (dmg * ya_ref[0].astype(F32) * sa * (1.0 - sa)).astype(BF16)
        dp_ref[0, :, D:2 * D] = (dmg * yb_ref[0].astype(F32) * sb * (1.0 - sb)).astype(BF16)

    bshape = SDS((bsz, seq, D), BF16)
    return pl.pallas_call(
        body, name=name, grid=(bsz, seq // tm),
        in_specs=[_row(tm, D), _seqvec(D), _row(tm, D), _row(tm, D), _row(tm, D), _row(tm, D, 0), _row(tm, D, 1),
                  _full2((D, D))],
        out_specs=[_row(tm, D), _row(tm, D), _row(tm, D), _row(tm, 2 * D), _seqvec(D)],
        out_shape=[bshape, bshape, bshape, SDS((bsz, seq, 2 * D), BF16), SDS((bsz, 1, D), F32)],
        compiler_params=_cp("arbitrary", "arbitrary"),
    )(dx2, gt, mix, ya, yb, p3, p3, wout)


def _final_loss(x3, gfin, target, name):
    bsz, seq, dm = x3.shape
    tm = _pick_tile(seq, (512, 256, 128))

    def body(x_ref, g_ref, t_ref, dx_ref, loss_ref, dg_ref):
        b = pl.program_id(0)
        i = pl.program_id(1)
        xf = x_ref[0]
        gv = g_ref[...]
        r = lax.rsqrt(jnp.mean(xf * xf, axis=-1, keepdims=True) + EPS)
        xhat = xf * r
        e = xhat * gv - t_ref[0]
        dy = e * (1.0 / dm)
        dxh = dy * gv
        dx_ref[0] = r * (dxh - xhat * jnp.mean(dxh * xhat, axis=-1, keepdims=True))
        p_l = jnp.sum(e * e, axis=0, keepdims=True) * (0.5 / dm)
        p_g = jnp.sum(dy * xhat, axis=0, keepdims=True)

        @pl.when((i == 0) & (b == 0))
        def _():
            loss_ref[...] = p_l
            dg_ref[...] = p_g

        @pl.when((i > 0) | (b > 0))
        def _():
            loss_ref[...] += p_l
            dg_ref[...] += p_g

    return pl.pallas_call(
        body, name=name, grid=(bsz, seq // tm),
        in_specs=[_row(tm, dm), _full2((1, dm)), _row(tm, dm)],
        out_specs=[_row(tm, dm), _full2((1, dm)), _full2((1, dm))],
        out_shape=[SDS((bsz, seq, dm), F32), SDS((1, dm), F32), SDS((1, dm), F32)],
        compiler_params=_cp("arbitrary", "arbitrary"),
    )(x3, gfin, target)


def _ada_fwd(c_all, w_shard, b_shard):
    nb = c_all.shape[0]
    n = w_shard.shape[2]

    def body(c_ref, w_ref, b_ref, o_ref):
        cv = c_ref[...]
        cond = (cv * _sigmoid(cv)).astype(BF16)
        o_ref[...] = _dot(cond, w_ref[0].astype(BF16)) + b_ref[...]

    return pl.pallas_call(body, name="ada_fwd", out_shape=SDS((nb, n), F32), compiler_params=_cp())(
        c_all, w_shard, b_shard)


def _ada_bwd(c_all, dmod_shard, dmod_all):
    n = dmod_shard.shape[1]

    def body(c_ref, ds_ref, da_ref, gw_ref, gb_ref):
        cv = c_ref[...]
        cond = (cv * _sigmoid(cv)).astype(BF16)
        gw_ref[...] = _dot_tn(cond, ds_ref[...].astype(BF16))
        gb_ref[...] = jnp.sum(da_ref[...], axis=0, keepdims=True)

    return pl.pallas_call(
        body, name="ada_bwd", out_shape=[SDS((D, n), F32), SDS((1, dmod_all.shape[1]), F32)], compiler_params=_cp(),
    )(c_all, dmod_shard, dmod_all)


def _adamw_math(w, g, m, v):
    m = B1 * m + (1.0 - B1) * g
    v = B2 * v + (1.0 - B2) * (g * g)
    delta = -LR * ((m / BC1) / (jnp.sqrt(v / BC2) + AEPS) + WD * w)
    return delta, m, v


def _adamw_big(w, m, v, recv_own, recv_sib, name):
    _, rows, cols = w.shape
    tr = _pick_tile(rows, tuple(t for t in (512, 256, 128, 64, 32, 16, 8) if t * cols <= 96 * 1024))

    def body(w_ref, m_ref, v_ref, a_ref, b_ref, g_ref, d_ref, mo_ref, vo_ref):
        def chip_sum(r):
            acc = r[0].astype(F32)
            for k in range(1, N_CHIPS):
                acc = acc + r[k].astype(F32)
            return acc

        g = chip_sum(a_ref) + chip_sum(b_ref)
        delta, mn, vn = _adamw_math(w_ref[0], g, m_ref[0], v_ref[0])
        g_ref[0] = g
        d_ref[0] = delta
        mo_ref[0] = mn
        vo_ref[0] = vn

    own = pl.BlockSpec((1, tr, cols), lambda i: (0, i, 0))
    rspec = pl.BlockSpec((N_CHIPS, tr, cols), lambda i: (0, i, 0))
    shp = SDS(w.shape, F32)
    return pl.pallas_call(
        body, name=name, grid=(rows // tr,),
        in_specs=[own, own, own, rspec, rspec], out_specs=[own, own, own, own], out_shape=[shp, shp, shp, shp],
        compiler_params=_cp("parallel"),
    )(w, m, v, recv_own, recv_sib)


def _adamw_plain(w, m, v, g, name):
    def body(w_ref, m_ref, v_ref, g_ref, d_ref, mo_ref, vo_ref):
        delta, mn, vn = _adamw_math(w_ref[...], g_ref[...], m_ref[...], v_ref[...])
        d_ref[...] = delta
        mo_ref[...] = mn
        vo_ref[...] = vn

    shp = SDS(w.shape, F32)
    return pl.pallas_call(body, name=name, out_shape=[shp, shp, shp], compiler_params=_cp())(w, m, v, g)


def _adamw_rows(w, m, v, g, name):
    _, rows, cols = w.shape
    tr = _pick_tile(rows, (128, 64, 32, 16, 8))

    def body(w_ref, m_ref, v_ref, g_ref, d_ref, mo_ref, vo_ref):
        delta, mn, vn = _adamw_math(w_ref[0], g_ref[...], m_ref[0], v_ref[0])
        d_ref[0] = delta
        mo_ref[0] = mn
        vo_ref[0] = vn

    spec = pl.BlockSpec((1, tr, cols), lambda i: (0, i, 0))
    shp = SDS(w.shape, F32)
    return pl.pallas_call(
        body, name=name, grid=(rows // tr,), in_specs=[spec] * 3 + [pl.BlockSpec((tr, cols), lambda i: (i, 0))],
        out_specs=[spec] * 3, out_shape=[shp] * 3, compiler_params=_cp("parallel"),
    )(w, m, v, g)


def _sum_slabs(r, name):
    n, rows, cols = r.shape
    tr = _pick_tile(rows, (256, 128, 64))

    def body(r_ref, o_ref):
        acc = r_ref[0].astype(F32)
        for j in range(1, n):
            acc = acc + r_ref[j].astype(F32)
        o_ref[...] = acc

    return pl.pallas_call(
        body, name=name, grid=(rows // tr,),
        in_specs=[pl.BlockSpec((n, tr, cols), lambda i: (0, i, 0))],
        out_specs=pl.BlockSpec((tr, cols), lambda i: (i, 0)), out_shape=SDS((rows, cols), F32),
        compiler_params=_cp("parallel"),
    )(r)


def _place():
    return lax.axis_index("x"), lax.axis_index("y"), lax.axis_index("c")


def _all_gather8(blk, name):
    m_per, n = blk.shape

    def body(x_ref, out_ref, send_sems, recv_sems, local_sem):
        x, y, c = _place()
        me, sibling = (x, y, c), (x, y, 1 - c)
        chips = [(1 - x, y), (x, 1 - y), (1 - x, 1 - y)]

        def rows(px, py, pc):
            return out_ref.at[pl.ds((4 * px + 2 * py + pc) * m_per, m_per), :]

        def copy(k, block, to, src=None):
            return pltpu.make_async_remote_copy(
                src_ref=rows(*block) if src is None else src, dst_ref=rows(*block),
                send_sem=send_sems.at[k], recv_sem=recv_sems.at[k], device_id=to, device_id_type=MESH)

        mine = pltpu.make_async_copy(x_ref, rows(*me), local_sem)
        mine.start()
        first = [copy(0, me, sibling, src=x_ref)]
        first += [copy(1 + j, me, (*chip, c), src=x_ref) for j, chip in enumerate(chips)]
        for cp in first:
            cp.start()
        passed = [copy(4 + j, (*chip, c), sibling) for j, chip in enumerate(chips)]
        for j, chip in enumerate(chips):
            copy(1 + j, (*chip, c), me).wait_recv()
            passed[j].start()
        copy(0, sibling, me).wait_recv()
        for j, chip in enumerate(chips):
            copy(4 + j, (*chip, 1 - c), me).wait_recv()
        for cp in first + passed:
            cp.wait_send()
        mine.wait()

    return pl.pallas_call(
        body, name=name, out_shape=SDS((N_DEV * m_per, n), blk.dtype),
        in_specs=[pl.BlockSpec(memory_space=pltpu.VMEM)], out_specs=pl.BlockSpec(memory_space=pltpu.VMEM),
        scratch_shapes=[pltpu.SemaphoreType.DMA((7,)), pltpu.SemaphoreType.DMA((7,)), pltpu.SemaphoreType.DMA],
        compiler_params=pltpu.CompilerParams(vmem_limit_bytes=VMEM_LIMIT),
    )(blk)


def _chip_peers(x, y):
    return [(1 - x, y), (x, 1 - y), (1 - x, 1 - y)]


def _peer_copies(src_refs, land_refs, send_sems, recv_sems, scatter, landed):
    x, y, c = _place()
    cps = []
    for a, (src_ref, land_ref) in enumerate(zip(src_refs, land_refs)):
        for j, (px, py) in enumerate(_chip_peers(x, y)):
            src = src_ref.at[2 * px + py] if scatter else src_ref
            dst = land_ref.at[2 * px + py] if landed else land_ref.at[2 * x + y]
            cps.append(pltpu.make_async_remote_copy(
                src_ref=src, dst_ref=dst, send_sem=send_sems.at[3 * a + j], recv_sem=recv_sems.at[3 * a + j],
                device_id=(px, py, c), device_id_type=MESH))
    return cps


def _exchange_chips(srcs, scatter, name):
    n = len(srcs)

    def body(*refs):
        src_refs, land_refs = refs[:n], refs[n:2 * n]
        send_sems, recv_sems, local_sems = refs[2 * n:]
        x, y, _ = _place()
        me = 2 * x + y
        mine = [pltpu.make_async_copy(s.at[me] if scatter else s, l.at[me], local_sems.at[a])
                for a, (s, l) in enumerate(zip(src_refs, land_refs))]
        for cp in mine:
            cp.start()
        out = _peer_copies(src_refs, land_refs, send_sems, recv_sems, scatter, False)
        for cp in out:
            cp.start()
        for cp in _peer_copies(src_refs, land_refs, send_sems, recv_sems, scatter, True):
            cp.wait_recv()
        for cp in out:
            cp.wait_send()
        for cp in mine:
            cp.wait()

    anyspec = pl.BlockSpec(memory_space=pl.ANY)
    shapes = [SDS(s.shape if scatter else (N_CHIPS,) + s.shape, s.dtype) for s in srcs]
    return pl.pallas_call(
        body, name=name, out_shape=shapes, in_specs=[anyspec] * n, out_specs=[anyspec] * n,
        scratch_shapes=[pltpu.SemaphoreType.DMA((3 * n,)), pltpu.SemaphoreType.DMA((3 * n,)),
                        pltpu.SemaphoreType.DMA((n,))],
        compiler_params=pltpu.CompilerParams(vmem_limit_bytes=VMEM_LIMIT),
    )(*srcs)


_HBM = pl.BlockSpec(memory_space=pltpu.HBM)
_SEM = pl.BlockSpec(memory_space=pltpu.SEMAPHORE)
_EFFECT = pltpu.SideEffectType.DATAFLOW_SIDE_EFFECTING


def _exchange_begin(srcs, lands, scatter, name):
    n = len(srcs)

    def body(*refs):
        src_refs, land_refs = refs[:n], refs[n:2 * n]
        send_sems, recv_sems = refs[2 * n:2 * n + 2]
        token = refs[-1]
        for cp in _peer_copies(src_refs, land_refs, send_sems, recv_sems, scatter, False):
            cp.start()
        token[...] = jnp.zeros_like(token)

    res = pl.pallas_call(
        body, name=name,
        out_shape=(pltpu.SemaphoreType.DMA((3 * n,)), pltpu.SemaphoreType.DMA((3 * n,)),
                   *[pltpu.HBM(s.shape, s.dtype) for s in srcs], *[pltpu.HBM(l.shape, l.dtype) for l in lands],
                   SDS((SUB, LANE), F32)),
        in_specs=[_HBM] * (2 * n), out_specs=(_SEM, _SEM, *[_HBM] * (2 * n), pl.BlockSpec(memory_space=pltpu.VMEM)),
        input_output_aliases={i: 2 + i for i in range(2 * n)},
        compiler_params=pltpu.CompilerParams(has_side_effects=_EFFECT),
    )(*[pltpu.with_memory_space_constraint(a, pltpu.HBM) for a in (*srcs, *lands)])
    return res[0], res[1], res[2:2 + n], res[2 + n:2 + 2 * n], res[-1]


def _exchange_end(handle, after, scatter, name):
    send_sems, recv_sems, srcs, lands, _ = handle
    n = len(srcs)

    def body(*refs):
        src_refs, land_refs = refs[:n], refs[n:2 * n]
        send_sems, recv_sems = refs[2 * n:2 * n + 2]
        for cp in _peer_copies(src_refs, land_refs, send_sems, recv_sems, scatter, True):
            cp.wait_send()
            cp.wait_recv()

    res = pl.pallas_call(
        body, name=name,
        out_shape=tuple(pltpu.HBM(a.shape, a.dtype) for a in (*srcs, *lands)),
        in_specs=[_HBM] * (2 * n) + [_SEM, _SEM, pl.BlockSpec(memory_space=pl.ANY)], out_specs=tuple([_HBM] * (2 * n)),
        input_output_aliases={i: i for i in range(2 * n)},
        compiler_params=pltpu.CompilerParams(has_side_effects=_EFFECT),
    )(*srcs, *lands, send_sems, recv_sems, after)
    return list(res[n:])


def _own_slab(stack4, chip):
    idx = lax.broadcasted_iota(jnp.int32, (N_CHIPS,) + (1,) * (stack4.ndim - 1), 0)
    return jnp.where(idx == chip, stack4, jnp.zeros((), stack4.dtype))


def _swap_sibling(vs, name):
    n = len(vs)

    def body(*refs):
        in_refs, out_refs = refs[:n], refs[n:2 * n]
        send_sems, recv_sems = refs[2 * n:]
        x, y, c = _place()
        cps = [pltpu.make_async_remote_copy(
            src_ref=i, dst_ref=o, send_sem=send_sems.at[a], recv_sem=recv_sems.at[a],
            device_id=(x, y, 1 - c), device_id_type=MESH) for a, (i, o) in enumerate(zip(in_refs, out_refs))]
        for cp in cps:
            cp.start()
        for cp in cps:
            cp.wait()

    anyspec = pl.BlockSpec(memory_space=pl.ANY)
    return pl.pallas_call(
        body, name=name, out_shape=[SDS(v.shape, v.dtype) for v in vs], in_specs=[anyspec] * n, out_specs=[anyspec] * n,
        scratch_shapes=[pltpu.SemaphoreType.DMA((n,)), pltpu.SemaphoreType.DMA((n,))],
        compiler_params=pltpu.CompilerParams(vmem_limit_bytes=VMEM_LIMIT),
    )(*vs)


def _select(stacked, idx):
    out = stacked[0]
    for j in range(1, stacked.shape[0]):
        out = jnp.where(idx == j, stacked[j], out)
    return out


BIG = (
    ("w1_a", DFF // 4, D, False), ("w3_a", DFF // 4, D, False), ("w2_a", DFF // 4, D, False),
    ("w_in", D, 5632 // 4, True), ("w_conv_out", CW // 4, D, False), ("w_glu", SW // 4, SW, False),
    ("w_ssm_out", SW, D // 4, True), ("w_out", D // 4, D, False),
    ("w1_b", DFF // 4, D, False), ("w3_b", DFF // 4, D, False), ("w2_b", DFF // 4, D, False),
)
TRANSPOSED = frozenset(("w1_a", "w3_a", "w1_b", "w3_b"))
PACK_COLS = 1024


def _view(a, name):
    return jnp.transpose(a, (0, 2, 1)) if name in TRANSPOSED else a


def _full_from_stacked(st, split_cols):
    _, rows, cols = st.shape
    if split_cols:
        return st.transpose(1, 0, 2).reshape(rows, N_CHIPS * cols)
    return st.reshape(N_CHIPS * rows, cols)


def _stacked_from_full(full, rows, cols, split_cols):
    if split_cols:
        return full.reshape(rows, N_CHIPS, cols).transpose(1, 0, 2)
    return full.reshape(N_CHIPS, rows, cols)


def _blockdiag(t):
    r = lax.broadcasted_iota(jnp.int32, (SW, GP), 0) // NH
    cidx = lax.broadcasted_iota(jnp.int32, (SW, GP), 1) // NP
    dense = jnp.where(r == cidx, jnp.tile(t, (NG, 1)), 0.0)
    ub, sb = SW // SSM_SUPER, GP // SSM_SUPER
    return jnp.concatenate([dense[s * ub:(s + 1) * ub, s * sb:(s + 1) * sb] for s in range(SSM_SUPER)], axis=0)


def _blockdiag_extract(acc):
    gs = NG // SSM_SUPER
    a = acc.reshape(NG, NH, gs, NP)
    sel = (lax.broadcasted_iota(jnp.int32, (NG, 1, gs, 1), 0) % gs) == lax.broadcasted_iota(jnp.int32, (NG, 1, gs, 1), 2)
    a = jnp.sum(jnp.where(sel, a, 0.0), axis=2)
    return a.transpose(1, 0, 2).reshape(NH, GP)


def _to_t(p):
    return p.transpose(2, 0, 1).reshape(NH, GP)


def _from_t(t):
    return t.reshape(NH, NG, NP).transpose(1, 2, 0)


def _c_to_t(p):
    return p.transpose(1, 0, 2).reshape(NH, GP)


def _c_from_t(t):
    return t.reshape(NH, NG, NP).transpose(1, 0, 2)


def _ffn_forward(x, g, sh, sc, gt, w1, w3, w2, tag):
    h = _norm_mod(x, g, sh, sc, f"{tag}_norm")
    if callable(w1):
        w1, w3 = w1(h)
    a, b, hid = _swiglu_up(h, w1, w3, f"{tag}_up")
    w2 = w2(hid) if callable(w2) else w2
    f, xo = _ffn_down(hid, w2, x, gt, f"{tag}_down")
    return xo, (x, h, a, b, hid, f), w2


def _ffn_backward(dxo, saved, g, sc, gt, w1, w3, w2, tag, emit=lambda key, gw: 0.0):
    x, h, a, b, hid, f = saved
    dfs, da, db, dgt = _ffn_bwd_hid(dxo, gt, f, a, b, w2, f"{tag}_bwd_hid")
    h2 = _flat(h)
    gw2 = _mm(_flat(hid), _flat(dfs), ta=True, name=f"{tag}_gw2")
    tok = emit("w2", gw2)
    gw1 = _mm(_flat(da), h2, ta=True, name=f"{tag}_gw1")
    tok = tok + emit("w1", gw1)
    gw3 = _mm(_flat(db), h2, ta=True, name=f"{tag}_gw3")
    tok = tok + emit("w3", gw3)
    dx, dsh, dsc, dg = _dh_norm_bwd([da, db], [w1, w3], x, g, sc + tok, dxo, f"{tag}_bwd_dh", transposed=True)
    return dx, (dsh, dsc, dgt, dg), (gw1, gw3, gw2)


def kernel(x, c, w_ada, b_ada, g_ffn1, w1_a, w3_a, w2_a, g_mix, w_in, conv_w, w_conv_out, a_re, a_im, b_re, b_im, c_re, c_im, log_dt, d_skip, w_glu, w_ssm_out, w_out, g_ffn2, w1_b, w3_b, w2_b, g_final, loss_target, m_w_ada, m_b_ada, m_g_ffn1, m_w1_a, m_w3_a, m_w2_a, m_g_mix, m_w_in, m_conv_w, m_w_conv_out, m_a_re, m_a_im, m_b_re, m_b_im, m_c_re, m_c_im, m_log_dt, m_d_skip, m_w_glu, m_w_ssm_out, m_w_out, m_g_ffn2, m_w1_b, m_w3_b, m_w2_b, m_g_final, v_w_ada, v_b_ada, v_g_ffn1, v_w1_a, v_w3_a, v_w2_a, v_g_mix, v_w_in, v_conv_w, v_w_conv_out, v_a_re, v_a_im, v_b_re, v_b_im, v_c_re, v_c_im, v_log_dt, v_d_skip, v_w_glu, v_w_ssm_out, v_w_out, v_g_ffn2, v_w1_b, v_w3_b, v_w2_b, v_g_final):
    args = dict(locals())
    names = ["w_ada", "b_ada", "g_ffn1", "w1_a", "w3_a", "w2_a", "g_mix", "w_in", "conv_w", "w_conv_out", "a_re",
             "a_im", "b_re", "b_im", "c_re", "c_im", "log_dt", "d_skip", "w_glu", "w_ssm_out", "w_out", "g_ffn2",
             "w1_b", "w3_b", "w2_b", "g_final"]
    bsz, seq, _ = x.shape
    mx, my, mc = _place()
    chip = 2 * mx + my
    dev = 4 * mx + 2 * my + mc

    groups = (BIG[:3], BIG[3:8], BIG[8:])
    wfull = {}

    def shards(grp):
        return [_view(args[n], n)[0] for n, _, _, _ in grp]

    def unpack_group(gathered, grp):
        for (n, _, _, split), st in zip(grp, gathered):
            wfull[n] = _full_from_stacked(st, split)

    up_grp, down_grp = groups[0][:2], groups[0][2:]

    nmod_shard = NMOD * D // N_CHIPS
    c_all = _all_gather8(c.reshape(SUB, -1), "gather_c").reshape(N_DEV * bsz, D)
    b_shard = _select(b_ada.reshape(N_CHIPS, 1, nmod_shard), chip)
    mod_shard = _ada_fwd(c_all, w_ada, b_shard)
    nb = N_DEV * bsz
    cw_pad = jnp.pad(conv_w[0], ((0, SUB - 3), (0, nmod_shard - CW // N_CHIPS)))
    mod_st = _exchange_chips([jnp.concatenate([mod_shard, cw_pad], axis=0)], False, "gather_mod")[0]
    mod_all = mod_st[:, :nb].transpose(1, 0, 2).reshape(N_DEV, bsz, NMOD * D)
    mod = _select(mod_all, dev)

    def gather_begin(raw, name):
        srcs = [a.astype(BF16) for a in raw]
        lands = [_own_slab(jnp.broadcast_to(s[None], (N_CHIPS,) + s.shape), chip) for s in srcs]
        return _exchange_begin(srcs, lands, False, name)

    up_raw, mod = lax.optimization_barrier((shards(up_grp), mod))
    up_handle = gather_begin(up_raw, "gather_w_ffn1_up_start")
    (down_raw, mix_raw, ffn2_raw), up_token = lax.optimization_barrier(
        ((shards(down_grp), shards(groups[1]), shards(groups[2])), up_handle[4][0:1, 0:1]))
    down_handle = gather_begin(down_raw, "gather_w_ffn1_down_start")
    mix_handle = gather_begin(mix_raw, "gather_w_mix_start")
    ffn2_handle = gather_begin(ffn2_raw, "gather_w_ffn2_start")
    start_tokens = up_token + down_handle[4][0:1, 0:1] + mix_handle[4][0:1, 0:1] + ffn2_handle[4][0:1, 0:1]

    sh1, sc1, gt1, sh2, sc2, gt2, sh3, sc3, gt3 = [mod[:, None, j * D:(j + 1) * D] for j in range(NMOD)]
    convw = mod_st[:, nb:nb + 3, :CW // N_CHIPS].transpose(1, 0, 2).reshape(3, CW)
    convw8 = jnp.pad(convw, ((0, SUB - 3), (0, 0)))

    are, aim = a_re.reshape(1, GP), a_im.reshape(1, GP)
    ldt = jnp.broadcast_to(log_dt.reshape(NG, 1), (NG, NP)).reshape(1, GP)
    bre_t, bim_t = _to_t(b_re[0]), _to_t(b_im[0])
    abr, abi, bbr_t, bbi_t = _ssm_disc(are, aim, ldt, bre_t, bim_t)
    wb = jnp.concatenate([_blockdiag(bbr_t), _blockdiag(bbi_t)], axis=1).astype(BF16)
    wct = jnp.concatenate([_blockdiag(_c_to_t(c_re[0])), _blockdiag(-_c_to_t(c_im[0]))], axis=1).astype(BF16)
    ar8 = jnp.broadcast_to(abr, (SUB, GP))
    ai8 = jnp.broadcast_to(abi, (SUB, GP))

    def late_w2a(hid):
        unpack_group(_exchange_end(down_handle, hid, False, "gather_w_ffn1_down_wait"), down_grp)
        return wfull["w2_a"]

    def late_w13a(h):
        unpack_group(_exchange_end(up_handle, h, False, "gather_w_ffn1_up_wait"), up_grp)
        return wfull["w1_a"], wfull["w3_a"]

    x1, ffn1_saved, _ = _ffn_forward(x, g_ffn1 + start_tokens, sh1, sc1, gt1, late_w13a, None, late_w2a, "ffn1")
    unpack_group(_exchange_end(mix_handle, x1, False, "gather_w_mix_wait"), groups[1])

    h2 = _norm_mod(x1, g_mix, sh2, sc2, "mix_norm")
    h2f = _flat(h2)
    win = wfull["w_in"]
    win1, winu, win3 = win[:, :3 * CW], win[:, 3 * CW:3 * CW + SW], win[:, 3 * CW + SW:]
    p1 = _mm(h2f, win1, out_dtype=BF16, name="mix_in1").reshape(bsz, seq, 3 * CW)
    u = _mm(h2f, winu, name="mix_inu").reshape(bsz, seq, SW)
    p3 = _mm(h2f, win3, out_dtype=BF16, name="mix_in3").reshape(bsz, seq, 2 * D)

    ya_in = _conv_fwd(p1, convw8, "conv_fwd")
    ya = _mm(_flat(ya_in), wfull["w_conv_out"], out_dtype=BF16, name="conv_out").reshape(bsz, seq, D)

    xs, ys = _ssm_fwd(u, wb, wct, ar8, ai8, "ssm_fwd")
    s0, z, s1, s2, yb = _ssm_post(ys, u, d_skip, wfull["w_glu"], wfull["w_ssm_out"], "ssm_post")

    merged, mix, x2 = _merge_out(ya, yb, p3, wfull["w_out"], x1, gt2, "merge_out")
    unpack_group(_exchange_end(ffn2_handle, x2, False, "gather_w_ffn2_wait"), groups[2])
    x3, ffn2_saved, _ = _ffn_forward(x2, g_ffn2, sh3, sc3, gt3, wfull["w1_b"], wfull["w3_b"], wfull["w2_b"], "ffn2")

    dx3, lossvec, dgfin = _final_loss(x3, g_final.reshape(1, D), loss_target, "final_loss")
    loss = lax.psum(jnp.sum(lossvec), ("x", "y", "c"))

    gfull = {}
    dx2, (dsh3, dsc3, dgt3, dg3), (gfull["w1_b"], gfull["w3_b"], gfull["w2_b"]) = _ffn_backward(
        dx3, ffn2_saved, g_ffn2, sc3, gt3, wfull["w1_b"], wfull["w3_b"], wfull["w2_b"], "ffn2")

    def stack_group(grp):
        return [_stacked_from_full(gfull[n], rows, cols, split).astype(BF16) for n, rows, cols, split in grp]

    st_ffn2 = stack_group(groups[2])
    h_ffn2 = _exchange_begin(st_ffn2, [_own_slab(s, chip) for s in st_ffn2], True, "scatter_ffn2_start")

    dmix, dya, dyb, dp3, dgt2 = _merge_bwd(dx2, gt2 + h_ffn2[4][0, 0], mix, ya, yb, p3, wfull["w_out"], "merge_bwd")
    gfull["w_out"] = _mm(_flat(merged), _flat(dmix), ta=True, name="gw_out")
    dp1, dconvw8 = _conv_bwd(dya, wfull["w_conv_out"], p1, convw8, "conv_bwd")
    gfull["w_conv_out"] = _mm(_flat(ya_in), _flat(dya), ta=True, name="gw_conv_out")
    ds0, dz, ddskip = _ssm_post_bwd(dyb, s0, z, u, d_skip, wfull["w_glu"], wfull["w_ssm_out"], "ssm_post_bwd")
    gfull["w_ssm_out"] = _mm(_flat(s2), _flat(dyb), ta=True, name="gw_ssm_out")
    gfull["w_glu"] = _mm(_flat(s1), _flat(dz), ta=True, name="gw_glu")
    du_ssm, dwb, dwct, dar8, dai8 = _ssm_bwd(ds0, u, xs, wb, wct, ar8, ai8, "ssm_bwd")
    du = _du_combine(du_ssm, ds0, d_skip, "du_combine")
    dx1, dsh2, dsc2, dgmix = _dh_norm_bwd([dp1, du, dp3], [win1, winu, win3], x1, g_mix, sc2, dx2, "mix_bwd_dh")
    gfull["w_in"] = jnp.concatenate([
        _mm(h2f, _flat(dp1), ta=True, name="gw_in1"), _mm(h2f, _flat(du), ta=True, name="gw_inu"),
        _mm(h2f, _flat(dp3), ta=True, name="gw_in3")], axis=1)

    st_mix = stack_group(groups[1])
    h_mix = _exchange_begin(st_mix, [_own_slab(s, chip) for s in st_mix], True, "scatter_mix_start")

    ffn1_names = {"w1": BIG[0], "w3": BIG[1], "w2": BIG[2]}
    ffn1_handles = {}

    def emit_ffn1(key, gw):
        n, rows, cols, split = ffn1_names[key]
        st = _stacked_from_full(gw, rows, cols, split).astype(BF16)
        ffn1_handles[n] = _exchange_begin([st], [_own_slab(st, chip)], True, f"scatter_ffn1_{key}_start")
        return ffn1_handles[n][4][0, 0]

    grad_x, (dsh1, dsc1, dgt1, dg1), _ = _ffn_backward(
        dx1, ffn1_saved, g_ffn1, sc1, gt1 + h_mix[4][0, 0], wfull["w1_a"], wfull["w3_a"], wfull["w2_a"], "ffn1",
        emit=emit_ffn1)

    sbw = GP // SSM_SUPER
    d_are, d_aim, d_ldt, d_bre_t, d_bim_t = _ssm_disc_bwd(
        are, aim, ldt, bre_t, bim_t, jnp.sum(dar8, axis=0, keepdims=True), jnp.sum(dai8, axis=0, keepdims=True),
        _blockdiag_extract(dwb[:, :sbw]), _blockdiag_extract(dwb[:, sbw:]))
    d_cre = _c_from_t(_blockdiag_extract(dwct[:, :sbw]))
    d_cim = -_c_from_t(_blockdiag_extract(dwct[:, sbw:]))

    small_parts = [dg1, dgmix, dg3, dgfin, dconvw8[:3], d_are, d_aim, _from_t(d_bre_t), _from_t(d_bim_t), d_cre, d_cim,
                   jnp.sum(d_ldt.reshape(NG, NP), axis=1), ddskip]
    small_sizes = [int(p.size) for p in small_parts]
    n_small = sum(small_sizes)
    n_small_pad = -(-n_small // (SUB * PACK_COLS)) * (SUB * PACK_COLS)
    dmod = jnp.concatenate([dsh1, dsc1, dgt1, dsh2, dsc2, dgt2, dsh3, dsc3, dgt3], axis=2).reshape(bsz * NMOD * D)
    flat = jnp.concatenate([p.reshape(-1) for p in small_parts] + [jnp.zeros((n_small_pad - n_small,), F32), dmod])
    allg = _all_gather8(flat.reshape(SUB, -1), "gather_small").reshape(N_DEV, -1)
    small = _sum_slabs(allg[:, :n_small_pad].reshape(N_DEV, -1, PACK_COLS), "sum_small").reshape(-1)
    sg, o = [], 0
    for p, sz in zip(small_parts, small_sizes):
        sg.append(small[o:o + sz].reshape(p.shape))
        o += sz
    (g_g1, g_gmix, g_g3, g_gfin, g_convw, g_are, g_aim, g_bre, g_bim, g_cre, g_cim, g_ldt, g_dskip) = sg

    dmod_all = allg[:, n_small_pad:].reshape(nb, NMOD * D)
    dmod_shard = _select(dmod_all.reshape(nb, N_CHIPS, nmod_shard).transpose(1, 0, 2), chip)
    g_wada, g_bada = _ada_bwd(c_all, dmod_shard, dmod_all)

    recv_ffn2 = _exchange_end(h_ffn2, g_wada, True, "scatter_ffn2_wait")
    recv_mix = _exchange_end(h_mix, g_wada, True, "scatter_mix_wait")
    recv_ffn1 = [_exchange_end(ffn1_handles[n], g_wada, True, f"scatter_ffn1_{n}_wait")[0] for n, _, _, _ in groups[0]]
    recv = [*recv_ffn1, *recv_mix, *recv_ffn2]
    recv_sib = _swap_sibling(recv, "swap_sibling")

    grads, deltas, new_m, new_v = {}, {}, {}, {}
    for (n, _, _, _), r_own, r_sib in zip(BIG, recv, recv_sib):
        res = _adamw_big(_view(args[n], n), _view(args["m_" + n], n), _view(args["v_" + n], n), r_own, r_sib,
                         f"adamw_{n}")
        grads[n], deltas[n], new_m[n], new_v[n] = [_view(r, n) for r in res]
    grads["w_ada"] = g_wada[None]
    deltas["w_ada"], new_m["w_ada"], new_v["w_ada"] = _adamw_rows(w_ada, m_w_ada, v_w_ada, g_wada, "adamw_w_ada")

    g_convw_shard = _select(g_convw.reshape(3, N_CHIPS, CW // N_CHIPS).transpose(1, 0, 2), chip)
    small_g = {"b_ada": g_bada, "g_ffn1": g_g1, "g_mix": g_gmix, "g_ffn2": g_g3, "g_final": g_gfin,
               "conv_w": g_convw_shard, "a_re": g_are, "a_im": g_aim, "b_re": g_bre, "b_im": g_bim,
               "c_re": g_cre, "c_im": g_cim, "log_dt": g_ldt, "d_skip": g_dskip}
    small_names = list(small_g)
    sizes = [int(args[n].size) for n in small_names]
    tot = sum(sizes)
    tot_pad = -(-tot // (SUB * PACK_COLS)) * (SUB * PACK_COLS)

    def pack(get):
        return jnp.concatenate([get(n).reshape(-1) for n in small_names] + [jnp.zeros((tot_pad - tot,), F32)]).reshape(
            -1, PACK_COLS)

    res = _adamw_plain(pack(lambda n: args[n]), pack(lambda n: args["m_" + n]), pack(lambda n: args["v_" + n]),
                       pack(lambda n: small_g[n]), "adamw_small")
    o = 0
    for n, sz in zip(small_names, sizes):
        shp = args[n].shape
        grads[n] = small_g[n].reshape(shp)
        deltas[n], new_m[n], new_v[n] = [r.reshape(-1)[o:o + sz].reshape(shp) for r in res]
        o += sz

    return (loss, grad_x, *[grads[n] for n in names], *[deltas[n] for n in names],
            *[new_m[n] for n in names], *[new_v[n] for n in names])
```

```python
import functools
import math

import jax
import jax.numpy as jnp
from jax import lax
from jax.experimental import pallas as pl
from jax.experimental.pallas import tpu as pltpu

F32 = jnp.float32
BF16 = jnp.bfloat16
SDS = jax.ShapeDtypeStruct
MESH = pl.DeviceIdType.MESH

D = 1024
DFF = 2816
CW = 1024
SW = 512
NG, NP, NH = 32, 64, 16
GP = NG * NP
NMOD = 9
EPS = 1e-6
N_CHIPS = 4
N_DEV = 8
SUB = 8
LANE = 128
SSM_SUPER = 4
VMEM_LIMIT = 50 * 1024 * 1024

LR, B1, B2, AEPS, WD, STEP = 0.001, 0.9, 0.999, 1e-08, 0.01, 10
BC1 = 1.0 - B1 ** STEP
BC2 = 1.0 - B2 ** STEP


def _cp(*sem):
    return pltpu.CompilerParams(dimension_semantics=sem or None, vmem_limit_bytes=VMEM_LIMIT)


def _pick_tile(n, cands):
    for t in cands:
        if t <= n and n % t == 0:
            return t
    return n


def _dot(a, b):
    return lax.dot_general(a, b, (((1,), (0,)), ((), ())), preferred_element_type=F32)


def _dot_nt(a, b):
    return lax.dot_general(a, b, (((1,), (1,)), ((), ())), preferred_element_type=F32)


def _dot_tn(a, b):
    return lax.dot_general(a, b, (((0,), (0,)), ((), ())), preferred_element_type=F32)


def _row(tm, width, col=0):
    return pl.BlockSpec((1, tm, width), lambda b, i, *_: (b, i, col))


def _seqvec(width):
    return pl.BlockSpec((1, 1, width), lambda b, *_: (b, 0, 0))


def _full2(shape):
    return pl.BlockSpec(shape, lambda *_: (0, 0))


def _sigmoid(x):
    return jax.nn.sigmoid(x)


def _mm(a, b, *, ta=False, tb=False, out_dtype=F32, name):
    if ta:
        kdim, m = a.shape
    else:
        m, kdim = a.shape
    n = b.shape[0] if tb else b.shape[1]
    tm = _pick_tile(m, (1408, 1024, 512, 256, 128))
    tn = _pick_tile(n, (1408, 1024, 512, 256, 128))
    tk = _pick_tile(kdim, (1024, 512, 256, 128))
    nk = kdim // tk

    def body(a_ref, b_ref, o_ref, acc_ref):
        k = pl.program_id(2)

        @pl.when(k == 0)
        def _():
            acc_ref[...] = jnp.zeros_like(acc_ref)

        av = a_ref[...].astype(BF16)
        bv = b_ref[...].astype(BF16)
        dn = (((0 if ta else 1,), (1 if tb else 0,)), ((), ()))
        acc_ref[...] += lax.dot_general(av, bv, dn, preferred_element_type=F32)

        @pl.when(k == nk - 1)
        def _():
            o_ref[...] = acc_ref[...].astype(out_dtype)

    a_spec = pl.BlockSpec((tk, tm), lambda i, j, k: (k, i)) if ta else pl.BlockSpec((tm, tk), lambda i, j, k: (i, k))
    b_spec = pl.BlockSpec((tn, tk), lambda i, j, k: (j, k)) if tb else pl.BlockSpec((tk, tn), lambda i, j, k: (k, j))
    return pl.pallas_call(
        body, name=name, grid=(m // tm, n // tn, nk),
        in_specs=[a_spec, b_spec],
        out_specs=pl.BlockSpec((tm, tn), lambda i, j, k: (i, j)),
        out_shape=SDS((m, n), out_dtype),
        scratch_shapes=[pltpu.VMEM((tm, tn), F32)],
        compiler_params=_cp("parallel", "parallel", "arbitrary"),
    )(a, b)


def _flat(a):
    return a.reshape(-1, a.shape[-1])


def _norm_mod(x, g, sh, sc, name):
    bsz, seq, dm = x.shape
    tm = _pick_tile(seq, (512, 256, 128))

    def body(x_ref, g_ref, sh_ref, sc_ref, o_ref):
        xf = x_ref[0]
        r = lax.rsqrt(jnp.mean(xf * xf, axis=-1, keepdims=True) + EPS)
        hn = xf * r * g_ref[...]
        o_ref[0] = (hn * (1.0 + sc_ref[0]) + sh_ref[0]).astype(BF16)

    return pl.pallas_call(
        body, name=name, grid=(bsz, seq // tm),
        in_specs=[_row(tm, dm), _full2((1, dm)), _seqvec(dm), _seqvec(dm)],
        out_specs=_row(tm, dm), out_shape=SDS((bsz, seq, dm), BF16),
        compiler_params=_cp("parallel", "parallel"),
    )(x, g, sh, sc)


def _swiglu_up(h, w1, w3, name):
    bsz, seq, dm = h.shape
    nf = w1.shape[0]
    tm = _pick_tile(seq, (512, 256, 128))
    tn = _pick_tile(nf, (1408, 512, 256, 128))

    def body(h_ref, w1_ref, w3_ref, a_ref, b_ref, hid_ref):
        hv = h_ref[0]
        a = _dot_nt(hv, w1_ref[...])
        b = _dot_nt(hv, w3_ref[...])
        sg = _sigmoid(a)
        sa = a * sg
        a_ref[0] = (b * (sg * (1.0 + a * (1.0 - sg)))).astype(BF16)
        b_ref[0] = sa.astype(BF16)
        hid_ref[0] = (sa * b).astype(BF16)

    wspec = pl.BlockSpec((tn, dm), lambda n, b, i: (n, 0))
    ospec = pl.BlockSpec((1, tm, tn), lambda n, b, i: (b, i, n))
    shp = SDS((bsz, seq, nf), BF16)
    return pl.pallas_call(
        body, name=name, grid=(nf // tn, bsz, seq // tm),
        in_specs=[pl.BlockSpec((1, tm, dm), lambda n, b, i: (b, i, 0)), wspec, wspec],
        out_specs=[ospec, ospec, ospec], out_shape=[shp, shp, shp],
        compiler_params=_cp("parallel", "parallel", "parallel"),
    )(h, w1, w3)


def _ffn_down(hid, w2, x, gt, name):
    bsz, seq, nf = hid.shape
    dm = w2.shape[1]
    tm = _pick_tile(seq, (512, 256, 128))

    def body(hid_ref, w2_ref, x_ref, gt_ref, f_ref, xo_ref):
        f = _dot(hid_ref[0], w2_ref[...])
        f_ref[0] = f.astype(BF16)
        xo_ref[0] = x_ref[0] + 0.5 * gt_ref[0] * f

    shp = SDS((bsz, seq, dm), F32)
    return pl.pallas_call(
        body, name=name, grid=(bsz, seq // tm),
        in_specs=[_row(tm, nf), _full2((nf, dm)), _row(tm, dm), _seqvec(dm)],
        out_specs=[_row(tm, dm), _row(tm, dm)], out_shape=[SDS((bsz, seq, dm), BF16), shp],
        compiler_params=_cp("parallel", "parallel"),
    )(hid, w2, x, gt)


def _ffn_bwd_hid(dxo, gt, f, a, b, w2, name):
    bsz, seq, dm = dxo.shape
    nf = a.shape[2]
    tm = _pick_tile(seq, (512, 256, 128))
    tn = _pick_tile(nf, (1408, 512, 256, 128))

    def body(dxo_ref, gt_ref, f_ref, a_ref, b_ref, w2_ref, dfs_ref, da_ref, db_ref, dgt_ref):
        i = pl.program_id(1)
        n = pl.program_id(2)

        @pl.when(n == 0)
        def _():
            dxo = dxo_ref[0]
            dfs_ref[0] = (0.5 * gt_ref[0] * dxo).astype(BF16)
            part = jnp.sum(0.5 * dxo * f_ref[0].astype(F32), axis=0, keepdims=True)

            @pl.when(i == 0)
            def _():
                dgt_ref[0] = part

            @pl.when(i > 0)
            def _():
                dgt_ref[0] += part

        dhid = _dot_nt(dfs_ref[0], w2_ref[pl.ds(pl.multiple_of(n * tn, tn), tn), :])
        dh16 = dhid.astype(BF16)
        da_ref[0] = dh16 * a_ref[0]
        db_ref[0] = dh16 * b_ref[0]

    hspec = pl.BlockSpec((1, tm, tn), lambda b, i, n: (b, i, n))
    return pl.pallas_call(
        body, name=name, grid=(bsz, seq // tm, nf // tn),
        in_specs=[_row(tm, dm), _seqvec(dm), _row(tm, dm), hspec, hspec, _full2((nf, dm))],
        out_specs=[_row(tm, dm), hspec, hspec, _seqvec(dm)],
        out_shape=[SDS((bsz, seq, dm), BF16), SDS((bsz, seq, nf), BF16), SDS((bsz, seq, nf), BF16),
                   SDS((bsz, 1, dm), F32)],
        compiler_params=_cp("arbitrary", "arbitrary", "arbitrary"),
    )(dxo, gt, f, a, b, w2)


def _dh_norm_bwd(pieces, weights, x, g, sc, dxo, name, transposed=False):
    bsz, seq, dm = x.shape
    tm = _pick_tile(seq, (512, 256, 128))
    npc = len(pieces)
    dot = _dot if transposed else _dot_nt

    def body(*refs):
        p_refs = refs[:npc]
        w_hbm = refs[npc:2 * npc]
        x_ref, g_ref, sc_ref, dxo_ref, dx_ref, dsh_ref, dsc_ref, dg_ref = refs[2 * npc:2 * npc + 8]
        w_refs = refs[2 * npc + 8:]
        b = pl.program_id(0)
        i = pl.program_id(1)

        @pl.when((i == 0) & (b == 0))
        def _():
            for src, dst in zip(w_hbm, w_refs):
                pltpu.sync_copy(src, dst)

        dh = dot(p_refs[0][0], w_refs[0][...])
        for j in range(1, npc):
            dh = dh + dot(p_refs[j][0], w_refs[j][...])
        xf = x_ref[0]
        gv = g_ref[...]
        r = lax.rsqrt(jnp.mean(xf * xf, axis=-1, keepdims=True) + EPS)
        xhat = xf * r
        dhn = dh * (1.0 + sc_ref[0])
        p_sh = jnp.sum(dh, axis=0, keepdims=True)
        p_sc = jnp.sum(dh * (xhat * gv), axis=0, keepdims=True)
        p_g = jnp.sum(dhn * xhat, axis=0, keepdims=True)
        dxh = dhn * gv
        dx_ref[0] = dxo_ref[0] + r * (dxh - xhat * jnp.mean(dxh * xhat, axis=-1, keepdims=True))

        @pl.when(i == 0)
        def _():
            dsh_ref[0] = p_sh
            dsc_ref[0] = p_sc

        @pl.when(i > 0)
        def _():
            dsh_ref[0] += p_sh
            dsc_ref[0] += p_sc

        @pl.when((i == 0) & (b == 0))
        def _():
            dg_ref[...] = p_g

        @pl.when((i > 0) | (b > 0))
        def _():
            dg_ref[...] += p_g

    return pl.pallas_call(
        body, name=name, grid=(bsz, seq // tm),
        in_specs=[_row(tm, p.shape[2]) for p in pieces] + [pl.BlockSpec(memory_space=pl.ANY)] * npc + [
            _row(tm, dm), _full2((1, dm)), _seqvec(dm), _row(tm, dm)],
        out_specs=[_row(tm, dm), _seqvec(dm), _seqvec(dm), _full2((1, dm))],
        out_shape=[SDS((bsz, seq, dm), F32), SDS((bsz, 1, dm), F32), SDS((bsz, 1, dm), F32), SDS((1, dm), F32)],
        scratch_shapes=[pltpu.VMEM(w.shape, w.dtype) for w in weights],
        compiler_params=_cp("arbitrary", "arbitrary"),
    )(*pieces, *weights, x, g, sc, dxo)


HALO = 16


def _conv_core(gc, v, gch, vh, w, first):
    cv = gc * v
    halo = jnp.where(first, 0.0, gch * vh)
    ext = jnp.concatenate([halo, cv], axis=0)
    cv1 = pltpu.roll(ext, 1, 0)[HALO:]
    cv2 = pltpu.roll(ext, 2, 0)[HALO:]
    conv = w[0:1] * cv2 + w[1:2] * cv1 + w[2:3] * cv
    return cv, cv1, cv2, conv


def _prev_halo(tm, col):
    return pl.BlockSpec((1, HALO, CW), lambda b, i, *_: (b, jnp.maximum(i * (tm // HALO) - 1, 0), col))


def _next_halo(tm, seq, col):
    return pl.BlockSpec((1, HALO, CW), lambda b, i, *_: (b, jnp.minimum((i + 1) * (tm // HALO), seq // HALO - 1), col))


def _conv_fwd(p1, convw8, name):
    bsz, seq, _ = p1.shape
    tm = _pick_tile(seq, (512, 256, 128))

    def body(gb_ref, gc_ref, v_ref, gch_ref, vh_ref, w_ref, o_ref):
        first = pl.program_id(1) == 0
        _, _, _, conv = _conv_core(gc_ref[0].astype(F32), v_ref[0].astype(F32), gch_ref[0].astype(F32),
                                   vh_ref[0].astype(F32), w_ref[...], first)
        o_ref[0] = (gb_ref[0].astype(F32) * conv).astype(BF16)

    return pl.pallas_call(
        body, name=name, grid=(bsz, seq // tm),
        in_specs=[_row(tm, CW, 0), _row(tm, CW, 1), _row(tm, CW, 2), _prev_halo(tm, 1), _prev_halo(tm, 2),
                  _full2((8, CW))],
        out_specs=_row(tm, CW), out_shape=SDS((bsz, seq, CW), BF16),
        compiler_params=_cp("parallel", "parallel"),
    )(p1, p1, p1, p1, p1, convw8)


def _conv_bwd(dya, wco, p1, convw8, name):
    bsz, seq, _ = p1.shape
    tm = _pick_tile(seq, (512, 256, 128))
    nt = seq // tm
    ext_rows = tm + HALO

    def body(dya_ref, dyan_ref, wco_ref, gb_ref, gbn_ref, gc_ref, v_ref, gch_ref, vh_ref, w_ref, dp_ref, dw_ref):
        b = pl.program_id(0)
        i = pl.program_id(1)
        w = w_ref[...]
        gc = gc_ref[0].astype(F32)
        vv = v_ref[0].astype(F32)
        cv, cv1, cv2, conv = _conv_core(gc, vv, gch_ref[0].astype(F32), vh_ref[0].astype(F32), w, i == 0)
        dya_ext = jnp.concatenate([dya_ref[0], dyan_ref[0]], axis=0)
        dyain_ext = _dot_nt(dya_ext, wco_ref[...])
        gb_ext = jnp.concatenate([gb_ref[0], gbn_ref[0]], axis=0).astype(F32)
        rows = lax.broadcasted_iota(jnp.int32, (ext_rows, 1), 0)
        dconv_ext = jnp.where((rows < tm) | (i < nt - 1), dyain_ext * gb_ext, 0.0)
        dconv = dconv_ext[:tm]
        dconv1 = pltpu.roll(dconv_ext, ext_rows - 1, 0)[:tm]
        dconv2 = pltpu.roll(dconv_ext, ext_rows - 2, 0)[:tm]
        dcv = w[2:3] * dconv + w[1:2] * dconv1 + w[0:1] * dconv2
        dp_ref[0, :, 0:CW] = (dyain_ext[:tm] * conv).astype(BF16)
        dp_ref[0, :, CW:2 * CW] = (dcv * vv).astype(BF16)
        dp_ref[0, :, 2 * CW:3 * CW] = (dcv * gc).astype(BF16)
        g0 = jnp.sum(dconv * cv2, axis=0, keepdims=True)
        g1 = jnp.sum(dconv * cv1, axis=0, keepdims=True)
        g2 = jnp.sum(dconv * cv, axis=0, keepdims=True)
        upd = jnp.concatenate([g0, g1, g2, jnp.zeros((5, CW), F32)], axis=0)

        @pl.when((i == 0) & (b == 0))
        def _():
            dw_ref[...] = upd

        @pl.when((i > 0) | (b > 0))
        def _():
            dw_ref[...] += upd

    return pl.pallas_call(
        body, name=name, grid=(bsz, seq // tm),
        in_specs=[_row(tm, CW), _next_halo(tm, seq, 0), _full2((CW, D)),
                  _row(tm, CW, 0), _next_halo(tm, seq, 0), _row(tm, CW, 1), _row(tm, CW, 2),
                  _prev_halo(tm, 1), _prev_halo(tm, 2), _full2((8, CW))],
        out_specs=[_row(tm, 3 * CW), _full2((8, CW))],
        out_shape=[SDS((bsz, seq, 3 * CW), BF16), SDS((8, CW), F32)],
        compiler_params=_cp("arbitrary", "arbitrary"),
    )(dya, dya, wco, p1, p1, p1, p1, p1, p1, convw8)


def _disc(are, aim, ldt, bre, bim):
    dt = jnp.exp(ldt)
    mag = jnp.exp(are * dt)
    ang = aim * dt
    abr = mag * jnp.cos(ang)
    abi = mag * jnp.sin(ang)
    nr = abr - 1.0
    den = are * are + aim * aim
    cr = (nr * are + abi * aim) / den
    ci = (abi * are - nr * aim) / den
    return abr, abi, cr * bre - ci * bim, cr * bim + ci * bre


def _ssm_disc(are, aim, ldt, bre_t, bim_t):
    def body(are_ref, aim_ref, ldt_ref, bre_ref, bim_ref, abr_ref, abi_ref, bbr_ref, bbi_ref):
        abr, abi, bbr, bbi = _disc(are_ref[...], aim_ref[...], ldt_ref[...], bre_ref[...], bim_ref[...])
        abr_ref[...] = abr
        abi_ref[...] = abi
        bbr_ref[...] = bbr
        bbi_ref[...] = bbi

    v1, vh = SDS((1, GP), F32), SDS((NH, GP), F32)
    return pl.pallas_call(body, name="ssm_disc", out_shape=[v1, v1, vh, vh], compiler_params=_cp())(
        are, aim, ldt, bre_t, bim_t)


def _ssm_disc_bwd(are, aim, ldt, bre_t, bim_t, dabr, dabi, dbbr, dbbi):
    def body(are_ref, aim_ref, ldt_ref, bre_ref, bim_ref, g0, g1, g2, g3, o0, o1, o2, o3, o4):
        prim = (are_ref[...], aim_ref[...], ldt_ref[...], bre_ref[...], bim_ref[...])
        _, vjp = jax.vjp(_disc, *prim)
        d_are, d_aim, d_ldt, d_bre, d_bim = vjp((g0[...], g1[...], g2[...], g3[...]))
        o0[...] = d_are
        o1[...] = d_aim
        o2[...] = d_ldt
        o3[...] = d_bre
        o4[...] = d_bim

    v1, vh = SDS((1, GP), F32), SDS((NH, GP), F32)
    return pl.pallas_call(body, name="ssm_disc_bwd", out_shape=[v1, v1, v1, vh, vh], compiler_params=_cp())(
        are, aim, ldt, bre_t, bim_t, dabr, dabi, dbbr, dbbi)


def _scan_chunk(buf_ref, nt, bsz, ar, ai, init_r, init_i, reverse):
    nsub = SUB // bsz
    row = lax.broadcasted_iota(jnp.int32, (SUB, GP), 0)
    shift = ((SUB - bsz) if reverse else bsz) % SUB
    order = list(range(nsub - 1, -1, -1)) if reverse else list(range(nsub))

    def step(j, carry):
        pr, pi = carry
        jj = (nt - 1 - j) if reverse else j
        off = pl.multiple_of(jj * SUB, SUB)
        br = buf_ref[pl.ds(off, SUB), 0:GP]
        bi = buf_ref[pl.ds(off, SUB), GP:2 * GP]
        nr, ni = pr, pi
        for s in order:
            sr, si = nr, ni
            if shift:
                sr = pltpu.roll(sr, shift, 0)
                si = pltpu.roll(si, shift, 0)
            cr = ar * sr - ai * si + br
            ci = ar * si + ai * sr + bi
            if nsub == 1:
                nr, ni = cr, ci
            else:
                m = (row >= s * bsz) & (row < (s + 1) * bsz)
                nr = jnp.where(m, cr, nr)
                ni = jnp.where(m, ci, ni)
        buf_ref[pl.ds(off, SUB), 0:GP] = nr
        buf_ref[pl.ds(off, SUB), GP:2 * GP] = ni
        return nr, ni

    return lax.fori_loop(0, nt, step, (init_r, init_i))


def _ssm_chunk_rows(total_rows, bsz):
    return min(total_rows, 64 * bsz)


def _interleave(src_ref, tmp_ref, bsz, steps):
    nl = tmp_ref.shape[0]
    for b in range(bsz):
        for j in range(nl):
            tmp_ref.at[j][pl.ds(b, steps, stride=bsz), :] = src_ref[b, :, j * LANE:(j + 1) * LANE]
    return jnp.concatenate([tmp_ref[j] for j in range(nl)], axis=1)


def _deinterleave(val, tmp_ref, dst_ref, bsz, steps, skip=None):
    nl = tmp_ref.shape[0]
    for j in range(nl):
        tmp_ref[j] = val[:, j * LANE:(j + 1) * LANE]
    for b in range(bsz):
        for j in range(nl):
            lanes = slice(j * LANE, (j + 1) * LANE)
            v = tmp_ref.at[j][pl.ds(b, steps, stride=bsz), :]
            if skip is not None:
                v = v + skip[0][b, :, lanes] * skip[1][:, lanes]
            dst_ref[b, :, lanes] = v.astype(dst_ref.dtype)


SSM_UB = SW // SSM_SUPER
SSM_SB = GP // SSM_SUPER


def _sb_cols(s, half):
    return slice(half * GP + s * SSM_SB, half * GP + (s + 1) * SSM_SB)


def _ssm_in(v16, w_ref, x_ref):
    for s in range(SSM_SUPER):
        vs = v16[:, s * SSM_UB:(s + 1) * SSM_UB]
        for half in range(2):
            x_ref[:, _sb_cols(s, half)] = _dot(vs, w_ref[s * SSM_UB:(s + 1) * SSM_UB, half * SSM_SB:(half + 1) * SSM_SB])


def _ssm_out(x16, w_ref):
    outs = []
    for s in range(SSM_SUPER):
        rows = slice(s * SSM_UB, (s + 1) * SSM_UB)
        outs.append(_dot_nt(x16[:, _sb_cols(s, 0)], w_ref[rows, 0:SSM_SB])
                    + _dot_nt(x16[:, _sb_cols(s, 1)], w_ref[rows, SSM_SB:2 * SSM_SB]))
    return jnp.concatenate(outs, axis=1)


def _ssm_fwd(u, wb, wct, ar8, ai8, name):
    bsz, seq, _ = u.shape
    rt = seq * bsz
    r = _ssm_chunk_rows(rt, bsz)
    nt = r // SUB
    steps = r // bsz

    def body(u_ref, wb_hbm, wct_hbm, ar_ref, ai_ref, x_ref, y_ref, wb_ref, wct_ref, st_ref, tmp_ref):
        @pl.when(pl.program_id(0) == 0)
        def _():
            pltpu.sync_copy(wb_hbm, wb_ref)
            pltpu.sync_copy(wct_hbm, wct_ref)
            st_ref[...] = jnp.zeros_like(st_ref)

        _ssm_in(_interleave(u_ref, tmp_ref, bsz, steps).astype(BF16), wb_ref, x_ref)
        fr, fi = _scan_chunk(x_ref, nt, bsz, ar_ref[...], ai_ref[...], st_ref[:, 0:GP], st_ref[:, GP:2 * GP], False)
        st_ref[:, 0:GP] = fr
        st_ref[:, GP:2 * GP] = fi
        _deinterleave(_ssm_out(x_ref[...].astype(BF16), wct_ref), tmp_ref, y_ref, bsz, steps)

    anyspec = pl.BlockSpec(memory_space=pl.ANY)
    seqs = pl.BlockSpec((bsz, steps, SW), lambda i: (0, i, 0))
    return pl.pallas_call(
        body, name=name, grid=(rt // r,),
        in_specs=[seqs, anyspec, anyspec, _full2((SUB, GP)), _full2((SUB, GP))],
        out_specs=[pl.BlockSpec((r, 2 * GP), lambda i: (i, 0)), seqs],
        out_shape=[SDS((rt, 2 * GP), F32), SDS((bsz, seq, SW), F32)],
        scratch_shapes=[pltpu.VMEM(wb.shape, BF16), pltpu.VMEM(wct.shape, BF16), pltpu.VMEM((SUB, 2 * GP), F32),
                        pltpu.VMEM((SW // LANE, r, LANE), F32)],
        compiler_params=_cp("arbitrary"),
    )(u, wb, wct, ar8, ai8)


def _ssm_bwd(dy, u, xs, wb, wct, ar8, ai8, dskip, name):
    bsz, seq, _ = u.shape
    rt = seq * bsz
    r = _ssm_chunk_rows(rt, bsz)
    nt = r // SUB
    nc = rt // r
    steps = r // bsz
    ub = SW // SSM_SUPER
    sb = GP // SSM_SUPER

    def body(dys_ref, us_ref, x_ref, xh_ref, wb_hbm, wct_hbm, ar_ref, ai_ref, d_ref,
             dus_ref, dwb_hbm, dwct_hbm, dar_ref, dai_ref, wb_ref, wct_ref, g_ref, st_ref, awb_ref, awct_ref,
             tmp_ref):
        i = pl.program_id(0)
        dyb = _interleave(dys_ref, tmp_ref, bsz, steps).astype(BF16)
        ub16 = _interleave(us_ref, tmp_ref, bsz, steps).astype(BF16)

        @pl.when(i == 0)
        def _():
            pltpu.sync_copy(wb_hbm, wb_ref)
            pltpu.sync_copy(wct_hbm, wct_ref)
            st_ref[...] = jnp.zeros_like(st_ref)
            awb_ref[...] = jnp.zeros_like(awb_ref)
            awct_ref[...] = jnp.zeros_like(awct_ref)
            dar_ref[...] = jnp.zeros_like(dar_ref)
            dai_ref[...] = jnp.zeros_like(dai_ref)

        _ssm_in(dyb, wct_ref, g_ref)
        ar = ar_ref[...]
        ai = ai_ref[...]
        fr, fi = _scan_chunk(g_ref, nt, bsz, ar, -ai, st_ref[:, 0:GP], st_ref[:, GP:2 * GP], True)
        st_ref[:, 0:GP] = fr
        st_ref[:, GP:2 * GP] = fi

        gb = g_ref[...].astype(BF16)
        _deinterleave(_ssm_out(gb, wb_ref), tmp_ref, dus_ref, bsz, steps, skip=(dys_ref, d_ref))
        xb16 = x_ref[...].astype(BF16)
        for s in range(SSM_SUPER):
            us = ub16[:, s * ub:(s + 1) * ub]
            ds = dyb[:, s * ub:(s + 1) * ub]
            for half in range(2):
                cols = slice(half * GP + s * sb, half * GP + (s + 1) * sb)
                ocols = slice(half * sb, (half + 1) * sb)
                awb_ref[s * ub:(s + 1) * ub, ocols] += _dot_tn(us, gb[:, cols])
                awct_ref[s * ub:(s + 1) * ub, ocols] += _dot_tn(ds, xb16[:, cols])

        gr = g_ref[:, 0:GP]
        gi = g_ref[:, GP:2 * GP]
        xsr = pltpu.roll(x_ref[:, 0:GP], bsz, 0)
        xsi = pltpu.roll(x_ref[:, GP:2 * GP], bsz, 0)
        inner = lax.broadcasted_iota(jnp.int32, (r, 1), 0) >= bsz
        t_r = jnp.where(inner, gr * xsr + gi * xsi, 0.0)
        t_i = jnp.where(inner, gi * xsr - gr * xsi, 0.0)
        acc_r = jnp.sum(t_r.reshape(nt, SUB, GP), axis=0)
        acc_i = jnp.sum(t_i.reshape(nt, SUB, GP), axis=0)
        hr = xh_ref[:, 0:GP]
        hi = xh_ref[:, GP:2 * GP]
        if bsz % SUB:
            hr = pltpu.roll(hr, bsz, 0)
            hi = pltpu.roll(hi, bsz, 0)
        edge = (lax.broadcasted_iota(jnp.int32, (SUB, 1), 0) < bsz) & (i < nc - 1)
        g0r = g_ref[0:SUB, 0:GP]
        g0i = g_ref[0:SUB, GP:2 * GP]
        dar_ref[...] += acc_r + jnp.where(edge, g0r * hr + g0i * hi, 0.0)
        dai_ref[...] += acc_i + jnp.where(edge, g0i * hr - g0r * hi, 0.0)

        @pl.when(i == nc - 1)
        def _():
            pltpu.sync_copy(awb_ref, dwb_hbm)
            pltpu.sync_copy(awct_ref, dwct_hbm)

    anyspec = pl.BlockSpec(memory_space=pl.ANY)
    rev = lambda i: (nc - 1 - i, 0)
    seqs = pl.BlockSpec((bsz, steps, SW), lambda i: (0, nc - 1 - i, 0))
    wshape = (SW, 2 * sb)
    return pl.pallas_call(
        body, name=name, grid=(nc,),
        in_specs=[seqs, seqs, pl.BlockSpec((r, 2 * GP), rev),
                  pl.BlockSpec((SUB, 2 * GP), lambda i: (jnp.maximum((nc - 1 - i) * nt - 1, 0), 0)),
                  anyspec, anyspec, _full2((SUB, GP)), _full2((SUB, GP)), _full2((1, SW))],
        out_specs=[seqs, anyspec, anyspec, _full2((SUB, GP)), _full2((SUB, GP))],
        out_shape=[SDS((bsz, seq, SW), BF16), SDS(wshape, F32), SDS(wshape, F32), SDS((SUB, GP), F32),
                   SDS((SUB, GP), F32)],
        scratch_shapes=[pltpu.VMEM(wb.shape, BF16), pltpu.VMEM(wct.shape, BF16),
                        pltpu.VMEM((r, 2 * GP), F32), pltpu.VMEM((SUB, 2 * GP), F32),
                        pltpu.VMEM(wshape, F32), pltpu.VMEM(wshape, F32),
                        pltpu.VMEM((SW // LANE, r, LANE), F32)],
        compiler_params=_cp("arbitrary"),
    )(dy, u, xs, xs, wb, wct, ar8, ai8, dskip)


GELU_C = math.sqrt(2.0 / math.pi)


def _gelu(x):
    return 0.5 * x * (1.0 + jnp.tanh(GELU_C * (x + 0.044715 * x * x * x)))


def _gelu_grad(x):
    th = jnp.tanh(GELU_C * (x + 0.044715 * x * x * x))
    return 0.5 * (1.0 + th) + 0.5 * x * (1.0 - th * th) * GELU_C * (1.0 + 3.0 * 0.044715 * x * x)


def _ssm_post(ys, u, dskip, wglu, wso, name):
    bsz, seq, _ = ys.shape
    tm = _pick_tile(seq, (512, 256, 128))

    def body(ys_ref, u_ref, d_ref, wg_ref, wo_ref, s0_ref, z_ref, s1_ref, s2_ref, yb_ref):
        s0 = ys_ref[0] + d_ref[...] * u_ref[0]
        s1 = _gelu(s0)
        s1b = s1.astype(BF16)
        z = _dot(s1b, wg_ref[...])
        s2b = (s1 * _sigmoid(z)).astype(BF16)
        s0_ref[0] = s0
        z_ref[0] = z
        s1_ref[0] = s1b
        s2_ref[0] = s2b
        yb_ref[0] = _dot(s2b, wo_ref[...]).astype(BF16)

    return pl.pallas_call(
        body, name=name, grid=(bsz, seq // tm),
        in_specs=[_row(tm, SW), _row(tm, SW), _full2((1, SW)), _full2((SW, SW)), _full2((SW, D))],
        out_specs=[_row(tm, SW), _row(tm, SW), _row(tm, SW), _row(tm, SW), _row(tm, D)],
        out_shape=[SDS((bsz, seq, SW), F32), SDS((bsz, seq, SW), F32), SDS((bsz, seq, SW), BF16),
                   SDS((bsz, seq, SW), BF16), SDS((bsz, seq, D), BF16)],
        compiler_params=_cp("parallel", "parallel"),
    )(ys, u, dskip, wglu, wso)


def _ssm_post_bwd(dyb, s0, z, u, dskip, wglu, wso, name):
    bsz, seq, _ = s0.shape
    tm = _pick_tile(seq, (512, 256, 128))

    def body(dyb_ref, s0_ref, z_ref, u_ref, wg_ref, wo_ref, ds0_ref, dz_ref, dd_ref):
        b = pl.program_id(0)
        i = pl.program_id(1)
        ds2 = _dot_nt(dyb_ref[0], wo_ref[...])
        s0 = s0_ref[0]
        s1 = _gelu(s0)
        sg = _sigmoid(z_ref[0])
        dz = ds2 * s1 * sg * (1.0 - sg)
        dzb = dz.astype(BF16)
        ds1 = ds2 * sg + _dot_nt(dzb, wg_ref[...])
        ds0 = ds1 * _gelu_grad(s0)
        ds0_ref[0] = ds0
        dz_ref[0] = dzb
        part = jnp.sum(ds0 * u_ref[0], axis=0, keepdims=True)

        @pl.when((i == 0) & (b == 0))
        def _():
            dd_ref[...] = part

        @pl.when((i > 0) | (b > 0))
        def _():
            dd_ref[...] += part

    del dskip
    return pl.pallas_call(
        body, name=name, grid=(bsz, seq // tm),
        in_specs=[_row(tm, D), _row(tm, SW), _row(tm, SW), _row(tm, SW), _full2((SW, SW)), _full2((SW, D))],
        out_specs=[_row(tm, SW), _row(tm, SW), _full2((1, SW))],
        out_shape=[SDS((bsz, seq, SW), F32), SDS((bsz, seq, SW), BF16), SDS((1, SW), F32)],
        compiler_params=_cp("arbitrary", "arbitrary"),
    )(dyb, s0, z, u, wglu, wso)


def _merge_out(ya, yb, p3, wout, x1, gt, name):
    bsz, seq, _ = ya.shape
    tm = _pick_tile(seq, (512, 256, 128))

    def body(ya_ref, yb_ref, ga_ref, gbb_ref, w_ref, x_ref, gt_ref, mg_ref, mix_ref, xo_ref):
        merged = (_sigmoid(ga_ref[0].astype(F32)) * ya_ref[0].astype(F32)
                  + _sigmoid(gbb_ref[0].astype(F32)) * yb_ref[0].astype(F32)).astype(BF16)
        mix = _dot(merged, w_ref[...])
        mg_ref[0] = merged
        mix_ref[0] = mix.astype(BF16)
        xo_ref[0] = x_ref[0] + gt_ref[0] * mix

    return pl.pallas_call(
        body, name=name, grid=(bsz, seq // tm),
        in_specs=[_row(tm, D), _row(tm, D), _row(tm, D, 0), _row(tm, D, 1), _full2((D, D)), _row(tm, D), _seqvec(D)],
        out_specs=[_row(tm, D), _row(tm, D), _row(tm, D)],
        out_shape=[SDS((bsz, seq, D), BF16), SDS((bsz, seq, D), BF16), SDS((bsz, seq, D), F32)],
        compiler_params=_cp("parallel", "parallel"),
    )(ya, yb, p3, p3, wout, x1, gt)


def _merge_bwd(dx2, gt, mix, ya, yb, p3, wout, name):
    bsz, seq, _ = ya.shape
    tm = _pick_tile(seq, (512, 256, 128))

    def body(dx_ref, gt_ref, mix_ref, ya_ref, yb_ref, ga_ref, gbb_ref, w_ref, dmix_ref, dya_ref, dyb_ref, dp_ref, dgt_ref):
        i = pl.program_id(1)
        dx = dx_ref[0]
        dmix = (gt_ref[0] * dx).astype(BF16)
        dmix_ref[0] = dmix
        part = jnp.sum(dx * mix_ref[0].astype(F32), axis=0, keepdims=True)

        @pl.when(i == 0)
        def _():
            dgt_ref[0] = part

        @pl.when(i > 0)
        def _():
            dgt_ref[0] += part

        dmg = _dot_nt(dmix, w_ref[...])
        sa = _sigmoid(ga_ref[0].astype(F32))
        sb = _sigmoid(gbb_ref[0].astype(F32))
        dya_ref[0] = (dmg * sa).astype(BF16)
        dyb_ref[0] = (dmg * sb).astype(BF16)
        dp_ref[0, :, 0:D] = (dmg * ya_ref[0].astype(F32) * sa * (1.0 - sa)).astype(BF16)
        dp_ref[0, :, D:2 * D] = (dmg * yb_ref[0].astype(F32) * sb * (1.0 - sb)).astype(BF16)

    bshape = SDS((bsz, seq, D), BF16)
    return pl.pallas_call(
        body, name=name, grid=(bsz, seq // tm),
        in_specs=[_row(tm, D), _seqvec(D), _row(tm, D), _row(tm, D), _row(tm, D), _row(tm, D, 0), _row(tm, D, 1),
                  _full2((D, D))],
        out_specs=[_row(tm, D), _row(tm, D), _row(tm, D), _row(tm, 2 * D), _seqvec(D)],
        out_shape=[bshape, bshape, bshape, SDS((bsz, seq, 2 * D), BF16), SDS((bsz, 1, D), F32)],
        compiler_params=_cp("arbitrary", "arbitrary"),
    )(dx2, gt, mix, ya, yb, p3, p3, wout)


def _final_loss(x3, gfin, target, name):
    bsz, seq, dm = x3.shape
    tm = _pick_tile(seq, (512, 256, 128))

    def body(x_ref, g_ref, t_ref, dx_ref, loss_ref, dg_ref):
        b = pl.program_id(0)
        i = pl.program_id(1)
        xf = x_ref[0]
        gv = g_ref[...]
        r = lax.rsqrt(jnp.mean(xf * xf, axis=-1, keepdims=True) + EPS)
        xhat = xf * r
        e = xhat * gv - t_ref[0]
        dy = e * (1.0 / dm)
        dxh = dy * gv
        dx_ref[0] = r * (dxh - xhat * jnp.mean(dxh * xhat, axis=-1, keepdims=True))
        p_l = jnp.sum(e * e, axis=0, keepdims=True) * (0.5 / dm)
        p_g = jnp.sum(dy * xhat, axis=0, keepdims=True)

        @pl.when((i == 0) & (b == 0))
        def _():
            loss_ref[...] = p_l
            dg_ref[...] = p_g

        @pl.when((i > 0) | (b > 0))
        def _():
            loss_ref[...] += p_l
            dg_ref[...] += p_g

    return pl.pallas_call(
        body, name=name, grid=(bsz, seq // tm),
        in_specs=[_row(tm, dm), _full2((1, dm)), _row(tm, dm)],
        out_specs=[_row(tm, dm), _full2((1, dm)), _full2((1, dm))],
        out_shape=[SDS((bsz, seq, dm), F32), SDS((1, dm), F32), SDS((1, dm), F32)],
        compiler_params=_cp("arbitrary", "arbitrary"),
    )(x3, gfin, target)


def _ada_fwd(c_all, w_shard, b_shard):
    nb = c_all.shape[0]
    n = w_shard.shape[2]

    def body(c_ref, w_ref, b_ref, o_ref):
        cv = c_ref[...]
        cond = (cv * _sigmoid(cv)).astype(BF16)
        o_ref[...] = _dot(cond, w_ref[0].astype(BF16)) + b_ref[...]

    return pl.pallas_call(body, name="ada_fwd", out_shape=SDS((nb, n), F32), compiler_params=_cp())(
        c_all, w_shard, b_shard)


def _ada_bwd(c_all, dmod_shard, dmod_all):
    n = dmod_shard.shape[1]

    def body(c_ref, ds_ref, da_ref, gw_ref, gb_ref):
        cv = c_ref[...]
        cond = (cv * _sigmoid(cv)).astype(BF16)
        gw_ref[...] = _dot_tn(cond, ds_ref[...].astype(BF16))
        gb_ref[...] = jnp.sum(da_ref[...], axis=0, keepdims=True)

    return pl.pallas_call(
        body, name="ada_bwd", out_shape=[SDS((D, n), F32), SDS((1, dmod_all.shape[1]), F32)], compiler_params=_cp(),
    )(c_all, dmod_shard, dmod_all)


def _adamw_math(w, g, m, v):
    m = B1 * m + (1.0 - B1) * g
    v = B2 * v + (1.0 - B2) * (g * g)
    delta = -LR * ((m / BC1) / (jnp.sqrt(v / BC2) + AEPS) + WD * w)
    return delta, m, v


def _adamw_big(w, m, v, recv_own, recv_sib, name):
    _, rows, cols = w.shape
    tr = _pick_tile(rows, tuple(t for t in (512, 256, 128, 64, 32, 16, 8) if t * cols <= 96 * 1024))

    def body(w_ref, m_ref, v_ref, a_ref, b_ref, g_ref, d_ref, mo_ref, vo_ref):
        def chip_sum(r):
            acc = r[0].astype(F32)
            for k in range(1, N_CHIPS):
                acc = acc + r[k].astype(F32)
            return acc

        g = chip_sum(a_ref) + chip_sum(b_ref)
        delta, mn, vn = _adamw_math(w_ref[0], g, m_ref[0], v_ref[0])
        g_ref[0] = g
        d_ref[0] = delta
        mo_ref[0] = mn
        vo_ref[0] = vn

    own = pl.BlockSpec((1, tr, cols), lambda i: (0, i, 0))
    rspec = pl.BlockSpec((N_CHIPS, tr, cols), lambda i: (0, i, 0))
    shp = SDS(w.shape, F32)
    return pl.pallas_call(
        body, name=name, grid=(rows // tr,),
        in_specs=[own, own, own, rspec, rspec], out_specs=[own, own, own, own], out_shape=[shp, shp, shp, shp],
        compiler_params=_cp("parallel"),
    )(w, m, v, recv_own, recv_sib)


def _adamw_plain(w, m, v, g, name):
    def body(w_ref, m_ref, v_ref, g_ref, d_ref, mo_ref, vo_ref):
        delta, mn, vn = _adamw_math(w_ref[...], g_ref[...], m_ref[...], v_ref[...])
        d_ref[...] = delta
        mo_ref[...] = mn
        vo_ref[...] = vn

    shp = SDS(w.shape, F32)
    return pl.pallas_call(body, name=name, out_shape=[shp, shp, shp], compiler_params=_cp())(w, m, v, g)


def _adamw_rows(w, m, v, g, name):
    _, rows, cols = w.shape
    tr = _pick_tile(rows, (128, 64, 32, 16, 8))

    def body(w_ref, m_ref, v_ref, g_ref, d_ref, mo_ref, vo_ref):
        delta, mn, vn = _adamw_math(w_ref[0], g_ref[...], m_ref[0], v_ref[0])
        d_ref[0] = delta
        mo_ref[0] = mn
        vo_ref[0] = vn

    spec = pl.BlockSpec((1, tr, cols), lambda i: (0, i, 0))
    shp = SDS(w.shape, F32)
    return pl.pallas_call(
        body, name=name, grid=(rows // tr,), in_specs=[spec] * 3 + [pl.BlockSpec((tr, cols), lambda i: (i, 0))],
        out_specs=[spec] * 3, out_shape=[shp] * 3, compiler_params=_cp("parallel"),
    )(w, m, v, g)


def _sum_slabs(r, name):
    n, rows, cols = r.shape
    tr = _pick_tile(rows, (256, 128, 64))

    def body(r_ref, o_ref):
        acc = r_ref[0].astype(F32)
        for j in range(1, n):
            acc = acc + r_ref[j].astype(F32)
        o_ref[...] = acc

    return pl.pallas_call(
        body, name=name, grid=(rows // tr,),
        in_specs=[pl.BlockSpec((n, tr, cols), lambda i: (0, i, 0))],
        out_specs=pl.BlockSpec((tr, cols), lambda i: (i, 0)), out_shape=SDS((rows, cols), F32),
        compiler_params=_cp("parallel"),
    )(r)


def _place():
    return lax.axis_index("x"), lax.axis_index("y"), lax.axis_index("c")


def _all_gather8(blk, name):
    m_per, n = blk.shape

    def body(x_ref, out_ref, send_sems, recv_sems, local_sem):
        x, y, c = _place()
        me, sibling = (x, y, c), (x, y, 1 - c)
        chips = [(1 - x, y), (x, 1 - y), (1 - x, 1 - y)]

        def rows(px, py, pc):
            return out_ref.at[pl.ds((4 * px + 2 * py + pc) * m_per, m_per), :]

        def copy(k, block, to, src=None):
            return pltpu.make_async_remote_copy(
                src_ref=rows(*block) if src is None else src, dst_ref=rows(*block),
                send_sem=send_sems.at[k], recv_sem=recv_sems.at[k], device_id=to, device_id_type=MESH)

        mine = pltpu.make_async_copy(x_ref, rows(*me), local_sem)
        mine.start()
        first = [copy(0, me, sibling, src=x_ref)]
        first += [copy(1 + j, me, (*chip, c), src=x_ref) for j, chip in enumerate(chips)]
        for cp in first:
            cp.start()
        passed = [copy(4 + j, (*chip, c), sibling) for j, chip in enumerate(chips)]
        for j, chip in enumerate(chips):
            copy(1 + j, (*chip, c), me).wait_recv()
            passed[j].start()
        copy(0, sibling, me).wait_recv()
        for j, chip in enumerate(chips):
            copy(4 + j, (*chip, 1 - c), me).wait_recv()
        for cp in first + passed:
            cp.wait_send()
        mine.wait()

    return pl.pallas_call(
        body, name=name, out_shape=SDS((N_DEV * m_per, n), blk.dtype),
        in_specs=[pl.BlockSpec(memory_space=pltpu.VMEM)], out_specs=pl.BlockSpec(memory_space=pltpu.VMEM),
        scratch_shapes=[pltpu.SemaphoreType.DMA((7,)), pltpu.SemaphoreType.DMA((7,)), pltpu.SemaphoreType.DMA],
        compiler_params=pltpu.CompilerParams(vmem_limit_bytes=VMEM_LIMIT),
    )(blk)


def _chip_peers(x, y):
    return [(1 - x, y), (x, 1 - y), (1 - x, 1 - y)]


def _peer_copies(src_refs, land_refs, send_sems, recv_sems, scatter, landed):
    x, y, c = _place()
    cps = []
    for a, (src_ref, land_ref) in enumerate(zip(src_refs, land_refs)):
        for j, (px, py) in enumerate(_chip_peers(x, y)):
            src = src_ref.at[2 * px + py] if scatter else src_ref
            dst = land_ref.at[2 * px + py] if landed else land_ref.at[2 * x + y]
            cps.append(pltpu.make_async_remote_copy(
                src_ref=src, dst_ref=dst, send_sem=send_sems.at[3 * a + j], recv_sem=recv_sems.at[3 * a + j],
                device_id=(px, py, c), device_id_type=MESH))
    return cps


def _exchange_chips(srcs, scatter, name):
    n = len(srcs)

    def body(*refs):
        src_refs, land_refs = refs[:n], refs[n:2 * n]
        send_sems, recv_sems, local_sems = refs[2 * n:]
        x, y, _ = _place()
        me = 2 * x + y
        mine = [pltpu.make_async_copy(s.at[me] if scatter else s, l.at[me], local_sems.at[a])
                for a, (s, l) in enumerate(zip(src_refs, land_refs))]
        for cp in mine:
            cp.start()
        out = _peer_copies(src_refs, land_refs, send_sems, recv_sems, scatter, False)
        for cp in out:
            cp.start()
        for cp in _peer_copies(src_refs, land_refs, send_sems, recv_sems, scatter, True):
            cp.wait_recv()
        for cp in out:
            cp.wait_send()
        for cp in mine:
            cp.wait()

    anyspec = pl.BlockSpec(memory_space=pl.ANY)
    shapes = [SDS(s.shape if scatter else (N_CHIPS,) + s.shape, s.dtype) for s in srcs]
    return pl.pallas_call(
        body, name=name, out_shape=shapes, in_specs=[anyspec] * n, out_specs=[anyspec] * n,
        scratch_shapes=[pltpu.SemaphoreType.DMA((3 * n,)), pltpu.SemaphoreType.DMA((3 * n,)),
                        pltpu.SemaphoreType.DMA((n,))],
        compiler_params=pltpu.CompilerParams(vmem_limit_bytes=VMEM_LIMIT),
    )(*srcs)


_HBM = pl.BlockSpec(memory_space=pltpu.HBM)
_SEM = pl.BlockSpec(memory_space=pltpu.SEMAPHORE)
_EFFECT = pltpu.SideEffectType.DATAFLOW_SIDE_EFFECTING


def _exchange_begin(srcs, lands, scatter, name):
    n = len(srcs)

    def body(*refs):
        src_refs, land_refs = refs[:n], refs[n:2 * n]
        send_sems, recv_sems = refs[2 * n:2 * n + 2]
        token = refs[-1]
        for cp in _peer_copies(src_refs, land_refs, send_sems, recv_sems, scatter, False):
            cp.start()
        token[...] = jnp.zeros_like(token)

    res = pl.pallas_call(
        body, name=name,
        out_shape=(pltpu.SemaphoreType.DMA((3 * n,)), pltpu.SemaphoreType.DMA((3 * n,)),
                   *[pltpu.HBM(s.shape, s.dtype) for s in srcs], *[pltpu.HBM(l.shape, l.dtype) for l in lands],
                   SDS((SUB, LANE), F32)),
        in_specs=[_HBM] * (2 * n), out_specs=(_SEM, _SEM, *[_HBM] * (2 * n), pl.BlockSpec(memory_space=pltpu.VMEM)),
        input_output_aliases={i: 2 + i for i in range(2 * n)},
        compiler_params=pltpu.CompilerParams(has_side_effects=_EFFECT),
    )(*[pltpu.with_memory_space_constraint(a, pltpu.HBM) for a in (*srcs, *lands)])
    return res[0], res[1], res[2:2 + n], res[2 + n:2 + 2 * n], res[-1]


def _exchange_end(handle, after, scatter, name):
    send_sems, recv_sems, srcs, lands, _ = handle
    n = len(srcs)

    def body(*refs):
        src_refs, land_refs = refs[:n], refs[n:2 * n]
        send_sems, recv_sems = refs[2 * n:2 * n + 2]
        for cp in _peer_copies(src_refs, land_refs, send_sems, recv_sems, scatter, True):
            cp.wait_send()
            cp.wait_recv()

    res = pl.pallas_call(
        body, name=name,
        out_shape=tuple(pltpu.HBM(a.shape, a.dtype) for a in (*srcs, *lands)),
        in_specs=[_HBM] * (2 * n) + [_SEM, _SEM, pl.BlockSpec(memory_space=pl.ANY)], out_specs=tuple([_HBM] * (2 * n)),
        input_output_aliases={i: i for i in range(2 * n)},
        compiler_params=pltpu.CompilerParams(has_side_effects=_EFFECT),
    )(*srcs, *lands, send_sems, recv_sems, after)
    return list(res[n:])


def _own_slab(stack4, chip):
    idx = lax.broadcasted_iota(jnp.int32, (N_CHIPS,) + (1,) * (stack4.ndim - 1), 0)
    return jnp.where(idx == chip, stack4, jnp.zeros((), stack4.dtype))


def _swap_sibling(vs, name):
    n = len(vs)

    def body(*refs):
        in_refs, out_refs = refs[:n], refs[n:2 * n]
        send_sems, recv_sems = refs[2 * n:]
        x, y, c = _place()
        cps = [pltpu.make_async_remote_copy(
            src_ref=i, dst_ref=o, send_sem=send_sems.at[a], recv_sem=recv_sems.at[a],
            device_id=(x, y, 1 - c), device_id_type=MESH) for a, (i, o) in enumerate(zip(in_refs, out_refs))]
        for cp in cps:
            cp.start()
        for cp in cps:
            cp.wait()

    anyspec = pl.BlockSpec(memory_space=pl.ANY)
    return pl.pallas_call(
        body, name=name, out_shape=[SDS(v.shape, v.dtype) for v in vs], in_specs=[anyspec] * n, out_specs=[anyspec] * n,
        scratch_shapes=[pltpu.SemaphoreType.DMA((n,)), pltpu.SemaphoreType.DMA((n,))],
        compiler_params=pltpu.CompilerParams(vmem_limit_bytes=VMEM_LIMIT),
    )(*vs)


def _select(stacked, idx):
    out = stacked[0]
    for j in range(1, stacked.shape[0]):
        out = jnp.where(idx == j, stacked[j], out)
    return out


BIG = (
    ("w1_a", DFF // 4, D, False), ("w3_a", DFF // 4, D, False), ("w2_a", DFF // 4, D, False),
    ("w_in", D, 5632 // 4, True), ("w_conv_out", CW // 4, D, False), ("w_glu", SW // 4, SW, False),
    ("w_ssm_out", SW, D // 4, True), ("w_out", D // 4, D, False),
    ("w1_b", DFF // 4, D, False), ("w3_b", DFF // 4, D, False), ("w2_b", DFF // 4, D, False),
)
TRANSPOSED = frozenset(("w1_a", "w3_a", "w1_b", "w3_b"))
PACK_COLS = 1024


def _view(a, name):
    return jnp.transpose(a, (0, 2, 1)) if name in TRANSPOSED else a


def _full_from_stacked(st, split_cols):
    _, rows, cols = st.shape
    if split_cols:
        return st.transpose(1, 0, 2).reshape(rows, N_CHIPS * cols)
    return st.reshape(N_CHIPS * rows, cols)


def _stacked_from_full(full, rows, cols, split_cols):
    if split_cols:
        return full.reshape(rows, N_CHIPS, cols).transpose(1, 0, 2)
    return full.reshape(N_CHIPS, rows, cols)


def _blockdiag(t):
    r = lax.broadcasted_iota(jnp.int32, (SW, GP), 0) // NH
    cidx = lax.broadcasted_iota(jnp.int32, (SW, GP), 1) // NP
    dense = jnp.where(r == cidx, jnp.tile(t, (NG, 1)), 0.0)
    ub, sb = SW // SSM_SUPER, GP // SSM_SUPER
    return jnp.concatenate([dense[s * ub:(s + 1) * ub, s * sb:(s + 1) * sb] for s in range(SSM_SUPER)], axis=0)


def _blockdiag_extract(acc):
    gs = NG // SSM_SUPER
    a = acc.reshape(NG, NH, gs, NP)
    sel = (lax.broadcasted_iota(jnp.int32, (NG, 1, gs, 1), 0) % gs) == lax.broadcasted_iota(jnp.int32, (NG, 1, gs, 1), 2)
    a = jnp.sum(jnp.where(sel, a, 0.0), axis=2)
    return a.transpose(1, 0, 2).reshape(NH, GP)


def _to_t(p):
    return p.transpose(2, 0, 1).reshape(NH, GP)


def _from_t(t):
    return t.reshape(NH, NG, NP).transpose(1, 2, 0)


def _c_to_t(p):
    return p.transpose(1, 0, 2).reshape(NH, GP)


def _c_from_t(t):
    return t.reshape(NH, NG, NP).transpose(1, 0, 2)


def _ffn_forward(x, g, sh, sc, gt, w1, w3, w2, tag):
    h = _norm_mod(x, g, sh, sc, f"{tag}_norm")
    if callable(w1):
        w1, w3 = w1(h)
    a, b, hid = _swiglu_up(h, w1, w3, f"{tag}_up")
    w2 = w2(hid) if callable(w2) else w2
    f, xo = _ffn_down(hid, w2, x, gt, f"{tag}_down")
    return xo, (x, h, a, b, hid, f), w2


def _ffn_backward(dxo, saved, g, sc, gt, w1, w3, w2, tag, emit=lambda key, gw: 0.0):
    x, h, a, b, hid, f = saved
    dfs, da, db, dgt = _ffn_bwd_hid(dxo, gt, f, a, b, w2, f"{tag}_bwd_hid")
    h2 = _flat(h)
    gw2 = _mm(_flat(hid), _flat(dfs), ta=True, name=f"{tag}_gw2")
    tok = emit("w2", gw2)
    gw1 = _mm(_flat(da), h2, ta=True, name=f"{tag}_gw1")
    tok = tok + emit("w1", gw1)
    gw3 = _mm(_flat(db), h2, ta=True, name=f"{tag}_gw3")
    tok = tok + emit("w3", gw3)
    dx, dsh, dsc, dg = _dh_norm_bwd([da, db], [w1, w3], x, g, sc + tok, dxo, f"{tag}_bwd_dh", transposed=True)
    return dx, (dsh, dsc, dgt, dg), (gw1, gw3, gw2)


def kernel(x, c, w_ada, b_ada, g_ffn1, w1_a, w3_a, w2_a, g_mix, w_in, conv_w, w_conv_out, a_re, a_im, b_re, b_im, c_re, c_im, log_dt, d_skip, w_glu, w_ssm_out, w_out, g_ffn2, w1_b, w3_b, w2_b, g_final, loss_target, m_w_ada, m_b_ada, m_g_ffn1, m_w1_a, m_w3_a, m_w2_a, m_g_mix, m_w_in, m_conv_w, m_w_conv_out, m_a_re, m_a_im, m_b_re, m_b_im, m_c_re, m_c_im, m_log_dt, m_d_skip, m_w_glu, m_w_ssm_out, m_w_out, m_g_ffn2, m_w1_b, m_w3_b, m_w2_b, m_g_final, v_w_ada, v_b_ada, v_g_ffn1, v_w1_a, v_w3_a, v_w2_a, v_g_mix, v_w_in, v_conv_w, v_w_conv_out, v_a_re, v_a_im, v_b_re, v_b_im, v_c_re, v_c_im, v_log_dt, v_d_skip, v_w_glu, v_w_ssm_out, v_w_out, v_g_ffn2, v_w1_b, v_w3_b, v_w2_b, v_g_final):
    args = dict(locals())
    names = ["w_ada", "b_ada", "g_ffn1", "w1_a", "w3_a", "w2_a", "g_mix", "w_in", "conv_w", "w_conv_out", "a_re",
             "a_im", "b_re", "b_im", "c_re", "c_im", "log_dt", "d_skip", "w_glu", "w_ssm_out", "w_out", "g_ffn2",
             "w1_b", "w3_b", "w2_b", "g_final"]
    bsz, seq, _ = x.shape
    mx, my, mc = _place()
    chip = 2 * mx + my
    dev = 4 * mx + 2 * my + mc

    groups = (BIG[:3], BIG[3:8], BIG[8:])
    wfull = {}

    def shards(grp):
        return [_view(args[n], n)[0] for n, _, _, _ in grp]

    def unpack_group(gathered, grp):
        for (n, _, _, split), st in zip(grp, gathered):
            wfull[n] = _full_from_stacked(st, split)

    up_grp, down_grp = groups[0][:2], groups[0][2:]

    nmod_shard = NMOD * D // N_CHIPS
    c_all = _all_gather8(c.reshape(SUB, -1), "gather_c").reshape(N_DEV * bsz, D)
    b_shard = _select(b_ada.reshape(N_CHIPS, 1, nmod_shard), chip)
    mod_shard = _ada_fwd(c_all, w_ada, b_shard)
    nb = N_DEV * bsz
    cw_pad = jnp.pad(conv_w[0], ((0, SUB - 3), (0, nmod_shard - CW // N_CHIPS)))
    mod_st = _exchange_chips([jnp.concatenate([mod_shard, cw_pad], axis=0)], False, "gather_mod")[0]
    mod_all = mod_st[:, :nb].transpose(1, 0, 2).reshape(N_DEV, bsz, NMOD * D)
    mod = _select(mod_all, dev)

    def gather_begin(raw, name):
        srcs = [a.astype(BF16) for a in raw]
        lands = [_own_slab(jnp.broadcast_to(s[None], (N_CHIPS,) + s.shape), chip) for s in srcs]
        return _exchange_begin(srcs, lands, False, name)

    up_raw, mod = lax.optimization_barrier((shards(up_grp), mod))
    up_handle = gather_begin(up_raw, "gather_w_ffn1_up_start")
    (down_raw, mix_raw, ffn2_raw), up_token = lax.optimization_barrier(
        ((shards(down_grp), shards(groups[1]), shards(groups[2])), up_handle[4][0:1, 0:1]))
    down_handle = gather_begin(down_raw, "gather_w_ffn1_down_start")
    mix_handle = gather_begin(mix_raw, "gather_w_mix_start")
    ffn2_handle = gather_begin(ffn2_raw, "gather_w_ffn2_start")
    start_tokens = up_token + down_handle[4][0:1, 0:1] + mix_handle[4][0:1, 0:1] + ffn2_handle[4][0:1, 0:1]

    sh1, sc1, gt1, sh2, sc2, gt2, sh3, sc3, gt3 = [mod[:, None, j * D:(j + 1) * D] for j in range(NMOD)]
    convw = mod_st[:, nb:nb + 3, :CW // N_CHIPS].transpose(1, 0, 2).reshape(3, CW)
    convw8 = jnp.pad(convw, ((0, SUB - 3), (0, 0)))

    are, aim = a_re.reshape(1, GP), a_im.reshape(1, GP)
    ldt = jnp.broadcast_to(log_dt.reshape(NG, 1), (NG, NP)).reshape(1, GP)
    bre_t, bim_t = _to_t(b_re[0]), _to_t(b_im[0])
    abr, abi, bbr_t, bbi_t = _ssm_disc(are, aim, ldt, bre_t, bim_t)
    wb = jnp.concatenate([_blockdiag(bbr_t), _blockdiag(bbi_t)], axis=1).astype(BF16)
    wct = jnp.concatenate([_blockdiag(_c_to_t(c_re[0])), _blockdiag(-_c_to_t(c_im[0]))], axis=1).astype(BF16)
    ar8 = jnp.broadcast_to(abr, (SUB, GP))
    ai8 = jnp.broadcast_to(abi, (SUB, GP))

    def late_w2a(hid):
        unpack_group(_exchange_end(down_handle, hid, False, "gather_w_ffn1_down_wait"), down_grp)
        return wfull["w2_a"]

    def late_w13a(h):
        unpack_group(_exchange_end(up_handle, h, False, "gather_w_ffn1_up_wait"), up_grp)
        return wfull["w1_a"], wfull["w3_a"]

    x1, ffn1_saved, _ = _ffn_forward(x, g_ffn1 + start_tokens, sh1, sc1, gt1, late_w13a, None, late_w2a, "ffn1")
    unpack_group(_exchange_end(mix_handle, x1, False, "gather_w_mix_wait"), groups[1])

    h2 = _norm_mod(x1, g_mix, sh2, sc2, "mix_norm")
    h2f = _flat(h2)
    win = wfull["w_in"]
    win1, winu, win3 = win[:, :3 * CW], win[:, 3 * CW:3 * CW + SW], win[:, 3 * CW + SW:]
    p1 = _mm(h2f, win1, out_dtype=BF16, name="mix_in1").reshape(bsz, seq, 3 * CW)
    u = _mm(h2f, winu, name="mix_inu").reshape(bsz, seq, SW)
    p3 = _mm(h2f, win3, out_dtype=BF16, name="mix_in3").reshape(bsz, seq, 2 * D)

    ya_in = _conv_fwd(p1, convw8, "conv_fwd")
    ya = _mm(_flat(ya_in), wfull["w_conv_out"], out_dtype=BF16, name="conv_out").reshape(bsz, seq, D)

    xs, ys = _ssm_fwd(u, wb, wct, ar8, ai8, "ssm_fwd")
    s0, z, s1, s2, yb = _ssm_post(ys, u, d_skip, wfull["w_glu"], wfull["w_ssm_out"], "ssm_post")

    merged, mix, x2 = _merge_out(ya, yb, p3, wfull["w_out"], x1, gt2, "merge_out")
    unpack_group(_exchange_end(ffn2_handle, x2, False, "gather_w_ffn2_wait"), groups[2])
    x3, ffn2_saved, _ = _ffn_forward(x2, g_ffn2, sh3, sc3, gt3, wfull["w1_b"], wfull["w3_b"], wfull["w2_b"], "ffn2")

    dx3, lossvec, dgfin = _final_loss(x3, g_final.reshape(1, D), loss_target, "final_loss")
    loss = lax.psum(jnp.sum(lossvec), ("x", "y", "c"))

    gfull = {}
    dx2, (dsh3, dsc3, dgt3, dg3), (gfull["w1_b"], gfull["w3_b"], gfull["w2_b"]) = _ffn_backward(
        dx3, ffn2_saved, g_ffn2, sc3, gt3, wfull["w1_b"], wfull["w3_b"], wfull["w2_b"], "ffn2")

    def stack_group(grp):
        return [_stacked_from_full(gfull[n], rows, cols, split).astype(BF16) for n, rows, cols, split in grp]

    st_ffn2 = stack_group(groups[2])
    h_ffn2 = _exchange_begin(st_ffn2, [_own_slab(s, chip) for s in st_ffn2], True, "scatter_ffn2_start")

    dmix, dya, dyb, dp3, dgt2 = _merge_bwd(dx2, gt2 + h_ffn2[4][0, 0], mix, ya, yb, p3, wfull["w_out"], "merge_bwd")
    gfull["w_out"] = _mm(_flat(merged), _flat(dmix), ta=True, name="gw_out")
    dp1, dconvw8 = _conv_bwd(dya, wfull["w_conv_out"], p1, convw8, "conv_bwd")
    gfull["w_conv_out"] = _mm(_flat(ya_in), _flat(dya), ta=True, name="gw_conv_out")
    ds0, dz, ddskip = _ssm_post_bwd(dyb, s0, z, u, d_skip, wfull["w_glu"], wfull["w_ssm_out"], "ssm_post_bwd")
    gfull["w_ssm_out"] = _mm(_flat(s2), _flat(dyb), ta=True, name="gw_ssm_out")
    gfull["w_glu"] = _mm(_flat(s1), _flat(dz), ta=True, name="gw_glu")
    du, dwb, dwct, dar8, dai8 = _ssm_bwd(ds0, u, xs, wb, wct, ar8, ai8, d_skip, "ssm_bwd")
    dx1, dsh2, dsc2, dgmix = _dh_norm_bwd([dp1, du, dp3], [win1, winu, win3], x1, g_mix, sc2, dx2, "mix_bwd_dh")
    gfull["w_in"] = jnp.concatenate([
        _mm(h2f, _flat(dp1), ta=True, name="gw_in1"), _mm(h2f, _flat(du), ta=True, name="gw_inu"),
        _mm(h2f, _flat(dp3), ta=True, name="gw_in3")], axis=1)

    st_mix = stack_group(groups[1])
    h_mix = _exchange_begin(st_mix, [_own_slab(s, chip) for s in st_mix], True, "scatter_mix_start")

    ffn1_names = {"w1": BIG[0], "w3": BIG[1], "w2": BIG[2]}
    ffn1_handles = {}

    def emit_ffn1(key, gw):
        n, rows, cols, split = ffn1_names[key]
        st = _stacked_from_full(gw, rows, cols, split).astype(BF16)
        ffn1_handles[n] = _exchange_begin([st], [_own_slab(st, chip)], True, f"scatter_ffn1_{key}_start")
        return ffn1_handles[n][4][0, 0]

    grad_x, (dsh1, dsc1, dgt1, dg1), _ = _ffn_backward(
        dx1, ffn1_saved, g_ffn1, sc1, gt1 + h_mix[4][0, 0], wfull["w1_a"], wfull["w3_a"], wfull["w2_a"], "ffn1",
        emit=emit_ffn1)

    sbw = GP // SSM_SUPER
    d_are, d_aim, d_ldt, d_bre_t, d_bim_t = _ssm_disc_bwd(
        are, aim, ldt, bre_t, bim_t, jnp.sum(dar8, axis=0, keepdims=True), jnp.sum(dai8, axis=0, keepdims=True),
        _blockdiag_extract(dwb[:, :sbw]), _blockdiag_extract(dwb[:, sbw:]))
    d_cre = _c_from_t(_blockdiag_extract(dwct[:, :sbw]))
    d_cim = -_c_from_t(_blockdiag_extract(dwct[:, sbw:]))

    small_parts = [dg1, dgmix, dg3, dgfin, dconvw8[:3], d_are, d_aim, _from_t(d_bre_t), _from_t(d_bim_t), d_cre, d_cim,
                   jnp.sum(d_ldt.reshape(NG, NP), axis=1), ddskip]
    small_sizes = [int(p.size) for p in small_parts]
    n_small = sum(small_sizes)
    n_small_pad = -(-n_small // (SUB * PACK_COLS)) * (SUB * PACK_COLS)
    dmod = jnp.concatenate([dsh1, dsc1, dgt1, dsh2, dsc2, dgt2, dsh3, dsc3, dgt3], axis=2).reshape(bsz * NMOD * D)
    flat = jnp.concatenate([p.reshape(-1) for p in small_parts] + [jnp.zeros((n_small_pad - n_small,), F32), dmod])
    allg = _all_gather8(flat.reshape(SUB, -1), "gather_small").reshape(N_DEV, -1)
    small = _sum_slabs(allg[:, :n_small_pad].reshape(N_DEV, -1, PACK_COLS), "sum_small").reshape(-1)
    sg, o = [], 0
    for p, sz in zip(small_parts, small_sizes):
        sg.append(small[o:o + sz].reshape(p.shape))
        o += sz
    (g_g1, g_gmix, g_g3, g_gfin, g_convw, g_are, g_aim, g_bre, g_bim, g_cre, g_cim, g_ldt, g_dskip) = sg

    dmod_all = allg[:, n_small_pad:].reshape(nb, NMOD * D)
    dmod_shard = _select(dmod_all.reshape(nb, N_CHIPS, nmod_shard).transpose(1, 0, 2), chip)
    g_wada, g_bada = _ada_bwd(c_all, dmod_shard, dmod_all)

    recv_ffn2 = _exchange_end(h_ffn2, g_wada, True, "scatter_ffn2_wait")
    recv_mix = _exchange_end(h_mix, g_wada, True, "scatter_mix_wait")
    recv_ffn1 = [_exchange_end(ffn1_handles[n], g_wada, True, f"scatter_ffn1_{n}_wait")[0] for n, _, _, _ in groups[0]]
    recv = [*recv_ffn1, *recv_mix, *recv_ffn2]
    recv_sib = _swap_sibling(recv, "swap_sibling")

    grads, deltas, new_m, new_v = {}, {}, {}, {}
    for (n, _, _, _), r_own, r_sib in zip(BIG, recv, recv_sib):
        res = _adamw_big(_view(args[n], n), _view(args["m_" + n], n), _view(args["v_" + n], n), r_own, r_sib,
                         f"adamw_{n}")
        grads[n], deltas[n], new_m[n], new_v[n] = [_view(r, n) for r in res]
    grads["w_ada"] = g_wada[None]
    deltas["w_ada"], new_m["w_ada"], new_v["w_ada"] = _adamw_rows(w_ada, m_w_ada, v_w_ada, g_wada, "adamw_w_ada")

    g_convw_shard = _select(g_convw.reshape(3, N_CHIPS, CW // N_CHIPS).transpose(1, 0, 2), chip)
    small_g = {"b_ada": g_bada, "g_ffn1": g_g1, "g_mix": g_gmix, "g_ffn2": g_g3, "g_final": g_gfin,
               "conv_w": g_convw_shard, "a_re": g_are, "a_im": g_aim, "b_re": g_bre, "b_im": g_bim,
               "c_re": g_cre, "c_im": g_cim, "log_dt": g_ldt, "d_skip": g_dskip}
    small_names = list(small_g)
    sizes = [int(args[n].size) for n in small_names]
    tot = sum(sizes)
    tot_pad = -(-tot // (SUB * PACK_COLS)) * (SUB * PACK_COLS)

    def pack(get):
        return jnp.concatenate([get(n).reshape(-1) for n in small_names] + [jnp.zeros((tot_pad - tot,), F32)]).reshape(
            -1, PACK_COLS)

    res = _adamw_plain(pack(lambda n: args[n]), pack(lambda n: args["m_" + n]), pack(lambda n: args["v_" + n]),
                       pack(lambda n: small_g[n]), "adamw_small")
    o = 0
    for n, sz in zip(small_names, sizes):
        shp = args[n].shape
        grads[n] = small_g[n].reshape(shp)
        deltas[n], new_m[n], new_v[n] = [r.reshape(-1)[o:o + sz].reshape(shp) for r in res]
        o += sz

    return (loss, grad_x, *[grads[n] for n in names], *[deltas[n] for n in names],
            *[new_m[n] for n in names], *[new_v[n] for n in names])
```

```python
import math

import jax
import jax.numpy as jnp
from jax import lax
from jax.experimental import pallas as pl
from jax.experimental.pallas import tpu as pltpu

F32 = jnp.float32
BF16 = jnp.bfloat16
SDS = jax.ShapeDtypeStruct
MESH = pl.DeviceIdType.MESH

D = 1024
DFF = 2816
CW = 1024
SW = 512
NG, NP, NH = 32, 64, 16
GP = NG * NP
NMOD = 9
EPS = 1e-6
N_CHIPS = 4
N_DEV = 8
SUB = 8
LANE = 128
SSM_SUPER = 4
VMEM_LIMIT = 50 * 1024 * 1024

LR, B1, B2, AEPS, WD, STEP = 0.001, 0.9, 0.999, 1e-08, 0.01, 10
BC1 = 1.0 - B1 ** STEP
BC2 = 1.0 - B2 ** STEP


def _cp(*sem):
    return pltpu.CompilerParams(dimension_semantics=sem or None, vmem_limit_bytes=VMEM_LIMIT)


def _pick_tile(n, cands):
    for t in cands:
        if t <= n and n % t == 0:
            return t
    return n


def _dot(a, b):
    return lax.dot_general(a, b, (((1,), (0,)), ((), ())), preferred_element_type=F32)


def _dot_nt(a, b):
    return lax.dot_general(a, b, (((1,), (1,)), ((), ())), preferred_element_type=F32)


def _dot_tn(a, b):
    return lax.dot_general(a, b, (((0,), (0,)), ((), ())), preferred_element_type=F32)


def _row(tm, width, col=0):
    return pl.BlockSpec((1, tm, width), lambda b, i, *_: (b, i, col))


def _seqvec(width):
    return pl.BlockSpec((1, 1, width), lambda b, *_: (b, 0, 0))


def _full2(shape):
    return pl.BlockSpec(shape, lambda *_: (0, 0))


def _sigmoid(x):
    return jax.nn.sigmoid(x)


def _mm(a, b, *, ta=False, tb=False, out_dtype=F32, name):
    if ta:
        kdim, m = a.shape
    else:
        m, kdim = a.shape
    n = b.shape[0] if tb else b.shape[1]
    tm = _pick_tile(m, (1408, 1024, 512, 256, 128))
    tn = _pick_tile(n, (1408, 1024, 512, 256, 128))
    tk = _pick_tile(kdim, (1024, 512, 256, 128))
    nk = kdim // tk

    def body(a_ref, b_ref, o_ref, acc_ref):
        k = pl.program_id(2)

        @pl.when(k == 0)
        def _():
            acc_ref[...] = jnp.zeros_like(acc_ref)

        av = a_ref[...].astype(BF16)
        bv = b_ref[...].astype(BF16)
        dn = (((0 if ta else 1,), (1 if tb else 0,)), ((), ()))
        acc_ref[...] += lax.dot_general(av, bv, dn, preferred_element_type=F32)

        @pl.when(k == nk - 1)
        def _():
            o_ref[...] = acc_ref[...].astype(out_dtype)

    a_spec = pl.BlockSpec((tk, tm), lambda i, j, k: (k, i)) if ta else pl.BlockSpec((tm, tk), lambda i, j, k: (i, k))
    b_spec = pl.BlockSpec((tn, tk), lambda i, j, k: (j, k)) if tb else pl.BlockSpec((tk, tn), lambda i, j, k: (k, j))
    return pl.pallas_call(
        body, name=name, grid=(m // tm, n // tn, nk),
        in_specs=[a_spec, b_spec],
        out_specs=pl.BlockSpec((tm, tn), lambda i, j, k: (i, j)),
        out_shape=SDS((m, n), out_dtype),
        scratch_shapes=[pltpu.VMEM((tm, tn), F32)],
        compiler_params=_cp("parallel", "parallel", "arbitrary"),
    )(a, b)


def _flat(a):
    return a.reshape(-1, a.shape[-1])


def _norm_mod(x, g, sh, sc, name):
    bsz, seq, dm = x.shape
    tm = _pick_tile(seq, (512, 256, 128))

    def body(x_ref, g_ref, sh_ref, sc_ref, o_ref):
        xf = x_ref[0]
        r = lax.rsqrt(jnp.mean(xf * xf, axis=-1, keepdims=True) + EPS)
        hn = xf * r * g_ref[...]
        o_ref[0] = (hn * (1.0 + sc_ref[0]) + sh_ref[0]).astype(BF16)

    return pl.pallas_call(
        body, name=name, grid=(bsz, seq // tm),
        in_specs=[_row(tm, dm), _full2((1, dm)), _seqvec(dm), _seqvec(dm)],
        out_specs=_row(tm, dm), out_shape=SDS((bsz, seq, dm), BF16),
        compiler_params=_cp("parallel", "parallel"),
    )(x, g, sh, sc)


def _swiglu_up(h, w1, w3, name):
    bsz, seq, dm = h.shape
    nf = w1.shape[0]
    tm = _pick_tile(seq, (512, 256, 128))
    tn = _pick_tile(nf, (1408, 512, 256, 128))

    def body(h_ref, w1_ref, w3_ref, a_ref, b_ref, hid_ref):
        hv = h_ref[0]
        a = _dot_nt(hv, w1_ref[...])
        b = _dot_nt(hv, w3_ref[...])
        sg = _sigmoid(a)
        sa = a * sg
        a_ref[0] = (b * (sg * (1.0 + a * (1.0 - sg)))).astype(BF16)
        b_ref[0] = sa.astype(BF16)
        hid_ref[0] = (sa * b).astype(BF16)

    wspec = pl.BlockSpec((tn, dm), lambda n, b, i: (n, 0))
    ospec = pl.BlockSpec((1, tm, tn), lambda n, b, i: (b, i, n))
    shp = SDS((bsz, seq, nf), BF16)
    return pl.pallas_call(
        body, name=name, grid=(nf // tn, bsz, seq // tm),
        in_specs=[pl.BlockSpec((1, tm, dm), lambda n, b, i: (b, i, 0)), wspec, wspec],
        out_specs=[ospec, ospec, ospec], out_shape=[shp, shp, shp],
        compiler_params=_cp("parallel", "parallel", "parallel"),
    )(h, w1, w3)


def _ffn_down(hid, w2, x, gt, name):
    bsz, seq, nf = hid.shape
    dm = w2.shape[1]
    tm = _pick_tile(seq, (512, 256, 128))

    def body(hid_ref, w2_ref, x_ref, gt_ref, f_ref, xo_ref):
        f = _dot(hid_ref[0], w2_ref[...])
        f_ref[0] = f.astype(BF16)
        xo_ref[0] = x_ref[0] + 0.5 * gt_ref[0] * f

    shp = SDS((bsz, seq, dm), F32)
    return pl.pallas_call(
        body, name=name, grid=(bsz, seq // tm),
        in_specs=[_row(tm, nf), _full2((nf, dm)), _row(tm, dm), _seqvec(dm)],
        out_specs=[_row(tm, dm), _row(tm, dm)], out_shape=[SDS((bsz, seq, dm), BF16), shp],
        compiler_params=_cp("parallel", "parallel"),
    )(hid, w2, x, gt)


def _ffn_bwd_hid(dxo, gt, f, a, b, w2, name):
    bsz, seq, dm = dxo.shape
    nf = a.shape[2]
    tm = _pick_tile(seq, (512, 256, 128))
    tn = _pick_tile(nf, (1408, 512, 256, 128))

    def body(dxo_ref, gt_ref, f_ref, a_ref, b_ref, w2_ref, dfs_ref, da_ref, db_ref, dgt_ref):
        i = pl.program_id(1)
        n = pl.program_id(2)

        @pl.when(n == 0)
        def _():
            dxo = dxo_ref[0]
            dfs_ref[0] = (0.5 * gt_ref[0] * dxo).astype(BF16)
            part = jnp.sum(0.5 * dxo * f_ref[0].astype(F32), axis=0, keepdims=True)

            @pl.when(i == 0)
            def _():
                dgt_ref[0] = part

            @pl.when(i > 0)
            def _():
                dgt_ref[0] += part

        dhid = _dot_nt(dfs_ref[0], w2_ref[pl.ds(pl.multiple_of(n * tn, tn), tn), :])
        dh16 = dhid.astype(BF16)
        da_ref[0] = dh16 * a_ref[0]
        db_ref[0] = dh16 * b_ref[0]

    hspec = pl.BlockSpec((1, tm, tn), lambda b, i, n: (b, i, n))
    return pl.pallas_call(
        body, name=name, grid=(bsz, seq // tm, nf // tn),
        in_specs=[_row(tm, dm), _seqvec(dm), _row(tm, dm), hspec, hspec, _full2((nf, dm))],
        out_specs=[_row(tm, dm), hspec, hspec, _seqvec(dm)],
        out_shape=[SDS((bsz, seq, dm), BF16), SDS((bsz, seq, nf), BF16), SDS((bsz, seq, nf), BF16),
                   SDS((bsz, 1, dm), F32)],
        compiler_params=_cp("arbitrary", "arbitrary", "arbitrary"),
    )(dxo, gt, f, a, b, w2)


def _dh_norm_bwd(pieces, weights, x, g, sc, dxo, name, transposed=False):
    bsz, seq, dm = x.shape
    tm = _pick_tile(seq, (512, 256, 128))
    npc = len(pieces)
    dot = _dot if transposed else _dot_nt

    def body(*refs):
        p_refs = refs[:npc]
        w_hbm = refs[npc:2 * npc]
        x_ref, g_ref, sc_ref, dxo_ref, dx_ref, dsh_ref, dsc_ref, dg_ref = refs[2 * npc:2 * npc + 8]
        w_refs = refs[2 * npc + 8:]
        b = pl.program_id(0)
        i = pl.program_id(1)

        @pl.when((i == 0) & (b == 0))
        def _():
            for src, dst in zip(w_hbm, w_refs):
                pltpu.sync_copy(src, dst)

        dh = dot(p_refs[0][0], w_refs[0][...])
        for j in range(1, npc):
            dh = dh + dot(p_refs[j][0], w_refs[j][...])
        xf = x_ref[0]
        gv = g_ref[...]
        r = lax.rsqrt(jnp.mean(xf * xf, axis=-1, keepdims=True) + EPS)
        xhat = xf * r
        dhn = dh * (1.0 + sc_ref[0])
        p_sh = jnp.sum(dh, axis=0, keepdims=True)
        p_sc = jnp.sum(dh * (xhat * gv), axis=0, keepdims=True)
        p_g = jnp.sum(dhn * xhat, axis=0, keepdims=True)
        dxh = dhn * gv
        dx_ref[0] = dxo_ref[0] + r * (dxh - xhat * jnp.mean(dxh * xhat, axis=-1, keepdims=True))

        @pl.when(i == 0)
        def _():
            dsh_ref[0] = p_sh
            dsc_ref[0] = p_sc

        @pl.when(i > 0)
        def _():
            dsh_ref[0] += p_sh
            dsc_ref[0] += p_sc

        @pl.when((i == 0) & (b == 0))
        def _():
            dg_ref[...] = p_g

        @pl.when((i > 0) | (b > 0))
        def _():
            dg_ref[...] += p_g

    return pl.pallas_call(
        body, name=name, grid=(bsz, seq // tm),
        in_specs=[_row(tm, p.shape[2]) for p in pieces] + [pl.BlockSpec(memory_space=pl.ANY)] * npc + [
            _row(tm, dm), _full2((1, dm)), _seqvec(dm), _row(tm, dm)],
        out_specs=[_row(tm, dm), _seqvec(dm), _seqvec(dm), _full2((1, dm))],
        out_shape=[SDS((bsz, seq, dm), F32), SDS((bsz, 1, dm), F32), SDS((bsz, 1, dm), F32), SDS((1, dm), F32)],
        scratch_shapes=[pltpu.VMEM(w.shape, w.dtype) for w in weights],
        compiler_params=_cp("arbitrary", "arbitrary"),
    )(*pieces, *weights, x, g, sc, dxo)


HALO = 16


def _conv_core(gc, v, gch, vh, w, first):
    cv = gc * v
    halo = jnp.where(first, 0.0, gch * vh)
    ext = jnp.concatenate([halo, cv], axis=0)
    cv1 = pltpu.roll(ext, 1, 0)[HALO:]
    cv2 = pltpu.roll(ext, 2, 0)[HALO:]
    conv = w[0:1] * cv2 + w[1:2] * cv1 + w[2:3] * cv
    return cv, cv1, cv2, conv


def _prev_halo(tm, col):
    return pl.BlockSpec((1, HALO, CW), lambda b, i, *_: (b, jnp.maximum(i * (tm // HALO) - 1, 0), col))


def _next_halo(tm, seq, col):
    return pl.BlockSpec((1, HALO, CW), lambda b, i, *_: (b, jnp.minimum((i + 1) * (tm // HALO), seq // HALO - 1), col))


def _conv_fwd(p1, convw8, name):
    bsz, seq, _ = p1.shape
    tm = _pick_tile(seq, (512, 256, 128))

    def body(gb_ref, gc_ref, v_ref, gch_ref, vh_ref, w_ref, o_ref):
        first = pl.program_id(1) == 0
        _, _, _, conv = _conv_core(gc_ref[0].astype(F32), v_ref[0].astype(F32), gch_ref[0].astype(F32),
                                   vh_ref[0].astype(F32), w_ref[...], first)
        o_ref[0] = (gb_ref[0].astype(F32) * conv).astype(BF16)

    return pl.pallas_call(
        body, name=name, grid=(bsz, seq // tm),
        in_specs=[_row(tm, CW, 0), _row(tm, CW, 1), _row(tm, CW, 2), _prev_halo(tm, 1), _prev_halo(tm, 2),
                  _full2((8, CW))],
        out_specs=_row(tm, CW), out_shape=SDS((bsz, seq, CW), BF16),
        compiler_params=_cp("parallel", "parallel"),
    )(p1, p1, p1, p1, p1, convw8)


def _conv_bwd(dya, wco, p1, convw8, name):
    bsz, seq, _ = p1.shape
    tm = _pick_tile(seq, (512, 256, 128))
    nt = seq // tm
    ext_rows = tm + HALO

    def body(dya_ref, dyan_ref, wco_ref, gb_ref, gbn_ref, gc_ref, v_ref, gch_ref, vh_ref, w_ref, dp_ref, dw_ref):
        b = pl.program_id(0)
        i = pl.program_id(1)
        w = w_ref[...]
        gc = gc_ref[0].astype(F32)
        vv = v_ref[0].astype(F32)
        cv, cv1, cv2, conv = _conv_core(gc, vv, gch_ref[0].astype(F32), vh_ref[0].astype(F32), w, i == 0)
        dya_ext = jnp.concatenate([dya_ref[0], dyan_ref[0]], axis=0)
        dyain_ext = _dot_nt(dya_ext, wco_ref[...])
        gb_ext = jnp.concatenate([gb_ref[0], gbn_ref[0]], axis=0).astype(F32)
        rows = lax.broadcasted_iota(jnp.int32, (ext_rows, 1), 0)
        dconv_ext = jnp.where((rows < tm) | (i < nt - 1), dyain_ext * gb_ext, 0.0)
        dconv = dconv_ext[:tm]
        dconv1 = pltpu.roll(dconv_ext, ext_rows - 1, 0)[:tm]
        dconv2 = pltpu.roll(dconv_ext, ext_rows - 2, 0)[:tm]
        dcv = w[2:3] * dconv + w[1:2] * dconv1 + w[0:1] * dconv2
        dp_ref[0, :, 0:CW] = (dyain_ext[:tm] * conv).astype(BF16)
        dp_ref[0, :, CW:2 * CW] = (dcv * vv).astype(BF16)
        dp_ref[0, :, 2 * CW:3 * CW] = (dcv * gc).astype(BF16)
        g0 = jnp.sum(dconv * cv2, axis=0, keepdims=True)
        g1 = jnp.sum(dconv * cv1, axis=0, keepdims=True)
        g2 = jnp.sum(dconv * cv, axis=0, keepdims=True)
        upd = jnp.concatenate([g0, g1, g2, jnp.zeros((5, CW), F32)], axis=0)

        @pl.when((i == 0) & (b == 0))
        def _():
            dw_ref[...] = upd

        @pl.when((i > 0) | (b > 0))
        def _():
            dw_ref[...] += upd

    return pl.pallas_call(
        body, name=name, grid=(bsz, seq // tm),
        in_specs=[_row(tm, CW), _next_halo(tm, seq, 0), _full2((CW, D)),
                  _row(tm, CW, 0), _next_halo(tm, seq, 0), _row(tm, CW, 1), _row(tm, CW, 2),
                  _prev_halo(tm, 1), _prev_halo(tm, 2), _full2((8, CW))],
        out_specs=[_row(tm, 3 * CW), _full2((8, CW))],
        out_shape=[SDS((bsz, seq, 3 * CW), BF16), SDS((8, CW), F32)],
        compiler_params=_cp("arbitrary", "arbitrary"),
    )(dya, dya, wco, p1, p1, p1, p1, p1, p1, convw8)


def _disc(are, aim, ldt, bre, bim):
    dt = jnp.exp(ldt)
    mag = jnp.exp(are * dt)
    ang = aim * dt
    abr = mag * jnp.cos(ang)
    abi = mag * jnp.sin(ang)
    nr = abr - 1.0
    den = are * are + aim * aim
    cr = (nr * are + abi * aim) / den
    ci = (abi * are - nr * aim) / den
    return abr, abi, cr * bre - ci * bim, cr * bim + ci * bre


def _ssm_disc(are, aim, ldt, bre_t, bim_t):
    def body(are_ref, aim_ref, ldt_ref, bre_ref, bim_ref, abr_ref, abi_ref, bbr_ref, bbi_ref):
        abr, abi, bbr, bbi = _disc(are_ref[...], aim_ref[...], ldt_ref[...], bre_ref[...], bim_ref[...])
        abr_ref[...] = abr
        abi_ref[...] = abi
        bbr_ref[...] = bbr
        bbi_ref[...] = bbi

    v1, vh = SDS((1, GP), F32), SDS((NH, GP), F32)
    return pl.pallas_call(body, name="ssm_disc", out_shape=[v1, v1, vh, vh], compiler_params=_cp())(
        are, aim, ldt, bre_t, bim_t)


def _ssm_disc_bwd(are, aim, ldt, bre_t, bim_t, dabr, dabi, dbbr, dbbi):
    def body(are_ref, aim_ref, ldt_ref, bre_ref, bim_ref, g0, g1, g2, g3, o0, o1, o2, o3, o4):
        prim = (are_ref[...], aim_ref[...], ldt_ref[...], bre_ref[...], bim_ref[...])
        _, vjp = jax.vjp(_disc, *prim)
        d_are, d_aim, d_ldt, d_bre, d_bim = vjp((g0[...], g1[...], g2[...], g3[...]))
        o0[...] = d_are
        o1[...] = d_aim
        o2[...] = d_ldt
        o3[...] = d_bre
        o4[...] = d_bim

    v1, vh = SDS((1, GP), F32), SDS((NH, GP), F32)
    return pl.pallas_call(body, name="ssm_disc_bwd", out_shape=[v1, v1, v1, vh, vh], compiler_params=_cp())(
        are, aim, ldt, bre_t, bim_t, dabr, dabi, dbbr, dbbi)


def _scan_chunk(buf_ref, nt, bsz, ar, ai, init_r, init_i, reverse):
    nsub = SUB // bsz
    row = lax.broadcasted_iota(jnp.int32, (SUB, GP), 0)
    shift = ((SUB - bsz) if reverse else bsz) % SUB
    order = list(range(nsub - 1, -1, -1)) if reverse else list(range(nsub))

    def step(j, carry):
        pr, pi = carry
        jj = (nt - 1 - j) if reverse else j
        off = pl.multiple_of(jj * SUB, SUB)
        br = buf_ref[pl.ds(off, SUB), 0:GP]
        bi = buf_ref[pl.ds(off, SUB), GP:2 * GP]
        nr, ni = pr, pi
        for s in order:
            sr, si = nr, ni
            if shift:
                sr = pltpu.roll(sr, shift, 0)
                si = pltpu.roll(si, shift, 0)
            cr = ar * sr - ai * si + br
            ci = ar * si + ai * sr + bi
            if nsub == 1:
                nr, ni = cr, ci
            else:
                m = (row >= s * bsz) & (row < (s + 1) * bsz)
                nr = jnp.where(m, cr, nr)
                ni = jnp.where(m, ci, ni)
        buf_ref[pl.ds(off, SUB), 0:GP] = nr
        buf_ref[pl.ds(off, SUB), GP:2 * GP] = ni
        return nr, ni

    return lax.fori_loop(0, nt, step, (init_r, init_i))


def _ssm_chunk_rows(total_rows, bsz):
    return min(total_rows, 64 * bsz)


def _interleave(src_ref, tmp_ref, bsz, steps):
    nl = tmp_ref.shape[0]
    for b in range(bsz):
        for j in range(nl):
            tmp_ref.at[j][pl.ds(b, steps, stride=bsz), :] = src_ref[b, :, j * LANE:(j + 1) * LANE]
    return jnp.concatenate([tmp_ref[j] for j in range(nl)], axis=1)


def _deinterleave(val, tmp_ref, dst_ref, bsz, steps, skip=None):
    nl = tmp_ref.shape[0]
    for j in range(nl):
        tmp_ref[j] = val[:, j * LANE:(j + 1) * LANE]
    for b in range(bsz):
        for j in range(nl):
            lanes = slice(j * LANE, (j + 1) * LANE)
            v = tmp_ref.at[j][pl.ds(b, steps, stride=bsz), :]
            if skip is not None:
                v = v + skip[0][b, :, lanes] * skip[1][:, lanes]
            dst_ref[b, :, lanes] = v.astype(dst_ref.dtype)


SSM_UB = SW // SSM_SUPER
SSM_SB = GP // SSM_SUPER


def _sb_cols(s, half):
    return slice(half * GP + s * SSM_SB, half * GP + (s + 1) * SSM_SB)


def _ssm_in(v16, w_ref, x_ref):
    for s in range(SSM_SUPER):
        vs = v16[:, s * SSM_UB:(s + 1) * SSM_UB]
        for half in range(2):
            x_ref[:, _sb_cols(s, half)] = _dot(vs, w_ref[s * SSM_UB:(s + 1) * SSM_UB, half * SSM_SB:(half + 1) * SSM_SB])


def _ssm_out(x16, w_ref):
    outs = []
    for s in range(SSM_SUPER):
        rows = slice(s * SSM_UB, (s + 1) * SSM_UB)
        outs.append(_dot_nt(x16[:, _sb_cols(s, 0)], w_ref[rows, 0:SSM_SB])
                    + _dot_nt(x16[:, _sb_cols(s, 1)], w_ref[rows, SSM_SB:2 * SSM_SB]))
    return jnp.concatenate(outs, axis=1)


def _ssm_fwd(u, wb, wct, ar8, ai8, name):
    bsz, seq, _ = u.shape
    rt = seq * bsz
    r = _ssm_chunk_rows(rt, bsz)
    nt = r // SUB
    steps = r // bsz

    def body(u_ref, wb_hbm, wct_hbm, ar_ref, ai_ref, x_ref, y_ref, wb_ref, wct_ref, st_ref, tmp_ref):
        @pl.when(pl.program_id(0) == 0)
        def _():
            pltpu.sync_copy(wb_hbm, wb_ref)
            pltpu.sync_copy(wct_hbm, wct_ref)
            st_ref[...] = jnp.zeros_like(st_ref)

        _ssm_in(_interleave(u_ref, tmp_ref, bsz, steps).astype(BF16), wb_ref, x_ref)
        fr, fi = _scan_chunk(x_ref, nt, bsz, ar_ref[...], ai_ref[...], st_ref[:, 0:GP], st_ref[:, GP:2 * GP], False)
        st_ref[:, 0:GP] = fr
        st_ref[:, GP:2 * GP] = fi
        _deinterleave(_ssm_out(x_ref[...].astype(BF16), wct_ref), tmp_ref, y_ref, bsz, steps)

    anyspec = pl.BlockSpec(memory_space=pl.ANY)
    seqs = pl.BlockSpec((bsz, steps, SW), lambda i: (0, i, 0))
    return pl.pallas_call(
        body, name=name, grid=(rt // r,),
        in_specs=[seqs, anyspec, anyspec, _full2((SUB, GP)), _full2((SUB, GP))],
        out_specs=[pl.BlockSpec((r, 2 * GP), lambda i: (i, 0)), seqs],
        out_shape=[SDS((rt, 2 * GP), F32), SDS((bsz, seq, SW), F32)],
        scratch_shapes=[pltpu.VMEM(wb.shape, BF16), pltpu.VMEM(wct.shape, BF16), pltpu.VMEM((SUB, 2 * GP), F32),
                        pltpu.VMEM((SW // LANE, r, LANE), F32)],
        compiler_params=_cp("arbitrary"),
    )(u, wb, wct, ar8, ai8)


def _ssm_bwd(dy, u, xs, wb, wct, ar8, ai8, dskip, name):
    bsz, seq, _ = u.shape
    rt = seq * bsz
    r = _ssm_chunk_rows(rt, bsz)
    nt = r // SUB
    nc = rt // r
    steps = r // bsz
    ub = SW // SSM_SUPER
    sb = GP // SSM_SUPER

    def body(dys_ref, us_ref, x_ref, xh_ref, wb_hbm, wct_hbm, ar_ref, ai_ref, d_ref,
             dus_ref, dwb_hbm, dwct_hbm, dar_ref, dai_ref, wb_ref, wct_ref, g_ref, st_ref, awb_ref, awct_ref,
             tmp_ref):
        i = pl.program_id(0)
        dyb = _interleave(dys_ref, tmp_ref, bsz, steps).astype(BF16)
        ub16 = _interleave(us_ref, tmp_ref, bsz, steps).astype(BF16)

        @pl.when(i == 0)
        def _():
            pltpu.sync_copy(wb_hbm, wb_ref)
            pltpu.sync_copy(wct_hbm, wct_ref)
            st_ref[...] = jnp.zeros_like(st_ref)
            awb_ref[...] = jnp.zeros_like(awb_ref)
            awct_ref[...] = jnp.zeros_like(awct_ref)
            dar_ref[...] = jnp.zeros_like(dar_ref)
            dai_ref[...] = jnp.zeros_like(dai_ref)

        _ssm_in(dyb, wct_ref, g_ref)
        ar = ar_ref[...]
        ai = ai_ref[...]
        fr, fi = _scan_chunk(g_ref, nt, bsz, ar, -ai, st_ref[:, 0:GP], st_ref[:, GP:2 * GP], True)
        st_ref[:, 0:GP] = fr
        st_ref[:, GP:2 * GP] = fi

        gb = g_ref[...].astype(BF16)
        _deinterleave(_ssm_out(gb, wb_ref), tmp_ref, dus_ref, bsz, steps, skip=(dys_ref, d_ref))
        xb16 = x_ref[...].astype(BF16)
        for s in range(SSM_SUPER):
            us = ub16[:, s * ub:(s + 1) * ub]
            ds = dyb[:, s * ub:(s + 1) * ub]
            for half in range(2):
                cols = slice(half * GP + s * sb, half * GP + (s + 1) * sb)
                ocols = slice(half * sb, (half + 1) * sb)
                awb_ref[s * ub:(s + 1) * ub, ocols] += _dot_tn(us, gb[:, cols])
                awct_ref[s * ub:(s + 1) * ub, ocols] += _dot_tn(ds, xb16[:, cols])

        gr = g_ref[:, 0:GP]
        gi = g_ref[:, GP:2 * GP]
        xsr = pltpu.roll(x_ref[:, 0:GP], bsz, 0)
        xsi = pltpu.roll(x_ref[:, GP:2 * GP], bsz, 0)
        inner = lax.broadcasted_iota(jnp.int32, (r, 1), 0) >= bsz
        t_r = jnp.where(inner, gr * xsr + gi * xsi, 0.0)
        t_i = jnp.where(inner, gi * xsr - gr * xsi, 0.0)
        acc_r = jnp.sum(t_r.reshape(nt, SUB, GP), axis=0)
        acc_i = jnp.sum(t_i.reshape(nt, SUB, GP), axis=0)
        hr = xh_ref[:, 0:GP]
        hi = xh_ref[:, GP:2 * GP]
        if bsz % SUB:
            hr = pltpu.roll(hr, bsz, 0)
            hi = pltpu.roll(hi, bsz, 0)
        edge = (lax.broadcasted_iota(jnp.int32, (SUB, 1), 0) < bsz) & (i < nc - 1)
        g0r = g_ref[0:SUB, 0:GP]
        g0i = g_ref[0:SUB, GP:2 * GP]
        dar_ref[...] += acc_r + jnp.where(edge, g0r * hr + g0i * hi, 0.0)
        dai_ref[...] += acc_i + jnp.where(edge, g0i * hr - g0r * hi, 0.0)

        @pl.when(i == nc - 1)
        def _():
            pltpu.sync_copy(awb_ref, dwb_hbm)
            pltpu.sync_copy(awct_ref, dwct_hbm)

    anyspec = pl.BlockSpec(memory_space=pl.ANY)
    rev = lambda i: (nc - 1 - i, 0)
    seqs = pl.BlockSpec((bsz, steps, SW), lambda i: (0, nc - 1 - i, 0))
    wshape = (SW, 2 * sb)
    return pl.pallas_call(
        body, name=name, grid=(nc,),
        in_specs=[seqs, seqs, pl.BlockSpec((r, 2 * GP), rev),
                  pl.BlockSpec((SUB, 2 * GP), lambda i: (jnp.maximum((nc - 1 - i) * nt - 1, 0), 0)),
                  anyspec, anyspec, _full2((SUB, GP)), _full2((SUB, GP)), _full2((1, SW))],
        out_specs=[seqs, anyspec, anyspec, _full2((SUB, GP)), _full2((SUB, GP))],
        out_shape=[SDS((bsz, seq, SW), BF16), SDS(wshape, F32), SDS(wshape, F32), SDS((SUB, GP), F32),
                   SDS((SUB, GP), F32)],
        scratch_shapes=[pltpu.VMEM(wb.shape, BF16), pltpu.VMEM(wct.shape, BF16),
                        pltpu.VMEM((r, 2 * GP), F32), pltpu.VMEM((SUB, 2 * GP), F32),
                        pltpu.VMEM(wshape, F32), pltpu.VMEM(wshape, F32),
                        pltpu.VMEM((SW // LANE, r, LANE), F32)],
        compiler_params=_cp("arbitrary"),
    )(dy, u, xs, xs, wb, wct, ar8, ai8, dskip)


GELU_C = math.sqrt(2.0 / math.pi)


def _gelu(x):
    return 0.5 * x * (1.0 + jnp.tanh(GELU_C * (x + 0.044715 * x * x * x)))


def _gelu_grad(x):
    th = jnp.tanh(GELU_C * (x + 0.044715 * x * x * x))
    return 0.5 * (1.0 + th) + 0.5 * x * (1.0 - th * th) * GELU_C * (1.0 + 3.0 * 0.044715 * x * x)


def _ssm_post(ys, u, dskip, wglu, wso, name):
    bsz, seq, _ = ys.shape
    tm = _pick_tile(seq, (512, 256, 128))

    def body(ys_ref, u_ref, d_ref, wg_ref, wo_ref, s0_ref, z_ref, s1_ref, s2_ref, yb_ref):
        s0 = ys_ref[0] + d_ref[...] * u_ref[0]
        s1 = _gelu(s0)
        s1b = s1.astype(BF16)
        z = _dot(s1b, wg_ref[...])
        s2b = (s1 * _sigmoid(z)).astype(BF16)
        s0_ref[0] = s0
        z_ref[0] = z
        s1_ref[0] = s1b
        s2_ref[0] = s2b
        yb_ref[0] = _dot(s2b, wo_ref[...]).astype(BF16)

    return pl.pallas_call(
        body, name=name, grid=(bsz, seq // tm),
        in_specs=[_row(tm, SW), _row(tm, SW), _full2((1, SW)), _full2((SW, SW)), _full2((SW, D))],
        out_specs=[_row(tm, SW), _row(tm, SW), _row(tm, SW), _row(tm, SW), _row(tm, D)],
        out_shape=[SDS((bsz, seq, SW), F32), SDS((bsz, seq, SW), F32), SDS((bsz, seq, SW), BF16),
                   SDS((bsz, seq, SW), BF16), SDS((bsz, seq, D), BF16)],
        compiler_params=_cp("parallel", "parallel"),
    )(ys, u, dskip, wglu, wso)


def _ssm_post_bwd(dyb, s0, z, u, dskip, wglu, wso, name):
    bsz, seq, _ = s0.shape
    tm = _pick_tile(seq, (512, 256, 128))

    def body(dyb_ref, s0_ref, z_ref, u_ref, wg_ref, wo_ref, ds0_ref, dz_ref, dd_ref):
        b = pl.program_id(0)
        i = pl.program_id(1)
        ds2 = _dot_nt(dyb_ref[0], wo_ref[...])
        s0 = s0_ref[0]
        s1 = _gelu(s0)
        sg = _sigmoid(z_ref[0])
        dz = ds2 * s1 * sg * (1.0 - sg)
        dzb = dz.astype(BF16)
        ds1 = ds2 * sg + _dot_nt(dzb, wg_ref[...])
        ds0 = ds1 * _gelu_grad(s0)
        ds0_ref[0] = ds0
        dz_ref[0] = dzb
        part = jnp.sum(ds0 * u_ref[0], axis=0, keepdims=True)

        @pl.when((i == 0) & (b == 0))
        def _():
            dd_ref[...] = part

        @pl.when((i > 0) | (b > 0))
        def _():
            dd_ref[...] += part

    del dskip
    return pl.pallas_call(
        body, name=name, grid=(bsz, seq // tm),
        in_specs=[_row(tm, D), _row(tm, SW), _row(tm, SW), _row(tm, SW), _full2((SW, SW)), _full2((SW, D))],
        out_specs=[_row(tm, SW), _row(tm, SW), _full2((1, SW))],
        out_shape=[SDS((bsz, seq, SW), F32), SDS((bsz, seq, SW), BF16), SDS((1, SW), F32)],
        compiler_params=_cp("arbitrary", "arbitrary"),
    )(dyb, s0, z, u, wglu, wso)


def _merge_out(ya, yb, p3, wout, x1, gt, name):
    bsz, seq, _ = ya.shape
    tm = _pick_tile(seq, (512, 256, 128))

    def body(ya_ref, yb_ref, ga_ref, gbb_ref, w_ref, x_ref, gt_ref, mg_ref, mix_ref, xo_ref):
        merged = (_sigmoid(ga_ref[0].astype(F32)) * ya_ref[0].astype(F32)
                  + _sigmoid(gbb_ref[0].astype(F32)) * yb_ref[0].astype(F32)).astype(BF16)
        mix = _dot(merged, w_ref[...])
        mg_ref[0] = merged
        mix_ref[0] = mix.astype(BF16)
        xo_ref[0] = x_ref[0] + gt_ref[0] * mix

    return pl.pallas_call(
        body, name=name, grid=(bsz, seq // tm),
        in_specs=[_row(tm, D), _row(tm, D), _row(tm, D, 0), _row(tm, D, 1), _full2((D, D)), _row(tm, D), _seqvec(D)],
        out_specs=[_row(tm, D), _row(tm, D), _row(tm, D)],
        out_shape=[SDS((bsz, seq, D), BF16), SDS((bsz, seq, D), BF16), SDS((bsz, seq, D), F32)],
        compiler_params=_cp("parallel", "parallel"),
    )(ya, yb, p3, p3, wout, x1, gt)


def _merge_bwd(dx2, gt, mix, ya, yb, p3, wout, name):
    bsz, seq, _ = ya.shape
    tm = _pick_tile(seq, (512, 256, 128))

    def body(dx_ref, gt_ref, mix_ref, ya_ref, yb_ref, ga_ref, gbb_ref, w_ref, dmix_ref, dya_ref, dyb_ref, dp_ref, dgt_ref):
        i = pl.program_id(1)
        dx = dx_ref[0]
        dmix = (gt_ref[0] * dx).astype(BF16)
        dmix_ref[0] = dmix
        part = jnp.sum(dx * mix_ref[0].astype(F32), axis=0, keepdims=True)

        @pl.when(i == 0)
        def _():
            dgt_ref[0] = part

        @pl.when(i > 0)
        def _():
            dgt_ref[0] += part

        dmg = _dot_nt(dmix, w_ref[...])
        sa = _sigmoid(ga_ref[0].astype(F32))
        sb = _sigmoid(gbb_ref[0].astype(F32))
        dya_ref[0] = (dmg * sa).astype(BF16)
        dyb_ref[0] = (dmg * sb).astype(BF16)
        dp_ref[0, :, 0:D] = (dmg * ya_ref[0].astype(F32) * sa * (1.0 - sa)).astype(BF16)
        dp_ref[0, :, D:2 * D] = (dmg * yb_ref[0].astype(F32) * sb * (1.0 - sb)).astype(BF16)

    bshape = SDS((bsz, seq, D), BF16)
    return pl.pallas_call(
        body, name=name, grid=(bsz, seq // tm),
        in_specs=[_row(tm, D), _seqvec(D), _row(tm, D), _row(tm, D), _row(tm, D), _row(tm, D, 0), _row(tm, D, 1),
                  _full2((D, D))],
        out_specs=[_row(tm, D), _row(tm, D), _row(tm, D), _row(tm, 2 * D), _seqvec(D)],
        out_shape=[bshape, bshape, bshape, SDS((bsz, seq, 2 * D), BF16), SDS((bsz, 1, D), F32)],
        compiler_params=_cp("arbitrary", "arbitrary"),
    )(dx2, gt, mix, ya, yb, p3, p3, wout)


def _final_loss(x3, gfin, target, name):
    bsz, seq, dm = x3.shape
    tm = _pick_tile(seq, (512, 256, 128))

    def body(x_ref, g_ref, t_ref, dx_ref, loss_ref, dg_ref):
        b = pl.program_id(0)
        i = pl.program_id(1)
        xf = x_ref[0]
        gv = g_ref[...]
        r = lax.rsqrt(jnp.mean(xf * xf, axis=-1, keepdims=True) + EPS)
        xhat = xf * r
        e = xhat * gv - t_ref[0]
        dy = e * (1.0 / dm)
        dxh = dy * gv
        dx_ref[0] = r * (dxh - xhat * jnp.mean(dxh * xhat, axis=-1, keepdims=True))
        p_l = jnp.sum(e * e, axis=0, keepdims=True) * (0.5 / dm)
        p_g = jnp.sum(dy * xhat, axis=0, keepdims=True)

        @pl.when((i == 0) & (b == 0))
        def _():
            loss_ref[...] = p_l
            dg_ref[...] = p_g

        @pl.when((i > 0) | (b > 0))
        def _():
            loss_ref[...] += p_l
            dg_ref[...] += p_g

    return pl.pallas_call(
        body, name=name, grid=(bsz, seq // tm),
        in_specs=[_row(tm, dm), _full2((1, dm)), _row(tm, dm)],
        out_specs=[_row(tm, dm), _full2((1, dm)), _full2((1, dm))],
        out_shape=[SDS((bsz, seq, dm), F32), SDS((1, dm), F32), SDS((1, dm), F32)],
        compiler_params=_cp("arbitrary", "arbitrary"),
    )(x3, gfin, target)


def _ada_fwd(c_all, w_shard, b_shard):
    nb = c_all.shape[0]
    n = w_shard.shape[2]

    def body(c_ref, w_ref, b_ref, o_ref):
        cv = c_ref[...]
        cond = (cv * _sigmoid(cv)).astype(BF16)
        o_ref[...] = _dot(cond, w_ref[0].astype(BF16)) + b_ref[...]

    return pl.pallas_call(body, name="ada_fwd", out_shape=SDS((nb, n), F32), compiler_params=_cp())(
        c_all, w_shard, b_shard)


def _ada_bwd(c_all, dmod_shard, dmod_all):
    n = dmod_shard.shape[1]

    def body(c_ref, ds_ref, da_ref, gw_ref, gb_ref):
        cv = c_ref[...]
        cond = (cv * _sigmoid(cv)).astype(BF16)
        gw_ref[...] = _dot_tn(cond, ds_ref[...].astype(BF16))
        gb_ref[...] = jnp.sum(da_ref[...], axis=0, keepdims=True)

    return pl.pallas_call(
        body, name="ada_bwd", out_shape=[SDS((D, n), F32), SDS((1, dmod_all.shape[1]), F32)], compiler_params=_cp(),
    )(c_all, dmod_shard, dmod_all)


def _adamw_math(w, g, m, v):
    m = B1 * m + (1.0 - B1) * g
    v = B2 * v + (1.0 - B2) * (g * g)
    delta = -LR * ((m / BC1) / (jnp.sqrt(v / BC2) + AEPS) + WD * w)
    return delta, m, v


def _adamw_big(w, m, v, recv_own, recv_sib, name):
    _, rows, cols = w.shape
    tr = _pick_tile(rows, tuple(t for t in (512, 256, 128, 64, 32, 16, 8) if t * cols <= 192 * 1024))

    def body(w_ref, m_ref, v_ref, a_ref, b_ref, g_ref, d_ref, mo_ref, vo_ref):
        def chip_sum(r):
            acc = r[0].astype(F32)
            for k in range(1, N_CHIPS):
                acc = acc + r[k].astype(F32)
            return acc

        g = chip_sum(a_ref) + chip_sum(b_ref)
        delta, mn, vn = _adamw_math(w_ref[0], g, m_ref[0], v_ref[0])
        g_ref[0] = g
        d_ref[0] = delta
        mo_ref[0] = mn
        vo_ref[0] = vn

    own = pl.BlockSpec((1, tr, cols), lambda i: (0, i, 0))
    rspec = pl.BlockSpec((N_CHIPS, tr, cols), lambda i: (0, i, 0))
    shp = SDS(w.shape, F32)
    return pl.pallas_call(
        body, name=name, grid=(rows // tr,),
        in_specs=[own, own, own, rspec, rspec], out_specs=[own, own, own, own], out_shape=[shp, shp, shp, shp],
        compiler_params=_cp("parallel"),
    )(w, m, v, recv_own, recv_sib)


def _adamw_plain(w, m, v, g, name):
    def body(w_ref, m_ref, v_ref, g_ref, d_ref, mo_ref, vo_ref):
        delta, mn, vn = _adamw_math(w_ref[...], g_ref[...], m_ref[...], v_ref[...])
        d_ref[...] = delta
        mo_ref[...] = mn
        vo_ref[...] = vn

    shp = SDS(w.shape, F32)
    return pl.pallas_call(body, name=name, out_shape=[shp, shp, shp], compiler_params=_cp())(w, m, v, g)


def _adamw_rows(w, m, v, g, name):
    _, rows, cols = w.shape
    tr = _pick_tile(rows, (128, 64, 32, 16, 8))

    def body(w_ref, m_ref, v_ref, g_ref, d_ref, mo_ref, vo_ref):
        delta, mn, vn = _adamw_math(w_ref[0], g_ref[...], m_ref[0], v_ref[0])
        d_ref[0] = delta
        mo_ref[0] = mn
        vo_ref[0] = vn

    spec = pl.BlockSpec((1, tr, cols), lambda i: (0, i, 0))
    shp = SDS(w.shape, F32)
    return pl.pallas_call(
        body, name=name, grid=(rows // tr,), in_specs=[spec] * 3 + [pl.BlockSpec((tr, cols), lambda i: (i, 0))],
        out_specs=[spec] * 3, out_shape=[shp] * 3, compiler_params=_cp("parallel"),
    )(w, m, v, g)


def _sum_slabs(r, name):
    n, rows, cols = r.shape
    tr = _pick_tile(rows, (256, 128, 64))

    def body(r_ref, o_ref):
        acc = r_ref[0].astype(F32)
        for j in range(1, n):
            acc = acc + r_ref[j].astype(F32)
        o_ref[...] = acc

    return pl.pallas_call(
        body, name=name, grid=(rows // tr,),
        in_specs=[pl.BlockSpec((n, tr, cols), lambda i: (0, i, 0))],
        out_specs=pl.BlockSpec((tr, cols), lambda i: (i, 0)), out_shape=SDS((rows, cols), F32),
        compiler_params=_cp("parallel"),
    )(r)


def _place():
    return lax.axis_index("x"), lax.axis_index("y"), lax.axis_index("c")


def _all_gather8(blk, name):
    m_per, n = blk.shape

    def body(x_ref, out_ref, send_sems, recv_sems, local_sem):
        x, y, c = _place()
        me, sibling = (x, y, c), (x, y, 1 - c)
        chips = [(1 - x, y), (x, 1 - y), (1 - x, 1 - y)]

        def rows(px, py, pc):
            return out_ref.at[pl.ds((4 * px + 2 * py + pc) * m_per, m_per), :]

        def copy(k, block, to, src=None):
            return pltpu.make_async_remote_copy(
                src_ref=rows(*block) if src is None else src, dst_ref=rows(*block),
                send_sem=send_sems.at[k], recv_sem=recv_sems.at[k], device_id=to, device_id_type=MESH)

        mine = pltpu.make_async_copy(x_ref, rows(*me), local_sem)
        mine.start()
        first = [copy(0, me, sibling, src=x_ref)]
        first += [copy(1 + j, me, (*chip, c), src=x_ref) for j, chip in enumerate(chips)]
        for cp in first:
            cp.start()
        passed = [copy(4 + j, (*chip, c), sibling) for j, chip in enumerate(chips)]
        for j, chip in enumerate(chips):
            copy(1 + j, (*chip, c), me).wait_recv()
            passed[j].start()
        copy(0, sibling, me).wait_recv()
        for j, chip in enumerate(chips):
            copy(4 + j, (*chip, 1 - c), me).wait_recv()
        for cp in first + passed:
            cp.wait_send()
        mine.wait()

    return pl.pallas_call(
        body, name=name, out_shape=SDS((N_DEV * m_per, n), blk.dtype),
        in_specs=[pl.BlockSpec(memory_space=pltpu.VMEM)], out_specs=pl.BlockSpec(memory_space=pltpu.VMEM),
        scratch_shapes=[pltpu.SemaphoreType.DMA((7,)), pltpu.SemaphoreType.DMA((7,)), pltpu.SemaphoreType.DMA],
        compiler_params=pltpu.CompilerParams(vmem_limit_bytes=VMEM_LIMIT),
    )(blk)


def _chip_peers(x, y):
    return [(1 - x, y), (x, 1 - y), (1 - x, 1 - y)]


SIBLING = "sibling"


def _peer_copies(src_refs, land_refs, send_sems, recv_sems, scatter, landed):
    x, y, c = _place()
    if scatter == SIBLING:
        return [pltpu.make_async_remote_copy(
            src_ref=s, dst_ref=l, send_sem=send_sems.at[a], recv_sem=recv_sems.at[a],
            device_id=(x, y, 1 - c), device_id_type=MESH) for a, (s, l) in enumerate(zip(src_refs, land_refs))]
    cps = []
    for a, (src_ref, land_ref) in enumerate(zip(src_refs, land_refs)):
        for j, (px, py) in enumerate(_chip_peers(x, y)):
            src = src_ref.at[2 * px + py] if scatter else src_ref
            dst = land_ref.at[2 * px + py] if landed else land_ref.at[2 * x + y]
            cps.append(pltpu.make_async_remote_copy(
                src_ref=src, dst_ref=dst, send_sem=send_sems.at[3 * a + j], recv_sem=recv_sems.at[3 * a + j],
                device_id=(px, py, c), device_id_type=MESH))
    return cps


def _exchange_chips(srcs, scatter, name):
    n = len(srcs)

    def body(*refs):
        src_refs, land_refs = refs[:n], refs[n:2 * n]
        send_sems, recv_sems, local_sems = refs[2 * n:]
        x, y, _ = _place()
        me = 2 * x + y
        mine = [pltpu.make_async_copy(s.at[me] if scatter else s, l.at[me], local_sems.at[a])
                for a, (s, l) in enumerate(zip(src_refs, land_refs))]
        for cp in mine:
            cp.start()
        out = _peer_copies(src_refs, land_refs, send_sems, recv_sems, scatter, False)
        for cp in out:
            cp.start()
        for cp in _peer_copies(src_refs, land_refs, send_sems, recv_sems, scatter, True):
            cp.wait_recv()
        for cp in out:
            cp.wait_send()
        for cp in mine:
            cp.wait()

    anyspec = pl.BlockSpec(memory_space=pl.ANY)
    shapes = [SDS(s.shape if scatter else (N_CHIPS,) + s.shape, s.dtype) for s in srcs]
    return pl.pallas_call(
        body, name=name, out_shape=shapes, in_specs=[anyspec] * n, out_specs=[anyspec] * n,
        scratch_shapes=[pltpu.SemaphoreType.DMA((3 * n,)), pltpu.SemaphoreType.DMA((3 * n,)),
                        pltpu.SemaphoreType.DMA((n,))],
        compiler_params=pltpu.CompilerParams(vmem_limit_bytes=VMEM_LIMIT),
    )(*srcs)


_HBM = pl.BlockSpec(memory_space=pltpu.HBM)
_SEM = pl.BlockSpec(memory_space=pltpu.SEMAPHORE)
_EFFECT = pltpu.SideEffectType.DATAFLOW_SIDE_EFFECTING


def _exchange_begin(srcs, lands, scatter, name):
    n = len(srcs)
    nsem = n if scatter == SIBLING else 3 * n

    def body(*refs):
        src_refs, land_refs = refs[:n], refs[n:2 * n]
        send_sems, recv_sems = refs[2 * n:2 * n + 2]
        token = refs[-1]
        for cp in _peer_copies(src_refs, land_refs, send_sems, recv_sems, scatter, False):
            cp.start()
        token[...] = jnp.zeros_like(token)

    res = pl.pallas_call(
        body, name=name,
        out_shape=(pltpu.SemaphoreType.DMA((nsem,)), pltpu.SemaphoreType.DMA((nsem,)),
                   *[pltpu.HBM(s.shape, s.dtype) for s in srcs], *[pltpu.HBM(l.shape, l.dtype) for l in lands],
                   SDS((SUB, LANE), F32)),
        in_specs=[_HBM] * (2 * n), out_specs=(_SEM, _SEM, *[_HBM] * (2 * n), pl.BlockSpec(memory_space=pltpu.VMEM)),
        input_output_aliases={i: 2 + i for i in range(2 * n)},
        compiler_params=pltpu.CompilerParams(has_side_effects=_EFFECT),
    )(*[pltpu.with_memory_space_constraint(a, pltpu.HBM) for a in (*srcs, *lands)])
    return res[0], res[1], res[2:2 + n], res[2 + n:2 + 2 * n], res[-1]


def _exchange_end(handle, after, scatter, name, with_srcs=False):
    send_sems, recv_sems, srcs, lands, _ = handle
    n = len(srcs)

    def body(*refs):
        src_refs, land_refs = refs[:n], refs[n:2 * n]
        send_sems, recv_sems = refs[2 * n:2 * n + 2]
        for cp in _peer_copies(src_refs, land_refs, send_sems, recv_sems, scatter, True):
            cp.wait_send()
            cp.wait_recv()

    res = pl.pallas_call(
        body, name=name,
        out_shape=tuple(pltpu.HBM(a.shape, a.dtype) for a in (*srcs, *lands)),
        in_specs=[_HBM] * (2 * n) + [_SEM, _SEM, pl.BlockSpec(memory_space=pl.ANY)], out_specs=tuple([_HBM] * (2 * n)),
        input_output_aliases={i: i for i in range(2 * n)},
        compiler_params=pltpu.CompilerParams(has_side_effects=_EFFECT),
    )(*srcs, *lands, send_sems, recv_sems, after)
    return (list(res[:n]), list(res[n:])) if with_srcs else list(res[n:])


def _own_slab(stack4, chip):
    idx = lax.broadcasted_iota(jnp.int32, (N_CHIPS,) + (1,) * (stack4.ndim - 1), 0)
    return jnp.where(idx == chip, stack4, jnp.zeros((), stack4.dtype))


def _swap_sibling(vs, name):
    n = len(vs)

    def body(*refs):
        in_refs, out_refs = refs[:n], refs[n:2 * n]
        send_sems, recv_sems = refs[2 * n:]
        x, y, c = _place()
        cps = [pltpu.make_async_remote_copy(
            src_ref=i, dst_ref=o, send_sem=send_sems.at[a], recv_sem=recv_sems.at[a],
            device_id=(x, y, 1 - c), device_id_type=MESH) for a, (i, o) in enumerate(zip(in_refs, out_refs))]
        for cp in cps:
            cp.start()
        for cp in cps:
            cp.wait()

    anyspec = pl.BlockSpec(memory_space=pl.ANY)
    return pl.pallas_call(
        body, name=name, out_shape=[SDS(v.shape, v.dtype) for v in vs], in_specs=[anyspec] * n, out_specs=[anyspec] * n,
        scratch_shapes=[pltpu.SemaphoreType.DMA((n,)), pltpu.SemaphoreType.DMA((n,))],
        compiler_params=pltpu.CompilerParams(vmem_limit_bytes=VMEM_LIMIT),
    )(*vs)


def _select(stacked, idx):
    out = stacked[0]
    for j in range(1, stacked.shape[0]):
        out = jnp.where(idx == j, stacked[j], out)
    return out


BIG = (
    ("w1_a", DFF // 4, D, False), ("w3_a", DFF // 4, D, False), ("w2_a", DFF // 4, D, False),
    ("w_in", D, 5632 // 4, True), ("w_conv_out", CW // 4, D, False), ("w_glu", SW // 4, SW, False),
    ("w_ssm_out", SW, D // 4, True), ("w_out", D // 4, D, False),
    ("w1_b", DFF // 4, D, False), ("w3_b", DFF // 4, D, False), ("w2_b", DFF // 4, D, False),
)
TRANSPOSED = frozenset(("w1_a", "w3_a", "w1_b", "w3_b"))
PACK_COLS = 1024


def _view(a, name):
    return jnp.transpose(a, (0, 2, 1)) if name in TRANSPOSED else a


def _full_from_stacked(st, split_cols):
    _, rows, cols = st.shape
    if split_cols:
        return st.transpose(1, 0, 2).reshape(rows, N_CHIPS * cols)
    return st.reshape(N_CHIPS * rows, cols)


def _stacked_from_full(full, rows, cols, split_cols):
    if split_cols:
        return full.reshape(rows, N_CHIPS, cols).transpose(1, 0, 2)
    return full.reshape(N_CHIPS, rows, cols)


def _blockdiag(t):
    r = lax.broadcasted_iota(jnp.int32, (SW, GP), 0) // NH
    cidx = lax.broadcasted_iota(jnp.int32, (SW, GP), 1) // NP
    dense = jnp.where(r == cidx, jnp.tile(t, (NG, 1)), 0.0)
    ub, sb = SW // SSM_SUPER, GP // SSM_SUPER
    return jnp.concatenate([dense[s * ub:(s + 1) * ub, s * sb:(s + 1) * sb] for s in range(SSM_SUPER)], axis=0)


def _blockdiag_extract(acc):
    gs = NG // SSM_SUPER
    a = acc.reshape(NG, NH, gs, NP)
    sel = (lax.broadcasted_iota(jnp.int32, (NG, 1, gs, 1), 0) % gs) == lax.broadcasted_iota(jnp.int32, (NG, 1, gs, 1), 2)
    a = jnp.sum(jnp.where(sel, a, 0.0), axis=2)
    return a.transpose(1, 0, 2).reshape(NH, GP)


def _to_t(p):
    return p.transpose(2, 0, 1).reshape(NH, GP)


def _from_t(t):
    return t.reshape(NH, NG, NP).transpose(1, 2, 0)


def _c_to_t(p):
    return p.transpose(1, 0, 2).reshape(NH, GP)


def _c_from_t(t):
    return t.reshape(NH, NG, NP).transpose(1, 0, 2)


def _ffn_forward(x, g, sh, sc, gt, w1, w3, w2, tag):
    h = _norm_mod(x, g, sh, sc, f"{tag}_norm")
    if callable(w1):
        w1, w3 = w1(h)
    a, b, hid = _swiglu_up(h, w1, w3, f"{tag}_up")
    w2 = w2(hid) if callable(w2) else w2
    f, xo = _ffn_down(hid, w2, x, gt, f"{tag}_down")
    return xo, (x, h, a, b, hid, f), w2


def _ffn_backward(dxo, saved, g, sc, gt, w1, w3, w2, tag, emit=lambda key, gw: 0.0):
    x, h, a, b, hid, f = saved
    dfs, da, db, dgt = _ffn_bwd_hid(dxo, gt, f, a, b, w2, f"{tag}_bwd_hid")
    h2 = _flat(h)
    gw2 = _mm(_flat(hid), _flat(dfs), ta=True, name=f"{tag}_gw2")
    tok = emit("w2", gw2)
    gw1 = _mm(_flat(da), h2, ta=True, name=f"{tag}_gw1")
    tok = tok + emit("w1", gw1)
    gw3 = _mm(_flat(db), h2, ta=True, name=f"{tag}_gw3")
    tok = tok + emit("w3", gw3)
    dx, dsh, dsc, dg = _dh_norm_bwd([da, db], [w1, w3], x, g, sc + tok, dxo, f"{tag}_bwd_dh", transposed=True)
    return dx, (dsh, dsc, dgt, dg), (gw1, gw3, gw2)


def kernel(x, c, w_ada, b_ada, g_ffn1, w1_a, w3_a, w2_a, g_mix, w_in, conv_w, w_conv_out, a_re, a_im, b_re, b_im, c_re, c_im, log_dt, d_skip, w_glu, w_ssm_out, w_out, g_ffn2, w1_b, w3_b, w2_b, g_final, loss_target, m_w_ada, m_b_ada, m_g_ffn1, m_w1_a, m_w3_a, m_w2_a, m_g_mix, m_w_in, m_conv_w, m_w_conv_out, m_a_re, m_a_im, m_b_re, m_b_im, m_c_re, m_c_im, m_log_dt, m_d_skip, m_w_glu, m_w_ssm_out, m_w_out, m_g_ffn2, m_w1_b, m_w3_b, m_w2_b, m_g_final, v_w_ada, v_b_ada, v_g_ffn1, v_w1_a, v_w3_a, v_w2_a, v_g_mix, v_w_in, v_conv_w, v_w_conv_out, v_a_re, v_a_im, v_b_re, v_b_im, v_c_re, v_c_im, v_log_dt, v_d_skip, v_w_glu, v_w_ssm_out, v_w_out, v_g_ffn2, v_w1_b, v_w3_b, v_w2_b, v_g_final):
    args = dict(locals())
    names = ["w_ada", "b_ada", "g_ffn1", "w1_a", "w3_a", "w2_a", "g_mix", "w_in", "conv_w", "w_conv_out", "a_re",
             "a_im", "b_re", "b_im", "c_re", "c_im", "log_dt", "d_skip", "w_glu", "w_ssm_out", "w_out", "g_ffn2",
             "w1_b", "w3_b", "w2_b", "g_final"]
    bsz, seq, _ = x.shape
    mx, my, mc = _place()
    chip = 2 * mx + my
    dev = 4 * mx + 2 * my + mc

    groups = (BIG[:3], BIG[3:8], BIG[8:])
    wfull = {}

    def shards(grp):
        return [_view(args[n], n)[0] for n, _, _, _ in grp]

    def unpack_group(gathered, grp):
        for (n, _, _, split), st in zip(grp, gathered):
            wfull[n] = _full_from_stacked(st, split)

    up_grp, down_grp = groups[0][:2], groups[0][2:]

    nmod_shard = NMOD * D // N_CHIPS
    c_all = _all_gather8(c.reshape(SUB, -1), "gather_c").reshape(N_DEV * bsz, D)
    b_shard = _select(b_ada.reshape(N_CHIPS, 1, nmod_shard), chip)
    mod_shard = _ada_fwd(c_all, w_ada, b_shard)
    nb = N_DEV * bsz
    cw_pad = jnp.pad(conv_w[0], ((0, SUB - 3), (0, nmod_shard - CW // N_CHIPS)))
    mod_st = _exchange_chips([jnp.concatenate([mod_shard, cw_pad], axis=0)], False, "gather_mod")[0]
    mod_all = mod_st[:, :nb].transpose(1, 0, 2).reshape(N_DEV, bsz, NMOD * D)
    mod = _select(mod_all, dev)

    def gather_begin(raw, name):
        srcs = [a.astype(BF16) for a in raw]
        lands = [_own_slab(jnp.broadcast_to(s[None], (N_CHIPS,) + s.shape), chip) for s in srcs]
        return _exchange_begin(srcs, lands, False, name)

    up_raw, mod = lax.optimization_barrier((shards(up_grp), mod))
    up_handle = gather_begin(up_raw, "gather_w_ffn1_up_start")
    (down_raw, mix_raw, ffn2_raw), up_token = lax.optimization_barrier(
        ((shards(down_grp), shards(groups[1]), shards(groups[2])), up_handle[4][0:1, 0:1]))
    down_handle = gather_begin(down_raw, "gather_w_ffn1_down_start")
    mix_handle = gather_begin(mix_raw, "gather_w_mix_start")
    ffn2_handle = gather_begin(ffn2_raw, "gather_w_ffn2_start")
    start_tokens = up_token + down_handle[4][0:1, 0:1] + mix_handle[4][0:1, 0:1] + ffn2_handle[4][0:1, 0:1]

    sh1, sc1, gt1, sh2, sc2, gt2, sh3, sc3, gt3 = [mod[:, None, j * D:(j + 1) * D] for j in range(NMOD)]
    convw = mod_st[:, nb:nb + 3, :CW // N_CHIPS].transpose(1, 0, 2).reshape(3, CW)
    convw8 = jnp.pad(convw, ((0, SUB - 3), (0, 0)))

    are, aim = a_re.reshape(1, GP), a_im.reshape(1, GP)
    ldt = jnp.broadcast_to(log_dt.reshape(NG, 1), (NG, NP)).reshape(1, GP)
    bre_t, bim_t = _to_t(b_re[0]), _to_t(b_im[0])
    abr, abi, bbr_t, bbi_t = _ssm_disc(are, aim, ldt, bre_t, bim_t)
    wb = jnp.concatenate([_blockdiag(bbr_t), _blockdiag(bbi_t)], axis=1).astype(BF16)
    wct = jnp.concatenate([_blockdiag(_c_to_t(c_re[0])), _blockdiag(-_c_to_t(c_im[0]))], axis=1).astype(BF16)
    ar8 = jnp.broadcast_to(abr, (SUB, GP))
    ai8 = jnp.broadcast_to(abi, (SUB, GP))

    def late_w2a(hid):
        unpack_group(_exchange_end(down_handle, hid, False, "gather_w_ffn1_down_wait"), down_grp)
        return wfull["w2_a"]

    def late_w13a(h):
        unpack_group(_exchange_end(up_handle, h, False, "gather_w_ffn1_up_wait"), up_grp)
        return wfull["w1_a"], wfull["w3_a"]

    x1, ffn1_saved, _ = _ffn_forward(x, g_ffn1 + start_tokens, sh1, sc1, gt1, late_w13a, None, late_w2a, "ffn1")
    unpack_group(_exchange_end(mix_handle, x1, False, "gather_w_mix_wait"), groups[1])

    h2 = _norm_mod(x1, g_mix, sh2, sc2, "mix_norm")
    h2f = _flat(h2)
    win = wfull["w_in"]
    win1, winu, win3 = win[:, :3 * CW], win[:, 3 * CW:3 * CW + SW], win[:, 3 * CW + SW:]
    p1 = _mm(h2f, win1, out_dtype=BF16, name="mix_in1").reshape(bsz, seq, 3 * CW)
    u = _mm(h2f, winu, name="mix_inu").reshape(bsz, seq, SW)
    p3 = _mm(h2f, win3, out_dtype=BF16, name="mix_in3").reshape(bsz, seq, 2 * D)

    ya_in = _conv_fwd(p1, convw8, "conv_fwd")
    ya = _mm(_flat(ya_in), wfull["w_conv_out"], out_dtype=BF16, name="conv_out").reshape(bsz, seq, D)

    xs, ys = _ssm_fwd(u, wb, wct, ar8, ai8, "ssm_fwd")
    s0, z, s1, s2, yb = _ssm_post(ys, u, d_skip, wfull["w_glu"], wfull["w_ssm_out"], "ssm_post")

    merged, mix, x2 = _merge_out(ya, yb, p3, wfull["w_out"], x1, gt2, "merge_out")
    unpack_group(_exchange_end(ffn2_handle, x2, False, "gather_w_ffn2_wait"), groups[2])
    x3, ffn2_saved, _ = _ffn_forward(x2, g_ffn2, sh3, sc3, gt3, wfull["w1_b"], wfull["w3_b"], wfull["w2_b"], "ffn2")

    dx3, lossvec, dgfin = _final_loss(x3, g_final.reshape(1, D), loss_target, "final_loss")
    loss = lax.psum(jnp.sum(lossvec), ("x", "y", "c"))

    gfull = {}
    dx2, (dsh3, dsc3, dgt3, dg3), (gfull["w1_b"], gfull["w3_b"], gfull["w2_b"]) = _ffn_backward(
        dx3, ffn2_saved, g_ffn2, sc3, gt3, wfull["w1_b"], wfull["w3_b"], wfull["w2_b"], "ffn2")

    def stack_group(grp):
        return [_stacked_from_full(gfull[n], rows, cols, split).astype(BF16) for n, rows, cols, split in grp]

    st_ffn2 = stack_group(groups[2])
    h_ffn2 = _exchange_begin(st_ffn2, [_own_slab(s, chip) for s in st_ffn2], True, "scatter_ffn2_start")

    dmix, dya, dyb, dp3, dgt2 = _merge_bwd(dx2, gt2 + h_ffn2[4][0, 0], mix, ya, yb, p3, wfull["w_out"], "merge_bwd")
    gfull["w_out"] = _mm(_flat(merged), _flat(dmix), ta=True, name="gw_out")
    dp1, dconvw8 = _conv_bwd(dya, wfull["w_conv_out"], p1, convw8, "conv_bwd")
    gfull["w_conv_out"] = _mm(_flat(ya_in), _flat(dya), ta=True, name="gw_conv_out")
    ds0, dz, ddskip = _ssm_post_bwd(dyb, s0, z, u, d_skip, wfull["w_glu"], wfull["w_ssm_out"], "ssm_post_bwd")
    gfull["w_ssm_out"] = _mm(_flat(s2), _flat(dyb), ta=True, name="gw_ssm_out")
    gfull["w_glu"] = _mm(_flat(s1), _flat(dz), ta=True, name="gw_glu")
    du, dwb, dwct, dar8, dai8 = _ssm_bwd(ds0, u, xs, wb, wct, ar8, ai8, d_skip, "ssm_bwd")
    dx1, dsh2, dsc2, dgmix = _dh_norm_bwd([dp1, du, dp3], [win1, winu, win3], x1, g_mix, sc2, dx2, "mix_bwd_dh")
    gfull["w_in"] = jnp.concatenate([
        _mm(h2f, _flat(dp1), ta=True, name="gw_in1"), _mm(h2f, _flat(du), ta=True, name="gw_inu"),
        _mm(h2f, _flat(dp3), ta=True, name="gw_in3")], axis=1)

    st_mix = stack_group(groups[1])
    h_mix = _exchange_begin(st_mix, [_own_slab(s, chip) for s in st_mix], True, "scatter_mix_start")

    def swap_begin(recv, name):
        return _exchange_begin(recv, [lax.empty(v.shape, v.dtype) for v in recv], SIBLING, name)

    recv_ffn2 = _exchange_end(h_ffn2, dx1, True, "scatter_ffn2_wait")
    sw_ffn2 = swap_begin(recv_ffn2, "swap_ffn2_start")

    ffn1_names = {"w1": BIG[0], "w3": BIG[1], "w2": BIG[2]}
    ffn1_handles = {}

    def emit_ffn1(key, gw):
        n, rows, cols, split = ffn1_names[key]
        st = _stacked_from_full(gw, rows, cols, split).astype(BF16)
        ffn1_handles[n] = _exchange_begin([st], [_own_slab(st, chip)], True, f"scatter_ffn1_{key}_start")
        return ffn1_handles[n][4][0, 0]

    grad_x, (dsh1, dsc1, dgt1, dg1), _ = _ffn_backward(
        dx1, ffn1_saved, g_ffn1, sc1, gt1 + (h_mix[4][0, 0] + sw_ffn2[4][0, 0]), wfull["w1_a"], wfull["w3_a"],
        wfull["w2_a"], "ffn1", emit=emit_ffn1)
    recv_mix = _exchange_end(h_mix, grad_x, True, "scatter_mix_wait")
    sw_mix = swap_begin(recv_mix, "swap_mix_start")
    dg1 = dg1 + sw_mix[4][0, 0]

    sbw = GP // SSM_SUPER
    d_are, d_aim, d_ldt, d_bre_t, d_bim_t = _ssm_disc_bwd(
        are, aim, ldt, bre_t, bim_t, jnp.sum(dar8, axis=0, keepdims=True), jnp.sum(dai8, axis=0, keepdims=True),
        _blockdiag_extract(dwb[:, :sbw]), _blockdiag_extract(dwb[:, sbw:]))
    d_cre = _c_from_t(_blockdiag_extract(dwct[:, :sbw]))
    d_cim = -_c_from_t(_blockdiag_extract(dwct[:, sbw:]))

    small_parts = [dg1, dgmix, dg3, dgfin, dconvw8[:3], d_are, d_aim, _from_t(d_bre_t), _from_t(d_bim_t), d_cre, d_cim,
                   jnp.sum(d_ldt.reshape(NG, NP), axis=1), ddskip]
    small_sizes = [int(p.size) for p in small_parts]
    n_small = sum(small_sizes)
    n_small_pad = -(-n_small // (SUB * PACK_COLS)) * (SUB * PACK_COLS)
    dmod = jnp.concatenate([dsh1, dsc1, dgt1, dsh2, dsc2, dgt2, dsh3, dsc3, dgt3], axis=2).reshape(bsz * NMOD * D)
    flat = jnp.concatenate([p.reshape(-1) for p in small_parts] + [jnp.zeros((n_small_pad - n_small,), F32), dmod])
    allg = _all_gather8(flat.reshape(SUB, -1), "gather_small").reshape(N_DEV, -1)
    small = _sum_slabs(allg[:, :n_small_pad].reshape(N_DEV, -1, PACK_COLS), "sum_small").reshape(-1)
    sg, o = [], 0
    for p, sz in zip(small_parts, small_sizes):
        sg.append(small[o:o + sz].reshape(p.shape))
        o += sz
    (g_g1, g_gmix, g_g3, g_gfin, g_convw, g_are, g_aim, g_bre, g_bim, g_cre, g_cim, g_ldt, g_dskip) = sg

    dmod_all = allg[:, n_small_pad:].reshape(nb, NMOD * D)
    dmod_shard = _select(dmod_all.reshape(nb, N_CHIPS, nmod_shard).transpose(1, 0, 2), chip)
    g_wada, g_bada = _ada_bwd(c_all, dmod_shard, dmod_all)

    recv_ffn1 = [_exchange_end(ffn1_handles[n], g_wada, True, f"scatter_ffn1_{n}_wait")[0] for n, _, _, _ in groups[0]]
    sib_ffn1 = _swap_sibling(recv_ffn1, "swap_ffn1")
    recv_mix, sib_mix = _exchange_end(sw_mix, sib_ffn1[0], SIBLING, "swap_mix_wait", with_srcs=True)
    recv_ffn2, sib_ffn2 = _exchange_end(sw_ffn2, sib_ffn1[0], SIBLING, "swap_ffn2_wait", with_srcs=True)
    recv = [*recv_ffn1, *recv_mix, *recv_ffn2]
    recv_sib = [*sib_ffn1, *sib_mix, *sib_ffn2]

    grads, deltas, new_m, new_v = {}, {}, {}, {}
    for (n, _, _, _), r_own, r_sib in zip(BIG, recv, recv_sib):
        res = _adamw_big(_view(args[n], n), _view(args["m_" + n], n), _view(args["v_" + n], n), r_own, r_sib,
                         f"adamw_{n}")
        grads[n], deltas[n], new_m[n], new_v[n] = [_view(r, n) for r in res]
    grads["w_ada"] = g_wada[None]
    deltas["w_ada"], new_m["w_ada"], new_v["w_ada"] = _adamw_rows(w_ada, m_w_ada, v_w_ada, g_wada, "adamw_w_ada")

    g_convw_shard = _select(g_convw.reshape(3, N_CHIPS, CW // N_CHIPS).transpose(1, 0, 2), chip)
    small_g = {"b_ada": g_bada, "g_ffn1": g_g1, "g_mix": g_gmix, "g_ffn2": g_g3, "g_final": g_gfin,
               "conv_w": g_convw_shard, "a_re": g_are, "a_im": g_aim, "b_re": g_bre, "b_im": g_bim,
               "c_re": g_cre, "c_im": g_cim, "log_dt": g_ldt, "d_skip": g_dskip}
    small_names = list(small_g)
    sizes = [int(args[n].size) for n in small_names]
    tot = sum(sizes)
    tot_pad = -(-tot // (SUB * PACK_COLS)) * (SUB * PACK_COLS)

    def pack(get):
        return jnp.concatenate([get(n).reshape(-1) for n in small_names] + [jnp.zeros((tot_pad - tot,), F32)]).reshape(
            -1, PACK_COLS)

    res = _adamw_plain(pack(lambda n: args[n]), pack(lambda n: args["m_" + n]), pack(lambda n: args["v_" + n]),
                       pack(lambda n: small_g[n]), "adamw_small")
    o = 0
    for n, sz in zip(small_names, sizes):
        shp = args[n].shape
        grads[n] = small_g[n].reshape(shp)
        deltas[n], new_m[n], new_v[n] = [r.reshape(-1)[o:o + sz].reshape(shp) for r in res]
        o += sz

    return (loss, grad_x, *[grads[n] for n in names], *[deltas[n] for n in names],
            *[new_m[n] for n in names], *[new_v[n] for n in names])
```

```python
import math

import jax
import jax.numpy as jnp
from jax import lax
from jax.experimental import pallas as pl
from jax.experimental.pallas import tpu as pltpu

F32 = jnp.float32
BF16 = jnp.bfloat16
SDS = jax.ShapeDtypeStruct
MESH = pl.DeviceIdType.MESH

D = 1024
DFF = 2816
CW = 1024
SW = 512
NG, NP, NH = 32, 64, 16
GP = NG * NP
NMOD = 9
EPS = 1e-6
N_CHIPS = 4
N_DEV = 8
SUB = 8
LANE = 128
SSM_SUPER = 4
VMEM_LIMIT = 50 * 1024 * 1024

LR, B1, B2, AEPS, WD, STEP = 0.001, 0.9, 0.999, 1e-08, 0.01, 10
BC1 = 1.0 - B1 ** STEP
BC2 = 1.0 - B2 ** STEP


def _cp(*sem):
    return pltpu.CompilerParams(dimension_semantics=sem or None, vmem_limit_bytes=VMEM_LIMIT)


def _pick_tile(n, cands):
    for t in cands:
        if t <= n and n % t == 0:
            return t
    return n


def _dot(a, b):
    return lax.dot_general(a, b, (((1,), (0,)), ((), ())), preferred_element_type=F32)


def _dot_nt(a, b):
    return lax.dot_general(a, b, (((1,), (1,)), ((), ())), preferred_element_type=F32)


def _dot_tn(a, b):
    return lax.dot_general(a, b, (((0,), (0,)), ((), ())), preferred_element_type=F32)


def _row(tm, width, col=0):
    return pl.BlockSpec((1, tm, width), lambda b, i, *_: (b, i, col))


def _seqvec(width):
    return pl.BlockSpec((1, 1, width), lambda b, *_: (b, 0, 0))


def _full2(shape):
    return pl.BlockSpec(shape, lambda *_: (0, 0))


def _sigmoid(x):
    return jax.nn.sigmoid(x)


def _mm(a, b, *, ta=False, tb=False, out_dtype=F32, name):
    if ta:
        kdim, m = a.shape
    else:
        m, kdim = a.shape
    n = b.shape[0] if tb else b.shape[1]
    tm = _pick_tile(m, (1408, 1024, 512, 256, 128))
    tn = _pick_tile(n, (1408, 1024, 512, 256, 128))
    tk = _pick_tile(kdim, (1024, 512, 256, 128))
    nk = kdim // tk

    def body(a_ref, b_ref, o_ref, acc_ref):
        k = pl.program_id(2)

        @pl.when(k == 0)
        def _():
            acc_ref[...] = jnp.zeros_like(acc_ref)

        av = a_ref[...].astype(BF16)
        bv = b_ref[...].astype(BF16)
        dn = (((0 if ta else 1,), (1 if tb else 0,)), ((), ()))
        acc_ref[...] += lax.dot_general(av, bv, dn, preferred_element_type=F32)

        @pl.when(k == nk - 1)
        def _():
            o_ref[...] = acc_ref[...].astype(out_dtype)

    a_spec = pl.BlockSpec((tk, tm), lambda i, j, k: (k, i)) if ta else pl.BlockSpec((tm, tk), lambda i, j, k: (i, k))
    b_spec = pl.BlockSpec((tn, tk), lambda i, j, k: (j, k)) if tb else pl.BlockSpec((tk, tn), lambda i, j, k: (k, j))
    return pl.pallas_call(
        body, name=name, grid=(m // tm, n // tn, nk),
        in_specs=[a_spec, b_spec],
        out_specs=pl.BlockSpec((tm, tn), lambda i, j, k: (i, j)),
        out_shape=SDS((m, n), out_dtype),
        scratch_shapes=[pltpu.VMEM((tm, tn), F32)],
        compiler_params=_cp("parallel", "parallel", "arbitrary"),
    )(a, b)


def _flat(a):
    return a.reshape(-1, a.shape[-1])


def _norm_mod(x, g, sh, sc, name):
    bsz, seq, dm = x.shape
    tm = _pick_tile(seq, (512, 256, 128))

    def body(x_ref, g_ref, sh_ref, sc_ref, o_ref):
        xf = x_ref[0]
        r = lax.rsqrt(jnp.mean(xf * xf, axis=-1, keepdims=True) + EPS)
        hn = xf * r * g_ref[...]
        o_ref[0] = (hn * (1.0 + sc_ref[0]) + sh_ref[0]).astype(BF16)

    return pl.pallas_call(
        body, name=name, grid=(bsz, seq // tm),
        in_specs=[_row(tm, dm), _full2((1, dm)), _seqvec(dm), _seqvec(dm)],
        out_specs=_row(tm, dm), out_shape=SDS((bsz, seq, dm), BF16),
        compiler_params=_cp("parallel", "parallel"),
    )(x, g, sh, sc)


def _swiglu_up(h, w1, w3, name):
    bsz, seq, dm = h.shape
    nf = w1.shape[0]
    tm = _pick_tile(seq, (512, 256, 128))
    tn = _pick_tile(nf, (1408, 512, 256, 128))

    def body(h_ref, w1_ref, w3_ref, a_ref, b_ref, hid_ref):
        hv = h_ref[0]
        a = _dot_nt(hv, w1_ref[...])
        b = _dot_nt(hv, w3_ref[...])
        sg = _sigmoid(a)
        sa = a * sg
        a_ref[0] = (b * (sg * (1.0 + a * (1.0 - sg)))).astype(BF16)
        b_ref[0] = sa.astype(BF16)
        hid_ref[0] = (sa * b).astype(BF16)

    wspec = pl.BlockSpec((tn, dm), lambda n, b, i: (n, 0))
    ospec = pl.BlockSpec((1, tm, tn), lambda n, b, i: (b, i, n))
    shp = SDS((bsz, seq, nf), BF16)
    return pl.pallas_call(
        body, name=name, grid=(nf // tn, bsz, seq // tm),
        in_specs=[pl.BlockSpec((1, tm, dm), lambda n, b, i: (b, i, 0)), wspec, wspec],
        out_specs=[ospec, ospec, ospec], out_shape=[shp, shp, shp],
        compiler_params=_cp("parallel", "parallel", "parallel"),
    )(h, w1, w3)


def _ffn_down(hid, w2, x, gt, name):
    bsz, seq, nf = hid.shape
    dm = w2.shape[1]
    tm = _pick_tile(seq, (512, 256, 128))

    def body(hid_ref, w2_ref, x_ref, gt_ref, f_ref, xo_ref):
        f = _dot(hid_ref[0], w2_ref[...])
        f_ref[0] = f.astype(BF16)
        xo_ref[0] = x_ref[0] + 0.5 * gt_ref[0] * f

    shp = SDS((bsz, seq, dm), F32)
    return pl.pallas_call(
        body, name=name, grid=(bsz, seq // tm),
        in_specs=[_row(tm, nf), _full2((nf, dm)), _row(tm, dm), _seqvec(dm)],
        out_specs=[_row(tm, dm), _row(tm, dm)], out_shape=[SDS((bsz, seq, dm), BF16), shp],
        compiler_params=_cp("parallel", "parallel"),
    )(hid, w2, x, gt)


def _ffn_bwd_hid(dxo, gt, f, a, b, w2, name):
    bsz, seq, dm = dxo.shape
    nf = a.shape[2]
    tm = _pick_tile(seq, (512, 256, 128))
    tn = _pick_tile(nf, (1408, 512, 256, 128))

    def body(dxo_ref, gt_ref, f_ref, a_ref, b_ref, w2_ref, dfs_ref, da_ref, db_ref, dgt_ref):
        i = pl.program_id(1)
        n = pl.program_id(2)

        @pl.when(n == 0)
        def _():
            dxo = dxo_ref[0]
            dfs_ref[0] = (0.5 * gt_ref[0] * dxo).astype(BF16)
            part = jnp.sum(0.5 * dxo * f_ref[0].astype(F32), axis=0, keepdims=True)

            @pl.when(i == 0)
            def _():
                dgt_ref[0] = part

            @pl.when(i > 0)
            def _():
                dgt_ref[0] += part

        dhid = _dot_nt(dfs_ref[0], w2_ref[pl.ds(pl.multiple_of(n * tn, tn), tn), :])
        dh16 = dhid.astype(BF16)
        da_ref[0] = dh16 * a_ref[0]
        db_ref[0] = dh16 * b_ref[0]

    hspec = pl.BlockSpec((1, tm, tn), lambda b, i, n: (b, i, n))
    return pl.pallas_call(
        body, name=name, grid=(bsz, seq // tm, nf // tn),
        in_specs=[_row(tm, dm), _seqvec(dm), _row(tm, dm), hspec, hspec, _full2((nf, dm))],
        out_specs=[_row(tm, dm), hspec, hspec, _seqvec(dm)],
        out_shape=[SDS((bsz, seq, dm), BF16), SDS((bsz, seq, nf), BF16), SDS((bsz, seq, nf), BF16),
                   SDS((bsz, 1, dm), F32)],
        compiler_params=_cp("arbitrary", "arbitrary", "arbitrary"),
    )(dxo, gt, f, a, b, w2)


def _dh_norm_bwd(pieces, weights, x, g, sc, dxo, name, transposed=False):
    bsz, seq, dm = x.shape
    tm = _pick_tile(seq, (512, 256, 128))
    npc = len(pieces)
    dot = _dot if transposed else _dot_nt

    def body(*refs):
        p_refs = refs[:npc]
        w_hbm = refs[npc:2 * npc]
        x_ref, g_ref, sc_ref, dxo_ref, dx_ref, dsh_ref, dsc_ref, dg_ref = refs[2 * npc:2 * npc + 8]
        w_refs = refs[2 * npc + 8:]
        b = pl.program_id(0)
        i = pl.program_id(1)

        @pl.when((i == 0) & (b == 0))
        def _():
            for src, dst in zip(w_hbm, w_refs):
                pltpu.sync_copy(src, dst)

        dh = dot(p_refs[0][0], w_refs[0][...])
        for j in range(1, npc):
            dh = dh + dot(p_refs[j][0], w_refs[j][...])
        xf = x_ref[0]
        gv = g_ref[...]
        r = lax.rsqrt(jnp.mean(xf * xf, axis=-1, keepdims=True) + EPS)
        xhat = xf * r
        dhn = dh * (1.0 + sc_ref[0])
        p_sh = jnp.sum(dh, axis=0, keepdims=True)
        p_sc = jnp.sum(dh * (xhat * gv), axis=0, keepdims=True)
        p_g = jnp.sum(dhn * xhat, axis=0, keepdims=True)
        dxh = dhn * gv
        dx_ref[0] = dxo_ref[0] + r * (dxh - xhat * jnp.mean(dxh * xhat, axis=-1, keepdims=True))

        @pl.when(i == 0)
        def _():
            dsh_ref[0] = p_sh
            dsc_ref[0] = p_sc

        @pl.when(i > 0)
        def _():
            dsh_ref[0] += p_sh
            dsc_ref[0] += p_sc

        @pl.when((i == 0) & (b == 0))
        def _():
            dg_ref[...] = p_g

        @pl.when((i > 0) | (b > 0))
        def _():
            dg_ref[...] += p_g

    return pl.pallas_call(
        body, name=name, grid=(bsz, seq // tm),
        in_specs=[_row(tm, p.shape[2]) for p in pieces] + [pl.BlockSpec(memory_space=pl.ANY)] * npc + [
            _row(tm, dm), _full2((1, dm)), _seqvec(dm), _row(tm, dm)],
        out_specs=[_row(tm, dm), _seqvec(dm), _seqvec(dm), _full2((1, dm))],
        out_shape=[SDS((bsz, seq, dm), F32), SDS((bsz, 1, dm), F32), SDS((bsz, 1, dm), F32), SDS((1, dm), F32)],
        scratch_shapes=[pltpu.VMEM(w.shape, w.dtype) for w in weights],
        compiler_params=_cp("arbitrary", "arbitrary"),
    )(*pieces, *weights, x, g, sc, dxo)


HALO = 16


def _conv_core(gc, v, gch, vh, w, first):
    cv = gc * v
    halo = jnp.where(first, 0.0, gch * vh)
    ext = jnp.concatenate([halo, cv], axis=0)
    cv1 = pltpu.roll(ext, 1, 0)[HALO:]
    cv2 = pltpu.roll(ext, 2, 0)[HALO:]
    conv = w[0:1] * cv2 + w[1:2] * cv1 + w[2:3] * cv
    return cv, cv1, cv2, conv


def _prev_halo(tm, col):
    return pl.BlockSpec((1, HALO, CW), lambda b, i, *_: (b, jnp.maximum(i * (tm // HALO) - 1, 0), col))


def _next_halo(tm, seq, col):
    return pl.BlockSpec((1, HALO, CW), lambda b, i, *_: (b, jnp.minimum((i + 1) * (tm // HALO), seq // HALO - 1), col))


def _conv_fwd(p1, convw8, name):
    bsz, seq, _ = p1.shape
    tm = _pick_tile(seq, (512, 256, 128))

    def body(gb_ref, gc_ref, v_ref, gch_ref, vh_ref, w_ref, o_ref):
        first = pl.program_id(1) == 0
        _, _, _, conv = _conv_core(gc_ref[0].astype(F32), v_ref[0].astype(F32), gch_ref[0].astype(F32),
                                   vh_ref[0].astype(F32), w_ref[...], first)
        o_ref[0] = (gb_ref[0].astype(F32) * conv).astype(BF16)

    return pl.pallas_call(
        body, name=name, grid=(bsz, seq // tm),
        in_specs=[_row(tm, CW, 0), _row(tm, CW, 1), _row(tm, CW, 2), _prev_halo(tm, 1), _prev_halo(tm, 2),
                  _full2((8, CW))],
        out_specs=_row(tm, CW), out_shape=SDS((bsz, seq, CW), BF16),
        compiler_params=_cp("parallel", "parallel"),
    )(p1, p1, p1, p1, p1, convw8)


def _conv_bwd(dya, wco, p1, convw8, name):
    bsz, seq, _ = p1.shape
    tm = _pick_tile(seq, (512, 256, 128))
    nt = seq // tm
    ext_rows = tm + HALO

    def body(dya_ref, dyan_ref, wco_ref, gb_ref, gbn_ref, gc_ref, v_ref, gch_ref, vh_ref, w_ref, dp_ref, dw_ref):
        b = pl.program_id(0)
        i = pl.program_id(1)
        w = w_ref[...]
        gc = gc_ref[0].astype(F32)
        vv = v_ref[0].astype(F32)
        cv, cv1, cv2, conv = _conv_core(gc, vv, gch_ref[0].astype(F32), vh_ref[0].astype(F32), w, i == 0)
        dya_ext = jnp.concatenate([dya_ref[0], dyan_ref[0]], axis=0)
        dyain_ext = _dot_nt(dya_ext, wco_ref[...])
        gb_ext = jnp.concatenate([gb_ref[0], gbn_ref[0]], axis=0).astype(F32)
        rows = lax.broadcasted_iota(jnp.int32, (ext_rows, 1), 0)
        dconv_ext = jnp.where((rows < tm) | (i < nt - 1), dyain_ext * gb_ext, 0.0)
        dconv = dconv_ext[:tm]
        dconv1 = pltpu.roll(dconv_ext, ext_rows - 1, 0)[:tm]
        dconv2 = pltpu.roll(dconv_ext, ext_rows - 2, 0)[:tm]
        dcv = w[2:3] * dconv + w[1:2] * dconv1 + w[0:1] * dconv2
        dp_ref[0, :, 0:CW] = (dyain_ext[:tm] * conv).astype(BF16)
        dp_ref[0, :, CW:2 * CW] = (dcv * vv).astype(BF16)
        dp_ref[0, :, 2 * CW:3 * CW] = (dcv * gc).astype(BF16)
        g0 = jnp.sum(dconv * cv2, axis=0, keepdims=True)
        g1 = jnp.sum(dconv * cv1, axis=0, keepdims=True)
        g2 = jnp.sum(dconv * cv, axis=0, keepdims=True)
        upd = jnp.concatenate([g0, g1, g2, jnp.zeros((5, CW), F32)], axis=0)

        @pl.when((i == 0) & (b == 0))
        def _():
            dw_ref[...] = upd

        @pl.when((i > 0) | (b > 0))
        def _():
            dw_ref[...] += upd

    return pl.pallas_call(
        body, name=name, grid=(bsz, seq // tm),
        in_specs=[_row(tm, CW), _next_halo(tm, seq, 0), _full2((CW, D)),
                  _row(tm, CW, 0), _next_halo(tm, seq, 0), _row(tm, CW, 1), _row(tm, CW, 2),
                  _prev_halo(tm, 1), _prev_halo(tm, 2), _full2((8, CW))],
        out_specs=[_row(tm, 3 * CW), _full2((8, CW))],
        out_shape=[SDS((bsz, seq, 3 * CW), BF16), SDS((8, CW), F32)],
        compiler_params=_cp("arbitrary", "arbitrary"),
    )(dya, dya, wco, p1, p1, p1, p1, p1, p1, convw8)


def _disc(are, aim, ldt, bre, bim):
    dt = jnp.exp(ldt)
    mag = jnp.exp(are * dt)
    ang = aim * dt
    abr = mag * jnp.cos(ang)
    abi = mag * jnp.sin(ang)
    nr = abr - 1.0
    den = are * are + aim * aim
    cr = (nr * are + abi * aim) / den
    ci = (abi * are - nr * aim) / den
    return abr, abi, cr * bre - ci * bim, cr * bim + ci * bre


def _ssm_disc(are, aim, ldt, bre_t, bim_t):
    def body(are_ref, aim_ref, ldt_ref, bre_ref, bim_ref, abr_ref, abi_ref, bbr_ref, bbi_ref):
        abr, abi, bbr, bbi = _disc(are_ref[...], aim_ref[...], ldt_ref[...], bre_ref[...], bim_ref[...])
        abr_ref[...] = abr
        abi_ref[...] = abi
        bbr_ref[...] = bbr
        bbi_ref[...] = bbi

    v1, vh = SDS((1, GP), F32), SDS((NH, GP), F32)
    return pl.pallas_call(body, name="ssm_disc", out_shape=[v1, v1, vh, vh], compiler_params=_cp())(
        are, aim, ldt, bre_t, bim_t)


def _ssm_disc_bwd(are, aim, ldt, bre_t, bim_t, dabr, dabi, dbbr, dbbi):
    def body(are_ref, aim_ref, ldt_ref, bre_ref, bim_ref, g0, g1, g2, g3, o0, o1, o2, o3, o4):
        prim = (are_ref[...], aim_ref[...], ldt_ref[...], bre_ref[...], bim_ref[...])
        _, vjp = jax.vjp(_disc, *prim)
        d_are, d_aim, d_ldt, d_bre, d_bim = vjp((g0[...], g1[...], g2[...], g3[...]))
        o0[...] = d_are
        o1[...] = d_aim
        o2[...] = d_ldt
        o3[...] = d_bre
        o4[...] = d_bim

    v1, vh = SDS((1, GP), F32), SDS((NH, GP), F32)
    return pl.pallas_call(body, name="ssm_disc_bwd", out_shape=[v1, v1, v1, vh, vh], compiler_params=_cp())(
        are, aim, ldt, bre_t, bim_t, dabr, dabi, dbbr, dbbi)


def _scan_chunk(buf_ref, nt, bsz, ar, ai, init_r, init_i, reverse):
    nsub = SUB // bsz
    row = lax.broadcasted_iota(jnp.int32, (SUB, GP), 0)
    shift = ((SUB - bsz) if reverse else bsz) % SUB
    order = list(range(nsub - 1, -1, -1)) if reverse else list(range(nsub))

    def step(j, carry):
        pr, pi = carry
        jj = (nt - 1 - j) if reverse else j
        off = pl.multiple_of(jj * SUB, SUB)
        br = buf_ref[pl.ds(off, SUB), 0:GP]
        bi = buf_ref[pl.ds(off, SUB), GP:2 * GP]
        nr, ni = pr, pi
        for s in order:
            sr, si = nr, ni
            if shift:
                sr = pltpu.roll(sr, shift, 0)
                si = pltpu.roll(si, shift, 0)
            cr = ar * sr - ai * si + br
            ci = ar * si + ai * sr + bi
            if nsub == 1:
                nr, ni = cr, ci
            else:
                m = (row >= s * bsz) & (row < (s + 1) * bsz)
                nr = jnp.where(m, cr, nr)
                ni = jnp.where(m, ci, ni)
        buf_ref[pl.ds(off, SUB), 0:GP] = nr
        buf_ref[pl.ds(off, SUB), GP:2 * GP] = ni
        return nr, ni

    return lax.fori_loop(0, nt, step, (init_r, init_i))


def _ssm_chunk_rows(total_rows, bsz):
    return min(total_rows, 64 * bsz)


def _interleave(src_ref, tmp_ref, bsz, steps):
    nl = tmp_ref.shape[0]
    for b in range(bsz):
        for j in range(nl):
            tmp_ref.at[j][pl.ds(b, steps, stride=bsz), :] = src_ref[b, :, j * LANE:(j + 1) * LANE]
    return jnp.concatenate([tmp_ref[j] for j in range(nl)], axis=1)


def _deinterleave(val, tmp_ref, dst_ref, bsz, steps, skip=None):
    nl = tmp_ref.shape[0]
    for j in range(nl):
        tmp_ref[j] = val[:, j * LANE:(j + 1) * LANE]
    for b in range(bsz):
        for j in range(nl):
            lanes = slice(j * LANE, (j + 1) * LANE)
            v = tmp_ref.at[j][pl.ds(b, steps, stride=bsz), :]
            if skip is not None:
                v = v + skip[0][b, :, lanes] * skip[1][:, lanes]
            dst_ref[b, :, lanes] = v.astype(dst_ref.dtype)


SSM_UB = SW // SSM_SUPER
SSM_SB = GP // SSM_SUPER


def _sb_cols(s, half):
    return slice(half * GP + s * SSM_SB, half * GP + (s + 1) * SSM_SB)


def _ssm_in(v16, w_ref, x_ref):
    for s in range(SSM_SUPER):
        vs = v16[:, s * SSM_UB:(s + 1) * SSM_UB]
        for half in range(2):
            x_ref[:, _sb_cols(s, half)] = _dot(vs, w_ref[s * SSM_UB:(s + 1) * SSM_UB, half * SSM_SB:(half + 1) * SSM_SB])


def _ssm_out(x16, w_ref):
    outs = []
    for s in range(SSM_SUPER):
        rows = slice(s * SSM_UB, (s + 1) * SSM_UB)
        outs.append(_dot_nt(x16[:, _sb_cols(s, 0)], w_ref[rows, 0:SSM_SB])
                    + _dot_nt(x16[:, _sb_cols(s, 1)], w_ref[rows, SSM_SB:2 * SSM_SB]))
    return jnp.concatenate(outs, axis=1)


def _ssm_fwd(u, wb, wct, ar8, ai8, name):
    bsz, seq, _ = u.shape
    rt = seq * bsz
    r = _ssm_chunk_rows(rt, bsz)
    nt = r // SUB
    steps = r // bsz

    def body(u_ref, wb_hbm, wct_hbm, ar_ref, ai_ref, x_ref, y_ref, wb_ref, wct_ref, st_ref, tmp_ref):
        @pl.when(pl.program_id(0) == 0)
        def _():
            pltpu.sync_copy(wb_hbm, wb_ref)
            pltpu.sync_copy(wct_hbm, wct_ref)
            st_ref[...] = jnp.zeros_like(st_ref)

        _ssm_in(_interleave(u_ref, tmp_ref, bsz, steps).astype(BF16), wb_ref, x_ref)
        fr, fi = _scan_chunk(x_ref, nt, bsz, ar_ref[...], ai_ref[...], st_ref[:, 0:GP], st_ref[:, GP:2 * GP], False)
        st_ref[:, 0:GP] = fr
        st_ref[:, GP:2 * GP] = fi
        _deinterleave(_ssm_out(x_ref[...].astype(BF16), wct_ref), tmp_ref, y_ref, bsz, steps)

    anyspec = pl.BlockSpec(memory_space=pl.ANY)
    seqs = pl.BlockSpec((bsz, steps, SW), lambda i: (0, i, 0))
    return pl.pallas_call(
        body, name=name, grid=(rt // r,),
        in_specs=[seqs, anyspec, anyspec, _full2((SUB, GP)), _full2((SUB, GP))],
        out_specs=[pl.BlockSpec((r, 2 * GP), lambda i: (i, 0)), seqs],
        out_shape=[SDS((rt, 2 * GP), F32), SDS((bsz, seq, SW), F32)],
        scratch_shapes=[pltpu.VMEM(wb.shape, BF16), pltpu.VMEM(wct.shape, BF16), pltpu.VMEM((SUB, 2 * GP), F32),
                        pltpu.VMEM((SW // LANE, r, LANE), F32)],
        compiler_params=_cp("arbitrary"),
    )(u, wb, wct, ar8, ai8)


def _ssm_bwd(dy, u, xs, wb, wct, ar8, ai8, dskip, name):
    bsz, seq, _ = u.shape
    rt = seq * bsz
    r = _ssm_chunk_rows(rt, bsz)
    nt = r // SUB
    nc = rt // r
    steps = r // bsz
    ub = SW // SSM_SUPER
    sb = GP // SSM_SUPER

    def body(dys_ref, us_ref, x_ref, xh_ref, wb_hbm, wct_hbm, ar_ref, ai_ref, d_ref,
             dus_ref, dwb_hbm, dwct_hbm, dar_ref, dai_ref, wb_ref, wct_ref, g_ref, st_ref, awb_ref, awct_ref,
             tmp_ref):
        i = pl.program_id(0)
        dyb = _interleave(dys_ref, tmp_ref, bsz, steps).astype(BF16)
        ub16 = _interleave(us_ref, tmp_ref, bsz, steps).astype(BF16)

        @pl.when(i == 0)
        def _():
            pltpu.sync_copy(wb_hbm, wb_ref)
            pltpu.sync_copy(wct_hbm, wct_ref)
            st_ref[...] = jnp.zeros_like(st_ref)
            awb_ref[...] = jnp.zeros_like(awb_ref)
            awct_ref[...] = jnp.zeros_like(awct_ref)
            dar_ref[...] = jnp.zeros_like(dar_ref)
            dai_ref[...] = jnp.zeros_like(dai_ref)

        _ssm_in(dyb, wct_ref, g_ref)
        ar = ar_ref[...]
        ai = ai_ref[...]
        fr, fi = _scan_chunk(g_ref, nt, bsz, ar, -ai, st_ref[:, 0:GP], st_ref[:, GP:2 * GP], True)
        st_ref[:, 0:GP] = fr
        st_ref[:, GP:2 * GP] = fi

        gb = g_ref[...].astype(BF16)
        _deinterleave(_ssm_out(gb, wb_ref), tmp_ref, dus_ref, bsz, steps, skip=(dys_ref, d_ref))
        xb16 = x_ref[...].astype(BF16)
        for s in range(SSM_SUPER):
            us = ub16[:, s * ub:(s + 1) * ub]
            ds = dyb[:, s * ub:(s + 1) * ub]
            for half in range(2):
                cols = slice(half * GP + s * sb, half * GP + (s + 1) * sb)
                ocols = slice(half * sb, (half + 1) * sb)
                awb_ref[s * ub:(s + 1) * ub, ocols] += _dot_tn(us, gb[:, cols])
                awct_ref[s * ub:(s + 1) * ub, ocols] += _dot_tn(ds, xb16[:, cols])

        gr = g_ref[:, 0:GP]
        gi = g_ref[:, GP:2 * GP]
        xsr = pltpu.roll(x_ref[:, 0:GP], bsz, 0)
        xsi = pltpu.roll(x_ref[:, GP:2 * GP], bsz, 0)
        inner = lax.broadcasted_iota(jnp.int32, (r, 1), 0) >= bsz
        t_r = jnp.where(inner, gr * xsr + gi * xsi, 0.0)
        t_i = jnp.where(inner, gi * xsr - gr * xsi, 0.0)
        acc_r = jnp.sum(t_r.reshape(nt, SUB, GP), axis=0)
        acc_i = jnp.sum(t_i.reshape(nt, SUB, GP), axis=0)
        hr = xh_ref[:, 0:GP]
        hi = xh_ref[:, GP:2 * GP]
        if bsz % SUB:
            hr = pltpu.roll(hr, bsz, 0)
            hi = pltpu.roll(hi, bsz, 0)
        edge = (lax.broadcasted_iota(jnp.int32, (SUB, 1), 0) < bsz) & (i < nc - 1)
        g0r = g_ref[0:SUB, 0:GP]
        g0i = g_ref[0:SUB, GP:2 * GP]
        dar_ref[...] += acc_r + jnp.where(edge, g0r * hr + g0i * hi, 0.0)
        dai_ref[...] += acc_i + jnp.where(edge, g0i * hr - g0r * hi, 0.0)

        @pl.when(i == nc - 1)
        def _():
            pltpu.sync_copy(awb_ref, dwb_hbm)
            pltpu.sync_copy(awct_ref, dwct_hbm)

    anyspec = pl.BlockSpec(memory_space=pl.ANY)
    rev = lambda i: (nc - 1 - i, 0)
    seqs = pl.BlockSpec((bsz, steps, SW), lambda i: (0, nc - 1 - i, 0))
    wshape = (SW, 2 * sb)
    return pl.pallas_call(
        body, name=name, grid=(nc,),
        in_specs=[seqs, seqs, pl.BlockSpec((r, 2 * GP), rev),
                  pl.BlockSpec((SUB, 2 * GP), lambda i: (jnp.maximum((nc - 1 - i) * nt - 1, 0), 0)),
                  anyspec, anyspec, _full2((SUB, GP)), _full2((SUB, GP)), _full2((1, SW))],
        out_specs=[seqs, anyspec, anyspec, _full2((SUB, GP)), _full2((SUB, GP))],
        out_shape=[SDS((bsz, seq, SW), BF16), SDS(wshape, F32), SDS(wshape, F32), SDS((SUB, GP), F32),
                   SDS((SUB, GP), F32)],
        scratch_shapes=[pltpu.VMEM(wb.shape, BF16), pltpu.VMEM(wct.shape, BF16),
                        pltpu.VMEM((r, 2 * GP), F32), pltpu.VMEM((SUB, 2 * GP), F32),
                        pltpu.VMEM(wshape, F32), pltpu.VMEM(wshape, F32),
                        pltpu.VMEM((SW // LANE, r, LANE), F32)],
        compiler_params=_cp("arbitrary"),
    )(dy, u, xs, xs, wb, wct, ar8, ai8, dskip)


GELU_C = math.sqrt(2.0 / math.pi)


def _gelu(x):
    return 0.5 * x * (1.0 + jnp.tanh(GELU_C * (x + 0.044715 * x * x * x)))


def _gelu_grad(x):
    th = jnp.tanh(GELU_C * (x + 0.044715 * x * x * x))
    return 0.5 * (1.0 + th) + 0.5 * x * (1.0 - th * th) * GELU_C * (1.0 + 3.0 * 0.044715 * x * x)


def _ssm_post(ys, u, dskip, wglu, wso, name):
    bsz, seq, _ = ys.shape
    tm = _pick_tile(seq, (512, 256, 128))

    def body(ys_ref, u_ref, d_ref, wg_ref, wo_ref, s0_ref, z_ref, s1_ref, s2_ref, yb_ref):
        s0 = ys_ref[0] + d_ref[...] * u_ref[0]
        s1 = _gelu(s0)
        s1b = s1.astype(BF16)
        z = _dot(s1b, wg_ref[...])
        s2b = (s1 * _sigmoid(z)).astype(BF16)
        s0_ref[0] = s0
        z_ref[0] = z
        s1_ref[0] = s1b
        s2_ref[0] = s2b
        yb_ref[0] = _dot(s2b, wo_ref[...]).astype(BF16)

    return pl.pallas_call(
        body, name=name, grid=(bsz, seq // tm),
        in_specs=[_row(tm, SW), _row(tm, SW), _full2((1, SW)), _full2((SW, SW)), _full2((SW, D))],
        out_specs=[_row(tm, SW), _row(tm, SW), _row(tm, SW), _row(tm, SW), _row(tm, D)],
        out_shape=[SDS((bsz, seq, SW), F32), SDS((bsz, seq, SW), F32), SDS((bsz, seq, SW), BF16),
                   SDS((bsz, seq, SW), BF16), SDS((bsz, seq, D), BF16)],
        compiler_params=_cp("parallel", "parallel"),
    )(ys, u, dskip, wglu, wso)


def _ssm_post_bwd(dyb, s0, z, u, dskip, wglu, wso, name):
    bsz, seq, _ = s0.shape
    tm = _pick_tile(seq, (512, 256, 128))

    def body(dyb_ref, s0_ref, z_ref, u_ref, wg_ref, wo_ref, ds0_ref, dz_ref, dd_ref):
        b = pl.program_id(0)
        i = pl.program_id(1)
        ds2 = _dot_nt(dyb_ref[0], wo_ref[...])
        s0 = s0_ref[0]
        s1 = _gelu(s0)
        sg = _sigmoid(z_ref[0])
        dz = ds2 * s1 * sg * (1.0 - sg)
        dzb = dz.astype(BF16)
        ds1 = ds2 * sg + _dot_nt(dzb, wg_ref[...])
        ds0 = ds1 * _gelu_grad(s0)
        ds0_ref[0] = ds0
        dz_ref[0] = dzb
        part = jnp.sum(ds0 * u_ref[0], axis=0, keepdims=True)

        @pl.when((i == 0) & (b == 0))
        def _():
            dd_ref[...] = part

        @pl.when((i > 0) | (b > 0))
        def _():
            dd_ref[...] += part

    del dskip
    return pl.pallas_call(
        body, name=name, grid=(bsz, seq // tm),
        in_specs=[_row(tm, D), _row(tm, SW), _row(tm, SW), _row(tm, SW), _full2((SW, SW)), _full2((SW, D))],
        out_specs=[_row(tm, SW), _row(tm, SW), _full2((1, SW))],
        out_shape=[SDS((bsz, seq, SW), F32), SDS((bsz, seq, SW), BF16), SDS((1, SW), F32)],
        compiler_params=_cp("arbitrary", "arbitrary"),
    )(dyb, s0, z, u, wglu, wso)


def _merge_out(ya, yb, p3, wout, x1, gt, name):
    bsz, seq, _ = ya.shape
    tm = _pick_tile(seq, (512, 256, 128))

    def body(ya_ref, yb_ref, ga_ref, gbb_ref, w_ref, x_ref, gt_ref, mg_ref, mix_ref, xo_ref):
        merged = (_sigmoid(ga_ref[0].astype(F32)) * ya_ref[0].astype(F32)
                  + _sigmoid(gbb_ref[0].astype(F32)) * yb_ref[0].astype(F32)).astype(BF16)
        mix = _dot(merged, w_ref[...])
        mg_ref[0] = merged
        mix_ref[0] = mix.astype(BF16)
        xo_ref[0] = x_ref[0] + gt_ref[0] * mix

    return pl.pallas_call(
        body, name=name, grid=(bsz, seq // tm),
        in_specs=[_row(tm, D), _row(tm, D), _row(tm, D, 0), _row(tm, D, 1), _full2((D, D)), _row(tm, D), _seqvec(D)],
        out_specs=[_row(tm, D), _row(tm, D), _row(tm, D)],
        out_shape=[SDS((bsz, seq, D), BF16), SDS((bsz, seq, D), BF16), SDS((bsz, seq, D), F32)],
        compiler_params=_cp("parallel", "parallel"),
    )(ya, yb, p3, p3, wout, x1, gt)


def _merge_bwd(dx2, gt, mix, ya, yb, p3, wout, name):
    bsz, seq, _ = ya.shape
    tm = _pick_tile(seq, (512, 256, 128))

    def body(dx_ref, gt_ref, mix_ref, ya_ref, yb_ref, ga_ref, gbb_ref, w_ref, dmix_ref, dya_ref, dyb_ref, dp_ref, dgt_ref):
        i = pl.program_id(1)
        dx = dx_ref[0]
        dmix = (gt_ref[0] * dx).astype(BF16)
        dmix_ref[0] = dmix
        part = jnp.sum(dx * mix_ref[0].astype(F32), axis=0, keepdims=True)

        @pl.when(i == 0)
        def _():
            dgt_ref[0] = part

        @pl.when(i > 0)
        def _():
            dgt_ref[0] += part

        dmg = _dot_nt(dmix, w_ref[...])
        sa = _sigmoid(ga_ref[0].astype(F32))
        sb = _sigmoid(gbb_ref[0].astype(F32))
        dya_ref[0] = (dmg * sa).astype(BF16)
        dyb_ref[0] = (dmg * sb).astype(BF16)
        dp_ref[0, :, 0:D] = (dmg * ya_ref[0].astype(F32) * sa * (1.0 - sa)).astype(BF16)
        dp_ref[0, :, D:2 * D] = (dmg * yb_ref[0].astype(F32) * sb * (1.0 - sb)).astype(BF16)

    bshape = SDS((bsz, seq, D), BF16)
    return pl.pallas_call(
        body, name=name, grid=(bsz, seq // tm),
        in_specs=[_row(tm, D), _seqvec(D), _row(tm, D), _row(tm, D), _row(tm, D), _row(tm, D, 0), _row(tm, D, 1),
                  _full2((D, D))],
        out_specs=[_row(tm, D), _row(tm, D), _row(tm, D), _row(tm, 2 * D), _seqvec(D)],
        out_shape=[bshape, bshape, bshape, SDS((bsz, seq, 2 * D), BF16), SDS((bsz, 1, D), F32)],
        compiler_params=_cp("arbitrary", "arbitrary"),
    )(dx2, gt, mix, ya, yb, p3, p3, wout)


def _final_loss(x3, gfin, target, name):
    bsz, seq, dm = x3.shape
    tm = _pick_tile(seq, (512, 256, 128))

    def body(x_ref, g_ref, t_ref, dx_ref, loss_ref, dg_ref):
        b = pl.program_id(0)
        i = pl.program_id(1)
        xf = x_ref[0]
        gv = g_ref[...]
        r = lax.rsqrt(jnp.mean(xf * xf, axis=-1, keepdims=True) + EPS)
        xhat = xf * r
        e = xhat * gv - t_ref[0]
        dy = e * (1.0 / dm)
        dxh = dy * gv
        dx_ref[0] = r * (dxh - xhat * jnp.mean(dxh * xhat, axis=-1, keepdims=True))
        p_l = jnp.sum(e * e, axis=0, keepdims=True) * (0.5 / dm)
        p_g = jnp.sum(dy * xhat, axis=0, keepdims=True)

        @pl.when((i == 0) & (b == 0))
        def _():
            loss_ref[...] = p_l
            dg_ref[...] = p_g

        @pl.when((i > 0) | (b > 0))
        def _():
            loss_ref[...] += p_l
            dg_ref[...] += p_g

    return pl.pallas_call(
        body, name=name, grid=(bsz, seq // tm),
        in_specs=[_row(tm, dm), _full2((1, dm)), _row(tm, dm)],
        out_specs=[_row(tm, dm), _full2((1, dm)), _full2((1, dm))],
        out_shape=[SDS((bsz, seq, dm), F32), SDS((1, dm), F32), SDS((1, dm), F32)],
        compiler_params=_cp("arbitrary", "arbitrary"),
    )(x3, gfin, target)


def _ada_fwd(c_all, w_shard, b_shard):
    nb = c_all.shape[0]
    n = w_shard.shape[2]

    def body(c_ref, w_ref, b_ref, o_ref):
        cv = c_ref[...]
        cond = (cv * _sigmoid(cv)).astype(BF16)
        o_ref[...] = _dot(cond, w_ref[0].astype(BF16)) + b_ref[...]

    return pl.pallas_call(body, name="ada_fwd", out_shape=SDS((nb, n), F32), compiler_params=_cp())(
        c_all, w_shard, b_shard)


def _ada_bwd(c_all, dmod_shard, dmod_all):
    n = dmod_shard.shape[1]

    def body(c_ref, ds_ref, da_ref, gw_ref, gb_ref):
        cv = c_ref[...]
        cond = (cv * _sigmoid(cv)).astype(BF16)
        gw_ref[...] = _dot_tn(cond, ds_ref[...].astype(BF16))
        gb_ref[...] = jnp.sum(da_ref[...], axis=0, keepdims=True)

    return pl.pallas_call(
        body, name="ada_bwd", out_shape=[SDS((D, n), F32), SDS((1, dmod_all.shape[1]), F32)], compiler_params=_cp(),
    )(c_all, dmod_shard, dmod_all)


def _adamw_math(w, g, m, v):
    m = B1 * m + (1.0 - B1) * g
    v = B2 * v + (1.0 - B2) * (g * g)
    delta = -LR * ((m / BC1) / (jnp.sqrt(v / BC2) + AEPS) + WD * w)
    return delta, m, v


def _adamw_big(w, m, v, recv_own, recv_sib, name):
    _, rows, cols = w.shape
    tr = _pick_tile(rows, tuple(t for t in (512, 256, 128, 64, 32, 16, 8) if t * cols <= 192 * 1024))

    def body(w_ref, m_ref, v_ref, a_ref, b_ref, g_ref, d_ref, mo_ref, vo_ref):
        def chip_sum(r):
            acc = r[0].astype(F32)
            for k in range(1, N_CHIPS):
                acc = acc + r[k].astype(F32)
            return acc

        g = chip_sum(a_ref) + chip_sum(b_ref)
        delta, mn, vn = _adamw_math(w_ref[0], g, m_ref[0], v_ref[0])
        g_ref[0] = g
        d_ref[0] = delta
        mo_ref[0] = mn
        vo_ref[0] = vn

    own = pl.BlockSpec((1, tr, cols), lambda i: (0, i, 0))
    rspec = pl.BlockSpec((N_CHIPS, tr, cols), lambda i: (0, i, 0))
    shp = SDS(w.shape, F32)
    return pl.pallas_call(
        body, name=name, grid=(rows // tr,),
        in_specs=[own, own, own, rspec, rspec], out_specs=[own, own, own, own], out_shape=[shp, shp, shp, shp],
        compiler_params=_cp("parallel"),
    )(w, m, v, recv_own, recv_sib)


def _adamw_plain(w, m, v, g, name):
    def body(w_ref, m_ref, v_ref, g_ref, d_ref, mo_ref, vo_ref):
        delta, mn, vn = _adamw_math(w_ref[...], g_ref[...], m_ref[...], v_ref[...])
        d_ref[...] = delta
        mo_ref[...] = mn
        vo_ref[...] = vn

    shp = SDS(w.shape, F32)
    return pl.pallas_call(body, name=name, out_shape=[shp, shp, shp], compiler_params=_cp())(w, m, v, g)


def _adamw_rows(w, m, v, g, name):
    _, rows, cols = w.shape
    tr = _pick_tile(rows, (128, 64, 32, 16, 8))

    def body(w_ref, m_ref, v_ref, g_ref, d_ref, mo_ref, vo_ref):
        delta, mn, vn = _adamw_math(w_ref[0], g_ref[...], m_ref[0], v_ref[0])
        d_ref[0] = delta
        mo_ref[0] = mn
        vo_ref[0] = vn

    spec = pl.BlockSpec((1, tr, cols), lambda i: (0, i, 0))
    shp = SDS(w.shape, F32)
    return pl.pallas_call(
        body, name=name, grid=(rows // tr,), in_specs=[spec] * 3 + [pl.BlockSpec((tr, cols), lambda i: (i, 0))],
        out_specs=[spec] * 3, out_shape=[shp] * 3, compiler_params=_cp("parallel"),
    )(w, m, v, g)


def _sum_slabs(r, name):
    n, rows, cols = r.shape
    tr = _pick_tile(rows, (256, 128, 64))

    def body(r_ref, o_ref):
        acc = r_ref[0].astype(F32)
        for j in range(1, n):
            acc = acc + r_ref[j].astype(F32)
        o_ref[...] = acc

    return pl.pallas_call(
        body, name=name, grid=(rows // tr,),
        in_specs=[pl.BlockSpec((n, tr, cols), lambda i: (0, i, 0))],
        out_specs=pl.BlockSpec((tr, cols), lambda i: (i, 0)), out_shape=SDS((rows, cols), F32),
        compiler_params=_cp("parallel"),
    )(r)


def _place():
    return lax.axis_index("x"), lax.axis_index("y"), lax.axis_index("c")


def _all_gather8(blk, name):
    m_per, n = blk.shape

    def body(x_ref, out_ref, send_sems, recv_sems, local_sem):
        x, y, c = _place()
        me, sibling = (x, y, c), (x, y, 1 - c)
        chips = [(1 - x, y), (x, 1 - y), (1 - x, 1 - y)]

        def rows(px, py, pc):
            return out_ref.at[pl.ds((4 * px + 2 * py + pc) * m_per, m_per), :]

        def copy(k, block, to, src=None):
            return pltpu.make_async_remote_copy(
                src_ref=rows(*block) if src is None else src, dst_ref=rows(*block),
                send_sem=send_sems.at[k], recv_sem=recv_sems.at[k], device_id=to, device_id_type=MESH)

        mine = pltpu.make_async_copy(x_ref, rows(*me), local_sem)
        mine.start()
        first = [copy(0, me, sibling, src=x_ref)]
        first += [copy(1 + j, me, (*chip, c), src=x_ref) for j, chip in enumerate(chips)]
        for cp in first:
            cp.start()
        passed = [copy(4 + j, (*chip, c), sibling) for j, chip in enumerate(chips)]
        for j, chip in enumerate(chips):
            copy(1 + j, (*chip, c), me).wait_recv()
            passed[j].start()
        copy(0, sibling, me).wait_recv()
        for j, chip in enumerate(chips):
            copy(4 + j, (*chip, 1 - c), me).wait_recv()
        for cp in first + passed:
            cp.wait_send()
        mine.wait()

    return pl.pallas_call(
        body, name=name, out_shape=SDS((N_DEV * m_per, n), blk.dtype),
        in_specs=[pl.BlockSpec(memory_space=pltpu.VMEM)], out_specs=pl.BlockSpec(memory_space=pltpu.VMEM),
        scratch_shapes=[pltpu.SemaphoreType.DMA((7,)), pltpu.SemaphoreType.DMA((7,)), pltpu.SemaphoreType.DMA],
        compiler_params=pltpu.CompilerParams(vmem_limit_bytes=VMEM_LIMIT),
    )(blk)


def _chip_peers(x, y):
    return [(1 - x, y), (x, 1 - y), (1 - x, 1 - y)]


SIBLING = "sibling"


def _peer_copies(src_refs, land_refs, send_sems, recv_sems, scatter, landed):
    x, y, c = _place()
    if scatter == SIBLING:
        return [pltpu.make_async_remote_copy(
            src_ref=s, dst_ref=l, send_sem=send_sems.at[a], recv_sem=recv_sems.at[a],
            device_id=(x, y, 1 - c), device_id_type=MESH) for a, (s, l) in enumerate(zip(src_refs, land_refs))]
    cps = []
    for a, (src_ref, land_ref) in enumerate(zip(src_refs, land_refs)):
        for j, (px, py) in enumerate(_chip_peers(x, y)):
            if scatter:
                src = src_ref.at[2 * px + py]
            else:
                src = land_ref.at[2 * x + y] if src_ref is None else src_ref
            dst = land_ref.at[2 * px + py] if landed else land_ref.at[2 * x + y]
            cps.append(pltpu.make_async_remote_copy(
                src_ref=src, dst_ref=dst, send_sem=send_sems.at[3 * a + j], recv_sem=recv_sems.at[3 * a + j],
                device_id=(px, py, c), device_id_type=MESH))
    return cps


def _exchange_chips(srcs, scatter, name):
    n = len(srcs)

    def body(*refs):
        src_refs, land_refs = refs[:n], refs[n:2 * n]
        send_sems, recv_sems, local_sems = refs[2 * n:]
        x, y, _ = _place()
        me = 2 * x + y
        mine = [pltpu.make_async_copy(s.at[me] if scatter else s, l.at[me], local_sems.at[a])
                for a, (s, l) in enumerate(zip(src_refs, land_refs))]
        for cp in mine:
            cp.start()
        out = _peer_copies(src_refs, land_refs, send_sems, recv_sems, scatter, False)
        for cp in out:
            cp.start()
        for cp in _peer_copies(src_refs, land_refs, send_sems, recv_sems, scatter, True):
            cp.wait_recv()
        for cp in out:
            cp.wait_send()
        for cp in mine:
            cp.wait()

    anyspec = pl.BlockSpec(memory_space=pl.ANY)
    shapes = [SDS(s.shape if scatter else (N_CHIPS,) + s.shape, s.dtype) for s in srcs]
    return pl.pallas_call(
        body, name=name, out_shape=shapes, in_specs=[anyspec] * n, out_specs=[anyspec] * n,
        scratch_shapes=[pltpu.SemaphoreType.DMA((3 * n,)), pltpu.SemaphoreType.DMA((3 * n,)),
                        pltpu.SemaphoreType.DMA((n,))],
        compiler_params=pltpu.CompilerParams(vmem_limit_bytes=VMEM_LIMIT),
    )(*srcs)


_HBM = pl.BlockSpec(memory_space=pltpu.HBM)
_SEM = pl.BlockSpec(memory_space=pltpu.SEMAPHORE)
_EFFECT = pltpu.SideEffectType.DATAFLOW_SIDE_EFFECTING


def _exchange_begin(srcs, lands, scatter, name):
    srcs = tuple(srcs or ())
    n, ns = len(lands), len(srcs)
    nsem = n if scatter == SIBLING else 3 * n

    def body(*refs):
        src_refs = refs[:ns] if ns else (None,) * n
        land_refs = refs[ns:ns + n]
        send_sems, recv_sems = refs[ns + n:ns + n + 2]
        token = refs[-1]
        for cp in _peer_copies(src_refs, land_refs, send_sems, recv_sems, scatter, False):
            cp.start()
        token[...] = jnp.zeros_like(token)

    ops = (*srcs, *lands)
    res = pl.pallas_call(
        body, name=name,
        out_shape=(pltpu.SemaphoreType.DMA((nsem,)), pltpu.SemaphoreType.DMA((nsem,)),
                   *[pltpu.HBM(a.shape, a.dtype) for a in ops], SDS((SUB, LANE), F32)),
        in_specs=[_HBM] * len(ops), out_specs=(_SEM, _SEM, *[_HBM] * len(ops), pl.BlockSpec(memory_space=pltpu.VMEM)),
        input_output_aliases={i: 2 + i for i in range(len(ops))},
        compiler_params=pltpu.CompilerParams(has_side_effects=_EFFECT),
    )(*[pltpu.with_memory_space_constraint(a, pltpu.HBM) for a in ops])
    return res[0], res[1], res[2:2 + ns], res[2 + ns:2 + ns + n], res[-1]


def _exchange_end(handle, after, scatter, name, with_srcs=False):
    send_sems, recv_sems, srcs, lands, _ = handle
    n, ns = len(lands), len(srcs)

    def body(*refs):
        src_refs = refs[:ns] if ns else (None,) * n
        land_refs = refs[ns:ns + n]
        send_sems, recv_sems = refs[ns + n:ns + n + 2]
        for cp in _peer_copies(src_refs, land_refs, send_sems, recv_sems, scatter, True):
            cp.wait_send()
            cp.wait_recv()

    ops = (*srcs, *lands)
    res = pl.pallas_call(
        body, name=name,
        out_shape=tuple(pltpu.HBM(a.shape, a.dtype) for a in ops),
        in_specs=[_HBM] * len(ops) + [_SEM, _SEM, pl.BlockSpec(memory_space=pl.ANY)], out_specs=tuple([_HBM] * len(ops)),
        input_output_aliases={i: i for i in range(len(ops))},
        compiler_params=pltpu.CompilerParams(has_side_effects=_EFFECT),
    )(*ops, send_sems, recv_sems, after)
    return (list(res[:ns]), list(res[ns:])) if with_srcs else list(res[ns:])


def _own_slab(stack4, chip):
    idx = lax.broadcasted_iota(jnp.int32, (N_CHIPS,) + (1,) * (stack4.ndim - 1), 0)
    return jnp.where(idx == chip, stack4, jnp.zeros((), stack4.dtype))


def _swap_sibling(vs, name):
    n = len(vs)

    def body(*refs):
        in_refs, out_refs = refs[:n], refs[n:2 * n]
        send_sems, recv_sems = refs[2 * n:]
        x, y, c = _place()
        cps = [pltpu.make_async_remote_copy(
            src_ref=i, dst_ref=o, send_sem=send_sems.at[a], recv_sem=recv_sems.at[a],
            device_id=(x, y, 1 - c), device_id_type=MESH) for a, (i, o) in enumerate(zip(in_refs, out_refs))]
        for cp in cps:
            cp.start()
        for cp in cps:
            cp.wait()

    anyspec = pl.BlockSpec(memory_space=pl.ANY)
    return pl.pallas_call(
        body, name=name, out_shape=[SDS(v.shape, v.dtype) for v in vs], in_specs=[anyspec] * n, out_specs=[anyspec] * n,
        scratch_shapes=[pltpu.SemaphoreType.DMA((n,)), pltpu.SemaphoreType.DMA((n,))],
        compiler_params=pltpu.CompilerParams(vmem_limit_bytes=VMEM_LIMIT),
    )(*vs)


def _select(stacked, idx):
    out = stacked[0]
    for j in range(1, stacked.shape[0]):
        out = jnp.where(idx == j, stacked[j], out)
    return out


BIG = (
    ("w1_a", DFF // 4, D, False), ("w3_a", DFF // 4, D, False), ("w2_a", DFF // 4, D, False),
    ("w_in", D, 5632 // 4, True), ("w_conv_out", CW // 4, D, False), ("w_glu", SW // 4, SW, False),
    ("w_ssm_out", SW, D // 4, True), ("w_out", D // 4, D, False),
    ("w1_b", DFF // 4, D, False), ("w3_b", DFF // 4, D, False), ("w2_b", DFF // 4, D, False),
)
TRANSPOSED = frozenset(("w1_a", "w3_a", "w1_b", "w3_b"))
PACK_COLS = 1024


def _view(a, name):
    return jnp.transpose(a, (0, 2, 1)) if name in TRANSPOSED else a


def _full_from_stacked(st, split_cols):
    _, rows, cols = st.shape
    if split_cols:
        return st.transpose(1, 0, 2).reshape(rows, N_CHIPS * cols)
    return st.reshape(N_CHIPS * rows, cols)


def _stacked_from_full(full, rows, cols, split_cols):
    if split_cols:
        return full.reshape(rows, N_CHIPS, cols).transpose(1, 0, 2)
    return full.reshape(N_CHIPS, rows, cols)


def _blockdiag(t):
    r = lax.broadcasted_iota(jnp.int32, (SW, GP), 0) // NH
    cidx = lax.broadcasted_iota(jnp.int32, (SW, GP), 1) // NP
    dense = jnp.where(r == cidx, jnp.tile(t, (NG, 1)), 0.0)
    ub, sb = SW // SSM_SUPER, GP // SSM_SUPER
    return jnp.concatenate([dense[s * ub:(s + 1) * ub, s * sb:(s + 1) * sb] for s in range(SSM_SUPER)], axis=0)


def _blockdiag_extract(acc):
    gs = NG // SSM_SUPER
    a = acc.reshape(NG, NH, gs, NP)
    sel = (lax.broadcasted_iota(jnp.int32, (NG, 1, gs, 1), 0) % gs) == lax.broadcasted_iota(jnp.int32, (NG, 1, gs, 1), 2)
    a = jnp.sum(jnp.where(sel, a, 0.0), axis=2)
    return a.transpose(1, 0, 2).reshape(NH, GP)


def _to_t(p):
    return p.transpose(2, 0, 1).reshape(NH, GP)


def _from_t(t):
    return t.reshape(NH, NG, NP).transpose(1, 2, 0)


def _c_to_t(p):
    return p.transpose(1, 0, 2).reshape(NH, GP)


def _c_from_t(t):
    return t.reshape(NH, NG, NP).transpose(1, 0, 2)


def _ffn_forward(x, g, sh, sc, gt, w1, w3, w2, tag):
    h = _norm_mod(x, g, sh, sc, f"{tag}_norm")
    if callable(w1):
        w1, w3 = w1(h)
    a, b, hid = _swiglu_up(h, w1, w3, f"{tag}_up")
    w2 = w2(hid) if callable(w2) else w2
    f, xo = _ffn_down(hid, w2, x, gt, f"{tag}_down")
    return xo, (x, h, a, b, hid, f), w2


def _ffn_backward(dxo, saved, g, sc, gt, w1, w3, w2, tag, emit=lambda key, gw: 0.0):
    x, h, a, b, hid, f = saved
    dfs, da, db, dgt = _ffn_bwd_hid(dxo, gt, f, a, b, w2, f"{tag}_bwd_hid")
    h2 = _flat(h)
    gw2 = _mm(_flat(hid), _flat(dfs), ta=True, name=f"{tag}_gw2")
    tok = emit("w2", gw2)
    gw1 = _mm(_flat(da), h2, ta=True, name=f"{tag}_gw1")
    tok = tok + emit("w1", gw1)
    gw3 = _mm(_flat(db), h2, ta=True, name=f"{tag}_gw3")
    tok = tok + emit("w3", gw3)
    dx, dsh, dsc, dg = _dh_norm_bwd([da, db], [w1, w3], x, g, sc + tok, dxo, f"{tag}_bwd_dh", transposed=True)
    return dx, (dsh, dsc, dgt, dg), (gw1, gw3, gw2)


def kernel(x, c, w_ada, b_ada, g_ffn1, w1_a, w3_a, w2_a, g_mix, w_in, conv_w, w_conv_out, a_re, a_im, b_re, b_im, c_re, c_im, log_dt, d_skip, w_glu, w_ssm_out, w_out, g_ffn2, w1_b, w3_b, w2_b, g_final, loss_target, m_w_ada, m_b_ada, m_g_ffn1, m_w1_a, m_w3_a, m_w2_a, m_g_mix, m_w_in, m_conv_w, m_w_conv_out, m_a_re, m_a_im, m_b_re, m_b_im, m_c_re, m_c_im, m_log_dt, m_d_skip, m_w_glu, m_w_ssm_out, m_w_out, m_g_ffn2, m_w1_b, m_w3_b, m_w2_b, m_g_final, v_w_ada, v_b_ada, v_g_ffn1, v_w1_a, v_w3_a, v_w2_a, v_g_mix, v_w_in, v_conv_w, v_w_conv_out, v_a_re, v_a_im, v_b_re, v_b_im, v_c_re, v_c_im, v_log_dt, v_d_skip, v_w_glu, v_w_ssm_out, v_w_out, v_g_ffn2, v_w1_b, v_w3_b, v_w2_b, v_g_final):
    args = dict(locals())
    names = ["w_ada", "b_ada", "g_ffn1", "w1_a", "w3_a", "w2_a", "g_mix", "w_in", "conv_w", "w_conv_out", "a_re",
             "a_im", "b_re", "b_im", "c_re", "c_im", "log_dt", "d_skip", "w_glu", "w_ssm_out", "w_out", "g_ffn2",
             "w1_b", "w3_b", "w2_b", "g_final"]
    bsz, seq, _ = x.shape
    mx, my, mc = _place()
    chip = 2 * mx + my
    dev = 4 * mx + 2 * my + mc

    groups = (BIG[:3], BIG[3:8], BIG[8:])
    wfull = {}

    def shards(grp):
        return [_view(args[n], n)[0] for n, _, _, _ in grp]

    def unpack_group(gathered, grp):
        for (n, _, _, split), st in zip(grp, gathered):
            wfull[n] = _full_from_stacked(st, split)

    up_grp, down_grp = groups[0][:2], groups[0][2:]

    nmod_shard = NMOD * D // N_CHIPS
    c_all = _all_gather8(c.reshape(SUB, -1), "gather_c").reshape(N_DEV * bsz, D)
    b_shard = _select(b_ada.reshape(N_CHIPS, 1, nmod_shard), chip)
    mod_shard = _ada_fwd(c_all, w_ada, b_shard)
    nb = N_DEV * bsz
    cw_pad = jnp.pad(conv_w[0], ((0, SUB - 3), (0, nmod_shard - CW // N_CHIPS)))
    mod_st = _exchange_chips([jnp.concatenate([mod_shard, cw_pad], axis=0)], False, "gather_mod")[0]
    mod_all = mod_st[:, :nb].transpose(1, 0, 2).reshape(N_DEV, bsz, NMOD * D)
    mod = _select(mod_all, dev)

    def gather_begin(raw, name):
        lands = [_own_slab(jnp.broadcast_to(a.astype(BF16)[None], (N_CHIPS,) + a.shape), chip) for a in raw]
        return _exchange_begin(None, lands, False, name)

    up_raw, mod = lax.optimization_barrier((shards(up_grp), mod))
    up_handle = gather_begin(up_raw, "gather_w_ffn1_up_start")
    (down_raw, mix_raw, ffn2_raw), up_token = lax.optimization_barrier(
        ((shards(down_grp), shards(groups[1]), shards(groups[2])), up_handle[4][0:1, 0:1]))
    down_handle = gather_begin(down_raw, "gather_w_ffn1_down_start")
    mix_handle = gather_begin(mix_raw, "gather_w_mix_start")
    ffn2_handle = gather_begin(ffn2_raw, "gather_w_ffn2_start")
    start_tokens = up_token + down_handle[4][0:1, 0:1] + mix_handle[4][0:1, 0:1] + ffn2_handle[4][0:1, 0:1]

    sh1, sc1, gt1, sh2, sc2, gt2, sh3, sc3, gt3 = [mod[:, None, j * D:(j + 1) * D] for j in range(NMOD)]
    convw = mod_st[:, nb:nb + 3, :CW // N_CHIPS].transpose(1, 0, 2).reshape(3, CW)
    convw8 = jnp.pad(convw, ((0, SUB - 3), (0, 0)))

    are, aim = a_re.reshape(1, GP), a_im.reshape(1, GP)
    ldt = jnp.broadcast_to(log_dt.reshape(NG, 1), (NG, NP)).reshape(1, GP)
    bre_t, bim_t = _to_t(b_re[0]), _to_t(b_im[0])
    abr, abi, bbr_t, bbi_t = _ssm_disc(are, aim, ldt, bre_t, bim_t)
    wb = jnp.concatenate([_blockdiag(bbr_t), _blockdiag(bbi_t)], axis=1).astype(BF16)
    wct = jnp.concatenate([_blockdiag(_c_to_t(c_re[0])), _blockdiag(-_c_to_t(c_im[0]))], axis=1).astype(BF16)
    ar8 = jnp.broadcast_to(abr, (SUB, GP))
    ai8 = jnp.broadcast_to(abi, (SUB, GP))

    def late_w2a(hid):
        unpack_group(_exchange_end(down_handle, hid, False, "gather_w_ffn1_down_wait"), down_grp)
        return wfull["w2_a"]

    def late_w13a(h):
        unpack_group(_exchange_end(up_handle, h, False, "gather_w_ffn1_up_wait"), up_grp)
        return wfull["w1_a"], wfull["w3_a"]

    x1, ffn1_saved, _ = _ffn_forward(x, g_ffn1 + start_tokens, sh1, sc1, gt1, late_w13a, None, late_w2a, "ffn1")
    unpack_group(_exchange_end(mix_handle, x1, False, "gather_w_mix_wait"), groups[1])

    h2 = _norm_mod(x1, g_mix, sh2, sc2, "mix_norm")
    h2f = _flat(h2)
    win = wfull["w_in"]
    win1, winu, win3 = win[:, :3 * CW], win[:, 3 * CW:3 * CW + SW], win[:, 3 * CW + SW:]
    p1 = _mm(h2f, win1, out_dtype=BF16, name="mix_in1").reshape(bsz, seq, 3 * CW)
    u = _mm(h2f, winu, name="mix_inu").reshape(bsz, seq, SW)
    p3 = _mm(h2f, win3, out_dtype=BF16, name="mix_in3").reshape(bsz, seq, 2 * D)

    ya_in = _conv_fwd(p1, convw8, "conv_fwd")
    ya = _mm(_flat(ya_in), wfull["w_conv_out"], out_dtype=BF16, name="conv_out").reshape(bsz, seq, D)

    xs, ys = _ssm_fwd(u, wb, wct, ar8, ai8, "ssm_fwd")
    s0, z, s1, s2, yb = _ssm_post(ys, u, d_skip, wfull["w_glu"], wfull["w_ssm_out"], "ssm_post")

    merged, mix, x2 = _merge_out(ya, yb, p3, wfull["w_out"], x1, gt2, "merge_out")
    unpack_group(_exchange_end(ffn2_handle, x2, False, "gather_w_ffn2_wait"), groups[2])
    x3, ffn2_saved, _ = _ffn_forward(x2, g_ffn2, sh3, sc3, gt3, wfull["w1_b"], wfull["w3_b"], wfull["w2_b"], "ffn2")

    dx3, lossvec, dgfin = _final_loss(x3, g_final.reshape(1, D), loss_target, "final_loss")
    loss = lax.psum(jnp.sum(lossvec), ("x", "y", "c"))

    gfull = {}
    dx2, (dsh3, dsc3, dgt3, dg3), (gfull["w1_b"], gfull["w3_b"], gfull["w2_b"]) = _ffn_backward(
        dx3, ffn2_saved, g_ffn2, sc3, gt3, wfull["w1_b"], wfull["w3_b"], wfull["w2_b"], "ffn2")

    def stack_group(grp):
        return [_stacked_from_full(gfull[n], rows, cols, split).astype(BF16) for n, rows, cols, split in grp]

    st_ffn2 = stack_group(groups[2])
    h_ffn2 = _exchange_begin(st_ffn2, [_own_slab(s, chip) for s in st_ffn2], True, "scatter_ffn2_start")

    dmix, dya, dyb, dp3, dgt2 = _merge_bwd(dx2, gt2 + h_ffn2[4][0, 0], mix, ya, yb, p3, wfull["w_out"], "merge_bwd")
    gfull["w_out"] = _mm(_flat(merged), _flat(dmix), ta=True, name="gw_out")
    dp1, dconvw8 = _conv_bwd(dya, wfull["w_conv_out"], p1, convw8, "conv_bwd")
    gfull["w_conv_out"] = _mm(_flat(ya_in), _flat(dya), ta=True, name="gw_conv_out")
    ds0, dz, ddskip = _ssm_post_bwd(dyb, s0, z, u, d_skip, wfull["w_glu"], wfull["w_ssm_out"], "ssm_post_bwd")
    gfull["w_ssm_out"] = _mm(_flat(s2), _flat(dyb), ta=True, name="gw_ssm_out")
    gfull["w_glu"] = _mm(_flat(s1), _flat(dz), ta=True, name="gw_glu")
    du, dwb, dwct, dar8, dai8 = _ssm_bwd(ds0, u, xs, wb, wct, ar8, ai8, d_skip, "ssm_bwd")
    dx1, dsh2, dsc2, dgmix = _dh_norm_bwd([dp1, du, dp3], [win1, winu, win3], x1, g_mix, sc2, dx2, "mix_bwd_dh")
    gfull["w_in"] = jnp.concatenate([
        _mm(h2f, _flat(dp1), ta=True, name="gw_in1"), _mm(h2f, _flat(du), ta=True, name="gw_inu"),
        _mm(h2f, _flat(dp3), ta=True, name="gw_in3")], axis=1)

    st_mix = stack_group(groups[1])
    h_mix = _exchange_begin(st_mix, [_own_slab(s, chip) for s in st_mix], True, "scatter_mix_start")

    def swap_begin(recv, name):
        return _exchange_begin(recv, [lax.empty(v.shape, v.dtype) for v in recv], SIBLING, name)

    recv_ffn2 = _exchange_end(h_ffn2, dx1, True, "scatter_ffn2_wait")
    sw_ffn2 = swap_begin(recv_ffn2, "swap_ffn2_start")

    ffn1_names = {"w1": BIG[0], "w3": BIG[1], "w2": BIG[2]}
    ffn1_handles = {}

    def emit_ffn1(key, gw):
        n, rows, cols, split = ffn1_names[key]
        st = _stacked_from_full(gw, rows, cols, split).astype(BF16)
        ffn1_handles[n] = _exchange_begin([st], [_own_slab(st, chip)], True, f"scatter_ffn1_{key}_start")
        return ffn1_handles[n][4][0, 0]

    grad_x, (dsh1, dsc1, dgt1, dg1), _ = _ffn_backward(
        dx1, ffn1_saved, g_ffn1, sc1, gt1 + (h_mix[4][0, 0] + sw_ffn2[4][0, 0]), wfull["w1_a"], wfull["w3_a"],
        wfull["w2_a"], "ffn1", emit=emit_ffn1)
    recv_mix = _exchange_end(h_mix, grad_x, True, "scatter_mix_wait")
    sw_mix = swap_begin(recv_mix, "swap_mix_start")
    dg1 = dg1 + sw_mix[4][0, 0]

    sbw = GP // SSM_SUPER
    d_are, d_aim, d_ldt, d_bre_t, d_bim_t = _ssm_disc_bwd(
        are, aim, ldt, bre_t, bim_t, jnp.sum(dar8, axis=0, keepdims=True), jnp.sum(dai8, axis=0, keepdims=True),
        _blockdiag_extract(dwb[:, :sbw]), _blockdiag_extract(dwb[:, sbw:]))
    d_cre = _c_from_t(_blockdiag_extract(dwct[:, :sbw]))
    d_cim = -_c_from_t(_blockdiag_extract(dwct[:, sbw:]))

    small_parts = [dg1, dgmix, dg3, dgfin, dconvw8[:3], d_are, d_aim, _from_t(d_bre_t), _from_t(d_bim_t), d_cre, d_cim,
                   jnp.sum(d_ldt.reshape(NG, NP), axis=1), ddskip]
    small_sizes = [int(p.size) for p in small_parts]
    n_small = sum(small_sizes)
    n_small_pad = -(-n_small // (SUB * PACK_COLS)) * (SUB * PACK_COLS)
    dmod = jnp.concatenate([dsh1, dsc1, dgt1, dsh2, dsc2, dgt2, dsh3, dsc3, dgt3], axis=2).reshape(bsz * NMOD * D)
    flat = jnp.concatenate([p.reshape(-1) for p in small_parts] + [jnp.zeros((n_small_pad - n_small,), F32), dmod])
    allg = _all_gather8(flat.reshape(SUB, -1), "gather_small").reshape(N_DEV, -1)
    small = _sum_slabs(allg[:, :n_small_pad].reshape(N_DEV, -1, PACK_COLS), "sum_small").reshape(-1)
    sg, o = [], 0
    for p, sz in zip(small_parts, small_sizes):
        sg.append(small[o:o + sz].reshape(p.shape))
        o += sz
    (g_g1, g_gmix, g_g3, g_gfin, g_convw, g_are, g_aim, g_bre, g_bim, g_cre, g_cim, g_ldt, g_dskip) = sg

    dmod_all = allg[:, n_small_pad:].reshape(nb, NMOD * D)
    dmod_shard = _select(dmod_all.reshape(nb, N_CHIPS, nmod_shard).transpose(1, 0, 2), chip)
    g_wada, g_bada = _ada_bwd(c_all, dmod_shard, dmod_all)

    recv_ffn1 = [_exchange_end(ffn1_handles[n], g_wada, True, f"scatter_ffn1_{n}_wait")[0] for n, _, _, _ in groups[0]]
    sib_ffn1 = _swap_sibling(recv_ffn1, "swap_ffn1")
    recv_mix, sib_mix = _exchange_end(sw_mix, sib_ffn1[0], SIBLING, "swap_mix_wait", with_srcs=True)
    recv_ffn2, sib_ffn2 = _exchange_end(sw_ffn2, sib_ffn1[0], SIBLING, "swap_ffn2_wait", with_srcs=True)
    recv = [*recv_ffn1, *recv_mix, *recv_ffn2]
    recv_sib = [*sib_ffn1, *sib_mix, *sib_ffn2]

    grads, deltas, new_m, new_v = {}, {}, {}, {}
    for (n, _, _, _), r_own, r_sib in zip(BIG, recv, recv_sib):
        res = _adamw_big(_view(args[n], n), _view(args["m_" + n], n), _view(args["v_" + n], n), r_own, r_sib,
                         f"adamw_{n}")
        grads[n], deltas[n], new_m[n], new_v[n] = [_view(r, n) for r in res]
    grads["w_ada"] = g_wada[None]
    deltas["w_ada"], new_m["w_ada"], new_v["w_ada"] = _adamw_rows(w_ada, m_w_ada, v_w_ada, g_wada, "adamw_w_ada")

    g_convw_shard = _select(g_convw.reshape(3, N_CHIPS, CW // N_CHIPS).transpose(1, 0, 2), chip)
    small_g = {"b_ada": g_bada, "g_ffn1": g_g1, "g_mix": g_gmix, "g_ffn2": g_g3, "g_final": g_gfin,
               "conv_w": g_convw_shard, "a_re": g_are, "a_im": g_aim, "b_re": g_bre, "b_im": g_bim,
               "c_re": g_cre, "c_im": g_cim, "log_dt": g_ldt, "d_skip": g_dskip}
    small_names = list(small_g)
    sizes = [int(args[n].size) for n in small_names]
    tot = sum(sizes)
    tot_pad = -(-tot // (SUB * PACK_COLS)) * (SUB * PACK_COLS)

    def pack(get):
        return jnp.concatenate([get(n).reshape(-1) for n in small_names] + [jnp.zeros((tot_pad - tot,), F32)]).reshape(
            -1, PACK_COLS)

    res = _adamw_plain(pack(lambda n: args[n]), pack(lambda n: args["m_" + n]), pack(lambda n: args["v_" + n]),
                       pack(lambda n: small_g[n]), "adamw_small")
    o = 0
    for n, sz in zip(small_names, sizes):
        shp = args[n].shape
        grads[n] = small_g[n].reshape(shp)
        deltas[n], new_m[n], new_v[n] = [r.reshape(-1)[o:o + sz].reshape(shp) for r in res]
        o += sz

    return (loss, grad_x, *[grads[n] for n in names], *[deltas[n] for n in names],
            *[new_m[n] for n in names], *[new_v[n] for n in names])
```

```python
import math

import jax
import jax.numpy as jnp
from jax import lax
from jax.experimental import pallas as pl
from jax.experimental.pallas import tpu as pltpu

F32 = jnp.float32
BF16 = jnp.bfloat16
SDS = jax.ShapeDtypeStruct
MESH = pl.DeviceIdType.MESH

D = 1024
DFF = 2816
CW = 1024
SW = 512
NG, NP, NH = 32, 64, 16
GP = NG * NP
NMOD = 9
EPS = 1e-6
N_CHIPS = 4
N_DEV = 8
SUB = 8
LANE = 128
SSM_SUPER = 4
VMEM_LIMIT = 50 * 1024 * 1024

LR, B1, B2, AEPS, WD, STEP = 0.001, 0.9, 0.999, 1e-08, 0.01, 10
BC1 = 1.0 - B1 ** STEP
BC2 = 1.0 - B2 ** STEP


def _cp(*sem):
    return pltpu.CompilerParams(dimension_semantics=sem or None, vmem_limit_bytes=VMEM_LIMIT)


def _pick_tile(n, cands):
    for t in cands:
        if t <= n and n % t == 0:
            return t
    return n


def _dot(a, b):
    return lax.dot_general(a, b, (((1,), (0,)), ((), ())), preferred_element_type=F32)


def _dot_nt(a, b):
    return lax.dot_general(a, b, (((1,), (1,)), ((), ())), preferred_element_type=F32)


def _dot_tn(a, b):
    return lax.dot_general(a, b, (((0,), (0,)), ((), ())), preferred_element_type=F32)


def _row(tm, width, col=0):
    return pl.BlockSpec((1, tm, width), lambda b, i, *_: (b, i, col))


def _seqvec(width):
    return pl.BlockSpec((1, 1, width), lambda b, *_: (b, 0, 0))


def _full2(shape):
    return pl.BlockSpec(shape, lambda *_: (0, 0))


def _sigmoid(x):
    return jax.nn.sigmoid(x)


def _mm(a, b, *, ta=False, tb=False, out_dtype=F32, name):
    if ta:
        kdim, m = a.shape
    else:
        m, kdim = a.shape
    n = b.shape[0] if tb else b.shape[1]
    tm = _pick_tile(m, (1408, 1024, 512, 256, 128))
    tn = _pick_tile(n, (1408, 1024, 512, 256, 128))
    tk = _pick_tile(kdim, (1024, 512, 256, 128))
    nk = kdim // tk

    def body(a_ref, b_ref, o_ref, acc_ref):
        k = pl.program_id(2)

        @pl.when(k == 0)
        def _():
            acc_ref[...] = jnp.zeros_like(acc_ref)

        av = a_ref[...].astype(BF16)
        bv = b_ref[...].astype(BF16)
        dn = (((0 if ta else 1,), (1 if tb else 0,)), ((), ()))
        acc_ref[...] += lax.dot_general(av, bv, dn, preferred_element_type=F32)

        @pl.when(k == nk - 1)
        def _():
            o_ref[...] = acc_ref[...].astype(out_dtype)

    a_spec = pl.BlockSpec((tk, tm), lambda i, j, k: (k, i)) if ta else pl.BlockSpec((tm, tk), lambda i, j, k: (i, k))
    b_spec = pl.BlockSpec((tn, tk), lambda i, j, k: (j, k)) if tb else pl.BlockSpec((tk, tn), lambda i, j, k: (k, j))
    return pl.pallas_call(
        body, name=name, grid=(m // tm, n // tn, nk),
        in_specs=[a_spec, b_spec],
        out_specs=pl.BlockSpec((tm, tn), lambda i, j, k: (i, j)),
        out_shape=SDS((m, n), out_dtype),
        scratch_shapes=[pltpu.VMEM((tm, tn), F32)],
        compiler_params=_cp("parallel", "parallel", "arbitrary"),
    )(a, b)


def _flat(a):
    return a.reshape(-1, a.shape[-1])


def _norm_mod(x, g, sh, sc, name):
    bsz, seq, dm = x.shape
    tm = _pick_tile(seq, (512, 256, 128))

    def body(x_ref, g_ref, sh_ref, sc_ref, o_ref):
        xf = x_ref[0]
        r = lax.rsqrt(jnp.mean(xf * xf, axis=-1, keepdims=True) + EPS)
        hn = xf * r * g_ref[...]
        o_ref[0] = (hn * (1.0 + sc_ref[0]) + sh_ref[0]).astype(BF16)

    return pl.pallas_call(
        body, name=name, grid=(bsz, seq // tm),
        in_specs=[_row(tm, dm), _full2((1, dm)), _seqvec(dm), _seqvec(dm)],
        out_specs=_row(tm, dm), out_shape=SDS((bsz, seq, dm), BF16),
        compiler_params=_cp("parallel", "parallel"),
    )(x, g, sh, sc)


def _swiglu_up(h, w1, w3, name):
    bsz, seq, dm = h.shape
    nf = w1.shape[0]
    tm = _pick_tile(seq, (512, 256, 128))
    tn = _pick_tile(nf, (1408, 512, 256, 128))

    def body(h_ref, w1_ref, w3_ref, a_ref, b_ref, hid_ref):
        hv = h_ref[0]
        a = _dot_nt(hv, w1_ref[...])
        b = _dot_nt(hv, w3_ref[...])
        sg = _sigmoid(a)
        sa = a * sg
        a_ref[0] = (b * (sg * (1.0 + a * (1.0 - sg)))).astype(BF16)
        b_ref[0] = sa.astype(BF16)
        hid_ref[0] = (sa * b).astype(BF16)

    wspec = pl.BlockSpec((tn, dm), lambda n, b, i: (n, 0))
    ospec = pl.BlockSpec((1, tm, tn), lambda n, b, i: (b, i, n))
    shp = SDS((bsz, seq, nf), BF16)
    return pl.pallas_call(
        body, name=name, grid=(nf // tn, bsz, seq // tm),
        in_specs=[pl.BlockSpec((1, tm, dm), lambda n, b, i: (b, i, 0)), wspec, wspec],
        out_specs=[ospec, ospec, ospec], out_shape=[shp, shp, shp],
        compiler_params=_cp("parallel", "parallel", "parallel"),
    )(h, w1, w3)


def _ffn_down(hid, w2, x, gt, name):
    bsz, seq, nf = hid.shape
    dm = w2.shape[1]
    tm = _pick_tile(seq, (512, 256, 128))

    def body(hid_ref, w2_ref, x_ref, gt_ref, f_ref, xo_ref):
        f = _dot(hid_ref[0], w2_ref[...])
        f_ref[0] = f.astype(BF16)
        xo_ref[0] = x_ref[0] + 0.5 * gt_ref[0] * f

    shp = SDS((bsz, seq, dm), F32)
    return pl.pallas_call(
        body, name=name, grid=(bsz, seq // tm),
        in_specs=[_row(tm, nf), _full2((nf, dm)), _row(tm, dm), _seqvec(dm)],
        out_specs=[_row(tm, dm), _row(tm, dm)], out_shape=[SDS((bsz, seq, dm), BF16), shp],
        compiler_params=_cp("parallel", "parallel"),
    )(hid, w2, x, gt)


def _ffn_bwd_hid(dxo, gt, f, a, b, w2, name):
    bsz, seq, dm = dxo.shape
    nf = a.shape[2]
    tm = _pick_tile(seq, (512, 256, 128))
    tn = _pick_tile(nf, (1408, 512, 256, 128))

    def body(dxo_ref, gt_ref, f_ref, a_ref, b_ref, w2_ref, dfs_ref, da_ref, db_ref, dgt_ref):
        i = pl.program_id(1)
        n = pl.program_id(2)

        @pl.when(n == 0)
        def _():
            dxo = dxo_ref[0]
            dfs_ref[0] = (0.5 * gt_ref[0] * dxo).astype(BF16)
            part = jnp.sum(0.5 * dxo * f_ref[0].astype(F32), axis=0, keepdims=True)

            @pl.when(i == 0)
            def _():
                dgt_ref[0] = part

            @pl.when(i > 0)
            def _():
                dgt_ref[0] += part

        dhid = _dot_nt(dfs_ref[0], w2_ref[pl.ds(pl.multiple_of(n * tn, tn), tn), :])
        dh16 = dhid.astype(BF16)
        da_ref[0] = dh16 * a_ref[0]
        db_ref[0] = dh16 * b_ref[0]

    hspec = pl.BlockSpec((1, tm, tn), lambda b, i, n: (b, i, n))
    return pl.pallas_call(
        body, name=name, grid=(bsz, seq // tm, nf // tn),
        in_specs=[_row(tm, dm), _seqvec(dm), _row(tm, dm), hspec, hspec, _full2((nf, dm))],
        out_specs=[_row(tm, dm), hspec, hspec, _seqvec(dm)],
        out_shape=[SDS((bsz, seq, dm), BF16), SDS((bsz, seq, nf), BF16), SDS((bsz, seq, nf), BF16),
                   SDS((bsz, 1, dm), F32)],
        compiler_params=_cp("arbitrary", "arbitrary", "arbitrary"),
    )(dxo, gt, f, a, b, w2)


def _dh_norm_bwd(pieces, weights, x, g, sc, dxo, name, transposed=False):
    bsz, seq, dm = x.shape
    tm = _pick_tile(seq, (512, 256, 128))
    npc = len(pieces)
    dot = _dot if transposed else _dot_nt

    def body(*refs):
        p_refs = refs[:npc]
        w_hbm = refs[npc:2 * npc]
        x_ref, g_ref, sc_ref, dxo_ref, dx_ref, dsh_ref, dsc_ref, dg_ref = refs[2 * npc:2 * npc + 8]
        w_refs = refs[2 * npc + 8:]
        b = pl.program_id(0)
        i = pl.program_id(1)

        @pl.when((i == 0) & (b == 0))
        def _():
            for src, dst in zip(w_hbm, w_refs):
                pltpu.sync_copy(src, dst)

        dh = dot(p_refs[0][0], w_refs[0][...])
        for j in range(1, npc):
            dh = dh + dot(p_refs[j][0], w_refs[j][...])
        xf = x_ref[0]
        gv = g_ref[...]
        r = lax.rsqrt(jnp.mean(xf * xf, axis=-1, keepdims=True) + EPS)
        xhat = xf * r
        dhn = dh * (1.0 + sc_ref[0])
        p_sh = jnp.sum(dh, axis=0, keepdims=True)
        p_sc = jnp.sum(dh * (xhat * gv), axis=0, keepdims=True)
        p_g = jnp.sum(dhn * xhat, axis=0, keepdims=True)
        dxh = dhn * gv
        dx_ref[0] = dxo_ref[0] + r * (dxh - xhat * jnp.mean(dxh * xhat, axis=-1, keepdims=True))

        @pl.when(i == 0)
        def _():
            dsh_ref[0] = p_sh
            dsc_ref[0] = p_sc

        @pl.when(i > 0)
        def _():
            dsh_ref[0] += p_sh
            dsc_ref[0] += p_sc

        @pl.when((i == 0) & (b == 0))
        def _():
            dg_ref[...] = p_g

        @pl.when((i > 0) | (b > 0))
        def _():
            dg_ref[...] += p_g

    return pl.pallas_call(
        body, name=name, grid=(bsz, seq // tm),
        in_specs=[_row(tm, p.shape[2]) for p in pieces] + [pl.BlockSpec(memory_space=pl.ANY)] * npc + [
            _row(tm, dm), _full2((1, dm)), _seqvec(dm), _row(tm, dm)],
        out_specs=[_row(tm, dm), _seqvec(dm), _seqvec(dm), _full2((1, dm))],
        out_shape=[SDS((bsz, seq, dm), F32), SDS((bsz, 1, dm), F32), SDS((bsz, 1, dm), F32), SDS((1, dm), F32)],
        scratch_shapes=[pltpu.VMEM(w.shape, w.dtype) for w in weights],
        compiler_params=_cp("arbitrary", "arbitrary"),
    )(*pieces, *weights, x, g, sc, dxo)


HALO = 16


def _conv_core(gc, v, gch, vh, w, first):
    cv = gc * v
    halo = jnp.where(first, 0.0, gch * vh)
    ext = jnp.concatenate([halo, cv], axis=0)
    cv1 = pltpu.roll(ext, 1, 0)[HALO:]
    cv2 = pltpu.roll(ext, 2, 0)[HALO:]
    conv = w[0:1] * cv2 + w[1:2] * cv1 + w[2:3] * cv
    return cv, cv1, cv2, conv


def _prev_halo(tm, col):
    return pl.BlockSpec((1, HALO, CW), lambda b, i, *_: (b, jnp.maximum(i * (tm // HALO) - 1, 0), col))


def _next_halo(tm, seq, col):
    return pl.BlockSpec((1, HALO, CW), lambda b, i, *_: (b, jnp.minimum((i + 1) * (tm // HALO), seq // HALO - 1), col))


def _conv_fwd(p1, convw8, name):
    bsz, seq, _ = p1.shape
    tm = _pick_tile(seq, (512, 256, 128))

    def body(gb_ref, gc_ref, v_ref, gch_ref, vh_ref, w_ref, o_ref):
        first = pl.program_id(1) == 0
        _, _, _, conv = _conv_core(gc_ref[0].astype(F32), v_ref[0].astype(F32), gch_ref[0].astype(F32),
                                   vh_ref[0].astype(F32), w_ref[...], first)
        o_ref[0] = (gb_ref[0].astype(F32) * conv).astype(BF16)

    return pl.pallas_call(
        body, name=name, grid=(bsz, seq // tm),
        in_specs=[_row(tm, CW, 0), _row(tm, CW, 1), _row(tm, CW, 2), _prev_halo(tm, 1), _prev_halo(tm, 2),
                  _full2((8, CW))],
        out_specs=_row(tm, CW), out_shape=SDS((bsz, seq, CW), BF16),
        compiler_params=_cp("parallel", "parallel"),
    )(p1, p1, p1, p1, p1, convw8)


def _conv_bwd(dya, wco, p1, convw8, name):
    bsz, seq, _ = p1.shape
    tm = _pick_tile(seq, (512, 256, 128))
    nt = seq // tm
    ext_rows = tm + HALO

    def body(dya_ref, dyan_ref, wco_ref, gb_ref, gbn_ref, gc_ref, v_ref, gch_ref, vh_ref, w_ref, dp_ref, dw_ref):
        b = pl.program_id(0)
        i = pl.program_id(1)
        w = w_ref[...]
        gc = gc_ref[0].astype(F32)
        vv = v_ref[0].astype(F32)
        cv, cv1, cv2, conv = _conv_core(gc, vv, gch_ref[0].astype(F32), vh_ref[0].astype(F32), w, i == 0)
        dya_ext = jnp.concatenate([dya_ref[0], dyan_ref[0]], axis=0)
        dyain_ext = _dot_nt(dya_ext, wco_ref[...])
        gb_ext = jnp.concatenate([gb_ref[0], gbn_ref[0]], axis=0).astype(F32)
        rows = lax.broadcasted_iota(jnp.int32, (ext_rows, 1), 0)
        dconv_ext = jnp.where((rows < tm) | (i < nt - 1), dyain_ext * gb_ext, 0.0)
        dconv = dconv_ext[:tm]
        dconv1 = pltpu.roll(dconv_ext, ext_rows - 1, 0)[:tm]
        dconv2 = pltpu.roll(dconv_ext, ext_rows - 2, 0)[:tm]
        dcv = w[2:3] * dconv + w[1:2] * dconv1 + w[0:1] * dconv2
        dp_ref[0, :, 0:CW] = (dyain_ext[:tm] * conv).astype(BF16)
        dp_ref[0, :, CW:2 * CW] = (dcv * vv).astype(BF16)
        dp_ref[0, :, 2 * CW:3 * CW] = (dcv * gc).astype(BF16)
        g0 = jnp.sum(dconv * cv2, axis=0, keepdims=True)
        g1 = jnp.sum(dconv * cv1, axis=0, keepdims=True)
        g2 = jnp.sum(dconv * cv, axis=0, keepdims=True)
        upd = jnp.concatenate([g0, g1, g2, jnp.zeros((5, CW), F32)], axis=0)

        @pl.when((i == 0) & (b == 0))
        def _():
            dw_ref[...] = upd

        @pl.when((i > 0) | (b > 0))
        def _():
            dw_ref[...] += upd

    return pl.pallas_call(
        body, name=name, grid=(bsz, seq // tm),
        in_specs=[_row(tm, CW), _next_halo(tm, seq, 0), _full2((CW, D)),
                  _row(tm, CW, 0), _next_halo(tm, seq, 0), _row(tm, CW, 1), _row(tm, CW, 2),
                  _prev_halo(tm, 1), _prev_halo(tm, 2), _full2((8, CW))],
        out_specs=[_row(tm, 3 * CW), _full2((8, CW))],
        out_shape=[SDS((bsz, seq, 3 * CW), BF16), SDS((8, CW), F32)],
        compiler_params=_cp("arbitrary", "arbitrary"),
    )(dya, dya, wco, p1, p1, p1, p1, p1, p1, convw8)


def _disc(are, aim, ldt, bre, bim):
    dt = jnp.exp(ldt)
    mag = jnp.exp(are * dt)
    ang = aim * dt
    abr = mag * jnp.cos(ang)
    abi = mag * jnp.sin(ang)
    nr = abr - 1.0
    den = are * are + aim * aim
    cr = (nr * are + abi * aim) / den
    ci = (abi * are - nr * aim) / den
    return abr, abi, cr * bre - ci * bim, cr * bim + ci * bre


def _ssm_disc(are, aim, ldt, bre_t, bim_t):
    def body(are_ref, aim_ref, ldt_ref, bre_ref, bim_ref, abr_ref, abi_ref, bbr_ref, bbi_ref):
        abr, abi, bbr, bbi = _disc(are_ref[...], aim_ref[...], ldt_ref[...], bre_ref[...], bim_ref[...])
        abr_ref[...] = abr
        abi_ref[...] = abi
        bbr_ref[...] = bbr
        bbi_ref[...] = bbi

    v1, vh = SDS((1, GP), F32), SDS((NH, GP), F32)
    return pl.pallas_call(body, name="ssm_disc", out_shape=[v1, v1, vh, vh], compiler_params=_cp())(
        are, aim, ldt, bre_t, bim_t)


def _ssm_disc_bwd(are, aim, ldt, bre_t, bim_t, dabr, dabi, dbbr, dbbi):
    def body(are_ref, aim_ref, ldt_ref, bre_ref, bim_ref, g0, g1, g2, g3, o0, o1, o2, o3, o4):
        prim = (are_ref[...], aim_ref[...], ldt_ref[...], bre_ref[...], bim_ref[...])
        _, vjp = jax.vjp(_disc, *prim)
        d_are, d_aim, d_ldt, d_bre, d_bim = vjp((g0[...], g1[...], g2[...], g3[...]))
        o0[...] = d_are
        o1[...] = d_aim
        o2[...] = d_ldt
        o3[...] = d_bre
        o4[...] = d_bim

    v1, vh = SDS((1, GP), F32), SDS((NH, GP), F32)
    return pl.pallas_call(body, name="ssm_disc_bwd", out_shape=[v1, v1, v1, vh, vh], compiler_params=_cp())(
        are, aim, ldt, bre_t, bim_t, dabr, dabi, dbbr, dbbi)


def _scan_chunk(buf_ref, nt, bsz, ar, ai, init_r, init_i, reverse):
    nsub = SUB // bsz
    row = lax.broadcasted_iota(jnp.int32, (SUB, GP), 0)
    shift = ((SUB - bsz) if reverse else bsz) % SUB
    order = list(range(nsub - 1, -1, -1)) if reverse else list(range(nsub))

    def step(j, carry):
        pr, pi = carry
        jj = (nt - 1 - j) if reverse else j
        off = pl.multiple_of(jj * SUB, SUB)
        br = buf_ref[pl.ds(off, SUB), 0:GP]
        bi = buf_ref[pl.ds(off, SUB), GP:2 * GP]
        nr, ni = pr, pi
        for s in order:
            sr, si = nr, ni
            if shift:
                sr = pltpu.roll(sr, shift, 0)
                si = pltpu.roll(si, shift, 0)
            cr = ar * sr - ai * si + br
            ci = ar * si + ai * sr + bi
            if nsub == 1:
                nr, ni = cr, ci
            else:
                m = (row >= s * bsz) & (row < (s + 1) * bsz)
                nr = jnp.where(m, cr, nr)
                ni = jnp.where(m, ci, ni)
        buf_ref[pl.ds(off, SUB), 0:GP] = nr
        buf_ref[pl.ds(off, SUB), GP:2 * GP] = ni
        return nr, ni

    return lax.fori_loop(0, nt, step, (init_r, init_i))


def _ssm_chunk_rows(total_rows, bsz):
    return min(total_rows, 64 * bsz)


def _interleave(src_ref, tmp_ref, bsz, steps):
    nl = tmp_ref.shape[0]
    for b in range(bsz):
        for j in range(nl):
            tmp_ref.at[j][pl.ds(b, steps, stride=bsz), :] = src_ref[b, :, j * LANE:(j + 1) * LANE]
    return jnp.concatenate([tmp_ref[j] for j in range(nl)], axis=1)


def _deinterleave(val, tmp_ref, dst_ref, bsz, steps, skip=None):
    nl = tmp_ref.shape[0]
    for j in range(nl):
        tmp_ref[j] = val[:, j * LANE:(j + 1) * LANE]
    for b in range(bsz):
        for j in range(nl):
            lanes = slice(j * LANE, (j + 1) * LANE)
            v = tmp_ref.at[j][pl.ds(b, steps, stride=bsz), :]
            if skip is not None:
                v = v + skip[0][b, :, lanes] * skip[1][:, lanes]
            dst_ref[b, :, lanes] = v.astype(dst_ref.dtype)


SSM_UB = SW // SSM_SUPER
SSM_SB = GP // SSM_SUPER


def _sb_cols(s, half):
    return slice(half * GP + s * SSM_SB, half * GP + (s + 1) * SSM_SB)


def _ssm_in(v16, w_ref, x_ref):
    for s in range(SSM_SUPER):
        vs = v16[:, s * SSM_UB:(s + 1) * SSM_UB]
        for half in range(2):
            x_ref[:, _sb_cols(s, half)] = _dot(vs, w_ref[s * SSM_UB:(s + 1) * SSM_UB, half * SSM_SB:(half + 1) * SSM_SB])


def _ssm_out(x16, w_ref):
    outs = []
    for s in range(SSM_SUPER):
        rows = slice(s * SSM_UB, (s + 1) * SSM_UB)
        outs.append(_dot_nt(x16[:, _sb_cols(s, 0)], w_ref[rows, 0:SSM_SB])
                    + _dot_nt(x16[:, _sb_cols(s, 1)], w_ref[rows, SSM_SB:2 * SSM_SB]))
    return jnp.concatenate(outs, axis=1)


def _ssm_fwd(u, wb, wct, ar8, ai8, name):
    bsz, seq, _ = u.shape
    rt = seq * bsz
    r = _ssm_chunk_rows(rt, bsz)
    nt = r // SUB
    steps = r // bsz

    def body(u_ref, wb_hbm, wct_hbm, ar_ref, ai_ref, x_ref, y_ref, wb_ref, wct_ref, st_ref, tmp_ref):
        @pl.when(pl.program_id(0) == 0)
        def _():
            pltpu.sync_copy(wb_hbm, wb_ref)
            pltpu.sync_copy(wct_hbm, wct_ref)
            st_ref[...] = jnp.zeros_like(st_ref)

        _ssm_in(_interleave(u_ref, tmp_ref, bsz, steps).astype(BF16), wb_ref, x_ref)
        fr, fi = _scan_chunk(x_ref, nt, bsz, ar_ref[...], ai_ref[...], st_ref[:, 0:GP], st_ref[:, GP:2 * GP], False)
        st_ref[:, 0:GP] = fr
        st_ref[:, GP:2 * GP] = fi
        _deinterleave(_ssm_out(x_ref[...].astype(BF16), wct_ref), tmp_ref, y_ref, bsz, steps)

    anyspec = pl.BlockSpec(memory_space=pl.ANY)
    seqs = pl.BlockSpec((bsz, steps, SW), lambda i: (0, i, 0))
    return pl.pallas_call(
        body, name=name, grid=(rt // r,),
        in_specs=[seqs, anyspec, anyspec, _full2((SUB, GP)), _full2((SUB, GP))],
        out_specs=[pl.BlockSpec((r, 2 * GP), lambda i: (i, 0)), seqs],
        out_shape=[SDS((rt, 2 * GP), F32), SDS((bsz, seq, SW), F32)],
        scratch_shapes=[pltpu.VMEM(wb.shape, BF16), pltpu.VMEM(wct.shape, BF16), pltpu.VMEM((SUB, 2 * GP), F32),
                        pltpu.VMEM((SW // LANE, r, LANE), F32)],
        compiler_params=_cp("arbitrary"),
    )(u, wb, wct, ar8, ai8)


def _ssm_bwd(dy, u, xs, wb, wct, ar8, ai8, dskip, name):
    bsz, seq, _ = u.shape
    rt = seq * bsz
    r = _ssm_chunk_rows(rt, bsz)
    nt = r // SUB
    nc = rt // r
    steps = r // bsz
    ub = SW // SSM_SUPER
    sb = GP // SSM_SUPER

    def body(dys_ref, us_ref, x_ref, xh_ref, wb_hbm, wct_hbm, ar_ref, ai_ref, d_ref,
             dus_ref, dwb_hbm, dwct_hbm, dar_ref, dai_ref, wb_ref, wct_ref, g_ref, st_ref, awb_ref, awct_ref,
             tmp_ref):
        i = pl.program_id(0)
        dyb = _interleave(dys_ref, tmp_ref, bsz, steps).astype(BF16)
        ub16 = _interleave(us_ref, tmp_ref, bsz, steps).astype(BF16)

        @pl.when(i == 0)
        def _():
            pltpu.sync_copy(wb_hbm, wb_ref)
            pltpu.sync_copy(wct_hbm, wct_ref)
            st_ref[...] = jnp.zeros_like(st_ref)
            awb_ref[...] = jnp.zeros_like(awb_ref)
            awct_ref[...] = jnp.zeros_like(awct_ref)
            dar_ref[...] = jnp.zeros_like(dar_ref)
            dai_ref[...] = jnp.zeros_like(dai_ref)

        _ssm_in(dyb, wct_ref, g_ref)
        ar = ar_ref[...]
        ai = ai_ref[...]
        fr, fi = _scan_chunk(g_ref, nt, bsz, ar, -ai, st_ref[:, 0:GP], st_ref[:, GP:2 * GP], True)
        st_ref[:, 0:GP] = fr
        st_ref[:, GP:2 * GP] = fi

        gb = g_ref[...].astype(BF16)
        _deinterleave(_ssm_out(gb, wb_ref), tmp_ref, dus_ref, bsz, steps, skip=(dys_ref, d_ref))
        xb16 = x_ref[...].astype(BF16)
        for s in range(SSM_SUPER):
            us = ub16[:, s * ub:(s + 1) * ub]
            ds = dyb[:, s * ub:(s + 1) * ub]
            for half in range(2):
                cols = slice(half * GP + s * sb, half * GP + (s + 1) * sb)
                ocols = slice(half * sb, (half + 1) * sb)
                awb_ref[s * ub:(s + 1) * ub, ocols] += _dot_tn(us, gb[:, cols])
                awct_ref[s * ub:(s + 1) * ub, ocols] += _dot_tn(ds, xb16[:, cols])

        gr = g_ref[:, 0:GP]
        gi = g_ref[:, GP:2 * GP]
        xsr = pltpu.roll(x_ref[:, 0:GP], bsz, 0)
        xsi = pltpu.roll(x_ref[:, GP:2 * GP], bsz, 0)
        inner = lax.broadcasted_iota(jnp.int32, (r, 1), 0) >= bsz
        t_r = jnp.where(inner, gr * xsr + gi * xsi, 0.0)
        t_i = jnp.where(inner, gi * xsr - gr * xsi, 0.0)
        acc_r = jnp.sum(t_r.reshape(nt, SUB, GP), axis=0)
        acc_i = jnp.sum(t_i.reshape(nt, SUB, GP), axis=0)
        hr = xh_ref[:, 0:GP]
        hi = xh_ref[:, GP:2 * GP]
        if bsz % SUB:
            hr = pltpu.roll(hr, bsz, 0)
            hi = pltpu.roll(hi, bsz, 0)
        edge = (lax.broadcasted_iota(jnp.int32, (SUB, 1), 0) < bsz) & (i < nc - 1)
        g0r = g_ref[0:SUB, 0:GP]
        g0i = g_ref[0:SUB, GP:2 * GP]
        dar_ref[...] += acc_r + jnp.where(edge, g0r * hr + g0i * hi, 0.0)
        dai_ref[...] += acc_i + jnp.where(edge, g0i * hr - g0r * hi, 0.0)

        @pl.when(i == nc - 1)
        def _():
            pltpu.sync_copy(awb_ref, dwb_hbm)
            pltpu.sync_copy(awct_ref, dwct_hbm)

    anyspec = pl.BlockSpec(memory_space=pl.ANY)
    rev = lambda i: (nc - 1 - i, 0)
    seqs = pl.BlockSpec((bsz, steps, SW), lambda i: (0, nc - 1 - i, 0))
    wshape = (SW, 2 * sb)
    return pl.pallas_call(
        body, name=name, grid=(nc,),
        in_specs=[seqs, seqs, pl.BlockSpec((r, 2 * GP), rev),
                  pl.BlockSpec((SUB, 2 * GP), lambda i: (jnp.maximum((nc - 1 - i) * nt - 1, 0), 0)),
                  anyspec, anyspec, _full2((SUB, GP)), _full2((SUB, GP)), _full2((1, SW))],
        out_specs=[seqs, anyspec, anyspec, _full2((SUB, GP)), _full2((SUB, GP))],
        out_shape=[SDS((bsz, seq, SW), BF16), SDS(wshape, F32), SDS(wshape, F32), SDS((SUB, GP), F32),
                   SDS((SUB, GP), F32)],
        scratch_shapes=[pltpu.VMEM(wb.shape, BF16), pltpu.VMEM(wct.shape, BF16),
                        pltpu.VMEM((r, 2 * GP), F32), pltpu.VMEM((SUB, 2 * GP), F32),
                        pltpu.VMEM(wshape, F32), pltpu.VMEM(wshape, F32),
                        pltpu.VMEM((SW // LANE, r, LANE), F32)],
        compiler_params=_cp("arbitrary"),
    )(dy, u, xs, xs, wb, wct, ar8, ai8, dskip)


GELU_C = math.sqrt(2.0 / math.pi)


def _gelu(x):
    return 0.5 * x * (1.0 + jnp.tanh(GELU_C * (x + 0.044715 * x * x * x)))


def _gelu_grad(x):
    th = jnp.tanh(GELU_C * (x + 0.044715 * x * x * x))
    return 0.5 * (1.0 + th) + 0.5 * x * (1.0 - th * th) * GELU_C * (1.0 + 3.0 * 0.044715 * x * x)


def _ssm_post(ys, u, dskip, wglu, wso, name):
    bsz, seq, _ = ys.shape
    tm = _pick_tile(seq, (512, 256, 128))

    def body(ys_ref, u_ref, d_ref, wg_ref, wo_ref, s0_ref, z_ref, s1_ref, s2_ref, yb_ref):
        s0 = ys_ref[0] + d_ref[...] * u_ref[0]
        s1 = _gelu(s0)
        s1b = s1.astype(BF16)
        z = _dot(s1b, wg_ref[...])
        s2b = (s1 * _sigmoid(z)).astype(BF16)
        s0_ref[0] = s0
        z_ref[0] = z
        s1_ref[0] = s1b
        s2_ref[0] = s2b
        yb_ref[0] = _dot(s2b, wo_ref[...]).astype(BF16)

    return pl.pallas_call(
        body, name=name, grid=(bsz, seq // tm),
        in_specs=[_row(tm, SW), _row(tm, SW), _full2((1, SW)), _full2((SW, SW)), _full2((SW, D))],
        out_specs=[_row(tm, SW), _row(tm, SW), _row(tm, SW), _row(tm, SW), _row(tm, D)],
        out_shape=[SDS((bsz, seq, SW), F32), SDS((bsz, seq, SW), F32), SDS((bsz, seq, SW), BF16),
                   SDS((bsz, seq, SW), BF16), SDS((bsz, seq, D), BF16)],
        compiler_params=_cp("parallel", "parallel"),
    )(ys, u, dskip, wglu, wso)


def _ssm_post_bwd(dyb, s0, z, u, dskip, wglu, wso, name):
    bsz, seq, _ = s0.shape
    tm = _pick_tile(seq, (512, 256, 128))

    def body(dyb_ref, s0_ref, z_ref, u_ref, wg_ref, wo_ref, ds0_ref, dz_ref, dd_ref):
        b = pl.program_id(0)
        i = pl.program_id(1)
        ds2 = _dot_nt(dyb_ref[0], wo_ref[...])
        s0 = s0_ref[0]
        s1 = _gelu(s0)
        sg = _sigmoid(z_ref[0])
        dz = ds2 * s1 * sg * (1.0 - sg)
        dzb = dz.astype(BF16)
        ds1 = ds2 * sg + _dot_nt(dzb, wg_ref[...])
        ds0 = ds1 * _gelu_grad(s0)
        ds0_ref[0] = ds0
        dz_ref[0] = dzb
        part = jnp.sum(ds0 * u_ref[0], axis=0, keepdims=True)

        @pl.when((i == 0) & (b == 0))
        def _():
            dd_ref[...] = part

        @pl.when((i > 0) | (b > 0))
        def _():
            dd_ref[...] += part

    del dskip
    return pl.pallas_call(
        body, name=name, grid=(bsz, seq // tm),
        in_specs=[_row(tm, D), _row(tm, SW), _row(tm, SW), _row(tm, SW), _full2((SW, SW)), _full2((SW, D))],
        out_specs=[_row(tm, SW), _row(tm, SW), _full2((1, SW))],
        out_shape=[SDS((bsz, seq, SW), F32), SDS((bsz, seq, SW), BF16), SDS((1, SW), F32)],
        compiler_params=_cp("arbitrary", "arbitrary"),
    )(dyb, s0, z, u, wglu, wso)


def _merge_out(ya, yb, p3, wout, x1, gt, name):
    bsz, seq, _ = ya.shape
    tm = _pick_tile(seq, (512, 256, 128))

    def body(ya_ref, yb_ref, ga_ref, gbb_ref, w_ref, x_ref, gt_ref, mg_ref, mix_ref, xo_ref):
        merged = (_sigmoid(ga_ref[0].astype(F32)) * ya_ref[0].astype(F32)
                  + _sigmoid(gbb_ref[0].astype(F32)) * yb_ref[0].astype(F32)).astype(BF16)
        mix = _dot(merged, w_ref[...])
        mg_ref[0] = merged
        mix_ref[0] = mix.astype(BF16)
        xo_ref[0] = x_ref[0] + gt_ref[0] * mix

    return pl.pallas_call(
        body, name=name, grid=(bsz, seq // tm),
        in_specs=[_row(tm, D), _row(tm, D), _row(tm, D, 0), _row(tm, D, 1), _full2((D, D)), _row(tm, D), _seqvec(D)],
        out_specs=[_row(tm, D), _row(tm, D), _row(tm, D)],
        out_shape=[SDS((bsz, seq, D), BF16), SDS((bsz, seq, D), BF16), SDS((bsz, seq, D), F32)],
        compiler_params=_cp("parallel", "parallel"),
    )(ya, yb, p3, p3, wout, x1, gt)


def _merge_bwd(dx2, gt, mix, ya, yb, p3, wout, name):
    bsz, seq, _ = ya.shape
    tm = _pick_tile(seq, (512, 256, 128))

    def body(dx_ref, gt_ref, mix_ref, ya_ref, yb_ref, ga_ref, gbb_ref, w_ref, dmix_ref, dya_ref, dyb_ref, dp_ref, dgt_ref):
        i = pl.program_id(1)
        dx = dx_ref[0]
        dmix = (gt_ref[0] * dx).astype(BF16)
        dmix_ref[0] = dmix
        part = jnp.sum(dx * mix_ref[0].astype(F32), axis=0, keepdims=True)

        @pl.when(i == 0)
        def _():
            dgt_ref[0] = part

        @pl.when(i > 0)
        def _():
            dgt_ref[0] += part

        dmg = _dot_nt(dmix, w_ref[...])
        sa = _sigmoid(ga_ref[0].astype(F32))
        sb = _sigmoid(gbb_ref[0].astype(F32))
        dya_ref[0] = (dmg * sa).astype(BF16)
        dyb_ref[0] = (dmg * sb).astype(BF16)
        dp_ref[0, :, 0:D] = (dmg * ya_ref[0].astype(F32) * sa * (1.0 - sa)).astype(BF16)
        dp_ref[0, :, D:2 * D] = (dmg * yb_ref[0].astype(F32) * sb * (1.0 - sb)).astype(BF16)

    bshape = SDS((bsz, seq, D), BF16)
    return pl.pallas_call(
        body, name=name, grid=(bsz, seq // tm),
        in_specs=[_row(tm, D), _seqvec(D), _row(tm, D), _row(tm, D), _row(tm, D), _row(tm, D, 0), _row(tm, D, 1),
                  _full2((D, D))],
        out_specs=[_row(tm, D), _row(tm, D), _row(tm, D), _row(tm, 2 * D), _seqvec(D)],
        out_shape=[bshape, bshape, bshape, SDS((bsz, seq, 2 * D), BF16), SDS((bsz, 1, D), F32)],
        compiler_params=_cp("arbitrary", "arbitrary"),
    )(dx2, gt, mix, ya, yb, p3, p3, wout)


def _final_loss(x3, gfin, target, name):
    bsz, seq, dm = x3.shape
    tm = _pick_tile(seq, (512, 256, 128))

    def body(x_ref, g_ref, t_ref, dx_ref, loss_ref, dg_ref):
        b = pl.program_id(0)
        i = pl.program_id(1)
        xf = x_ref[0]
        gv = g_ref[...]
        r = lax.rsqrt(jnp.mean(xf * xf, axis=-1, keepdims=True) + EPS)
        xhat = xf * r
        e = xhat * gv - t_ref[0]
        dy = e * (1.0 / dm)
        dxh = dy * gv
        dx_ref[0] = r * (dxh - xhat * jnp.mean(dxh * xhat, axis=-1, keepdims=True))
        p_l = jnp.sum(e * e, axis=0, keepdims=True) * (0.5 / dm)
        p_g = jnp.sum(dy * xhat, axis=0, keepdims=True)

        @pl.when((i == 0) & (b == 0))
        def _():
            loss_ref[...] = p_l
            dg_ref[...] = p_g

        @pl.when((i > 0) | (b > 0))
        def _():
            loss_ref[...] += p_l
            dg_ref[...] += p_g

    return pl.pallas_call(
        body, name=name, grid=(bsz, seq // tm),
        in_specs=[_row(tm, dm), _full2((1, dm)), _row(tm, dm)],
        out_specs=[_row(tm, dm), _full2((1, dm)), _full2((1, dm))],
        out_shape=[SDS((bsz, seq, dm), F32), SDS((1, dm), F32), SDS((1, dm), F32)],
        compiler_params=_cp("arbitrary", "arbitrary"),
    )(x3, gfin, target)


def _ada_fwd(c_all, w_shard, b_shard):
    nb = c_all.shape[0]
    n = w_shard.shape[2]

    def body(c_ref, w_ref, b_ref, o_ref):
        cv = c_ref[...]
        cond = (cv * _sigmoid(cv)).astype(BF16)
        o_ref[...] = _dot(cond, w_ref[0].astype(BF16)) + b_ref[...]

    return pl.pallas_call(body, name="ada_fwd", out_shape=SDS((nb, n), F32), compiler_params=_cp())(
        c_all, w_shard, b_shard)


def _ada_bwd(c_all, dmod_shard, dmod_all):
    n = dmod_shard.shape[1]

    def body(c_ref, ds_ref, da_ref, gw_ref, gb_ref):
        cv = c_ref[...]
        cond = (cv * _sigmoid(cv)).astype(BF16)
        gw_ref[...] = _dot_tn(cond, ds_ref[...].astype(BF16))
        gb_ref[...] = jnp.sum(da_ref[...], axis=0, keepdims=True)

    return pl.pallas_call(
        body, name="ada_bwd", out_shape=[SDS((D, n), F32), SDS((1, dmod_all.shape[1]), F32)], compiler_params=_cp(),
    )(c_all, dmod_shard, dmod_all)


def _adamw_math(w, g, m, v):
    m = B1 * m + (1.0 - B1) * g
    v = B2 * v + (1.0 - B2) * (g * g)
    delta = -LR * ((m / BC1) / (jnp.sqrt(v / BC2) + AEPS) + WD * w)
    return delta, m, v


def _adamw_big(w, m, v, recv_own, recv_sib, name):
    _, rows, cols = w.shape
    tr = _pick_tile(rows, tuple(t for t in (512, 256, 128, 64, 32, 16, 8) if t * cols <= 192 * 1024))

    def body(w_ref, m_ref, v_ref, a_ref, b_ref, g_ref, d_ref, mo_ref, vo_ref):
        def chip_sum(r):
            acc = r[0].astype(F32)
            for k in range(1, N_CHIPS):
                acc = acc + r[k].astype(F32)
            return acc

        g = chip_sum(a_ref) + chip_sum(b_ref)
        delta, mn, vn = _adamw_math(w_ref[0], g, m_ref[0], v_ref[0])
        g_ref[0] = g
        d_ref[0] = delta
        mo_ref[0] = mn
        vo_ref[0] = vn

    own = pl.BlockSpec((1, tr, cols), lambda i: (0, i, 0))
    rspec = pl.BlockSpec((N_CHIPS, tr, cols), lambda i: (0, i, 0))
    shp = SDS(w.shape, F32)
    return pl.pallas_call(
        body, name=name, grid=(rows // tr,),
        in_specs=[own, own, own, rspec, rspec], out_specs=[own, own, own, own], out_shape=[shp, shp, shp, shp],
        compiler_params=_cp("parallel"),
    )(w, m, v, recv_own, recv_sib)


def _adamw_plain(w, m, v, g, name):
    def body(w_ref, m_ref, v_ref, g_ref, d_ref, mo_ref, vo_ref):
        delta, mn, vn = _adamw_math(w_ref[...], g_ref[...], m_ref[...], v_ref[...])
        d_ref[...] = delta
        mo_ref[...] = mn
        vo_ref[...] = vn

    shp = SDS(w.shape, F32)
    return pl.pallas_call(body, name=name, out_shape=[shp, shp, shp], compiler_params=_cp())(w, m, v, g)


def _adamw_rows(w, m, v, g, name):
    _, rows, cols = w.shape
    tr = _pick_tile(rows, (128, 64, 32, 16, 8))

    def body(w_ref, m_ref, v_ref, g_ref, d_ref, mo_ref, vo_ref):
        delta, mn, vn = _adamw_math(w_ref[0], g_ref[...], m_ref[0], v_ref[0])
        d_ref[0] = delta
        mo_ref[0] = mn
        vo_ref[0] = vn

    spec = pl.BlockSpec((1, tr, cols), lambda i: (0, i, 0))
    shp = SDS(w.shape, F32)
    return pl.pallas_call(
        body, name=name, grid=(rows // tr,), in_specs=[spec] * 3 + [pl.BlockSpec((tr, cols), lambda i: (i, 0))],
        out_specs=[spec] * 3, out_shape=[shp] * 3, compiler_params=_cp("parallel"),
    )(w, m, v, g)


def _sum_slabs(r, name):
    n, rows, cols = r.shape
    tr = _pick_tile(rows, (256, 128, 64))

    def body(r_ref, o_ref):
        acc = r_ref[0].astype(F32)
        for j in range(1, n):
            acc = acc + r_ref[j].astype(F32)
        o_ref[...] = acc

    return pl.pallas_call(
        body, name=name, grid=(rows // tr,),
        in_specs=[pl.BlockSpec((n, tr, cols), lambda i: (0, i, 0))],
        out_specs=pl.BlockSpec((tr, cols), lambda i: (i, 0)), out_shape=SDS((rows, cols), F32),
        compiler_params=_cp("parallel"),
    )(r)


def _place():
    return lax.axis_index("x"), lax.axis_index("y"), lax.axis_index("c")


def _all_gather8(blk, name):
    m_per, n = blk.shape

    def body(x_ref, out_ref, send_sems, recv_sems, local_sem):
        x, y, c = _place()
        me, sibling = (x, y, c), (x, y, 1 - c)
        chips = [(1 - x, y), (x, 1 - y), (1 - x, 1 - y)]

        def rows(px, py, pc):
            return out_ref.at[pl.ds((4 * px + 2 * py + pc) * m_per, m_per), :]

        def copy(k, block, to, src=None):
            return pltpu.make_async_remote_copy(
                src_ref=rows(*block) if src is None else src, dst_ref=rows(*block),
                send_sem=send_sems.at[k], recv_sem=recv_sems.at[k], device_id=to, device_id_type=MESH)

        mine = pltpu.make_async_copy(x_ref, rows(*me), local_sem)
        mine.start()
        first = [copy(0, me, sibling, src=x_ref)]
        first += [copy(1 + j, me, (*chip, c), src=x_ref) for j, chip in enumerate(chips)]
        for cp in first:
            cp.start()
        passed = [copy(4 + j, (*chip, c), sibling) for j, chip in enumerate(chips)]
        for j, chip in enumerate(chips):
            copy(1 + j, (*chip, c), me).wait_recv()
            passed[j].start()
        copy(0, sibling, me).wait_recv()
        for j, chip in enumerate(chips):
            copy(4 + j, (*chip, 1 - c), me).wait_recv()
        for cp in first + passed:
            cp.wait_send()
        mine.wait()

    return pl.pallas_call(
        body, name=name, out_shape=SDS((N_DEV * m_per, n), blk.dtype),
        in_specs=[pl.BlockSpec(memory_space=pltpu.VMEM)], out_specs=pl.BlockSpec(memory_space=pltpu.VMEM),
        scratch_shapes=[pltpu.SemaphoreType.DMA((7,)), pltpu.SemaphoreType.DMA((7,)), pltpu.SemaphoreType.DMA],
        compiler_params=pltpu.CompilerParams(vmem_limit_bytes=VMEM_LIMIT),
    )(blk)


def _chip_peers(x, y):
    return [(1 - x, y), (x, 1 - y), (1 - x, 1 - y)]


SIBLING = "sibling"


def _peer_copies(src_refs, land_refs, send_sems, recv_sems, scatter, landed):
    x, y, c = _place()
    if scatter == SIBLING:
        return [pltpu.make_async_remote_copy(
            src_ref=s, dst_ref=l, send_sem=send_sems.at[a], recv_sem=recv_sems.at[a],
            device_id=(x, y, 1 - c), device_id_type=MESH) for a, (s, l) in enumerate(zip(src_refs, land_refs))]
    cps = []
    for a, (src_ref, land_ref) in enumerate(zip(src_refs, land_refs)):
        for j, (px, py) in enumerate(_chip_peers(x, y)):
            if scatter:
                src = src_ref.at[2 * px + py]
            else:
                src = land_ref.at[2 * x + y] if src_ref is None else src_ref
            dst = land_ref.at[2 * px + py] if landed else land_ref.at[2 * x + y]
            cps.append(pltpu.make_async_remote_copy(
                src_ref=src, dst_ref=dst, send_sem=send_sems.at[3 * a + j], recv_sem=recv_sems.at[3 * a + j],
                device_id=(px, py, c), device_id_type=MESH))
    return cps


def _exchange_chips(srcs, scatter, name):
    n = len(srcs)

    def body(*refs):
        src_refs, land_refs = refs[:n], refs[n:2 * n]
        send_sems, recv_sems, local_sems = refs[2 * n:]
        x, y, _ = _place()
        me = 2 * x + y
        mine = [pltpu.make_async_copy(s.at[me] if scatter else s, l.at[me], local_sems.at[a])
                for a, (s, l) in enumerate(zip(src_refs, land_refs))]
        for cp in mine:
            cp.start()
        out = _peer_copies(src_refs, land_refs, send_sems, recv_sems, scatter, False)
        for cp in out:
            cp.start()
        for cp in _peer_copies(src_refs, land_refs, send_sems, recv_sems, scatter, True):
            cp.wait_recv()
        for cp in out:
            cp.wait_send()
        for cp in mine:
            cp.wait()

    anyspec = pl.BlockSpec(memory_space=pl.ANY)
    shapes = [SDS(s.shape if scatter else (N_CHIPS,) + s.shape, s.dtype) for s in srcs]
    return pl.pallas_call(
        body, name=name, out_shape=shapes, in_specs=[anyspec] * n, out_specs=[anyspec] * n,
        scratch_shapes=[pltpu.SemaphoreType.DMA((3 * n,)), pltpu.SemaphoreType.DMA((3 * n,)),
                        pltpu.SemaphoreType.DMA((n,))],
        compiler_params=pltpu.CompilerParams(vmem_limit_bytes=VMEM_LIMIT),
    )(*srcs)


_HBM = pl.BlockSpec(memory_space=pltpu.HBM)
_SEM = pl.BlockSpec(memory_space=pltpu.SEMAPHORE)
_EFFECT = pltpu.SideEffectType.DATAFLOW_SIDE_EFFECTING


def _exchange_begin(srcs, lands, scatter, name):
    srcs = tuple(srcs or ())
    n, ns = len(lands), len(srcs)
    nsem = n if scatter == SIBLING else 3 * n

    def body(*refs):
        src_refs = refs[:ns] if ns else (None,) * n
        land_refs = refs[ns:ns + n]
        send_sems, recv_sems = refs[ns + n:ns + n + 2]
        token = refs[-1]
        for cp in _peer_copies(src_refs, land_refs, send_sems, recv_sems, scatter, False):
            cp.start()
        token[...] = jnp.zeros_like(token)

    ops = (*srcs, *lands)
    res = pl.pallas_call(
        body, name=name,
        out_shape=(pltpu.SemaphoreType.DMA((nsem,)), pltpu.SemaphoreType.DMA((nsem,)),
                   *[pltpu.HBM(a.shape, a.dtype) for a in ops], SDS((SUB, LANE), F32)),
        in_specs=[_HBM] * len(ops), out_specs=(_SEM, _SEM, *[_HBM] * len(ops), pl.BlockSpec(memory_space=pltpu.VMEM)),
        input_output_aliases={i: 2 + i for i in range(len(ops))},
        compiler_params=pltpu.CompilerParams(has_side_effects=_EFFECT),
    )(*[pltpu.with_memory_space_constraint(a, pltpu.HBM) for a in ops])
    return res[0], res[1], res[2:2 + ns], res[2 + ns:2 + ns + n], res[-1]


def _exchange_end(handle, after, scatter, name, with_srcs=False):
    send_sems, recv_sems, srcs, lands, _ = handle
    n, ns = len(lands), len(srcs)

    def body(*refs):
        src_refs = refs[:ns] if ns else (None,) * n
        land_refs = refs[ns:ns + n]
        send_sems, recv_sems = refs[ns + n:ns + n + 2]
        for cp in _peer_copies(src_refs, land_refs, send_sems, recv_sems, scatter, True):
            cp.wait_send()
            cp.wait_recv()

    ops = (*srcs, *lands)
    res = pl.pallas_call(
        body, name=name,
        out_shape=tuple(pltpu.HBM(a.shape, a.dtype) for a in ops),
        in_specs=[_HBM] * len(ops) + [_SEM, _SEM, pl.BlockSpec(memory_space=pl.ANY)], out_specs=tuple([_HBM] * len(ops)),
        input_output_aliases={i: i for i in range(len(ops))},
        compiler_params=pltpu.CompilerParams(has_side_effects=_EFFECT),
    )(*ops, send_sems, recv_sems, after)
    return (list(res[:ns]), list(res[ns:])) if with_srcs else list(res[ns:])


def _own_slab(stack4, chip):
    idx = lax.broadcasted_iota(jnp.int32, (N_CHIPS,) + (1,) * (stack4.ndim - 1), 0)
    return jnp.where(idx == chip, stack4, jnp.zeros((), stack4.dtype))


def _swap_sibling(vs, name):
    n = len(vs)

    def body(*refs):
        in_refs, out_refs = refs[:n], refs[n:2 * n]
        send_sems, recv_sems = refs[2 * n:]
        x, y, c = _place()
        cps = [pltpu.make_async_remote_copy(
            src_ref=i, dst_ref=o, send_sem=send_sems.at[a], recv_sem=recv_sems.at[a],
            device_id=(x, y, 1 - c), device_id_type=MESH) for a, (i, o) in enumerate(zip(in_refs, out_refs))]
        for cp in cps:
            cp.start()
        for cp in cps:
            cp.wait()

    anyspec = pl.BlockSpec(memory_space=pl.ANY)
    return pl.pallas_call(
        body, name=name, out_shape=[SDS(v.shape, v.dtype) for v in vs], in_specs=[anyspec] * n, out_specs=[anyspec] * n,
        scratch_shapes=[pltpu.SemaphoreType.DMA((n,)), pltpu.SemaphoreType.DMA((n,))],
        compiler_params=pltpu.CompilerParams(vmem_limit_bytes=VMEM_LIMIT),
    )(*vs)


def _select(stacked, idx):
    out = stacked[0]
    for j in range(1, stacked.shape[0]):
        out = jnp.where(idx == j, stacked[j], out)
    return out


BIG = (
    ("w1_a", DFF // 4, D, False), ("w3_a", DFF // 4, D, False), ("w2_a", DFF // 4, D, False),
    ("w_in", D, 5632 // 4, True), ("w_conv_out", CW // 4, D, False), ("w_glu", SW // 4, SW, False),
    ("w_ssm_out", SW, D // 4, True), ("w_out", D // 4, D, False),
    ("w1_b", DFF // 4, D, False), ("w3_b", DFF // 4, D, False), ("w2_b", DFF // 4, D, False),
)
TRANSPOSED = frozenset(("w1_a", "w3_a", "w1_b", "w3_b"))
PACK_COLS = 1024


def _view(a, name):
    return jnp.transpose(a, (0, 2, 1)) if name in TRANSPOSED else a


def _full_from_stacked(st, split_cols):
    _, rows, cols = st.shape
    if split_cols:
        return st.transpose(1, 0, 2).reshape(rows, N_CHIPS * cols)
    return st.reshape(N_CHIPS * rows, cols)


def _stacked_from_full(full, rows, cols, split_cols):
    if split_cols:
        return full.reshape(rows, N_CHIPS, cols).transpose(1, 0, 2)
    return full.reshape(N_CHIPS, rows, cols)


def _blockdiag(t):
    r = lax.broadcasted_iota(jnp.int32, (SW, GP), 0) // NH
    cidx = lax.broadcasted_iota(jnp.int32, (SW, GP), 1) // NP
    dense = jnp.where(r == cidx, jnp.tile(t, (NG, 1)), 0.0)
    ub, sb = SW // SSM_SUPER, GP // SSM_SUPER
    return jnp.concatenate([dense[s * ub:(s + 1) * ub, s * sb:(s + 1) * sb] for s in range(SSM_SUPER)], axis=0)


def _blockdiag_extract(acc):
    gs = NG // SSM_SUPER
    a = acc.reshape(NG, NH, gs, NP)
    sel = (lax.broadcasted_iota(jnp.int32, (NG, 1, gs, 1), 0) % gs) == lax.broadcasted_iota(jnp.int32, (NG, 1, gs, 1), 2)
    a = jnp.sum(jnp.where(sel, a, 0.0), axis=2)
    return a.transpose(1, 0, 2).reshape(NH, GP)


def _to_t(p):
    return p.transpose(2, 0, 1).reshape(NH, GP)


def _from_t(t):
    return t.reshape(NH, NG, NP).transpose(1, 2, 0)


def _c_to_t(p):
    return p.transpose(1, 0, 2).reshape(NH, GP)


def _c_from_t(t):
    return t.reshape(NH, NG, NP).transpose(1, 0, 2)


def _ffn_forward(x, g, sh, sc, gt, w1, w3, w2, tag):
    h = _norm_mod(x, g, sh, sc, f"{tag}_norm")
    if callable(w1):
        w1, w3 = w1(h)
    a, b, hid = _swiglu_up(h, w1, w3, f"{tag}_up")
    w2 = w2(hid) if callable(w2) else w2
    f, xo = _ffn_down(hid, w2, x, gt, f"{tag}_down")
    return xo, (x, h, a, b, hid, f), w2


def _ffn_backward(dxo, saved, g, sc, gt, w1, w3, w2, tag, emit=lambda key, gw: 0.0):
    x, h, a, b, hid, f = saved
    dfs, da, db, dgt = _ffn_bwd_hid(dxo, gt, f, a, b, w2, f"{tag}_bwd_hid")
    h2 = _flat(h)
    gw2 = _mm(_flat(hid), _flat(dfs), ta=True, out_dtype=BF16, name=f"{tag}_gw2")
    tok = emit("w2", gw2)
    gw1 = _mm(_flat(da), h2, ta=True, out_dtype=BF16, name=f"{tag}_gw1")
    tok = tok + emit("w1", gw1)
    gw3 = _mm(_flat(db), h2, ta=True, out_dtype=BF16, name=f"{tag}_gw3")
    tok = tok + emit("w3", gw3)
    dx, dsh, dsc, dg = _dh_norm_bwd([da, db], [w1, w3], x, g, sc + tok, dxo, f"{tag}_bwd_dh", transposed=True)
    return dx, (dsh, dsc, dgt, dg), (gw1, gw3, gw2)


def kernel(x, c, w_ada, b_ada, g_ffn1, w1_a, w3_a, w2_a, g_mix, w_in, conv_w, w_conv_out, a_re, a_im, b_re, b_im, c_re, c_im, log_dt, d_skip, w_glu, w_ssm_out, w_out, g_ffn2, w1_b, w3_b, w2_b, g_final, loss_target, m_w_ada, m_b_ada, m_g_ffn1, m_w1_a, m_w3_a, m_w2_a, m_g_mix, m_w_in, m_conv_w, m_w_conv_out, m_a_re, m_a_im, m_b_re, m_b_im, m_c_re, m_c_im, m_log_dt, m_d_skip, m_w_glu, m_w_ssm_out, m_w_out, m_g_ffn2, m_w1_b, m_w3_b, m_w2_b, m_g_final, v_w_ada, v_b_ada, v_g_ffn1, v_w1_a, v_w3_a, v_w2_a, v_g_mix, v_w_in, v_conv_w, v_w_conv_out, v_a_re, v_a_im, v_b_re, v_b_im, v_c_re, v_c_im, v_log_dt, v_d_skip, v_w_glu, v_w_ssm_out, v_w_out, v_g_ffn2, v_w1_b, v_w3_b, v_w2_b, v_g_final):
    args = dict(locals())
    names = ["w_ada", "b_ada", "g_ffn1", "w1_a", "w3_a", "w2_a", "g_mix", "w_in", "conv_w", "w_conv_out", "a_re",
             "a_im", "b_re", "b_im", "c_re", "c_im", "log_dt", "d_skip", "w_glu", "w_ssm_out", "w_out", "g_ffn2",
             "w1_b", "w3_b", "w2_b", "g_final"]
    bsz, seq, _ = x.shape
    mx, my, mc = _place()
    chip = 2 * mx + my
    dev = 4 * mx + 2 * my + mc

    groups = (BIG[:3], BIG[3:8], BIG[8:])
    wfull = {}

    def shards(grp):
        return [_view(args[n], n)[0] for n, _, _, _ in grp]

    def unpack_group(gathered, grp):
        for (n, _, _, split), st in zip(grp, gathered):
            wfull[n] = _full_from_stacked(st, split)

    up_grp, down_grp = groups[0][:2], groups[0][2:]

    nmod_shard = NMOD * D // N_CHIPS
    c_all = _all_gather8(c.reshape(SUB, -1), "gather_c").reshape(N_DEV * bsz, D)
    b_shard = _select(b_ada.reshape(N_CHIPS, 1, nmod_shard), chip)
    mod_shard = _ada_fwd(c_all, w_ada, b_shard)
    nb = N_DEV * bsz
    cw_pad = jnp.pad(conv_w[0], ((0, SUB - 3), (0, nmod_shard - CW // N_CHIPS)))
    mod_st = _exchange_chips([jnp.concatenate([mod_shard, cw_pad], axis=0)], False, "gather_mod")[0]
    mod_all = mod_st[:, :nb].transpose(1, 0, 2).reshape(N_DEV, bsz, NMOD * D)
    mod = _select(mod_all, dev)

    def gather_begin(raw, name):
        lands = [_own_slab(jnp.broadcast_to(a.astype(BF16)[None], (N_CHIPS,) + a.shape), chip) for a in raw]
        return _exchange_begin(None, lands, False, name)

    up_raw, mod = lax.optimization_barrier((shards(up_grp), mod))
    up_handle = gather_begin(up_raw, "gather_w_ffn1_up_start")
    (down_raw, mix_raw, ffn2_raw), up_token = lax.optimization_barrier(
        ((shards(down_grp), shards(groups[1]), shards(groups[2])), up_handle[4][0:1, 0:1]))
    down_handle = gather_begin(down_raw, "gather_w_ffn1_down_start")
    mix_handle = gather_begin(mix_raw, "gather_w_mix_start")
    ffn2_handle = gather_begin(ffn2_raw, "gather_w_ffn2_start")
    start_tokens = up_token + down_handle[4][0:1, 0:1] + mix_handle[4][0:1, 0:1] + ffn2_handle[4][0:1, 0:1]

    sh1, sc1, gt1, sh2, sc2, gt2, sh3, sc3, gt3 = [mod[:, None, j * D:(j + 1) * D] for j in range(NMOD)]
    convw = mod_st[:, nb:nb + 3, :CW // N_CHIPS].transpose(1, 0, 2).reshape(3, CW)
    convw8 = jnp.pad(convw, ((0, SUB - 3), (0, 0)))

    are, aim = a_re.reshape(1, GP), a_im.reshape(1, GP)
    ldt = jnp.broadcast_to(log_dt.reshape(NG, 1), (NG, NP)).reshape(1, GP)
    bre_t, bim_t = _to_t(b_re[0]), _to_t(b_im[0])
    abr, abi, bbr_t, bbi_t = _ssm_disc(are, aim, ldt, bre_t, bim_t)
    wb = jnp.concatenate([_blockdiag(bbr_t), _blockdiag(bbi_t)], axis=1).astype(BF16)
    wct = jnp.concatenate([_blockdiag(_c_to_t(c_re[0])), _blockdiag(-_c_to_t(c_im[0]))], axis=1).astype(BF16)
    ar8 = jnp.broadcast_to(abr, (SUB, GP))
    ai8 = jnp.broadcast_to(abi, (SUB, GP))

    def late_w2a(hid):
        unpack_group(_exchange_end(down_handle, hid, False, "gather_w_ffn1_down_wait"), down_grp)
        return wfull["w2_a"]

    def late_w13a(h):
        unpack_group(_exchange_end(up_handle, h, False, "gather_w_ffn1_up_wait"), up_grp)
        return wfull["w1_a"], wfull["w3_a"]

    x1, ffn1_saved, _ = _ffn_forward(x, g_ffn1 + start_tokens, sh1, sc1, gt1, late_w13a, None, late_w2a, "ffn1")
    unpack_group(_exchange_end(mix_handle, x1, False, "gather_w_mix_wait"), groups[1])

    h2 = _norm_mod(x1, g_mix, sh2, sc2, "mix_norm")
    h2f = _flat(h2)
    win = wfull["w_in"]
    win1, winu, win3 = win[:, :3 * CW], win[:, 3 * CW:3 * CW + SW], win[:, 3 * CW + SW:]
    p1 = _mm(h2f, win1, out_dtype=BF16, name="mix_in1").reshape(bsz, seq, 3 * CW)
    u = _mm(h2f, winu, name="mix_inu").reshape(bsz, seq, SW)
    p3 = _mm(h2f, win3, out_dtype=BF16, name="mix_in3").reshape(bsz, seq, 2 * D)

    ya_in = _conv_fwd(p1, convw8, "conv_fwd")
    ya = _mm(_flat(ya_in), wfull["w_conv_out"], out_dtype=BF16, name="conv_out").reshape(bsz, seq, D)

    xs, ys = _ssm_fwd(u, wb, wct, ar8, ai8, "ssm_fwd")
    s0, z, s1, s2, yb = _ssm_post(ys, u, d_skip, wfull["w_glu"], wfull["w_ssm_out"], "ssm_post")

    merged, mix, x2 = _merge_out(ya, yb, p3, wfull["w_out"], x1, gt2, "merge_out")
    unpack_group(_exchange_end(ffn2_handle, x2, False, "gather_w_ffn2_wait"), groups[2])
    x3, ffn2_saved, _ = _ffn_forward(x2, g_ffn2, sh3, sc3, gt3, wfull["w1_b"], wfull["w3_b"], wfull["w2_b"], "ffn2")

    dx3, lossvec, dgfin = _final_loss(x3, g_final.reshape(1, D), loss_target, "final_loss")
    loss = lax.psum(jnp.sum(lossvec), ("x", "y", "c"))

    gfull = {}
    dx2, (dsh3, dsc3, dgt3, dg3), (gfull["w1_b"], gfull["w3_b"], gfull["w2_b"]) = _ffn_backward(
        dx3, ffn2_saved, g_ffn2, sc3, gt3, wfull["w1_b"], wfull["w3_b"], wfull["w2_b"], "ffn2")

    def stack_group(grp):
        return [_stacked_from_full(gfull[n], rows, cols, split) for n, rows, cols, split in grp]

    st_ffn2 = stack_group(groups[2])
    h_ffn2 = _exchange_begin(st_ffn2, [_own_slab(s, chip) for s in st_ffn2], True, "scatter_ffn2_start")

    dmix, dya, dyb, dp3, dgt2 = _merge_bwd(dx2, gt2 + h_ffn2[4][0, 0], mix, ya, yb, p3, wfull["w_out"], "merge_bwd")
    gfull["w_out"] = _mm(_flat(merged), _flat(dmix), ta=True, out_dtype=BF16, name="gw_out")
    dp1, dconvw8 = _conv_bwd(dya, wfull["w_conv_out"], p1, convw8, "conv_bwd")
    gfull["w_conv_out"] = _mm(_flat(ya_in), _flat(dya), ta=True, out_dtype=BF16, name="gw_conv_out")
    ds0, dz, ddskip = _ssm_post_bwd(dyb, s0, z, u, d_skip, wfull["w_glu"], wfull["w_ssm_out"], "ssm_post_bwd")
    gfull["w_ssm_out"] = _mm(_flat(s2), _flat(dyb), ta=True, out_dtype=BF16, name="gw_ssm_out")
    gfull["w_glu"] = _mm(_flat(s1), _flat(dz), ta=True, out_dtype=BF16, name="gw_glu")
    du, dwb, dwct, dar8, dai8 = _ssm_bwd(ds0, u, xs, wb, wct, ar8, ai8, d_skip, "ssm_bwd")
    dx1, dsh2, dsc2, dgmix = _dh_norm_bwd([dp1, du, dp3], [win1, winu, win3], x1, g_mix, sc2, dx2, "mix_bwd_dh")
    gfull["w_in"] = jnp.concatenate([
        _mm(h2f, _flat(dp1), ta=True, out_dtype=BF16, name="gw_in1"), _mm(h2f, _flat(du), ta=True, out_dtype=BF16, name="gw_inu"),
        _mm(h2f, _flat(dp3), ta=True, out_dtype=BF16, name="gw_in3")], axis=1)

    st_mix = stack_group(groups[1])
    h_mix = _exchange_begin(st_mix, [_own_slab(s, chip) for s in st_mix], True, "scatter_mix_start")

    def swap_begin(recv, name):
        return _exchange_begin(recv, [lax.empty(v.shape, v.dtype) for v in recv], SIBLING, name)

    recv_ffn2 = _exchange_end(h_ffn2, dx1, True, "scatter_ffn2_wait")
    sw_ffn2 = swap_begin(recv_ffn2, "swap_ffn2_start")

    ffn1_names = {"w1": BIG[0], "w3": BIG[1], "w2": BIG[2]}
    ffn1_handles = {}

    def emit_ffn1(key, gw):
        n, rows, cols, split = ffn1_names[key]
        st = _stacked_from_full(gw, rows, cols, split)
        ffn1_handles[n] = _exchange_begin([st], [_own_slab(st, chip)], True, f"scatter_ffn1_{key}_start")
        return ffn1_handles[n][4][0, 0]

    grad_x, (dsh1, dsc1, dgt1, dg1), _ = _ffn_backward(
        dx1, ffn1_saved, g_ffn1, sc1, gt1 + (h_mix[4][0, 0] + sw_ffn2[4][0, 0]), wfull["w1_a"], wfull["w3_a"],
        wfull["w2_a"], "ffn1", emit=emit_ffn1)
    recv_mix = _exchange_end(h_mix, grad_x, True, "scatter_mix_wait")
    sw_mix = swap_begin(recv_mix, "swap_mix_start")
    dg1 = dg1 + sw_mix[4][0, 0]

    sbw = GP // SSM_SUPER
    d_are, d_aim, d_ldt, d_bre_t, d_bim_t = _ssm_disc_bwd(
        are, aim, ldt, bre_t, bim_t, jnp.sum(dar8, axis=0, keepdims=True), jnp.sum(dai8, axis=0, keepdims=True),
        _blockdiag_extract(dwb[:, :sbw]), _blockdiag_extract(dwb[:, sbw:]))
    d_cre = _c_from_t(_blockdiag_extract(dwct[:, :sbw]))
    d_cim = -_c_from_t(_blockdiag_extract(dwct[:, sbw:]))

    small_parts = [dg1, dgmix, dg3, dgfin, dconvw8[:3], d_are, d_aim, _from_t(d_bre_t), _from_t(d_bim_t), d_cre, d_cim,
                   jnp.sum(d_ldt.reshape(NG, NP), axis=1), ddskip]
    small_sizes = [int(p.size) for p in small_parts]
    n_small = sum(small_sizes)
    n_small_pad = -(-n_small // (SUB * PACK_COLS)) * (SUB * PACK_COLS)
    dmod = jnp.concatenate([dsh1, dsc1, dgt1, dsh2, dsc2, dgt2, dsh3, dsc3, dgt3], axis=2).reshape(bsz * NMOD * D)
    flat = jnp.concatenate([p.reshape(-1) for p in small_parts] + [jnp.zeros((n_small_pad - n_small,), F32), dmod])
    allg = _all_gather8(flat.reshape(SUB, -1), "gather_small").reshape(N_DEV, -1)
    small = _sum_slabs(allg[:, :n_small_pad].reshape(N_DEV, -1, PACK_COLS), "sum_small").reshape(-1)
    sg, o = [], 0
    for p, sz in zip(small_parts, small_sizes):
        sg.append(small[o:o + sz].reshape(p.shape))
        o += sz
    (g_g1, g_gmix, g_g3, g_gfin, g_convw, g_are, g_aim, g_bre, g_bim, g_cre, g_cim, g_ldt, g_dskip) = sg

    dmod_all = allg[:, n_small_pad:].reshape(nb, NMOD * D)
    dmod_shard = _select(dmod_all.reshape(nb, N_CHIPS, nmod_shard).transpose(1, 0, 2), chip)
    g_wada, g_bada = _ada_bwd(c_all, dmod_shard, dmod_all)

    recv_ffn1 = [_exchange_end(ffn1_handles[n], g_wada, True, f"scatter_ffn1_{n}_wait")[0] for n, _, _, _ in groups[0]]
    sib_ffn1 = _swap_sibling(recv_ffn1, "swap_ffn1")
    recv_mix, sib_mix = _exchange_end(sw_mix, sib_ffn1[0], SIBLING, "swap_mix_wait", with_srcs=True)
    recv_ffn2, sib_ffn2 = _exchange_end(sw_ffn2, sib_ffn1[0], SIBLING, "swap_ffn2_wait", with_srcs=True)
    recv = [*recv_ffn1, *recv_mix, *recv_ffn2]
    recv_sib = [*sib_ffn1, *sib_mix, *sib_ffn2]

    grads, deltas, new_m, new_v = {}, {}, {}, {}
    for (n, _, _, _), r_own, r_sib in zip(BIG, recv, recv_sib):
        res = _adamw_big(_view(args[n], n), _view(args["m_" + n], n), _view(args["v_" + n], n), r_own, r_sib,
                         f"adamw_{n}")
        grads[n], deltas[n], new_m[n], new_v[n] = [_view(r, n) for r in res]
    grads["w_ada"] = g_wada[None]
    deltas["w_ada"], new_m["w_ada"], new_v["w_ada"] = _adamw_rows(w_ada, m_w_ada, v_w_ada, g_wada, "adamw_w_ada")

    g_convw_shard = _select(g_convw.reshape(3, N_CHIPS, CW // N_CHIPS).transpose(1, 0, 2), chip)
    small_g = {"b_ada": g_bada, "g_ffn1": g_g1, "g_mix": g_gmix, "g_ffn2": g_g3, "g_final": g_gfin,
               "conv_w": g_convw_shard, "a_re": g_are, "a_im": g_aim, "b_re": g_bre, "b_im": g_bim,
               "c_re": g_cre, "c_im": g_cim, "log_dt": g_ldt, "d_skip": g_dskip}
    small_names = list(small_g)
    sizes = [int(args[n].size) for n in small_names]
    tot = sum(sizes)
    tot_pad = -(-tot // (SUB * PACK_COLS)) * (SUB * PACK_COLS)

    def pack(get):
        return jnp.concatenate([get(n).reshape(-1) for n in small_names] + [jnp.zeros((tot_pad - tot,), F32)]).reshape(
            -1, PACK_COLS)

    res = _adamw_plain(pack(lambda n: args[n]), pack(lambda n: args["m_" + n]), pack(lambda n: args["v_" + n]),
                       pack(lambda n: small_g[n]), "adamw_small")
    o = 0
    for n, sz in zip(small_names, sizes):
        shp = args[n].shape
        grads[n] = small_g[n].reshape(shp)
        deltas[n], new_m[n], new_v[n] = [r.reshape(-1)[o:o + sz].reshape(shp) for r in res]
        o += sz

    return (loss, grad_x, *[grads[n] for n in names], *[deltas[n] for n in names],
            *[new_m[n] for n in names], *[new_v[n] for n in names])
```

```python
import math

import jax
import jax.numpy as jnp
from jax import lax
from jax.experimental import pallas as pl
from jax.experimental.pallas import tpu as pltpu

F32 = jnp.float32
BF16 = jnp.bfloat16
SDS = jax.ShapeDtypeStruct
MESH = pl.DeviceIdType.MESH

D = 1024
DFF = 2816
CW = 1024
SW = 512
NG, NP, NH = 32, 64, 16
GP = NG * NP
NMOD = 9
EPS = 1e-6
N_CHIPS = 4
N_DEV = 8
SUB = 8
LANE = 128
SSM_SUPER = 4
VMEM_LIMIT = 50 * 1024 * 1024

LR, B1, B2, AEPS, WD, STEP = 0.001, 0.9, 0.999, 1e-08, 0.01, 10
BC1 = 1.0 - B1 ** STEP
BC2 = 1.0 - B2 ** STEP


def _cp(*sem):
    return pltpu.CompilerParams(dimension_semantics=sem or None, vmem_limit_bytes=VMEM_LIMIT)


def _pick_tile(n, cands):
    for t in cands:
        if t <= n and n % t == 0:
            return t
    return n


def _dot(a, b):
    return lax.dot_general(a, b, (((1,), (0,)), ((), ())), preferred_element_type=F32)


def _dot_nt(a, b):
    return lax.dot_general(a, b, (((1,), (1,)), ((), ())), preferred_element_type=F32)


def _dot_tn(a, b):
    return lax.dot_general(a, b, (((0,), (0,)), ((), ())), preferred_element_type=F32)


def _row(tm, width, col=0):
    return pl.BlockSpec((1, tm, width), lambda b, i, *_: (b, i, col))


def _seqvec(width):
    return pl.BlockSpec((1, 1, width), lambda b, *_: (b, 0, 0))


def _full2(shape):
    return pl.BlockSpec(shape, lambda *_: (0, 0))


def _sigmoid(x):
    return jax.nn.sigmoid(x)


def _mm(a, b, *, ta=False, tb=False, out_dtype=F32, name):
    if ta:
        kdim, m = a.shape
    else:
        m, kdim = a.shape
    n = b.shape[0] if tb else b.shape[1]
    tm = _pick_tile(m, (1408, 1024, 512, 256, 128))
    tn = _pick_tile(n, (1408, 1024, 512, 256, 128))
    tk = _pick_tile(kdim, (1024, 512, 256, 128))
    nk = kdim // tk

    def body(a_ref, b_ref, o_ref, acc_ref):
        k = pl.program_id(2)

        @pl.when(k == 0)
        def _():
            acc_ref[...] = jnp.zeros_like(acc_ref)

        av = a_ref[...].astype(BF16)
        bv = b_ref[...].astype(BF16)
        dn = (((0 if ta else 1,), (1 if tb else 0,)), ((), ()))
        acc_ref[...] += lax.dot_general(av, bv, dn, preferred_element_type=F32)

        @pl.when(k == nk - 1)
        def _():
            o_ref[...] = acc_ref[...].astype(out_dtype)

    a_spec = pl.BlockSpec((tk, tm), lambda i, j, k: (k, i)) if ta else pl.BlockSpec((tm, tk), lambda i, j, k: (i, k))
    b_spec = pl.BlockSpec((tn, tk), lambda i, j, k: (j, k)) if tb else pl.BlockSpec((tk, tn), lambda i, j, k: (k, j))
    return pl.pallas_call(
        body, name=name, grid=(m // tm, n // tn, nk),
        in_specs=[a_spec, b_spec],
        out_specs=pl.BlockSpec((tm, tn), lambda i, j, k: (i, j)),
        out_shape=SDS((m, n), out_dtype),
        scratch_shapes=[pltpu.VMEM((tm, tn), F32)],
        compiler_params=_cp("parallel", "parallel", "arbitrary"),
    )(a, b)


def _flat(a):
    return a.reshape(-1, a.shape[-1])


def _norm_mod(x, g, sh, sc, name):
    bsz, seq, dm = x.shape
    tm = _pick_tile(seq, (512, 256, 128))

    def body(x_ref, g_ref, sh_ref, sc_ref, o_ref):
        xf = x_ref[0]
        r = lax.rsqrt(jnp.mean(xf * xf, axis=-1, keepdims=True) + EPS)
        hn = xf * r * g_ref[...]
        o_ref[0] = (hn * (1.0 + sc_ref[0]) + sh_ref[0]).astype(BF16)

    return pl.pallas_call(
        body, name=name, grid=(bsz, seq // tm),
        in_specs=[_row(tm, dm), _full2((1, dm)), _seqvec(dm), _seqvec(dm)],
        out_specs=_row(tm, dm), out_shape=SDS((bsz, seq, dm), BF16),
        compiler_params=_cp("parallel", "parallel"),
    )(x, g, sh, sc)


def _swiglu_up(h, w1, w3, name):
    bsz, seq, dm = h.shape
    nf = w1.shape[0]
    tm = _pick_tile(seq, (512, 256, 128))
    tn = _pick_tile(nf, (1408, 512, 256, 128))

    def body(h_ref, w1_ref, w3_ref, a_ref, b_ref, hid_ref):
        hv = h_ref[0]
        a = _dot_nt(hv, w1_ref[...])
        b = _dot_nt(hv, w3_ref[...])
        sg = _sigmoid(a)
        sa = a * sg
        a_ref[0] = (b * (sg * (1.0 + a * (1.0 - sg)))).astype(BF16)
        b_ref[0] = sa.astype(BF16)
        hid_ref[0] = (sa * b).astype(BF16)

    wspec = pl.BlockSpec((tn, dm), lambda n, b, i: (n, 0))
    ospec = pl.BlockSpec((1, tm, tn), lambda n, b, i: (b, i, n))
    shp = SDS((bsz, seq, nf), BF16)
    return pl.pallas_call(
        body, name=name, grid=(nf // tn, bsz, seq // tm),
        in_specs=[pl.BlockSpec((1, tm, dm), lambda n, b, i: (b, i, 0)), wspec, wspec],
        out_specs=[ospec, ospec, ospec], out_shape=[shp, shp, shp],
        compiler_params=_cp("parallel", "parallel", "parallel"),
    )(h, w1, w3)


def _ffn_down(hid, w2, x, gt, name):
    bsz, seq, nf = hid.shape
    dm = w2.shape[1]
    tm = _pick_tile(seq, (512, 256, 128))

    def body(hid_ref, w2_ref, x_ref, gt_ref, f_ref, xo_ref):
        f = _dot(hid_ref[0], w2_ref[...])
        f_ref[0] = f.astype(BF16)
        xo_ref[0] = x_ref[0] + 0.5 * gt_ref[0] * f

    shp = SDS((bsz, seq, dm), F32)
    return pl.pallas_call(
        body, name=name, grid=(bsz, seq // tm),
        in_specs=[_row(tm, nf), _full2((nf, dm)), _row(tm, dm), _seqvec(dm)],
        out_specs=[_row(tm, dm), _row(tm, dm)], out_shape=[SDS((bsz, seq, dm), BF16), shp],
        compiler_params=_cp("parallel", "parallel"),
    )(hid, w2, x, gt)


def _ffn_bwd_hid(dxo, gt, f, a, b, w2, name):
    bsz, seq, dm = dxo.shape
    nf = a.shape[2]
    tm = _pick_tile(seq, (512, 256, 128))
    tn = _pick_tile(nf, (1408, 512, 256, 128))

    def body(dxo_ref, gt_ref, f_ref, a_ref, b_ref, w2_ref, dfs_ref, da_ref, db_ref, dgt_ref):
        i = pl.program_id(1)
        n = pl.program_id(2)

        @pl.when(n == 0)
        def _():
            dxo = dxo_ref[0]
            dfs_ref[0] = (0.5 * gt_ref[0] * dxo).astype(BF16)
            part = jnp.sum(0.5 * dxo * f_ref[0].astype(F32), axis=0, keepdims=True)

            @pl.when(i == 0)
            def _():
                dgt_ref[0] = part

            @pl.when(i > 0)
            def _():
                dgt_ref[0] += part

        dhid = _dot_nt(dfs_ref[0], w2_ref[pl.ds(pl.multiple_of(n * tn, tn), tn), :])
        dh16 = dhid.astype(BF16)
        da_ref[0] = dh16 * a_ref[0]
        db_ref[0] = dh16 * b_ref[0]

    hspec = pl.BlockSpec((1, tm, tn), lambda b, i, n: (b, i, n))
    return pl.pallas_call(
        body, name=name, grid=(bsz, seq // tm, nf // tn),
        in_specs=[_row(tm, dm), _seqvec(dm), _row(tm, dm), hspec, hspec, _full2((nf, dm))],
        out_specs=[_row(tm, dm), hspec, hspec, _seqvec(dm)],
        out_shape=[SDS((bsz, seq, dm), BF16), SDS((bsz, seq, nf), BF16), SDS((bsz, seq, nf), BF16),
                   SDS((bsz, 1, dm), F32)],
        compiler_params=_cp("arbitrary", "arbitrary", "arbitrary"),
    )(dxo, gt, f, a, b, w2)


def _dh_norm_bwd(pieces, weights, x, g, sc, dxo, name, transposed=False):
    bsz, seq, dm = x.shape
    tm = _pick_tile(seq, (512, 256, 128))
    npc = len(pieces)
    dot = _dot if transposed else _dot_nt

    def body(*refs):
        p_refs = refs[:npc]
        w_hbm = refs[npc:2 * npc]
        x_ref, g_ref, sc_ref, dxo_ref, dx_ref, dsh_ref, dsc_ref, dg_ref = refs[2 * npc:2 * npc + 8]
        w_refs = refs[2 * npc + 8:]
        b = pl.program_id(0)
        i = pl.program_id(1)

        @pl.when((i == 0) & (b == 0))
        def _():
            for src, dst in zip(w_hbm, w_refs):
                pltpu.sync_copy(src, dst)

        dh = dot(p_refs[0][0], w_refs[0][...])
        for j in range(1, npc):
            dh = dh + dot(p_refs[j][0], w_refs[j][...])
        xf = x_ref[0]
        gv = g_ref[...]
        r = lax.rsqrt(jnp.mean(xf * xf, axis=-1, keepdims=True) + EPS)
        xhat = xf * r
        dhn = dh * (1.0 + sc_ref[0])
        p_sh = jnp.sum(dh, axis=0, keepdims=True)
        p_sc = jnp.sum(dh * (xhat * gv), axis=0, keepdims=True)
        p_g = jnp.sum(dhn * xhat, axis=0, keepdims=True)
        dxh = dhn * gv
        dx_ref[0] = dxo_ref[0] + r * (dxh - xhat * jnp.mean(dxh * xhat, axis=-1, keepdims=True))

        @pl.when(i == 0)
        def _():
            dsh_ref[0] = p_sh
            dsc_ref[0] = p_sc

        @pl.when(i > 0)
        def _():
            dsh_ref[0] += p_sh
            dsc_ref[0] += p_sc

        @pl.when((i == 0) & (b == 0))
        def _():
            dg_ref[...] = p_g

        @pl.when((i > 0) | (b > 0))
        def _():
            dg_ref[...] += p_g

    return pl.pallas_call(
        body, name=name, grid=(bsz, seq // tm),
        in_specs=[_row(tm, p.shape[2]) for p in pieces] + [pl.BlockSpec(memory_space=pl.ANY)] * npc + [
            _row(tm, dm), _full2((1, dm)), _seqvec(dm), _row(tm, dm)],
        out_specs=[_row(tm, dm), _seqvec(dm), _seqvec(dm), _full2((1, dm))],
        out_shape=[SDS((bsz, seq, dm), F32), SDS((bsz, 1, dm), F32), SDS((bsz, 1, dm), F32), SDS((1, dm), F32)],
        scratch_shapes=[pltpu.VMEM(w.shape, w.dtype) for w in weights],
        compiler_params=_cp("arbitrary", "arbitrary"),
    )(*pieces, *weights, x, g, sc, dxo)


HALO = 16


def _conv_core(gc, v, gch, vh, w, first):
    cv = gc * v
    halo = jnp.where(first, 0.0, gch * vh)
    ext = jnp.concatenate([halo, cv], axis=0)
    cv1 = pltpu.roll(ext, 1, 0)[HALO:]
    cv2 = pltpu.roll(ext, 2, 0)[HALO:]
    conv = w[0:1] * cv2 + w[1:2] * cv1 + w[2:3] * cv
    return cv, cv1, cv2, conv


def _prev_halo(tm, col):
    return pl.BlockSpec((1, HALO, CW), lambda b, i, *_: (b, jnp.maximum(i * (tm // HALO) - 1, 0), col))


def _next_halo(tm, seq, col):
    return pl.BlockSpec((1, HALO, CW), lambda b, i, *_: (b, jnp.minimum((i + 1) * (tm // HALO), seq // HALO - 1), col))


def _conv_fwd(p1, convw8, name):
    bsz, seq, _ = p1.shape
    tm = _pick_tile(seq, (512, 256, 128))

    def body(gb_ref, gc_ref, v_ref, gch_ref, vh_ref, w_ref, o_ref):
        first = pl.program_id(1) == 0
        _, _, _, conv = _conv_core(gc_ref[0].astype(F32), v_ref[0].astype(F32), gch_ref[0].astype(F32),
                                   vh_ref[0].astype(F32), w_ref[...], first)
        o_ref[0] = (gb_ref[0].astype(F32) * conv).astype(BF16)

    return pl.pallas_call(
        body, name=name, grid=(bsz, seq // tm),
        in_specs=[_row(tm, CW, 0), _row(tm, CW, 1), _row(tm, CW, 2), _prev_halo(tm, 1), _prev_halo(tm, 2),
                  _full2((8, CW))],
        out_specs=_row(tm, CW), out_shape=SDS((bsz, seq, CW), BF16),
        compiler_params=_cp("parallel", "parallel"),
    )(p1, p1, p1, p1, p1, convw8)


def _conv_bwd(dya, wco, p1, convw8, name):
    bsz, seq, _ = p1.shape
    tm = _pick_tile(seq, (512, 256, 128))
    nt = seq // tm
    ext_rows = tm + HALO

    def body(dya_ref, dyan_ref, wco_ref, gb_ref, gbn_ref, gc_ref, v_ref, gch_ref, vh_ref, w_ref, dp_ref, dw_ref):
        b = pl.program_id(0)
        i = pl.program_id(1)
        w = w_ref[...]
        gc = gc_ref[0].astype(F32)
        vv = v_ref[0].astype(F32)
        cv, cv1, cv2, conv = _conv_core(gc, vv, gch_ref[0].astype(F32), vh_ref[0].astype(F32), w, i == 0)
        dya_ext = jnp.concatenate([dya_ref[0], dyan_ref[0]], axis=0)
        dyain_ext = _dot_nt(dya_ext, wco_ref[...])
        gb_ext = jnp.concatenate([gb_ref[0], gbn_ref[0]], axis=0).astype(F32)
        rows = lax.broadcasted_iota(jnp.int32, (ext_rows, 1), 0)
        dconv_ext = jnp.where((rows < tm) | (i < nt - 1), dyain_ext * gb_ext, 0.0)
        dconv = dconv_ext[:tm]
        dconv1 = pltpu.roll(dconv_ext, ext_rows - 1, 0)[:tm]
        dconv2 = pltpu.roll(dconv_ext, ext_rows - 2, 0)[:tm]
        dcv = w[2:3] * dconv + w[1:2] * dconv1 + w[0:1] * dconv2
        dp_ref[0, :, 0:CW] = (dyain_ext[:tm] * conv).astype(BF16)
        dp_ref[0, :, CW:2 * CW] = (dcv * vv).astype(BF16)
        dp_ref[0, :, 2 * CW:3 * CW] = (dcv * gc).astype(BF16)
        g0 = jnp.sum(dconv * cv2, axis=0, keepdims=True)
        g1 = jnp.sum(dconv * cv1, axis=0, keepdims=True)
        g2 = jnp.sum(dconv * cv, axis=0, keepdims=True)
        upd = jnp.concatenate([g0, g1, g2, jnp.zeros((5, CW), F32)], axis=0)

        @pl.when((i == 0) & (b == 0))
        def _():
            dw_ref[...] = upd

        @pl.when((i > 0) | (b > 0))
        def _():
            dw_ref[...] += upd

    return pl.pallas_call(
        body, name=name, grid=(bsz, seq // tm),
        in_specs=[_row(tm, CW), _next_halo(tm, seq, 0), _full2((CW, D)),
                  _row(tm, CW, 0), _next_halo(tm, seq, 0), _row(tm, CW, 1), _row(tm, CW, 2),
                  _prev_halo(tm, 1), _prev_halo(tm, 2), _full2((8, CW))],
        out_specs=[_row(tm, 3 * CW), _full2((8, CW))],
        out_shape=[SDS((bsz, seq, 3 * CW), BF16), SDS((8, CW), F32)],
        compiler_params=_cp("arbitrary", "arbitrary"),
    )(dya, dya, wco, p1, p1, p1, p1, p1, p1, convw8)


def _disc(are, aim, ldt, bre, bim):
    dt = jnp.exp(ldt)
    mag = jnp.exp(are * dt)
    ang = aim * dt
    abr = mag * jnp.cos(ang)
    abi = mag * jnp.sin(ang)
    nr = abr - 1.0
    den = are * are + aim * aim
    cr = (nr * are + abi * aim) / den
    ci = (abi * are - nr * aim) / den
    return abr, abi, cr * bre - ci * bim, cr * bim + ci * bre


def _ssm_disc(are, aim, ldt, bre_t, bim_t):
    def body(are_ref, aim_ref, ldt_ref, bre_ref, bim_ref, abr_ref, abi_ref, bbr_ref, bbi_ref):
        abr, abi, bbr, bbi = _disc(are_ref[...], aim_ref[...], ldt_ref[...], bre_ref[...], bim_ref[...])
        abr_ref[...] = abr
        abi_ref[...] = abi
        bbr_ref[...] = bbr
        bbi_ref[...] = bbi

    v1, vh = SDS((1, GP), F32), SDS((NH, GP), F32)
    return pl.pallas_call(body, name="ssm_disc", out_shape=[v1, v1, vh, vh], compiler_params=_cp())(
        are, aim, ldt, bre_t, bim_t)


def _ssm_disc_bwd(are, aim, ldt, bre_t, bim_t, dabr, dabi, dbbr, dbbi):
    def body(are_ref, aim_ref, ldt_ref, bre_ref, bim_ref, g0, g1, g2, g3, o0, o1, o2, o3, o4):
        prim = (are_ref[...], aim_ref[...], ldt_ref[...], bre_ref[...], bim_ref[...])
        _, vjp = jax.vjp(_disc, *prim)
        d_are, d_aim, d_ldt, d_bre, d_bim = vjp((g0[...], g1[...], g2[...], g3[...]))
        o0[...] = d_are
        o1[...] = d_aim
        o2[...] = d_ldt
        o3[...] = d_bre
        o4[...] = d_bim

    v1, vh = SDS((1, GP), F32), SDS((NH, GP), F32)
    return pl.pallas_call(body, name="ssm_disc_bwd", out_shape=[v1, v1, v1, vh, vh], compiler_params=_cp())(
        are, aim, ldt, bre_t, bim_t, dabr, dabi, dbbr, dbbi)


def _scan_chunk(buf_ref, nt, bsz, ar, ai, init_r, init_i, reverse):
    nsub = SUB // bsz
    row = lax.broadcasted_iota(jnp.int32, (SUB, GP), 0)
    shift = ((SUB - bsz) if reverse else bsz) % SUB
    order = list(range(nsub - 1, -1, -1)) if reverse else list(range(nsub))

    def step(j, carry):
        pr, pi = carry
        jj = (nt - 1 - j) if reverse else j
        off = pl.multiple_of(jj * SUB, SUB)
        br = buf_ref[pl.ds(off, SUB), 0:GP]
        bi = buf_ref[pl.ds(off, SUB), GP:2 * GP]
        nr, ni = pr, pi
        for s in order:
            sr, si = nr, ni
            if shift:
                sr = pltpu.roll(sr, shift, 0)
                si = pltpu.roll(si, shift, 0)
            cr = ar * sr - ai * si + br
            ci = ar * si + ai * sr + bi
            if nsub == 1:
                nr, ni = cr, ci
            else:
                m = (row >= s * bsz) & (row < (s + 1) * bsz)
                nr = jnp.where(m, cr, nr)
                ni = jnp.where(m, ci, ni)
        buf_ref[pl.ds(off, SUB), 0:GP] = nr
        buf_ref[pl.ds(off, SUB), GP:2 * GP] = ni
        return nr, ni

    return lax.fori_loop(0, nt, step, (init_r, init_i))


def _ssm_chunk_rows(total_rows, bsz):
    return min(total_rows, 64 * bsz)


def _interleave(src_ref, tmp_ref, bsz, steps):
    nl = tmp_ref.shape[0]
    for b in range(bsz):
        for j in range(nl):
            tmp_ref.at[j][pl.ds(b, steps, stride=bsz), :] = src_ref[b, :, j * LANE:(j + 1) * LANE]
    return jnp.concatenate([tmp_ref[j] for j in range(nl)], axis=1)


def _deinterleave(val, tmp_ref, dst_ref, bsz, steps, skip=None):
    nl = tmp_ref.shape[0]
    for j in range(nl):
        tmp_ref[j] = val[:, j * LANE:(j + 1) * LANE]
    for b in range(bsz):
        for j in range(nl):
            lanes = slice(j * LANE, (j + 1) * LANE)
            v = tmp_ref.at[j][pl.ds(b, steps, stride=bsz), :]
            if skip is not None:
                v = v + skip[0][b, :, lanes] * skip[1][:, lanes]
            dst_ref[b, :, lanes] = v.astype(dst_ref.dtype)


SSM_UB = SW // SSM_SUPER
SSM_SB = GP // SSM_SUPER


def _sb_cols(s, half):
    return slice(half * GP + s * SSM_SB, half * GP + (s + 1) * SSM_SB)


def _ssm_in(v16, w_ref, x_ref):
    for s in range(SSM_SUPER):
        vs = v16[:, s * SSM_UB:(s + 1) * SSM_UB]
        for half in range(2):
            x_ref[:, _sb_cols(s, half)] = _dot(vs, w_ref[s * SSM_UB:(s + 1) * SSM_UB, half * SSM_SB:(half + 1) * SSM_SB])


def _ssm_out(x16, w_ref):
    outs = []
    for s in range(SSM_SUPER):
        rows = slice(s * SSM_UB, (s + 1) * SSM_UB)
        outs.append(_dot_nt(x16[:, _sb_cols(s, 0)], w_ref[rows, 0:SSM_SB])
                    + _dot_nt(x16[:, _sb_cols(s, 1)], w_ref[rows, SSM_SB:2 * SSM_SB]))
    return jnp.concatenate(outs, axis=1)


def _ssm_fwd(u, wb, wct, ar8, ai8, name):
    bsz, seq, _ = u.shape
    rt = seq * bsz
    r = _ssm_chunk_rows(rt, bsz)
    nt = r // SUB
    steps = r // bsz

    def body(u_ref, wb_hbm, wct_hbm, ar_ref, ai_ref, x_ref, y_ref, wb_ref, wct_ref, st_ref, tmp_ref):
        @pl.when(pl.program_id(0) == 0)
        def _():
            pltpu.sync_copy(wb_hbm, wb_ref)
            pltpu.sync_copy(wct_hbm, wct_ref)
            st_ref[...] = jnp.zeros_like(st_ref)

        _ssm_in(_interleave(u_ref, tmp_ref, bsz, steps).astype(BF16), wb_ref, x_ref)
        fr, fi = _scan_chunk(x_ref, nt, bsz, ar_ref[...], ai_ref[...], st_ref[:, 0:GP], st_ref[:, GP:2 * GP], False)
        st_ref[:, 0:GP] = fr
        st_ref[:, GP:2 * GP] = fi
        _deinterleave(_ssm_out(x_ref[...].astype(BF16), wct_ref), tmp_ref, y_ref, bsz, steps)

    anyspec = pl.BlockSpec(memory_space=pl.ANY)
    seqs = pl.BlockSpec((bsz, steps, SW), lambda i: (0, i, 0))
    return pl.pallas_call(
        body, name=name, grid=(rt // r,),
        in_specs=[seqs, anyspec, anyspec, _full2((SUB, GP)), _full2((SUB, GP))],
        out_specs=[pl.BlockSpec((r, 2 * GP), lambda i: (i, 0)), seqs],
        out_shape=[SDS((rt, 2 * GP), F32), SDS((bsz, seq, SW), F32)],
        scratch_shapes=[pltpu.VMEM(wb.shape, BF16), pltpu.VMEM(wct.shape, BF16), pltpu.VMEM((SUB, 2 * GP), F32),
                        pltpu.VMEM((SW // LANE, r, LANE), F32)],
        compiler_params=_cp("arbitrary"),
    )(u, wb, wct, ar8, ai8)


def _ssm_bwd(dy, u, xs, wb, wct, ar8, ai8, dskip, name):
    bsz, seq, _ = u.shape
    rt = seq * bsz
    r = _ssm_chunk_rows(rt, bsz)
    nt = r // SUB
    nc = rt // r
    steps = r // bsz
    ub = SW // SSM_SUPER
    sb = GP // SSM_SUPER

    def body(dys_ref, us_ref, x_ref, xh_ref, wb_hbm, wct_hbm, ar_ref, ai_ref, d_ref,
             dus_ref, dwb_hbm, dwct_hbm, dar_ref, dai_ref, wb_ref, wct_ref, g_ref, st_ref, awb_ref, awct_ref,
             tmp_ref):
        i = pl.program_id(0)
        dyb = _interleave(dys_ref, tmp_ref, bsz, steps).astype(BF16)
        ub16 = _interleave(us_ref, tmp_ref, bsz, steps).astype(BF16)

        @pl.when(i == 0)
        def _():
            pltpu.sync_copy(wb_hbm, wb_ref)
            pltpu.sync_copy(wct_hbm, wct_ref)
            st_ref[...] = jnp.zeros_like(st_ref)
            awb_ref[...] = jnp.zeros_like(awb_ref)
            awct_ref[...] = jnp.zeros_like(awct_ref)
            dar_ref[...] = jnp.zeros_like(dar_ref)
            dai_ref[...] = jnp.zeros_like(dai_ref)

        _ssm_in(dyb, wct_ref, g_ref)
        ar = ar_ref[...]
        ai = ai_ref[...]
        fr, fi = _scan_chunk(g_ref, nt, bsz, ar, -ai, st_ref[:, 0:GP], st_ref[:, GP:2 * GP], True)
        st_ref[:, 0:GP] = fr
        st_ref[:, GP:2 * GP] = fi

        gb = g_ref[...].astype(BF16)
        _deinterleave(_ssm_out(gb, wb_ref), tmp_ref, dus_ref, bsz, steps, skip=(dys_ref, d_ref))
        xb16 = x_ref[...].astype(BF16)
        for s in range(SSM_SUPER):
            us = ub16[:, s * ub:(s + 1) * ub]
            ds = dyb[:, s * ub:(s + 1) * ub]
            for half in range(2):
                cols = slice(half * GP + s * sb, half * GP + (s + 1) * sb)
                ocols = slice(half * sb, (half + 1) * sb)
                awb_ref[s * ub:(s + 1) * ub, ocols] += _dot_tn(us, gb[:, cols])
                awct_ref[s * ub:(s + 1) * ub, ocols] += _dot_tn(ds, xb16[:, cols])

        gr = g_ref[:, 0:GP]
        gi = g_ref[:, GP:2 * GP]
        xsr = pltpu.roll(x_ref[:, 0:GP], bsz, 0)
        xsi = pltpu.roll(x_ref[:, GP:2 * GP], bsz, 0)
        inner = lax.broadcasted_iota(jnp.int32, (r, 1), 0) >= bsz
        t_r = jnp.where(inner, gr * xsr + gi * xsi, 0.0)
        t_i = jnp.where(inner, gi * xsr - gr * xsi, 0.0)
        acc_r = jnp.sum(t_r.reshape(nt, SUB, GP), axis=0)
        acc_i = jnp.sum(t_i.reshape(nt, SUB, GP), axis=0)
        hr = xh_ref[:, 0:GP]
        hi = xh_ref[:, GP:2 * GP]
        if bsz % SUB:
            hr = pltpu.roll(hr, bsz, 0)
            hi = pltpu.roll(hi, bsz, 0)
        edge = (lax.broadcasted_iota(jnp.int32, (SUB, 1), 0) < bsz) & (i < nc - 1)
        g0r = g_ref[0:SUB, 0:GP]
        g0i = g_ref[0:SUB, GP:2 * GP]
        dar_ref[...] += acc_r + jnp.where(edge, g0r * hr + g0i * hi, 0.0)
        dai_ref[...] += acc_i + jnp.where(edge, g0i * hr - g0r * hi, 0.0)

        @pl.when(i == nc - 1)
        def _():
            pltpu.sync_copy(awb_ref, dwb_hbm)
            pltpu.sync_copy(awct_ref, dwct_hbm)

    anyspec = pl.BlockSpec(memory_space=pl.ANY)
    rev = lambda i: (nc - 1 - i, 0)
    seqs = pl.BlockSpec((bsz, steps, SW), lambda i: (0, nc - 1 - i, 0))
    wshape = (SW, 2 * sb)
    return pl.pallas_call(
        body, name=name, grid=(nc,),
        in_specs=[seqs, seqs, pl.BlockSpec((r, 2 * GP), rev),
                  pl.BlockSpec((SUB, 2 * GP), lambda i: (jnp.maximum((nc - 1 - i) * nt - 1, 0), 0)),
                  anyspec, anyspec, _full2((SUB, GP)), _full2((SUB, GP)), _full2((1, SW))],
        out_specs=[seqs, anyspec, anyspec, _full2((SUB, GP)), _full2((SUB, GP))],
        out_shape=[SDS((bsz, seq, SW), BF16), SDS(wshape, F32), SDS(wshape, F32), SDS((SUB, GP), F32),
                   SDS((SUB, GP), F32)],
        scratch_shapes=[pltpu.VMEM(wb.shape, BF16), pltpu.VMEM(wct.shape, BF16),
                        pltpu.VMEM((r, 2 * GP), F32), pltpu.VMEM((SUB, 2 * GP), F32),
                        pltpu.VMEM(wshape, F32), pltpu.VMEM(wshape, F32),
                        pltpu.VMEM((SW // LANE, r, LANE), F32)],
        compiler_params=_cp("arbitrary"),
    )(dy, u, xs, xs, wb, wct, ar8, ai8, dskip)


GELU_C = math.sqrt(2.0 / math.pi)


def _gelu(x):
    return 0.5 * x * (1.0 + jnp.tanh(GELU_C * (x + 0.044715 * x * x * x)))


def _gelu_grad(x):
    th = jnp.tanh(GELU_C * (x + 0.044715 * x * x * x))
    return 0.5 * (1.0 + th) + 0.5 * x * (1.0 - th * th) * GELU_C * (1.0 + 3.0 * 0.044715 * x * x)


def _ssm_post(ys, u, dskip, wglu, wso, name):
    bsz, seq, _ = ys.shape
    tm = _pick_tile(seq, (512, 256, 128))

    def body(ys_ref, u_ref, d_ref, wg_ref, wo_ref, s0_ref, z_ref, s1_ref, s2_ref, yb_ref):
        s0 = ys_ref[0] + d_ref[...] * u_ref[0]
        s1 = _gelu(s0)
        s1b = s1.astype(BF16)
        z = _dot(s1b, wg_ref[...])
        s2b = (s1 * _sigmoid(z)).astype(BF16)
        s0_ref[0] = s0
        z_ref[0] = z
        s1_ref[0] = s1b
        s2_ref[0] = s2b
        yb_ref[0] = _dot(s2b, wo_ref[...]).astype(BF16)

    return pl.pallas_call(
        body, name=name, grid=(bsz, seq // tm),
        in_specs=[_row(tm, SW), _row(tm, SW), _full2((1, SW)), _full2((SW, SW)), _full2((SW, D))],
        out_specs=[_row(tm, SW), _row(tm, SW), _row(tm, SW), _row(tm, SW), _row(tm, D)],
        out_shape=[SDS((bsz, seq, SW), F32), SDS((bsz, seq, SW), F32), SDS((bsz, seq, SW), BF16),
                   SDS((bsz, seq, SW), BF16), SDS((bsz, seq, D), BF16)],
        compiler_params=_cp("parallel", "parallel"),
    )(ys, u, dskip, wglu, wso)


def _ssm_post_bwd(dyb, s0, z, u, dskip, wglu, wso, name):
    bsz, seq, _ = s0.shape
    tm = _pick_tile(seq, (512, 256, 128))

    def body(dyb_ref, s0_ref, z_ref, u_ref, wg_ref, wo_ref, ds0_ref, dz_ref, dd_ref):
        b = pl.program_id(0)
        i = pl.program_id(1)
        ds2 = _dot_nt(dyb_ref[0], wo_ref[...])
        s0 = s0_ref[0]
        s1 = _gelu(s0)
        sg = _sigmoid(z_ref[0])
        dz = ds2 * s1 * sg * (1.0 - sg)
        dzb = dz.astype(BF16)
        ds1 = ds2 * sg + _dot_nt(dzb, wg_ref[...])
        ds0 = ds1 * _gelu_grad(s0)
        ds0_ref[0] = ds0
        dz_ref[0] = dzb
        part = jnp.sum(ds0 * u_ref[0], axis=0, keepdims=True)

        @pl.when((i == 0) & (b == 0))
        def _():
            dd_ref[...] = part

        @pl.when((i > 0) | (b > 0))
        def _():
            dd_ref[...] += part

    del dskip
    return pl.pallas_call(
        body, name=name, grid=(bsz, seq // tm),
        in_specs=[_row(tm, D), _row(tm, SW), _row(tm, SW), _row(tm, SW), _full2((SW, SW)), _full2((SW, D))],
        out_specs=[_row(tm, SW), _row(tm, SW), _full2((1, SW))],
        out_shape=[SDS((bsz, seq, SW), F32), SDS((bsz, seq, SW), BF16), SDS((1, SW), F32)],
        compiler_params=_cp("arbitrary", "arbitrary"),
    )(dyb, s0, z, u, wglu, wso)


def _merge_out(ya, yb, p3, wout, x1, gt, name):
    bsz, seq, _ = ya.shape
    tm = _pick_tile(seq, (512, 256, 128))

    def body(ya_ref, yb_ref, ga_ref, gbb_ref, w_ref, x_ref, gt_ref, mg_ref, mix_ref, xo_ref):
        merged = (_sigmoid(ga_ref[0].astype(F32)) * ya_ref[0].astype(F32)
                  + _sigmoid(gbb_ref[0].astype(F32)) * yb_ref[0].astype(F32)).astype(BF16)
        mix = _dot(merged, w_ref[...])
        mg_ref[0] = merged
        mix_ref[0] = mix.astype(BF16)
        xo_ref[0] = x_ref[0] + gt_ref[0] * mix

    return pl.pallas_call(
        body, name=name, grid=(bsz, seq // tm),
        in_specs=[_row(tm, D), _row(tm, D), _row(tm, D, 0), _row(tm, D, 1), _full2((D, D)), _row(tm, D), _seqvec(D)],
        out_specs=[_row(tm, D), _row(tm, D), _row(tm, D)],
        out_shape=[SDS((bsz, seq, D), BF16), SDS((bsz, seq, D), BF16), SDS((bsz, seq, D), F32)],
        compiler_params=_cp("parallel", "parallel"),
    )(ya, yb, p3, p3, wout, x1, gt)


def _merge_bwd(dx2, gt, mix, ya, yb, p3, wout, name):
    bsz, seq, _ = ya.shape
    tm = _pick_tile(seq, (512, 256, 128))

    def body(dx_ref, gt_ref, mix_ref, ya_ref, yb_ref, ga_ref, gbb_ref, w_ref, dmix_ref, dya_ref, dyb_ref, dp_ref, dgt_ref):
        i = pl.program_id(1)
        dx = dx_ref[0]
        dmix = (gt_ref[0] * dx).astype(BF16)
        dmix_ref[0] = dmix
        part = jnp.sum(dx * mix_ref[0].astype(F32), axis=0, keepdims=True)

        @pl.when(i == 0)
        def _():
            dgt_ref[0] = part

        @pl.when(i > 0)
        def _():
            dgt_ref[0] += part

        dmg = _dot_nt(dmix, w_ref[...])
        sa = _sigmoid(ga_ref[0].astype(F32))
        sb = _sigmoid(gbb_ref[0].astype(F32))
        dya_ref[0] = (dmg * sa).astype(BF16)
        dyb_ref[0] = (dmg * sb).astype(BF16)
        dp_ref[0, :, 0:D] = (dmg * ya_ref[0].astype(F32) * sa * (1.0 - sa)).astype(BF16)
        dp_ref[0, :, D:2 * D] = (dmg * yb_ref[0].astype(F32) * sb * (1.0 - sb)).astype(BF16)

    bshape = SDS((bsz, seq, D), BF16)
    return pl.pallas_call(
        body, name=name, grid=(bsz, seq // tm),
        in_specs=[_row(tm, D), _seqvec(D), _row(tm, D), _row(tm, D), _row(tm, D), _row(tm, D, 0), _row(tm, D, 1),
                  _full2((D, D))],
        out_specs=[_row(tm, D), _row(tm, D), _row(tm, D), _row(tm, 2 * D), _seqvec(D)],
        out_shape=[bshape, bshape, bshape, SDS((bsz, seq, 2 * D), BF16), SDS((bsz, 1, D), F32)],
        compiler_params=_cp("arbitrary", "arbitrary"),
    )(dx2, gt, mix, ya, yb, p3, p3, wout)


def _final_loss(x3, gfin, target, name):
    bsz, seq, dm = x3.shape
    tm = _pick_tile(seq, (512, 256, 128))

    def body(x_ref, g_ref, t_ref, dx_ref, loss_ref, dg_ref):
        b = pl.program_id(0)
        i = pl.program_id(1)
        xf = x_ref[0]
        gv = g_ref[...]
        r = lax.rsqrt(jnp.mean(xf * xf, axis=-1, keepdims=True) + EPS)
        xhat = xf * r
        e = xhat * gv - t_ref[0]
        dy = e * (1.0 / dm)
        dxh = dy * gv
        dx_ref[0] = r * (dxh - xhat * jnp.mean(dxh * xhat, axis=-1, keepdims=True))
        p_l = jnp.sum(e * e, axis=0, keepdims=True) * (0.5 / dm)
        p_g = jnp.sum(dy * xhat, axis=0, keepdims=True)

        @pl.when((i == 0) & (b == 0))
        def _():
            loss_ref[...] = p_l
            dg_ref[...] = p_g

        @pl.when((i > 0) | (b > 0))
        def _():
            loss_ref[...] += p_l
            dg_ref[...] += p_g

    return pl.pallas_call(
        body, name=name, grid=(bsz, seq // tm),
        in_specs=[_row(tm, dm), _full2((1, dm)), _row(tm, dm)],
        out_specs=[_row(tm, dm), _full2((1, dm)), _full2((1, dm))],
        out_shape=[SDS((bsz, seq, dm), F32), SDS((1, dm), F32), SDS((1, dm), F32)],
        compiler_params=_cp("arbitrary", "arbitrary"),
    )(x3, gfin, target)


def _ada_fwd(c_all, w_shard, b_shard):
    nb = c_all.shape[0]
    n = w_shard.shape[2]

    def body(c_ref, w_ref, b_ref, o_ref):
        cv = c_ref[...]
        cond = (cv * _sigmoid(cv)).astype(BF16)
        o_ref[...] = _dot(cond, w_ref[0].astype(BF16)) + b_ref[...]

    return pl.pallas_call(body, name="ada_fwd", out_shape=SDS((nb, n), F32), compiler_params=_cp())(
        c_all, w_shard, b_shard)


def _ada_bwd(c_all, dmod_shard, dmod_all):
    n = dmod_shard.shape[1]

    def body(c_ref, ds_ref, da_ref, gw_ref, gb_ref):
        cv = c_ref[...]
        cond = (cv * _sigmoid(cv)).astype(BF16)
        gw_ref[...] = _dot_tn(cond, ds_ref[...].astype(BF16))
        gb_ref[...] = jnp.sum(da_ref[...], axis=0, keepdims=True)

    return pl.pallas_call(
        body, name="ada_bwd", out_shape=[SDS((D, n), F32), SDS((1, dmod_all.shape[1]), F32)], compiler_params=_cp(),
    )(c_all, dmod_shard, dmod_all)


def _adamw_math(w, g, m, v):
    m = B1 * m + (1.0 - B1) * g
    v = B2 * v + (1.0 - B2) * (g * g)
    delta = -LR * ((m / BC1) / (jnp.sqrt(v / BC2) + AEPS) + WD * w)
    return delta, m, v


def _adamw_big(w, m, v, recv_own, recv_sib, name):
    _, rows, cols = w.shape
    tr = _pick_tile(rows, tuple(t for t in (512, 256, 128, 64, 32, 16, 8) if t * cols <= 192 * 1024))

    def body(w_ref, m_ref, v_ref, a_ref, b_ref, g_ref, d_ref, mo_ref, vo_ref):
        def chip_sum(r):
            acc = r[0].astype(F32)
            for k in range(1, N_CHIPS):
                acc = acc + r[k].astype(F32)
            return acc

        g = chip_sum(a_ref) + chip_sum(b_ref)
        delta, mn, vn = _adamw_math(w_ref[0], g, m_ref[0], v_ref[0])
        g_ref[0] = g
        d_ref[0] = delta
        mo_ref[0] = mn
        vo_ref[0] = vn

    own = pl.BlockSpec((1, tr, cols), lambda i: (0, i, 0))
    rspec = pl.BlockSpec((N_CHIPS, tr, cols), lambda i: (0, i, 0))
    shp = SDS(w.shape, F32)
    return pl.pallas_call(
        body, name=name, grid=(rows // tr,),
        in_specs=[own, own, own, rspec, rspec], out_specs=[own, own, own, own], out_shape=[shp, shp, shp, shp],
        compiler_params=_cp("parallel"),
    )(w, m, v, recv_own, recv_sib)


def _adamw_plain(w, m, v, g, name):
    def body(w_ref, m_ref, v_ref, g_ref, d_ref, mo_ref, vo_ref):
        delta, mn, vn = _adamw_math(w_ref[...], g_ref[...], m_ref[...], v_ref[...])
        d_ref[...] = delta
        mo_ref[...] = mn
        vo_ref[...] = vn

    shp = SDS(w.shape, F32)
    return pl.pallas_call(body, name=name, out_shape=[shp, shp, shp], compiler_params=_cp())(w, m, v, g)


def _adamw_rows(w, m, v, g, name):
    _, rows, cols = w.shape
    tr = _pick_tile(rows, (128, 64, 32, 16, 8))

    def body(w_ref, m_ref, v_ref, g_ref, d_ref, mo_ref, vo_ref):
        delta, mn, vn = _adamw_math(w_ref[0], g_ref[...], m_ref[0], v_ref[0])
        d_ref[0] = delta
        mo_ref[0] = mn
        vo_ref[0] = vn

    spec = pl.BlockSpec((1, tr, cols), lambda i: (0, i, 0))
    shp = SDS(w.shape, F32)
    return pl.pallas_call(
        body, name=name, grid=(rows // tr,), in_specs=[spec] * 3 + [pl.BlockSpec((tr, cols), lambda i: (i, 0))],
        out_specs=[spec] * 3, out_shape=[shp] * 3, compiler_params=_cp("parallel"),
    )(w, m, v, g)


def _sum_slabs(r, name):
    n, rows, cols = r.shape
    tr = _pick_tile(rows, (256, 128, 64))

    def body(r_ref, o_ref):
        acc = r_ref[0].astype(F32)
        for j in range(1, n):
            acc = acc + r_ref[j].astype(F32)
        o_ref[...] = acc

    return pl.pallas_call(
        body, name=name, grid=(rows // tr,),
        in_specs=[pl.BlockSpec((n, tr, cols), lambda i: (0, i, 0))],
        out_specs=pl.BlockSpec((tr, cols), lambda i: (i, 0)), out_shape=SDS((rows, cols), F32),
        compiler_params=_cp("parallel"),
    )(r)


def _place():
    return lax.axis_index("x"), lax.axis_index("y"), lax.axis_index("c")


def _all_gather8(blk, name):
    m_per, n = blk.shape

    def body(x_ref, out_ref, send_sems, recv_sems, local_sem):
        x, y, c = _place()
        me, sibling = (x, y, c), (x, y, 1 - c)
        chips = [(1 - x, y), (x, 1 - y), (1 - x, 1 - y)]

        def rows(px, py, pc):
            return out_ref.at[pl.ds((4 * px + 2 * py + pc) * m_per, m_per), :]

        def copy(k, block, to, src=None):
            return pltpu.make_async_remote_copy(
                src_ref=rows(*block) if src is None else src, dst_ref=rows(*block),
                send_sem=send_sems.at[k], recv_sem=recv_sems.at[k], device_id=to, device_id_type=MESH)

        mine = pltpu.make_async_copy(x_ref, rows(*me), local_sem)
        mine.start()
        first = [copy(0, me, sibling, src=x_ref)]
        first += [copy(1 + j, me, (*chip, c), src=x_ref) for j, chip in enumerate(chips)]
        for cp in first:
            cp.start()
        passed = [copy(4 + j, (*chip, c), sibling) for j, chip in enumerate(chips)]
        for j, chip in enumerate(chips):
            copy(1 + j, (*chip, c), me).wait_recv()
            passed[j].start()
        copy(0, sibling, me).wait_recv()
        for j, chip in enumerate(chips):
            copy(4 + j, (*chip, 1 - c), me).wait_recv()
        for cp in first + passed:
            cp.wait_send()
        mine.wait()

    return pl.pallas_call(
        body, name=name, out_shape=SDS((N_DEV * m_per, n), blk.dtype),
        in_specs=[pl.BlockSpec(memory_space=pltpu.VMEM)], out_specs=pl.BlockSpec(memory_space=pltpu.VMEM),
        scratch_shapes=[pltpu.SemaphoreType.DMA((7,)), pltpu.SemaphoreType.DMA((7,)), pltpu.SemaphoreType.DMA],
        compiler_params=pltpu.CompilerParams(vmem_limit_bytes=VMEM_LIMIT),
    )(blk)


def _chip_peers(x, y):
    return [(1 - x, y), (x, 1 - y), (1 - x, 1 - y)]


SIBLING = "sibling"


def _peer_copies(src_refs, land_refs, send_sems, recv_sems, scatter, landed):
    x, y, c = _place()
    if scatter == SIBLING:
        return [pltpu.make_async_remote_copy(
            src_ref=s, dst_ref=l, send_sem=send_sems.at[a], recv_sem=recv_sems.at[a],
            device_id=(x, y, 1 - c), device_id_type=MESH) for a, (s, l) in enumerate(zip(src_refs, land_refs))]
    cps = []
    for a, (src_ref, land_ref) in enumerate(zip(src_refs, land_refs)):
        for j, (px, py) in enumerate(_chip_peers(x, y)):
            if scatter:
                src = src_ref.at[2 * px + py]
            else:
                src = land_ref.at[2 * x + y] if src_ref is None else src_ref
            dst = land_ref.at[2 * px + py] if landed else land_ref.at[2 * x + y]
            cps.append(pltpu.make_async_remote_copy(
                src_ref=src, dst_ref=dst, send_sem=send_sems.at[3 * a + j], recv_sem=recv_sems.at[3 * a + j],
                device_id=(px, py, c), device_id_type=MESH))
    return cps


def _exchange_chips(srcs, scatter, name):
    n = len(srcs)

    def body(*refs):
        src_refs, land_refs = refs[:n], refs[n:2 * n]
        send_sems, recv_sems, local_sems = refs[2 * n:]
        x, y, _ = _place()
        me = 2 * x + y
        mine = [pltpu.make_async_copy(s.at[me] if scatter else s, l.at[me], local_sems.at[a])
                for a, (s, l) in enumerate(zip(src_refs, land_refs))]
        for cp in mine:
            cp.start()
        out = _peer_copies(src_refs, land_refs, send_sems, recv_sems, scatter, False)
        for cp in out:
            cp.start()
        for cp in _peer_copies(src_refs, land_refs, send_sems, recv_sems, scatter, True):
            cp.wait_recv()
        for cp in out:
            cp.wait_send()
        for cp in mine:
            cp.wait()

    anyspec = pl.BlockSpec(memory_space=pl.ANY)
    shapes = [SDS(s.shape if scatter else (N_CHIPS,) + s.shape, s.dtype) for s in srcs]
    return pl.pallas_call(
        body, name=name, out_shape=shapes, in_specs=[anyspec] * n, out_specs=[anyspec] * n,
        scratch_shapes=[pltpu.SemaphoreType.DMA((3 * n,)), pltpu.SemaphoreType.DMA((3 * n,)),
                        pltpu.SemaphoreType.DMA((n,))],
        compiler_params=pltpu.CompilerParams(vmem_limit_bytes=VMEM_LIMIT),
    )(*srcs)


_HBM = pl.BlockSpec(memory_space=pltpu.HBM)
_SEM = pl.BlockSpec(memory_space=pltpu.SEMAPHORE)
_EFFECT = pltpu.SideEffectType.DATAFLOW_SIDE_EFFECTING


def _exchange_begin(srcs, lands, scatter, name):
    srcs = tuple(srcs or ())
    n, ns = len(lands), len(srcs)
    nsem = n if scatter == SIBLING else 3 * n

    def body(*refs):
        src_refs = refs[:ns] if ns else (None,) * n
        land_refs = refs[ns:ns + n]
        send_sems, recv_sems = refs[ns + n:ns + n + 2]
        token = refs[-1]
        for cp in _peer_copies(src_refs, land_refs, send_sems, recv_sems, scatter, False):
            cp.start()
        token[...] = jnp.zeros_like(token)

    ops = (*srcs, *lands)
    res = pl.pallas_call(
        body, name=name,
        out_shape=(pltpu.SemaphoreType.DMA((nsem,)), pltpu.SemaphoreType.DMA((nsem,)),
                   *[pltpu.HBM(a.shape, a.dtype) for a in ops], SDS((SUB, LANE), F32)),
        in_specs=[_HBM] * len(ops), out_specs=(_SEM, _SEM, *[_HBM] * len(ops), pl.BlockSpec(memory_space=pltpu.VMEM)),
        input_output_aliases={i: 2 + i for i in range(len(ops))},
        compiler_params=pltpu.CompilerParams(has_side_effects=_EFFECT),
    )(*[pltpu.with_memory_space_constraint(a, pltpu.HBM) for a in ops])
    return res[0], res[1], res[2:2 + ns], res[2 + ns:2 + ns + n], res[-1]


def _exchange_end(handle, after, scatter, name, with_srcs=False):
    send_sems, recv_sems, srcs, lands, _ = handle
    n, ns = len(lands), len(srcs)

    def body(*refs):
        src_refs = refs[:ns] if ns else (None,) * n
        land_refs = refs[ns:ns + n]
        send_sems, recv_sems = refs[ns + n:ns + n + 2]
        for cp in _peer_copies(src_refs, land_refs, send_sems, recv_sems, scatter, True):
            cp.wait_send()
            cp.wait_recv()

    ops = (*srcs, *lands)
    res = pl.pallas_call(
        body, name=name,
        out_shape=tuple(pltpu.HBM(a.shape, a.dtype) for a in ops),
        in_specs=[_HBM] * len(ops) + [_SEM, _SEM, pl.BlockSpec(memory_space=pl.ANY)], out_specs=tuple([_HBM] * len(ops)),
        input_output_aliases={i: i for i in range(len(ops))},
        compiler_params=pltpu.CompilerParams(has_side_effects=_EFFECT),
    )(*ops, send_sems, recv_sems, after)
    return (list(res[:ns]), list(res[ns:])) if with_srcs else list(res[ns:])


def _own_slab(stack4, chip):
    idx = lax.broadcasted_iota(jnp.int32, (N_CHIPS,) + (1,) * (stack4.ndim - 1), 0)
    return jnp.where(idx == chip, stack4, jnp.zeros((), stack4.dtype))


def _swap_sibling(vs, name):
    n = len(vs)

    def body(*refs):
        in_refs, out_refs = refs[:n], refs[n:2 * n]
        send_sems, recv_sems = refs[2 * n:]
        x, y, c = _place()
        cps = [pltpu.make_async_remote_copy(
            src_ref=i, dst_ref=o, send_sem=send_sems.at[a], recv_sem=recv_sems.at[a],
            device_id=(x, y, 1 - c), device_id_type=MESH) for a, (i, o) in enumerate(zip(in_refs, out_refs))]
        for cp in cps:
            cp.start()
        for cp in cps:
            cp.wait()

    anyspec = pl.BlockSpec(memory_space=pl.ANY)
    return pl.pallas_call(
        body, name=name, out_shape=[SDS(v.shape, v.dtype) for v in vs], in_specs=[anyspec] * n, out_specs=[anyspec] * n,
        scratch_shapes=[pltpu.SemaphoreType.DMA((n,)), pltpu.SemaphoreType.DMA((n,))],
        compiler_params=pltpu.CompilerParams(vmem_limit_bytes=VMEM_LIMIT),
    )(*vs)


def _select(stacked, idx):
    out = stacked[0]
    for j in range(1, stacked.shape[0]):
        out = jnp.where(idx == j, stacked[j], out)
    return out


BIG = (
    ("w1_a", DFF // 4, D, False), ("w3_a", DFF // 4, D, False), ("w2_a", DFF // 4, D, False),
    ("w_in", D, 5632 // 4, True), ("w_conv_out", CW // 4, D, False), ("w_glu", SW // 4, SW, False),
    ("w_ssm_out", SW, D // 4, True), ("w_out", D // 4, D, False),
    ("w1_b", DFF // 4, D, False), ("w3_b", DFF // 4, D, False), ("w2_b", DFF // 4, D, False),
)
TRANSPOSED = frozenset(("w1_a", "w3_a", "w1_b", "w3_b"))
PACK_COLS = 1024


def _view(a, name):
    return jnp.transpose(a, (0, 2, 1)) if name in TRANSPOSED else a


def _full_from_stacked(st, split_cols):
    _, rows, cols = st.shape
    if split_cols:
        return st.transpose(1, 0, 2).reshape(rows, N_CHIPS * cols)
    return st.reshape(N_CHIPS * rows, cols)


def _stacked_from_full(full, rows, cols, split_cols):
    if split_cols:
        return full.reshape(rows, N_CHIPS, cols).transpose(1, 0, 2)
    return full.reshape(N_CHIPS, rows, cols)


def _blockdiag(t):
    r = lax.broadcasted_iota(jnp.int32, (SW, GP), 0) // NH
    cidx = lax.broadcasted_iota(jnp.int32, (SW, GP), 1) // NP
    dense = jnp.where(r == cidx, jnp.tile(t, (NG, 1)), 0.0)
    ub, sb = SW // SSM_SUPER, GP // SSM_SUPER
    return jnp.concatenate([dense[s * ub:(s + 1) * ub, s * sb:(s + 1) * sb] for s in range(SSM_SUPER)], axis=0)


def _blockdiag_extract(acc):
    gs = NG // SSM_SUPER
    a = acc.reshape(NG, NH, gs, NP)
    sel = (lax.broadcasted_iota(jnp.int32, (NG, 1, gs, 1), 0) % gs) == lax.broadcasted_iota(jnp.int32, (NG, 1, gs, 1), 2)
    a = jnp.sum(jnp.where(sel, a, 0.0), axis=2)
    return a.transpose(1, 0, 2).reshape(NH, GP)


def _to_t(p):
    return p.transpose(2, 0, 1).reshape(NH, GP)


def _from_t(t):
    return t.reshape(NH, NG, NP).transpose(1, 2, 0)


def _c_to_t(p):
    return p.transpose(1, 0, 2).reshape(NH, GP)


def _c_from_t(t):
    return t.reshape(NH, NG, NP).transpose(1, 0, 2)


def _ffn_forward(x, g, sh, sc, gt, w1, w3, w2, tag):
    h = _norm_mod(x, g, sh, sc, f"{tag}_norm")
    if callable(w1):
        w1, w3 = w1(h)
    a, b, hid = _swiglu_up(h, w1, w3, f"{tag}_up")
    w2 = w2(hid) if callable(w2) else w2
    f, xo = _ffn_down(hid, w2, x, gt, f"{tag}_down")
    return xo, (x, h, a, b, hid, f), w2


def _ffn_backward(dxo, saved, g, sc, gt, w1, w3, w2, tag, emit=lambda key, gw: 0.0):
    x, h, a, b, hid, f = saved
    dfs, da, db, dgt = _ffn_bwd_hid(dxo, gt, f, a, b, w2, f"{tag}_bwd_hid")
    h2 = _flat(h)
    gw2 = _mm(_flat(hid), _flat(dfs), ta=True, out_dtype=BF16, name=f"{tag}_gw2")
    tok = emit("w2", gw2)
    gw1 = _mm(_flat(da), h2, ta=True, out_dtype=BF16, name=f"{tag}_gw1")
    tok = tok + emit("w1", gw1)
    gw3 = _mm(_flat(db), h2, ta=True, out_dtype=BF16, name=f"{tag}_gw3")
    tok = tok + emit("w3", gw3)
    dx, dsh, dsc, dg = _dh_norm_bwd([da, db], [w1, w3], x, g, sc + tok, dxo, f"{tag}_bwd_dh", transposed=True)
    return dx, (dsh, dsc, dgt, dg), (gw1, gw3, gw2)


def kernel(x, c, w_ada, b_ada, g_ffn1, w1_a, w3_a, w2_a, g_mix, w_in, conv_w, w_conv_out, a_re, a_im, b_re, b_im, c_re, c_im, log_dt, d_skip, w_glu, w_ssm_out, w_out, g_ffn2, w1_b, w3_b, w2_b, g_final, loss_target, m_w_ada, m_b_ada, m_g_ffn1, m_w1_a, m_w3_a, m_w2_a, m_g_mix, m_w_in, m_conv_w, m_w_conv_out, m_a_re, m_a_im, m_b_re, m_b_im, m_c_re, m_c_im, m_log_dt, m_d_skip, m_w_glu, m_w_ssm_out, m_w_out, m_g_ffn2, m_w1_b, m_w3_b, m_w2_b, m_g_final, v_w_ada, v_b_ada, v_g_ffn1, v_w1_a, v_w3_a, v_w2_a, v_g_mix, v_w_in, v_conv_w, v_w_conv_out, v_a_re, v_a_im, v_b_re, v_b_im, v_c_re, v_c_im, v_log_dt, v_d_skip, v_w_glu, v_w_ssm_out, v_w_out, v_g_ffn2, v_w1_b, v_w3_b, v_w2_b, v_g_final):
    args = dict(locals())
    names = ["w_ada", "b_ada", "g_ffn1", "w1_a", "w3_a", "w2_a", "g_mix", "w_in", "conv_w", "w_conv_out", "a_re",
             "a_im", "b_re", "b_im", "c_re", "c_im", "log_dt", "d_skip", "w_glu", "w_ssm_out", "w_out", "g_ffn2",
             "w1_b", "w3_b", "w2_b", "g_final"]
    bsz, seq, _ = x.shape
    mx, my, mc = _place()
    chip = 2 * mx + my
    dev = 4 * mx + 2 * my + mc

    groups = (BIG[:3], BIG[3:8], BIG[8:])
    wfull = {}

    def shards(grp):
        return [_view(args[n], n)[0] for n, _, _, _ in grp]

    def unpack_group(gathered, grp):
        for (n, _, _, split), st in zip(grp, gathered):
            wfull[n] = _full_from_stacked(st, split)

    up_grp, down_grp = groups[0][:2], groups[0][2:]

    nmod_shard = NMOD * D // N_CHIPS
    c_all = _all_gather8(c.reshape(SUB, -1), "gather_c").reshape(N_DEV * bsz, D)
    b_shard = _select(b_ada.reshape(N_CHIPS, 1, nmod_shard), chip)
    mod_shard = _ada_fwd(c_all, w_ada, b_shard)
    nb = N_DEV * bsz
    cw_pad = jnp.pad(conv_w[0], ((0, SUB - 3), (0, nmod_shard - CW // N_CHIPS)))
    mod_st = _exchange_chips([jnp.concatenate([mod_shard, cw_pad], axis=0)], False, "gather_mod")[0]
    mod_all = mod_st[:, :nb].transpose(1, 0, 2).reshape(N_DEV, bsz, NMOD * D)
    mod = _select(mod_all, dev)

    def gather_begin(raw, name):
        lands = [_own_slab(jnp.broadcast_to(a.astype(BF16)[None], (N_CHIPS,) + a.shape), chip) for a in raw]
        return _exchange_begin(None, lands, False, name)

    up_raw, mod = lax.optimization_barrier((shards(up_grp), mod))
    up_handle = gather_begin(up_raw, "gather_w_ffn1_up_start")
    (down_raw, mix_raw, ffn2_raw), up_token = lax.optimization_barrier(
        ((shards(down_grp), shards(groups[1]), shards(groups[2])), up_handle[4][0:1, 0:1]))
    down_handle = gather_begin(down_raw, "gather_w_ffn1_down_start")
    mix_handle = gather_begin(mix_raw, "gather_w_mix_start")
    ffn2_handle = gather_begin(ffn2_raw, "gather_w_ffn2_start")
    start_tokens = up_token + down_handle[4][0:1, 0:1] + mix_handle[4][0:1, 0:1] + ffn2_handle[4][0:1, 0:1]

    sh1, sc1, gt1, sh2, sc2, gt2, sh3, sc3, gt3 = [mod[:, None, j * D:(j + 1) * D] for j in range(NMOD)]
    convw = mod_st[:, nb:nb + 3, :CW // N_CHIPS].transpose(1, 0, 2).reshape(3, CW)
    convw8 = jnp.pad(convw, ((0, SUB - 3), (0, 0)))

    are, aim = a_re.reshape(1, GP), a_im.reshape(1, GP)
    ldt = jnp.broadcast_to(log_dt.reshape(NG, 1), (NG, NP)).reshape(1, GP)
    bre_t, bim_t = _to_t(b_re[0]), _to_t(b_im[0])
    abr, abi, bbr_t, bbi_t = _ssm_disc(are, aim, ldt, bre_t, bim_t)
    wb = jnp.concatenate([_blockdiag(bbr_t), _blockdiag(bbi_t)], axis=1).astype(BF16)
    wct = jnp.concatenate([_blockdiag(_c_to_t(c_re[0])), _blockdiag(-_c_to_t(c_im[0]))], axis=1).astype(BF16)
    ar8 = jnp.broadcast_to(abr, (SUB, GP))
    ai8 = jnp.broadcast_to(abi, (SUB, GP))

    def late_w2a(hid):
        unpack_group(_exchange_end(down_handle, hid, False, "gather_w_ffn1_down_wait"), down_grp)
        return wfull["w2_a"]

    def late_w13a(h):
        unpack_group(_exchange_end(up_handle, h, False, "gather_w_ffn1_up_wait"), up_grp)
        return wfull["w1_a"], wfull["w3_a"]

    x1, ffn1_saved, _ = _ffn_forward(x, g_ffn1 + start_tokens, sh1, sc1, gt1, late_w13a, None, late_w2a, "ffn1")
    unpack_group(_exchange_end(mix_handle, x1, False, "gather_w_mix_wait"), groups[1])

    h2 = _norm_mod(x1, g_mix, sh2, sc2, "mix_norm")
    h2f = _flat(h2)
    win = wfull["w_in"]
    win1, winu, win3 = win[:, :3 * CW], win[:, 3 * CW:3 * CW + SW], win[:, 3 * CW + SW:]
    p1 = _mm(h2f, win1, out_dtype=BF16, name="mix_in1").reshape(bsz, seq, 3 * CW)
    u = _mm(h2f, winu, name="mix_inu").reshape(bsz, seq, SW)
    p3 = _mm(h2f, win3, out_dtype=BF16, name="mix_in3").reshape(bsz, seq, 2 * D)

    ya_in = _conv_fwd(p1, convw8, "conv_fwd")
    ya = _mm(_flat(ya_in), wfull["w_conv_out"], out_dtype=BF16, name="conv_out").reshape(bsz, seq, D)

    xs, ys = _ssm_fwd(u, wb, wct, ar8, ai8, "ssm_fwd")
    s0, z, s1, s2, yb = _ssm_post(ys, u, d_skip, wfull["w_glu"], wfull["w_ssm_out"], "ssm_post")

    merged, mix, x2 = _merge_out(ya, yb, p3, wfull["w_out"], x1, gt2, "merge_out")
    unpack_group(_exchange_end(ffn2_handle, x2, False, "gather_w_ffn2_wait"), groups[2])
    x3, ffn2_saved, _ = _ffn_forward(x2, g_ffn2, sh3, sc3, gt3, wfull["w1_b"], wfull["w3_b"], wfull["w2_b"], "ffn2")

    dx3, lossvec, dgfin = _final_loss(x3, g_final.reshape(1, D), loss_target, "final_loss")
    loss = lax.psum(jnp.sum(lossvec), ("x", "y", "c"))

    gfull = {}
    dx2, (dsh3, dsc3, dgt3, dg3), (gfull["w1_b"], gfull["w3_b"], gfull["w2_b"]) = _ffn_backward(
        dx3, ffn2_saved, g_ffn2, sc3, gt3, wfull["w1_b"], wfull["w3_b"], wfull["w2_b"], "ffn2")

    def stack_group(grp):
        return [_stacked_from_full(gfull[n], rows, cols, split) for n, rows, cols, split in grp]

    st_ffn2 = stack_group(groups[2])
    h_ffn2 = _exchange_begin(st_ffn2, [_own_slab(s, chip) for s in st_ffn2], True, "scatter_ffn2_start")

    dmix, dya, dyb, dp3, dgt2 = _merge_bwd(dx2, gt2 + h_ffn2[4][0, 0], mix, ya, yb, p3, wfull["w_out"], "merge_bwd")
    gfull["w_out"] = _mm(_flat(merged), _flat(dmix), ta=True, out_dtype=BF16, name="gw_out")
    dp1, dconvw8 = _conv_bwd(dya, wfull["w_conv_out"], p1, convw8, "conv_bwd")
    gfull["w_conv_out"] = _mm(_flat(ya_in), _flat(dya), ta=True, out_dtype=BF16, name="gw_conv_out")
    ds0, dz, ddskip = _ssm_post_bwd(dyb, s0, z, u, d_skip, wfull["w_glu"], wfull["w_ssm_out"], "ssm_post_bwd")
    gfull["w_ssm_out"] = _mm(_flat(s2), _flat(dyb), ta=True, out_dtype=BF16, name="gw_ssm_out")
    gfull["w_glu"] = _mm(_flat(s1), _flat(dz), ta=True, out_dtype=BF16, name="gw_glu")
    du, dwb, dwct, dar8, dai8 = _ssm_bwd(ds0, u, xs, wb, wct, ar8, ai8, d_skip, "ssm_bwd")
    dx1, dsh2, dsc2, dgmix = _dh_norm_bwd([dp1, du, dp3], [win1, winu, win3], x1, g_mix, sc2, dx2, "mix_bwd_dh")
    gfull["w_in"] = jnp.concatenate([
        _mm(h2f, _flat(dp1), ta=True, out_dtype=BF16, name="gw_in1"), _mm(h2f, _flat(du), ta=True, out_dtype=BF16, name="gw_inu"),
        _mm(h2f, _flat(dp3), ta=True, out_dtype=BF16, name="gw_in3")], axis=1)

    st_mix = stack_group(groups[1])
    h_mix = _exchange_begin(st_mix, [_own_slab(s, chip) for s in st_mix], True, "scatter_mix_start")

    def swap_begin(recv, name):
        return _exchange_begin(recv, [lax.empty(v.shape, v.dtype) for v in recv], SIBLING, name)

    recv_ffn2 = _exchange_end(h_ffn2, dx1, True, "scatter_ffn2_wait")
    sw_ffn2 = swap_begin(recv_ffn2, "swap_ffn2_start")

    ffn1_names = {"w1": BIG[0], "w3": BIG[1], "w2": BIG[2]}
    ffn1_handles = {}

    def emit_ffn1(key, gw):
        n, rows, cols, split = ffn1_names[key]
        st = _stacked_from_full(gw, rows, cols, split)
        ffn1_handles[n] = _exchange_begin([st], [_own_slab(st, chip)], True, f"scatter_ffn1_{key}_start")
        return ffn1_handles[n][4][0, 0]

    grad_x, (dsh1, dsc1, dgt1, dg1), _ = _ffn_backward(
        dx1, ffn1_saved, g_ffn1, sc1, gt1 + (h_mix[4][0, 0] + sw_ffn2[4][0, 0]), wfull["w1_a"], wfull["w3_a"],
        wfull["w2_a"], "ffn1", emit=emit_ffn1)
    recv_mix = _exchange_end(h_mix, grad_x, True, "scatter_mix_wait")
    sw_mix = swap_begin(recv_mix, "swap_mix_start")
    dg1 = dg1 + sw_mix[4][0, 0]

    sbw = GP // SSM_SUPER
    d_are, d_aim, d_ldt, d_bre_t, d_bim_t = _ssm_disc_bwd(
        are, aim, ldt, bre_t, bim_t, jnp.sum(dar8, axis=0, keepdims=True), jnp.sum(dai8, axis=0, keepdims=True),
        _blockdiag_extract(dwb[:, :sbw]), _blockdiag_extract(dwb[:, sbw:]))
    d_cre = _c_from_t(_blockdiag_extract(dwct[:, :sbw]))
    d_cim = -_c_from_t(_blockdiag_extract(dwct[:, sbw:]))

    small_parts = [dg1, dgmix, dg3, dgfin, dconvw8[:3], d_are, d_aim, _from_t(d_bre_t), _from_t(d_bim_t), d_cre, d_cim,
                   jnp.sum(d_ldt.reshape(NG, NP), axis=1), ddskip]
    small_sizes = [int(p.size) for p in small_parts]
    n_small = sum(small_sizes)
    n_small_pad = -(-n_small // (SUB * PACK_COLS)) * (SUB * PACK_COLS)
    dmod = jnp.concatenate([dsh1, dsc1, dgt1, dsh2, dsc2, dgt2, dsh3, dsc3, dgt3], axis=2).reshape(bsz * NMOD * D)
    flat = jnp.concatenate([p.reshape(-1) for p in small_parts] + [jnp.zeros((n_small_pad - n_small,), F32), dmod])
    grads, deltas, new_m, new_v = {}, {}, {}, {}

    def adamw_group(grp, recv_own, recv_sib, token):
        for (n, _, _, _), r_own, r_sib in zip(grp, recv_own, recv_sib):
            wmv, _ = lax.optimization_barrier(
                ((_view(args[n], n), _view(args["m_" + n], n), _view(args["v_" + n], n)), token))
            res = _adamw_big(*wmv, r_own, r_sib, f"adamw_{n}")
            grads[n], deltas[n], new_m[n], new_v[n] = [_view(r, n) for r in res]
        return deltas[grp[-1][0]]

    flat2d = flat.reshape(-1, PACK_COLS)
    h_small = _exchange_begin(None, [_own_slab(jnp.broadcast_to(flat2d[None], (N_CHIPS,) + flat2d.shape), chip)],
                              False, "gather_small_chips_start")
    recv_ffn2, sib_ffn2 = _exchange_end(sw_ffn2, flat2d, SIBLING, "swap_ffn2_wait", with_srcs=True)
    done = adamw_group(groups[2], recv_ffn2, sib_ffn2, h_small[4])
    own_c = _exchange_end(h_small, done, False, "gather_small_chips_wait")[0]
    h_small2 = _exchange_begin([own_c], [lax.empty(own_c.shape, own_c.dtype)], SIBLING, "gather_small_sib_start")
    recv_mix, sib_mix = _exchange_end(sw_mix, own_c, SIBLING, "swap_mix_wait", with_srcs=True)
    done = adamw_group(groups[1], recv_mix, sib_mix, h_small2[4])
    (own_c,), (sib_c,) = _exchange_end(h_small2, done, SIBLING, "gather_small_sib_wait", with_srcs=True)
    core0 = jnp.where(mc == 0, own_c, sib_c)
    core1 = jnp.where(mc == 0, sib_c, own_c)
    allg = jnp.stack([core0, core1], axis=1).reshape(N_DEV, -1)
    small = _sum_slabs(allg[:, :n_small_pad].reshape(N_DEV, -1, PACK_COLS), "sum_small").reshape(-1)
    sg, o = [], 0
    for p, sz in zip(small_parts, small_sizes):
        sg.append(small[o:o + sz].reshape(p.shape))
        o += sz
    (g_g1, g_gmix, g_g3, g_gfin, g_convw, g_are, g_aim, g_bre, g_bim, g_cre, g_cim, g_ldt, g_dskip) = sg

    dmod_all = allg[:, n_small_pad:].reshape(nb, NMOD * D)
    dmod_shard = _select(dmod_all.reshape(nb, N_CHIPS, nmod_shard).transpose(1, 0, 2), chip)
    g_wada, g_bada = _ada_bwd(c_all, dmod_shard, dmod_all)

    recv_ffn1 = [_exchange_end(ffn1_handles[n], g_wada, True, f"scatter_ffn1_{n}_wait")[0] for n, _, _, _ in groups[0]]
    sib_ffn1 = _swap_sibling(recv_ffn1, "swap_ffn1")

    adamw_group(groups[0], recv_ffn1, sib_ffn1, jnp.zeros((), F32))
    grads["w_ada"] = g_wada[None]
    deltas["w_ada"], new_m["w_ada"], new_v["w_ada"] = _adamw_rows(w_ada, m_w_ada, v_w_ada, g_wada, "adamw_w_ada")

    g_convw_shard = _select(g_convw.reshape(3, N_CHIPS, CW // N_CHIPS).transpose(1, 0, 2), chip)
    small_g = {"b_ada": g_bada, "g_ffn1": g_g1, "g_mix": g_gmix, "g_ffn2": g_g3, "g_final": g_gfin,
               "conv_w": g_convw_shard, "a_re": g_are, "a_im": g_aim, "b_re": g_bre, "b_im": g_bim,
               "c_re": g_cre, "c_im": g_cim, "log_dt": g_ldt, "d_skip": g_dskip}
    small_names = list(small_g)
    sizes = [int(args[n].size) for n in small_names]
    tot = sum(sizes)
    tot_pad = -(-tot // (SUB * PACK_COLS)) * (SUB * PACK_COLS)

    def pack(get):
        return jnp.concatenate([get(n).reshape(-1) for n in small_names] + [jnp.zeros((tot_pad - tot,), F32)]).reshape(
            -1, PACK_COLS)

    res = _adamw_plain(pack(lambda n: args[n]), pack(lambda n: args["m_" + n]), pack(lambda n: args["v_" + n]),
                       pack(lambda n: small_g[n]), "adamw_small")
    o = 0
    for n, sz in zip(small_names, sizes):
        shp = args[n].shape
        grads[n] = small_g[n].reshape(shp)
        deltas[n], new_m[n], new_v[n] = [r.reshape(-1)[o:o + sz].reshape(shp) for r in res]
        o += sz

    return (loss, grad_x, *[grads[n] for n in names], *[deltas[n] for n in names],
            *[new_m[n] for n in names], *[new_v[n] for n in names])
```

```python
import math

import jax
import jax.numpy as jnp
from jax import lax
from jax.experimental import pallas as pl
from jax.experimental.pallas import tpu as pltpu

F32 = jnp.float32
BF16 = jnp.bfloat16
SDS = jax.ShapeDtypeStruct
MESH = pl.DeviceIdType.MESH

D = 1024
DFF = 2816
CW = 1024
SW = 512
NG, NP, NH = 32, 64, 16
GP = NG * NP
NMOD = 9
EPS = 1e-6
N_CHIPS = 4
N_DEV = 8
SUB = 8
LANE = 128
SSM_SUPER = 4
VMEM_LIMIT = 50 * 1024 * 1024

LR, B1, B2, AEPS, WD, STEP = 0.001, 0.9, 0.999, 1e-08, 0.01, 10
BC1 = 1.0 - B1 ** STEP
BC2 = 1.0 - B2 ** STEP


def _cp(*sem):
    return pltpu.CompilerParams(dimension_semantics=sem or None, vmem_limit_bytes=VMEM_LIMIT)


def _pick_tile(n, cands):
    for t in cands:
        if t <= n and n % t == 0:
            return t
    return n


def _dot(a, b):
    return lax.dot_general(a, b, (((1,), (0,)), ((), ())), preferred_element_type=F32)


def _dot_nt(a, b):
    return lax.dot_general(a, b, (((1,), (1,)), ((), ())), preferred_element_type=F32)


def _dot_tn(a, b):
    return lax.dot_general(a, b, (((0,), (0,)), ((), ())), preferred_element_type=F32)


def _row(tm, width, col=0):
    return pl.BlockSpec((1, tm, width), lambda b, i, *_: (b, i, col))


def _seqvec(width):
    return pl.BlockSpec((1, 1, width), lambda b, *_: (b, 0, 0))


def _full2(shape):
    return pl.BlockSpec(shape, lambda *_: (0, 0))


def _sigmoid(x):
    return jax.nn.sigmoid(x)


def _mm(a, b, *, ta=False, tb=False, out_dtype=F32, name):
    if ta:
        kdim, m = a.shape
    else:
        m, kdim = a.shape
    n = b.shape[0] if tb else b.shape[1]
    tm = _pick_tile(m, (1408, 1024, 512, 256, 128))
    tn = _pick_tile(n, (1408, 1024, 512, 256, 128))
    tk = _pick_tile(kdim, (1024, 512, 256, 128))
    nk = kdim // tk

    def body(a_ref, b_ref, o_ref, acc_ref):
        k = pl.program_id(2)

        @pl.when(k == 0)
        def _():
            acc_ref[...] = jnp.zeros_like(acc_ref)

        av = a_ref[...].astype(BF16)
        bv = b_ref[...].astype(BF16)
        dn = (((0 if ta else 1,), (1 if tb else 0,)), ((), ()))
        acc_ref[...] += lax.dot_general(av, bv, dn, preferred_element_type=F32)

        @pl.when(k == nk - 1)
        def _():
            o_ref[...] = acc_ref[...].astype(out_dtype)

    a_spec = pl.BlockSpec((tk, tm), lambda i, j, k: (k, i)) if ta else pl.BlockSpec((tm, tk), lambda i, j, k: (i, k))
    b_spec = pl.BlockSpec((tn, tk), lambda i, j, k: (j, k)) if tb else pl.BlockSpec((tk, tn), lambda i, j, k: (k, j))
    return pl.pallas_call(
        body, name=name, grid=(m // tm, n // tn, nk),
        in_specs=[a_spec, b_spec],
        out_specs=pl.BlockSpec((tm, tn), lambda i, j, k: (i, j)),
        out_shape=SDS((m, n), out_dtype),
        scratch_shapes=[pltpu.VMEM((tm, tn), F32)],
        compiler_params=_cp("parallel", "parallel", "arbitrary"),
    )(a, b)


def _flat(a):
    return a.reshape(-1, a.shape[-1])


def _norm_mod(x, g, sh, sc, name):
    bsz, seq, dm = x.shape
    tm = _pick_tile(seq, (512, 256, 128))

    def body(x_ref, g_ref, sh_ref, sc_ref, o_ref):
        xf = x_ref[0]
        r = lax.rsqrt(jnp.mean(xf * xf, axis=-1, keepdims=True) + EPS)
        hn = xf * r * g_ref[...]
        o_ref[0] = (hn * (1.0 + sc_ref[0]) + sh_ref[0]).astype(BF16)

    return pl.pallas_call(
        body, name=name, grid=(bsz, seq // tm),
        in_specs=[_row(tm, dm), _full2((1, dm)), _seqvec(dm), _seqvec(dm)],
        out_specs=_row(tm, dm), out_shape=SDS((bsz, seq, dm), BF16),
        compiler_params=_cp("parallel", "parallel"),
    )(x, g, sh, sc)


def _swiglu_up(h, w1, w3, name):
    bsz, seq, dm = h.shape
    nf = w1.shape[0]
    tm = _pick_tile(seq, (512, 256, 128))
    tn = _pick_tile(nf, (1408, 512, 256, 128))

    def body(h_ref, w1_ref, w3_ref, a_ref, b_ref, hid_ref):
        hv = h_ref[0]
        a = _dot_nt(hv, w1_ref[...])
        b = _dot_nt(hv, w3_ref[...])
        sg = _sigmoid(a)
        sa = a * sg
        a_ref[0] = (b * (sg * (1.0 + a * (1.0 - sg)))).astype(BF16)
        b_ref[0] = sa.astype(BF16)
        hid_ref[0] = (sa * b).astype(BF16)

    wspec = pl.BlockSpec((tn, dm), lambda n, b, i: (n, 0))
    ospec = pl.BlockSpec((1, tm, tn), lambda n, b, i: (b, i, n))
    shp = SDS((bsz, seq, nf), BF16)
    return pl.pallas_call(
        body, name=name, grid=(nf // tn, bsz, seq // tm),
        in_specs=[pl.BlockSpec((1, tm, dm), lambda n, b, i: (b, i, 0)), wspec, wspec],
        out_specs=[ospec, ospec, ospec], out_shape=[shp, shp, shp],
        compiler_params=_cp("parallel", "parallel", "parallel"),
    )(h, w1, w3)


def _ffn_down(hid, w2, x, gt, name):
    bsz, seq, nf = hid.shape
    dm = w2.shape[1]
    tm = _pick_tile(seq, (512, 256, 128))

    def body(hid_ref, w2_ref, x_ref, gt_ref, f_ref, xo_ref):
        f = _dot(hid_ref[0], w2_ref[...])
        f_ref[0] = f.astype(BF16)
        xo_ref[0] = x_ref[0] + 0.5 * gt_ref[0] * f

    shp = SDS((bsz, seq, dm), F32)
    return pl.pallas_call(
        body, name=name, grid=(bsz, seq // tm),
        in_specs=[_row(tm, nf), _full2((nf, dm)), _row(tm, dm), _seqvec(dm)],
        out_specs=[_row(tm, dm), _row(tm, dm)], out_shape=[SDS((bsz, seq, dm), BF16), shp],
        compiler_params=_cp("parallel", "parallel"),
    )(hid, w2, x, gt)


def _ffn_bwd_hid(dxo, gt, f, a, b, w2, name):
    bsz, seq, dm = dxo.shape
    nf = a.shape[2]
    tm = _pick_tile(seq, (512, 256, 128))
    tn = _pick_tile(nf, (1408, 512, 256, 128))

    def body(dxo_ref, gt_ref, f_ref, a_ref, b_ref, w2_ref, dfs_ref, da_ref, db_ref, dgt_ref):
        i = pl.program_id(1)
        n = pl.program_id(2)

        @pl.when(n == 0)
        def _():
            dxo = dxo_ref[0]
            dfs_ref[0] = (0.5 * gt_ref[0] * dxo).astype(BF16)
            part = jnp.sum(0.5 * dxo * f_ref[0].astype(F32), axis=0, keepdims=True)

            @pl.when(i == 0)
            def _():
                dgt_ref[0] = part

            @pl.when(i > 0)
            def _():
                dgt_ref[0] += part

        dhid = _dot_nt(dfs_ref[0], w2_ref[pl.ds(pl.multiple_of(n * tn, tn), tn), :])
        dh16 = dhid.astype(BF16)
        da_ref[0] = dh16 * a_ref[0]
        db_ref[0] = dh16 * b_ref[0]

    hspec = pl.BlockSpec((1, tm, tn), lambda b, i, n: (b, i, n))
    return pl.pallas_call(
        body, name=name, grid=(bsz, seq // tm, nf // tn),
        in_specs=[_row(tm, dm), _seqvec(dm), _row(tm, dm), hspec, hspec, _full2((nf, dm))],
        out_specs=[_row(tm, dm), hspec, hspec, _seqvec(dm)],
        out_shape=[SDS((bsz, seq, dm), BF16), SDS((bsz, seq, nf), BF16), SDS((bsz, seq, nf), BF16),
                   SDS((bsz, 1, dm), F32)],
        compiler_params=_cp("arbitrary", "arbitrary", "arbitrary"),
    )(dxo, gt, f, a, b, w2)


def _dh_norm_bwd(pieces, weights, x, g, sc, dxo, name, transposed=False):
    bsz, seq, dm = x.shape
    tm = _pick_tile(seq, (512, 256, 128))
    npc = len(pieces)
    dot = _dot if transposed else _dot_nt

    def body(*refs):
        p_refs = refs[:npc]
        w_hbm = refs[npc:2 * npc]
        x_ref, g_ref, sc_ref, dxo_ref, dx_ref, dsh_ref, dsc_ref, dg_ref = refs[2 * npc:2 * npc + 8]
        w_refs = refs[2 * npc + 8:]
        b = pl.program_id(0)
        i = pl.program_id(1)

        @pl.when((i == 0) & (b == 0))
        def _():
            for src, dst in zip(w_hbm, w_refs):
                pltpu.sync_copy(src, dst)

        dh = dot(p_refs[0][0], w_refs[0][...])
        for j in range(1, npc):
            dh = dh + dot(p_refs[j][0], w_refs[j][...])
        xf = x_ref[0]
        gv = g_ref[...]
        r = lax.rsqrt(jnp.mean(xf * xf, axis=-1, keepdims=True) + EPS)
        xhat = xf * r
        dhn = dh * (1.0 + sc_ref[0])
        p_sh = jnp.sum(dh, axis=0, keepdims=True)
        p_sc = jnp.sum(dh * (xhat * gv), axis=0, keepdims=True)
        p_g = jnp.sum(dhn * xhat, axis=0, keepdims=True)
        dxh = dhn * gv
        dx_ref[0] = dxo_ref[0] + r * (dxh - xhat * jnp.mean(dxh * xhat, axis=-1, keepdims=True))

        @pl.when(i == 0)
        def _():
            dsh_ref[0] = p_sh
            dsc_ref[0] = p_sc

        @pl.when(i > 0)
        def _():
            dsh_ref[0] += p_sh
            dsc_ref[0] += p_sc

        @pl.when((i == 0) & (b == 0))
        def _():
            dg_ref[...] = p_g

        @pl.when((i > 0) | (b > 0))
        def _():
            dg_ref[...] += p_g

    return pl.pallas_call(
        body, name=name, grid=(bsz, seq // tm),
        in_specs=[_row(tm, p.shape[2]) for p in pieces] + [pl.BlockSpec(memory_space=pl.ANY)] * npc + [
            _row(tm, dm), _full2((1, dm)), _seqvec(dm), _row(tm, dm)],
        out_specs=[_row(tm, dm), _seqvec(dm), _seqvec(dm), _full2((1, dm))],
        out_shape=[SDS((bsz, seq, dm), F32), SDS((bsz, 1, dm), F32), SDS((bsz, 1, dm), F32), SDS((1, dm), F32)],
        scratch_shapes=[pltpu.VMEM(w.shape, w.dtype) for w in weights],
        compiler_params=_cp("arbitrary", "arbitrary"),
    )(*pieces, *weights, x, g, sc, dxo)


HALO = 16


def _conv_core(gc, v, gch, vh, w, first):
    cv = gc * v
    halo = jnp.where(first, 0.0, gch * vh)
    ext = jnp.concatenate([halo, cv], axis=0)
    cv1 = pltpu.roll(ext, 1, 0)[HALO:]
    cv2 = pltpu.roll(ext, 2, 0)[HALO:]
    conv = w[0:1] * cv2 + w[1:2] * cv1 + w[2:3] * cv
    return cv, cv1, cv2, conv


def _prev_halo(tm, col):
    return pl.BlockSpec((1, HALO, CW), lambda b, i, *_: (b, jnp.maximum(i * (tm // HALO) - 1, 0), col))


def _next_halo(tm, seq, col):
    return pl.BlockSpec((1, HALO, CW), lambda b, i, *_: (b, jnp.minimum((i + 1) * (tm // HALO), seq // HALO - 1), col))


def _conv_fwd(p1, convw8, name):
    bsz, seq, _ = p1.shape
    tm = _pick_tile(seq, (512, 256, 128))

    def body(gb_ref, gc_ref, v_ref, gch_ref, vh_ref, w_ref, o_ref):
        first = pl.program_id(1) == 0
        _, _, _, conv = _conv_core(gc_ref[0].astype(F32), v_ref[0].astype(F32), gch_ref[0].astype(F32),
                                   vh_ref[0].astype(F32), w_ref[...], first)
        o_ref[0] = (gb_ref[0].astype(F32) * conv).astype(BF16)

    return pl.pallas_call(
        body, name=name, grid=(bsz, seq // tm),
        in_specs=[_row(tm, CW, 0), _row(tm, CW, 1), _row(tm, CW, 2), _prev_halo(tm, 1), _prev_halo(tm, 2),
                  _full2((8, CW))],
        out_specs=_row(tm, CW), out_shape=SDS((bsz, seq, CW), BF16),
        compiler_params=_cp("parallel", "parallel"),
    )(p1, p1, p1, p1, p1, convw8)


def _conv_bwd(dya, wco, p1, convw8, name):
    bsz, seq, _ = p1.shape
    tm = _pick_tile(seq, (512, 256, 128))
    nt = seq // tm
    ext_rows = tm + HALO

    def body(dya_ref, dyan_ref, wco_ref, gb_ref, gbn_ref, gc_ref, v_ref, gch_ref, vh_ref, w_ref, dp_ref, dw_ref):
        b = pl.program_id(0)
        i = pl.program_id(1)
        w = w_ref[...]
        gc = gc_ref[0].astype(F32)
        vv = v_ref[0].astype(F32)
        cv, cv1, cv2, conv = _conv_core(gc, vv, gch_ref[0].astype(F32), vh_ref[0].astype(F32), w, i == 0)
        dya_ext = jnp.concatenate([dya_ref[0], dyan_ref[0]], axis=0)
        dyain_ext = _dot_nt(dya_ext, wco_ref[...])
        gb_ext = jnp.concatenate([gb_ref[0], gbn_ref[0]], axis=0).astype(F32)
        rows = lax.broadcasted_iota(jnp.int32, (ext_rows, 1), 0)
        dconv_ext = jnp.where((rows < tm) | (i < nt - 1), dyain_ext * gb_ext, 0.0)
        dconv = dconv_ext[:tm]
        dconv1 = pltpu.roll(dconv_ext, ext_rows - 1, 0)[:tm]
        dconv2 = pltpu.roll(dconv_ext, ext_rows - 2, 0)[:tm]
        dcv = w[2:3] * dconv + w[1:2] * dconv1 + w[0:1] * dconv2
        dp_ref[0, :, 0:CW] = (dyain_ext[:tm] * conv).astype(BF16)
        dp_ref[0, :, CW:2 * CW] = (dcv * vv).astype(BF16)
        dp_ref[0, :, 2 * CW:3 * CW] = (dcv * gc).astype(BF16)
        g0 = jnp.sum(dconv * cv2, axis=0, keepdims=True)
        g1 = jnp.sum(dconv * cv1, axis=0, keepdims=True)
        g2 = jnp.sum(dconv * cv, axis=0, keepdims=True)
        upd = jnp.concatenate([g0, g1, g2, jnp.zeros((5, CW), F32)], axis=0)

        @pl.when((i == 0) & (b == 0))
        def _():
            dw_ref[...] = upd

        @pl.when((i > 0) | (b > 0))
        def _():
            dw_ref[...] += upd

    return pl.pallas_call(
        body, name=name, grid=(bsz, seq // tm),
        in_specs=[_row(tm, CW), _next_halo(tm, seq, 0), _full2((CW, D)),
                  _row(tm, CW, 0), _next_halo(tm, seq, 0), _row(tm, CW, 1), _row(tm, CW, 2),
                  _prev_halo(tm, 1), _prev_halo(tm, 2), _full2((8, CW))],
        out_specs=[_row(tm, 3 * CW), _full2((8, CW))],
        out_shape=[SDS((bsz, seq, 3 * CW), BF16), SDS((8, CW), F32)],
        compiler_params=_cp("arbitrary", "arbitrary"),
    )(dya, dya, wco, p1, p1, p1, p1, p1, p1, convw8)


def _disc(are, aim, ldt, bre, bim):
    dt = jnp.exp(ldt)
    mag = jnp.exp(are * dt)
    ang = aim * dt
    abr = mag * jnp.cos(ang)
    abi = mag * jnp.sin(ang)
    nr = abr - 1.0
    den = are * are + aim * aim
    cr = (nr * are + abi * aim) / den
    ci = (abi * are - nr * aim) / den
    return abr, abi, cr * bre - ci * bim, cr * bim + ci * bre


def _ssm_disc(are, aim, ldt, bre_t, bim_t):
    def body(are_ref, aim_ref, ldt_ref, bre_ref, bim_ref, abr_ref, abi_ref, bbr_ref, bbi_ref):
        abr, abi, bbr, bbi = _disc(are_ref[...], aim_ref[...], ldt_ref[...], bre_ref[...], bim_ref[...])
        abr_ref[...] = abr
        abi_ref[...] = abi
        bbr_ref[...] = bbr
        bbi_ref[...] = bbi

    v1, vh = SDS((1, GP), F32), SDS((NH, GP), F32)
    return pl.pallas_call(body, name="ssm_disc", out_shape=[v1, v1, vh, vh], compiler_params=_cp())(
        are, aim, ldt, bre_t, bim_t)


def _ssm_disc_bwd(are, aim, ldt, bre_t, bim_t, dabr, dabi, dbbr, dbbi):
    def body(are_ref, aim_ref, ldt_ref, bre_ref, bim_ref, g0, g1, g2, g3, o0, o1, o2, o3, o4):
        prim = (are_ref[...], aim_ref[...], ldt_ref[...], bre_ref[...], bim_ref[...])
        _, vjp = jax.vjp(_disc, *prim)
        d_are, d_aim, d_ldt, d_bre, d_bim = vjp((g0[...], g1[...], g2[...], g3[...]))
        o0[...] = d_are
        o1[...] = d_aim
        o2[...] = d_ldt
        o3[...] = d_bre
        o4[...] = d_bim

    v1, vh = SDS((1, GP), F32), SDS((NH, GP), F32)
    return pl.pallas_call(body, name="ssm_disc_bwd", out_shape=[v1, v1, v1, vh, vh], compiler_params=_cp())(
        are, aim, ldt, bre_t, bim_t, dabr, dabi, dbbr, dbbi)


def _scan_chunk(buf_ref, nt, bsz, ar, ai, init_r, init_i, reverse):
    nsub = SUB // bsz
    row = lax.broadcasted_iota(jnp.int32, (SUB, GP), 0)
    shift = ((SUB - bsz) if reverse else bsz) % SUB
    order = list(range(nsub - 1, -1, -1)) if reverse else list(range(nsub))

    def step(j, carry):
        pr, pi = carry
        jj = (nt - 1 - j) if reverse else j
        off = pl.multiple_of(jj * SUB, SUB)
        br = buf_ref[pl.ds(off, SUB), 0:GP]
        bi = buf_ref[pl.ds(off, SUB), GP:2 * GP]
        nr, ni = pr, pi
        for s in order:
            sr, si = nr, ni
            if shift:
                sr = pltpu.roll(sr, shift, 0)
                si = pltpu.roll(si, shift, 0)
            cr = ar * sr - ai * si + br
            ci = ar * si + ai * sr + bi
            if nsub == 1:
                nr, ni = cr, ci
            else:
                m = (row >= s * bsz) & (row < (s + 1) * bsz)
                nr = jnp.where(m, cr, nr)
                ni = jnp.where(m, ci, ni)
        buf_ref[pl.ds(off, SUB), 0:GP] = nr
        buf_ref[pl.ds(off, SUB), GP:2 * GP] = ni
        return nr, ni

    return lax.fori_loop(0, nt, step, (init_r, init_i))


def _ssm_chunk_rows(total_rows, bsz):
    return min(total_rows, 64 * bsz)


def _interleave(src_ref, tmp_ref, bsz, steps):
    nl = tmp_ref.shape[0]
    for b in range(bsz):
        for j in range(nl):
            tmp_ref.at[j][pl.ds(b, steps, stride=bsz), :] = src_ref[b, :, j * LANE:(j + 1) * LANE]
    return jnp.concatenate([tmp_ref[j] for j in range(nl)], axis=1)


def _deinterleave(val, tmp_ref, dst_ref, bsz, steps, skip=None):
    nl = tmp_ref.shape[0]
    for j in range(nl):
        tmp_ref[j] = val[:, j * LANE:(j + 1) * LANE]
    for b in range(bsz):
        for j in range(nl):
            lanes = slice(j * LANE, (j + 1) * LANE)
            v = tmp_ref.at[j][pl.ds(b, steps, stride=bsz), :]
            if skip is not None:
                v = v + skip[0][b, :, lanes] * skip[1][:, lanes]
            dst_ref[b, :, lanes] = v.astype(dst_ref.dtype)


SSM_UB = SW // SSM_SUPER
SSM_SB = GP // SSM_SUPER


def _sb_cols(s, half):
    return slice(half * GP + s * SSM_SB, half * GP + (s + 1) * SSM_SB)


def _ssm_in(v16, w_ref, x_ref):
    for s in range(SSM_SUPER):
        vs = v16[:, s * SSM_UB:(s + 1) * SSM_UB]
        for half in range(2):
            x_ref[:, _sb_cols(s, half)] = _dot(vs, w_ref[s * SSM_UB:(s + 1) * SSM_UB, half * SSM_SB:(half + 1) * SSM_SB])


def _ssm_out(x16, w_ref):
    outs = []
    for s in range(SSM_SUPER):
        rows = slice(s * SSM_UB, (s + 1) * SSM_UB)
        outs.append(_dot_nt(x16[:, _sb_cols(s, 0)], w_ref[rows, 0:SSM_SB])
                    + _dot_nt(x16[:, _sb_cols(s, 1)], w_ref[rows, SSM_SB:2 * SSM_SB]))
    return jnp.concatenate(outs, axis=1)


def _ssm_fwd(u, wb, wct, ar8, ai8, name):
    bsz, seq, _ = u.shape
    rt = seq * bsz
    r = _ssm_chunk_rows(rt, bsz)
    nt = r // SUB
    steps = r // bsz

    def body(u_ref, wb_hbm, wct_hbm, ar_ref, ai_ref, x_ref, y_ref, wb_ref, wct_ref, st_ref, tmp_ref):
        @pl.when(pl.program_id(0) == 0)
        def _():
            pltpu.sync_copy(wb_hbm, wb_ref)
            pltpu.sync_copy(wct_hbm, wct_ref)
            st_ref[...] = jnp.zeros_like(st_ref)

        _ssm_in(_interleave(u_ref, tmp_ref, bsz, steps).astype(BF16), wb_ref, x_ref)
        fr, fi = _scan_chunk(x_ref, nt, bsz, ar_ref[...], ai_ref[...], st_ref[:, 0:GP], st_ref[:, GP:2 * GP], False)
        st_ref[:, 0:GP] = fr
        st_ref[:, GP:2 * GP] = fi
        _deinterleave(_ssm_out(x_ref[...].astype(BF16), wct_ref), tmp_ref, y_ref, bsz, steps)

    anyspec = pl.BlockSpec(memory_space=pl.ANY)
    seqs = pl.BlockSpec((bsz, steps, SW), lambda i: (0, i, 0))
    return pl.pallas_call(
        body, name=name, grid=(rt // r,),
        in_specs=[seqs, anyspec, anyspec, _full2((SUB, GP)), _full2((SUB, GP))],
        out_specs=[pl.BlockSpec((r, 2 * GP), lambda i: (i, 0)), seqs],
        out_shape=[SDS((rt, 2 * GP), F32), SDS((bsz, seq, SW), F32)],
        scratch_shapes=[pltpu.VMEM(wb.shape, BF16), pltpu.VMEM(wct.shape, BF16), pltpu.VMEM((SUB, 2 * GP), F32),
                        pltpu.VMEM((SW // LANE, r, LANE), F32)],
        compiler_params=_cp("arbitrary"),
    )(u, wb, wct, ar8, ai8)


def _ssm_bwd(dy, u, xs, wb, wct, ar8, ai8, dskip, name):
    bsz, seq, _ = u.shape
    rt = seq * bsz
    r = _ssm_chunk_rows(rt, bsz)
    nt = r // SUB
    nc = rt // r
    steps = r // bsz
    ub = SW // SSM_SUPER
    sb = GP // SSM_SUPER

    def body(dys_ref, us_ref, x_ref, xh_ref, wb_hbm, wct_hbm, ar_ref, ai_ref, d_ref,
             dus_ref, dwb_hbm, dwct_hbm, dar_ref, dai_ref, wb_ref, wct_ref, g_ref, st_ref, awb_ref, awct_ref,
             tmp_ref):
        i = pl.program_id(0)
        dyb = _interleave(dys_ref, tmp_ref, bsz, steps).astype(BF16)
        ub16 = _interleave(us_ref, tmp_ref, bsz, steps).astype(BF16)

        @pl.when(i == 0)
        def _():
            pltpu.sync_copy(wb_hbm, wb_ref)
            pltpu.sync_copy(wct_hbm, wct_ref)
            st_ref[...] = jnp.zeros_like(st_ref)
            awb_ref[...] = jnp.zeros_like(awb_ref)
            awct_ref[...] = jnp.zeros_like(awct_ref)
            dar_ref[...] = jnp.zeros_like(dar_ref)
            dai_ref[...] = jnp.zeros_like(dai_ref)

        _ssm_in(dyb, wct_ref, g_ref)
        ar = ar_ref[...]
        ai = ai_ref[...]
        fr, fi = _scan_chunk(g_ref, nt, bsz, ar, -ai, st_ref[:, 0:GP], st_ref[:, GP:2 * GP], True)
        st_ref[:, 0:GP] = fr
        st_ref[:, GP:2 * GP] = fi

        gb = g_ref[...].astype(BF16)
        _deinterleave(_ssm_out(gb, wb_ref), tmp_ref, dus_ref, bsz, steps, skip=(dys_ref, d_ref))
        xb16 = x_ref[...].astype(BF16)
        for s in range(SSM_SUPER):
            us = ub16[:, s * ub:(s + 1) * ub]
            ds = dyb[:, s * ub:(s + 1) * ub]
            for half in range(2):
                cols = slice(half * GP + s * sb, half * GP + (s + 1) * sb)
                ocols = slice(half * sb, (half + 1) * sb)
                awb_ref[s * ub:(s + 1) * ub, ocols] += _dot_tn(us, gb[:, cols])
                awct_ref[s * ub:(s + 1) * ub, ocols] += _dot_tn(ds, xb16[:, cols])

        gr = g_ref[:, 0:GP]
        gi = g_ref[:, GP:2 * GP]
        xsr = pltpu.roll(x_ref[:, 0:GP], bsz, 0)
        xsi = pltpu.roll(x_ref[:, GP:2 * GP], bsz, 0)
        inner = lax.broadcasted_iota(jnp.int32, (r, 1), 0) >= bsz
        t_r = jnp.where(inner, gr * xsr + gi * xsi, 0.0)
        t_i = jnp.where(inner, gi * xsr - gr * xsi, 0.0)
        acc_r = jnp.sum(t_r.reshape(nt, SUB, GP), axis=0)
        acc_i = jnp.sum(t_i.reshape(nt, SUB, GP), axis=0)
        hr = xh_ref[:, 0:GP]
        hi = xh_ref[:, GP:2 * GP]
        if bsz % SUB:
            hr = pltpu.roll(hr, bsz, 0)
            hi = pltpu.roll(hi, bsz, 0)
        edge = (lax.broadcasted_iota(jnp.int32, (SUB, 1), 0) < bsz) & (i < nc - 1)
        g0r = g_ref[0:SUB, 0:GP]
        g0i = g_ref[0:SUB, GP:2 * GP]
        dar_ref[...] += acc_r + jnp.where(edge, g0r * hr + g0i * hi, 0.0)
        dai_ref[...] += acc_i + jnp.where(edge, g0i * hr - g0r * hi, 0.0)

        @pl.when(i == nc - 1)
        def _():
            pltpu.sync_copy(awb_ref, dwb_hbm)
            pltpu.sync_copy(awct_ref, dwct_hbm)

    anyspec = pl.BlockSpec(memory_space=pl.ANY)
    rev = lambda i: (nc - 1 - i, 0)
    seqs = pl.BlockSpec((bsz, steps, SW), lambda i: (0, nc - 1 - i, 0))
    wshape = (SW, 2 * sb)
    return pl.pallas_call(
        body, name=name, grid=(nc,),
        in_specs=[seqs, seqs, pl.BlockSpec((r, 2 * GP), rev),
                  pl.BlockSpec((SUB, 2 * GP), lambda i: (jnp.maximum((nc - 1 - i) * nt - 1, 0), 0)),
                  anyspec, anyspec, _full2((SUB, GP)), _full2((SUB, GP)), _full2((1, SW))],
        out_specs=[seqs, anyspec, anyspec, _full2((SUB, GP)), _full2((SUB, GP))],
        out_shape=[SDS((bsz, seq, SW), BF16), SDS(wshape, F32), SDS(wshape, F32), SDS((SUB, GP), F32),
                   SDS((SUB, GP), F32)],
        scratch_shapes=[pltpu.VMEM(wb.shape, BF16), pltpu.VMEM(wct.shape, BF16),
                        pltpu.VMEM((r, 2 * GP), F32), pltpu.VMEM((SUB, 2 * GP), F32),
                        pltpu.VMEM(wshape, F32), pltpu.VMEM(wshape, F32),
                        pltpu.VMEM((SW // LANE, r, LANE), F32)],
        compiler_params=_cp("arbitrary"),
    )(dy, u, xs, xs, wb, wct, ar8, ai8, dskip)


GELU_C = math.sqrt(2.0 / math.pi)


def _gelu(x):
    return 0.5 * x * (1.0 + jnp.tanh(GELU_C * (x + 0.044715 * x * x * x)))


def _gelu_grad(x):
    th = jnp.tanh(GELU_C * (x + 0.044715 * x * x * x))
    return 0.5 * (1.0 + th) + 0.5 * x * (1.0 - th * th) * GELU_C * (1.0 + 3.0 * 0.044715 * x * x)


def _ssm_post(ys, u, dskip, wglu, wso, name):
    bsz, seq, _ = ys.shape
    tm = _pick_tile(seq, (512, 256, 128))

    def body(ys_ref, u_ref, d_ref, wg_ref, wo_ref, s0_ref, z_ref, s1_ref, s2_ref, yb_ref):
        s0 = ys_ref[0] + d_ref[...] * u_ref[0]
        s1 = _gelu(s0)
        s1b = s1.astype(BF16)
        z = _dot(s1b, wg_ref[...])
        s2b = (s1 * _sigmoid(z)).astype(BF16)
        s0_ref[0] = s0
        z_ref[0] = z
        s1_ref[0] = s1b
        s2_ref[0] = s2b
        yb_ref[0] = _dot(s2b, wo_ref[...]).astype(BF16)

    return pl.pallas_call(
        body, name=name, grid=(bsz, seq // tm),
        in_specs=[_row(tm, SW), _row(tm, SW), _full2((1, SW)), _full2((SW, SW)), _full2((SW, D))],
        out_specs=[_row(tm, SW), _row(tm, SW), _row(tm, SW), _row(tm, SW), _row(tm, D)],
        out_shape=[SDS((bsz, seq, SW), F32), SDS((bsz, seq, SW), F32), SDS((bsz, seq, SW), BF16),
                   SDS((bsz, seq, SW), BF16), SDS((bsz, seq, D), BF16)],
        compiler_params=_cp("parallel", "parallel"),
    )(ys, u, dskip, wglu, wso)


def _ssm_post_bwd(dyb, s0, z, u, dskip, wglu, wso, name):
    bsz, seq, _ = s0.shape
    tm = _pick_tile(seq, (512, 256, 128))

    def body(dyb_ref, s0_ref, z_ref, u_ref, wg_ref, wo_ref, ds0_ref, dz_ref, dd_ref):
        b = pl.program_id(0)
        i = pl.program_id(1)
        ds2 = _dot_nt(dyb_ref[0], wo_ref[...])
        s0 = s0_ref[0]
        s1 = _gelu(s0)
        sg = _sigmoid(z_ref[0])
        dz = ds2 * s1 * sg * (1.0 - sg)
        dzb = dz.astype(BF16)
        ds1 = ds2 * sg + _dot_nt(dzb, wg_ref[...])
        ds0 = ds1 * _gelu_grad(s0)
        ds0_ref[0] = ds0
        dz_ref[0] = dzb
        part = jnp.sum(ds0 * u_ref[0], axis=0, keepdims=True)

        @pl.when((i == 0) & (b == 0))
        def _():
            dd_ref[...] = part

        @pl.when((i > 0) | (b > 0))
        def _():
            dd_ref[...] += part

    del dskip
    return pl.pallas_call(
        body, name=name, grid=(bsz, seq // tm),
        in_specs=[_row(tm, D), _row(tm, SW), _row(tm, SW), _row(tm, SW), _full2((SW, SW)), _full2((SW, D))],
        out_specs=[_row(tm, SW), _row(tm, SW), _full2((1, SW))],
        out_shape=[SDS((bsz, seq, SW), F32), SDS((bsz, seq, SW), BF16), SDS((1, SW), F32)],
        compiler_params=_cp("arbitrary", "arbitrary"),
    )(dyb, s0, z, u, wglu, wso)


def _merge_out(ya, yb, p3, wout, x1, gt, name):
    bsz, seq, _ = ya.shape
    tm = _pick_tile(seq, (512, 256, 128))

    def body(ya_ref, yb_ref, ga_ref, gbb_ref, w_ref, x_ref, gt_ref, mg_ref, mix_ref, xo_ref):
        merged = (_sigmoid(ga_ref[0].astype(F32)) * ya_ref[0].astype(F32)
                  + _sigmoid(gbb_ref[0].astype(F32)) * yb_ref[0].astype(F32)).astype(BF16)
        mix = _dot(merged, w_ref[...])
        mg_ref[0] = merged
        mix_ref[0] = mix.astype(BF16)
        xo_ref[0] = x_ref[0] + gt_ref[0] * mix

    return pl.pallas_call(
        body, name=name, grid=(bsz, seq // tm),
        in_specs=[_row(tm, D), _row(tm, D), _row(tm, D, 0), _row(tm, D, 1), _full2((D, D)), _row(tm, D), _seqvec(D)],
        out_specs=[_row(tm, D), _row(tm, D), _row(tm, D)],
        out_shape=[SDS((bsz, seq, D), BF16), SDS((bsz, seq, D), BF16), SDS((bsz, seq, D), F32)],
        compiler_params=_cp("parallel", "parallel"),
    )(ya, yb, p3, p3, wout, x1, gt)


def _merge_bwd(dx2, gt, mix, ya, yb, p3, wout, name):
    bsz, seq, _ = ya.shape
    tm = _pick_tile(seq, (512, 256, 128))

    def body(dx_ref, gt_ref, mix_ref, ya_ref, yb_ref, ga_ref, gbb_ref, w_ref, dmix_ref, dya_ref, dyb_ref, dp_ref, dgt_ref):
        i = pl.program_id(1)
        dx = dx_ref[0]
        dmix = (gt_ref[0] * dx).astype(BF16)
        dmix_ref[0] = dmix
        part = jnp.sum(dx * mix_ref[0].astype(F32), axis=0, keepdims=True)

        @pl.when(i == 0)
        def _():
            dgt_ref[0] = part

        @pl.when(i > 0)
        def _():
            dgt_ref[0] += part

        dmg = _dot_nt(dmix, w_ref[...])
        sa = _sigmoid(ga_ref[0].astype(F32))
        sb = _sigmoid(gbb_ref[0].astype(F32))
        dya_ref[0] = (dmg * sa).astype(BF16)
        dyb_ref[0] = (dmg * sb).astype(BF16)
        dp_ref[0, :, 0:D] = (dmg * ya_ref[0].astype(F32) * sa * (1.0 - sa)).astype(BF16)
        dp_ref[0, :, D:2 * D] = (dmg * yb_ref[0].astype(F32) * sb * (1.0 - sb)).astype(BF16)

    bshape = SDS((bsz, seq, D), BF16)
    return pl.pallas_call(
        body, name=name, grid=(bsz, seq // tm),
        in_specs=[_row(tm, D), _seqvec(D), _row(tm, D), _row(tm, D), _row(tm, D), _row(tm, D, 0), _row(tm, D, 1),
                  _full2((D, D))],
        out_specs=[_row(tm, D), _row(tm, D), _row(tm, D), _row(tm, 2 * D), _seqvec(D)],
        out_shape=[bshape, bshape, bshape, SDS((bsz, seq, 2 * D), BF16), SDS((bsz, 1, D), F32)],
        compiler_params=_cp("arbitrary", "arbitrary"),
    )(dx2, gt, mix, ya, yb, p3, p3, wout)


def _final_loss(x3, gfin, target, name):
    bsz, seq, dm = x3.shape
    tm = _pick_tile(seq, (512, 256, 128))

    def body(x_ref, g_ref, t_ref, dx_ref, loss_ref, dg_ref):
        b = pl.program_id(0)
        i = pl.program_id(1)
        xf = x_ref[0]
        gv = g_ref[...]
        r = lax.rsqrt(jnp.mean(xf * xf, axis=-1, keepdims=True) + EPS)
        xhat = xf * r
        e = xhat * gv - t_ref[0]
        dy = e * (1.0 / dm)
        dxh = dy * gv
        dx_ref[0] = r * (dxh - xhat * jnp.mean(dxh * xhat, axis=-1, keepdims=True))
        p_l = jnp.sum(e * e, axis=0, keepdims=True) * (0.5 / dm)
        p_g = jnp.sum(dy * xhat, axis=0, keepdims=True)

        @pl.when((i == 0) & (b == 0))
        def _():
            loss_ref[...] = p_l
            dg_ref[...] = p_g

        @pl.when((i > 0) | (b > 0))
        def _():
            loss_ref[...] += p_l
            dg_ref[...] += p_g

    return pl.pallas_call(
        body, name=name, grid=(bsz, seq // tm),
        in_specs=[_row(tm, dm), _full2((1, dm)), _row(tm, dm)],
        out_specs=[_row(tm, dm), _full2((1, dm)), _full2((1, dm))],
        out_shape=[SDS((bsz, seq, dm), F32), SDS((1, dm), F32), SDS((1, dm), F32)],
        compiler_params=_cp("arbitrary", "arbitrary"),
    )(x3, gfin, target)


def _ada_fwd(c_all, w_shard, b_shard):
    nb = c_all.shape[0]
    n = w_shard.shape[2]

    def body(c_ref, w_ref, b_ref, o_ref):
        cv = c_ref[...]
        cond = (cv * _sigmoid(cv)).astype(BF16)
        o_ref[...] = _dot(cond, w_ref[0].astype(BF16)) + b_ref[...]

    return pl.pallas_call(body, name="ada_fwd", out_shape=SDS((nb, n), F32), compiler_params=_cp())(
        c_all, w_shard, b_shard)


def _ada_bwd(c_all, dmod_shard, dmod_all):
    n = dmod_shard.shape[1]

    def body(c_ref, ds_ref, da_ref, gw_ref, gb_ref):
        cv = c_ref[...]
        cond = (cv * _sigmoid(cv)).astype(BF16)
        gw_ref[...] = _dot_tn(cond, ds_ref[...].astype(BF16))
        gb_ref[...] = jnp.sum(da_ref[...], axis=0, keepdims=True)

    return pl.pallas_call(
        body, name="ada_bwd", out_shape=[SDS((D, n), F32), SDS((1, dmod_all.shape[1]), F32)], compiler_params=_cp(),
    )(c_all, dmod_shard, dmod_all)


def _adamw_math(w, g, m, v):
    m = B1 * m + (1.0 - B1) * g
    v = B2 * v + (1.0 - B2) * (g * g)
    delta = -LR * ((m / BC1) / (jnp.sqrt(v / BC2) + AEPS) + WD * w)
    return delta, m, v


def _adamw_big(w, m, v, recv_own, recv_sib, name):
    _, rows, cols = w.shape
    tr = _pick_tile(rows, tuple(t for t in (512, 256, 128, 64, 32, 16, 8) if t * cols <= 192 * 1024))

    def body(w_ref, m_ref, v_ref, a_ref, b_ref, g_ref, d_ref, mo_ref, vo_ref):
        def chip_sum(r):
            acc = r[0].astype(F32)
            for k in range(1, N_CHIPS):
                acc = acc + r[k].astype(F32)
            return acc

        g = chip_sum(a_ref) + chip_sum(b_ref)
        delta, mn, vn = _adamw_math(w_ref[0], g, m_ref[0], v_ref[0])
        g_ref[0] = g
        d_ref[0] = delta
        mo_ref[0] = mn
        vo_ref[0] = vn

    own = pl.BlockSpec((1, tr, cols), lambda i: (0, i, 0))
    rspec = pl.BlockSpec((N_CHIPS, tr, cols), lambda i: (0, i, 0))
    shp = SDS(w.shape, F32)
    return pl.pallas_call(
        body, name=name, grid=(rows // tr,),
        in_specs=[own, own, own, rspec, rspec], out_specs=[own, own, own, own], out_shape=[shp, shp, shp, shp],
        compiler_params=_cp("parallel"),
    )(w, m, v, recv_own, recv_sib)


def _adamw_plain(w, m, v, g, name):
    def body(w_ref, m_ref, v_ref, g_ref, d_ref, mo_ref, vo_ref):
        delta, mn, vn = _adamw_math(w_ref[...], g_ref[...], m_ref[...], v_ref[...])
        d_ref[...] = delta
        mo_ref[...] = mn
        vo_ref[...] = vn

    shp = SDS(w.shape, F32)
    return pl.pallas_call(body, name=name, out_shape=[shp, shp, shp], compiler_params=_cp())(w, m, v, g)


def _adamw_rows(w, m, v, g, name):
    _, rows, cols = w.shape
    tr = _pick_tile(rows, (128, 64, 32, 16, 8))

    def body(w_ref, m_ref, v_ref, g_ref, d_ref, mo_ref, vo_ref):
        delta, mn, vn = _adamw_math(w_ref[0], g_ref[...], m_ref[0], v_ref[0])
        d_ref[0] = delta
        mo_ref[0] = mn
        vo_ref[0] = vn

    spec = pl.BlockSpec((1, tr, cols), lambda i: (0, i, 0))
    shp = SDS(w.shape, F32)
    return pl.pallas_call(
        body, name=name, grid=(rows // tr,), in_specs=[spec] * 3 + [pl.BlockSpec((tr, cols), lambda i: (i, 0))],
        out_specs=[spec] * 3, out_shape=[shp] * 3, compiler_params=_cp("parallel"),
    )(w, m, v, g)


def _sum_slabs(r, name):
    n, rows, cols = r.shape
    tr = _pick_tile(rows, (256, 128, 64))

    def body(r_ref, o_ref):
        acc = r_ref[0].astype(F32)
        for j in range(1, n):
            acc = acc + r_ref[j].astype(F32)
        o_ref[...] = acc

    return pl.pallas_call(
        body, name=name, grid=(rows // tr,),
        in_specs=[pl.BlockSpec((n, tr, cols), lambda i: (0, i, 0))],
        out_specs=pl.BlockSpec((tr, cols), lambda i: (i, 0)), out_shape=SDS((rows, cols), F32),
        compiler_params=_cp("parallel"),
    )(r)


def _place():
    return lax.axis_index("x"), lax.axis_index("y"), lax.axis_index("c")


def _all_gather8(blk, name):
    m_per, n = blk.shape

    def body(x_ref, out_ref, send_sems, recv_sems, local_sem):
        x, y, c = _place()
        me, sibling = (x, y, c), (x, y, 1 - c)
        chips = [(1 - x, y), (x, 1 - y), (1 - x, 1 - y)]

        def rows(px, py, pc):
            return out_ref.at[pl.ds((4 * px + 2 * py + pc) * m_per, m_per), :]

        def copy(k, block, to, src=None):
            return pltpu.make_async_remote_copy(
                src_ref=rows(*block) if src is None else src, dst_ref=rows(*block),
                send_sem=send_sems.at[k], recv_sem=recv_sems.at[k], device_id=to, device_id_type=MESH)

        mine = pltpu.make_async_copy(x_ref, rows(*me), local_sem)
        mine.start()
        first = [copy(0, me, sibling, src=x_ref)]
        first += [copy(1 + j, me, (*chip, c), src=x_ref) for j, chip in enumerate(chips)]
        for cp in first:
            cp.start()
        passed = [copy(4 + j, (*chip, c), sibling) for j, chip in enumerate(chips)]
        for j, chip in enumerate(chips):
            copy(1 + j, (*chip, c), me).wait_recv()
            passed[j].start()
        copy(0, sibling, me).wait_recv()
        for j, chip in enumerate(chips):
            copy(4 + j, (*chip, 1 - c), me).wait_recv()
        for cp in first + passed:
            cp.wait_send()
        mine.wait()

    return pl.pallas_call(
        body, name=name, out_shape=SDS((N_DEV * m_per, n), blk.dtype),
        in_specs=[pl.BlockSpec(memory_space=pltpu.VMEM)], out_specs=pl.BlockSpec(memory_space=pltpu.VMEM),
        scratch_shapes=[pltpu.SemaphoreType.DMA((7,)), pltpu.SemaphoreType.DMA((7,)), pltpu.SemaphoreType.DMA],
        compiler_params=pltpu.CompilerParams(vmem_limit_bytes=VMEM_LIMIT),
    )(blk)


def _chip_peers(x, y):
    return [(1 - x, y), (x, 1 - y), (1 - x, 1 - y)]


SIBLING = "sibling"


def _peer_copies(src_refs, land_refs, send_sems, recv_sems, scatter, landed):
    x, y, c = _place()
    if scatter == SIBLING:
        return [pltpu.make_async_remote_copy(
            src_ref=s, dst_ref=l, send_sem=send_sems.at[a], recv_sem=recv_sems.at[a],
            device_id=(x, y, 1 - c), device_id_type=MESH) for a, (s, l) in enumerate(zip(src_refs, land_refs))]
    cps = []
    for a, (src_ref, land_ref) in enumerate(zip(src_refs, land_refs)):
        for j, (px, py) in enumerate(_chip_peers(x, y)):
            if scatter:
                src = src_ref.at[2 * px + py]
            else:
                src = land_ref.at[2 * x + y] if src_ref is None else src_ref
            dst = land_ref.at[2 * px + py] if landed else land_ref.at[2 * x + y]
            cps.append(pltpu.make_async_remote_copy(
                src_ref=src, dst_ref=dst, send_sem=send_sems.at[3 * a + j], recv_sem=recv_sems.at[3 * a + j],
                device_id=(px, py, c), device_id_type=MESH))
    return cps


def _exchange_chips(srcs, scatter, name):
    n = len(srcs)

    def body(*refs):
        src_refs, land_refs = refs[:n], refs[n:2 * n]
        send_sems, recv_sems, local_sems = refs[2 * n:]
        x, y, _ = _place()
        me = 2 * x + y
        mine = [pltpu.make_async_copy(s.at[me] if scatter else s, l.at[me], local_sems.at[a])
                for a, (s, l) in enumerate(zip(src_refs, land_refs))]
        for cp in mine:
            cp.start()
        out = _peer_copies(src_refs, land_refs, send_sems, recv_sems, scatter, False)
        for cp in out:
            cp.start()
        for cp in _peer_copies(src_refs, land_refs, send_sems, recv_sems, scatter, True):
            cp.wait_recv()
        for cp in out:
            cp.wait_send()
        for cp in mine:
            cp.wait()

    anyspec = pl.BlockSpec(memory_space=pl.ANY)
    shapes = [SDS(s.shape if scatter else (N_CHIPS,) + s.shape, s.dtype) for s in srcs]
    return pl.pallas_call(
        body, name=name, out_shape=shapes, in_specs=[anyspec] * n, out_specs=[anyspec] * n,
        scratch_shapes=[pltpu.SemaphoreType.DMA((3 * n,)), pltpu.SemaphoreType.DMA((3 * n,)),
                        pltpu.SemaphoreType.DMA((n,))],
        compiler_params=pltpu.CompilerParams(vmem_limit_bytes=VMEM_LIMIT),
    )(*srcs)


_HBM = pl.BlockSpec(memory_space=pltpu.HBM)
_SEM = pl.BlockSpec(memory_space=pltpu.SEMAPHORE)
_EFFECT = pltpu.SideEffectType.DATAFLOW_SIDE_EFFECTING


def _exchange_begin(srcs, lands, scatter, name):
    srcs = tuple(srcs or ())
    n, ns = len(lands), len(srcs)
    nsem = n if scatter == SIBLING else 3 * n

    def body(*refs):
        src_refs = refs[:ns] if ns else (None,) * n
        land_refs = refs[ns:ns + n]
        send_sems, recv_sems = refs[ns + n:ns + n + 2]
        token = refs[-1]
        for cp in _peer_copies(src_refs, land_refs, send_sems, recv_sems, scatter, False):
            cp.start()
        token[...] = jnp.zeros_like(token)

    ops = (*srcs, *lands)
    res = pl.pallas_call(
        body, name=name,
        out_shape=(pltpu.SemaphoreType.DMA((nsem,)), pltpu.SemaphoreType.DMA((nsem,)),
                   *[pltpu.HBM(a.shape, a.dtype) for a in ops], SDS((SUB, LANE), F32)),
        in_specs=[_HBM] * len(ops), out_specs=(_SEM, _SEM, *[_HBM] * len(ops), pl.BlockSpec(memory_space=pltpu.VMEM)),
        input_output_aliases={i: 2 + i for i in range(len(ops))},
        compiler_params=pltpu.CompilerParams(has_side_effects=_EFFECT),
    )(*[pltpu.with_memory_space_constraint(a, pltpu.HBM) for a in ops])
    return res[0], res[1], res[2:2 + ns], res[2 + ns:2 + ns + n], res[-1]


def _exchange_end(handle, after, scatter, name, with_srcs=False):
    send_sems, recv_sems, srcs, lands, _ = handle
    n, ns = len(lands), len(srcs)

    def body(*refs):
        src_refs = refs[:ns] if ns else (None,) * n
        land_refs = refs[ns:ns + n]
        send_sems, recv_sems = refs[ns + n:ns + n + 2]
        for cp in _peer_copies(src_refs, land_refs, send_sems, recv_sems, scatter, True):
            cp.wait_send()
            cp.wait_recv()

    ops = (*srcs, *lands)
    res = pl.pallas_call(
        body, name=name,
        out_shape=tuple(pltpu.HBM(a.shape, a.dtype) for a in ops),
        in_specs=[_HBM] * len(ops) + [_SEM, _SEM, pl.BlockSpec(memory_space=pl.ANY)], out_specs=tuple([_HBM] * len(ops)),
        input_output_aliases={i: i for i in range(len(ops))},
        compiler_params=pltpu.CompilerParams(has_side_effects=_EFFECT),
    )(*ops, send_sems, recv_sems, after)
    return (list(res[:ns]), list(res[ns:])) if with_srcs else list(res[ns:])


def _own_slab(stack4, chip):
    idx = lax.broadcasted_iota(jnp.int32, (N_CHIPS,) + (1,) * (stack4.ndim - 1), 0)
    return jnp.where(idx == chip, stack4, jnp.zeros((), stack4.dtype))


def _swap_sibling(vs, name):
    n = len(vs)

    def body(*refs):
        in_refs, out_refs = refs[:n], refs[n:2 * n]
        send_sems, recv_sems = refs[2 * n:]
        x, y, c = _place()
        cps = [pltpu.make_async_remote_copy(
            src_ref=i, dst_ref=o, send_sem=send_sems.at[a], recv_sem=recv_sems.at[a],
            device_id=(x, y, 1 - c), device_id_type=MESH) for a, (i, o) in enumerate(zip(in_refs, out_refs))]
        for cp in cps:
            cp.start()
        for cp in cps:
            cp.wait()

    anyspec = pl.BlockSpec(memory_space=pl.ANY)
    return pl.pallas_call(
        body, name=name, out_shape=[SDS(v.shape, v.dtype) for v in vs], in_specs=[anyspec] * n, out_specs=[anyspec] * n,
        scratch_shapes=[pltpu.SemaphoreType.DMA((n,)), pltpu.SemaphoreType.DMA((n,))],
        compiler_params=pltpu.CompilerParams(vmem_limit_bytes=VMEM_LIMIT),
    )(*vs)


def _select(stacked, idx):
    out = stacked[0]
    for j in range(1, stacked.shape[0]):
        out = jnp.where(idx == j, stacked[j], out)
    return out


BIG = (
    ("w1_a", DFF // 4, D, False), ("w3_a", DFF // 4, D, False), ("w2_a", DFF // 4, D, False),
    ("w_in", D, 5632 // 4, True), ("w_conv_out", CW // 4, D, False), ("w_glu", SW // 4, SW, False),
    ("w_ssm_out", SW, D // 4, True), ("w_out", D // 4, D, False),
    ("w1_b", DFF // 4, D, False), ("w3_b", DFF // 4, D, False), ("w2_b", DFF // 4, D, False),
)
TRANSPOSED = frozenset(("w1_a", "w3_a", "w1_b", "w3_b"))
PACK_COLS = 1024


def _view(a, name):
    return jnp.transpose(a, (0, 2, 1)) if name in TRANSPOSED else a


def _full_from_stacked(st, split_cols):
    _, rows, cols = st.shape
    if split_cols:
        return st.transpose(1, 0, 2).reshape(rows, N_CHIPS * cols)
    return st.reshape(N_CHIPS * rows, cols)


def _stacked_from_full(full, rows, cols, split_cols):
    if split_cols:
        return full.reshape(rows, N_CHIPS, cols).transpose(1, 0, 2)
    return full.reshape(N_CHIPS, rows, cols)


def _blockdiag(t):
    r = lax.broadcasted_iota(jnp.int32, (SW, GP), 0) // NH
    cidx = lax.broadcasted_iota(jnp.int32, (SW, GP), 1) // NP
    dense = jnp.where(r == cidx, jnp.tile(t, (NG, 1)), 0.0)
    ub, sb = SW // SSM_SUPER, GP // SSM_SUPER
    return jnp.concatenate([dense[s * ub:(s + 1) * ub, s * sb:(s + 1) * sb] for s in range(SSM_SUPER)], axis=0)


def _blockdiag_extract(acc):
    gs = NG // SSM_SUPER
    a = acc.reshape(NG, NH, gs, NP)
    sel = (lax.broadcasted_iota(jnp.int32, (NG, 1, gs, 1), 0) % gs) == lax.broadcasted_iota(jnp.int32, (NG, 1, gs, 1), 2)
    a = jnp.sum(jnp.where(sel, a, 0.0), axis=2)
    return a.transpose(1, 0, 2).reshape(NH, GP)


def _to_t(p):
    return p.transpose(2, 0, 1).reshape(NH, GP)


def _from_t(t):
    return t.reshape(NH, NG, NP).transpose(1, 2, 0)


def _c_to_t(p):
    return p.transpose(1, 0, 2).reshape(NH, GP)


def _c_from_t(t):
    return t.reshape(NH, NG, NP).transpose(1, 0, 2)


def _ffn_forward(x, g, sh, sc, gt, w1, w3, w2, tag):
    h = _norm_mod(x, g, sh, sc, f"{tag}_norm")
    if callable(w1):
        w1, w3 = w1(h)
    a, b, hid = _swiglu_up(h, w1, w3, f"{tag}_up")
    w2 = w2(hid) if callable(w2) else w2
    f, xo = _ffn_down(hid, w2, x, gt, f"{tag}_down")
    return xo, (x, h, a, b, hid, f), w2


def _ffn_backward(dxo, saved, g, sc, gt, w1, w3, w2, tag, emit=lambda key, gw: 0.0):
    x, h, a, b, hid, f = saved
    dfs, da, db, dgt = _ffn_bwd_hid(dxo, gt, f, a, b, w2, f"{tag}_bwd_hid")
    h2 = _flat(h)
    gw2 = _mm(_flat(hid), _flat(dfs), ta=True, out_dtype=BF16, name=f"{tag}_gw2")
    tok = emit("w2", gw2)
    gw1 = _mm(_flat(da), h2, ta=True, out_dtype=BF16, name=f"{tag}_gw1")
    tok = tok + emit("w1", gw1)
    gw3 = _mm(_flat(db), h2, ta=True, out_dtype=BF16, name=f"{tag}_gw3")
    tok = tok + emit("w3", gw3)
    dx, dsh, dsc, dg = _dh_norm_bwd([da, db], [w1, w3], x, g, sc + tok, dxo, f"{tag}_bwd_dh", transposed=True)
    return dx, (dsh, dsc, dgt, dg), (gw1, gw3, gw2)


def kernel(x, c, w_ada, b_ada, g_ffn1, w1_a, w3_a, w2_a, g_mix, w_in, conv_w, w_conv_out, a_re, a_im, b_re, b_im, c_re, c_im, log_dt, d_skip, w_glu, w_ssm_out, w_out, g_ffn2, w1_b, w3_b, w2_b, g_final, loss_target, m_w_ada, m_b_ada, m_g_ffn1, m_w1_a, m_w3_a, m_w2_a, m_g_mix, m_w_in, m_conv_w, m_w_conv_out, m_a_re, m_a_im, m_b_re, m_b_im, m_c_re, m_c_im, m_log_dt, m_d_skip, m_w_glu, m_w_ssm_out, m_w_out, m_g_ffn2, m_w1_b, m_w3_b, m_w2_b, m_g_final, v_w_ada, v_b_ada, v_g_ffn1, v_w1_a, v_w3_a, v_w2_a, v_g_mix, v_w_in, v_conv_w, v_w_conv_out, v_a_re, v_a_im, v_b_re, v_b_im, v_c_re, v_c_im, v_log_dt, v_d_skip, v_w_glu, v_w_ssm_out, v_w_out, v_g_ffn2, v_w1_b, v_w3_b, v_w2_b, v_g_final):
    args = dict(locals())
    names = ["w_ada", "b_ada", "g_ffn1", "w1_a", "w3_a", "w2_a", "g_mix", "w_in", "conv_w", "w_conv_out", "a_re",
             "a_im", "b_re", "b_im", "c_re", "c_im", "log_dt", "d_skip", "w_glu", "w_ssm_out", "w_out", "g_ffn2",
             "w1_b", "w3_b", "w2_b", "g_final"]
    bsz, seq, _ = x.shape
    mx, my, mc = _place()
    chip = 2 * mx + my
    dev = 4 * mx + 2 * my + mc

    groups = (BIG[:3], BIG[3:8], BIG[8:])
    wfull = {}

    def shards(grp):
        return [_view(args[n], n)[0] for n, _, _, _ in grp]

    def unpack_group(gathered, grp):
        for (n, _, _, split), st in zip(grp, gathered):
            wfull[n] = _full_from_stacked(st, split)

    up_grp, down_grp = groups[0][:2], groups[0][2:]

    nmod_shard = NMOD * D // N_CHIPS
    c_all = _all_gather8(c.reshape(SUB, -1), "gather_c").reshape(N_DEV * bsz, D)
    b_shard = _select(b_ada.reshape(N_CHIPS, 1, nmod_shard), chip)
    mod_shard = _ada_fwd(c_all, w_ada, b_shard)
    nb = N_DEV * bsz
    cw_pad = jnp.pad(conv_w[0], ((0, SUB - 3), (0, nmod_shard - CW // N_CHIPS)))
    mod_st = _exchange_chips([jnp.concatenate([mod_shard, cw_pad], axis=0)], False, "gather_mod")[0]
    mod_all = mod_st[:, :nb].transpose(1, 0, 2).reshape(N_DEV, bsz, NMOD * D)
    mod = _select(mod_all, dev)

    def gather_begin(raw, name):
        lands = [_own_slab(jnp.broadcast_to(a.astype(BF16)[None], (N_CHIPS,) + a.shape), chip) for a in raw]
        return _exchange_begin(None, lands, False, name)

    up_raw, mod = lax.optimization_barrier((shards(up_grp), mod))
    up_handle = gather_begin(up_raw, "gather_w_ffn1_up_start")
    (down_raw, mix_raw, ffn2_raw), up_token = lax.optimization_barrier(
        ((shards(down_grp), shards(groups[1]), shards(groups[2])), up_handle[4][0:1, 0:1]))
    down_handle = gather_begin(down_raw, "gather_w_ffn1_down_start")
    mix_handle = gather_begin(mix_raw, "gather_w_mix_start")
    ffn2_handle = gather_begin(ffn2_raw, "gather_w_ffn2_start")
    start_tokens = up_token + down_handle[4][0:1, 0:1] + mix_handle[4][0:1, 0:1] + ffn2_handle[4][0:1, 0:1]

    sh1, sc1, gt1, sh2, sc2, gt2, sh3, sc3, gt3 = [mod[:, None, j * D:(j + 1) * D] for j in range(NMOD)]
    convw = mod_st[:, nb:nb + 3, :CW // N_CHIPS].transpose(1, 0, 2).reshape(3, CW)
    convw8 = jnp.pad(convw, ((0, SUB - 3), (0, 0)))

    are, aim = a_re.reshape(1, GP), a_im.reshape(1, GP)
    ldt = jnp.broadcast_to(log_dt.reshape(NG, 1), (NG, NP)).reshape(1, GP)
    bre_t, bim_t = _to_t(b_re[0]), _to_t(b_im[0])
    abr, abi, bbr_t, bbi_t = _ssm_disc(are, aim, ldt, bre_t, bim_t)
    wb = jnp.concatenate([_blockdiag(bbr_t), _blockdiag(bbi_t)], axis=1).astype(BF16)
    wct = jnp.concatenate([_blockdiag(_c_to_t(c_re[0])), _blockdiag(-_c_to_t(c_im[0]))], axis=1).astype(BF16)
    ar8 = jnp.broadcast_to(abr, (SUB, GP))
    ai8 = jnp.broadcast_to(abi, (SUB, GP))

    def late_w2a(hid):
        unpack_group(_exchange_end(down_handle, hid, False, "gather_w_ffn1_down_wait"), down_grp)
        return wfull["w2_a"]

    def late_w13a(h):
        unpack_group(_exchange_end(up_handle, h, False, "gather_w_ffn1_up_wait"), up_grp)
        return wfull["w1_a"], wfull["w3_a"]

    x1, ffn1_saved, _ = _ffn_forward(x, g_ffn1 + start_tokens, sh1, sc1, gt1, late_w13a, None, late_w2a, "ffn1")
    unpack_group(_exchange_end(mix_handle, x1, False, "gather_w_mix_wait"), groups[1])

    h2 = _norm_mod(x1, g_mix, sh2, sc2, "mix_norm")
    h2f = _flat(h2)
    win = wfull["w_in"]
    win1, winu, win3 = win[:, :3 * CW], win[:, 3 * CW:3 * CW + SW], win[:, 3 * CW + SW:]
    p1 = _mm(h2f, win1, out_dtype=BF16, name="mix_in1").reshape(bsz, seq, 3 * CW)
    u = _mm(h2f, winu, name="mix_inu").reshape(bsz, seq, SW)
    p3 = _mm(h2f, win3, out_dtype=BF16, name="mix_in3").reshape(bsz, seq, 2 * D)

    ya_in = _conv_fwd(p1, convw8, "conv_fwd")
    ya = _mm(_flat(ya_in), wfull["w_conv_out"], out_dtype=BF16, name="conv_out").reshape(bsz, seq, D)

    xs, ys = _ssm_fwd(u, wb, wct, ar8, ai8, "ssm_fwd")
    s0, z, s1, s2, yb = _ssm_post(ys, u, d_skip, wfull["w_glu"], wfull["w_ssm_out"], "ssm_post")

    merged, mix, x2 = _merge_out(ya, yb, p3, wfull["w_out"], x1, gt2, "merge_out")
    unpack_group(_exchange_end(ffn2_handle, x2, False, "gather_w_ffn2_wait"), groups[2])
    x3, ffn2_saved, _ = _ffn_forward(x2, g_ffn2, sh3, sc3, gt3, wfull["w1_b"], wfull["w3_b"], wfull["w2_b"], "ffn2")

    dx3, lossvec, dgfin = _final_loss(x3, g_final.reshape(1, D), loss_target, "final_loss")
    loss = lax.psum(jnp.sum(lossvec), ("x", "y", "c"))

    gfull = {}
    dx2, (dsh3, dsc3, dgt3, dg3), (gfull["w1_b"], gfull["w3_b"], gfull["w2_b"]) = _ffn_backward(
        dx3, ffn2_saved, g_ffn2, sc3, gt3, wfull["w1_b"], wfull["w3_b"], wfull["w2_b"], "ffn2")

    def stack_group(grp):
        return [_stacked_from_full(gfull[n], rows, cols, split) for n, rows, cols, split in grp]

    st_ffn2 = stack_group(groups[2])
    h_ffn2 = _exchange_begin(st_ffn2, [_own_slab(s, chip) for s in st_ffn2], True, "scatter_ffn2_start")

    dmix, dya, dyb, dp3, dgt2 = _merge_bwd(dx2, gt2 + h_ffn2[4][0, 0], mix, ya, yb, p3, wfull["w_out"], "merge_bwd")
    gfull["w_out"] = _mm(_flat(merged), _flat(dmix), ta=True, out_dtype=BF16, name="gw_out")
    dp1, dconvw8 = _conv_bwd(dya, wfull["w_conv_out"], p1, convw8, "conv_bwd")
    gfull["w_conv_out"] = _mm(_flat(ya_in), _flat(dya), ta=True, out_dtype=BF16, name="gw_conv_out")
    ds0, dz, ddskip = _ssm_post_bwd(dyb, s0, z, u, d_skip, wfull["w_glu"], wfull["w_ssm_out"], "ssm_post_bwd")
    gfull["w_ssm_out"] = _mm(_flat(s2), _flat(dyb), ta=True, out_dtype=BF16, name="gw_ssm_out")
    gfull["w_glu"] = _mm(_flat(s1), _flat(dz), ta=True, out_dtype=BF16, name="gw_glu")
    du, dwb, dwct, dar8, dai8 = _ssm_bwd(ds0, u, xs, wb, wct, ar8, ai8, d_skip, "ssm_bwd")
    dx1, dsh2, dsc2, dgmix = _dh_norm_bwd([dp1, du, dp3], [win1, winu, win3], x1, g_mix, sc2, dx2, "mix_bwd_dh")
    gfull["w_in"] = jnp.concatenate([
        _mm(h2f, _flat(dp1), ta=True, out_dtype=BF16, name="gw_in1"), _mm(h2f, _flat(du), ta=True, out_dtype=BF16, name="gw_inu"),
        _mm(h2f, _flat(dp3), ta=True, out_dtype=BF16, name="gw_in3")], axis=1)

    st_mix = stack_group(groups[1])
    h_mix = _exchange_begin(st_mix, [_own_slab(s, chip) for s in st_mix], True, "scatter_mix_start")

    def swap_begin(recv, name):
        return _exchange_begin(recv, [lax.empty(v.shape, v.dtype) for v in recv], SIBLING, name)

    recv_ffn2 = _exchange_end(h_ffn2, dx1, True, "scatter_ffn2_wait")
    sw_ffn2 = swap_begin(recv_ffn2, "swap_ffn2_start")

    ffn1_names = {"w1": BIG[0], "w3": BIG[1], "w2": BIG[2]}
    ffn1_handles = {}

    def emit_ffn1(key, gw):
        n, rows, cols, split = ffn1_names[key]
        st = _stacked_from_full(gw, rows, cols, split)
        ffn1_handles[n] = _exchange_begin([st], [_own_slab(st, chip)], True, f"scatter_ffn1_{key}_start")
        return ffn1_handles[n][4][0, 0]

    grad_x, (dsh1, dsc1, dgt1, dg1), _ = _ffn_backward(
        dx1, ffn1_saved, g_ffn1, sc1, gt1 + (h_mix[4][0, 0] + sw_ffn2[4][0, 0]), wfull["w1_a"], wfull["w3_a"],
        wfull["w2_a"], "ffn1", emit=emit_ffn1)
    recv_mix = _exchange_end(h_mix, grad_x, True, "scatter_mix_wait")
    sw_mix = swap_begin(recv_mix, "swap_mix_start")
    dg1 = dg1 + sw_mix[4][0, 0]

    sbw = GP // SSM_SUPER
    d_are, d_aim, d_ldt, d_bre_t, d_bim_t = _ssm_disc_bwd(
        are, aim, ldt, bre_t, bim_t, jnp.sum(dar8, axis=0, keepdims=True), jnp.sum(dai8, axis=0, keepdims=True),
        _blockdiag_extract(dwb[:, :sbw]), _blockdiag_extract(dwb[:, sbw:]))
    d_cre = _c_from_t(_blockdiag_extract(dwct[:, :sbw]))
    d_cim = -_c_from_t(_blockdiag_extract(dwct[:, sbw:]))

    small_parts = [dg1, dgmix, dg3, dgfin, dconvw8[:3], d_are, d_aim, _from_t(d_bre_t), _from_t(d_bim_t), d_cre, d_cim,
                   jnp.sum(d_ldt.reshape(NG, NP), axis=1), ddskip]
    small_sizes = [int(p.size) for p in small_parts]
    n_small = sum(small_sizes)
    n_small_pad = -(-n_small // (SUB * PACK_COLS)) * (SUB * PACK_COLS)
    dmod = jnp.concatenate([dsh1, dsc1, dgt1, dsh2, dsc2, dgt2, dsh3, dsc3, dgt3], axis=2).reshape(bsz * NMOD * D)
    flat = jnp.concatenate([p.reshape(-1) for p in small_parts] + [jnp.zeros((n_small_pad - n_small,), F32), dmod])
    grads, deltas, new_m, new_v = {}, {}, {}, {}

    def adamw_group(grp, recv_own, recv_sib, token):
        for (n, _, _, _), r_own, r_sib in zip(grp, recv_own, recv_sib):
            wmv, _ = lax.optimization_barrier(
                ((_view(args[n], n), _view(args["m_" + n], n), _view(args["v_" + n], n)), token))
            res = _adamw_big(*wmv, r_own, r_sib, f"adamw_{n}")
            grads[n], deltas[n], new_m[n], new_v[n] = [_view(r, n) for r in res]
        return deltas[grp[-1][0]]

    flat2d = flat.reshape(-1, PACK_COLS)
    h_small = _exchange_begin(None, [_own_slab(jnp.broadcast_to(flat2d[None], (N_CHIPS,) + flat2d.shape), chip)],
                              False, "gather_small_chips_start")
    recv_ffn2, sib_ffn2 = _exchange_end(sw_ffn2, flat2d, SIBLING, "swap_ffn2_wait", with_srcs=True)
    done = adamw_group(groups[2], recv_ffn2, sib_ffn2, h_small[4])
    recv_mix, sib_mix = _exchange_end(sw_mix, done, SIBLING, "swap_mix_wait", with_srcs=True)
    done = adamw_group(groups[1], recv_mix, sib_mix, h_small[4])
    own_c = _exchange_end(h_small, done, False, "gather_small_chips_wait")[0]
    sib_c = _swap_sibling([own_c], "gather_small_sib")[0]
    core0 = jnp.where(mc == 0, own_c, sib_c)
    core1 = jnp.where(mc == 0, sib_c, own_c)
    allg = jnp.stack([core0, core1], axis=1).reshape(N_DEV, -1)
    small = _sum_slabs(allg[:, :n_small_pad].reshape(N_DEV, -1, PACK_COLS), "sum_small").reshape(-1)
    sg, o = [], 0
    for p, sz in zip(small_parts, small_sizes):
        sg.append(small[o:o + sz].reshape(p.shape))
        o += sz
    (g_g1, g_gmix, g_g3, g_gfin, g_convw, g_are, g_aim, g_bre, g_bim, g_cre, g_cim, g_ldt, g_dskip) = sg

    dmod_all = allg[:, n_small_pad:].reshape(nb, NMOD * D)
    dmod_shard = _select(dmod_all.reshape(nb, N_CHIPS, nmod_shard).transpose(1, 0, 2), chip)
    g_wada, g_bada = _ada_bwd(c_all, dmod_shard, dmod_all)

    recv_ffn1 = [_exchange_end(ffn1_handles[n], g_wada, True, f"scatter_ffn1_{n}_wait")[0] for n, _, _, _ in groups[0]]
    sib_ffn1 = _swap_sibling(recv_ffn1, "swap_ffn1")

    adamw_group(groups[0], recv_ffn1, sib_ffn1, jnp.zeros((), F32))
    grads["w_ada"] = g_wada[None]
    deltas["w_ada"], new_m["w_ada"], new_v["w_ada"] = _adamw_rows(w_ada, m_w_ada, v_w_ada, g_wada, "adamw_w_ada")

    g_convw_shard = _select(g_convw.reshape(3, N_CHIPS, CW // N_CHIPS).transpose(1, 0, 2), chip)
    small_g = {"b_ada": g_bada, "g_ffn1": g_g1, "g_mix": g_gmix, "g_ffn2": g_g3, "g_final": g_gfin,
               "conv_w": g_convw_shard, "a_re": g_are, "a_im": g_aim, "b_re": g_bre, "b_im": g_bim,
               "c_re": g_cre, "c_im": g_cim, "log_dt": g_ldt, "d_skip": g_dskip}
    small_names = list(small_g)
    sizes = [int(args[n].size) for n in small_names]
    tot = sum(sizes)
    tot_pad = -(-tot // (SUB * PACK_COLS)) * (SUB * PACK_COLS)

    def pack(get):
        return jnp.concatenate([get(n).reshape(-1) for n in small_names] + [jnp.zeros((tot_pad - tot,), F32)]).reshape(
            -1, PACK_COLS)

    res = _adamw_plain(pack(lambda n: args[n]), pack(lambda n: args["m_" + n]), pack(lambda n: args["v_" + n]),
                       pack(lambda n: small_g[n]), "adamw_small")
    o = 0
    for n, sz in zip(small_names, sizes):
        shp = args[n].shape
        grads[n] = small_g[n].reshape(shp)
        deltas[n], new_m[n], new_v[n] = [r.reshape(-1)[o:o + sz].reshape(shp) for r in res]
        o += sz

    return (loss, grad_x, *[grads[n] for n in names], *[deltas[n] for n in names],
            *[new_m[n] for n in names], *[new_v[n] for n in names])
```

```python
import math

import jax
import jax.numpy as jnp
from jax import lax
from jax.experimental import pallas as pl
from jax.experimental.pallas import tpu as pltpu

F32 = jnp.float32
BF16 = jnp.bfloat16
SDS = jax.ShapeDtypeStruct
MESH = pl.DeviceIdType.MESH

D = 1024
DFF = 2816
CW = 1024
SW = 512
NG, NP, NH = 32, 64, 16
GP = NG * NP
NMOD = 9
EPS = 1e-6
N_CHIPS = 4
N_DEV = 8
SUB = 8
LANE = 128
SSM_SUPER = 4
VMEM_LIMIT = 50 * 1024 * 1024

LR, B1, B2, AEPS, WD, STEP = 0.001, 0.9, 0.999, 1e-08, 0.01, 10
BC1 = 1.0 - B1 ** STEP
BC2 = 1.0 - B2 ** STEP


def _cp(*sem):
    return pltpu.CompilerParams(dimension_semantics=sem or None, vmem_limit_bytes=VMEM_LIMIT)


def _pick_tile(n, cands):
    for t in cands:
        if t <= n and n % t == 0:
            return t
    return n


def _dot(a, b):
    return lax.dot_general(a, b, (((1,), (0,)), ((), ())), preferred_element_type=F32)


def _dot_nt(a, b):
    return lax.dot_general(a, b, (((1,), (1,)), ((), ())), preferred_element_type=F32)


def _dot_tn(a, b):
    return lax.dot_general(a, b, (((0,), (0,)), ((), ())), preferred_element_type=F32)


def _row(tm, width, col=0):
    return pl.BlockSpec((1, tm, width), lambda b, i, *_: (b, i, col))


def _seqvec(width):
    return pl.BlockSpec((1, 1, width), lambda b, *_: (b, 0, 0))


def _full2(shape):
    return pl.BlockSpec(shape, lambda *_: (0, 0))


def _sigmoid(x):
    return jax.nn.sigmoid(x)


def _mm(a, b, *, ta=False, tb=False, out_dtype=F32, name):
    if ta:
        kdim, m = a.shape
    else:
        m, kdim = a.shape
    n = b.shape[0] if tb else b.shape[1]
    tm = _pick_tile(m, (1408, 1024, 512, 256, 128))
    tn = _pick_tile(n, (1408, 1024, 512, 256, 128))
    tk = _pick_tile(kdim, (1024, 512, 256, 128))
    nk = kdim // tk

    def body(a_ref, b_ref, o_ref, acc_ref):
        k = pl.program_id(2)

        @pl.when(k == 0)
        def _():
            acc_ref[...] = jnp.zeros_like(acc_ref)

        av = a_ref[...].astype(BF16)
        bv = b_ref[...].astype(BF16)
        dn = (((0 if ta else 1,), (1 if tb else 0,)), ((), ()))
        acc_ref[...] += lax.dot_general(av, bv, dn, preferred_element_type=F32)

        @pl.when(k == nk - 1)
        def _():
            o_ref[...] = acc_ref[...].astype(out_dtype)

    a_spec = pl.BlockSpec((tk, tm), lambda i, j, k: (k, i)) if ta else pl.BlockSpec((tm, tk), lambda i, j, k: (i, k))
    b_spec = pl.BlockSpec((tn, tk), lambda i, j, k: (j, k)) if tb else pl.BlockSpec((tk, tn), lambda i, j, k: (k, j))
    return pl.pallas_call(
        body, name=name, grid=(m // tm, n // tn, nk),
        in_specs=[a_spec, b_spec],
        out_specs=pl.BlockSpec((tm, tn), lambda i, j, k: (i, j)),
        out_shape=SDS((m, n), out_dtype),
        scratch_shapes=[pltpu.VMEM((tm, tn), F32)],
        compiler_params=_cp("parallel", "parallel", "arbitrary"),
    )(a, b)


def _flat(a):
    return a.reshape(-1, a.shape[-1])


def _norm_mod(x, g, sh, sc, name):
    bsz, seq, dm = x.shape
    tm = _pick_tile(seq, (512, 256, 128))

    def body(x_ref, g_ref, sh_ref, sc_ref, o_ref):
        xf = x_ref[0]
        r = lax.rsqrt(jnp.mean(xf * xf, axis=-1, keepdims=True) + EPS)
        hn = xf * r * g_ref[...]
        o_ref[0] = (hn * (1.0 + sc_ref[0]) + sh_ref[0]).astype(BF16)

    return pl.pallas_call(
        body, name=name, grid=(bsz, seq // tm),
        in_specs=[_row(tm, dm), _full2((1, dm)), _seqvec(dm), _seqvec(dm)],
        out_specs=_row(tm, dm), out_shape=SDS((bsz, seq, dm), BF16),
        compiler_params=_cp("parallel", "parallel"),
    )(x, g, sh, sc)


def _swiglu_up(h, w1, w3, name):
    bsz, seq, dm = h.shape
    nf = w1.shape[0]
    tm = _pick_tile(seq, (512, 256, 128))
    tn = _pick_tile(nf, (1408, 512, 256, 128))

    def body(h_ref, w1_ref, w3_ref, a_ref, b_ref, hid_ref):
        hv = h_ref[0]
        a = _dot_nt(hv, w1_ref[...])
        b = _dot_nt(hv, w3_ref[...])
        sg = _sigmoid(a)
        sa = a * sg
        a_ref[0] = (b * (sg * (1.0 + a * (1.0 - sg)))).astype(BF16)
        b_ref[0] = sa.astype(BF16)
        hid_ref[0] = (sa * b).astype(BF16)

    wspec = pl.BlockSpec((tn, dm), lambda n, b, i: (n, 0))
    ospec = pl.BlockSpec((1, tm, tn), lambda n, b, i: (b, i, n))
    shp = SDS((bsz, seq, nf), BF16)
    return pl.pallas_call(
        body, name=name, grid=(nf // tn, bsz, seq // tm),
        in_specs=[pl.BlockSpec((1, tm, dm), lambda n, b, i: (b, i, 0)), wspec, wspec],
        out_specs=[ospec, ospec, ospec], out_shape=[shp, shp, shp],
        compiler_params=_cp("parallel", "parallel", "parallel"),
    )(h, w1, w3)


def _ffn_down(hid, w2, x, gt, name):
    bsz, seq, nf = hid.shape
    dm = w2.shape[1]
    tm = _pick_tile(seq, (512, 256, 128))

    def body(hid_ref, w2_ref, x_ref, gt_ref, f_ref, xo_ref):
        f = _dot(hid_ref[0], w2_ref[...])
        f_ref[0] = f.astype(BF16)
        xo_ref[0] = x_ref[0] + 0.5 * gt_ref[0] * f

    shp = SDS((bsz, seq, dm), F32)
    return pl.pallas_call(
        body, name=name, grid=(bsz, seq // tm),
        in_specs=[_row(tm, nf), _full2((nf, dm)), _row(tm, dm), _seqvec(dm)],
        out_specs=[_row(tm, dm), _row(tm, dm)], out_shape=[SDS((bsz, seq, dm), BF16), shp],
        compiler_params=_cp("parallel", "parallel"),
    )(hid, w2, x, gt)


def _ffn_bwd_hid(dxo, gt, f, a, b, w2, name):
    bsz, seq, dm = dxo.shape
    nf = a.shape[2]
    tm = _pick_tile(seq, (512, 256, 128))
    tn = _pick_tile(nf, (1408, 512, 256, 128))

    def body(dxo_ref, gt_ref, f_ref, a_ref, b_ref, w2_ref, dfs_ref, da_ref, db_ref, dgt_ref):
        i = pl.program_id(1)
        n = pl.program_id(2)

        @pl.when(n == 0)
        def _():
            dxo = dxo_ref[0]
            dfs_ref[0] = (0.5 * gt_ref[0] * dxo).astype(BF16)
            part = jnp.sum(0.5 * dxo * f_ref[0].astype(F32), axis=0, keepdims=True)

            @pl.when(i == 0)
            def _():
                dgt_ref[0] = part

            @pl.when(i > 0)
            def _():
                dgt_ref[0] += part

        dhid = _dot_nt(dfs_ref[0], w2_ref[pl.ds(pl.multiple_of(n * tn, tn), tn), :])
        dh16 = dhid.astype(BF16)
        da_ref[0] = dh16 * a_ref[0]
        db_ref[0] = dh16 * b_ref[0]

    hspec = pl.BlockSpec((1, tm, tn), lambda b, i, n: (b, i, n))
    return pl.pallas_call(
        body, name=name, grid=(bsz, seq // tm, nf // tn),
        in_specs=[_row(tm, dm), _seqvec(dm), _row(tm, dm), hspec, hspec, _full2((nf, dm))],
        out_specs=[_row(tm, dm), hspec, hspec, _seqvec(dm)],
        out_shape=[SDS((bsz, seq, dm), BF16), SDS((bsz, seq, nf), BF16), SDS((bsz, seq, nf), BF16),
                   SDS((bsz, 1, dm), F32)],
        compiler_params=_cp("arbitrary", "arbitrary", "arbitrary"),
    )(dxo, gt, f, a, b, w2)


def _dh_norm_bwd(pieces, weights, x, g, sc, dxo, name, transposed=False):
    bsz, seq, dm = x.shape
    tm = _pick_tile(seq, (512, 256, 128))
    npc = len(pieces)
    dot = _dot if transposed else _dot_nt

    def body(*refs):
        p_refs = refs[:npc]
        w_hbm = refs[npc:2 * npc]
        x_ref, g_ref, sc_ref, dxo_ref, dx_ref, dsh_ref, dsc_ref, dg_ref = refs[2 * npc:2 * npc + 8]
        w_refs = refs[2 * npc + 8:]
        b = pl.program_id(0)
        i = pl.program_id(1)

        @pl.when((i == 0) & (b == 0))
        def _():
            for src, dst in zip(w_hbm, w_refs):
                pltpu.sync_copy(src, dst)

        dh = dot(p_refs[0][0], w_refs[0][...])
        for j in range(1, npc):
            dh = dh + dot(p_refs[j][0], w_refs[j][...])
        xf = x_ref[0]
        gv = g_ref[...]
        r = lax.rsqrt(jnp.mean(xf * xf, axis=-1, keepdims=True) + EPS)
        xhat = xf * r
        dhn = dh * (1.0 + sc_ref[0])
        p_sh = jnp.sum(dh, axis=0, keepdims=True)
        p_sc = jnp.sum(dh * (xhat * gv), axis=0, keepdims=True)
        p_g = jnp.sum(dhn * xhat, axis=0, keepdims=True)
        dxh = dhn * gv
        dx_ref[0] = dxo_ref[0] + r * (dxh - xhat * jnp.mean(dxh * xhat, axis=-1, keepdims=True))

        @pl.when(i == 0)
        def _():
            dsh_ref[0] = p_sh
            dsc_ref[0] = p_sc

        @pl.when(i > 0)
        def _():
            dsh_ref[0] += p_sh
            dsc_ref[0] += p_sc

        @pl.when((i == 0) & (b == 0))
        def _():
            dg_ref[...] = p_g

        @pl.when((i > 0) | (b > 0))
        def _():
            dg_ref[...] += p_g

    return pl.pallas_call(
        body, name=name, grid=(bsz, seq // tm),
        in_specs=[_row(tm, p.shape[2]) for p in pieces] + [pl.BlockSpec(memory_space=pl.ANY)] * npc + [
            _row(tm, dm), _full2((1, dm)), _seqvec(dm), _row(tm, dm)],
        out_specs=[_row(tm, dm), _seqvec(dm), _seqvec(dm), _full2((1, dm))],
        out_shape=[SDS((bsz, seq, dm), F32), SDS((bsz, 1, dm), F32), SDS((bsz, 1, dm), F32), SDS((1, dm), F32)],
        scratch_shapes=[pltpu.VMEM(w.shape, w.dtype) for w in weights],
        compiler_params=_cp("arbitrary", "arbitrary"),
    )(*pieces, *weights, x, g, sc, dxo)


HALO = 16


def _conv_core(gc, v, gch, vh, w, first):
    cv = gc * v
    halo = jnp.where(first, 0.0, gch * vh)
    ext = jnp.concatenate([halo, cv], axis=0)
    cv1 = pltpu.roll(ext, 1, 0)[HALO:]
    cv2 = pltpu.roll(ext, 2, 0)[HALO:]
    conv = w[0:1] * cv2 + w[1:2] * cv1 + w[2:3] * cv
    return cv, cv1, cv2, conv


def _prev_halo(tm, col):
    return pl.BlockSpec((1, HALO, CW), lambda b, i, *_: (b, jnp.maximum(i * (tm // HALO) - 1, 0), col))


def _next_halo(tm, seq, col):
    return pl.BlockSpec((1, HALO, CW), lambda b, i, *_: (b, jnp.minimum((i + 1) * (tm // HALO), seq // HALO - 1), col))


def _conv_fwd(p1, convw8, name):
    bsz, seq, _ = p1.shape
    tm = _pick_tile(seq, (512, 256, 128))

    def body(gb_ref, gc_ref, v_ref, gch_ref, vh_ref, w_ref, o_ref):
        first = pl.program_id(1) == 0
        _, _, _, conv = _conv_core(gc_ref[0].astype(F32), v_ref[0].astype(F32), gch_ref[0].astype(F32),
                                   vh_ref[0].astype(F32), w_ref[...], first)
        o_ref[0] = (gb_ref[0].astype(F32) * conv).astype(BF16)

    return pl.pallas_call(
        body, name=name, grid=(bsz, seq // tm),
        in_specs=[_row(tm, CW, 0), _row(tm, CW, 1), _row(tm, CW, 2), _prev_halo(tm, 1), _prev_halo(tm, 2),
                  _full2((8, CW))],
        out_specs=_row(tm, CW), out_shape=SDS((bsz, seq, CW), BF16),
        compiler_params=_cp("parallel", "parallel"),
    )(p1, p1, p1, p1, p1, convw8)


def _conv_bwd(dya, wco, p1, convw8, name):
    bsz, seq, _ = p1.shape
    tm = _pick_tile(seq, (512, 256, 128))
    nt = seq // tm
    ext_rows = tm + HALO

    def body(dya_ref, dyan_ref, wco_ref, gb_ref, gbn_ref, gc_ref, v_ref, gch_ref, vh_ref, w_ref, dp_ref, dw_ref):
        b = pl.program_id(0)
        i = pl.program_id(1)
        w = w_ref[...]
        gc = gc_ref[0].astype(F32)
        vv = v_ref[0].astype(F32)
        cv, cv1, cv2, conv = _conv_core(gc, vv, gch_ref[0].astype(F32), vh_ref[0].astype(F32), w, i == 0)
        dya_ext = jnp.concatenate([dya_ref[0], dyan_ref[0]], axis=0)
        dyain_ext = _dot_nt(dya_ext, wco_ref[...])
        gb_ext = jnp.concatenate([gb_ref[0], gbn_ref[0]], axis=0).astype(F32)
        rows = lax.broadcasted_iota(jnp.int32, (ext_rows, 1), 0)
        dconv_ext = jnp.where((rows < tm) | (i < nt - 1), dyain_ext * gb_ext, 0.0)
        dconv = dconv_ext[:tm]
        dconv1 = pltpu.roll(dconv_ext, ext_rows - 1, 0)[:tm]
        dconv2 = pltpu.roll(dconv_ext, ext_rows - 2, 0)[:tm]
        dcv = w[2:3] * dconv + w[1:2] * dconv1 + w[0:1] * dconv2
        dp_ref[0, :, 0:CW] = (dyain_ext[:tm] * conv).astype(BF16)
        dp_ref[0, :, CW:2 * CW] = (dcv * vv).astype(BF16)
        dp_ref[0, :, 2 * CW:3 * CW] = (dcv * gc).astype(BF16)
        g0 = jnp.sum(dconv * cv2, axis=0, keepdims=True)
        g1 = jnp.sum(dconv * cv1, axis=0, keepdims=True)
        g2 = jnp.sum(dconv * cv, axis=0, keepdims=True)
        upd = jnp.concatenate([g0, g1, g2, jnp.zeros((5, CW), F32)], axis=0)

        @pl.when((i == 0) & (b == 0))
        def _():
            dw_ref[...] = upd

        @pl.when((i > 0) | (b > 0))
        def _():
            dw_ref[...] += upd

    return pl.pallas_call(
        body, name=name, grid=(bsz, seq // tm),
        in_specs=[_row(tm, CW), _next_halo(tm, seq, 0), _full2((CW, D)),
                  _row(tm, CW, 0), _next_halo(tm, seq, 0), _row(tm, CW, 1), _row(tm, CW, 2),
                  _prev_halo(tm, 1), _prev_halo(tm, 2), _full2((8, CW))],
        out_specs=[_row(tm, 3 * CW), _full2((8, CW))],
        out_shape=[SDS((bsz, seq, 3 * CW), BF16), SDS((8, CW), F32)],
        compiler_params=_cp("arbitrary", "arbitrary"),
    )(dya, dya, wco, p1, p1, p1, p1, p1, p1, convw8)


def _disc(are, aim, ldt, bre, bim):
    dt = jnp.exp(ldt)
    mag = jnp.exp(are * dt)
    ang = aim * dt
    abr = mag * jnp.cos(ang)
    abi = mag * jnp.sin(ang)
    nr = abr - 1.0
    den = are * are + aim * aim
    cr = (nr * are + abi * aim) / den
    ci = (abi * are - nr * aim) / den
    return abr, abi, cr * bre - ci * bim, cr * bim + ci * bre


def _ssm_disc(are, aim, ldt, bre_t, bim_t):
    def body(are_ref, aim_ref, ldt_ref, bre_ref, bim_ref, abr_ref, abi_ref, bbr_ref, bbi_ref):
        abr, abi, bbr, bbi = _disc(are_ref[...], aim_ref[...], ldt_ref[...], bre_ref[...], bim_ref[...])
        abr_ref[...] = abr
        abi_ref[...] = abi
        bbr_ref[...] = bbr
        bbi_ref[...] = bbi

    v1, vh = SDS((1, GP), F32), SDS((NH, GP), F32)
    return pl.pallas_call(body, name="ssm_disc", out_shape=[v1, v1, vh, vh], compiler_params=_cp())(
        are, aim, ldt, bre_t, bim_t)


def _ssm_disc_bwd(are, aim, ldt, bre_t, bim_t, dabr, dabi, dbbr, dbbi):
    def body(are_ref, aim_ref, ldt_ref, bre_ref, bim_ref, g0, g1, g2, g3, o0, o1, o2, o3, o4):
        prim = (are_ref[...], aim_ref[...], ldt_ref[...], bre_ref[...], bim_ref[...])
        _, vjp = jax.vjp(_disc, *prim)
        d_are, d_aim, d_ldt, d_bre, d_bim = vjp((g0[...], g1[...], g2[...], g3[...]))
        o0[...] = d_are
        o1[...] = d_aim
        o2[...] = d_ldt
        o3[...] = d_bre
        o4[...] = d_bim

    v1, vh = SDS((1, GP), F32), SDS((NH, GP), F32)
    return pl.pallas_call(body, name="ssm_disc_bwd", out_shape=[v1, v1, v1, vh, vh], compiler_params=_cp())(
        are, aim, ldt, bre_t, bim_t, dabr, dabi, dbbr, dbbi)


def _scan_chunk(buf_ref, nt, bsz, ar, ai, init_r, init_i, reverse):
    nsub = SUB // bsz
    row = lax.broadcasted_iota(jnp.int32, (SUB, GP), 0)
    shift = ((SUB - bsz) if reverse else bsz) % SUB
    order = list(range(nsub - 1, -1, -1)) if reverse else list(range(nsub))

    def step(j, carry):
        pr, pi = carry
        jj = (nt - 1 - j) if reverse else j
        off = pl.multiple_of(jj * SUB, SUB)
        br = buf_ref[pl.ds(off, SUB), 0:GP]
        bi = buf_ref[pl.ds(off, SUB), GP:2 * GP]
        nr, ni = pr, pi
        for s in order:
            sr, si = nr, ni
            if shift:
                sr = pltpu.roll(sr, shift, 0)
                si = pltpu.roll(si, shift, 0)
            cr = ar * sr - ai * si + br
            ci = ar * si + ai * sr + bi
            if nsub == 1:
                nr, ni = cr, ci
            else:
                m = (row >= s * bsz) & (row < (s + 1) * bsz)
                nr = jnp.where(m, cr, nr)
                ni = jnp.where(m, ci, ni)
        buf_ref[pl.ds(off, SUB), 0:GP] = nr
        buf_ref[pl.ds(off, SUB), GP:2 * GP] = ni
        return nr, ni

    return lax.fori_loop(0, nt, step, (init_r, init_i))


def _ssm_chunk_rows(total_rows, bsz):
    return min(total_rows, 64 * bsz)


def _interleave(src_ref, tmp_ref, bsz, steps):
    nl = tmp_ref.shape[0]
    for b in range(bsz):
        for j in range(nl):
            tmp_ref.at[j][pl.ds(b, steps, stride=bsz), :] = src_ref[b, :, j * LANE:(j + 1) * LANE]
    return jnp.concatenate([tmp_ref[j] for j in range(nl)], axis=1)


def _deinterleave(val, tmp_ref, dst_ref, bsz, steps, skip=None):
    nl = tmp_ref.shape[0]
    for j in range(nl):
        tmp_ref[j] = val[:, j * LANE:(j + 1) * LANE]
    for b in range(bsz):
        for j in range(nl):
            lanes = slice(j * LANE, (j + 1) * LANE)
            v = tmp_ref.at[j][pl.ds(b, steps, stride=bsz), :]
            if skip is not None:
                v = v + skip[0][b, :, lanes] * skip[1][:, lanes]
            dst_ref[b, :, lanes] = v.astype(dst_ref.dtype)


SSM_UB = SW // SSM_SUPER
SSM_SB = GP // SSM_SUPER


def _sb_cols(s, half):
    return slice(half * GP + s * SSM_SB, half * GP + (s + 1) * SSM_SB)


def _ssm_in(v16, w_ref, x_ref):
    for s in range(SSM_SUPER):
        vs = v16[:, s * SSM_UB:(s + 1) * SSM_UB]
        for half in range(2):
            x_ref[:, _sb_cols(s, half)] = _dot(vs, w_ref[s * SSM_UB:(s + 1) * SSM_UB, half * SSM_SB:(half + 1) * SSM_SB])


def _ssm_out(x16, w_ref):
    outs = []
    for s in range(SSM_SUPER):
        rows = slice(s * SSM_UB, (s + 1) * SSM_UB)
        outs.append(_dot_nt(x16[:, _sb_cols(s, 0)], w_ref[rows, 0:SSM_SB])
                    + _dot_nt(x16[:, _sb_cols(s, 1)], w_ref[rows, SSM_SB:2 * SSM_SB]))
    return jnp.concatenate(outs, axis=1)


def _ssm_fwd(u, wb, wct, ar8, ai8, name):
    bsz, seq, _ = u.shape
    rt = seq * bsz
    r = _ssm_chunk_rows(rt, bsz)
    nt = r // SUB
    steps = r // bsz

    def body(u_ref, wb_hbm, wct_hbm, ar_ref, ai_ref, x_ref, y_ref, wb_ref, wct_ref, st_ref, tmp_ref):
        @pl.when(pl.program_id(0) == 0)
        def _():
            pltpu.sync_copy(wb_hbm, wb_ref)
            pltpu.sync_copy(wct_hbm, wct_ref)
            st_ref[...] = jnp.zeros_like(st_ref)

        _ssm_in(_interleave(u_ref, tmp_ref, bsz, steps).astype(BF16), wb_ref, x_ref)
        fr, fi = _scan_chunk(x_ref, nt, bsz, ar_ref[...], ai_ref[...], st_ref[:, 0:GP], st_ref[:, GP:2 * GP], False)
        st_ref[:, 0:GP] = fr
        st_ref[:, GP:2 * GP] = fi
        _deinterleave(_ssm_out(x_ref[...].astype(BF16), wct_ref), tmp_ref, y_ref, bsz, steps)

    anyspec = pl.BlockSpec(memory_space=pl.ANY)
    seqs = pl.BlockSpec((bsz, steps, SW), lambda i: (0, i, 0))
    return pl.pallas_call(
        body, name=name, grid=(rt // r,),
        in_specs=[seqs, anyspec, anyspec, _full2((SUB, GP)), _full2((SUB, GP))],
        out_specs=[pl.BlockSpec((r, 2 * GP), lambda i: (i, 0)), seqs],
        out_shape=[SDS((rt, 2 * GP), F32), SDS((bsz, seq, SW), F32)],
        scratch_shapes=[pltpu.VMEM(wb.shape, BF16), pltpu.VMEM(wct.shape, BF16), pltpu.VMEM((SUB, 2 * GP), F32),
                        pltpu.VMEM((SW // LANE, r, LANE), F32)],
        compiler_params=_cp("arbitrary"),
    )(u, wb, wct, ar8, ai8)


def _ssm_bwd(dy, u, xs, wb, wct, ar8, ai8, dskip, name):
    bsz, seq, _ = u.shape
    rt = seq * bsz
    r = _ssm_chunk_rows(rt, bsz)
    nt = r // SUB
    nc = rt // r
    steps = r // bsz
    ub = SW // SSM_SUPER
    sb = GP // SSM_SUPER

    def body(dys_ref, us_ref, x_ref, xh_ref, wb_hbm, wct_hbm, ar_ref, ai_ref, d_ref,
             dus_ref, dwb_hbm, dwct_hbm, dar_ref, dai_ref, wb_ref, wct_ref, g_ref, st_ref, awb_ref, awct_ref,
             tmp_ref):
        i = pl.program_id(0)
        dyb = _interleave(dys_ref, tmp_ref, bsz, steps).astype(BF16)
        ub16 = _interleave(us_ref, tmp_ref, bsz, steps).astype(BF16)

        @pl.when(i == 0)
        def _():
            pltpu.sync_copy(wb_hbm, wb_ref)
            pltpu.sync_copy(wct_hbm, wct_ref)
            st_ref[...] = jnp.zeros_like(st_ref)
            awb_ref[...] = jnp.zeros_like(awb_ref)
            awct_ref[...] = jnp.zeros_like(awct_ref)
            dar_ref[...] = jnp.zeros_like(dar_ref)
            dai_ref[...] = jnp.zeros_like(dai_ref)

        _ssm_in(dyb, wct_ref, g_ref)
        ar = ar_ref[...]
        ai = ai_ref[...]
        fr, fi = _scan_chunk(g_ref, nt, bsz, ar, -ai, st_ref[:, 0:GP], st_ref[:, GP:2 * GP], True)
        st_ref[:, 0:GP] = fr
        st_ref[:, GP:2 * GP] = fi

        gb = g_ref[...].astype(BF16)
        _deinterleave(_ssm_out(gb, wb_ref), tmp_ref, dus_ref, bsz, steps, skip=(dys_ref, d_ref))
        xb16 = x_ref[...].astype(BF16)
        for s in range(SSM_SUPER):
            us = ub16[:, s * ub:(s + 1) * ub]
            ds = dyb[:, s * ub:(s + 1) * ub]
            for half in range(2):
                cols = slice(half * GP + s * sb, half * GP + (s + 1) * sb)
                ocols = slice(half * sb, (half + 1) * sb)
                awb_ref[s * ub:(s + 1) * ub, ocols] += _dot_tn(us, gb[:, cols])
                awct_ref[s * ub:(s + 1) * ub, ocols] += _dot_tn(ds, xb16[:, cols])

        gr = g_ref[:, 0:GP]
        gi = g_ref[:, GP:2 * GP]
        xsr = pltpu.roll(x_ref[:, 0:GP], bsz, 0)
        xsi = pltpu.roll(x_ref[:, GP:2 * GP], bsz, 0)
        inner = lax.broadcasted_iota(jnp.int32, (r, 1), 0) >= bsz
        t_r = jnp.where(inner, gr * xsr + gi * xsi, 0.0)
        t_i = jnp.where(inner, gi * xsr - gr * xsi, 0.0)
        acc_r = jnp.sum(t_r.reshape(nt, SUB, GP), axis=0)
        acc_i = jnp.sum(t_i.reshape(nt, SUB, GP), axis=0)
        hr = xh_ref[:, 0:GP]
        hi = xh_ref[:, GP:2 * GP]
        if bsz % SUB:
            hr = pltpu.roll(hr, bsz, 0)
            hi = pltpu.roll(hi, bsz, 0)
        edge = (lax.broadcasted_iota(jnp.int32, (SUB, 1), 0) < bsz) & (i < nc - 1)
        g0r = g_ref[0:SUB, 0:GP]
        g0i = g_ref[0:SUB, GP:2 * GP]
        dar_ref[...] += acc_r + jnp.where(edge, g0r * hr + g0i * hi, 0.0)
        dai_ref[...] += acc_i + jnp.where(edge, g0i * hr - g0r * hi, 0.0)

        @pl.when(i == nc - 1)
        def _():
            pltpu.sync_copy(awb_ref, dwb_hbm)
            pltpu.sync_copy(awct_ref, dwct_hbm)

    anyspec = pl.BlockSpec(memory_space=pl.ANY)
    rev = lambda i: (nc - 1 - i, 0)
    seqs = pl.BlockSpec((bsz, steps, SW), lambda i: (0, nc - 1 - i, 0))
    wshape = (SW, 2 * sb)
    return pl.pallas_call(
        body, name=name, grid=(nc,),
        in_specs=[seqs, seqs, pl.BlockSpec((r, 2 * GP), rev),
                  pl.BlockSpec((SUB, 2 * GP), lambda i: (jnp.maximum((nc - 1 - i) * nt - 1, 0), 0)),
                  anyspec, anyspec, _full2((SUB, GP)), _full2((SUB, GP)), _full2((1, SW))],
        out_specs=[seqs, anyspec, anyspec, _full2((SUB, GP)), _full2((SUB, GP))],
        out_shape=[SDS((bsz, seq, SW), BF16), SDS(wshape, F32), SDS(wshape, F32), SDS((SUB, GP), F32),
                   SDS((SUB, GP), F32)],
        scratch_shapes=[pltpu.VMEM(wb.shape, BF16), pltpu.VMEM(wct.shape, BF16),
                        pltpu.VMEM((r, 2 * GP), F32), pltpu.VMEM((SUB, 2 * GP), F32),
                        pltpu.VMEM(wshape, F32), pltpu.VMEM(wshape, F32),
                        pltpu.VMEM((SW // LANE, r, LANE), F32)],
        compiler_params=_cp("arbitrary"),
    )(dy, u, xs, xs, wb, wct, ar8, ai8, dskip)


GELU_C = math.sqrt(2.0 / math.pi)


def _gelu(x):
    return 0.5 * x * (1.0 + jnp.tanh(GELU_C * (x + 0.044715 * x * x * x)))


def _gelu_grad(x):
    th = jnp.tanh(GELU_C * (x + 0.044715 * x * x * x))
    return 0.5 * (1.0 + th) + 0.5 * x * (1.0 - th * th) * GELU_C * (1.0 + 3.0 * 0.044715 * x * x)


def _ssm_post(ys, u, dskip, wglu, wso, name):
    bsz, seq, _ = ys.shape
    tm = _pick_tile(seq, (512, 256, 128))

    def body(ys_ref, u_ref, d_ref, wg_ref, wo_ref, s0_ref, z_ref, s1_ref, s2_ref, yb_ref):
        s0 = ys_ref[0] + d_ref[...] * u_ref[0]
        s1 = _gelu(s0)
        s1b = s1.astype(BF16)
        z = _dot(s1b, wg_ref[...])
        s2b = (s1 * _sigmoid(z)).astype(BF16)
        s0_ref[0] = s0
        z_ref[0] = z
        s1_ref[0] = s1b
        s2_ref[0] = s2b
        yb_ref[0] = _dot(s2b, wo_ref[...]).astype(BF16)

    return pl.pallas_call(
        body, name=name, grid=(bsz, seq // tm),
        in_specs=[_row(tm, SW), _row(tm, SW), _full2((1, SW)), _full2((SW, SW)), _full2((SW, D))],
        out_specs=[_row(tm, SW), _row(tm, SW), _row(tm, SW), _row(tm, SW), _row(tm, D)],
        out_shape=[SDS((bsz, seq, SW), F32), SDS((bsz, seq, SW), F32), SDS((bsz, seq, SW), BF16),
                   SDS((bsz, seq, SW), BF16), SDS((bsz, seq, D), BF16)],
        compiler_params=_cp("parallel", "parallel"),
    )(ys, u, dskip, wglu, wso)


def _ssm_post_bwd(dyb, s0, z, u, dskip, wglu, wso, name):
    bsz, seq, _ = s0.shape
    tm = _pick_tile(seq, (512, 256, 128))

    def body(dyb_ref, s0_ref, z_ref, u_ref, wg_ref, wo_ref, ds0_ref, dz_ref, dd_ref):
        b = pl.program_id(0)
        i = pl.program_id(1)
        ds2 = _dot_nt(dyb_ref[0], wo_ref[...])
        s0 = s0_ref[0]
        s1 = _gelu(s0)
        sg = _sigmoid(z_ref[0])
        dz = ds2 * s1 * sg * (1.0 - sg)
        dzb = dz.astype(BF16)
        ds1 = ds2 * sg + _dot_nt(dzb, wg_ref[...])
        ds0 = ds1 * _gelu_grad(s0)
        ds0_ref[0] = ds0
        dz_ref[0] = dzb
        part = jnp.sum(ds0 * u_ref[0], axis=0, keepdims=True)

        @pl.when((i == 0) & (b == 0))
        def _():
            dd_ref[...] = part

        @pl.when((i > 0) | (b > 0))
        def _():
            dd_ref[...] += part

    del dskip
    return pl.pallas_call(
        body, name=name, grid=(bsz, seq // tm),
        in_specs=[_row(tm, D), _row(tm, SW), _row(tm, SW), _row(tm, SW), _full2((SW, SW)), _full2((SW, D))],
        out_specs=[_row(tm, SW), _row(tm, SW), _full2((1, SW))],
        out_shape=[SDS((bsz, seq, SW), F32), SDS((bsz, seq, SW), BF16), SDS((1, SW), F32)],
        compiler_params=_cp("arbitrary", "arbitrary"),
    )(dyb, s0, z, u, wglu, wso)


def _merge_out(ya, yb, p3, wout, x1, gt, name):
    bsz, seq, _ = ya.shape
    tm = _pick_tile(seq, (512, 256, 128))

    def body(ya_ref, yb_ref, ga_ref, gbb_ref, w_ref, x_ref, gt_ref, mg_ref, mix_ref, xo_ref):
        merged = (_sigmoid(ga_ref[0].astype(F32)) * ya_ref[0].astype(F32)
                  + _sigmoid(gbb_ref[0].astype(F32)) * yb_ref[0].astype(F32)).astype(BF16)
        mix = _dot(merged, w_ref[...])
        mg_ref[0] = merged
        mix_ref[0] = mix.astype(BF16)
        xo_ref[0] = x_ref[0] + gt_ref[0] * mix

    return pl.pallas_call(
        body, name=name, grid=(bsz, seq // tm),
        in_specs=[_row(tm, D), _row(tm, D), _row(tm, D, 0), _row(tm, D, 1), _full2((D, D)), _row(tm, D), _seqvec(D)],
        out_specs=[_row(tm, D), _row(tm, D), _row(tm, D)],
        out_shape=[SDS((bsz, seq, D), BF16), SDS((bsz, seq, D), BF16), SDS((bsz, seq, D), F32)],
        compiler_params=_cp("parallel", "parallel"),
    )(ya, yb, p3, p3, wout, x1, gt)


def _merge_bwd(dx2, gt, mix, ya, yb, p3, wout, name):
    bsz, seq, _ = ya.shape
    tm = _pick_tile(seq, (512, 256, 128))

    def body(dx_ref, gt_ref, mix_ref, ya_ref, yb_ref, ga_ref, gbb_ref, w_ref, dmix_ref, dya_ref, dyb_ref, dp_ref, dgt_ref):
        i = pl.program_id(1)
        dx = dx_ref[0]
        dmix = (gt_ref[0] * dx).astype(BF16)
        dmix_ref[0] = dmix
        part = jnp.sum(dx * mix_ref[0].astype(F32), axis=0, keepdims=True)

        @pl.when(i == 0)
        def _():
            dgt_ref[0] = part

        @pl.when(i > 0)
        def _():
            dgt_ref[0] += part

        dmg = _dot_nt(dmix, w_ref[...])
        sa = _sigmoid(ga_ref[0].astype(F32))
        sb = _sigmoid(gbb_ref[0].astype(F32))
        dya_ref[0] = (dmg * sa).astype(BF16)
        dyb_ref[0] = (dmg * sb).astype(BF16)
        dp_ref[0, :, 0:D] = (dmg * ya_ref[0].astype(F32) * sa * (1.0 - sa)).astype(BF16)
        dp_ref[0, :, D:2 * D] = (dmg * yb_ref[0].astype(F32) * sb * (1.0 - sb)).astype(BF16)

    bshape = SDS((bsz, seq, D), BF16)
    return pl.pallas_call(
        body, name=name, grid=(bsz, seq // tm),
        in_specs=[_row(tm, D), _seqvec(D), _row(tm, D), _row(tm, D), _row(tm, D), _row(tm, D, 0), _row(tm, D, 1),
                  _full2((D, D))],
        out_specs=[_row(tm, D), _row(tm, D), _row(tm, D), _row(tm, 2 * D), _seqvec(D)],
        out_shape=[bshape, bshape, bshape, SDS((bsz, seq, 2 * D), BF16), SDS((bsz, 1, D), F32)],
        compiler_params=_cp("arbitrary", "arbitrary"),
    )(dx2, gt, mix, ya, yb, p3, p3, wout)


def _final_loss(x3, gfin, target, name):
    bsz, seq, dm = x3.shape
    tm = _pick_tile(seq, (512, 256, 128))

    def body(x_ref, g_ref, t_ref, dx_ref, loss_ref, dg_ref):
        b = pl.program_id(0)
        i = pl.program_id(1)
        xf = x_ref[0]
        gv = g_ref[...]
        r = lax.rsqrt(jnp.mean(xf * xf, axis=-1, keepdims=True) + EPS)
        xhat = xf * r
        e = xhat * gv - t_ref[0]
        dy = e * (1.0 / dm)
        dxh = dy * gv
        dx_ref[0] = r * (dxh - xhat * jnp.mean(dxh * xhat, axis=-1, keepdims=True))
        p_l = jnp.sum(e * e, axis=0, keepdims=True) * (0.5 / dm)
        p_g = jnp.sum(dy * xhat, axis=0, keepdims=True)

        @pl.when((i == 0) & (b == 0))
        def _():
            loss_ref[...] = p_l
            dg_ref[...] = p_g

        @pl.when((i > 0) | (b > 0))
        def _():
            loss_ref[...] += p_l
            dg_ref[...] += p_g

    return pl.pallas_call(
        body, name=name, grid=(bsz, seq // tm),
        in_specs=[_row(tm, dm), _full2((1, dm)), _row(tm, dm)],
        out_specs=[_row(tm, dm), _full2((1, dm)), _full2((1, dm))],
        out_shape=[SDS((bsz, seq, dm), F32), SDS((1, dm), F32), SDS((1, dm), F32)],
        compiler_params=_cp("arbitrary", "arbitrary"),
    )(x3, gfin, target)


def _ada_fwd(c_all, w_shard, b_shard):
    nb = c_all.shape[0]
    n = w_shard.shape[2]

    def body(c_ref, w_ref, b_ref, o_ref):
        cv = c_ref[...]
        cond = (cv * _sigmoid(cv)).astype(BF16)
        o_ref[...] = _dot(cond, w_ref[0].astype(BF16)) + b_ref[...]

    return pl.pallas_call(body, name="ada_fwd", out_shape=SDS((nb, n), F32), compiler_params=_cp())(
        c_all, w_shard, b_shard)


def _ada_bwd(c_all, dmod_shard, dmod_all):
    n = dmod_shard.shape[1]

    def body(c_ref, ds_ref, da_ref, gw_ref, gb_ref):
        cv = c_ref[...]
        cond = (cv * _sigmoid(cv)).astype(BF16)
        gw_ref[...] = _dot_tn(cond, ds_ref[...].astype(BF16))
        gb_ref[...] = jnp.sum(da_ref[...], axis=0, keepdims=True)

    return pl.pallas_call(
        body, name="ada_bwd", out_shape=[SDS((D, n), F32), SDS((1, dmod_all.shape[1]), F32)], compiler_params=_cp(),
    )(c_all, dmod_shard, dmod_all)


def _adamw_math(w, g, m, v):
    m = B1 * m + (1.0 - B1) * g
    v = B2 * v + (1.0 - B2) * (g * g)
    delta = -LR * ((m / BC1) / (jnp.sqrt(v / BC2) + AEPS) + WD * w)
    return delta, m, v


def _adamw_big(w, m, v, recv_own, recv_sib, name):
    _, rows, cols = w.shape
    tr = _pick_tile(rows, tuple(t for t in (512, 256, 128, 64, 32, 16, 8) if t * cols <= 192 * 1024))

    def body(w_ref, m_ref, v_ref, a_ref, b_ref, g_ref, d_ref, mo_ref, vo_ref):
        def chip_sum(r):
            acc = r[0].astype(F32)
            for k in range(1, N_CHIPS):
                acc = acc + r[k].astype(F32)
            return acc

        g = chip_sum(a_ref) + chip_sum(b_ref)
        delta, mn, vn = _adamw_math(w_ref[0], g, m_ref[0], v_ref[0])
        g_ref[0] = g
        d_ref[0] = delta
        mo_ref[0] = mn
        vo_ref[0] = vn

    own = pl.BlockSpec((1, tr, cols), lambda i: (0, i, 0))
    rspec = pl.BlockSpec((N_CHIPS, tr, cols), lambda i: (0, i, 0))
    shp = SDS(w.shape, F32)
    return pl.pallas_call(
        body, name=name, grid=(rows // tr,),
        in_specs=[own, own, own, rspec, rspec], out_specs=[own, own, own, own], out_shape=[shp, shp, shp, shp],
        compiler_params=_cp("parallel"),
    )(w, m, v, recv_own, recv_sib)


def _adamw_plain(w, m, v, g, name):
    def body(w_ref, m_ref, v_ref, g_ref, d_ref, mo_ref, vo_ref):
        delta, mn, vn = _adamw_math(w_ref[...], g_ref[...], m_ref[...], v_ref[...])
        d_ref[...] = delta
        mo_ref[...] = mn
        vo_ref[...] = vn

    shp = SDS(w.shape, F32)
    return pl.pallas_call(body, name=name, out_shape=[shp, shp, shp], compiler_params=_cp())(w, m, v, g)


def _adamw_rows(w, m, v, g, name):
    _, rows, cols = w.shape
    tr = _pick_tile(rows, (128, 64, 32, 16, 8))

    def body(w_ref, m_ref, v_ref, g_ref, d_ref, mo_ref, vo_ref):
        delta, mn, vn = _adamw_math(w_ref[0], g_ref[...], m_ref[0], v_ref[0])
        d_ref[0] = delta
        mo_ref[0] = mn
        vo_ref[0] = vn

    spec = pl.BlockSpec((1, tr, cols), lambda i: (0, i, 0))
    shp = SDS(w.shape, F32)
    return pl.pallas_call(
        body, name=name, grid=(rows // tr,), in_specs=[spec] * 3 + [pl.BlockSpec((tr, cols), lambda i: (i, 0))],
        out_specs=[spec] * 3, out_shape=[shp] * 3, compiler_params=_cp("parallel"),
    )(w, m, v, g)


def _sum_slabs(r, name):
    n, rows, cols = r.shape
    tr = _pick_tile(rows, (256, 128, 64))

    def body(r_ref, o_ref):
        acc = r_ref[0].astype(F32)
        for j in range(1, n):
            acc = acc + r_ref[j].astype(F32)
        o_ref[...] = acc

    return pl.pallas_call(
        body, name=name, grid=(rows // tr,),
        in_specs=[pl.BlockSpec((n, tr, cols), lambda i: (0, i, 0))],
        out_specs=pl.BlockSpec((tr, cols), lambda i: (i, 0)), out_shape=SDS((rows, cols), F32),
        compiler_params=_cp("parallel"),
    )(r)


def _place():
    return lax.axis_index("x"), lax.axis_index("y"), lax.axis_index("c")


def _all_gather8(blk, name):
    m_per, n = blk.shape

    def body(x_ref, out_ref, send_sems, recv_sems, local_sem):
        x, y, c = _place()
        me, sibling = (x, y, c), (x, y, 1 - c)
        chips = [(1 - x, y), (x, 1 - y), (1 - x, 1 - y)]

        def rows(px, py, pc):
            return out_ref.at[pl.ds((4 * px + 2 * py + pc) * m_per, m_per), :]

        def copy(k, block, to, src=None):
            return pltpu.make_async_remote_copy(
                src_ref=rows(*block) if src is None else src, dst_ref=rows(*block),
                send_sem=send_sems.at[k], recv_sem=recv_sems.at[k], device_id=to, device_id_type=MESH)

        mine = pltpu.make_async_copy(x_ref, rows(*me), local_sem)
        mine.start()
        first = [copy(0, me, sibling, src=x_ref)]
        first += [copy(1 + j, me, (*chip, c), src=x_ref) for j, chip in enumerate(chips)]
        for cp in first:
            cp.start()
        passed = [copy(4 + j, (*chip, c), sibling) for j, chip in enumerate(chips)]
        for j, chip in enumerate(chips):
            copy(1 + j, (*chip, c), me).wait_recv()
            passed[j].start()
        copy(0, sibling, me).wait_recv()
        for j, chip in enumerate(chips):
            copy(4 + j, (*chip, 1 - c), me).wait_recv()
        for cp in first + passed:
            cp.wait_send()
        mine.wait()

    return pl.pallas_call(
        body, name=name, out_shape=SDS((N_DEV * m_per, n), blk.dtype),
        in_specs=[pl.BlockSpec(memory_space=pltpu.VMEM)], out_specs=pl.BlockSpec(memory_space=pltpu.VMEM),
        scratch_shapes=[pltpu.SemaphoreType.DMA((7,)), pltpu.SemaphoreType.DMA((7,)), pltpu.SemaphoreType.DMA],
        compiler_params=pltpu.CompilerParams(vmem_limit_bytes=VMEM_LIMIT),
    )(blk)


def _chip_peers(x, y):
    return [(1 - x, y), (x, 1 - y), (1 - x, 1 - y)]


SIBLING = "sibling"


def _peer_copies(src_refs, land_refs, send_sems, recv_sems, scatter, landed):
    x, y, c = _place()
    if scatter == SIBLING:
        return [pltpu.make_async_remote_copy(
            src_ref=s, dst_ref=l, send_sem=send_sems.at[a], recv_sem=recv_sems.at[a],
            device_id=(x, y, 1 - c), device_id_type=MESH) for a, (s, l) in enumerate(zip(src_refs, land_refs))]
    cps = []
    for a, (src_ref, land_ref) in enumerate(zip(src_refs, land_refs)):
        for j, (px, py) in enumerate(_chip_peers(x, y)):
            if scatter:
                src = src_ref.at[2 * px + py]
            else:
                src = land_ref.at[2 * x + y] if src_ref is None else src_ref
            dst = land_ref.at[2 * px + py] if landed else land_ref.at[2 * x + y]
            cps.append(pltpu.make_async_remote_copy(
                src_ref=src, dst_ref=dst, send_sem=send_sems.at[3 * a + j], recv_sem=recv_sems.at[3 * a + j],
                device_id=(px, py, c), device_id_type=MESH))
    return cps


def _exchange_chips(srcs, scatter, name):
    n = len(srcs)

    def body(*refs):
        src_refs, land_refs = refs[:n], refs[n:2 * n]
        send_sems, recv_sems, local_sems = refs[2 * n:]
        x, y, _ = _place()
        me = 2 * x + y
        mine = [pltpu.make_async_copy(s.at[me] if scatter else s, l.at[me], local_sems.at[a])
                for a, (s, l) in enumerate(zip(src_refs, land_refs))]
        for cp in mine:
            cp.start()
        out = _peer_copies(src_refs, land_refs, send_sems, recv_sems, scatter, False)
        for cp in out:
            cp.start()
        for cp in _peer_copies(src_refs, land_refs, send_sems, recv_sems, scatter, True):
            cp.wait_recv()
        for cp in out:
            cp.wait_send()
        for cp in mine:
            cp.wait()

    anyspec = pl.BlockSpec(memory_space=pl.ANY)
    shapes = [SDS(s.shape if scatter else (N_CHIPS,) + s.shape, s.dtype) for s in srcs]
    return pl.pallas_call(
        body, name=name, out_shape=shapes, in_specs=[anyspec] * n, out_specs=[anyspec] * n,
        scratch_shapes=[pltpu.SemaphoreType.DMA((3 * n,)), pltpu.SemaphoreType.DMA((3 * n,)),
                        pltpu.SemaphoreType.DMA((n,))],
        compiler_params=pltpu.CompilerParams(vmem_limit_bytes=VMEM_LIMIT),
    )(*srcs)


_HBM = pl.BlockSpec(memory_space=pltpu.HBM)
_SEM = pl.BlockSpec(memory_space=pltpu.SEMAPHORE)
_EFFECT = pltpu.SideEffectType.DATAFLOW_SIDE_EFFECTING


def _exchange_begin(srcs, lands, scatter, name):
    srcs = tuple(srcs or ())
    n, ns = len(lands), len(srcs)
    nsem = n if scatter == SIBLING else 3 * n

    def body(*refs):
        src_refs = refs[:ns] if ns else (None,) * n
        land_refs = refs[ns:ns + n]
        send_sems, recv_sems = refs[ns + n:ns + n + 2]
        token = refs[-1]
        for cp in _peer_copies(src_refs, land_refs, send_sems, recv_sems, scatter, False):
            cp.start()
        token[...] = jnp.zeros_like(token)

    ops = (*srcs, *lands)
    res = pl.pallas_call(
        body, name=name,
        out_shape=(pltpu.SemaphoreType.DMA((nsem,)), pltpu.SemaphoreType.DMA((nsem,)),
                   *[pltpu.HBM(a.shape, a.dtype) for a in ops], SDS((SUB, LANE), F32)),
        in_specs=[_HBM] * len(ops), out_specs=(_SEM, _SEM, *[_HBM] * len(ops), pl.BlockSpec(memory_space=pltpu.VMEM)),
        input_output_aliases={i: 2 + i for i in range(len(ops))},
        compiler_params=pltpu.CompilerParams(has_side_effects=_EFFECT),
    )(*[pltpu.with_memory_space_constraint(a, pltpu.HBM) for a in ops])
    return res[0], res[1], res[2:2 + ns], res[2 + ns:2 + ns + n], res[-1]


def _exchange_end(handle, after, scatter, name, with_srcs=False):
    send_sems, recv_sems, srcs, lands, _ = handle
    n, ns = len(lands), len(srcs)

    def body(*refs):
        src_refs = refs[:ns] if ns else (None,) * n
        land_refs = refs[ns:ns + n]
        send_sems, recv_sems = refs[ns + n:ns + n + 2]
        for cp in _peer_copies(src_refs, land_refs, send_sems, recv_sems, scatter, True):
            cp.wait_send()
            cp.wait_recv()

    ops = (*srcs, *lands)
    res = pl.pallas_call(
        body, name=name,
        out_shape=tuple(pltpu.HBM(a.shape, a.dtype) for a in ops),
        in_specs=[_HBM] * len(ops) + [_SEM, _SEM, pl.BlockSpec(memory_space=pl.ANY)], out_specs=tuple([_HBM] * len(ops)),
        input_output_aliases={i: i for i in range(len(ops))},
        compiler_params=pltpu.CompilerParams(has_side_effects=_EFFECT),
    )(*ops, send_sems, recv_sems, after)
    return (list(res[:ns]), list(res[ns:])) if with_srcs else list(res[ns:])


def _own_slab(stack4, chip):
    idx = lax.broadcasted_iota(jnp.int32, (N_CHIPS,) + (1,) * (stack4.ndim - 1), 0)
    return jnp.where(idx == chip, stack4, jnp.zeros((), stack4.dtype))


def _swap_sibling(vs, name):
    n = len(vs)

    def body(*refs):
        in_refs, out_refs = refs[:n], refs[n:2 * n]
        send_sems, recv_sems = refs[2 * n:]
        x, y, c = _place()
        cps = [pltpu.make_async_remote_copy(
            src_ref=i, dst_ref=o, send_sem=send_sems.at[a], recv_sem=recv_sems.at[a],
            device_id=(x, y, 1 - c), device_id_type=MESH) for a, (i, o) in enumerate(zip(in_refs, out_refs))]
        for cp in cps:
            cp.start()
        for cp in cps:
            cp.wait()

    anyspec = pl.BlockSpec(memory_space=pl.ANY)
    return pl.pallas_call(
        body, name=name, out_shape=[SDS(v.shape, v.dtype) for v in vs], in_specs=[anyspec] * n, out_specs=[anyspec] * n,
        scratch_shapes=[pltpu.SemaphoreType.DMA((n,)), pltpu.SemaphoreType.DMA((n,))],
        compiler_params=pltpu.CompilerParams(vmem_limit_bytes=VMEM_LIMIT),
    )(*vs)


def _select(stacked, idx):
    out = stacked[0]
    for j in range(1, stacked.shape[0]):
        out = jnp.where(idx == j, stacked[j], out)
    return out


BIG = (
    ("w1_a", DFF // 4, D, False), ("w3_a", DFF // 4, D, False), ("w2_a", DFF // 4, D, False),
    ("w_in", D, 5632 // 4, True), ("w_conv_out", CW // 4, D, False), ("w_glu", SW // 4, SW, False),
    ("w_ssm_out", SW, D // 4, True), ("w_out", D // 4, D, False),
    ("w1_b", DFF // 4, D, False), ("w3_b", DFF // 4, D, False), ("w2_b", DFF // 4, D, False),
)
TRANSPOSED = frozenset(("w1_a", "w3_a", "w1_b", "w3_b"))
PACK_COLS = 1024


def _view(a, name):
    return jnp.transpose(a, (0, 2, 1)) if name in TRANSPOSED else a


def _full_from_stacked(st, split_cols):
    _, rows, cols = st.shape
    if split_cols:
        return st.transpose(1, 0, 2).reshape(rows, N_CHIPS * cols)
    return st.reshape(N_CHIPS * rows, cols)


def _stacked_from_full(full, rows, cols, split_cols):
    if split_cols:
        return full.reshape(rows, N_CHIPS, cols).transpose(1, 0, 2)
    return full.reshape(N_CHIPS, rows, cols)


def _blockdiag(t):
    r = lax.broadcasted_iota(jnp.int32, (SW, GP), 0) // NH
    cidx = lax.broadcasted_iota(jnp.int32, (SW, GP), 1) // NP
    dense = jnp.where(r == cidx, jnp.tile(t, (NG, 1)), 0.0)
    ub, sb = SW // SSM_SUPER, GP // SSM_SUPER
    return jnp.concatenate([dense[s * ub:(s + 1) * ub, s * sb:(s + 1) * sb] for s in range(SSM_SUPER)], axis=0)


def _blockdiag_extract(acc):
    gs = NG // SSM_SUPER
    a = acc.reshape(NG, NH, gs, NP)
    sel = (lax.broadcasted_iota(jnp.int32, (NG, 1, gs, 1), 0) % gs) == lax.broadcasted_iota(jnp.int32, (NG, 1, gs, 1), 2)
    a = jnp.sum(jnp.where(sel, a, 0.0), axis=2)
    return a.transpose(1, 0, 2).reshape(NH, GP)


def _to_t(p):
    return p.transpose(2, 0, 1).reshape(NH, GP)


def _from_t(t):
    return t.reshape(NH, NG, NP).transpose(1, 2, 0)


def _c_to_t(p):
    return p.transpose(1, 0, 2).reshape(NH, GP)


def _c_from_t(t):
    return t.reshape(NH, NG, NP).transpose(1, 0, 2)


def _ffn_forward(x, g, sh, sc, gt, w1, w3, w2, tag):
    h = _norm_mod(x, g, sh, sc, f"{tag}_norm")
    if callable(w1):
        w1, w3 = w1(h)
    a, b, hid = _swiglu_up(h, w1, w3, f"{tag}_up")
    w2 = w2(hid) if callable(w2) else w2
    f, xo = _ffn_down(hid, w2, x, gt, f"{tag}_down")
    return xo, (x, h, a, b, hid, f), w2


def _ffn_backward(dxo, saved, g, sc, gt, w1, w3, w2, tag, emit=lambda key, gw: 0.0):
    x, h, a, b, hid, f = saved
    dfs, da, db, dgt = _ffn_bwd_hid(dxo, gt, f, a, b, w2, f"{tag}_bwd_hid")
    h2 = _flat(h)
    gw2 = _mm(_flat(hid), _flat(dfs), ta=True, out_dtype=BF16, name=f"{tag}_gw2")
    tok = emit("w2", gw2)
    gw1 = _mm(_flat(da), h2, ta=True, out_dtype=BF16, name=f"{tag}_gw1")
    tok = tok + emit("w1", gw1)
    gw3 = _mm(_flat(db), h2, ta=True, out_dtype=BF16, name=f"{tag}_gw3")
    tok = tok + emit("w3", gw3)
    dx, dsh, dsc, dg = _dh_norm_bwd([da, db], [w1, w3], x, g, sc + tok, dxo, f"{tag}_bwd_dh", transposed=True)
    return dx, (dsh, dsc, dgt, dg), (gw1, gw3, gw2)


def kernel(x, c, w_ada, b_ada, g_ffn1, w1_a, w3_a, w2_a, g_mix, w_in, conv_w, w_conv_out, a_re, a_im, b_re, b_im, c_re, c_im, log_dt, d_skip, w_glu, w_ssm_out, w_out, g_ffn2, w1_b, w3_b, w2_b, g_final, loss_target, m_w_ada, m_b_ada, m_g_ffn1, m_w1_a, m_w3_a, m_w2_a, m_g_mix, m_w_in, m_conv_w, m_w_conv_out, m_a_re, m_a_im, m_b_re, m_b_im, m_c_re, m_c_im, m_log_dt, m_d_skip, m_w_glu, m_w_ssm_out, m_w_out, m_g_ffn2, m_w1_b, m_w3_b, m_w2_b, m_g_final, v_w_ada, v_b_ada, v_g_ffn1, v_w1_a, v_w3_a, v_w2_a, v_g_mix, v_w_in, v_conv_w, v_w_conv_out, v_a_re, v_a_im, v_b_re, v_b_im, v_c_re, v_c_im, v_log_dt, v_d_skip, v_w_glu, v_w_ssm_out, v_w_out, v_g_ffn2, v_w1_b, v_w3_b, v_w2_b, v_g_final):
    args = dict(locals())
    names = ["w_ada", "b_ada", "g_ffn1", "w1_a", "w3_a", "w2_a", "g_mix", "w_in", "conv_w", "w_conv_out", "a_re",
             "a_im", "b_re", "b_im", "c_re", "c_im", "log_dt", "d_skip", "w_glu", "w_ssm_out", "w_out", "g_ffn2",
             "w1_b", "w3_b", "w2_b", "g_final"]
    bsz, seq, _ = x.shape
    mx, my, mc = _place()
    chip = 2 * mx + my
    dev = 4 * mx + 2 * my + mc

    groups = (BIG[:3], BIG[3:8], BIG[8:])
    wfull = {}

    def shards(grp):
        return [_view(args[n], n)[0] for n, _, _, _ in grp]

    def unpack_group(gathered, grp):
        for (n, _, _, split), st in zip(grp, gathered):
            wfull[n] = _full_from_stacked(st, split)

    up_grp, down_grp = groups[0][:2], groups[0][2:]

    nmod_shard = NMOD * D // N_CHIPS
    c_all = _all_gather8(c.reshape(SUB, -1), "gather_c").reshape(N_DEV * bsz, D)
    b_shard = _select(b_ada.reshape(N_CHIPS, 1, nmod_shard), chip)
    mod_shard = _ada_fwd(c_all, w_ada, b_shard)
    nb = N_DEV * bsz
    cw_pad = jnp.pad(conv_w[0], ((0, SUB - 3), (0, nmod_shard - CW // N_CHIPS)))
    mod_st = _exchange_chips([jnp.concatenate([mod_shard, cw_pad], axis=0)], False, "gather_mod")[0]
    mod_all = mod_st[:, :nb].transpose(1, 0, 2).reshape(N_DEV, bsz, NMOD * D)
    mod = _select(mod_all, dev)

    def gather_begin(raw, name):
        lands = [_own_slab(jnp.broadcast_to(a.astype(BF16)[None], (N_CHIPS,) + a.shape), chip) for a in raw]
        return _exchange_begin(None, lands, False, name)

    up_raw, mod = lax.optimization_barrier((shards(up_grp), mod))
    up_handle = gather_begin(up_raw, "gather_w_ffn1_up_start")
    (down_raw, mix_raw, ffn2_raw), up_token = lax.optimization_barrier(
        ((shards(down_grp), shards(groups[1]), shards(groups[2])), up_handle[4][0:1, 0:1]))
    down_handle = gather_begin(down_raw, "gather_w_ffn1_down_start")
    mix_handle = gather_begin(mix_raw, "gather_w_mix_start")
    ffn2_handle = gather_begin(ffn2_raw, "gather_w_ffn2_start")
    start_tokens = up_token + down_handle[4][0:1, 0:1] + mix_handle[4][0:1, 0:1] + ffn2_handle[4][0:1, 0:1]

    sh1, sc1, gt1, sh2, sc2, gt2, sh3, sc3, gt3 = [mod[:, None, j * D:(j + 1) * D] for j in range(NMOD)]
    convw = mod_st[:, nb:nb + 3, :CW // N_CHIPS].transpose(1, 0, 2).reshape(3, CW)
    convw8 = jnp.pad(convw, ((0, SUB - 3), (0, 0)))

    are, aim = a_re.reshape(1, GP), a_im.reshape(1, GP)
    ldt = jnp.broadcast_to(log_dt.reshape(NG, 1), (NG, NP)).reshape(1, GP)
    bre_t, bim_t = _to_t(b_re[0]), _to_t(b_im[0])
    abr, abi, bbr_t, bbi_t = _ssm_disc(are, aim, ldt, bre_t, bim_t)
    wb = jnp.concatenate([_blockdiag(bbr_t), _blockdiag(bbi_t)], axis=1).astype(BF16)
    wct = jnp.concatenate([_blockdiag(_c_to_t(c_re[0])), _blockdiag(-_c_to_t(c_im[0]))], axis=1).astype(BF16)
    ar8 = jnp.broadcast_to(abr, (SUB, GP))
    ai8 = jnp.broadcast_to(abi, (SUB, GP))

    def late_w2a(hid):
        unpack_group(_exchange_end(down_handle, hid, False, "gather_w_ffn1_down_wait"), down_grp)
        return wfull["w2_a"]

    def late_w13a(h):
        unpack_group(_exchange_end(up_handle, h, False, "gather_w_ffn1_up_wait"), up_grp)
        return wfull["w1_a"], wfull["w3_a"]

    x1, ffn1_saved, _ = _ffn_forward(x, g_ffn1 + start_tokens, sh1, sc1, gt1, late_w13a, None, late_w2a, "ffn1")
    unpack_group(_exchange_end(mix_handle, x1, False, "gather_w_mix_wait"), groups[1])

    h2 = _norm_mod(x1, g_mix, sh2, sc2, "mix_norm")
    h2f = _flat(h2)
    win = wfull["w_in"]
    win1, winu, win3 = win[:, :3 * CW], win[:, 3 * CW:3 * CW + SW], win[:, 3 * CW + SW:]
    p1 = _mm(h2f, win1, out_dtype=BF16, name="mix_in1").reshape(bsz, seq, 3 * CW)
    u = _mm(h2f, winu, name="mix_inu").reshape(bsz, seq, SW)
    p3 = _mm(h2f, win3, out_dtype=BF16, name="mix_in3").reshape(bsz, seq, 2 * D)

    ya_in = _conv_fwd(p1, convw8, "conv_fwd")
    ya = _mm(_flat(ya_in), wfull["w_conv_out"], out_dtype=BF16, name="conv_out").reshape(bsz, seq, D)

    xs, ys = _ssm_fwd(u, wb, wct, ar8, ai8, "ssm_fwd")
    s0, z, s1, s2, yb = _ssm_post(ys, u, d_skip, wfull["w_glu"], wfull["w_ssm_out"], "ssm_post")

    merged, mix, x2 = _merge_out(ya, yb, p3, wfull["w_out"], x1, gt2, "merge_out")
    unpack_group(_exchange_end(ffn2_handle, x2, False, "gather_w_ffn2_wait"), groups[2])
    x3, ffn2_saved, _ = _ffn_forward(x2, g_ffn2, sh3, sc3, gt3, wfull["w1_b"], wfull["w3_b"], wfull["w2_b"], "ffn2")

    dx3, lossvec, dgfin = _final_loss(x3, g_final.reshape(1, D), loss_target, "final_loss")
    loss = lax.psum(jnp.sum(lossvec), ("x", "y", "c"))

    gfull = {}
    dx2, (dsh3, dsc3, dgt3, dg3), (gfull["w1_b"], gfull["w3_b"], gfull["w2_b"]) = _ffn_backward(
        dx3, ffn2_saved, g_ffn2, sc3, gt3, wfull["w1_b"], wfull["w3_b"], wfull["w2_b"], "ffn2")

    def stack_group(grp):
        return [_stacked_from_full(gfull[n], rows, cols, split) for n, rows, cols, split in grp]

    st_ffn2 = stack_group(groups[2])
    h_ffn2 = _exchange_begin(st_ffn2, [_own_slab(s, chip) for s in st_ffn2], True, "scatter_ffn2_start")

    dmix, dya, dyb, dp3, dgt2 = _merge_bwd(dx2, gt2 + h_ffn2[4][0, 0], mix, ya, yb, p3, wfull["w_out"], "merge_bwd")
    gfull["w_out"] = _mm(_flat(merged), _flat(dmix), ta=True, out_dtype=BF16, name="gw_out")
    dp1, dconvw8 = _conv_bwd(dya, wfull["w_conv_out"], p1, convw8, "conv_bwd")
    gfull["w_conv_out"] = _mm(_flat(ya_in), _flat(dya), ta=True, out_dtype=BF16, name="gw_conv_out")
    ds0, dz, ddskip = _ssm_post_bwd(dyb, s0, z, u, d_skip, wfull["w_glu"], wfull["w_ssm_out"], "ssm_post_bwd")
    gfull["w_ssm_out"] = _mm(_flat(s2), _flat(dyb), ta=True, out_dtype=BF16, name="gw_ssm_out")
    gfull["w_glu"] = _mm(_flat(s1), _flat(dz), ta=True, out_dtype=BF16, name="gw_glu")
    du, dwb, dwct, dar8, dai8 = _ssm_bwd(ds0, u, xs, wb, wct, ar8, ai8, d_skip, "ssm_bwd")
    dx1, dsh2, dsc2, dgmix = _dh_norm_bwd([dp1, du, dp3], [win1, winu, win3], x1, g_mix, sc2, dx2, "mix_bwd_dh")
    gfull["w_in"] = jnp.concatenate([
        _mm(h2f, _flat(dp1), ta=True, out_dtype=BF16, name="gw_in1"), _mm(h2f, _flat(du), ta=True, out_dtype=BF16, name="gw_inu"),
        _mm(h2f, _flat(dp3), ta=True, out_dtype=BF16, name="gw_in3")], axis=1)

    st_mix = stack_group(groups[1])
    h_mix = _exchange_begin(st_mix, [_own_slab(s, chip) for s in st_mix], True, "scatter_mix_start")

    def swap_begin(recv, name):
        return _exchange_begin(recv, [lax.empty(v.shape, v.dtype) for v in recv], SIBLING, name)

    recv_ffn2 = _exchange_end(h_ffn2, dx1, True, "scatter_ffn2_wait")
    sw_ffn2 = swap_begin(recv_ffn2, "swap_ffn2_start")

    ffn1_names = {"w1": BIG[0], "w3": BIG[1], "w2": BIG[2]}
    ffn1_handles = {}

    def emit_ffn1(key, gw):
        n, rows, cols, split = ffn1_names[key]
        st = _stacked_from_full(gw, rows, cols, split)
        ffn1_handles[n] = _exchange_begin([st], [_own_slab(st, chip)], True, f"scatter_ffn1_{key}_start")
        return ffn1_handles[n][4][0, 0]

    grad_x, (dsh1, dsc1, dgt1, dg1), _ = _ffn_backward(
        dx1, ffn1_saved, g_ffn1, sc1, gt1 + (h_mix[4][0, 0] + sw_ffn2[4][0, 0]), wfull["w1_a"], wfull["w3_a"],
        wfull["w2_a"], "ffn1", emit=emit_ffn1)
    recv_mix = _exchange_end(h_mix, grad_x, True, "scatter_mix_wait")
    sw_mix = swap_begin(recv_mix, "swap_mix_start")
    dg1 = dg1 + sw_mix[4][0, 0]

    sbw = GP // SSM_SUPER
    d_are, d_aim, d_ldt, d_bre_t, d_bim_t = _ssm_disc_bwd(
        are, aim, ldt, bre_t, bim_t, jnp.sum(dar8, axis=0, keepdims=True), jnp.sum(dai8, axis=0, keepdims=True),
        _blockdiag_extract(dwb[:, :sbw]), _blockdiag_extract(dwb[:, sbw:]))
    d_cre = _c_from_t(_blockdiag_extract(dwct[:, :sbw]))
    d_cim = -_c_from_t(_blockdiag_extract(dwct[:, sbw:]))

    small_parts = [dg1, dgmix, dg3, dgfin, dconvw8[:3], d_are, d_aim, _from_t(d_bre_t), _from_t(d_bim_t), d_cre, d_cim,
                   jnp.sum(d_ldt.reshape(NG, NP), axis=1), ddskip]
    small_sizes = [int(p.size) for p in small_parts]
    n_small = sum(small_sizes)
    n_small_pad = -(-n_small // (SUB * PACK_COLS)) * (SUB * PACK_COLS)
    dmod = jnp.concatenate([dsh1, dsc1, dgt1, dsh2, dsc2, dgt2, dsh3, dsc3, dgt3], axis=2).reshape(bsz * NMOD * D)
    flat = jnp.concatenate([p.reshape(-1) for p in small_parts] + [jnp.zeros((n_small_pad - n_small,), F32), dmod])
    grads, deltas, new_m, new_v = {}, {}, {}, {}

    def adamw_group(grp, recv_own, recv_sib, token):
        for (n, _, _, _), r_own, r_sib in zip(grp, recv_own, recv_sib):
            wmv, _ = lax.optimization_barrier(
                ((_view(args[n], n), _view(args["m_" + n], n), _view(args["v_" + n], n)), token))
            res = _adamw_big(*wmv, r_own, r_sib, f"adamw_{n}")
            grads[n], deltas[n], new_m[n], new_v[n] = [_view(r, n) for r in res]
        return deltas[grp[-1][0]]

    flat2d = flat.reshape(-1, PACK_COLS)
    h_small = _exchange_begin(None, [_own_slab(jnp.broadcast_to(flat2d[None], (N_CHIPS,) + flat2d.shape), chip)],
                              False, "gather_small_chips_start")
    recv_ffn2, sib_ffn2 = _exchange_end(sw_ffn2, flat2d, SIBLING, "swap_ffn2_wait", with_srcs=True)
    done = adamw_group(groups[2], recv_ffn2, sib_ffn2, h_small[4])
    recv_mix, sib_mix = _exchange_end(sw_mix, done, SIBLING, "swap_mix_wait", with_srcs=True)
    done = adamw_group(groups[1], recv_mix, sib_mix, h_small[4])
    own_c = _exchange_end(h_small, done, False, "gather_small_chips_wait")[0]
    sib_c = _swap_sibling([own_c], "gather_small_sib")[0]

    recv_ffn1 = [_exchange_end(ffn1_handles[n], sib_c, True, f"scatter_ffn1_{n}_wait")[0] for n, _, _, _ in groups[0]]
    sw_ffn1 = swap_begin(recv_ffn1, "swap_ffn1_start")
    (own_c, sib_c), _ = lax.optimization_barrier(((own_c, sib_c), sw_ffn1[4]))
    core0 = jnp.where(mc == 0, own_c, sib_c)
    core1 = jnp.where(mc == 0, sib_c, own_c)
    allg = jnp.stack([core0, core1], axis=1).reshape(N_DEV, -1)
    small = _sum_slabs(allg[:, :n_small_pad].reshape(N_DEV, -1, PACK_COLS), "sum_small").reshape(-1)
    sg, o = [], 0
    for p, sz in zip(small_parts, small_sizes):
        sg.append(small[o:o + sz].reshape(p.shape))
        o += sz
    (g_g1, g_gmix, g_g3, g_gfin, g_convw, g_are, g_aim, g_bre, g_bim, g_cre, g_cim, g_ldt, g_dskip) = sg

    dmod_all = allg[:, n_small_pad:].reshape(nb, NMOD * D)
    dmod_shard = _select(dmod_all.reshape(nb, N_CHIPS, nmod_shard).transpose(1, 0, 2), chip)
    g_wada, g_bada = _ada_bwd(c_all, dmod_shard, dmod_all)

    grads["w_ada"] = g_wada[None]
    deltas["w_ada"], new_m["w_ada"], new_v["w_ada"] = _adamw_rows(w_ada, m_w_ada, v_w_ada, g_wada, "adamw_w_ada")

    g_convw_shard = _select(g_convw.reshape(3, N_CHIPS, CW // N_CHIPS).transpose(1, 0, 2), chip)
    small_g = {"b_ada": g_bada, "g_ffn1": g_g1, "g_mix": g_gmix, "g_ffn2": g_g3, "g_final": g_gfin,
               "conv_w": g_convw_shard, "a_re": g_are, "a_im": g_aim, "b_re": g_bre, "b_im": g_bim,
               "c_re": g_cre, "c_im": g_cim, "log_dt": g_ldt, "d_skip": g_dskip}
    small_names = list(small_g)
    sizes = [int(args[n].size) for n in small_names]
    tot = sum(sizes)
    tot_pad = -(-tot // (SUB * PACK_COLS)) * (SUB * PACK_COLS)

    def pack(get):
        return jnp.concatenate([get(n).reshape(-1) for n in small_names] + [jnp.zeros((tot_pad - tot,), F32)]).reshape(
            -1, PACK_COLS)

    res = _adamw_plain(pack(lambda n: args[n]), pack(lambda n: args["m_" + n]), pack(lambda n: args["v_" + n]),
                       pack(lambda n: small_g[n]), "adamw_small")
    o = 0
    for n, sz in zip(small_names, sizes):
        shp = args[n].shape
        grads[n] = small_g[n].reshape(shp)
        deltas[n], new_m[n], new_v[n] = [r.reshape(-1)[o:o + sz].reshape(shp) for r in res]
        o += sz

    recv_ffn1, sib_ffn1 = _exchange_end(sw_ffn1, res[0], SIBLING, "swap_ffn1_wait", with_srcs=True)
    adamw_group(groups[0], recv_ffn1, sib_ffn1, jnp.zeros((), F32))

    return (loss, grad_x, *[grads[n] for n in names], *[deltas[n] for n in names],
            *[new_m[n] for n in names], *[new_v[n] for n in names])
```

```python
import math

import jax
import jax.numpy as jnp
from jax import lax
from jax.experimental import pallas as pl
from jax.experimental.pallas import tpu as pltpu

F32 = jnp.float32
BF16 = jnp.bfloat16
SDS = jax.ShapeDtypeStruct
MESH = pl.DeviceIdType.MESH

D = 1024
DFF = 2816
CW = 1024
SW = 512
NG, NP, NH = 32, 64, 16
GP = NG * NP
NMOD = 9
EPS = 1e-6
N_CHIPS = 4
N_DEV = 8
SUB = 8
LANE = 128
SSM_SUPER = 4
VMEM_LIMIT = 50 * 1024 * 1024

LR, B1, B2, AEPS, WD, STEP = 0.001, 0.9, 0.999, 1e-08, 0.01, 10
BC1 = 1.0 - B1 ** STEP
BC2 = 1.0 - B2 ** STEP


def _cp(*sem):
    return pltpu.CompilerParams(dimension_semantics=sem or None, vmem_limit_bytes=VMEM_LIMIT)


def _pick_tile(n, cands):
    for t in cands:
        if t <= n and n % t == 0:
            return t
    return n


def _dot(a, b):
    return lax.dot_general(a, b, (((1,), (0,)), ((), ())), preferred_element_type=F32)


def _dot_nt(a, b):
    return lax.dot_general(a, b, (((1,), (1,)), ((), ())), preferred_element_type=F32)


def _dot_tn(a, b):
    return lax.dot_general(a, b, (((0,), (0,)), ((), ())), preferred_element_type=F32)


def _row(tm, width, col=0):
    return pl.BlockSpec((1, tm, width), lambda b, i, *_: (b, i, col))


def _seqvec(width):
    return pl.BlockSpec((1, 1, width), lambda b, *_: (b, 0, 0))


def _full2(shape):
    return pl.BlockSpec(shape, lambda *_: (0, 0))


def _sigmoid(x):
    return jax.nn.sigmoid(x)


def _mm(a, b, *, ta=False, tb=False, out_dtype=F32, name):
    if ta:
        kdim, m = a.shape
    else:
        m, kdim = a.shape
    n = b.shape[0] if tb else b.shape[1]
    tm = _pick_tile(m, (1408, 1024, 512, 256, 128))
    tn = _pick_tile(n, (1408, 1024, 512, 256, 128))
    tk = _pick_tile(kdim, (1024, 512, 256, 128))
    nk = kdim // tk

    def body(a_ref, b_ref, o_ref, acc_ref):
        k = pl.program_id(2)

        @pl.when(k == 0)
        def _():
            acc_ref[...] = jnp.zeros_like(acc_ref)

        av = a_ref[...].astype(BF16)
        bv = b_ref[...].astype(BF16)
        dn = (((0 if ta else 1,), (1 if tb else 0,)), ((), ()))
        acc_ref[...] += lax.dot_general(av, bv, dn, preferred_element_type=F32)

        @pl.when(k == nk - 1)
        def _():
            o_ref[...] = acc_ref[...].astype(out_dtype)

    a_spec = pl.BlockSpec((tk, tm), lambda i, j, k: (k, i)) if ta else pl.BlockSpec((tm, tk), lambda i, j, k: (i, k))
    b_spec = pl.BlockSpec((tn, tk), lambda i, j, k: (j, k)) if tb else pl.BlockSpec((tk, tn), lambda i, j, k: (k, j))
    return pl.pallas_call(
        body, name=name, grid=(m // tm, n // tn, nk),
        in_specs=[a_spec, b_spec],
        out_specs=pl.BlockSpec((tm, tn), lambda i, j, k: (i, j)),
        out_shape=SDS((m, n), out_dtype),
        scratch_shapes=[pltpu.VMEM((tm, tn), F32)],
        compiler_params=_cp("parallel", "parallel", "arbitrary"),
    )(a, b)


def _flat(a):
    return a.reshape(-1, a.shape[-1])


def _norm_mod(x, g, sh, sc, name):
    bsz, seq, dm = x.shape
    tm = _pick_tile(seq, (512, 256, 128))

    def body(x_ref, g_ref, sh_ref, sc_ref, o_ref):
        xf = x_ref[0]
        r = lax.rsqrt(jnp.mean(xf * xf, axis=-1, keepdims=True) + EPS)
        hn = xf * r * g_ref[...]
        o_ref[0] = (hn * (1.0 + sc_ref[0]) + sh_ref[0]).astype(BF16)

    return pl.pallas_call(
        body, name=name, grid=(bsz, seq // tm),
        in_specs=[_row(tm, dm), _full2((1, dm)), _seqvec(dm), _seqvec(dm)],
        out_specs=_row(tm, dm), out_shape=SDS((bsz, seq, dm), BF16),
        compiler_params=_cp("parallel", "parallel"),
    )(x, g, sh, sc)


def _swiglu_up(h, w1, w3, name):
    bsz, seq, dm = h.shape
    nf = w1.shape[0]
    tm = _pick_tile(seq, (512, 256, 128))
    tn = _pick_tile(nf, (1408, 512, 256, 128))

    def body(h_ref, w1_ref, w3_ref, a_ref, b_ref, hid_ref):
        hv = h_ref[0]
        a = _dot_nt(hv, w1_ref[...])
        b = _dot_nt(hv, w3_ref[...])
        sg = _sigmoid(a)
        sa = a * sg
        a_ref[0] = (b * (sg * (1.0 + a * (1.0 - sg)))).astype(BF16)
        b_ref[0] = sa.astype(BF16)
        hid_ref[0] = (sa * b).astype(BF16)

    wspec = pl.BlockSpec((tn, dm), lambda n, b, i: (n, 0))
    ospec = pl.BlockSpec((1, tm, tn), lambda n, b, i: (b, i, n))
    shp = SDS((bsz, seq, nf), BF16)
    return pl.pallas_call(
        body, name=name, grid=(nf // tn, bsz, seq // tm),
        in_specs=[pl.BlockSpec((1, tm, dm), lambda n, b, i: (b, i, 0)), wspec, wspec],
        out_specs=[ospec, ospec, ospec], out_shape=[shp, shp, shp],
        compiler_params=_cp("parallel", "parallel", "parallel"),
    )(h, w1, w3)


def _ffn_down(hid, w2, x, gt, name):
    bsz, seq, nf = hid.shape
    dm = w2.shape[1]
    tm = _pick_tile(seq, (512, 256, 128))

    def body(hid_ref, w2_ref, x_ref, gt_ref, f_ref, xo_ref):
        f = _dot(hid_ref[0], w2_ref[...])
        f_ref[0] = f.astype(BF16)
        xo_ref[0] = x_ref[0] + 0.5 * gt_ref[0] * f

    shp = SDS((bsz, seq, dm), F32)
    return pl.pallas_call(
        body, name=name, grid=(bsz, seq // tm),
        in_specs=[_row(tm, nf), _full2((nf, dm)), _row(tm, dm), _seqvec(dm)],
        out_specs=[_row(tm, dm), _row(tm, dm)], out_shape=[SDS((bsz, seq, dm), BF16), shp],
        compiler_params=_cp("parallel", "parallel"),
    )(hid, w2, x, gt)


def _ffn_bwd_hid(dxo, gt, f, a, b, w2, name):
    bsz, seq, dm = dxo.shape
    nf = a.shape[2]
    tm = _pick_tile(seq, (512, 256, 128))
    tn = _pick_tile(nf, (1408, 512, 256, 128))

    def body(dxo_ref, gt_ref, f_ref, a_ref, b_ref, w2_ref, dfs_ref, da_ref, db_ref, dgt_ref):
        i = pl.program_id(1)
        n = pl.program_id(2)

        @pl.when(n == 0)
        def _():
            dxo = dxo_ref[0]
            dfs_ref[0] = (0.5 * gt_ref[0] * dxo).astype(BF16)
            part = jnp.sum(0.5 * dxo * f_ref[0].astype(F32), axis=0, keepdims=True)

            @pl.when(i == 0)
            def _():
                dgt_ref[0] = part

            @pl.when(i > 0)
            def _():
                dgt_ref[0] += part

        dhid = _dot_nt(dfs_ref[0], w2_ref[pl.ds(pl.multiple_of(n * tn, tn), tn), :])
        dh16 = dhid.astype(BF16)
        da_ref[0] = dh16 * a_ref[0]
        db_ref[0] = dh16 * b_ref[0]

    hspec = pl.BlockSpec((1, tm, tn), lambda b, i, n: (b, i, n))
    return pl.pallas_call(
        body, name=name, grid=(bsz, seq // tm, nf // tn),
        in_specs=[_row(tm, dm), _seqvec(dm), _row(tm, dm), hspec, hspec, _full2((nf, dm))],
        out_specs=[_row(tm, dm), hspec, hspec, _seqvec(dm)],
        out_shape=[SDS((bsz, seq, dm), BF16), SDS((bsz, seq, nf), BF16), SDS((bsz, seq, nf), BF16),
                   SDS((bsz, 1, dm), F32)],
        compiler_params=_cp("arbitrary", "arbitrary", "arbitrary"),
    )(dxo, gt, f, a, b, w2)


def _dh_norm_bwd(pieces, weights, x, g, sc, dxo, name, transposed=False):
    bsz, seq, dm = x.shape
    tm = _pick_tile(seq, (512, 256, 128))
    npc = len(pieces)
    dot = _dot if transposed else _dot_nt

    def body(*refs):
        p_refs = refs[:npc]
        w_hbm = refs[npc:2 * npc]
        x_ref, g_ref, sc_ref, dxo_ref, dx_ref, dsh_ref, dsc_ref, dg_ref = refs[2 * npc:2 * npc + 8]
        w_refs = refs[2 * npc + 8:]
        b = pl.program_id(0)
        i = pl.program_id(1)

        @pl.when((i == 0) & (b == 0))
        def _():
            for src, dst in zip(w_hbm, w_refs):
                pltpu.sync_copy(src, dst)

        dh = dot(p_refs[0][0], w_refs[0][...])
        for j in range(1, npc):
            dh = dh + dot(p_refs[j][0], w_refs[j][...])
        xf = x_ref[0]
        gv = g_ref[...]
        r = lax.rsqrt(jnp.mean(xf * xf, axis=-1, keepdims=True) + EPS)
        xhat = xf * r
        dhn = dh * (1.0 + sc_ref[0])
        p_sh = jnp.sum(dh, axis=0, keepdims=True)
        p_sc = jnp.sum(dh * (xhat * gv), axis=0, keepdims=True)
        p_g = jnp.sum(dhn * xhat, axis=0, keepdims=True)
        dxh = dhn * gv
        dx_ref[0] = dxo_ref[0] + r * (dxh - xhat * jnp.mean(dxh * xhat, axis=-1, keepdims=True))

        @pl.when(i == 0)
        def _():
            dsh_ref[0] = p_sh
            dsc_ref[0] = p_sc

        @pl.when(i > 0)
        def _():
            dsh_ref[0] += p_sh
            dsc_ref[0] += p_sc

        @pl.when((i == 0) & (b == 0))
        def _():
            dg_ref[...] = p_g

        @pl.when((i > 0) | (b > 0))
        def _():
            dg_ref[...] += p_g

    return pl.pallas_call(
        body, name=name, grid=(bsz, seq // tm),
        in_specs=[_row(tm, p.shape[2]) for p in pieces] + [pl.BlockSpec(memory_space=pl.ANY)] * npc + [
            _row(tm, dm), _full2((1, dm)), _seqvec(dm), _row(tm, dm)],
        out_specs=[_row(tm, dm), _seqvec(dm), _seqvec(dm), _full2((1, dm))],
        out_shape=[SDS((bsz, seq, dm), F32), SDS((bsz, 1, dm), F32), SDS((bsz, 1, dm), F32), SDS((1, dm), F32)],
        scratch_shapes=[pltpu.VMEM(w.shape, w.dtype) for w in weights],
        compiler_params=_cp("arbitrary", "arbitrary"),
    )(*pieces, *weights, x, g, sc, dxo)


HALO = 16


def _conv_core(gc, v, gch, vh, w, first):
    cv = gc * v
    halo = jnp.where(first, 0.0, gch * vh)
    ext = jnp.concatenate([halo, cv], axis=0)
    cv1 = pltpu.roll(ext, 1, 0)[HALO:]
    cv2 = pltpu.roll(ext, 2, 0)[HALO:]
    conv = w[0:1] * cv2 + w[1:2] * cv1 + w[2:3] * cv
    return cv, cv1, cv2, conv


def _prev_halo(tm, col):
    return pl.BlockSpec((1, HALO, CW), lambda b, i, *_: (b, jnp.maximum(i * (tm // HALO) - 1, 0), col))


def _next_halo(tm, seq, col):
    return pl.BlockSpec((1, HALO, CW), lambda b, i, *_: (b, jnp.minimum((i + 1) * (tm // HALO), seq // HALO - 1), col))


def _conv_fwd(p1, convw8, name):
    bsz, seq, _ = p1.shape
    tm = _pick_tile(seq, (512, 256, 128))

    def body(gb_ref, gc_ref, v_ref, gch_ref, vh_ref, w_ref, o_ref):
        first = pl.program_id(1) == 0
        _, _, _, conv = _conv_core(gc_ref[0].astype(F32), v_ref[0].astype(F32), gch_ref[0].astype(F32),
                                   vh_ref[0].astype(F32), w_ref[...], first)
        o_ref[0] = (gb_ref[0].astype(F32) * conv).astype(BF16)

    return pl.pallas_call(
        body, name=name, grid=(bsz, seq // tm),
        in_specs=[_row(tm, CW, 0), _row(tm, CW, 1), _row(tm, CW, 2), _prev_halo(tm, 1), _prev_halo(tm, 2),
                  _full2((8, CW))],
        out_specs=_row(tm, CW), out_shape=SDS((bsz, seq, CW), BF16),
        compiler_params=_cp("parallel", "parallel"),
    )(p1, p1, p1, p1, p1, convw8)


def _conv_bwd(dya, wco, p1, convw8, name):
    bsz, seq, _ = p1.shape
    tm = _pick_tile(seq, (512, 256, 128))
    nt = seq // tm
    ext_rows = tm + HALO

    def body(dya_ref, dyan_ref, wco_ref, gb_ref, gbn_ref, gc_ref, v_ref, gch_ref, vh_ref, w_ref, dp_ref, dw_ref):
        b = pl.program_id(0)
        i = pl.program_id(1)
        w = w_ref[...]
        gc = gc_ref[0].astype(F32)
        vv = v_ref[0].astype(F32)
        cv, cv1, cv2, conv = _conv_core(gc, vv, gch_ref[0].astype(F32), vh_ref[0].astype(F32), w, i == 0)
        dya_ext = jnp.concatenate([dya_ref[0], dyan_ref[0]], axis=0)
        dyain_ext = _dot_nt(dya_ext, wco_ref[...])
        gb_ext = jnp.concatenate([gb_ref[0], gbn_ref[0]], axis=0).astype(F32)
        rows = lax.broadcasted_iota(jnp.int32, (ext_rows, 1), 0)
        dconv_ext = jnp.where((rows < tm) | (i < nt - 1), dyain_ext * gb_ext, 0.0)
        dconv = dconv_ext[:tm]
        dconv1 = pltpu.roll(dconv_ext, ext_rows - 1, 0)[:tm]
        dconv2 = pltpu.roll(dconv_ext, ext_rows - 2, 0)[:tm]
        dcv = w[2:3] * dconv + w[1:2] * dconv1 + w[0:1] * dconv2
        dp_ref[0, :, 0:CW] = (dyain_ext[:tm] * conv).astype(BF16)
        dp_ref[0, :, CW:2 * CW] = (dcv * vv).astype(BF16)
        dp_ref[0, :, 2 * CW:3 * CW] = (dcv * gc).astype(BF16)
        g0 = jnp.sum(dconv * cv2, axis=0, keepdims=True)
        g1 = jnp.sum(dconv * cv1, axis=0, keepdims=True)
        g2 = jnp.sum(dconv * cv, axis=0, keepdims=True)
        upd = jnp.concatenate([g0, g1, g2, jnp.zeros((5, CW), F32)], axis=0)

        @pl.when((i == 0) & (b == 0))
        def _():
            dw_ref[...] = upd

        @pl.when((i > 0) | (b > 0))
        def _():
            dw_ref[...] += upd

    return pl.pallas_call(
        body, name=name, grid=(bsz, seq // tm),
        in_specs=[_row(tm, CW), _next_halo(tm, seq, 0), _full2((CW, D)),
                  _row(tm, CW, 0), _next_halo(tm, seq, 0), _row(tm, CW, 1), _row(tm, CW, 2),
                  _prev_halo(tm, 1), _prev_halo(tm, 2), _full2((8, CW))],
        out_specs=[_row(tm, 3 * CW), _full2((8, CW))],
        out_shape=[SDS((bsz, seq, 3 * CW), BF16), SDS((8, CW), F32)],
        compiler_params=_cp("arbitrary", "arbitrary"),
    )(dya, dya, wco, p1, p1, p1, p1, p1, p1, convw8)


def _disc(are, aim, ldt, bre, bim):
    dt = jnp.exp(ldt)
    mag = jnp.exp(are * dt)
    ang = aim * dt
    abr = mag * jnp.cos(ang)
    abi = mag * jnp.sin(ang)
    nr = abr - 1.0
    den = are * are + aim * aim
    cr = (nr * are + abi * aim) / den
    ci = (abi * are - nr * aim) / den
    return abr, abi, cr * bre - ci * bim, cr * bim + ci * bre


def _ssm_disc(are, aim, ldt, bre_t, bim_t):
    def body(are_ref, aim_ref, ldt_ref, bre_ref, bim_ref, abr_ref, abi_ref, bbr_ref, bbi_ref):
        abr, abi, bbr, bbi = _disc(are_ref[...], aim_ref[...], ldt_ref[...], bre_ref[...], bim_ref[...])
        abr_ref[...] = abr
        abi_ref[...] = abi
        bbr_ref[...] = bbr
        bbi_ref[...] = bbi

    v1, vh = SDS((1, GP), F32), SDS((NH, GP), F32)
    return pl.pallas_call(body, name="ssm_disc", out_shape=[v1, v1, vh, vh], compiler_params=_cp())(
        are, aim, ldt, bre_t, bim_t)


def _ssm_disc_bwd(are, aim, ldt, bre_t, bim_t, dabr, dabi, dbbr, dbbi):
    def body(are_ref, aim_ref, ldt_ref, bre_ref, bim_ref, g0, g1, g2, g3, o0, o1, o2, o3, o4):
        prim = (are_ref[...], aim_ref[...], ldt_ref[...], bre_ref[...], bim_ref[...])
        _, vjp = jax.vjp(_disc, *prim)
        d_are, d_aim, d_ldt, d_bre, d_bim = vjp((g0[...], g1[...], g2[...], g3[...]))
        o0[...] = d_are
        o1[...] = d_aim
        o2[...] = d_ldt
        o3[...] = d_bre
        o4[...] = d_bim

    v1, vh = SDS((1, GP), F32), SDS((NH, GP), F32)
    return pl.pallas_call(body, name="ssm_disc_bwd", out_shape=[v1, v1, v1, vh, vh], compiler_params=_cp())(
        are, aim, ldt, bre_t, bim_t, dabr, dabi, dbbr, dbbi)


def _scan_chunk(buf_ref, nt, bsz, ar, ai, init_r, init_i, reverse):
    nsub = SUB // bsz
    row = lax.broadcasted_iota(jnp.int32, (SUB, GP), 0)
    shift = ((SUB - bsz) if reverse else bsz) % SUB
    order = list(range(nsub - 1, -1, -1)) if reverse else list(range(nsub))

    def step(j, carry):
        pr, pi = carry
        jj = (nt - 1 - j) if reverse else j
        off = pl.multiple_of(jj * SUB, SUB)
        br = buf_ref[pl.ds(off, SUB), 0:GP]
        bi = buf_ref[pl.ds(off, SUB), GP:2 * GP]
        nr, ni = pr, pi
        for s in order:
            sr, si = nr, ni
            if shift:
                sr = pltpu.roll(sr, shift, 0)
                si = pltpu.roll(si, shift, 0)
            cr = ar * sr - ai * si + br
            ci = ar * si + ai * sr + bi
            if nsub == 1:
                nr, ni = cr, ci
            else:
                m = (row >= s * bsz) & (row < (s + 1) * bsz)
                nr = jnp.where(m, cr, nr)
                ni = jnp.where(m, ci, ni)
        buf_ref[pl.ds(off, SUB), 0:GP] = nr
        buf_ref[pl.ds(off, SUB), GP:2 * GP] = ni
        return nr, ni

    return lax.fori_loop(0, nt, step, (init_r, init_i))


def _ssm_chunk_rows(total_rows, bsz):
    return min(total_rows, 64 * bsz)


def _interleave(src_ref, tmp_ref, bsz, steps):
    nl = tmp_ref.shape[0]
    for b in range(bsz):
        for j in range(nl):
            tmp_ref.at[j][pl.ds(b, steps, stride=bsz), :] = src_ref[b, :, j * LANE:(j + 1) * LANE]
    return jnp.concatenate([tmp_ref[j] for j in range(nl)], axis=1)


def _deinterleave(val, tmp_ref, dst_ref, bsz, steps, skip=None):
    nl = tmp_ref.shape[0]
    for j in range(nl):
        tmp_ref[j] = val[:, j * LANE:(j + 1) * LANE]
    for b in range(bsz):
        for j in range(nl):
            lanes = slice(j * LANE, (j + 1) * LANE)
            v = tmp_ref.at[j][pl.ds(b, steps, stride=bsz), :]
            if skip is not None:
                v = v + skip[0][b, :, lanes] * skip[1][:, lanes]
            dst_ref[b, :, lanes] = v.astype(dst_ref.dtype)


SSM_UB = SW // SSM_SUPER
SSM_SB = GP // SSM_SUPER


def _sb_cols(s, half):
    return slice(half * GP + s * SSM_SB, half * GP + (s + 1) * SSM_SB)


def _ssm_in(v16, w_ref, x_ref):
    for s in range(SSM_SUPER):
        vs = v16[:, s * SSM_UB:(s + 1) * SSM_UB]
        for half in range(2):
            x_ref[:, _sb_cols(s, half)] = _dot(vs, w_ref[s * SSM_UB:(s + 1) * SSM_UB, half * SSM_SB:(half + 1) * SSM_SB])


def _ssm_out(x16, w_ref):
    outs = []
    for s in range(SSM_SUPER):
        rows = slice(s * SSM_UB, (s + 1) * SSM_UB)
        outs.append(_dot_nt(x16[:, _sb_cols(s, 0)], w_ref[rows, 0:SSM_SB])
                    + _dot_nt(x16[:, _sb_cols(s, 1)], w_ref[rows, SSM_SB:2 * SSM_SB]))
    return jnp.concatenate(outs, axis=1)


def _ssm_fwd(u, wb, wct, ar8, ai8, name):
    bsz, seq, _ = u.shape
    rt = seq * bsz
    r = _ssm_chunk_rows(rt, bsz)
    nt = r // SUB
    steps = r // bsz

    def body(u_ref, wb_hbm, wct_hbm, ar_ref, ai_ref, x_ref, y_ref, wb_ref, wct_ref, st_ref, tmp_ref):
        @pl.when(pl.program_id(0) == 0)
        def _():
            pltpu.sync_copy(wb_hbm, wb_ref)
            pltpu.sync_copy(wct_hbm, wct_ref)
            st_ref[...] = jnp.zeros_like(st_ref)

        _ssm_in(_interleave(u_ref, tmp_ref, bsz, steps).astype(BF16), wb_ref, x_ref)
        fr, fi = _scan_chunk(x_ref, nt, bsz, ar_ref[...], ai_ref[...], st_ref[:, 0:GP], st_ref[:, GP:2 * GP], False)
        st_ref[:, 0:GP] = fr
        st_ref[:, GP:2 * GP] = fi
        _deinterleave(_ssm_out(x_ref[...].astype(BF16), wct_ref), tmp_ref, y_ref, bsz, steps)

    anyspec = pl.BlockSpec(memory_space=pl.ANY)
    seqs = pl.BlockSpec((bsz, steps, SW), lambda i: (0, i, 0))
    return pl.pallas_call(
        body, name=name, grid=(rt // r,),
        in_specs=[seqs, anyspec, anyspec, _full2((SUB, GP)), _full2((SUB, GP))],
        out_specs=[pl.BlockSpec((r, 2 * GP), lambda i: (i, 0)), seqs],
        out_shape=[SDS((rt, 2 * GP), F32), SDS((bsz, seq, SW), F32)],
        scratch_shapes=[pltpu.VMEM(wb.shape, BF16), pltpu.VMEM(wct.shape, BF16), pltpu.VMEM((SUB, 2 * GP), F32),
                        pltpu.VMEM((SW // LANE, r, LANE), F32)],
        compiler_params=_cp("arbitrary"),
    )(u, wb, wct, ar8, ai8)


def _ssm_bwd(dy, u, xs, wb, wct, ar8, ai8, dskip, name):
    bsz, seq, _ = u.shape
    rt = seq * bsz
    r = _ssm_chunk_rows(rt, bsz)
    nt = r // SUB
    nc = rt // r
    steps = r // bsz
    ub = SW // SSM_SUPER
    sb = GP // SSM_SUPER

    def body(dys_ref, us_ref, x_ref, xh_ref, wb_hbm, wct_hbm, ar_ref, ai_ref, d_ref,
             dus_ref, dwb_hbm, dwct_hbm, dar_ref, dai_ref, wb_ref, wct_ref, g_ref, st_ref, awb_ref, awct_ref,
             tmp_ref):
        i = pl.program_id(0)
        dyb = _interleave(dys_ref, tmp_ref, bsz, steps).astype(BF16)
        ub16 = _interleave(us_ref, tmp_ref, bsz, steps).astype(BF16)

        @pl.when(i == 0)
        def _():
            pltpu.sync_copy(wb_hbm, wb_ref)
            pltpu.sync_copy(wct_hbm, wct_ref)
            st_ref[...] = jnp.zeros_like(st_ref)
            awb_ref[...] = jnp.zeros_like(awb_ref)
            awct_ref[...] = jnp.zeros_like(awct_ref)
            dar_ref[...] = jnp.zeros_like(dar_ref)
            dai_ref[...] = jnp.zeros_like(dai_ref)

        _ssm_in(dyb, wct_ref, g_ref)
        ar = ar_ref[...]
        ai = ai_ref[...]
        fr, fi = _scan_chunk(g_ref, nt, bsz, ar, -ai, st_ref[:, 0:GP], st_ref[:, GP:2 * GP], True)
        st_ref[:, 0:GP] = fr
        st_ref[:, GP:2 * GP] = fi

        gb = g_ref[...].astype(BF16)
        _deinterleave(_ssm_out(gb, wb_ref), tmp_ref, dus_ref, bsz, steps, skip=(dys_ref, d_ref))
        xb16 = x_ref[...].astype(BF16)
        for s in range(SSM_SUPER):
            us = ub16[:, s * ub:(s + 1) * ub]
            ds = dyb[:, s * ub:(s + 1) * ub]
            for half in range(2):
                cols = slice(half * GP + s * sb, half * GP + (s + 1) * sb)
                ocols = slice(half * sb, (half + 1) * sb)
                awb_ref[s * ub:(s + 1) * ub, ocols] += _dot_tn(us, gb[:, cols])
                awct_ref[s * ub:(s + 1) * ub, ocols] += _dot_tn(ds, xb16[:, cols])

        gr = g_ref[:, 0:GP]
        gi = g_ref[:, GP:2 * GP]
        xsr = pltpu.roll(x_ref[:, 0:GP], bsz, 0)
        xsi = pltpu.roll(x_ref[:, GP:2 * GP], bsz, 0)
        inner = lax.broadcasted_iota(jnp.int32, (r, 1), 0) >= bsz
        t_r = jnp.where(inner, gr * xsr + gi * xsi, 0.0)
        t_i = jnp.where(inner, gi * xsr - gr * xsi, 0.0)
        acc_r = jnp.sum(t_r.reshape(nt, SUB, GP), axis=0)
        acc_i = jnp.sum(t_i.reshape(nt, SUB, GP), axis=0)
        hr = xh_ref[:, 0:GP]
        hi = xh_ref[:, GP:2 * GP]
        if bsz % SUB:
            hr = pltpu.roll(hr, bsz, 0)
            hi = pltpu.roll(hi, bsz, 0)
        edge = (lax.broadcasted_iota(jnp.int32, (SUB, 1), 0) < bsz) & (i < nc - 1)
        g0r = g_ref[0:SUB, 0:GP]
        g0i = g_ref[0:SUB, GP:2 * GP]
        dar_ref[...] += acc_r + jnp.where(edge, g0r * hr + g0i * hi, 0.0)
        dai_ref[...] += acc_i + jnp.where(edge, g0i * hr - g0r * hi, 0.0)

        @pl.when(i == nc - 1)
        def _():
            pltpu.sync_copy(awb_ref, dwb_hbm)
            pltpu.sync_copy(awct_ref, dwct_hbm)

    anyspec = pl.BlockSpec(memory_space=pl.ANY)
    rev = lambda i: (nc - 1 - i, 0)
    seqs = pl.BlockSpec((bsz, steps, SW), lambda i: (0, nc - 1 - i, 0))
    wshape = (SW, 2 * sb)
    return pl.pallas_call(
        body, name=name, grid=(nc,),
        in_specs=[seqs, seqs, pl.BlockSpec((r, 2 * GP), rev),
                  pl.BlockSpec((SUB, 2 * GP), lambda i: (jnp.maximum((nc - 1 - i) * nt - 1, 0), 0)),
                  anyspec, anyspec, _full2((SUB, GP)), _full2((SUB, GP)), _full2((1, SW))],
        out_specs=[seqs, anyspec, anyspec, _full2((SUB, GP)), _full2((SUB, GP))],
        out_shape=[SDS((bsz, seq, SW), BF16), SDS(wshape, F32), SDS(wshape, F32), SDS((SUB, GP), F32),
                   SDS((SUB, GP), F32)],
        scratch_shapes=[pltpu.VMEM(wb.shape, BF16), pltpu.VMEM(wct.shape, BF16),
                        pltpu.VMEM((r, 2 * GP), F32), pltpu.VMEM((SUB, 2 * GP), F32),
                        pltpu.VMEM(wshape, F32), pltpu.VMEM(wshape, F32),
                        pltpu.VMEM((SW // LANE, r, LANE), F32)],
        compiler_params=_cp("arbitrary"),
    )(dy, u, xs, xs, wb, wct, ar8, ai8, dskip)


GELU_C = math.sqrt(2.0 / math.pi)


def _gelu(x):
    return 0.5 * x * (1.0 + jnp.tanh(GELU_C * (x + 0.044715 * x * x * x)))


def _gelu_grad(x):
    th = jnp.tanh(GELU_C * (x + 0.044715 * x * x * x))
    return 0.5 * (1.0 + th) + 0.5 * x * (1.0 - th * th) * GELU_C * (1.0 + 3.0 * 0.044715 * x * x)


def _ssm_post(ys, u, dskip, wglu, wso, name):
    bsz, seq, _ = ys.shape
    tm = _pick_tile(seq, (512, 256, 128))

    def body(ys_ref, u_ref, d_ref, wg_ref, wo_ref, s0_ref, z_ref, s1_ref, s2_ref, yb_ref):
        s0 = ys_ref[0] + d_ref[...] * u_ref[0]
        s1 = _gelu(s0)
        s1b = s1.astype(BF16)
        z = _dot(s1b, wg_ref[...])
        s2b = (s1 * _sigmoid(z)).astype(BF16)
        s0_ref[0] = s0
        z_ref[0] = z
        s1_ref[0] = s1b
        s2_ref[0] = s2b
        yb_ref[0] = _dot(s2b, wo_ref[...]).astype(BF16)

    return pl.pallas_call(
        body, name=name, grid=(bsz, seq // tm),
        in_specs=[_row(tm, SW), _row(tm, SW), _full2((1, SW)), _full2((SW, SW)), _full2((SW, D))],
        out_specs=[_row(tm, SW), _row(tm, SW), _row(tm, SW), _row(tm, SW), _row(tm, D)],
        out_shape=[SDS((bsz, seq, SW), F32), SDS((bsz, seq, SW), F32), SDS((bsz, seq, SW), BF16),
                   SDS((bsz, seq, SW), BF16), SDS((bsz, seq, D), BF16)],
        compiler_params=_cp("parallel", "parallel"),
    )(ys, u, dskip, wglu, wso)


def _ssm_post_bwd(dyb, s0, z, u, dskip, wglu, wso, name):
    bsz, seq, _ = s0.shape
    tm = _pick_tile(seq, (512, 256, 128))

    def body(dyb_ref, s0_ref, z_ref, u_ref, wg_ref, wo_ref, ds0_ref, dz_ref, dd_ref):
        b = pl.program_id(0)
        i = pl.program_id(1)
        ds2 = _dot_nt(dyb_ref[0], wo_ref[...])
        s0 = s0_ref[0]
        s1 = _gelu(s0)
        sg = _sigmoid(z_ref[0])
        dz = ds2 * s1 * sg * (1.0 - sg)
        dzb = dz.astype(BF16)
        ds1 = ds2 * sg + _dot_nt(dzb, wg_ref[...])
        ds0 = ds1 * _gelu_grad(s0)
        ds0_ref[0] = ds0
        dz_ref[0] = dzb
        part = jnp.sum(ds0 * u_ref[0], axis=0, keepdims=True)

        @pl.when((i == 0) & (b == 0))
        def _():
            dd_ref[...] = part

        @pl.when((i > 0) | (b > 0))
        def _():
            dd_ref[...] += part

    del dskip
    return pl.pallas_call(
        body, name=name, grid=(bsz, seq // tm),
        in_specs=[_row(tm, D), _row(tm, SW), _row(tm, SW), _row(tm, SW), _full2((SW, SW)), _full2((SW, D))],
        out_specs=[_row(tm, SW), _row(tm, SW), _full2((1, SW))],
        out_shape=[SDS((bsz, seq, SW), F32), SDS((bsz, seq, SW), BF16), SDS((1, SW), F32)],
        compiler_params=_cp("arbitrary", "arbitrary"),
    )(dyb, s0, z, u, wglu, wso)


def _merge_out(ya, yb, p3, wout, x1, gt, name):
    bsz, seq, _ = ya.shape
    tm = _pick_tile(seq, (512, 256, 128))

    def body(ya_ref, yb_ref, ga_ref, gbb_ref, w_ref, x_ref, gt_ref, mg_ref, mix_ref, xo_ref):
        merged = (_sigmoid(ga_ref[0].astype(F32)) * ya_ref[0].astype(F32)
                  + _sigmoid(gbb_ref[0].astype(F32)) * yb_ref[0].astype(F32)).astype(BF16)
        mix = _dot(merged, w_ref[...])
        mg_ref[0] = merged
        mix_ref[0] = mix.astype(BF16)
        xo_ref[0] = x_ref[0] + gt_ref[0] * mix

    return pl.pallas_call(
        body, name=name, grid=(bsz, seq // tm),
        in_specs=[_row(tm, D), _row(tm, D), _row(tm, D, 0), _row(tm, D, 1), _full2((D, D)), _row(tm, D), _seqvec(D)],
        out_specs=[_row(tm, D), _row(tm, D), _row(tm, D)],
        out_shape=[SDS((bsz, seq, D), BF16), SDS((bsz, seq, D), BF16), SDS((bsz, seq, D), F32)],
        compiler_params=_cp("parallel", "parallel"),
    )(ya, yb, p3, p3, wout, x1, gt)


def _merge_bwd(dx2, gt, mix, ya, yb, p3, wout, name):
    bsz, seq, _ = ya.shape
    tm = _pick_tile(seq, (512, 256, 128))

    def body(dx_ref, gt_ref, mix_ref, ya_ref, yb_ref, ga_ref, gbb_ref, w_ref, dmix_ref, dya_ref, dyb_ref, dp_ref, dgt_ref):
        i = pl.program_id(1)
        dx = dx_ref[0]
        dmix = (gt_ref[0] * dx).astype(BF16)
        dmix_ref[0] = dmix
        part = jnp.sum(dx * mix_ref[0].astype(F32), axis=0, keepdims=True)

        @pl.when(i == 0)
        def _():
            dgt_ref[0] = part

        @pl.when(i > 0)
        def _():
            dgt_ref[0] += part

        dmg = _dot_nt(dmix, w_ref[...])
        sa = _sigmoid(ga_ref[0].astype(F32))
        sb = _sigmoid(gbb_ref[0].astype(F32))
        dya_ref[0] = (dmg * sa).astype(BF16)
        dyb_ref[0] = (dmg * sb).astype(BF16)
        dp_ref[0, :, 0:D] = (dmg * ya_ref[0].astype(F32) * sa * (1.0 - sa)).astype(BF16)
        dp_ref[0, :, D:2 * D] = (dmg * yb_ref[0].astype(F32) * sb * (1.0 - sb)).astype(BF16)

    bshape = SDS((bsz, seq, D), BF16)
    return pl.pallas_call(
        body, name=name, grid=(bsz, seq // tm),
        in_specs=[_row(tm, D), _seqvec(D), _row(tm, D), _row(tm, D), _row(tm, D), _row(tm, D, 0), _row(tm, D, 1),
                  _full2((D, D))],
        out_specs=[_row(tm, D), _row(tm, D), _row(tm, D), _row(tm, 2 * D), _seqvec(D)],
        out_shape=[bshape, bshape, bshape, SDS((bsz, seq, 2 * D), BF16), SDS((bsz, 1, D), F32)],
        compiler_params=_cp("arbitrary", "arbitrary"),
    )(dx2, gt, mix, ya, yb, p3, p3, wout)


def _final_loss(x3, gfin, target, name):
    bsz, seq, dm = x3.shape
    tm = _pick_tile(seq, (512, 256, 128))

    def body(x_ref, g_ref, t_ref, dx_ref, loss_ref, dg_ref):
        b = pl.program_id(0)
        i = pl.program_id(1)
        xf = x_ref[0]
        gv = g_ref[...]
        r = lax.rsqrt(jnp.mean(xf * xf, axis=-1, keepdims=True) + EPS)
        xhat = xf * r
        e = xhat * gv - t_ref[0]
        dy = e * (1.0 / dm)
        dxh = dy * gv
        dx_ref[0] = r * (dxh - xhat * jnp.mean(dxh * xhat, axis=-1, keepdims=True))
        p_l = jnp.sum(e * e, axis=0, keepdims=True) * (0.5 / dm)
        p_g = jnp.sum(dy * xhat, axis=0, keepdims=True)

        @pl.when((i == 0) & (b == 0))
        def _():
            loss_ref[...] = p_l
            dg_ref[...] = p_g

        @pl.when((i > 0) | (b > 0))
        def _():
            loss_ref[...] += p_l
            dg_ref[...] += p_g

    return pl.pallas_call(
        body, name=name, grid=(bsz, seq // tm),
        in_specs=[_row(tm, dm), _full2((1, dm)), _row(tm, dm)],
        out_specs=[_row(tm, dm), _full2((1, dm)), _full2((1, dm))],
        out_shape=[SDS((bsz, seq, dm), F32), SDS((1, dm), F32), SDS((1, dm), F32)],
        compiler_params=_cp("arbitrary", "arbitrary"),
    )(x3, gfin, target)


def _ada_fwd(c_all, w_shard, b_shard):
    nb = c_all.shape[0]
    n = w_shard.shape[2]

    def body(c_ref, w_ref, b_ref, o_ref):
        cv = c_ref[...]
        cond = (cv * _sigmoid(cv)).astype(BF16)
        o_ref[...] = _dot(cond, w_ref[0].astype(BF16)) + b_ref[...]

    return pl.pallas_call(body, name="ada_fwd", out_shape=SDS((nb, n), F32), compiler_params=_cp())(
        c_all, w_shard, b_shard)


def _ada_bwd(c_all, dmod_shard, dmod_all):
    n = dmod_shard.shape[1]

    def body(c_ref, ds_ref, da_ref, gw_ref, gb_ref):
        cv = c_ref[...]
        cond = (cv * _sigmoid(cv)).astype(BF16)
        gw_ref[...] = _dot_tn(cond, ds_ref[...].astype(BF16))
        gb_ref[...] = jnp.sum(da_ref[...], axis=0, keepdims=True)

    return pl.pallas_call(
        body, name="ada_bwd", out_shape=[SDS((D, n), F32), SDS((1, dmod_all.shape[1]), F32)], compiler_params=_cp(),
    )(c_all, dmod_shard, dmod_all)


def _adamw_math(w, g, m, v):
    m = B1 * m + (1.0 - B1) * g
    v = B2 * v + (1.0 - B2) * (g * g)
    delta = -LR * ((m / BC1) / (jnp.sqrt(v / BC2) + AEPS) + WD * w)
    return delta, m, v


def _adamw_big(w, m, v, recv_own, recv_sib, name):
    _, rows, cols = w.shape
    tr = _pick_tile(rows, tuple(t for t in (512, 256, 128, 64, 32, 16, 8) if t * cols <= 192 * 1024))

    def body(w_ref, m_ref, v_ref, a_ref, b_ref, g_ref, d_ref, mo_ref, vo_ref):
        def chip_sum(r):
            acc = r[0].astype(F32)
            for k in range(1, N_CHIPS):
                acc = acc + r[k].astype(F32)
            return acc

        g = chip_sum(a_ref) + chip_sum(b_ref)
        delta, mn, vn = _adamw_math(w_ref[0], g, m_ref[0], v_ref[0])
        g_ref[0] = g
        d_ref[0] = delta
        mo_ref[0] = mn
        vo_ref[0] = vn

    own = pl.BlockSpec((1, tr, cols), lambda i: (0, i, 0))
    rspec = pl.BlockSpec((N_CHIPS, tr, cols), lambda i: (0, i, 0))
    shp = SDS(w.shape, F32)
    return pl.pallas_call(
        body, name=name, grid=(rows // tr,),
        in_specs=[own, own, own, rspec, rspec], out_specs=[own, own, own, own], out_shape=[shp, shp, shp, shp],
        compiler_params=_cp("parallel"),
    )(w, m, v, recv_own, recv_sib)


def _adamw_plain(w, m, v, g, name):
    def body(w_ref, m_ref, v_ref, g_ref, d_ref, mo_ref, vo_ref):
        delta, mn, vn = _adamw_math(w_ref[...], g_ref[...], m_ref[...], v_ref[...])
        d_ref[...] = delta
        mo_ref[...] = mn
        vo_ref[...] = vn

    shp = SDS(w.shape, F32)
    return pl.pallas_call(body, name=name, out_shape=[shp, shp, shp], compiler_params=_cp())(w, m, v, g)


def _adamw_rows(w, m, v, g, name):
    _, rows, cols = w.shape
    tr = _pick_tile(rows, (128, 64, 32, 16, 8))

    def body(w_ref, m_ref, v_ref, g_ref, d_ref, mo_ref, vo_ref):
        delta, mn, vn = _adamw_math(w_ref[0], g_ref[...], m_ref[0], v_ref[0])
        d_ref[0] = delta
        mo_ref[0] = mn
        vo_ref[0] = vn

    spec = pl.BlockSpec((1, tr, cols), lambda i: (0, i, 0))
    shp = SDS(w.shape, F32)
    return pl.pallas_call(
        body, name=name, grid=(rows // tr,), in_specs=[spec] * 3 + [pl.BlockSpec((tr, cols), lambda i: (i, 0))],
        out_specs=[spec] * 3, out_shape=[shp] * 3, compiler_params=_cp("parallel"),
    )(w, m, v, g)


def _sum_slabs(r, name):
    n, rows, cols = r.shape
    tr = _pick_tile(rows, (256, 128, 64))

    def body(r_ref, o_ref):
        acc = r_ref[0].astype(F32)
        for j in range(1, n):
            acc = acc + r_ref[j].astype(F32)
        o_ref[...] = acc

    return pl.pallas_call(
        body, name=name, grid=(rows // tr,),
        in_specs=[pl.BlockSpec((n, tr, cols), lambda i: (0, i, 0))],
        out_specs=pl.BlockSpec((tr, cols), lambda i: (i, 0)), out_shape=SDS((rows, cols), F32),
        compiler_params=_cp("parallel"),
    )(r)


def _place():
    return lax.axis_index("x"), lax.axis_index("y"), lax.axis_index("c")


def _all_gather8(blk, name):
    m_per, n = blk.shape

    def body(x_ref, out_ref, send_sems, recv_sems, local_sem):
        x, y, c = _place()
        me, sibling = (x, y, c), (x, y, 1 - c)
        chips = [(1 - x, y), (x, 1 - y), (1 - x, 1 - y)]

        def rows(px, py, pc):
            return out_ref.at[pl.ds((4 * px + 2 * py + pc) * m_per, m_per), :]

        def copy(k, block, to, src=None):
            return pltpu.make_async_remote_copy(
                src_ref=rows(*block) if src is None else src, dst_ref=rows(*block),
                send_sem=send_sems.at[k], recv_sem=recv_sems.at[k], device_id=to, device_id_type=MESH)

        mine = pltpu.make_async_copy(x_ref, rows(*me), local_sem)
        mine.start()
        first = [copy(0, me, sibling, src=x_ref)]
        first += [copy(1 + j, me, (*chip, c), src=x_ref) for j, chip in enumerate(chips)]
        for cp in first:
            cp.start()
        passed = [copy(4 + j, (*chip, c), sibling) for j, chip in enumerate(chips)]
        for j, chip in enumerate(chips):
            copy(1 + j, (*chip, c), me).wait_recv()
            passed[j].start()
        copy(0, sibling, me).wait_recv()
        for j, chip in enumerate(chips):
            copy(4 + j, (*chip, 1 - c), me).wait_recv()
        for cp in first + passed:
            cp.wait_send()
        mine.wait()

    return pl.pallas_call(
        body, name=name, out_shape=SDS((N_DEV * m_per, n), blk.dtype),
        in_specs=[pl.BlockSpec(memory_space=pltpu.VMEM)], out_specs=pl.BlockSpec(memory_space=pltpu.VMEM),
        scratch_shapes=[pltpu.SemaphoreType.DMA((7,)), pltpu.SemaphoreType.DMA((7,)), pltpu.SemaphoreType.DMA],
        compiler_params=pltpu.CompilerParams(vmem_limit_bytes=VMEM_LIMIT),
    )(blk)


def _chip_peers(x, y):
    return [(1 - x, y), (x, 1 - y), (1 - x, 1 - y)]


SIBLING = "sibling"


def _peer_copies(src_refs, land_refs, send_sems, recv_sems, scatter, landed):
    x, y, c = _place()
    if scatter == SIBLING:
        return [pltpu.make_async_remote_copy(
            src_ref=s, dst_ref=l, send_sem=send_sems.at[a], recv_sem=recv_sems.at[a],
            device_id=(x, y, 1 - c), device_id_type=MESH) for a, (s, l) in enumerate(zip(src_refs, land_refs))]
    cps = []
    for a, (src_ref, land_ref) in enumerate(zip(src_refs, land_refs)):
        for j, (px, py) in enumerate(_chip_peers(x, y)):
            if scatter:
                src = src_ref.at[2 * px + py]
            else:
                src = land_ref.at[2 * x + y] if src_ref is None else src_ref
            dst = land_ref.at[2 * px + py] if landed else land_ref.at[2 * x + y]
            cps.append(pltpu.make_async_remote_copy(
                src_ref=src, dst_ref=dst, send_sem=send_sems.at[3 * a + j], recv_sem=recv_sems.at[3 * a + j],
                device_id=(px, py, c), device_id_type=MESH))
    return cps


def _exchange_chips(srcs, scatter, name):
    n = len(srcs)

    def body(*refs):
        src_refs, land_refs = refs[:n], refs[n:2 * n]
        send_sems, recv_sems, local_sems = refs[2 * n:]
        x, y, _ = _place()
        me = 2 * x + y
        mine = [pltpu.make_async_copy(s.at[me] if scatter else s, l.at[me], local_sems.at[a])
                for a, (s, l) in enumerate(zip(src_refs, land_refs))]
        for cp in mine:
            cp.start()
        out = _peer_copies(src_refs, land_refs, send_sems, recv_sems, scatter, False)
        for cp in out:
            cp.start()
        for cp in _peer_copies(src_refs, land_refs, send_sems, recv_sems, scatter, True):
            cp.wait_recv()
        for cp in out:
            cp.wait_send()
        for cp in mine:
            cp.wait()

    anyspec = pl.BlockSpec(memory_space=pl.ANY)
    shapes = [SDS(s.shape if scatter else (N_CHIPS,) + s.shape, s.dtype) for s in srcs]
    return pl.pallas_call(
        body, name=name, out_shape=shapes, in_specs=[anyspec] * n, out_specs=[anyspec] * n,
        scratch_shapes=[pltpu.SemaphoreType.DMA((3 * n,)), pltpu.SemaphoreType.DMA((3 * n,)),
                        pltpu.SemaphoreType.DMA((n,))],
        compiler_params=pltpu.CompilerParams(vmem_limit_bytes=VMEM_LIMIT),
    )(*srcs)


_HBM = pl.BlockSpec(memory_space=pltpu.HBM)
_SEM = pl.BlockSpec(memory_space=pltpu.SEMAPHORE)
_EFFECT = pltpu.SideEffectType.DATAFLOW_SIDE_EFFECTING


def _exchange_begin(srcs, lands, scatter, name):
    srcs = tuple(srcs or ())
    n, ns = len(lands), len(srcs)
    nsem = n if scatter == SIBLING else 3 * n

    def body(*refs):
        src_refs = refs[:ns] if ns else (None,) * n
        land_refs = refs[ns:ns + n]
        send_sems, recv_sems = refs[ns + n:ns + n + 2]
        token = refs[-1]
        for cp in _peer_copies(src_refs, land_refs, send_sems, recv_sems, scatter, False):
            cp.start()
        token[...] = jnp.zeros_like(token)

    ops = (*srcs, *lands)
    res = pl.pallas_call(
        body, name=name,
        out_shape=(pltpu.SemaphoreType.DMA((nsem,)), pltpu.SemaphoreType.DMA((nsem,)),
                   *[pltpu.HBM(a.shape, a.dtype) for a in ops], SDS((SUB, LANE), F32)),
        in_specs=[_HBM] * len(ops), out_specs=(_SEM, _SEM, *[_HBM] * len(ops), pl.BlockSpec(memory_space=pltpu.VMEM)),
        input_output_aliases={i: 2 + i for i in range(len(ops))},
        compiler_params=pltpu.CompilerParams(has_side_effects=_EFFECT),
    )(*[pltpu.with_memory_space_constraint(a, pltpu.HBM) for a in ops])
    return res[0], res[1], res[2:2 + ns], res[2 + ns:2 + ns + n], res[-1]


def _exchange_end(handle, after, scatter, name, with_srcs=False):
    send_sems, recv_sems, srcs, lands, _ = handle
    n, ns = len(lands), len(srcs)

    def body(*refs):
        src_refs = refs[:ns] if ns else (None,) * n
        land_refs = refs[ns:ns + n]
        send_sems, recv_sems = refs[ns + n:ns + n + 2]
        for cp in _peer_copies(src_refs, land_refs, send_sems, recv_sems, scatter, True):
            cp.wait_send()
            cp.wait_recv()

    ops = (*srcs, *lands)
    res = pl.pallas_call(
        body, name=name,
        out_shape=tuple(pltpu.HBM(a.shape, a.dtype) for a in ops),
        in_specs=[_HBM] * len(ops) + [_SEM, _SEM, pl.BlockSpec(memory_space=pl.ANY)], out_specs=tuple([_HBM] * len(ops)),
        input_output_aliases={i: i for i in range(len(ops))},
        compiler_params=pltpu.CompilerParams(has_side_effects=_EFFECT),
    )(*ops, send_sems, recv_sems, after)
    return (list(res[:ns]), list(res[ns:])) if with_srcs else list(res[ns:])


def _own_slab(stack4, chip):
    idx = lax.broadcasted_iota(jnp.int32, (N_CHIPS,) + (1,) * (stack4.ndim - 1), 0)
    return jnp.where(idx == chip, stack4, jnp.zeros((), stack4.dtype))


def _swap_sibling(vs, name):
    n = len(vs)

    def body(*refs):
        in_refs, out_refs = refs[:n], refs[n:2 * n]
        send_sems, recv_sems = refs[2 * n:]
        x, y, c = _place()
        cps = [pltpu.make_async_remote_copy(
            src_ref=i, dst_ref=o, send_sem=send_sems.at[a], recv_sem=recv_sems.at[a],
            device_id=(x, y, 1 - c), device_id_type=MESH) for a, (i, o) in enumerate(zip(in_refs, out_refs))]
        for cp in cps:
            cp.start()
        for cp in cps:
            cp.wait()

    anyspec = pl.BlockSpec(memory_space=pl.ANY)
    return pl.pallas_call(
        body, name=name, out_shape=[SDS(v.shape, v.dtype) for v in vs], in_specs=[anyspec] * n, out_specs=[anyspec] * n,
        scratch_shapes=[pltpu.SemaphoreType.DMA((n,)), pltpu.SemaphoreType.DMA((n,))],
        compiler_params=pltpu.CompilerParams(vmem_limit_bytes=VMEM_LIMIT),
    )(*vs)


def _select(stacked, idx):
    out = stacked[0]
    for j in range(1, stacked.shape[0]):
        out = jnp.where(idx == j, stacked[j], out)
    return out


BIG = (
    ("w1_a", DFF // 4, D, False), ("w3_a", DFF // 4, D, False), ("w2_a", DFF // 4, D, False),
    ("w_in", D, 5632 // 4, True), ("w_conv_out", CW // 4, D, False), ("w_glu", SW // 4, SW, False),
    ("w_ssm_out", SW, D // 4, True), ("w_out", D // 4, D, False),
    ("w1_b", DFF // 4, D, False), ("w3_b", DFF // 4, D, False), ("w2_b", DFF // 4, D, False),
)
TRANSPOSED = frozenset(("w1_a", "w3_a", "w1_b", "w3_b"))
PACK_COLS = 1024


def _view(a, name):
    return jnp.transpose(a, (0, 2, 1)) if name in TRANSPOSED else a


def _full_from_stacked(st, split_cols):
    _, rows, cols = st.shape
    if split_cols:
        return st.transpose(1, 0, 2).reshape(rows, N_CHIPS * cols)
    return st.reshape(N_CHIPS * rows, cols)


def _stacked_from_full(full, rows, cols, split_cols):
    if split_cols:
        return full.reshape(rows, N_CHIPS, cols).transpose(1, 0, 2)
    return full.reshape(N_CHIPS, rows, cols)


def _cols_from_stacked(st, c0, c1):
    w = st.shape[2]
    parts = [st[j][:, max(c0, j * w) - j * w:min(c1, (j + 1) * w) - j * w]
             for j in range(N_CHIPS) if max(c0, j * w) < min(c1, (j + 1) * w)]
    return parts[0] if len(parts) == 1 else jnp.concatenate(parts, axis=1)


def _stack_cols(pieces, w):
    offs = [sum(p.shape[1] for p in pieces[:i]) for i in range(len(pieces))]
    shards = []
    for j in range(N_CHIPS):
        lo, hi = j * w, (j + 1) * w
        parts = [p[:, max(lo, o) - o:min(hi, o + p.shape[1]) - o]
                 for p, o in zip(pieces, offs) if max(lo, o) < min(hi, o + p.shape[1])]
        shards.append(parts[0] if len(parts) == 1 else jnp.concatenate(parts, axis=1))
    return jnp.stack(shards)


def _blockdiag(t):
    r = lax.broadcasted_iota(jnp.int32, (SW, GP), 0) // NH
    cidx = lax.broadcasted_iota(jnp.int32, (SW, GP), 1) // NP
    dense = jnp.where(r == cidx, jnp.tile(t, (NG, 1)), 0.0)
    ub, sb = SW // SSM_SUPER, GP // SSM_SUPER
    return jnp.concatenate([dense[s * ub:(s + 1) * ub, s * sb:(s + 1) * sb] for s in range(SSM_SUPER)], axis=0)


def _blockdiag_extract(acc):
    gs = NG // SSM_SUPER
    a = acc.reshape(NG, NH, gs, NP)
    sel = (lax.broadcasted_iota(jnp.int32, (NG, 1, gs, 1), 0) % gs) == lax.broadcasted_iota(jnp.int32, (NG, 1, gs, 1), 2)
    a = jnp.sum(jnp.where(sel, a, 0.0), axis=2)
    return a.transpose(1, 0, 2).reshape(NH, GP)


def _to_t(p):
    return p.transpose(2, 0, 1).reshape(NH, GP)


def _from_t(t):
    return t.reshape(NH, NG, NP).transpose(1, 2, 0)


def _c_to_t(p):
    return p.transpose(1, 0, 2).reshape(NH, GP)


def _c_from_t(t):
    return t.reshape(NH, NG, NP).transpose(1, 0, 2)


def _ffn_forward(x, g, sh, sc, gt, w1, w3, w2, tag):
    h = _norm_mod(x, g, sh, sc, f"{tag}_norm")
    if callable(w1):
        w1, w3 = w1(h)
    a, b, hid = _swiglu_up(h, w1, w3, f"{tag}_up")
    w2 = w2(hid) if callable(w2) else w2
    f, xo = _ffn_down(hid, w2, x, gt, f"{tag}_down")
    return xo, (x, h, a, b, hid, f), w2


def _ffn_backward(dxo, saved, g, sc, gt, w1, w3, w2, tag, emit=lambda key, gw: 0.0):
    x, h, a, b, hid, f = saved
    dfs, da, db, dgt = _ffn_bwd_hid(dxo, gt, f, a, b, w2, f"{tag}_bwd_hid")
    h2 = _flat(h)
    gw2 = _mm(_flat(hid), _flat(dfs), ta=True, out_dtype=BF16, name=f"{tag}_gw2")
    tok = emit("w2", gw2)
    gw1 = _mm(_flat(da), h2, ta=True, out_dtype=BF16, name=f"{tag}_gw1")
    tok = tok + emit("w1", gw1)
    gw3 = _mm(_flat(db), h2, ta=True, out_dtype=BF16, name=f"{tag}_gw3")
    tok = tok + emit("w3", gw3)
    dx, dsh, dsc, dg = _dh_norm_bwd([da, db], [w1, w3], x, g, sc + tok, dxo, f"{tag}_bwd_dh", transposed=True)
    return dx, (dsh, dsc, dgt, dg), (gw1, gw3, gw2)


def kernel(x, c, w_ada, b_ada, g_ffn1, w1_a, w3_a, w2_a, g_mix, w_in, conv_w, w_conv_out, a_re, a_im, b_re, b_im, c_re, c_im, log_dt, d_skip, w_glu, w_ssm_out, w_out, g_ffn2, w1_b, w3_b, w2_b, g_final, loss_target, m_w_ada, m_b_ada, m_g_ffn1, m_w1_a, m_w3_a, m_w2_a, m_g_mix, m_w_in, m_conv_w, m_w_conv_out, m_a_re, m_a_im, m_b_re, m_b_im, m_c_re, m_c_im, m_log_dt, m_d_skip, m_w_glu, m_w_ssm_out, m_w_out, m_g_ffn2, m_w1_b, m_w3_b, m_w2_b, m_g_final, v_w_ada, v_b_ada, v_g_ffn1, v_w1_a, v_w3_a, v_w2_a, v_g_mix, v_w_in, v_conv_w, v_w_conv_out, v_a_re, v_a_im, v_b_re, v_b_im, v_c_re, v_c_im, v_log_dt, v_d_skip, v_w_glu, v_w_ssm_out, v_w_out, v_g_ffn2, v_w1_b, v_w3_b, v_w2_b, v_g_final):
    args = dict(locals())
    names = ["w_ada", "b_ada", "g_ffn1", "w1_a", "w3_a", "w2_a", "g_mix", "w_in", "conv_w", "w_conv_out", "a_re",
             "a_im", "b_re", "b_im", "c_re", "c_im", "log_dt", "d_skip", "w_glu", "w_ssm_out", "w_out", "g_ffn2",
             "w1_b", "w3_b", "w2_b", "g_final"]
    bsz, seq, _ = x.shape
    mx, my, mc = _place()
    chip = 2 * mx + my
    dev = 4 * mx + 2 * my + mc

    groups = (BIG[:3], BIG[3:8], BIG[8:])
    wfull = {}

    def shards(grp):
        return [_view(args[n], n)[0] for n, _, _, _ in grp]

    def unpack_group(gathered, grp):
        for (n, _, _, split), st in zip(grp, gathered):
            wfull[n] = st if n == "w_in" else _full_from_stacked(st, split)

    up_grp, down_grp = groups[0][:2], groups[0][2:]

    nmod_shard = NMOD * D // N_CHIPS
    c_all = _all_gather8(c.reshape(SUB, -1), "gather_c").reshape(N_DEV * bsz, D)
    b_shard = _select(b_ada.reshape(N_CHIPS, 1, nmod_shard), chip)
    mod_shard = _ada_fwd(c_all, w_ada, b_shard)
    nb = N_DEV * bsz
    cw_pad = jnp.pad(conv_w[0], ((0, SUB - 3), (0, nmod_shard - CW // N_CHIPS)))
    mod_st = _exchange_chips([jnp.concatenate([mod_shard, cw_pad], axis=0)], False, "gather_mod")[0]
    mod_all = mod_st[:, :nb].transpose(1, 0, 2).reshape(N_DEV, bsz, NMOD * D)
    mod = _select(mod_all, dev)

    def gather_begin(raw, name):
        lands = [_own_slab(jnp.broadcast_to(a.astype(BF16)[None], (N_CHIPS,) + a.shape), chip) for a in raw]
        return _exchange_begin(None, lands, False, name)

    up_raw, mod = lax.optimization_barrier((shards(up_grp), mod))
    up_handle = gather_begin(up_raw, "gather_w_ffn1_up_start")
    (down_raw, mix_raw, ffn2_raw), up_token = lax.optimization_barrier(
        ((shards(down_grp), shards(groups[1]), shards(groups[2])), up_handle[4][0:1, 0:1]))
    down_handle = gather_begin(down_raw, "gather_w_ffn1_down_start")
    mix_handle = gather_begin(mix_raw, "gather_w_mix_start")
    ffn2_handle = gather_begin(ffn2_raw, "gather_w_ffn2_start")
    start_tokens = up_token + down_handle[4][0:1, 0:1] + mix_handle[4][0:1, 0:1] + ffn2_handle[4][0:1, 0:1]

    sh1, sc1, gt1, sh2, sc2, gt2, sh3, sc3, gt3 = [mod[:, None, j * D:(j + 1) * D] for j in range(NMOD)]
    convw = mod_st[:, nb:nb + 3, :CW // N_CHIPS].transpose(1, 0, 2).reshape(3, CW)
    convw8 = jnp.pad(convw, ((0, SUB - 3), (0, 0)))

    are, aim = a_re.reshape(1, GP), a_im.reshape(1, GP)
    ldt = jnp.broadcast_to(log_dt.reshape(NG, 1), (NG, NP)).reshape(1, GP)
    bre_t, bim_t = _to_t(b_re[0]), _to_t(b_im[0])
    abr, abi, bbr_t, bbi_t = _ssm_disc(are, aim, ldt, bre_t, bim_t)
    wb = jnp.concatenate([_blockdiag(bbr_t), _blockdiag(bbi_t)], axis=1).astype(BF16)
    wct = jnp.concatenate([_blockdiag(_c_to_t(c_re[0])), _blockdiag(-_c_to_t(c_im[0]))], axis=1).astype(BF16)
    ar8 = jnp.broadcast_to(abr, (SUB, GP))
    ai8 = jnp.broadcast_to(abi, (SUB, GP))

    def late_w2a(hid):
        unpack_group(_exchange_end(down_handle, hid, False, "gather_w_ffn1_down_wait"), down_grp)
        return wfull["w2_a"]

    def late_w13a(h):
        unpack_group(_exchange_end(up_handle, h, False, "gather_w_ffn1_up_wait"), up_grp)
        return wfull["w1_a"], wfull["w3_a"]

    x1, ffn1_saved, _ = _ffn_forward(x, g_ffn1 + start_tokens, sh1, sc1, gt1, late_w13a, None, late_w2a, "ffn1")
    unpack_group(_exchange_end(mix_handle, x1, False, "gather_w_mix_wait"), groups[1])

    h2 = _norm_mod(x1, g_mix, sh2, sc2, "mix_norm")
    h2f = _flat(h2)
    win_st = wfull["w_in"]
    win1 = _cols_from_stacked(win_st, 0, 3 * CW)
    winu = _cols_from_stacked(win_st, 3 * CW, 3 * CW + SW)
    win3 = _cols_from_stacked(win_st, 3 * CW + SW, 3 * CW + SW + 2 * D)
    p1 = _mm(h2f, win1, out_dtype=BF16, name="mix_in1").reshape(bsz, seq, 3 * CW)
    u = _mm(h2f, winu, name="mix_inu").reshape(bsz, seq, SW)
    p3 = _mm(h2f, win3, out_dtype=BF16, name="mix_in3").reshape(bsz, seq, 2 * D)

    ya_in = _conv_fwd(p1, convw8, "conv_fwd")
    ya = _mm(_flat(ya_in), wfull["w_conv_out"], out_dtype=BF16, name="conv_out").reshape(bsz, seq, D)

    xs, ys = _ssm_fwd(u, wb, wct, ar8, ai8, "ssm_fwd")
    s0, z, s1, s2, yb = _ssm_post(ys, u, d_skip, wfull["w_glu"], wfull["w_ssm_out"], "ssm_post")

    merged, mix, x2 = _merge_out(ya, yb, p3, wfull["w_out"], x1, gt2, "merge_out")
    unpack_group(_exchange_end(ffn2_handle, x2, False, "gather_w_ffn2_wait"), groups[2])
    x3, ffn2_saved, _ = _ffn_forward(x2, g_ffn2, sh3, sc3, gt3, wfull["w1_b"], wfull["w3_b"], wfull["w2_b"], "ffn2")

    dx3, lossvec, dgfin = _final_loss(x3, g_final.reshape(1, D), loss_target, "final_loss")
    loss = lax.psum(jnp.sum(lossvec), ("x", "y", "c"))

    gfull = {}
    dx2, (dsh3, dsc3, dgt3, dg3), (gfull["w1_b"], gfull["w3_b"], gfull["w2_b"]) = _ffn_backward(
        dx3, ffn2_saved, g_ffn2, sc3, gt3, wfull["w1_b"], wfull["w3_b"], wfull["w2_b"], "ffn2")

    def stack_group(grp):
        return [gfull[n] if gfull[n].ndim == 3 else _stacked_from_full(gfull[n], rows, cols, split)
                for n, rows, cols, split in grp]

    st_ffn2 = stack_group(groups[2])
    h_ffn2 = _exchange_begin(st_ffn2, [_own_slab(s, chip) for s in st_ffn2], True, "scatter_ffn2_start")

    dmix, dya, dyb, dp3, dgt2 = _merge_bwd(dx2, gt2 + h_ffn2[4][0, 0], mix, ya, yb, p3, wfull["w_out"], "merge_bwd")
    gfull["w_out"] = _mm(_flat(merged), _flat(dmix), ta=True, out_dtype=BF16, name="gw_out")
    dp1, dconvw8 = _conv_bwd(dya, wfull["w_conv_out"], p1, convw8, "conv_bwd")
    gfull["w_conv_out"] = _mm(_flat(ya_in), _flat(dya), ta=True, out_dtype=BF16, name="gw_conv_out")
    ds0, dz, ddskip = _ssm_post_bwd(dyb, s0, z, u, d_skip, wfull["w_glu"], wfull["w_ssm_out"], "ssm_post_bwd")
    gfull["w_ssm_out"] = _mm(_flat(s2), _flat(dyb), ta=True, out_dtype=BF16, name="gw_ssm_out")
    gfull["w_glu"] = _mm(_flat(s1), _flat(dz), ta=True, out_dtype=BF16, name="gw_glu")
    du, dwb, dwct, dar8, dai8 = _ssm_bwd(ds0, u, xs, wb, wct, ar8, ai8, d_skip, "ssm_bwd")
    dx1, dsh2, dsc2, dgmix = _dh_norm_bwd([dp1, du, dp3], [win1, winu, win3], x1, g_mix, sc2, dx2, "mix_bwd_dh")
    gfull["w_in"] = _stack_cols([
        _mm(h2f, _flat(dp1), ta=True, out_dtype=BF16, name="gw_in1"), _mm(h2f, _flat(du), ta=True, out_dtype=BF16, name="gw_inu"),
        _mm(h2f, _flat(dp3), ta=True, out_dtype=BF16, name="gw_in3")], BIG[3][2])

    st_mix = stack_group(groups[1])
    h_mix = _exchange_begin(st_mix, [_own_slab(s, chip) for s in st_mix], True, "scatter_mix_start")

    def swap_begin(recv, name):
        return _exchange_begin(recv, [lax.empty(v.shape, v.dtype) for v in recv], SIBLING, name)

    recv_ffn2 = _exchange_end(h_ffn2, dx1, True, "scatter_ffn2_wait")
    sw_ffn2 = swap_begin(recv_ffn2, "swap_ffn2_start")

    ffn1_names = {"w1": BIG[0], "w3": BIG[1], "w2": BIG[2]}
    ffn1_handles = {}

    def emit_ffn1(key, gw):
        n, rows, cols, split = ffn1_names[key]
        st = _stacked_from_full(gw, rows, cols, split)
        ffn1_handles[n] = _exchange_begin([st], [_own_slab(st, chip)], True, f"scatter_ffn1_{key}_start")
        return ffn1_handles[n][4][0, 0]

    grad_x, (dsh1, dsc1, dgt1, dg1), _ = _ffn_backward(
        dx1, ffn1_saved, g_ffn1, sc1, gt1 + (h_mix[4][0, 0] + sw_ffn2[4][0, 0]), wfull["w1_a"], wfull["w3_a"],
        wfull["w2_a"], "ffn1", emit=emit_ffn1)
    recv_mix = _exchange_end(h_mix, grad_x, True, "scatter_mix_wait")
    sw_mix = swap_begin(recv_mix, "swap_mix_start")
    dg1 = dg1 + sw_mix[4][0, 0]

    sbw = GP // SSM_SUPER
    d_are, d_aim, d_ldt, d_bre_t, d_bim_t = _ssm_disc_bwd(
        are, aim, ldt, bre_t, bim_t, jnp.sum(dar8, axis=0, keepdims=True), jnp.sum(dai8, axis=0, keepdims=True),
        _blockdiag_extract(dwb[:, :sbw]), _blockdiag_extract(dwb[:, sbw:]))
    d_cre = _c_from_t(_blockdiag_extract(dwct[:, :sbw]))
    d_cim = -_c_from_t(_blockdiag_extract(dwct[:, sbw:]))

    small_parts = [dg1, dgmix, dg3, dgfin, dconvw8[:3], d_are, d_aim, _from_t(d_bre_t), _from_t(d_bim_t), d_cre, d_cim,
                   jnp.sum(d_ldt.reshape(NG, NP), axis=1), ddskip]
    small_sizes = [int(p.size) for p in small_parts]
    n_small = sum(small_sizes)
    n_small_pad = -(-n_small // (SUB * PACK_COLS)) * (SUB * PACK_COLS)
    dmod = jnp.concatenate([dsh1, dsc1, dgt1, dsh2, dsc2, dgt2, dsh3, dsc3, dgt3], axis=2).reshape(bsz * NMOD * D)
    flat = jnp.concatenate([p.reshape(-1) for p in small_parts] + [jnp.zeros((n_small_pad - n_small,), F32), dmod])
    grads, deltas, new_m, new_v = {}, {}, {}, {}

    def adamw_group(grp, recv_own, recv_sib, token):
        for (n, _, _, _), r_own, r_sib in zip(grp, recv_own, recv_sib):
            wmv, _ = lax.optimization_barrier(
                ((_view(args[n], n), _view(args["m_" + n], n), _view(args["v_" + n], n)), token))
            res = _adamw_big(*wmv, r_own, r_sib, f"adamw_{n}")
            grads[n], deltas[n], new_m[n], new_v[n] = [_view(r, n) for r in res]
        return deltas[grp[-1][0]]

    flat2d = flat.reshape(-1, PACK_COLS)
    h_small = _exchange_begin(None, [_own_slab(jnp.broadcast_to(flat2d[None], (N_CHIPS,) + flat2d.shape), chip)],
                              False, "gather_small_chips_start")
    recv_ffn2, sib_ffn2 = _exchange_end(sw_ffn2, flat2d, SIBLING, "swap_ffn2_wait", with_srcs=True)
    done = adamw_group(groups[2], recv_ffn2, sib_ffn2, h_small[4])
    recv_mix, sib_mix = _exchange_end(sw_mix, done, SIBLING, "swap_mix_wait", with_srcs=True)
    done = adamw_group(groups[1], recv_mix, sib_mix, h_small[4])
    own_c = _exchange_end(h_small, done, False, "gather_small_chips_wait")[0]
    sib_c = _swap_sibling([own_c], "gather_small_sib")[0]

    recv_ffn1 = [_exchange_end(ffn1_handles[n], sib_c, True, f"scatter_ffn1_{n}_wait")[0] for n, _, _, _ in groups[0]]
    sw_ffn1 = swap_begin(recv_ffn1, "swap_ffn1_start")
    (own_c, sib_c), _ = lax.optimization_barrier(((own_c, sib_c), sw_ffn1[4]))
    core0 = jnp.where(mc == 0, own_c, sib_c)
    core1 = jnp.where(mc == 0, sib_c, own_c)
    allg = jnp.stack([core0, core1], axis=1).reshape(N_DEV, -1)
    small = _sum_slabs(allg[:, :n_small_pad].reshape(N_DEV, -1, PACK_COLS), "sum_small").reshape(-1)
    sg, o = [], 0
    for p, sz in zip(small_parts, small_sizes):
        sg.append(small[o:o + sz].reshape(p.shape))
        o += sz
    (g_g1, g_gmix, g_g3, g_gfin, g_convw, g_are, g_aim, g_bre, g_bim, g_cre, g_cim, g_ldt, g_dskip) = sg

    dmod_all = allg[:, n_small_pad:].reshape(nb, NMOD * D)
    dmod_shard = _select(dmod_all.reshape(nb, N_CHIPS, nmod_shard).transpose(1, 0, 2), chip)
    g_wada, g_bada = _ada_bwd(c_all, dmod_shard, dmod_all)

    grads["w_ada"] = g_wada[None]
    deltas["w_ada"], new_m["w_ada"], new_v["w_ada"] = _adamw_rows(w_ada, m_w_ada, v_w_ada, g_wada, "adamw_w_ada")

    g_convw_shard = _select(g_convw.reshape(3, N_CHIPS, CW // N_CHIPS).transpose(1, 0, 2), chip)
    small_g = {"b_ada": g_bada, "g_ffn1": g_g1, "g_mix": g_gmix, "g_ffn2": g_g3, "g_final": g_gfin,
               "conv_w": g_convw_shard, "a_re": g_are, "a_im": g_aim, "b_re": g_bre, "b_im": g_bim,
               "c_re": g_cre, "c_im": g_cim, "log_dt": g_ldt, "d_skip": g_dskip}
    small_names = list(small_g)
    sizes = [int(args[n].size) for n in small_names]
    tot = sum(sizes)
    tot_pad = -(-tot // (SUB * PACK_COLS)) * (SUB * PACK_COLS)

    def pack(get):
        return jnp.concatenate([get(n).reshape(-1) for n in small_names] + [jnp.zeros((tot_pad - tot,), F32)]).reshape(
            -1, PACK_COLS)

    res = _adamw_plain(pack(lambda n: args[n]), pack(lambda n: args["m_" + n]), pack(lambda n: args["v_" + n]),
                       pack(lambda n: small_g[n]), "adamw_small")
    o = 0
    for n, sz in zip(small_names, sizes):
        shp = args[n].shape
        grads[n] = small_g[n].reshape(shp)
        deltas[n], new_m[n], new_v[n] = [r.reshape(-1)[o:o + sz].reshape(shp) for r in res]
        o += sz

    recv_ffn1, sib_ffn1 = _exchange_end(sw_ffn1, res[0], SIBLING, "swap_ffn1_wait", with_srcs=True)
    adamw_group(groups[0], recv_ffn1, sib_ffn1, jnp.zeros((), F32))

    return (loss, grad_x, *[grads[n] for n in names], *[deltas[n] for n in names],
            *[new_m[n] for n in names], *[new_v[n] for n in names])
```

```python
import math

import jax
import jax.numpy as jnp
from jax import lax
from jax.experimental import pallas as pl
from jax.experimental.pallas import tpu as pltpu

F32 = jnp.float32
BF16 = jnp.bfloat16
SDS = jax.ShapeDtypeStruct
MESH = pl.DeviceIdType.MESH

D = 1024
DFF = 2816
CW = 1024
SW = 512
NG, NP, NH = 32, 64, 16
GP = NG * NP
NMOD = 9
EPS = 1e-6
N_CHIPS = 4
N_DEV = 8
SUB = 8
LANE = 128
SSM_SUPER = 4
VMEM_LIMIT = 50 * 1024 * 1024

LR, B1, B2, AEPS, WD, STEP = 0.001, 0.9, 0.999, 1e-08, 0.01, 10
BC1 = 1.0 - B1 ** STEP
BC2 = 1.0 - B2 ** STEP


def _cp(*sem):
    return pltpu.CompilerParams(dimension_semantics=sem or None, vmem_limit_bytes=VMEM_LIMIT)


def _pick_tile(n, cands):
    for t in cands:
        if t <= n and n % t == 0:
            return t
    return n


def _dot(a, b):
    return lax.dot_general(a, b, (((1,), (0,)), ((), ())), preferred_element_type=F32)


def _dot_nt(a, b):
    return lax.dot_general(a, b, (((1,), (1,)), ((), ())), preferred_element_type=F32)


def _dot_tn(a, b):
    return lax.dot_general(a, b, (((0,), (0,)), ((), ())), preferred_element_type=F32)


def _row(tm, width, col=0):
    return pl.BlockSpec((1, tm, width), lambda b, i, *_: (b, i, col))


def _seqvec(width):
    return pl.BlockSpec((1, 1, width), lambda b, *_: (b, 0, 0))


def _full2(shape):
    return pl.BlockSpec(shape, lambda *_: (0, 0))


def _sigmoid(x):
    return jax.nn.sigmoid(x)


def _mm(a, b, *, ta=False, tb=False, out_dtype=F32, name):
    if ta:
        kdim, m = a.shape
    else:
        m, kdim = a.shape
    n = b.shape[0] if tb else b.shape[1]
    tm = _pick_tile(m, (1408, 1024, 512, 256, 128))
    tn = _pick_tile(n, (1408, 1024, 512, 256, 128))
    tk = _pick_tile(kdim, (1024, 512, 256, 128))
    nk = kdim // tk

    def body(a_ref, b_ref, o_ref, acc_ref):
        k = pl.program_id(2)

        @pl.when(k == 0)
        def _():
            acc_ref[...] = jnp.zeros_like(acc_ref)

        av = a_ref[...].astype(BF16)
        bv = b_ref[...].astype(BF16)
        dn = (((0 if ta else 1,), (1 if tb else 0,)), ((), ()))
        acc_ref[...] += lax.dot_general(av, bv, dn, preferred_element_type=F32)

        @pl.when(k == nk - 1)
        def _():
            o_ref[...] = acc_ref[...].astype(out_dtype)

    a_spec = pl.BlockSpec((tk, tm), lambda i, j, k: (k, i)) if ta else pl.BlockSpec((tm, tk), lambda i, j, k: (i, k))
    b_spec = pl.BlockSpec((tn, tk), lambda i, j, k: (j, k)) if tb else pl.BlockSpec((tk, tn), lambda i, j, k: (k, j))
    return pl.pallas_call(
        body, name=name, grid=(m // tm, n // tn, nk),
        in_specs=[a_spec, b_spec],
        out_specs=pl.BlockSpec((tm, tn), lambda i, j, k: (i, j)),
        out_shape=SDS((m, n), out_dtype),
        scratch_shapes=[pltpu.VMEM((tm, tn), F32)],
        compiler_params=_cp("parallel", "parallel", "arbitrary"),
    )(a, b)


def _flat(a):
    return a.reshape(-1, a.shape[-1])


def _norm_mod(x, g, sh, sc, name):
    bsz, seq, dm = x.shape
    tm = _pick_tile(seq, (512, 256, 128))

    def body(x_ref, g_ref, sh_ref, sc_ref, o_ref):
        xf = x_ref[0]
        r = lax.rsqrt(jnp.mean(xf * xf, axis=-1, keepdims=True) + EPS)
        hn = xf * r * g_ref[...]
        o_ref[0] = (hn * (1.0 + sc_ref[0]) + sh_ref[0]).astype(BF16)

    return pl.pallas_call(
        body, name=name, grid=(bsz, seq // tm),
        in_specs=[_row(tm, dm), _full2((1, dm)), _seqvec(dm), _seqvec(dm)],
        out_specs=_row(tm, dm), out_shape=SDS((bsz, seq, dm), BF16),
        compiler_params=_cp("parallel", "parallel"),
    )(x, g, sh, sc)


def _swiglu_up(h, w1, w3, name):
    bsz, seq, dm = h.shape
    nf = w1.shape[0]
    tm = _pick_tile(seq, (512, 256, 128))
    tn = _pick_tile(nf, (1408, 512, 256, 128))

    def body(h_ref, w1_ref, w3_ref, a_ref, b_ref, hid_ref):
        hv = h_ref[0]
        a = _dot_nt(hv, w1_ref[...])
        b = _dot_nt(hv, w3_ref[...])
        sg = _sigmoid(a)
        sa = a * sg
        a_ref[0] = (b * (sg * (1.0 + a * (1.0 - sg)))).astype(BF16)
        b_ref[0] = sa.astype(BF16)
        hid_ref[0] = (sa * b).astype(BF16)

    wspec = pl.BlockSpec((tn, dm), lambda n, b, i: (n, 0))
    ospec = pl.BlockSpec((1, tm, tn), lambda n, b, i: (b, i, n))
    shp = SDS((bsz, seq, nf), BF16)
    return pl.pallas_call(
        body, name=name, grid=(nf // tn, bsz, seq // tm),
        in_specs=[pl.BlockSpec((1, tm, dm), lambda n, b, i: (b, i, 0)), wspec, wspec],
        out_specs=[ospec, ospec, ospec], out_shape=[shp, shp, shp],
        compiler_params=_cp("parallel", "parallel", "parallel"),
    )(h, w1, w3)


def _ffn_down(hid, w2, x, gt, name):
    bsz, seq, nf = hid.shape
    dm = w2.shape[1]
    tm = _pick_tile(seq, (512, 256, 128))

    def body(hid_ref, w2_ref, x_ref, gt_ref, f_ref, xo_ref):
        f = _dot(hid_ref[0], w2_ref[...])
        f_ref[0] = f.astype(BF16)
        xo_ref[0] = x_ref[0] + 0.5 * gt_ref[0] * f

    shp = SDS((bsz, seq, dm), F32)
    return pl.pallas_call(
        body, name=name, grid=(bsz, seq // tm),
        in_specs=[_row(tm, nf), _full2((nf, dm)), _row(tm, dm), _seqvec(dm)],
        out_specs=[_row(tm, dm), _row(tm, dm)], out_shape=[SDS((bsz, seq, dm), BF16), shp],
        compiler_params=_cp("parallel", "parallel"),
    )(hid, w2, x, gt)


def _ffn_bwd_hid(dxo, gt, f, a, b, w2, name):
    bsz, seq, dm = dxo.shape
    nf = a.shape[2]
    tm = _pick_tile(seq, (512, 256, 128))
    tn = _pick_tile(nf, (1408, 512, 256, 128))

    def body(dxo_ref, gt_ref, f_ref, a_ref, b_ref, w2_ref, dfs_ref, da_ref, db_ref, dgt_ref):
        i = pl.program_id(1)
        n = pl.program_id(2)

        @pl.when(n == 0)
        def _():
            dxo = dxo_ref[0]
            dfs_ref[0] = (0.5 * gt_ref[0] * dxo).astype(BF16)
            part = jnp.sum(0.5 * dxo * f_ref[0].astype(F32), axis=0, keepdims=True)

            @pl.when(i == 0)
            def _():
                dgt_ref[0] = part

            @pl.when(i > 0)
            def _():
                dgt_ref[0] += part

        dhid = _dot_nt(dfs_ref[0], w2_ref[pl.ds(pl.multiple_of(n * tn, tn), tn), :])
        dh16 = dhid.astype(BF16)
        da_ref[0] = dh16 * a_ref[0]
        db_ref[0] = dh16 * b_ref[0]

    hspec = pl.BlockSpec((1, tm, tn), lambda b, i, n: (b, i, n))
    return pl.pallas_call(
        body, name=name, grid=(bsz, seq // tm, nf // tn),
        in_specs=[_row(tm, dm), _seqvec(dm), _row(tm, dm), hspec, hspec, _full2((nf, dm))],
        out_specs=[_row(tm, dm), hspec, hspec, _seqvec(dm)],
        out_shape=[SDS((bsz, seq, dm), BF16), SDS((bsz, seq, nf), BF16), SDS((bsz, seq, nf), BF16),
                   SDS((bsz, 1, dm), F32)],
        compiler_params=_cp("arbitrary", "arbitrary", "arbitrary"),
    )(dxo, gt, f, a, b, w2)


def _dh_norm_bwd(pieces, weights, x, g, sc, dxo, name, transposed=False):
    bsz, seq, dm = x.shape
    tm = _pick_tile(seq, (512, 256, 128))
    npc = len(pieces)
    dot = _dot if transposed else _dot_nt

    def body(*refs):
        p_refs = refs[:npc]
        w_hbm = refs[npc:2 * npc]
        x_ref, g_ref, sc_ref, dxo_ref, dx_ref, dsh_ref, dsc_ref, dg_ref = refs[2 * npc:2 * npc + 8]
        w_refs = refs[2 * npc + 8:]
        b = pl.program_id(0)
        i = pl.program_id(1)

        @pl.when((i == 0) & (b == 0))
        def _():
            for src, dst in zip(w_hbm, w_refs):
                pltpu.sync_copy(src, dst)

        dh = dot(p_refs[0][0], w_refs[0][...])
        for j in range(1, npc):
            dh = dh + dot(p_refs[j][0], w_refs[j][...])
        xf = x_ref[0]
        gv = g_ref[...]
        r = lax.rsqrt(jnp.mean(xf * xf, axis=-1, keepdims=True) + EPS)
        xhat = xf * r
        dhn = dh * (1.0 + sc_ref[0])
        p_sh = jnp.sum(dh, axis=0, keepdims=True)
        p_sc = jnp.sum(dh * (xhat * gv), axis=0, keepdims=True)
        p_g = jnp.sum(dhn * xhat, axis=0, keepdims=True)
        dxh = dhn * gv
        dx_ref[0] = dxo_ref[0] + r * (dxh - xhat * jnp.mean(dxh * xhat, axis=-1, keepdims=True))

        @pl.when(i == 0)
        def _():
            dsh_ref[0] = p_sh
            dsc_ref[0] = p_sc

        @pl.when(i > 0)
        def _():
            dsh_ref[0] += p_sh
            dsc_ref[0] += p_sc

        @pl.when((i == 0) & (b == 0))
        def _():
            dg_ref[...] = p_g

        @pl.when((i > 0) | (b > 0))
        def _():
            dg_ref[...] += p_g

    return pl.pallas_call(
        body, name=name, grid=(bsz, seq // tm),
        in_specs=[_row(tm, p.shape[2]) for p in pieces] + [pl.BlockSpec(memory_space=pl.ANY)] * npc + [
            _row(tm, dm), _full2((1, dm)), _seqvec(dm), _row(tm, dm)],
        out_specs=[_row(tm, dm), _seqvec(dm), _seqvec(dm), _full2((1, dm))],
        out_shape=[SDS((bsz, seq, dm), F32), SDS((bsz, 1, dm), F32), SDS((bsz, 1, dm), F32), SDS((1, dm), F32)],
        scratch_shapes=[pltpu.VMEM(w.shape, w.dtype) for w in weights],
        compiler_params=_cp("arbitrary", "arbitrary"),
    )(*pieces, *weights, x, g, sc, dxo)


HALO = 16


def _conv_core(gc, v, gch, vh, w, first):
    cv = gc * v
    halo = jnp.where(first, 0.0, gch * vh)
    ext = jnp.concatenate([halo, cv], axis=0)
    cv1 = pltpu.roll(ext, 1, 0)[HALO:]
    cv2 = pltpu.roll(ext, 2, 0)[HALO:]
    conv = w[0:1] * cv2 + w[1:2] * cv1 + w[2:3] * cv
    return cv, cv1, cv2, conv


def _prev_halo(tm, col):
    return pl.BlockSpec((1, HALO, CW), lambda b, i, *_: (b, jnp.maximum(i * (tm // HALO) - 1, 0), col))


def _next_halo(tm, seq, col):
    return pl.BlockSpec((1, HALO, CW), lambda b, i, *_: (b, jnp.minimum((i + 1) * (tm // HALO), seq // HALO - 1), col))


def _conv_fwd(p1, convw8, name):
    bsz, seq, _ = p1.shape
    tm = _pick_tile(seq, (512, 256, 128))

    def body(gb_ref, gc_ref, v_ref, gch_ref, vh_ref, w_ref, o_ref):
        first = pl.program_id(1) == 0
        _, _, _, conv = _conv_core(gc_ref[0].astype(F32), v_ref[0].astype(F32), gch_ref[0].astype(F32),
                                   vh_ref[0].astype(F32), w_ref[...], first)
        o_ref[0] = (gb_ref[0].astype(F32) * conv).astype(BF16)

    return pl.pallas_call(
        body, name=name, grid=(bsz, seq // tm),
        in_specs=[_row(tm, CW, 0), _row(tm, CW, 1), _row(tm, CW, 2), _prev_halo(tm, 1), _prev_halo(tm, 2),
                  _full2((8, CW))],
        out_specs=_row(tm, CW), out_shape=SDS((bsz, seq, CW), BF16),
        compiler_params=_cp("parallel", "parallel"),
    )(p1, p1, p1, p1, p1, convw8)


def _conv_bwd(dya, wco, p1, convw8, name):
    bsz, seq, _ = p1.shape
    tm = _pick_tile(seq, (512, 256, 128))
    nt = seq // tm
    ext_rows = tm + HALO

    def body(dya_ref, dyan_ref, wco_ref, gb_ref, gbn_ref, gc_ref, v_ref, gch_ref, vh_ref, w_ref, dp_ref, dw_ref):
        b = pl.program_id(0)
        i = pl.program_id(1)
        w = w_ref[...]
        gc = gc_ref[0].astype(F32)
        vv = v_ref[0].astype(F32)
        cv, cv1, cv2, conv = _conv_core(gc, vv, gch_ref[0].astype(F32), vh_ref[0].astype(F32), w, i == 0)
        dya_ext = jnp.concatenate([dya_ref[0], dyan_ref[0]], axis=0)
        dyain_ext = _dot_nt(dya_ext, wco_ref[...])
        gb_ext = jnp.concatenate([gb_ref[0], gbn_ref[0]], axis=0).astype(F32)
        rows = lax.broadcasted_iota(jnp.int32, (ext_rows, 1), 0)
        dconv_ext = jnp.where((rows < tm) | (i < nt - 1), dyain_ext * gb_ext, 0.0)
        dconv = dconv_ext[:tm]
        dconv1 = pltpu.roll(dconv_ext, ext_rows - 1, 0)[:tm]
        dconv2 = pltpu.roll(dconv_ext, ext_rows - 2, 0)[:tm]
        dcv = w[2:3] * dconv + w[1:2] * dconv1 + w[0:1] * dconv2
        dp_ref[0, :, 0:CW] = (dyain_ext[:tm] * conv).astype(BF16)
        dp_ref[0, :, CW:2 * CW] = (dcv * vv).astype(BF16)
        dp_ref[0, :, 2 * CW:3 * CW] = (dcv * gc).astype(BF16)
        g0 = jnp.sum(dconv * cv2, axis=0, keepdims=True)
        g1 = jnp.sum(dconv * cv1, axis=0, keepdims=True)
        g2 = jnp.sum(dconv * cv, axis=0, keepdims=True)
        upd = jnp.concatenate([g0, g1, g2, jnp.zeros((5, CW), F32)], axis=0)

        @pl.when((i == 0) & (b == 0))
        def _():
            dw_ref[...] = upd

        @pl.when((i > 0) | (b > 0))
        def _():
            dw_ref[...] += upd

    return pl.pallas_call(
        body, name=name, grid=(bsz, seq // tm),
        in_specs=[_row(tm, CW), _next_halo(tm, seq, 0), _full2((CW, D)),
                  _row(tm, CW, 0), _next_halo(tm, seq, 0), _row(tm, CW, 1), _row(tm, CW, 2),
                  _prev_halo(tm, 1), _prev_halo(tm, 2), _full2((8, CW))],
        out_specs=[_row(tm, 3 * CW), _full2((8, CW))],
        out_shape=[SDS((bsz, seq, 3 * CW), BF16), SDS((8, CW), F32)],
        compiler_params=_cp("arbitrary", "arbitrary"),
    )(dya, dya, wco, p1, p1, p1, p1, p1, p1, convw8)


def _disc(are, aim, ldt, bre, bim):
    dt = jnp.exp(ldt)
    mag = jnp.exp(are * dt)
    ang = aim * dt
    abr = mag * jnp.cos(ang)
    abi = mag * jnp.sin(ang)
    nr = abr - 1.0
    den = are * are + aim * aim
    cr = (nr * are + abi * aim) / den
    ci = (abi * are - nr * aim) / den
    return abr, abi, cr * bre - ci * bim, cr * bim + ci * bre


def _ssm_disc(are, aim, ldt, bre_t, bim_t):
    def body(are_ref, aim_ref, ldt_ref, bre_ref, bim_ref, abr_ref, abi_ref, bbr_ref, bbi_ref):
        abr, abi, bbr, bbi = _disc(are_ref[...], aim_ref[...], ldt_ref[...], bre_ref[...], bim_ref[...])
        abr_ref[...] = abr
        abi_ref[...] = abi
        bbr_ref[...] = bbr
        bbi_ref[...] = bbi

    v1, vh = SDS((1, GP), F32), SDS((NH, GP), F32)
    return pl.pallas_call(body, name="ssm_disc", out_shape=[v1, v1, vh, vh], compiler_params=_cp())(
        are, aim, ldt, bre_t, bim_t)


def _ssm_disc_bwd(are, aim, ldt, bre_t, bim_t, dabr, dabi, dbbr, dbbi):
    def body(are_ref, aim_ref, ldt_ref, bre_ref, bim_ref, g0, g1, g2, g3, o0, o1, o2, o3, o4):
        prim = (are_ref[...], aim_ref[...], ldt_ref[...], bre_ref[...], bim_ref[...])
        _, vjp = jax.vjp(_disc, *prim)
        d_are, d_aim, d_ldt, d_bre, d_bim = vjp((g0[...], g1[...], g2[...], g3[...]))
        o0[...] = d_are
        o1[...] = d_aim
        o2[...] = d_ldt
        o3[...] = d_bre
        o4[...] = d_bim

    v1, vh = SDS((1, GP), F32), SDS((NH, GP), F32)
    return pl.pallas_call(body, name="ssm_disc_bwd", out_shape=[v1, v1, v1, vh, vh], compiler_params=_cp())(
        are, aim, ldt, bre_t, bim_t, dabr, dabi, dbbr, dbbi)


def _scan_chunk(buf_ref, nt, bsz, ar, ai, init_r, init_i, reverse):
    nsub = SUB // bsz
    row = lax.broadcasted_iota(jnp.int32, (SUB, GP), 0)
    shift = ((SUB - bsz) if reverse else bsz) % SUB
    order = list(range(nsub - 1, -1, -1)) if reverse else list(range(nsub))

    def step(j, carry):
        pr, pi = carry
        jj = (nt - 1 - j) if reverse else j
        off = pl.multiple_of(jj * SUB, SUB)
        br = buf_ref[pl.ds(off, SUB), 0:GP]
        bi = buf_ref[pl.ds(off, SUB), GP:2 * GP]
        nr, ni = pr, pi
        for s in order:
            sr, si = nr, ni
            if shift:
                sr = pltpu.roll(sr, shift, 0)
                si = pltpu.roll(si, shift, 0)
            cr = ar * sr - ai * si + br
            ci = ar * si + ai * sr + bi
            if nsub == 1:
                nr, ni = cr, ci
            else:
                m = (row >= s * bsz) & (row < (s + 1) * bsz)
                nr = jnp.where(m, cr, nr)
                ni = jnp.where(m, ci, ni)
        buf_ref[pl.ds(off, SUB), 0:GP] = nr
        buf_ref[pl.ds(off, SUB), GP:2 * GP] = ni
        return nr, ni

    return lax.fori_loop(0, nt, step, (init_r, init_i))


def _ssm_chunk_rows(total_rows, bsz):
    return min(total_rows, 64 * bsz)


def _interleave(src_ref, tmp_ref, bsz, steps):
    nl = tmp_ref.shape[0]
    for b in range(bsz):
        for j in range(nl):
            tmp_ref.at[j][pl.ds(b, steps, stride=bsz), :] = src_ref[b, :, j * LANE:(j + 1) * LANE]
    return jnp.concatenate([tmp_ref[j] for j in range(nl)], axis=1)


def _deinterleave(val, tmp_ref, dst_ref, bsz, steps, skip=None):
    nl = tmp_ref.shape[0]
    for j in range(nl):
        tmp_ref[j] = val[:, j * LANE:(j + 1) * LANE]
    for b in range(bsz):
        for j in range(nl):
            lanes = slice(j * LANE, (j + 1) * LANE)
            v = tmp_ref.at[j][pl.ds(b, steps, stride=bsz), :]
            if skip is not None:
                v = v + skip[0][b, :, lanes] * skip[1][:, lanes]
            dst_ref[b, :, lanes] = v.astype(dst_ref.dtype)


SSM_UB = SW // SSM_SUPER
SSM_SB = GP // SSM_SUPER


def _sb_cols(s, half):
    return slice(half * GP + s * SSM_SB, half * GP + (s + 1) * SSM_SB)


def _ssm_in(v16, w_ref, x_ref):
    for s in range(SSM_SUPER):
        vs = v16[:, s * SSM_UB:(s + 1) * SSM_UB]
        for half in range(2):
            x_ref[:, _sb_cols(s, half)] = _dot(vs, w_ref[s * SSM_UB:(s + 1) * SSM_UB, half * SSM_SB:(half + 1) * SSM_SB])


def _ssm_out(x16, w_ref):
    outs = []
    for s in range(SSM_SUPER):
        rows = slice(s * SSM_UB, (s + 1) * SSM_UB)
        outs.append(_dot_nt(x16[:, _sb_cols(s, 0)], w_ref[rows, 0:SSM_SB])
                    + _dot_nt(x16[:, _sb_cols(s, 1)], w_ref[rows, SSM_SB:2 * SSM_SB]))
    return jnp.concatenate(outs, axis=1)


def _ssm_fwd(u, wb, wct, ar8, ai8, name):
    bsz, seq, _ = u.shape
    rt = seq * bsz
    r = _ssm_chunk_rows(rt, bsz)
    nt = r // SUB
    steps = r // bsz

    def body(u_ref, wb_hbm, wct_hbm, ar_ref, ai_ref, x_ref, y_ref, wb_ref, wct_ref, st_ref, tmp_ref):
        @pl.when(pl.program_id(0) == 0)
        def _():
            pltpu.sync_copy(wb_hbm, wb_ref)
            pltpu.sync_copy(wct_hbm, wct_ref)
            st_ref[...] = jnp.zeros_like(st_ref)

        _ssm_in(_interleave(u_ref, tmp_ref, bsz, steps).astype(BF16), wb_ref, x_ref)
        fr, fi = _scan_chunk(x_ref, nt, bsz, ar_ref[...], ai_ref[...], st_ref[:, 0:GP], st_ref[:, GP:2 * GP], False)
        st_ref[:, 0:GP] = fr
        st_ref[:, GP:2 * GP] = fi
        _deinterleave(_ssm_out(x_ref[...].astype(BF16), wct_ref), tmp_ref, y_ref, bsz, steps)

    anyspec = pl.BlockSpec(memory_space=pl.ANY)
    seqs = pl.BlockSpec((bsz, steps, SW), lambda i: (0, i, 0))
    return pl.pallas_call(
        body, name=name, grid=(rt // r,),
        in_specs=[seqs, anyspec, anyspec, _full2((SUB, GP)), _full2((SUB, GP))],
        out_specs=[pl.BlockSpec((r, 2 * GP), lambda i: (i, 0)), seqs],
        out_shape=[SDS((rt, 2 * GP), F32), SDS((bsz, seq, SW), F32)],
        scratch_shapes=[pltpu.VMEM(wb.shape, BF16), pltpu.VMEM(wct.shape, BF16), pltpu.VMEM((SUB, 2 * GP), F32),
                        pltpu.VMEM((SW // LANE, r, LANE), F32)],
        compiler_params=_cp("arbitrary"),
    )(u, wb, wct, ar8, ai8)


def _ssm_bwd(dy, u, xs, wb, wct, ar8, ai8, dskip, name):
    bsz, seq, _ = u.shape
    rt = seq * bsz
    r = _ssm_chunk_rows(rt, bsz)
    nt = r // SUB
    nc = rt // r
    steps = r // bsz
    ub = SW // SSM_SUPER
    sb = GP // SSM_SUPER

    def body(dys_ref, us_ref, x_ref, xh_ref, wb_hbm, wct_hbm, ar_ref, ai_ref, d_ref,
             dus_ref, dwb_hbm, dwct_hbm, dar_ref, dai_ref, wb_ref, wct_ref, g_ref, st_ref, awb_ref, awct_ref,
             tmp_ref):
        i = pl.program_id(0)
        dyb = _interleave(dys_ref, tmp_ref, bsz, steps).astype(BF16)
        ub16 = _interleave(us_ref, tmp_ref, bsz, steps).astype(BF16)

        @pl.when(i == 0)
        def _():
            pltpu.sync_copy(wb_hbm, wb_ref)
            pltpu.sync_copy(wct_hbm, wct_ref)
            st_ref[...] = jnp.zeros_like(st_ref)
            awb_ref[...] = jnp.zeros_like(awb_ref)
            awct_ref[...] = jnp.zeros_like(awct_ref)
            dar_ref[...] = jnp.zeros_like(dar_ref)
            dai_ref[...] = jnp.zeros_like(dai_ref)

        _ssm_in(dyb, wct_ref, g_ref)
        ar = ar_ref[...]
        ai = ai_ref[...]
        fr, fi = _scan_chunk(g_ref, nt, bsz, ar, -ai, st_ref[:, 0:GP], st_ref[:, GP:2 * GP], True)
        st_ref[:, 0:GP] = fr
        st_ref[:, GP:2 * GP] = fi

        gb = g_ref[...].astype(BF16)
        _deinterleave(_ssm_out(gb, wb_ref), tmp_ref, dus_ref, bsz, steps, skip=(dys_ref, d_ref))
        xb16 = x_ref[...].astype(BF16)
        for s in range(SSM_SUPER):
            us = ub16[:, s * ub:(s + 1) * ub]
            ds = dyb[:, s * ub:(s + 1) * ub]
            for half in range(2):
                cols = slice(half * GP + s * sb, half * GP + (s + 1) * sb)
                ocols = slice(half * sb, (half + 1) * sb)
                awb_ref[s * ub:(s + 1) * ub, ocols] += _dot_tn(us, gb[:, cols])
                awct_ref[s * ub:(s + 1) * ub, ocols] += _dot_tn(ds, xb16[:, cols])

        gr = g_ref[:, 0:GP]
        gi = g_ref[:, GP:2 * GP]
        xsr = pltpu.roll(x_ref[:, 0:GP], bsz, 0)
        xsi = pltpu.roll(x_ref[:, GP:2 * GP], bsz, 0)
        inner = lax.broadcasted_iota(jnp.int32, (r, 1), 0) >= bsz
        t_r = jnp.where(inner, gr * xsr + gi * xsi, 0.0)
        t_i = jnp.where(inner, gi * xsr - gr * xsi, 0.0)
        acc_r = jnp.sum(t_r.reshape(nt, SUB, GP), axis=0)
        acc_i = jnp.sum(t_i.reshape(nt, SUB, GP), axis=0)
        hr = xh_ref[:, 0:GP]
        hi = xh_ref[:, GP:2 * GP]
        if bsz % SUB:
            hr = pltpu.roll(hr, bsz, 0)
            hi = pltpu.roll(hi, bsz, 0)
        edge = (lax.broadcasted_iota(jnp.int32, (SUB, 1), 0) < bsz) & (i < nc - 1)
        g0r = g_ref[0:SUB, 0:GP]
        g0i = g_ref[0:SUB, GP:2 * GP]
        dar_ref[...] += acc_r + jnp.where(edge, g0r * hr + g0i * hi, 0.0)
        dai_ref[...] += acc_i + jnp.where(edge, g0i * hr - g0r * hi, 0.0)

        @pl.when(i == nc - 1)
        def _():
            pltpu.sync_copy(awb_ref, dwb_hbm)
            pltpu.sync_copy(awct_ref, dwct_hbm)

    anyspec = pl.BlockSpec(memory_space=pl.ANY)
    rev = lambda i: (nc - 1 - i, 0)
    seqs = pl.BlockSpec((bsz, steps, SW), lambda i: (0, nc - 1 - i, 0))
    wshape = (SW, 2 * sb)
    return pl.pallas_call(
        body, name=name, grid=(nc,),
        in_specs=[seqs, seqs, pl.BlockSpec((r, 2 * GP), rev),
                  pl.BlockSpec((SUB, 2 * GP), lambda i: (jnp.maximum((nc - 1 - i) * nt - 1, 0), 0)),
                  anyspec, anyspec, _full2((SUB, GP)), _full2((SUB, GP)), _full2((1, SW))],
        out_specs=[seqs, anyspec, anyspec, _full2((SUB, GP)), _full2((SUB, GP))],
        out_shape=[SDS((bsz, seq, SW), BF16), SDS(wshape, F32), SDS(wshape, F32), SDS((SUB, GP), F32),
                   SDS((SUB, GP), F32)],
        scratch_shapes=[pltpu.VMEM(wb.shape, BF16), pltpu.VMEM(wct.shape, BF16),
                        pltpu.VMEM((r, 2 * GP), F32), pltpu.VMEM((SUB, 2 * GP), F32),
                        pltpu.VMEM(wshape, F32), pltpu.VMEM(wshape, F32),
                        pltpu.VMEM((SW // LANE, r, LANE), F32)],
        compiler_params=_cp("arbitrary"),
    )(dy, u, xs, xs, wb, wct, ar8, ai8, dskip)


GELU_C = math.sqrt(2.0 / math.pi)


def _gelu(x):
    return 0.5 * x * (1.0 + jnp.tanh(GELU_C * (x + 0.044715 * x * x * x)))


def _gelu_grad(x):
    th = jnp.tanh(GELU_C * (x + 0.044715 * x * x * x))
    return 0.5 * (1.0 + th) + 0.5 * x * (1.0 - th * th) * GELU_C * (1.0 + 3.0 * 0.044715 * x * x)


def _ssm_post(ys, u, dskip, wglu, wso, name):
    bsz, seq, _ = ys.shape
    tm = _pick_tile(seq, (512, 256, 128))

    def body(ys_ref, u_ref, d_ref, wg_ref, wo_ref, s0_ref, z_ref, s1_ref, s2_ref, yb_ref):
        s0 = ys_ref[0] + d_ref[...] * u_ref[0]
        s1 = _gelu(s0)
        s1b = s1.astype(BF16)
        z = _dot(s1b, wg_ref[...])
        s2b = (s1 * _sigmoid(z)).astype(BF16)
        s0_ref[0] = s0
        z_ref[0] = z
        s1_ref[0] = s1b
        s2_ref[0] = s2b
        yb_ref[0] = _dot(s2b, wo_ref[...]).astype(BF16)

    return pl.pallas_call(
        body, name=name, grid=(bsz, seq // tm),
        in_specs=[_row(tm, SW), _row(tm, SW), _full2((1, SW)), _full2((SW, SW)), _full2((SW, D))],
        out_specs=[_row(tm, SW), _row(tm, SW), _row(tm, SW), _row(tm, SW), _row(tm, D)],
        out_shape=[SDS((bsz, seq, SW), F32), SDS((bsz, seq, SW), F32), SDS((bsz, seq, SW), BF16),
                   SDS((bsz, seq, SW), BF16), SDS((bsz, seq, D), BF16)],
        compiler_params=_cp("parallel", "parallel"),
    )(ys, u, dskip, wglu, wso)


def _ssm_post_bwd(dyb, s0, z, u, dskip, wglu, wso, name):
    bsz, seq, _ = s0.shape
    tm = _pick_tile(seq, (512, 256, 128))

    def body(dyb_ref, s0_ref, z_ref, u_ref, wg_ref, wo_ref, ds0_ref, dz_ref, dd_ref):
        b = pl.program_id(0)
        i = pl.program_id(1)
        ds2 = _dot_nt(dyb_ref[0], wo_ref[...])
        s0 = s0_ref[0]
        s1 = _gelu(s0)
        sg = _sigmoid(z_ref[0])
        dz = ds2 * s1 * sg * (1.0 - sg)
        dzb = dz.astype(BF16)
        ds1 = ds2 * sg + _dot_nt(dzb, wg_ref[...])
        ds0 = ds1 * _gelu_grad(s0)
        ds0_ref[0] = ds0
        dz_ref[0] = dzb
        part = jnp.sum(ds0 * u_ref[0], axis=0, keepdims=True)

        @pl.when((i == 0) & (b == 0))
        def _():
            dd_ref[...] = part

        @pl.when((i > 0) | (b > 0))
        def _():
            dd_ref[...] += part

    del dskip
    return pl.pallas_call(
        body, name=name, grid=(bsz, seq // tm),
        in_specs=[_row(tm, D), _row(tm, SW), _row(tm, SW), _row(tm, SW), _full2((SW, SW)), _full2((SW, D))],
        out_specs=[_row(tm, SW), _row(tm, SW), _full2((1, SW))],
        out_shape=[SDS((bsz, seq, SW), F32), SDS((bsz, seq, SW), BF16), SDS((1, SW), F32)],
        compiler_params=_cp("arbitrary", "arbitrary"),
    )(dyb, s0, z, u, wglu, wso)


def _merge_out(ya, yb, p3, wout, x1, gt, name):
    bsz, seq, _ = ya.shape
    tm = _pick_tile(seq, (512, 256, 128))

    def body(ya_ref, yb_ref, ga_ref, gbb_ref, w_ref, x_ref, gt_ref, mg_ref, mix_ref, xo_ref):
        merged = (_sigmoid(ga_ref[0].astype(F32)) * ya_ref[0].astype(F32)
                  + _sigmoid(gbb_ref[0].astype(F32)) * yb_ref[0].astype(F32)).astype(BF16)
        mix = _dot(merged, w_ref[...])
        mg_ref[0] = merged
        mix_ref[0] = mix.astype(BF16)
        xo_ref[0] = x_ref[0] + gt_ref[0] * mix

    return pl.pallas_call(
        body, name=name, grid=(bsz, seq // tm),
        in_specs=[_row(tm, D), _row(tm, D), _row(tm, D, 0), _row(tm, D, 1), _full2((D, D)), _row(tm, D), _seqvec(D)],
        out_specs=[_row(tm, D), _row(tm, D), _row(tm, D)],
        out_shape=[SDS((bsz, seq, D), BF16), SDS((bsz, seq, D), BF16), SDS((bsz, seq, D), F32)],
        compiler_params=_cp("parallel", "parallel"),
    )(ya, yb, p3, p3, wout, x1, gt)


def _merge_bwd(dx2, gt, mix, ya, yb, p3, wout, name):
    bsz, seq, _ = ya.shape
    tm = _pick_tile(seq, (512, 256, 128))

    def body(dx_ref, gt_ref, mix_ref, ya_ref, yb_ref, ga_ref, gbb_ref, w_ref, dmix_ref, dya_ref, dyb_ref, dp_ref, dgt_ref):
        i = pl.program_id(1)
        dx = dx_ref[0]
        dmix = (gt_ref[0] * dx).astype(BF16)
        dmix_ref[0] = dmix
        part = jnp.sum(dx * mix_ref[0].astype(F32), axis=0, keepdims=True)

        @pl.when(i == 0)
        def _():
            dgt_ref[0] = part

        @pl.when(i > 0)
        def _():
            dgt_ref[0] += part

        dmg = _dot_nt(dmix, w_ref[...])
        sa = _sigmoid(ga_ref[0].astype(F32))
        sb = _sigmoid(gbb_ref[0].astype(F32))
        dya_ref[0] = (dmg * sa).astype(BF16)
        dyb_ref[0] = (dmg * sb).astype(BF16)
        dp_ref[0, :, 0:D] = (dmg * ya_ref[0].astype(F32) * sa * (1.0 - sa)).astype(BF16)
        dp_ref[0, :, D:2 * D] = (dmg * yb_ref[0].astype(F32) * sb * (1.0 - sb)).astype(BF16)

    bshape = SDS((bsz, seq, D), BF16)
    return pl.pallas_call(
        body, name=name, grid=(bsz, seq // tm),
        in_specs=[_row(tm, D), _seqvec(D), _row(tm, D), _row(tm, D), _row(tm, D), _row(tm, D, 0), _row(tm, D, 1),
                  _full2((D, D))],
        out_specs=[_row(tm, D), _row(tm, D), _row(tm, D), _row(tm, 2 * D), _seqvec(D)],
        out_shape=[bshape, bshape, bshape, SDS((bsz, seq, 2 * D), BF16), SDS((bsz, 1, D), F32)],
        compiler_params=_cp("arbitrary", "arbitrary"),
    )(dx2, gt, mix, ya, yb, p3, p3, wout)


def _final_loss(x3, gfin, target, name):
    bsz, seq, dm = x3.shape
    tm = _pick_tile(seq, (512, 256, 128))

    def body(x_ref, g_ref, t_ref, dx_ref, loss_ref, dg_ref):
        b = pl.program_id(0)
        i = pl.program_id(1)
        xf = x_ref[0]
        gv = g_ref[...]
        r = lax.rsqrt(jnp.mean(xf * xf, axis=-1, keepdims=True) + EPS)
        xhat = xf * r
        e = xhat * gv - t_ref[0]
        dy = e * (1.0 / dm)
        dxh = dy * gv
        dx_ref[0] = r * (dxh - xhat * jnp.mean(dxh * xhat, axis=-1, keepdims=True))
        p_l = jnp.sum(e * e, axis=0, keepdims=True) * (0.5 / dm)
        p_g = jnp.sum(dy * xhat, axis=0, keepdims=True)

        @pl.when((i == 0) & (b == 0))
        def _():
            loss_ref[...] = p_l
            dg_ref[...] = p_g

        @pl.when((i > 0) | (b > 0))
        def _():
            loss_ref[...] += p_l
            dg_ref[...] += p_g

    return pl.pallas_call(
        body, name=name, grid=(bsz, seq // tm),
        in_specs=[_row(tm, dm), _full2((1, dm)), _row(tm, dm)],
        out_specs=[_row(tm, dm), _full2((1, dm)), _full2((1, dm))],
        out_shape=[SDS((bsz, seq, dm), F32), SDS((1, dm), F32), SDS((1, dm), F32)],
        compiler_params=_cp("arbitrary", "arbitrary"),
    )(x3, gfin, target)


def _ada_fwd(c_all, w_shard, b_shard):
    nb = c_all.shape[0]
    n = w_shard.shape[2]

    def body(c_ref, w_ref, b_ref, o_ref):
        cv = c_ref[...]
        cond = (cv * _sigmoid(cv)).astype(BF16)
        o_ref[...] = _dot(cond, w_ref[0].astype(BF16)) + b_ref[...]

    return pl.pallas_call(body, name="ada_fwd", out_shape=SDS((nb, n), F32), compiler_params=_cp())(
        c_all, w_shard, b_shard)


def _ada_bwd(c_all, dmod_shard, dmod_all):
    n = dmod_shard.shape[1]

    def body(c_ref, ds_ref, da_ref, gw_ref, gb_ref):
        cv = c_ref[...]
        cond = (cv * _sigmoid(cv)).astype(BF16)
        gw_ref[...] = _dot_tn(cond, ds_ref[...].astype(BF16))
        gb_ref[...] = jnp.sum(da_ref[...], axis=0, keepdims=True)

    return pl.pallas_call(
        body, name="ada_bwd", out_shape=[SDS((D, n), F32), SDS((1, dmod_all.shape[1]), F32)], compiler_params=_cp(),
    )(c_all, dmod_shard, dmod_all)


def _adamw_math(w, g, m, v):
    m = B1 * m + (1.0 - B1) * g
    v = B2 * v + (1.0 - B2) * (g * g)
    delta = -LR * ((m / BC1) / (jnp.sqrt(v / BC2) + AEPS) + WD * w)
    return delta, m, v


def _adamw_big(w, m, v, recv_own, recv_sib, name):
    _, rows, cols = w.shape
    tr = _pick_tile(rows, tuple(t for t in (512, 256, 128, 64, 32, 16, 8) if t * cols <= 192 * 1024))

    def body(w_ref, m_ref, v_ref, a_ref, b_ref, g_ref, d_ref, mo_ref, vo_ref):
        def chip_sum(r):
            acc = r[0].astype(F32)
            for k in range(1, N_CHIPS):
                acc = acc + r[k].astype(F32)
            return acc

        g = chip_sum(a_ref) + chip_sum(b_ref)
        delta, mn, vn = _adamw_math(w_ref[0], g, m_ref[0], v_ref[0])
        g_ref[0] = g
        d_ref[0] = delta
        mo_ref[0] = mn
        vo_ref[0] = vn

    own = pl.BlockSpec((1, tr, cols), lambda i: (0, i, 0))
    rspec = pl.BlockSpec((N_CHIPS, tr, cols), lambda i: (0, i, 0))
    shp = SDS(w.shape, F32)
    return pl.pallas_call(
        body, name=name, grid=(rows // tr,),
        in_specs=[own, own, own, rspec, rspec], out_specs=[own, own, own, own], out_shape=[shp, shp, shp, shp],
        compiler_params=_cp("parallel"),
    )(w, m, v, recv_own, recv_sib)


def _adamw_plain(w, m, v, g, name):
    def body(w_ref, m_ref, v_ref, g_ref, d_ref, mo_ref, vo_ref):
        delta, mn, vn = _adamw_math(w_ref[...], g_ref[...], m_ref[...], v_ref[...])
        d_ref[...] = delta
        mo_ref[...] = mn
        vo_ref[...] = vn

    shp = SDS(w.shape, F32)
    return pl.pallas_call(body, name=name, out_shape=[shp, shp, shp], compiler_params=_cp())(w, m, v, g)


def _adamw_rows(w, m, v, g, name):
    _, rows, cols = w.shape
    tr = _pick_tile(rows, (128, 64, 32, 16, 8))

    def body(w_ref, m_ref, v_ref, g_ref, d_ref, mo_ref, vo_ref):
        delta, mn, vn = _adamw_math(w_ref[0], g_ref[...], m_ref[0], v_ref[0])
        d_ref[0] = delta
        mo_ref[0] = mn
        vo_ref[0] = vn

    spec = pl.BlockSpec((1, tr, cols), lambda i: (0, i, 0))
    shp = SDS(w.shape, F32)
    return pl.pallas_call(
        body, name=name, grid=(rows // tr,), in_specs=[spec] * 3 + [pl.BlockSpec((tr, cols), lambda i: (i, 0))],
        out_specs=[spec] * 3, out_shape=[shp] * 3, compiler_params=_cp("parallel"),
    )(w, m, v, g)


def _sum_slabs(r, name):
    n, rows, cols = r.shape
    tr = _pick_tile(rows, (256, 128, 64))

    def body(r_ref, o_ref):
        acc = r_ref[0].astype(F32)
        for j in range(1, n):
            acc = acc + r_ref[j].astype(F32)
        o_ref[...] = acc

    return pl.pallas_call(
        body, name=name, grid=(rows // tr,),
        in_specs=[pl.BlockSpec((n, tr, cols), lambda i: (0, i, 0))],
        out_specs=pl.BlockSpec((tr, cols), lambda i: (i, 0)), out_shape=SDS((rows, cols), F32),
        compiler_params=_cp("parallel"),
    )(r)


def _place():
    return lax.axis_index("x"), lax.axis_index("y"), lax.axis_index("c")


def _all_gather8(blk, name):
    m_per, n = blk.shape

    def body(x_ref, out_ref, send_sems, recv_sems, local_sem):
        x, y, c = _place()
        me, sibling = (x, y, c), (x, y, 1 - c)
        chips = [(1 - x, y), (x, 1 - y), (1 - x, 1 - y)]

        def rows(px, py, pc):
            return out_ref.at[pl.ds((4 * px + 2 * py + pc) * m_per, m_per), :]

        def copy(k, block, to, src=None):
            return pltpu.make_async_remote_copy(
                src_ref=rows(*block) if src is None else src, dst_ref=rows(*block),
                send_sem=send_sems.at[k], recv_sem=recv_sems.at[k], device_id=to, device_id_type=MESH)

        mine = pltpu.make_async_copy(x_ref, rows(*me), local_sem)
        mine.start()
        first = [copy(0, me, sibling, src=x_ref)]
        first += [copy(1 + j, me, (*chip, c), src=x_ref) for j, chip in enumerate(chips)]
        for cp in first:
            cp.start()
        passed = [copy(4 + j, (*chip, c), sibling) for j, chip in enumerate(chips)]
        for j, chip in enumerate(chips):
            copy(1 + j, (*chip, c), me).wait_recv()
            passed[j].start()
        copy(0, sibling, me).wait_recv()
        for j, chip in enumerate(chips):
            copy(4 + j, (*chip, 1 - c), me).wait_recv()
        for cp in first + passed:
            cp.wait_send()
        mine.wait()

    return pl.pallas_call(
        body, name=name, out_shape=SDS((N_DEV * m_per, n), blk.dtype),
        in_specs=[pl.BlockSpec(memory_space=pltpu.VMEM)], out_specs=pl.BlockSpec(memory_space=pltpu.VMEM),
        scratch_shapes=[pltpu.SemaphoreType.DMA((7,)), pltpu.SemaphoreType.DMA((7,)), pltpu.SemaphoreType.DMA],
        compiler_params=pltpu.CompilerParams(vmem_limit_bytes=VMEM_LIMIT),
    )(blk)


def _chip_peers(x, y):
    return [(1 - x, y), (x, 1 - y), (1 - x, 1 - y)]


SIBLING = "sibling"
HALF = "half"


def _forward_halves(lands, name):
    n = len(lands)

    def body(*refs):
        out_refs = refs[n:2 * n]
        send_sems, recv_sems = refs[2 * n:]
        x, y, c = _place()

        def copies(core):
            cps = []
            for a, ref in enumerate(out_refs):
                half = ref.shape[1] // 2
                rows = pl.ds(pl.multiple_of(core * half, 16), half)
                for j, (px, py) in enumerate(_chip_peers(x, y)):
                    cps.append(pltpu.make_async_remote_copy(
                        src_ref=ref.at[2 * px + py, rows], dst_ref=ref.at[2 * px + py, rows],
                        send_sem=send_sems.at[3 * a + j], recv_sem=recv_sems.at[3 * a + j],
                        device_id=(x, y, 1 - c), device_id_type=MESH))
            return cps

        mine = copies(c)
        for cp in mine:
            cp.start()
        for cp in copies(1 - c):
            cp.wait_recv()
        for cp in mine:
            cp.wait_send()

    anyspec = pl.BlockSpec(memory_space=pl.ANY)
    return pl.pallas_call(
        body, name=name, out_shape=[SDS(l.shape, l.dtype) for l in lands], in_specs=[anyspec] * n,
        out_specs=[anyspec] * n, input_output_aliases={i: i for i in range(n)},
        scratch_shapes=[pltpu.SemaphoreType.DMA((3 * n,)), pltpu.SemaphoreType.DMA((3 * n,))],
        compiler_params=pltpu.CompilerParams(vmem_limit_bytes=VMEM_LIMIT),
    )(*lands)


def _peer_copies(src_refs, land_refs, send_sems, recv_sems, scatter, landed):
    x, y, c = _place()
    if scatter == SIBLING:
        return [pltpu.make_async_remote_copy(
            src_ref=s, dst_ref=l, send_sem=send_sems.at[a], recv_sem=recv_sems.at[a],
            device_id=(x, y, 1 - c), device_id_type=MESH) for a, (s, l) in enumerate(zip(src_refs, land_refs))]
    cps = []
    for a, (src_ref, land_ref) in enumerate(zip(src_refs, land_refs)):
        for j, (px, py) in enumerate(_chip_peers(x, y)):
            if scatter == HALF:
                half = land_ref.shape[1] // 2
                rows = pl.ds(pl.multiple_of(c * half, 16), half)
                src = land_ref.at[2 * x + y, rows]
                dst = land_ref.at[2 * px + py, rows] if landed else land_ref.at[2 * x + y, rows]
                cps.append(pltpu.make_async_remote_copy(
                    src_ref=src, dst_ref=dst, send_sem=send_sems.at[3 * a + j], recv_sem=recv_sems.at[3 * a + j],
                    device_id=(px, py, c), device_id_type=MESH))
                continue
            if scatter:
                src = src_ref.at[2 * px + py]
            else:
                src = land_ref.at[2 * x + y] if src_ref is None else src_ref
            dst = land_ref.at[2 * px + py] if landed else land_ref.at[2 * x + y]
            cps.append(pltpu.make_async_remote_copy(
                src_ref=src, dst_ref=dst, send_sem=send_sems.at[3 * a + j], recv_sem=recv_sems.at[3 * a + j],
                device_id=(px, py, c), device_id_type=MESH))
    return cps


def _exchange_chips(srcs, scatter, name):
    n = len(srcs)

    def body(*refs):
        src_refs, land_refs = refs[:n], refs[n:2 * n]
        send_sems, recv_sems, local_sems = refs[2 * n:]
        x, y, _ = _place()
        me = 2 * x + y
        mine = [pltpu.make_async_copy(s.at[me] if scatter else s, l.at[me], local_sems.at[a])
                for a, (s, l) in enumerate(zip(src_refs, land_refs))]
        for cp in mine:
            cp.start()
        out = _peer_copies(src_refs, land_refs, send_sems, recv_sems, scatter, False)
        for cp in out:
            cp.start()
        for cp in _peer_copies(src_refs, land_refs, send_sems, recv_sems, scatter, True):
            cp.wait_recv()
        for cp in out:
            cp.wait_send()
        for cp in mine:
            cp.wait()

    anyspec = pl.BlockSpec(memory_space=pl.ANY)
    shapes = [SDS(s.shape if scatter else (N_CHIPS,) + s.shape, s.dtype) for s in srcs]
    return pl.pallas_call(
        body, name=name, out_shape=shapes, in_specs=[anyspec] * n, out_specs=[anyspec] * n,
        scratch_shapes=[pltpu.SemaphoreType.DMA((3 * n,)), pltpu.SemaphoreType.DMA((3 * n,)),
                        pltpu.SemaphoreType.DMA((n,))],
        compiler_params=pltpu.CompilerParams(vmem_limit_bytes=VMEM_LIMIT),
    )(*srcs)


_HBM = pl.BlockSpec(memory_space=pltpu.HBM)
_SEM = pl.BlockSpec(memory_space=pltpu.SEMAPHORE)
_EFFECT = pltpu.SideEffectType.DATAFLOW_SIDE_EFFECTING


def _exchange_begin(srcs, lands, scatter, name):
    srcs = tuple(srcs or ())
    n, ns = len(lands), len(srcs)
    nsem = n if scatter == SIBLING else 3 * n

    def body(*refs):
        src_refs = refs[:ns] if ns else (None,) * n
        land_refs = refs[ns:ns + n]
        send_sems, recv_sems = refs[ns + n:ns + n + 2]
        token = refs[-1]
        for cp in _peer_copies(src_refs, land_refs, send_sems, recv_sems, scatter, False):
            cp.start()
        token[...] = jnp.zeros_like(token)

    ops = (*srcs, *lands)
    res = pl.pallas_call(
        body, name=name,
        out_shape=(pltpu.SemaphoreType.DMA((nsem,)), pltpu.SemaphoreType.DMA((nsem,)),
                   *[pltpu.HBM(a.shape, a.dtype) for a in ops], SDS((SUB, LANE), F32)),
        in_specs=[_HBM] * len(ops), out_specs=(_SEM, _SEM, *[_HBM] * len(ops), pl.BlockSpec(memory_space=pltpu.VMEM)),
        input_output_aliases={i: 2 + i for i in range(len(ops))},
        compiler_params=pltpu.CompilerParams(has_side_effects=_EFFECT),
    )(*[pltpu.with_memory_space_constraint(a, pltpu.HBM) for a in ops])
    return res[0], res[1], res[2:2 + ns], res[2 + ns:2 + ns + n], res[-1]


def _exchange_end(handle, after, scatter, name, with_srcs=False):
    send_sems, recv_sems, srcs, lands, _ = handle
    n, ns = len(lands), len(srcs)

    def body(*refs):
        src_refs = refs[:ns] if ns else (None,) * n
        land_refs = refs[ns:ns + n]
        send_sems, recv_sems = refs[ns + n:ns + n + 2]
        for cp in _peer_copies(src_refs, land_refs, send_sems, recv_sems, scatter, True):
            cp.wait_send()
            cp.wait_recv()

    ops = (*srcs, *lands)
    res = pl.pallas_call(
        body, name=name,
        out_shape=tuple(pltpu.HBM(a.shape, a.dtype) for a in ops),
        in_specs=[_HBM] * len(ops) + [_SEM, _SEM, pl.BlockSpec(memory_space=pl.ANY)], out_specs=tuple([_HBM] * len(ops)),
        input_output_aliases={i: i for i in range(len(ops))},
        compiler_params=pltpu.CompilerParams(has_side_effects=_EFFECT),
    )(*ops, send_sems, recv_sems, after)
    return (list(res[:ns]), list(res[ns:])) if with_srcs else list(res[ns:])


def _own_slab(stack4, chip):
    idx = lax.broadcasted_iota(jnp.int32, (N_CHIPS,) + (1,) * (stack4.ndim - 1), 0)
    return jnp.where(idx == chip, stack4, jnp.zeros((), stack4.dtype))


def _swap_sibling(vs, name):
    n = len(vs)

    def body(*refs):
        in_refs, out_refs = refs[:n], refs[n:2 * n]
        send_sems, recv_sems = refs[2 * n:]
        x, y, c = _place()
        cps = [pltpu.make_async_remote_copy(
            src_ref=i, dst_ref=o, send_sem=send_sems.at[a], recv_sem=recv_sems.at[a],
            device_id=(x, y, 1 - c), device_id_type=MESH) for a, (i, o) in enumerate(zip(in_refs, out_refs))]
        for cp in cps:
            cp.start()
        for cp in cps:
            cp.wait()

    anyspec = pl.BlockSpec(memory_space=pl.ANY)
    return pl.pallas_call(
        body, name=name, out_shape=[SDS(v.shape, v.dtype) for v in vs], in_specs=[anyspec] * n, out_specs=[anyspec] * n,
        scratch_shapes=[pltpu.SemaphoreType.DMA((n,)), pltpu.SemaphoreType.DMA((n,))],
        compiler_params=pltpu.CompilerParams(vmem_limit_bytes=VMEM_LIMIT),
    )(*vs)


def _select(stacked, idx):
    out = stacked[0]
    for j in range(1, stacked.shape[0]):
        out = jnp.where(idx == j, stacked[j], out)
    return out


BIG = (
    ("w1_a", DFF // 4, D, False), ("w3_a", DFF // 4, D, False), ("w2_a", DFF // 4, D, False),
    ("w_in", D, 5632 // 4, True), ("w_conv_out", CW // 4, D, False), ("w_glu", SW // 4, SW, False),
    ("w_ssm_out", SW, D // 4, True), ("w_out", D // 4, D, False),
    ("w1_b", DFF // 4, D, False), ("w3_b", DFF // 4, D, False), ("w2_b", DFF // 4, D, False),
)
TRANSPOSED = frozenset(("w1_a", "w3_a", "w1_b", "w3_b"))
PACK_COLS = 1024


def _view(a, name):
    return jnp.transpose(a, (0, 2, 1)) if name in TRANSPOSED else a


def _full_from_stacked(st, split_cols):
    _, rows, cols = st.shape
    if split_cols:
        return st.transpose(1, 0, 2).reshape(rows, N_CHIPS * cols)
    return st.reshape(N_CHIPS * rows, cols)


def _stacked_from_full(full, rows, cols, split_cols):
    if split_cols:
        return full.reshape(rows, N_CHIPS, cols).transpose(1, 0, 2)
    return full.reshape(N_CHIPS, rows, cols)


def _cols_from_stacked(st, c0, c1):
    w = st.shape[2]
    parts = [st[j][:, max(c0, j * w) - j * w:min(c1, (j + 1) * w) - j * w]
             for j in range(N_CHIPS) if max(c0, j * w) < min(c1, (j + 1) * w)]
    return parts[0] if len(parts) == 1 else jnp.concatenate(parts, axis=1)


def _stack_cols(pieces, w):
    offs = [sum(p.shape[1] for p in pieces[:i]) for i in range(len(pieces))]
    shards = []
    for j in range(N_CHIPS):
        lo, hi = j * w, (j + 1) * w
        parts = [p[:, max(lo, o) - o:min(hi, o + p.shape[1]) - o]
                 for p, o in zip(pieces, offs) if max(lo, o) < min(hi, o + p.shape[1])]
        shards.append(parts[0] if len(parts) == 1 else jnp.concatenate(parts, axis=1))
    return jnp.stack(shards)


def _blockdiag(t):
    r = lax.broadcasted_iota(jnp.int32, (SW, GP), 0) // NH
    cidx = lax.broadcasted_iota(jnp.int32, (SW, GP), 1) // NP
    dense = jnp.where(r == cidx, jnp.tile(t, (NG, 1)), 0.0)
    ub, sb = SW // SSM_SUPER, GP // SSM_SUPER
    return jnp.concatenate([dense[s * ub:(s + 1) * ub, s * sb:(s + 1) * sb] for s in range(SSM_SUPER)], axis=0)


def _blockdiag_extract(acc):
    gs = NG // SSM_SUPER
    a = acc.reshape(NG, NH, gs, NP)
    sel = (lax.broadcasted_iota(jnp.int32, (NG, 1, gs, 1), 0) % gs) == lax.broadcasted_iota(jnp.int32, (NG, 1, gs, 1), 2)
    a = jnp.sum(jnp.where(sel, a, 0.0), axis=2)
    return a.transpose(1, 0, 2).reshape(NH, GP)


def _to_t(p):
    return p.transpose(2, 0, 1).reshape(NH, GP)


def _from_t(t):
    return t.reshape(NH, NG, NP).transpose(1, 2, 0)


def _c_to_t(p):
    return p.transpose(1, 0, 2).reshape(NH, GP)


def _c_from_t(t):
    return t.reshape(NH, NG, NP).transpose(1, 0, 2)


def _ffn_forward(x, g, sh, sc, gt, w1, w3, w2, tag):
    h = _norm_mod(x, g, sh, sc, f"{tag}_norm")
    if callable(w1):
        w1, w3 = w1(h)
    a, b, hid = _swiglu_up(h, w1, w3, f"{tag}_up")
    w2 = w2(hid) if callable(w2) else w2
    f, xo = _ffn_down(hid, w2, x, gt, f"{tag}_down")
    return xo, (x, h, a, b, hid, f), w2


def _ffn_backward(dxo, saved, g, sc, gt, w1, w3, w2, tag, emit=lambda key, gw: 0.0):
    x, h, a, b, hid, f = saved
    dfs, da, db, dgt = _ffn_bwd_hid(dxo, gt, f, a, b, w2, f"{tag}_bwd_hid")
    h2 = _flat(h)
    gw2 = _mm(_flat(hid), _flat(dfs), ta=True, out_dtype=BF16, name=f"{tag}_gw2")
    tok = emit("w2", gw2)
    gw1 = _mm(_flat(da), h2, ta=True, out_dtype=BF16, name=f"{tag}_gw1")
    tok = tok + emit("w1", gw1)
    gw3 = _mm(_flat(db), h2, ta=True, out_dtype=BF16, name=f"{tag}_gw3")
    tok = tok + emit("w3", gw3)
    dx, dsh, dsc, dg = _dh_norm_bwd([da, db], [w1, w3], x, g, sc + tok, dxo, f"{tag}_bwd_dh", transposed=True)
    return dx, (dsh, dsc, dgt, dg), (gw1, gw3, gw2)


def kernel(x, c, w_ada, b_ada, g_ffn1, w1_a, w3_a, w2_a, g_mix, w_in, conv_w, w_conv_out, a_re, a_im, b_re, b_im, c_re, c_im, log_dt, d_skip, w_glu, w_ssm_out, w_out, g_ffn2, w1_b, w3_b, w2_b, g_final, loss_target, m_w_ada, m_b_ada, m_g_ffn1, m_w1_a, m_w3_a, m_w2_a, m_g_mix, m_w_in, m_conv_w, m_w_conv_out, m_a_re, m_a_im, m_b_re, m_b_im, m_c_re, m_c_im, m_log_dt, m_d_skip, m_w_glu, m_w_ssm_out, m_w_out, m_g_ffn2, m_w1_b, m_w3_b, m_w2_b, m_g_final, v_w_ada, v_b_ada, v_g_ffn1, v_w1_a, v_w3_a, v_w2_a, v_g_mix, v_w_in, v_conv_w, v_w_conv_out, v_a_re, v_a_im, v_b_re, v_b_im, v_c_re, v_c_im, v_log_dt, v_d_skip, v_w_glu, v_w_ssm_out, v_w_out, v_g_ffn2, v_w1_b, v_w3_b, v_w2_b, v_g_final):
    args = dict(locals())
    names = ["w_ada", "b_ada", "g_ffn1", "w1_a", "w3_a", "w2_a", "g_mix", "w_in", "conv_w", "w_conv_out", "a_re",
             "a_im", "b_re", "b_im", "c_re", "c_im", "log_dt", "d_skip", "w_glu", "w_ssm_out", "w_out", "g_ffn2",
             "w1_b", "w3_b", "w2_b", "g_final"]
    bsz, seq, _ = x.shape
    mx, my, mc = _place()
    chip = 2 * mx + my
    dev = 4 * mx + 2 * my + mc

    groups = (BIG[:3], BIG[3:8], BIG[8:])
    wfull = {}

    def shards(grp):
        return [_view(args[n], n)[0] for n, _, _, _ in grp]

    def unpack_group(gathered, grp):
        for (n, _, _, split), st in zip(grp, gathered):
            wfull[n] = st if n == "w_in" else _full_from_stacked(st, split)

    up_grp, down_grp = groups[0][:2], groups[0][2:]

    nmod_shard = NMOD * D // N_CHIPS
    c_all = _all_gather8(c.reshape(SUB, -1), "gather_c").reshape(N_DEV * bsz, D)
    b_shard = _select(b_ada.reshape(N_CHIPS, 1, nmod_shard), chip)
    mod_shard = _ada_fwd(c_all, w_ada, b_shard)
    nb = N_DEV * bsz
    cw_pad = jnp.pad(conv_w[0], ((0, SUB - 3), (0, nmod_shard - CW // N_CHIPS)))
    mod_st = _exchange_chips([jnp.concatenate([mod_shard, cw_pad], axis=0)], False, "gather_mod")[0]
    mod_all = mod_st[:, :nb].transpose(1, 0, 2).reshape(N_DEV, bsz, NMOD * D)
    mod = _select(mod_all, dev)

    def gather_begin(raw, name, mode=False):
        lands = [_own_slab(jnp.broadcast_to(a.astype(BF16)[None], (N_CHIPS,) + a.shape), chip) for a in raw]
        return _exchange_begin(None, lands, mode, name)

    def gather_end_halves(handle, after, name):
        return _forward_halves(_exchange_end(handle, after, HALF, f"{name}_wait"), f"{name}_forward")

    up_raw, mod = lax.optimization_barrier((shards(up_grp), mod))
    up_handle = gather_begin(up_raw, "gather_w_ffn1_up_start", HALF)
    (down_raw, mix_raw, ffn2_raw), up_token = lax.optimization_barrier(
        ((shards(down_grp), shards(groups[1]), shards(groups[2])), up_handle[4][0:1, 0:1]))
    down_handle = gather_begin(down_raw, "gather_w_ffn1_down_start", HALF)
    mix_handle = gather_begin(mix_raw, "gather_w_mix_start", HALF)
    ffn2_handle = gather_begin(ffn2_raw, "gather_w_ffn2_start")
    start_tokens = up_token + down_handle[4][0:1, 0:1] + mix_handle[4][0:1, 0:1] + ffn2_handle[4][0:1, 0:1]

    sh1, sc1, gt1, sh2, sc2, gt2, sh3, sc3, gt3 = [mod[:, None, j * D:(j + 1) * D] for j in range(NMOD)]
    convw = mod_st[:, nb:nb + 3, :CW // N_CHIPS].transpose(1, 0, 2).reshape(3, CW)
    convw8 = jnp.pad(convw, ((0, SUB - 3), (0, 0)))

    are, aim = a_re.reshape(1, GP), a_im.reshape(1, GP)
    ldt = jnp.broadcast_to(log_dt.reshape(NG, 1), (NG, NP)).reshape(1, GP)
    bre_t, bim_t = _to_t(b_re[0]), _to_t(b_im[0])
    abr, abi, bbr_t, bbi_t = _ssm_disc(are, aim, ldt, bre_t, bim_t)
    wb = jnp.concatenate([_blockdiag(bbr_t), _blockdiag(bbi_t)], axis=1).astype(BF16)
    wct = jnp.concatenate([_blockdiag(_c_to_t(c_re[0])), _blockdiag(-_c_to_t(c_im[0]))], axis=1).astype(BF16)
    ar8 = jnp.broadcast_to(abr, (SUB, GP))
    ai8 = jnp.broadcast_to(abi, (SUB, GP))

    def late_w2a(hid):
        unpack_group(gather_end_halves(down_handle, hid, "gather_w_ffn1_down"), down_grp)
        return wfull["w2_a"]

    def late_w13a(h):
        unpack_group(gather_end_halves(up_handle, h, "gather_w_ffn1_up"), up_grp)
        return wfull["w1_a"], wfull["w3_a"]

    x1, ffn1_saved, _ = _ffn_forward(x, g_ffn1 + start_tokens, sh1, sc1, gt1, late_w13a, None, late_w2a, "ffn1")
    unpack_group(gather_end_halves(mix_handle, x1, "gather_w_mix"), groups[1])

    h2 = _norm_mod(x1, g_mix, sh2, sc2, "mix_norm")
    h2f = _flat(h2)
    win_st = wfull["w_in"]
    win1 = _cols_from_stacked(win_st, 0, 3 * CW)
    winu = _cols_from_stacked(win_st, 3 * CW, 3 * CW + SW)
    win3 = _cols_from_stacked(win_st, 3 * CW + SW, 3 * CW + SW + 2 * D)
    p1 = _mm(h2f, win1, out_dtype=BF16, name="mix_in1").reshape(bsz, seq, 3 * CW)
    u = _mm(h2f, winu, name="mix_inu").reshape(bsz, seq, SW)
    p3 = _mm(h2f, win3, out_dtype=BF16, name="mix_in3").reshape(bsz, seq, 2 * D)

    ya_in = _conv_fwd(p1, convw8, "conv_fwd")
    ya = _mm(_flat(ya_in), wfull["w_conv_out"], out_dtype=BF16, name="conv_out").reshape(bsz, seq, D)

    xs, ys = _ssm_fwd(u, wb, wct, ar8, ai8, "ssm_fwd")
    s0, z, s1, s2, yb = _ssm_post(ys, u, d_skip, wfull["w_glu"], wfull["w_ssm_out"], "ssm_post")

    merged, mix, x2 = _merge_out(ya, yb, p3, wfull["w_out"], x1, gt2, "merge_out")
    unpack_group(_exchange_end(ffn2_handle, x2, False, "gather_w_ffn2_wait"), groups[2])
    x3, ffn2_saved, _ = _ffn_forward(x2, g_ffn2, sh3, sc3, gt3, wfull["w1_b"], wfull["w3_b"], wfull["w2_b"], "ffn2")

    dx3, lossvec, dgfin = _final_loss(x3, g_final.reshape(1, D), loss_target, "final_loss")
    loss = lax.psum(jnp.sum(lossvec), ("x", "y", "c"))

    gfull = {}
    dx2, (dsh3, dsc3, dgt3, dg3), (gfull["w1_b"], gfull["w3_b"], gfull["w2_b"]) = _ffn_backward(
        dx3, ffn2_saved, g_ffn2, sc3, gt3, wfull["w1_b"], wfull["w3_b"], wfull["w2_b"], "ffn2")

    def stack_group(grp):
        return [gfull[n] if gfull[n].ndim == 3 else _stacked_from_full(gfull[n], rows, cols, split)
                for n, rows, cols, split in grp]

    st_ffn2 = stack_group(groups[2])
    h_ffn2 = _exchange_begin(st_ffn2, [_own_slab(s, chip) for s in st_ffn2], True, "scatter_ffn2_start")

    dmix, dya, dyb, dp3, dgt2 = _merge_bwd(dx2, gt2 + h_ffn2[4][0, 0], mix, ya, yb, p3, wfull["w_out"], "merge_bwd")
    gfull["w_out"] = _mm(_flat(merged), _flat(dmix), ta=True, out_dtype=BF16, name="gw_out")
    dp1, dconvw8 = _conv_bwd(dya, wfull["w_conv_out"], p1, convw8, "conv_bwd")
    gfull["w_conv_out"] = _mm(_flat(ya_in), _flat(dya), ta=True, out_dtype=BF16, name="gw_conv_out")
    ds0, dz, ddskip = _ssm_post_bwd(dyb, s0, z, u, d_skip, wfull["w_glu"], wfull["w_ssm_out"], "ssm_post_bwd")
    gfull["w_ssm_out"] = _mm(_flat(s2), _flat(dyb), ta=True, out_dtype=BF16, name="gw_ssm_out")
    gfull["w_glu"] = _mm(_flat(s1), _flat(dz), ta=True, out_dtype=BF16, name="gw_glu")
    du, dwb, dwct, dar8, dai8 = _ssm_bwd(ds0, u, xs, wb, wct, ar8, ai8, d_skip, "ssm_bwd")
    dx1, dsh2, dsc2, dgmix = _dh_norm_bwd([dp1, du, dp3], [win1, winu, win3], x1, g_mix, sc2, dx2, "mix_bwd_dh")
    gfull["w_in"] = _stack_cols([
        _mm(h2f, _flat(dp1), ta=True, out_dtype=BF16, name="gw_in1"), _mm(h2f, _flat(du), ta=True, out_dtype=BF16, name="gw_inu"),
        _mm(h2f, _flat(dp3), ta=True, out_dtype=BF16, name="gw_in3")], BIG[3][2])

    st_mix = stack_group(groups[1])
    h_mix = _exchange_begin(st_mix, [_own_slab(s, chip) for s in st_mix], True, "scatter_mix_start")

    def swap_begin(recv, name):
        return _exchange_begin(recv, [lax.empty(v.shape, v.dtype) for v in recv], SIBLING, name)

    recv_ffn2 = _exchange_end(h_ffn2, dx1, True, "scatter_ffn2_wait")
    sw_ffn2 = swap_begin(recv_ffn2, "swap_ffn2_start")

    ffn1_names = {"w1": BIG[0], "w3": BIG[1], "w2": BIG[2]}
    ffn1_handles = {}

    def emit_ffn1(key, gw):
        n, rows, cols, split = ffn1_names[key]
        st = _stacked_from_full(gw, rows, cols, split)
        ffn1_handles[n] = _exchange_begin([st], [_own_slab(st, chip)], True, f"scatter_ffn1_{key}_start")
        return ffn1_handles[n][4][0, 0]

    grad_x, (dsh1, dsc1, dgt1, dg1), _ = _ffn_backward(
        dx1, ffn1_saved, g_ffn1, sc1, gt1 + (h_mix[4][0, 0] + sw_ffn2[4][0, 0]), wfull["w1_a"], wfull["w3_a"],
        wfull["w2_a"], "ffn1", emit=emit_ffn1)
    recv_mix = _exchange_end(h_mix, grad_x, True, "scatter_mix_wait")
    sw_mix = swap_begin(recv_mix, "swap_mix_start")
    dg1 = dg1 + sw_mix[4][0, 0]

    sbw = GP // SSM_SUPER
    d_are, d_aim, d_ldt, d_bre_t, d_bim_t = _ssm_disc_bwd(
        are, aim, ldt, bre_t, bim_t, jnp.sum(dar8, axis=0, keepdims=True), jnp.sum(dai8, axis=0, keepdims=True),
        _blockdiag_extract(dwb[:, :sbw]), _blockdiag_extract(dwb[:, sbw:]))
    d_cre = _c_from_t(_blockdiag_extract(dwct[:, :sbw]))
    d_cim = -_c_from_t(_blockdiag_extract(dwct[:, sbw:]))

    small_parts = [dg1, dgmix, dg3, dgfin, dconvw8[:3], d_are, d_aim, _from_t(d_bre_t), _from_t(d_bim_t), d_cre, d_cim,
                   jnp.sum(d_ldt.reshape(NG, NP), axis=1), ddskip]
    small_sizes = [int(p.size) for p in small_parts]
    n_small = sum(small_sizes)
    n_small_pad = -(-n_small // (SUB * PACK_COLS)) * (SUB * PACK_COLS)
    dmod = jnp.concatenate([dsh1, dsc1, dgt1, dsh2, dsc2, dgt2, dsh3, dsc3, dgt3], axis=2).reshape(bsz * NMOD * D)
    flat = jnp.concatenate([p.reshape(-1) for p in small_parts] + [jnp.zeros((n_small_pad - n_small,), F32), dmod])
    grads, deltas, new_m, new_v = {}, {}, {}, {}

    def adamw_group(grp, recv_own, recv_sib, token):
        for (n, _, _, _), r_own, r_sib in zip(grp, recv_own, recv_sib):
            wmv, _ = lax.optimization_barrier(
                ((_view(args[n], n), _view(args["m_" + n], n), _view(args["v_" + n], n)), token))
            res = _adamw_big(*wmv, r_own, r_sib, f"adamw_{n}")
            grads[n], deltas[n], new_m[n], new_v[n] = [_view(r, n) for r in res]
        return deltas[grp[-1][0]]

    flat2d = flat.reshape(-1, PACK_COLS)
    h_small = _exchange_begin(None, [_own_slab(jnp.broadcast_to(flat2d[None], (N_CHIPS,) + flat2d.shape), chip)],
                              False, "gather_small_chips_start")
    recv_ffn2, sib_ffn2 = _exchange_end(sw_ffn2, flat2d, SIBLING, "swap_ffn2_wait", with_srcs=True)
    done = adamw_group(groups[2], recv_ffn2, sib_ffn2, h_small[4])
    recv_mix, sib_mix = _exchange_end(sw_mix, done, SIBLING, "swap_mix_wait", with_srcs=True)
    done = adamw_group(groups[1], recv_mix, sib_mix, h_small[4])
    own_c = _exchange_end(h_small, done, False, "gather_small_chips_wait")[0]
    sib_c = _swap_sibling([own_c], "gather_small_sib")[0]

    recv_ffn1 = [_exchange_end(ffn1_handles[n], sib_c, True, f"scatter_ffn1_{n}_wait")[0] for n, _, _, _ in groups[0]]
    sw_ffn1 = swap_begin(recv_ffn1, "swap_ffn1_start")
    (own_c, sib_c), _ = lax.optimization_barrier(((own_c, sib_c), sw_ffn1[4]))
    core0 = jnp.where(mc == 0, own_c, sib_c)
    core1 = jnp.where(mc == 0, sib_c, own_c)
    allg = jnp.stack([core0, core1], axis=1).reshape(N_DEV, -1)
    small = _sum_slabs(allg[:, :n_small_pad].reshape(N_DEV, -1, PACK_COLS), "sum_small").reshape(-1)
    sg, o = [], 0
    for p, sz in zip(small_parts, small_sizes):
        sg.append(small[o:o + sz].reshape(p.shape))
        o += sz
    (g_g1, g_gmix, g_g3, g_gfin, g_convw, g_are, g_aim, g_bre, g_bim, g_cre, g_cim, g_ldt, g_dskip) = sg

    dmod_all = allg[:, n_small_pad:].reshape(nb, NMOD * D)
    dmod_shard = _select(dmod_all.reshape(nb, N_CHIPS, nmod_shard).transpose(1, 0, 2), chip)
    g_wada, g_bada = _ada_bwd(c_all, dmod_shard, dmod_all)

    grads["w_ada"] = g_wada[None]
    deltas["w_ada"], new_m["w_ada"], new_v["w_ada"] = _adamw_rows(w_ada, m_w_ada, v_w_ada, g_wada, "adamw_w_ada")

    g_convw_shard = _select(g_convw.reshape(3, N_CHIPS, CW // N_CHIPS).transpose(1, 0, 2), chip)
    small_g = {"b_ada": g_bada, "g_ffn1": g_g1, "g_mix": g_gmix, "g_ffn2": g_g3, "g_final": g_gfin,
               "conv_w": g_convw_shard, "a_re": g_are, "a_im": g_aim, "b_re": g_bre, "b_im": g_bim,
               "c_re": g_cre, "c_im": g_cim, "log_dt": g_ldt, "d_skip": g_dskip}
    small_names = list(small_g)
    sizes = [int(args[n].size) for n in small_names]
    tot = sum(sizes)
    tot_pad = -(-tot // (SUB * PACK_COLS)) * (SUB * PACK_COLS)

    def pack(get):
        return jnp.concatenate([get(n).reshape(-1) for n in small_names] + [jnp.zeros((tot_pad - tot,), F32)]).reshape(
            -1, PACK_COLS)

    res = _adamw_plain(pack(lambda n: args[n]), pack(lambda n: args["m_" + n]), pack(lambda n: args["v_" + n]),
                       pack(lambda n: small_g[n]), "adamw_small")
    o = 0
    for n, sz in zip(small_names, sizes):
        shp = args[n].shape
        grads[n] = small_g[n].reshape(shp)
        deltas[n], new_m[n], new_v[n] = [r.reshape(-1)[o:o + sz].reshape(shp) for r in res]
        o += sz

    recv_ffn1, sib_ffn1 = _exchange_end(sw_ffn1, res[0], SIBLING, "swap_ffn1_wait", with_srcs=True)
    adamw_group(groups[0], recv_ffn1, sib_ffn1, jnp.zeros((), F32))

    return (loss, grad_x, *[grads[n] for n in names], *[deltas[n] for n in names],
            *[new_m[n] for n in names], *[new_v[n] for n in names])
```

```python
import math

import jax
import jax.numpy as jnp
from jax import lax
from jax.experimental import pallas as pl
from jax.experimental.pallas import tpu as pltpu

F32 = jnp.float32
BF16 = jnp.bfloat16
SDS = jax.ShapeDtypeStruct
MESH = pl.DeviceIdType.MESH

D = 1024
DFF = 2816
CW = 1024
SW = 512
NG, NP, NH = 32, 64, 16
GP = NG * NP
NMOD = 9
EPS = 1e-6
N_CHIPS = 4
N_DEV = 8
SUB = 8
LANE = 128
SSM_SUPER = 4
VMEM_LIMIT = 50 * 1024 * 1024

LR, B1, B2, AEPS, WD, STEP = 0.001, 0.9, 0.999, 1e-08, 0.01, 10
BC1 = 1.0 - B1 ** STEP
BC2 = 1.0 - B2 ** STEP


def _cp(*sem):
    return pltpu.CompilerParams(dimension_semantics=sem or None, vmem_limit_bytes=VMEM_LIMIT)


def _pick_tile(n, cands):
    for t in cands:
        if t <= n and n % t == 0:
            return t
    return n


def _dot(a, b):
    return lax.dot_general(a, b, (((1,), (0,)), ((), ())), preferred_element_type=F32)


def _dot_nt(a, b):
    return lax.dot_general(a, b, (((1,), (1,)), ((), ())), preferred_element_type=F32)


def _dot_tn(a, b):
    return lax.dot_general(a, b, (((0,), (0,)), ((), ())), preferred_element_type=F32)


def _row(tm, width, col=0):
    return pl.BlockSpec((1, tm, width), lambda b, i, *_: (b, i, col))


def _seqvec(width):
    return pl.BlockSpec((1, 1, width), lambda b, *_: (b, 0, 0))


def _full2(shape):
    return pl.BlockSpec(shape, lambda *_: (0, 0))


def _sigmoid(x):
    return jax.nn.sigmoid(x)


def _mm(a, b, *, ta=False, tb=False, out_dtype=F32, name):
    if ta:
        kdim, m = a.shape
    else:
        m, kdim = a.shape
    n = b.shape[0] if tb else b.shape[1]
    tm = _pick_tile(m, (1408, 1024, 512, 256, 128))
    tn = _pick_tile(n, (1408, 1024, 512, 256, 128))
    tk = _pick_tile(kdim, (1024, 512, 256, 128))
    nk = kdim // tk

    def body(a_ref, b_ref, o_ref, acc_ref):
        k = pl.program_id(2)

        @pl.when(k == 0)
        def _():
            acc_ref[...] = jnp.zeros_like(acc_ref)

        av = a_ref[...].astype(BF16)
        bv = b_ref[...].astype(BF16)
        dn = (((0 if ta else 1,), (1 if tb else 0,)), ((), ()))
        acc_ref[...] += lax.dot_general(av, bv, dn, preferred_element_type=F32)

        @pl.when(k == nk - 1)
        def _():
            o_ref[...] = acc_ref[...].astype(out_dtype)

    a_spec = pl.BlockSpec((tk, tm), lambda i, j, k: (k, i)) if ta else pl.BlockSpec((tm, tk), lambda i, j, k: (i, k))
    b_spec = pl.BlockSpec((tn, tk), lambda i, j, k: (j, k)) if tb else pl.BlockSpec((tk, tn), lambda i, j, k: (k, j))
    return pl.pallas_call(
        body, name=name, grid=(m // tm, n // tn, nk),
        in_specs=[a_spec, b_spec],
        out_specs=pl.BlockSpec((tm, tn), lambda i, j, k: (i, j)),
        out_shape=SDS((m, n), out_dtype),
        scratch_shapes=[pltpu.VMEM((tm, tn), F32)],
        compiler_params=_cp("parallel", "parallel", "arbitrary"),
    )(a, b)


def _flat(a):
    return a.reshape(-1, a.shape[-1])


def _norm_mod(x, g, sh, sc, name):
    bsz, seq, dm = x.shape
    tm = _pick_tile(seq, (512, 256, 128))

    def body(x_ref, g_ref, sh_ref, sc_ref, o_ref):
        xf = x_ref[0]
        r = lax.rsqrt(jnp.mean(xf * xf, axis=-1, keepdims=True) + EPS)
        hn = xf * r * g_ref[...]
        o_ref[0] = (hn * (1.0 + sc_ref[0]) + sh_ref[0]).astype(BF16)

    return pl.pallas_call(
        body, name=name, grid=(bsz, seq // tm),
        in_specs=[_row(tm, dm), _full2((1, dm)), _seqvec(dm), _seqvec(dm)],
        out_specs=_row(tm, dm), out_shape=SDS((bsz, seq, dm), BF16),
        compiler_params=_cp("parallel", "parallel"),
    )(x, g, sh, sc)


def _swiglu_up(h, w1, w3, name):
    bsz, seq, dm = h.shape
    nf = w1.shape[0]
    tm = _pick_tile(seq, (512, 256, 128))
    tn = _pick_tile(nf, (1408, 512, 256, 128))

    def body(h_ref, w1_ref, w3_ref, a_ref, b_ref, hid_ref):
        hv = h_ref[0]
        a = _dot_nt(hv, w1_ref[...])
        b = _dot_nt(hv, w3_ref[...])
        sg = _sigmoid(a)
        sa = a * sg
        a_ref[0] = (b * (sg * (1.0 + a * (1.0 - sg)))).astype(BF16)
        b_ref[0] = sa.astype(BF16)
        hid_ref[0] = (sa * b).astype(BF16)

    wspec = pl.BlockSpec((tn, dm), lambda n, b, i: (n, 0))
    ospec = pl.BlockSpec((1, tm, tn), lambda n, b, i: (b, i, n))
    shp = SDS((bsz, seq, nf), BF16)
    return pl.pallas_call(
        body, name=name, grid=(nf // tn, bsz, seq // tm),
        in_specs=[pl.BlockSpec((1, tm, dm), lambda n, b, i: (b, i, 0)), wspec, wspec],
        out_specs=[ospec, ospec, ospec], out_shape=[shp, shp, shp],
        compiler_params=_cp("parallel", "parallel", "parallel"),
    )(h, w1, w3)


def _ffn_down(hid, w2, x, gt, name):
    bsz, seq, nf = hid.shape
    dm = w2.shape[1]
    tm = _pick_tile(seq, (512, 256, 128))

    def body(hid_ref, w2_ref, x_ref, gt_ref, f_ref, xo_ref):
        f = _dot(hid_ref[0], w2_ref[...])
        f_ref[0] = f.astype(BF16)
        xo_ref[0] = x_ref[0] + 0.5 * gt_ref[0] * f

    shp = SDS((bsz, seq, dm), F32)
    return pl.pallas_call(
        body, name=name, grid=(bsz, seq // tm),
        in_specs=[_row(tm, nf), _full2((nf, dm)), _row(tm, dm), _seqvec(dm)],
        out_specs=[_row(tm, dm), _row(tm, dm)], out_shape=[SDS((bsz, seq, dm), BF16), shp],
        compiler_params=_cp("parallel", "parallel"),
    )(hid, w2, x, gt)


def _ffn_bwd_hid(dxo, gt, f, a, b, w2, name):
    bsz, seq, dm = dxo.shape
    nf = a.shape[2]
    tm = _pick_tile(seq, (512, 256, 128))
    tn = _pick_tile(nf, (1408, 512, 256, 128))

    def body(dxo_ref, gt_ref, f_ref, a_ref, b_ref, w2_ref, dfs_ref, da_ref, db_ref, dgt_ref):
        i = pl.program_id(1)
        n = pl.program_id(2)

        @pl.when(n == 0)
        def _():
            dxo = dxo_ref[0]
            dfs_ref[0] = (0.5 * gt_ref[0] * dxo).astype(BF16)
            part = jnp.sum(0.5 * dxo * f_ref[0].astype(F32), axis=0, keepdims=True)

            @pl.when(i == 0)
            def _():
                dgt_ref[0] = part

            @pl.when(i > 0)
            def _():
                dgt_ref[0] += part

        dhid = _dot_nt(dfs_ref[0], w2_ref[pl.ds(pl.multiple_of(n * tn, tn), tn), :])
        dh16 = dhid.astype(BF16)
        da_ref[0] = dh16 * a_ref[0]
        db_ref[0] = dh16 * b_ref[0]

    hspec = pl.BlockSpec((1, tm, tn), lambda b, i, n: (b, i, n))
    return pl.pallas_call(
        body, name=name, grid=(bsz, seq // tm, nf // tn),
        in_specs=[_row(tm, dm), _seqvec(dm), _row(tm, dm), hspec, hspec, _full2((nf, dm))],
        out_specs=[_row(tm, dm), hspec, hspec, _seqvec(dm)],
        out_shape=[SDS((bsz, seq, dm), BF16), SDS((bsz, seq, nf), BF16), SDS((bsz, seq, nf), BF16),
                   SDS((bsz, 1, dm), F32)],
        compiler_params=_cp("arbitrary", "arbitrary", "arbitrary"),
    )(dxo, gt, f, a, b, w2)


def _dh_norm_bwd(pieces, weights, x, g, sc, dxo, name, transposed=False):
    bsz, seq, dm = x.shape
    tm = _pick_tile(seq, (512, 256, 128))
    npc = len(pieces)
    dot = _dot if transposed else _dot_nt

    def body(*refs):
        p_refs = refs[:npc]
        w_hbm = refs[npc:2 * npc]
        x_ref, g_ref, sc_ref, dxo_ref, dx_ref, dsh_ref, dsc_ref, dg_ref = refs[2 * npc:2 * npc + 8]
        w_refs = refs[2 * npc + 8:]
        b = pl.program_id(0)
        i = pl.program_id(1)

        @pl.when((i == 0) & (b == 0))
        def _():
            for src, dst in zip(w_hbm, w_refs):
                pltpu.sync_copy(src, dst)

        dh = dot(p_refs[0][0], w_refs[0][...])
        for j in range(1, npc):
            dh = dh + dot(p_refs[j][0], w_refs[j][...])
        xf = x_ref[0]
        gv = g_ref[...]
        r = lax.rsqrt(jnp.mean(xf * xf, axis=-1, keepdims=True) + EPS)
        xhat = xf * r
        dhn = dh * (1.0 + sc_ref[0])
        p_sh = jnp.sum(dh, axis=0, keepdims=True)
        p_sc = jnp.sum(dh * (xhat * gv), axis=0, keepdims=True)
        p_g = jnp.sum(dhn * xhat, axis=0, keepdims=True)
        dxh = dhn * gv
        dx_ref[0] = dxo_ref[0] + r * (dxh - xhat * jnp.mean(dxh * xhat, axis=-1, keepdims=True))

        @pl.when(i == 0)
        def _():
            dsh_ref[0] = p_sh
            dsc_ref[0] = p_sc

        @pl.when(i > 0)
        def _():
            dsh_ref[0] += p_sh
            dsc_ref[0] += p_sc

        @pl.when((i == 0) & (b == 0))
        def _():
            dg_ref[...] = p_g

        @pl.when((i > 0) | (b > 0))
        def _():
            dg_ref[...] += p_g

    return pl.pallas_call(
        body, name=name, grid=(bsz, seq // tm),
        in_specs=[_row(tm, p.shape[2]) for p in pieces] + [pl.BlockSpec(memory_space=pl.ANY)] * npc + [
            _row(tm, dm), _full2((1, dm)), _seqvec(dm), _row(tm, dm)],
        out_specs=[_row(tm, dm), _seqvec(dm), _seqvec(dm), _full2((1, dm))],
        out_shape=[SDS((bsz, seq, dm), F32), SDS((bsz, 1, dm), F32), SDS((bsz, 1, dm), F32), SDS((1, dm), F32)],
        scratch_shapes=[pltpu.VMEM(w.shape, w.dtype) for w in weights],
        compiler_params=_cp("arbitrary", "arbitrary"),
    )(*pieces, *weights, x, g, sc, dxo)


HALO = 16


def _conv_core(gc, v, gch, vh, w, first):
    cv = gc * v
    halo = jnp.where(first, 0.0, gch * vh)
    ext = jnp.concatenate([halo, cv], axis=0)
    cv1 = pltpu.roll(ext, 1, 0)[HALO:]
    cv2 = pltpu.roll(ext, 2, 0)[HALO:]
    conv = w[0:1] * cv2 + w[1:2] * cv1 + w[2:3] * cv
    return cv, cv1, cv2, conv


def _prev_halo(tm, col):
    return pl.BlockSpec((1, HALO, CW), lambda b, i, *_: (b, jnp.maximum(i * (tm // HALO) - 1, 0), col))


def _next_halo(tm, seq, col):
    return pl.BlockSpec((1, HALO, CW), lambda b, i, *_: (b, jnp.minimum((i + 1) * (tm // HALO), seq // HALO - 1), col))


def _conv_fwd(p1, convw8, name):
    bsz, seq, _ = p1.shape
    tm = _pick_tile(seq, (512, 256, 128))

    def body(gb_ref, gc_ref, v_ref, gch_ref, vh_ref, w_ref, o_ref):
        first = pl.program_id(1) == 0
        _, _, _, conv = _conv_core(gc_ref[0].astype(F32), v_ref[0].astype(F32), gch_ref[0].astype(F32),
                                   vh_ref[0].astype(F32), w_ref[...], first)
        o_ref[0] = (gb_ref[0].astype(F32) * conv).astype(BF16)

    return pl.pallas_call(
        body, name=name, grid=(bsz, seq // tm),
        in_specs=[_row(tm, CW, 0), _row(tm, CW, 1), _row(tm, CW, 2), _prev_halo(tm, 1), _prev_halo(tm, 2),
                  _full2((8, CW))],
        out_specs=_row(tm, CW), out_shape=SDS((bsz, seq, CW), BF16),
        compiler_params=_cp("parallel", "parallel"),
    )(p1, p1, p1, p1, p1, convw8)


def _conv_bwd(dya, wco, p1, convw8, name):
    bsz, seq, _ = p1.shape
    tm = _pick_tile(seq, (512, 256, 128))
    nt = seq // tm
    ext_rows = tm + HALO

    def body(dya_ref, dyan_ref, wco_ref, gb_ref, gbn_ref, gc_ref, v_ref, gch_ref, vh_ref, w_ref, dp_ref, dw_ref):
        b = pl.program_id(0)
        i = pl.program_id(1)
        w = w_ref[...]
        gc = gc_ref[0].astype(F32)
        vv = v_ref[0].astype(F32)
        cv, cv1, cv2, conv = _conv_core(gc, vv, gch_ref[0].astype(F32), vh_ref[0].astype(F32), w, i == 0)
        dya_ext = jnp.concatenate([dya_ref[0], dyan_ref[0]], axis=0)
        dyain_ext = _dot_nt(dya_ext, wco_ref[...])
        gb_ext = jnp.concatenate([gb_ref[0], gbn_ref[0]], axis=0).astype(F32)
        rows = lax.broadcasted_iota(jnp.int32, (ext_rows, 1), 0)
        dconv_ext = jnp.where((rows < tm) | (i < nt - 1), dyain_ext * gb_ext, 0.0)
        dconv = dconv_ext[:tm]
        dconv1 = pltpu.roll(dconv_ext, ext_rows - 1, 0)[:tm]
        dconv2 = pltpu.roll(dconv_ext, ext_rows - 2, 0)[:tm]
        dcv = w[2:3] * dconv + w[1:2] * dconv1 + w[0:1] * dconv2
        dp_ref[0, :, 0:CW] = (dyain_ext[:tm] * conv).astype(BF16)
        dp_ref[0, :, CW:2 * CW] = (dcv * vv).astype(BF16)
        dp_ref[0, :, 2 * CW:3 * CW] = (dcv * gc).astype(BF16)
        g0 = jnp.sum(dconv * cv2, axis=0, keepdims=True)
        g1 = jnp.sum(dconv * cv1, axis=0, keepdims=True)
        g2 = jnp.sum(dconv * cv, axis=0, keepdims=True)
        upd = jnp.concatenate([g0, g1, g2, jnp.zeros((5, CW), F32)], axis=0)

        @pl.when((i == 0) & (b == 0))
        def _():
            dw_ref[...] = upd

        @pl.when((i > 0) | (b > 0))
        def _():
            dw_ref[...] += upd

    return pl.pallas_call(
        body, name=name, grid=(bsz, seq // tm),
        in_specs=[_row(tm, CW), _next_halo(tm, seq, 0), _full2((CW, D)),
                  _row(tm, CW, 0), _next_halo(tm, seq, 0), _row(tm, CW, 1), _row(tm, CW, 2),
                  _prev_halo(tm, 1), _prev_halo(tm, 2), _full2((8, CW))],
        out_specs=[_row(tm, 3 * CW), _full2((8, CW))],
        out_shape=[SDS((bsz, seq, 3 * CW), BF16), SDS((8, CW), F32)],
        compiler_params=_cp("arbitrary", "arbitrary"),
    )(dya, dya, wco, p1, p1, p1, p1, p1, p1, convw8)


def _disc(are, aim, ldt, bre, bim):
    dt = jnp.exp(ldt)
    mag = jnp.exp(are * dt)
    ang = aim * dt
    abr = mag * jnp.cos(ang)
    abi = mag * jnp.sin(ang)
    nr = abr - 1.0
    den = are * are + aim * aim
    cr = (nr * are + abi * aim) / den
    ci = (abi * are - nr * aim) / den
    return abr, abi, cr * bre - ci * bim, cr * bim + ci * bre


def _ssm_disc(are, aim, ldt, bre_t, bim_t):
    def body(are_ref, aim_ref, ldt_ref, bre_ref, bim_ref, abr_ref, abi_ref, bbr_ref, bbi_ref):
        abr, abi, bbr, bbi = _disc(are_ref[...], aim_ref[...], ldt_ref[...], bre_ref[...], bim_ref[...])
        abr_ref[...] = abr
        abi_ref[...] = abi
        bbr_ref[...] = bbr
        bbi_ref[...] = bbi

    v1, vh = SDS((1, GP), F32), SDS((NH, GP), F32)
    return pl.pallas_call(body, name="ssm_disc", out_shape=[v1, v1, vh, vh], compiler_params=_cp())(
        are, aim, ldt, bre_t, bim_t)


def _ssm_disc_bwd(are, aim, ldt, bre_t, bim_t, dabr, dabi, dbbr, dbbi):
    def body(are_ref, aim_ref, ldt_ref, bre_ref, bim_ref, g0, g1, g2, g3, o0, o1, o2, o3, o4):
        prim = (are_ref[...], aim_ref[...], ldt_ref[...], bre_ref[...], bim_ref[...])
        _, vjp = jax.vjp(_disc, *prim)
        d_are, d_aim, d_ldt, d_bre, d_bim = vjp((g0[...], g1[...], g2[...], g3[...]))
        o0[...] = d_are
        o1[...] = d_aim
        o2[...] = d_ldt
        o3[...] = d_bre
        o4[...] = d_bim

    v1, vh = SDS((1, GP), F32), SDS((NH, GP), F32)
    return pl.pallas_call(body, name="ssm_disc_bwd", out_shape=[v1, v1, v1, vh, vh], compiler_params=_cp())(
        are, aim, ldt, bre_t, bim_t, dabr, dabi, dbbr, dbbi)


SCAN_LANE_BLOCKS = 2


def _scan_chunk(buf_ref, nt, bsz, ar, ai, init_r, init_i, reverse):
    nsub = SUB // bsz
    wid = GP // SCAN_LANE_BLOCKS
    row = lax.broadcasted_iota(jnp.int32, (SUB, wid), 0)
    shift = ((SUB - bsz) if reverse else bsz) % SUB
    order = list(range(nsub - 1, -1, -1)) if reverse else list(range(nsub))

    def scan_block(lo):
        cre, cim = slice(lo, lo + wid), slice(GP + lo, GP + lo + wid)
        a_r, a_i = ar[:, lo:lo + wid], ai[:, lo:lo + wid]

        def step(j, carry):
            pr, pi = carry
            jj = (nt - 1 - j) if reverse else j
            off = pl.multiple_of(jj * SUB, SUB)
            br = buf_ref[pl.ds(off, SUB), cre]
            bi = buf_ref[pl.ds(off, SUB), cim]
            nr, ni = pr, pi
            for s in order:
                sr, si = nr, ni
                if shift:
                    sr = pltpu.roll(sr, shift, 0)
                    si = pltpu.roll(si, shift, 0)
                cr = a_r * sr - a_i * si + br
                ci = a_r * si + a_i * sr + bi
                if s == order[0]:
                    nr, ni = cr, ci
                else:
                    m = (row >= s * bsz) & (row < (s + 1) * bsz)
                    nr = jnp.where(m, cr, nr)
                    ni = jnp.where(m, ci, ni)
            buf_ref[pl.ds(off, SUB), cre] = nr
            buf_ref[pl.ds(off, SUB), cim] = ni
            return nr, ni

        return lax.fori_loop(0, nt, step, (init_r[:, lo:lo + wid], init_i[:, lo:lo + wid]))

    lasts = [scan_block(k * wid) for k in range(SCAN_LANE_BLOCKS)]
    return jnp.concatenate([l[0] for l in lasts], axis=1), jnp.concatenate([l[1] for l in lasts], axis=1)


def _ssm_chunk_rows(total_rows, bsz):
    return min(total_rows, 64 * bsz)


def _interleave(src_ref, tmp_ref, bsz, steps):
    nl = tmp_ref.shape[0]
    for b in range(bsz):
        for j in range(nl):
            tmp_ref.at[j][pl.ds(b, steps, stride=bsz), :] = src_ref[b, :, j * LANE:(j + 1) * LANE]
    return jnp.concatenate([tmp_ref[j] for j in range(nl)], axis=1)


def _deinterleave(val, tmp_ref, dst_ref, bsz, steps, skip=None):
    nl = tmp_ref.shape[0]
    for j in range(nl):
        tmp_ref[j] = val[:, j * LANE:(j + 1) * LANE]
    for b in range(bsz):
        for j in range(nl):
            lanes = slice(j * LANE, (j + 1) * LANE)
            v = tmp_ref.at[j][pl.ds(b, steps, stride=bsz), :]
            if skip is not None:
                v = v + skip[0][b, :, lanes] * skip[1][:, lanes]
            dst_ref[b, :, lanes] = v.astype(dst_ref.dtype)


SSM_UB = SW // SSM_SUPER
SSM_SB = GP // SSM_SUPER


def _sb_cols(s, half):
    return slice(half * GP + s * SSM_SB, half * GP + (s + 1) * SSM_SB)


def _ssm_in(v16, w_ref, x_ref):
    for s in range(SSM_SUPER):
        vs = v16[:, s * SSM_UB:(s + 1) * SSM_UB]
        for half in range(2):
            x_ref[:, _sb_cols(s, half)] = _dot(vs, w_ref[s * SSM_UB:(s + 1) * SSM_UB, half * SSM_SB:(half + 1) * SSM_SB])


def _ssm_out(x16, w_ref):
    outs = []
    for s in range(SSM_SUPER):
        rows = slice(s * SSM_UB, (s + 1) * SSM_UB)
        outs.append(_dot_nt(x16[:, _sb_cols(s, 0)], w_ref[rows, 0:SSM_SB])
                    + _dot_nt(x16[:, _sb_cols(s, 1)], w_ref[rows, SSM_SB:2 * SSM_SB]))
    return jnp.concatenate(outs, axis=1)


def _ssm_fwd(u, wb, wct, ar8, ai8, name):
    bsz, seq, _ = u.shape
    rt = seq * bsz
    r = _ssm_chunk_rows(rt, bsz)
    nt = r // SUB
    steps = r // bsz

    def body(u_ref, wb_hbm, wct_hbm, ar_ref, ai_ref, x_ref, y_ref, wb_ref, wct_ref, st_ref, tmp_ref):
        @pl.when(pl.program_id(0) == 0)
        def _():
            pltpu.sync_copy(wb_hbm, wb_ref)
            pltpu.sync_copy(wct_hbm, wct_ref)
            st_ref[...] = jnp.zeros_like(st_ref)

        _ssm_in(_interleave(u_ref, tmp_ref, bsz, steps).astype(BF16), wb_ref, x_ref)
        fr, fi = _scan_chunk(x_ref, nt, bsz, ar_ref[...], ai_ref[...], st_ref[:, 0:GP], st_ref[:, GP:2 * GP], False)
        st_ref[:, 0:GP] = fr
        st_ref[:, GP:2 * GP] = fi
        _deinterleave(_ssm_out(x_ref[...].astype(BF16), wct_ref), tmp_ref, y_ref, bsz, steps)

    anyspec = pl.BlockSpec(memory_space=pl.ANY)
    seqs = pl.BlockSpec((bsz, steps, SW), lambda i: (0, i, 0))
    return pl.pallas_call(
        body, name=name, grid=(rt // r,),
        in_specs=[seqs, anyspec, anyspec, _full2((SUB, GP)), _full2((SUB, GP))],
        out_specs=[pl.BlockSpec((r, 2 * GP), lambda i: (i, 0)), seqs],
        out_shape=[SDS((rt, 2 * GP), F32), SDS((bsz, seq, SW), F32)],
        scratch_shapes=[pltpu.VMEM(wb.shape, BF16), pltpu.VMEM(wct.shape, BF16), pltpu.VMEM((SUB, 2 * GP), F32),
                        pltpu.VMEM((SW // LANE, r, LANE), F32)],
        compiler_params=_cp("arbitrary"),
    )(u, wb, wct, ar8, ai8)


def _ssm_bwd(dy, u, xs, wb, wct, ar8, ai8, dskip, name):
    bsz, seq, _ = u.shape
    rt = seq * bsz
    r = _ssm_chunk_rows(rt, bsz)
    nt = r // SUB
    nc = rt // r
    steps = r // bsz
    ub = SW // SSM_SUPER
    sb = GP // SSM_SUPER

    def body(dys_ref, us_ref, x_ref, xh_ref, wb_hbm, wct_hbm, ar_ref, ai_ref, d_ref,
             dus_ref, dwb_hbm, dwct_hbm, dar_ref, dai_ref, wb_ref, wct_ref, g_ref, st_ref, awb_ref, awct_ref,
             tmp_ref):
        i = pl.program_id(0)
        dyb = _interleave(dys_ref, tmp_ref, bsz, steps).astype(BF16)
        ub16 = _interleave(us_ref, tmp_ref, bsz, steps).astype(BF16)

        @pl.when(i == 0)
        def _():
            pltpu.sync_copy(wb_hbm, wb_ref)
            pltpu.sync_copy(wct_hbm, wct_ref)
            st_ref[...] = jnp.zeros_like(st_ref)
            awb_ref[...] = jnp.zeros_like(awb_ref)
            awct_ref[...] = jnp.zeros_like(awct_ref)
            dar_ref[...] = jnp.zeros_like(dar_ref)
            dai_ref[...] = jnp.zeros_like(dai_ref)

        _ssm_in(dyb, wct_ref, g_ref)
        ar = ar_ref[...]
        ai = ai_ref[...]
        fr, fi = _scan_chunk(g_ref, nt, bsz, ar, -ai, st_ref[:, 0:GP], st_ref[:, GP:2 * GP], True)
        st_ref[:, 0:GP] = fr
        st_ref[:, GP:2 * GP] = fi

        gb = g_ref[...].astype(BF16)
        _deinterleave(_ssm_out(gb, wb_ref), tmp_ref, dus_ref, bsz, steps, skip=(dys_ref, d_ref))
        xb16 = x_ref[...].astype(BF16)
        for s in range(SSM_SUPER):
            us = ub16[:, s * ub:(s + 1) * ub]
            ds = dyb[:, s * ub:(s + 1) * ub]
            for half in range(2):
                cols = slice(half * GP + s * sb, half * GP + (s + 1) * sb)
                ocols = slice(half * sb, (half + 1) * sb)
                awb_ref[s * ub:(s + 1) * ub, ocols] += _dot_tn(us, gb[:, cols])
                awct_ref[s * ub:(s + 1) * ub, ocols] += _dot_tn(ds, xb16[:, cols])

        gr = g_ref[:, 0:GP]
        gi = g_ref[:, GP:2 * GP]
        xsr = pltpu.roll(x_ref[:, 0:GP], bsz, 0)
        xsi = pltpu.roll(x_ref[:, GP:2 * GP], bsz, 0)
        inner = lax.broadcasted_iota(jnp.int32, (r, 1), 0) >= bsz
        t_r = jnp.where(inner, gr * xsr + gi * xsi, 0.0)
        t_i = jnp.where(inner, gi * xsr - gr * xsi, 0.0)
        acc_r = jnp.sum(t_r.reshape(nt, SUB, GP), axis=0)
        acc_i = jnp.sum(t_i.reshape(nt, SUB, GP), axis=0)
        hr = xh_ref[:, 0:GP]
        hi = xh_ref[:, GP:2 * GP]
        if bsz % SUB:
            hr = pltpu.roll(hr, bsz, 0)
            hi = pltpu.roll(hi, bsz, 0)
        edge = (lax.broadcasted_iota(jnp.int32, (SUB, 1), 0) < bsz) & (i < nc - 1)
        g0r = g_ref[0:SUB, 0:GP]
        g0i = g_ref[0:SUB, GP:2 * GP]
        dar_ref[...] += acc_r + jnp.where(edge, g0r * hr + g0i * hi, 0.0)
        dai_ref[...] += acc_i + jnp.where(edge, g0i * hr - g0r * hi, 0.0)

        @pl.when(i == nc - 1)
        def _():
            pltpu.sync_copy(awb_ref, dwb_hbm)
            pltpu.sync_copy(awct_ref, dwct_hbm)

    anyspec = pl.BlockSpec(memory_space=pl.ANY)
    rev = lambda i: (nc - 1 - i, 0)
    seqs = pl.BlockSpec((bsz, steps, SW), lambda i: (0, nc - 1 - i, 0))
    wshape = (SW, 2 * sb)
    return pl.pallas_call(
        body, name=name, grid=(nc,),
        in_specs=[seqs, seqs, pl.BlockSpec((r, 2 * GP), rev),
                  pl.BlockSpec((SUB, 2 * GP), lambda i: (jnp.maximum((nc - 1 - i) * nt - 1, 0), 0)),
                  anyspec, anyspec, _full2((SUB, GP)), _full2((SUB, GP)), _full2((1, SW))],
        out_specs=[seqs, anyspec, anyspec, _full2((SUB, GP)), _full2((SUB, GP))],
        out_shape=[SDS((bsz, seq, SW), BF16), SDS(wshape, F32), SDS(wshape, F32), SDS((SUB, GP), F32),
                   SDS((SUB, GP), F32)],
        scratch_shapes=[pltpu.VMEM(wb.shape, BF16), pltpu.VMEM(wct.shape, BF16),
                        pltpu.VMEM((r, 2 * GP), F32), pltpu.VMEM((SUB, 2 * GP), F32),
                        pltpu.VMEM(wshape, F32), pltpu.VMEM(wshape, F32),
                        pltpu.VMEM((SW // LANE, r, LANE), F32)],
        compiler_params=_cp("arbitrary"),
    )(dy, u, xs, xs, wb, wct, ar8, ai8, dskip)


GELU_C = math.sqrt(2.0 / math.pi)


def _gelu(x):
    return 0.5 * x * (1.0 + jnp.tanh(GELU_C * (x + 0.044715 * x * x * x)))


def _gelu_grad(x):
    th = jnp.tanh(GELU_C * (x + 0.044715 * x * x * x))
    return 0.5 * (1.0 + th) + 0.5 * x * (1.0 - th * th) * GELU_C * (1.0 + 3.0 * 0.044715 * x * x)


def _ssm_post(ys, u, dskip, wglu, wso, name):
    bsz, seq, _ = ys.shape
    tm = _pick_tile(seq, (512, 256, 128))

    def body(ys_ref, u_ref, d_ref, wg_ref, wo_ref, s0_ref, z_ref, s1_ref, s2_ref, yb_ref):
        s0 = ys_ref[0] + d_ref[...] * u_ref[0]
        s1 = _gelu(s0)
        s1b = s1.astype(BF16)
        z = _dot(s1b, wg_ref[...])
        s2b = (s1 * _sigmoid(z)).astype(BF16)
        s0_ref[0] = s0
        z_ref[0] = z
        s1_ref[0] = s1b
        s2_ref[0] = s2b
        yb_ref[0] = _dot(s2b, wo_ref[...]).astype(BF16)

    return pl.pallas_call(
        body, name=name, grid=(bsz, seq // tm),
        in_specs=[_row(tm, SW), _row(tm, SW), _full2((1, SW)), _full2((SW, SW)), _full2((SW, D))],
        out_specs=[_row(tm, SW), _row(tm, SW), _row(tm, SW), _row(tm, SW), _row(tm, D)],
        out_shape=[SDS((bsz, seq, SW), F32), SDS((bsz, seq, SW), F32), SDS((bsz, seq, SW), BF16),
                   SDS((bsz, seq, SW), BF16), SDS((bsz, seq, D), BF16)],
        compiler_params=_cp("parallel", "parallel"),
    )(ys, u, dskip, wglu, wso)


def _ssm_post_bwd(dyb, s0, z, u, dskip, wglu, wso, name):
    bsz, seq, _ = s0.shape
    tm = _pick_tile(seq, (512, 256, 128))

    def body(dyb_ref, s0_ref, z_ref, u_ref, wg_ref, wo_ref, ds0_ref, dz_ref, dd_ref):
        b = pl.program_id(0)
        i = pl.program_id(1)
        ds2 = _dot_nt(dyb_ref[0], wo_ref[...])
        s0 = s0_ref[0]
        s1 = _gelu(s0)
        sg = _sigmoid(z_ref[0])
        dz = ds2 * s1 * sg * (1.0 - sg)
        dzb = dz.astype(BF16)
        ds1 = ds2 * sg + _dot_nt(dzb, wg_ref[...])
        ds0 = ds1 * _gelu_grad(s0)
        ds0_ref[0] = ds0
        dz_ref[0] = dzb
        part = jnp.sum(ds0 * u_ref[0], axis=0, keepdims=True)

        @pl.when((i == 0) & (b == 0))
        def _():
            dd_ref[...] = part

        @pl.when((i > 0) | (b > 0))
        def _():
            dd_ref[...] += part

    del dskip
    return pl.pallas_call(
        body, name=name, grid=(bsz, seq // tm),
        in_specs=[_row(tm, D), _row(tm, SW), _row(tm, SW), _row(tm, SW), _full2((SW, SW)), _full2((SW, D))],
        out_specs=[_row(tm, SW), _row(tm, SW), _full2((1, SW))],
        out_shape=[SDS((bsz, seq, SW), F32), SDS((bsz, seq, SW), BF16), SDS((1, SW), F32)],
        compiler_params=_cp("arbitrary", "arbitrary"),
    )(dyb, s0, z, u, wglu, wso)


def _merge_out(ya, yb, p3, wout, x1, gt, name):
    bsz, seq, _ = ya.shape
    tm = _pick_tile(seq, (512, 256, 128))

    def body(ya_ref, yb_ref, ga_ref, gbb_ref, w_ref, x_ref, gt_ref, mg_ref, mix_ref, xo_ref):
        merged = (_sigmoid(ga_ref[0].astype(F32)) * ya_ref[0].astype(F32)
                  + _sigmoid(gbb_ref[0].astype(F32)) * yb_ref[0].astype(F32)).astype(BF16)
        mix = _dot(merged, w_ref[...])
        mg_ref[0] = merged
        mix_ref[0] = mix.astype(BF16)
        xo_ref[0] = x_ref[0] + gt_ref[0] * mix

    return pl.pallas_call(
        body, name=name, grid=(bsz, seq // tm),
        in_specs=[_row(tm, D), _row(tm, D), _row(tm, D, 0), _row(tm, D, 1), _full2((D, D)), _row(tm, D), _seqvec(D)],
        out_specs=[_row(tm, D), _row(tm, D), _row(tm, D)],
        out_shape=[SDS((bsz, seq, D), BF16), SDS((bsz, seq, D), BF16), SDS((bsz, seq, D), F32)],
        compiler_params=_cp("parallel", "parallel"),
    )(ya, yb, p3, p3, wout, x1, gt)


def _merge_bwd(dx2, gt, mix, ya, yb, p3, wout, name):
    bsz, seq, _ = ya.shape
    tm = _pick_tile(seq, (512, 256, 128))

    def body(dx_ref, gt_ref, mix_ref, ya_ref, yb_ref, ga_ref, gbb_ref, w_ref, dmix_ref, dya_ref, dyb_ref, dp_ref, dgt_ref):
        i = pl.program_id(1)
        dx = dx_ref[0]
        dmix = (gt_ref[0] * dx).astype(BF16)
        dmix_ref[0] = dmix
        part = jnp.sum(dx * mix_ref[0].astype(F32), axis=0, keepdims=True)

        @pl.when(i == 0)
        def _():
            dgt_ref[0] = part

        @pl.when(i > 0)
        def _():
            dgt_ref[0] += part

        dmg = _dot_nt(dmix, w_ref[...])
        sa = _sigmoid(ga_ref[0].astype(F32))
        sb = _sigmoid(gbb_ref[0].astype(F32))
        dya_ref[0] = (dmg * sa).astype(BF16)
        dyb_ref[0] = (dmg * sb).astype(BF16)
        dp_ref[0, :, 0:D] = (dmg * ya_ref[0].astype(F32) * sa * (1.0 - sa)).astype(BF16)
        dp_ref[0, :, D:2 * D] = (dmg * yb_ref[0].astype(F32) * sb * (1.0 - sb)).astype(BF16)

    bshape = SDS((bsz, seq, D), BF16)
    return pl.pallas_call(
        body, name=name, grid=(bsz, seq // tm),
        in_specs=[_row(tm, D), _seqvec(D), _row(tm, D), _row(tm, D), _row(tm, D), _row(tm, D, 0), _row(tm, D, 1),
                  _full2((D, D))],
        out_specs=[_row(tm, D), _row(tm, D), _row(tm, D), _row(tm, 2 * D), _seqvec(D)],
        out_shape=[bshape, bshape, bshape, SDS((bsz, seq, 2 * D), BF16), SDS((bsz, 1, D), F32)],
        compiler_params=_cp("arbitrary", "arbitrary"),
    )(dx2, gt, mix, ya, yb, p3, p3, wout)


def _final_loss(x3, gfin, target, name):
    bsz, seq, dm = x3.shape
    tm = _pick_tile(seq, (512, 256, 128))

    def body(x_ref, g_ref, t_ref, dx_ref, loss_ref, dg_ref):
        b = pl.program_id(0)
        i = pl.program_id(1)
        xf = x_ref[0]
        gv = g_ref[...]
        r = lax.rsqrt(jnp.mean(xf * xf, axis=-1, keepdims=True) + EPS)
        xhat = xf * r
        e = xhat * gv - t_ref[0]
        dy = e * (1.0 / dm)
        dxh = dy * gv
        dx_ref[0] = r * (dxh - xhat * jnp.mean(dxh * xhat, axis=-1, keepdims=True))
        p_l = jnp.sum(e * e, axis=0, keepdims=True) * (0.5 / dm)
        p_g = jnp.sum(dy * xhat, axis=0, keepdims=True)

        @pl.when((i == 0) & (b == 0))
        def _():
            loss_ref[...] = p_l
            dg_ref[...] = p_g

        @pl.when((i > 0) | (b > 0))
        def _():
            loss_ref[...] += p_l
            dg_ref[...] += p_g

    return pl.pallas_call(
        body, name=name, grid=(bsz, seq // tm),
        in_specs=[_row(tm, dm), _full2((1, dm)), _row(tm, dm)],
        out_specs=[_row(tm, dm), _full2((1, dm)), _full2((1, dm))],
        out_shape=[SDS((bsz, seq, dm), F32), SDS((1, dm), F32), SDS((1, dm), F32)],
        compiler_params=_cp("arbitrary", "arbitrary"),
    )(x3, gfin, target)


def _ada_fwd(c_all, w_shard, b_shard):
    nb = c_all.shape[0]
    n = w_shard.shape[2]

    def body(c_ref, w_ref, b_ref, o_ref):
        cv = c_ref[...]
        cond = (cv * _sigmoid(cv)).astype(BF16)
        o_ref[...] = _dot(cond, w_ref[0].astype(BF16)) + b_ref[...]

    return pl.pallas_call(body, name="ada_fwd", out_shape=SDS((nb, n), F32), compiler_params=_cp())(
        c_all, w_shard, b_shard)


def _ada_bwd(c_all, dmod_shard, dmod_all):
    n = dmod_shard.shape[1]

    def body(c_ref, ds_ref, da_ref, gw_ref, gb_ref):
        cv = c_ref[...]
        cond = (cv * _sigmoid(cv)).astype(BF16)
        gw_ref[...] = _dot_tn(cond, ds_ref[...].astype(BF16))
        gb_ref[...] = jnp.sum(da_ref[...], axis=0, keepdims=True)

    return pl.pallas_call(
        body, name="ada_bwd", out_shape=[SDS((D, n), F32), SDS((1, dmod_all.shape[1]), F32)], compiler_params=_cp(),
    )(c_all, dmod_shard, dmod_all)


def _adamw_math(w, g, m, v):
    m = B1 * m + (1.0 - B1) * g
    v = B2 * v + (1.0 - B2) * (g * g)
    delta = -LR * ((m / BC1) / (jnp.sqrt(v / BC2) + AEPS) + WD * w)
    return delta, m, v


def _adamw_big(w, m, v, recv_own, recv_sib, name):
    _, rows, cols = w.shape
    tr = _pick_tile(rows, tuple(t for t in (512, 256, 128, 64, 32, 16, 8) if t * cols <= 192 * 1024))

    def body(w_ref, m_ref, v_ref, a_ref, b_ref, g_ref, d_ref, mo_ref, vo_ref):
        def chip_sum(r):
            acc = r[0].astype(F32)
            for k in range(1, N_CHIPS):
                acc = acc + r[k].astype(F32)
            return acc

        g = chip_sum(a_ref) + chip_sum(b_ref)
        delta, mn, vn = _adamw_math(w_ref[0], g, m_ref[0], v_ref[0])
        g_ref[0] = g
        d_ref[0] = delta
        mo_ref[0] = mn
        vo_ref[0] = vn

    own = pl.BlockSpec((1, tr, cols), lambda i: (0, i, 0))
    rspec = pl.BlockSpec((N_CHIPS, tr, cols), lambda i: (0, i, 0))
    shp = SDS(w.shape, F32)
    return pl.pallas_call(
        body, name=name, grid=(rows // tr,),
        in_specs=[own, own, own, rspec, rspec], out_specs=[own, own, own, own], out_shape=[shp, shp, shp, shp],
        compiler_params=_cp("parallel"),
    )(w, m, v, recv_own, recv_sib)


def _adamw_plain(w, m, v, g, name):
    def body(w_ref, m_ref, v_ref, g_ref, d_ref, mo_ref, vo_ref):
        delta, mn, vn = _adamw_math(w_ref[...], g_ref[...], m_ref[...], v_ref[...])
        d_ref[...] = delta
        mo_ref[...] = mn
        vo_ref[...] = vn

    shp = SDS(w.shape, F32)
    return pl.pallas_call(body, name=name, out_shape=[shp, shp, shp], compiler_params=_cp())(w, m, v, g)


def _adamw_rows(w, m, v, g, name):
    _, rows, cols = w.shape
    tr = _pick_tile(rows, (128, 64, 32, 16, 8))

    def body(w_ref, m_ref, v_ref, g_ref, d_ref, mo_ref, vo_ref):
        delta, mn, vn = _adamw_math(w_ref[0], g_ref[...], m_ref[0], v_ref[0])
        d_ref[0] = delta
        mo_ref[0] = mn
        vo_ref[0] = vn

    spec = pl.BlockSpec((1, tr, cols), lambda i: (0, i, 0))
    shp = SDS(w.shape, F32)
    return pl.pallas_call(
        body, name=name, grid=(rows // tr,), in_specs=[spec] * 3 + [pl.BlockSpec((tr, cols), lambda i: (i, 0))],
        out_specs=[spec] * 3, out_shape=[shp] * 3, compiler_params=_cp("parallel"),
    )(w, m, v, g)


def _sum_slabs(r, name):
    n, rows, cols = r.shape
    tr = _pick_tile(rows, (256, 128, 64))

    def body(r_ref, o_ref):
        acc = r_ref[0].astype(F32)
        for j in range(1, n):
            acc = acc + r_ref[j].astype(F32)
        o_ref[...] = acc

    return pl.pallas_call(
        body, name=name, grid=(rows // tr,),
        in_specs=[pl.BlockSpec((n, tr, cols), lambda i: (0, i, 0))],
        out_specs=pl.BlockSpec((tr, cols), lambda i: (i, 0)), out_shape=SDS((rows, cols), F32),
        compiler_params=_cp("parallel"),
    )(r)


def _place():
    return lax.axis_index("x"), lax.axis_index("y"), lax.axis_index("c")


def _all_gather8(blk, name):
    m_per, n = blk.shape

    def body(x_ref, out_ref, send_sems, recv_sems, local_sem):
        x, y, c = _place()
        me, sibling = (x, y, c), (x, y, 1 - c)
        chips = [(1 - x, y), (x, 1 - y), (1 - x, 1 - y)]

        def rows(px, py, pc):
            return out_ref.at[pl.ds((4 * px + 2 * py + pc) * m_per, m_per), :]

        def copy(k, block, to, src=None):
            return pltpu.make_async_remote_copy(
                src_ref=rows(*block) if src is None else src, dst_ref=rows(*block),
                send_sem=send_sems.at[k], recv_sem=recv_sems.at[k], device_id=to, device_id_type=MESH)

        mine = pltpu.make_async_copy(x_ref, rows(*me), local_sem)
        mine.start()
        first = [copy(0, me, sibling, src=x_ref)]
        first += [copy(1 + j, me, (*chip, c), src=x_ref) for j, chip in enumerate(chips)]
        for cp in first:
            cp.start()
        passed = [copy(4 + j, (*chip, c), sibling) for j, chip in enumerate(chips)]
        for j, chip in enumerate(chips):
            copy(1 + j, (*chip, c), me).wait_recv()
            passed[j].start()
        copy(0, sibling, me).wait_recv()
        for j, chip in enumerate(chips):
            copy(4 + j, (*chip, 1 - c), me).wait_recv()
        for cp in first + passed:
            cp.wait_send()
        mine.wait()

    return pl.pallas_call(
        body, name=name, out_shape=SDS((N_DEV * m_per, n), blk.dtype),
        in_specs=[pl.BlockSpec(memory_space=pltpu.VMEM)], out_specs=pl.BlockSpec(memory_space=pltpu.VMEM),
        scratch_shapes=[pltpu.SemaphoreType.DMA((7,)), pltpu.SemaphoreType.DMA((7,)), pltpu.SemaphoreType.DMA],
        compiler_params=pltpu.CompilerParams(vmem_limit_bytes=VMEM_LIMIT),
    )(blk)


def _chip_peers(x, y):
    return [(1 - x, y), (x, 1 - y), (1 - x, 1 - y)]


SIBLING = "sibling"
HALF = "half"


def _forward_halves(lands, name):
    n = len(lands)

    def body(*refs):
        out_refs = refs[n:2 * n]
        send_sems, recv_sems = refs[2 * n:]
        x, y, c = _place()

        def copies(core):
            cps = []
            for a, ref in enumerate(out_refs):
                half = ref.shape[1] // 2
                rows = pl.ds(pl.multiple_of(core * half, 16), half)
                for j, (px, py) in enumerate(_chip_peers(x, y)):
                    cps.append(pltpu.make_async_remote_copy(
                        src_ref=ref.at[2 * px + py, rows], dst_ref=ref.at[2 * px + py, rows],
                        send_sem=send_sems.at[3 * a + j], recv_sem=recv_sems.at[3 * a + j],
                        device_id=(x, y, 1 - c), device_id_type=MESH))
            return cps

        mine = copies(c)
        for cp in mine:
            cp.start()
        for cp in copies(1 - c):
            cp.wait_recv()
        for cp in mine:
            cp.wait_send()

    anyspec = pl.BlockSpec(memory_space=pl.ANY)
    return pl.pallas_call(
        body, name=name, out_shape=[SDS(l.shape, l.dtype) for l in lands], in_specs=[anyspec] * n,
        out_specs=[anyspec] * n, input_output_aliases={i: i for i in range(n)},
        scratch_shapes=[pltpu.SemaphoreType.DMA((3 * n,)), pltpu.SemaphoreType.DMA((3 * n,))],
        compiler_params=pltpu.CompilerParams(vmem_limit_bytes=VMEM_LIMIT),
    )(*lands)


def _peer_copies(src_refs, land_refs, send_sems, recv_sems, scatter, landed):
    x, y, c = _place()
    if scatter == SIBLING:
        return [pltpu.make_async_remote_copy(
            src_ref=s, dst_ref=l, send_sem=send_sems.at[a], recv_sem=recv_sems.at[a],
            device_id=(x, y, 1 - c), device_id_type=MESH) for a, (s, l) in enumerate(zip(src_refs, land_refs))]
    cps = []
    for a, (src_ref, land_ref) in enumerate(zip(src_refs, land_refs)):
        for j, (px, py) in enumerate(_chip_peers(x, y)):
            if scatter == HALF:
                half = land_ref.shape[1] // 2
                rows = pl.ds(pl.multiple_of(c * half, 16), half)
                src = land_ref.at[2 * x + y, rows]
                dst = land_ref.at[2 * px + py, rows] if landed else land_ref.at[2 * x + y, rows]
                cps.append(pltpu.make_async_remote_copy(
                    src_ref=src, dst_ref=dst, send_sem=send_sems.at[3 * a + j], recv_sem=recv_sems.at[3 * a + j],
                    device_id=(px, py, c), device_id_type=MESH))
                continue
            if scatter:
                src = src_ref.at[2 * px + py]
            else:
                src = land_ref.at[2 * x + y] if src_ref is None else src_ref
            dst = land_ref.at[2 * px + py] if landed else land_ref.at[2 * x + y]
            cps.append(pltpu.make_async_remote_copy(
                src_ref=src, dst_ref=dst, send_sem=send_sems.at[3 * a + j], recv_sem=recv_sems.at[3 * a + j],
                device_id=(px, py, c), device_id_type=MESH))
    return cps


def _exchange_chips(srcs, scatter, name):
    n = len(srcs)

    def body(*refs):
        src_refs, land_refs = refs[:n], refs[n:2 * n]
        send_sems, recv_sems, local_sems = refs[2 * n:]
        x, y, _ = _place()
        me = 2 * x + y
        mine = [pltpu.make_async_copy(s.at[me] if scatter else s, l.at[me], local_sems.at[a])
                for a, (s, l) in enumerate(zip(src_refs, land_refs))]
        for cp in mine:
            cp.start()
        out = _peer_copies(src_refs, land_refs, send_sems, recv_sems, scatter, False)
        for cp in out:
            cp.start()
        for cp in _peer_copies(src_refs, land_refs, send_sems, recv_sems, scatter, True):
            cp.wait_recv()
        for cp in out:
            cp.wait_send()
        for cp in mine:
            cp.wait()

    anyspec = pl.BlockSpec(memory_space=pl.ANY)
    shapes = [SDS(s.shape if scatter else (N_CHIPS,) + s.shape, s.dtype) for s in srcs]
    return pl.pallas_call(
        body, name=name, out_shape=shapes, in_specs=[anyspec] * n, out_specs=[anyspec] * n,
        scratch_shapes=[pltpu.SemaphoreType.DMA((3 * n,)), pltpu.SemaphoreType.DMA((3 * n,)),
                        pltpu.SemaphoreType.DMA((n,))],
        compiler_params=pltpu.CompilerParams(vmem_limit_bytes=VMEM_LIMIT),
    )(*srcs)


_HBM = pl.BlockSpec(memory_space=pltpu.HBM)
_SEM = pl.BlockSpec(memory_space=pltpu.SEMAPHORE)
_EFFECT = pltpu.SideEffectType.DATAFLOW_SIDE_EFFECTING


def _exchange_begin(srcs, lands, scatter, name):
    srcs = tuple(srcs or ())
    n, ns = len(lands), len(srcs)
    nsem = n if scatter == SIBLING else 3 * n

    def body(*refs):
        src_refs = refs[:ns] if ns else (None,) * n
        land_refs = refs[ns:ns + n]
        send_sems, recv_sems = refs[ns + n:ns + n + 2]
        token = refs[-1]
        for cp in _peer_copies(src_refs, land_refs, send_sems, recv_sems, scatter, False):
            cp.start()
        token[...] = jnp.zeros_like(token)

    ops = (*srcs, *lands)
    res = pl.pallas_call(
        body, name=name,
        out_shape=(pltpu.SemaphoreType.DMA((nsem,)), pltpu.SemaphoreType.DMA((nsem,)),
                   *[pltpu.HBM(a.shape, a.dtype) for a in ops], SDS((SUB, LANE), F32)),
        in_specs=[_HBM] * len(ops), out_specs=(_SEM, _SEM, *[_HBM] * len(ops), pl.BlockSpec(memory_space=pltpu.VMEM)),
        input_output_aliases={i: 2 + i for i in range(len(ops))},
        compiler_params=pltpu.CompilerParams(has_side_effects=_EFFECT),
    )(*[pltpu.with_memory_space_constraint(a, pltpu.HBM) for a in ops])
    return res[0], res[1], res[2:2 + ns], res[2 + ns:2 + ns + n], res[-1]


def _exchange_end(handle, after, scatter, name, with_srcs=False):
    send_sems, recv_sems, srcs, lands, _ = handle
    n, ns = len(lands), len(srcs)

    def body(*refs):
        src_refs = refs[:ns] if ns else (None,) * n
        land_refs = refs[ns:ns + n]
        send_sems, recv_sems = refs[ns + n:ns + n + 2]
        for cp in _peer_copies(src_refs, land_refs, send_sems, recv_sems, scatter, True):
            cp.wait_send()
            cp.wait_recv()

    ops = (*srcs, *lands)
    res = pl.pallas_call(
        body, name=name,
        out_shape=tuple(pltpu.HBM(a.shape, a.dtype) for a in ops),
        in_specs=[_HBM] * len(ops) + [_SEM, _SEM, pl.BlockSpec(memory_space=pl.ANY)], out_specs=tuple([_HBM] * len(ops)),
        input_output_aliases={i: i for i in range(len(ops))},
        compiler_params=pltpu.CompilerParams(has_side_effects=_EFFECT),
    )(*ops, send_sems, recv_sems, after)
    return (list(res[:ns]), list(res[ns:])) if with_srcs else list(res[ns:])


def _own_slab(stack4, chip):
    idx = lax.broadcasted_iota(jnp.int32, (N_CHIPS,) + (1,) * (stack4.ndim - 1), 0)
    return jnp.where(idx == chip, stack4, jnp.zeros((), stack4.dtype))


def _swap_sibling(vs, name):
    n = len(vs)

    def body(*refs):
        in_refs, out_refs = refs[:n], refs[n:2 * n]
        send_sems, recv_sems = refs[2 * n:]
        x, y, c = _place()
        cps = [pltpu.make_async_remote_copy(
            src_ref=i, dst_ref=o, send_sem=send_sems.at[a], recv_sem=recv_sems.at[a],
            device_id=(x, y, 1 - c), device_id_type=MESH) for a, (i, o) in enumerate(zip(in_refs, out_refs))]
        for cp in cps:
            cp.start()
        for cp in cps:
            cp.wait()

    anyspec = pl.BlockSpec(memory_space=pl.ANY)
    return pl.pallas_call(
        body, name=name, out_shape=[SDS(v.shape, v.dtype) for v in vs], in_specs=[anyspec] * n, out_specs=[anyspec] * n,
        scratch_shapes=[pltpu.SemaphoreType.DMA((n,)), pltpu.SemaphoreType.DMA((n,))],
        compiler_params=pltpu.CompilerParams(vmem_limit_bytes=VMEM_LIMIT),
    )(*vs)


def _select(stacked, idx):
    out = stacked[0]
    for j in range(1, stacked.shape[0]):
        out = jnp.where(idx == j, stacked[j], out)
    return out


BIG = (
    ("w1_a", DFF // 4, D, False), ("w3_a", DFF // 4, D, False), ("w2_a", DFF // 4, D, False),
    ("w_in", D, 5632 // 4, True), ("w_conv_out", CW // 4, D, False), ("w_glu", SW // 4, SW, False),
    ("w_ssm_out", SW, D // 4, True), ("w_out", D // 4, D, False),
    ("w1_b", DFF // 4, D, False), ("w3_b", DFF // 4, D, False), ("w2_b", DFF // 4, D, False),
)
TRANSPOSED = frozenset(("w1_a", "w3_a", "w1_b", "w3_b"))
PACK_COLS = 1024


def _view(a, name):
    return jnp.transpose(a, (0, 2, 1)) if name in TRANSPOSED else a


def _full_from_stacked(st, split_cols):
    _, rows, cols = st.shape
    if split_cols:
        return st.transpose(1, 0, 2).reshape(rows, N_CHIPS * cols)
    return st.reshape(N_CHIPS * rows, cols)


def _stacked_from_full(full, rows, cols, split_cols):
    if split_cols:
        return full.reshape(rows, N_CHIPS, cols).transpose(1, 0, 2)
    return full.reshape(N_CHIPS, rows, cols)


def _cols_from_stacked(st, c0, c1):
    w = st.shape[2]
    parts = [st[j][:, max(c0, j * w) - j * w:min(c1, (j + 1) * w) - j * w]
             for j in range(N_CHIPS) if max(c0, j * w) < min(c1, (j + 1) * w)]
    return parts[0] if len(parts) == 1 else jnp.concatenate(parts, axis=1)


def _stack_cols(pieces, w):
    offs = [sum(p.shape[1] for p in pieces[:i]) for i in range(len(pieces))]
    shards = []
    for j in range(N_CHIPS):
        lo, hi = j * w, (j + 1) * w
        parts = [p[:, max(lo, o) - o:min(hi, o + p.shape[1]) - o]
                 for p, o in zip(pieces, offs) if max(lo, o) < min(hi, o + p.shape[1])]
        shards.append(parts[0] if len(parts) == 1 else jnp.concatenate(parts, axis=1))
    return jnp.stack(shards)


def _blockdiag(t):
    r = lax.broadcasted_iota(jnp.int32, (SW, GP), 0) // NH
    cidx = lax.broadcasted_iota(jnp.int32, (SW, GP), 1) // NP
    dense = jnp.where(r == cidx, jnp.tile(t, (NG, 1)), 0.0)
    ub, sb = SW // SSM_SUPER, GP // SSM_SUPER
    return jnp.concatenate([dense[s * ub:(s + 1) * ub, s * sb:(s + 1) * sb] for s in range(SSM_SUPER)], axis=0)


def _blockdiag_extract(acc):
    gs = NG // SSM_SUPER
    a = acc.reshape(NG, NH, gs, NP)
    sel = (lax.broadcasted_iota(jnp.int32, (NG, 1, gs, 1), 0) % gs) == lax.broadcasted_iota(jnp.int32, (NG, 1, gs, 1), 2)
    a = jnp.sum(jnp.where(sel, a, 0.0), axis=2)
    return a.transpose(1, 0, 2).reshape(NH, GP)


def _to_t(p):
    return p.transpose(2, 0, 1).reshape(NH, GP)


def _from_t(t):
    return t.reshape(NH, NG, NP).transpose(1, 2, 0)


def _c_to_t(p):
    return p.transpose(1, 0, 2).reshape(NH, GP)


def _c_from_t(t):
    return t.reshape(NH, NG, NP).transpose(1, 0, 2)


def _ffn_forward(x, g, sh, sc, gt, w1, w3, w2, tag):
    h = _norm_mod(x, g, sh, sc, f"{tag}_norm")
    if callable(w1):
        w1, w3 = w1(h)
    a, b, hid = _swiglu_up(h, w1, w3, f"{tag}_up")
    w2 = w2(hid) if callable(w2) else w2
    f, xo = _ffn_down(hid, w2, x, gt, f"{tag}_down")
    return xo, (x, h, a, b, hid, f), w2


def _ffn_backward(dxo, saved, g, sc, gt, w1, w3, w2, tag, emit=lambda key, gw: 0.0):
    x, h, a, b, hid, f = saved
    dfs, da, db, dgt = _ffn_bwd_hid(dxo, gt, f, a, b, w2, f"{tag}_bwd_hid")
    h2 = _flat(h)
    gw2 = _mm(_flat(hid), _flat(dfs), ta=True, out_dtype=BF16, name=f"{tag}_gw2")
    tok = emit("w2", gw2)
    gw1 = _mm(_flat(da), h2, ta=True, out_dtype=BF16, name=f"{tag}_gw1")
    tok = tok + emit("w1", gw1)
    gw3 = _mm(_flat(db), h2, ta=True, out_dtype=BF16, name=f"{tag}_gw3")
    tok = tok + emit("w3", gw3)
    dx, dsh, dsc, dg = _dh_norm_bwd([da, db], [w1, w3], x, g, sc + tok, dxo, f"{tag}_bwd_dh", transposed=True)
    return dx, (dsh, dsc, dgt, dg), (gw1, gw3, gw2)


def kernel(x, c, w_ada, b_ada, g_ffn1, w1_a, w3_a, w2_a, g_mix, w_in, conv_w, w_conv_out, a_re, a_im, b_re, b_im, c_re, c_im, log_dt, d_skip, w_glu, w_ssm_out, w_out, g_ffn2, w1_b, w3_b, w2_b, g_final, loss_target, m_w_ada, m_b_ada, m_g_ffn1, m_w1_a, m_w3_a, m_w2_a, m_g_mix, m_w_in, m_conv_w, m_w_conv_out, m_a_re, m_a_im, m_b_re, m_b_im, m_c_re, m_c_im, m_log_dt, m_d_skip, m_w_glu, m_w_ssm_out, m_w_out, m_g_ffn2, m_w1_b, m_w3_b, m_w2_b, m_g_final, v_w_ada, v_b_ada, v_g_ffn1, v_w1_a, v_w3_a, v_w2_a, v_g_mix, v_w_in, v_conv_w, v_w_conv_out, v_a_re, v_a_im, v_b_re, v_b_im, v_c_re, v_c_im, v_log_dt, v_d_skip, v_w_glu, v_w_ssm_out, v_w_out, v_g_ffn2, v_w1_b, v_w3_b, v_w2_b, v_g_final):
    args = dict(locals())
    names = ["w_ada", "b_ada", "g_ffn1", "w1_a", "w3_a", "w2_a", "g_mix", "w_in", "conv_w", "w_conv_out", "a_re",
             "a_im", "b_re", "b_im", "c_re", "c_im", "log_dt", "d_skip", "w_glu", "w_ssm_out", "w_out", "g_ffn2",
             "w1_b", "w3_b", "w2_b", "g_final"]
    bsz, seq, _ = x.shape
    mx, my, mc = _place()
    chip = 2 * mx + my
    dev = 4 * mx + 2 * my + mc

    groups = (BIG[:3], BIG[3:8], BIG[8:])
    wfull = {}

    def shards(grp):
        return [_view(args[n], n)[0] for n, _, _, _ in grp]

    def unpack_group(gathered, grp):
        for (n, _, _, split), st in zip(grp, gathered):
            wfull[n] = st if n == "w_in" else _full_from_stacked(st, split)

    up_grp, down_grp = groups[0][:2], groups[0][2:]

    nmod_shard = NMOD * D // N_CHIPS
    c_all = _all_gather8(c.reshape(SUB, -1), "gather_c").reshape(N_DEV * bsz, D)
    b_shard = _select(b_ada.reshape(N_CHIPS, 1, nmod_shard), chip)
    mod_shard = _ada_fwd(c_all, w_ada, b_shard)
    nb = N_DEV * bsz
    cw_pad = jnp.pad(conv_w[0], ((0, SUB - 3), (0, nmod_shard - CW // N_CHIPS)))
    mod_st = _exchange_chips([jnp.concatenate([mod_shard, cw_pad], axis=0)], False, "gather_mod")[0]
    mod_all = mod_st[:, :nb].transpose(1, 0, 2).reshape(N_DEV, bsz, NMOD * D)
    mod = _select(mod_all, dev)

    def gather_begin(raw, name, mode=False):
        lands = [_own_slab(jnp.broadcast_to(a.astype(BF16)[None], (N_CHIPS,) + a.shape), chip) for a in raw]
        return _exchange_begin(None, lands, mode, name)

    def gather_end_halves(handle, after, name):
        return _forward_halves(_exchange_end(handle, after, HALF, f"{name}_wait"), f"{name}_forward")

    up_raw, mod = lax.optimization_barrier((shards(up_grp), mod))
    up_handle = gather_begin(up_raw, "gather_w_ffn1_up_start", HALF)
    (down_raw, mix_raw, ffn2_raw), up_token = lax.optimization_barrier(
        ((shards(down_grp), shards(groups[1]), shards(groups[2])), up_handle[4][0:1, 0:1]))
    down_handle = gather_begin(down_raw, "gather_w_ffn1_down_start", HALF)
    mix_handle = gather_begin(mix_raw, "gather_w_mix_start", HALF)
    ffn2_handle = gather_begin(ffn2_raw, "gather_w_ffn2_start")
    start_tokens = up_token + down_handle[4][0:1, 0:1] + mix_handle[4][0:1, 0:1] + ffn2_handle[4][0:1, 0:1]

    sh1, sc1, gt1, sh2, sc2, gt2, sh3, sc3, gt3 = [mod[:, None, j * D:(j + 1) * D] for j in range(NMOD)]
    convw = mod_st[:, nb:nb + 3, :CW // N_CHIPS].transpose(1, 0, 2).reshape(3, CW)
    convw8 = jnp.pad(convw, ((0, SUB - 3), (0, 0)))

    are, aim = a_re.reshape(1, GP), a_im.reshape(1, GP)
    ldt = jnp.broadcast_to(log_dt.reshape(NG, 1), (NG, NP)).reshape(1, GP)
    bre_t, bim_t = _to_t(b_re[0]), _to_t(b_im[0])
    abr, abi, bbr_t, bbi_t = _ssm_disc(are, aim, ldt, bre_t, bim_t)
    wb = jnp.concatenate([_blockdiag(bbr_t), _blockdiag(bbi_t)], axis=1).astype(BF16)
    wct = jnp.concatenate([_blockdiag(_c_to_t(c_re[0])), _blockdiag(-_c_to_t(c_im[0]))], axis=1).astype(BF16)
    ar8 = jnp.broadcast_to(abr, (SUB, GP))
    ai8 = jnp.broadcast_to(abi, (SUB, GP))

    def late_w2a(hid):
        unpack_group(gather_end_halves(down_handle, hid, "gather_w_ffn1_down"), down_grp)
        return wfull["w2_a"]

    def late_w13a(h):
        unpack_group(gather_end_halves(up_handle, h, "gather_w_ffn1_up"), up_grp)
        return wfull["w1_a"], wfull["w3_a"]

    x1, ffn1_saved, _ = _ffn_forward(x, g_ffn1 + start_tokens, sh1, sc1, gt1, late_w13a, None, late_w2a, "ffn1")
    unpack_group(gather_end_halves(mix_handle, x1, "gather_w_mix"), groups[1])

    h2 = _norm_mod(x1, g_mix, sh2, sc2, "mix_norm")
    h2f = _flat(h2)
    win_st = wfull["w_in"]
    win1 = _cols_from_stacked(win_st, 0, 3 * CW)
    winu = _cols_from_stacked(win_st, 3 * CW, 3 * CW + SW)
    win3 = _cols_from_stacked(win_st, 3 * CW + SW, 3 * CW + SW + 2 * D)
    p1 = _mm(h2f, win1, out_dtype=BF16, name="mix_in1").reshape(bsz, seq, 3 * CW)
    u = _mm(h2f, winu, name="mix_inu").reshape(bsz, seq, SW)
    p3 = _mm(h2f, win3, out_dtype=BF16, name="mix_in3").reshape(bsz, seq, 2 * D)

    ya_in = _conv_fwd(p1, convw8, "conv_fwd")
    ya = _mm(_flat(ya_in), wfull["w_conv_out"], out_dtype=BF16, name="conv_out").reshape(bsz, seq, D)

    xs, ys = _ssm_fwd(u, wb, wct, ar8, ai8, "ssm_fwd")
    s0, z, s1, s2, yb = _ssm_post(ys, u, d_skip, wfull["w_glu"], wfull["w_ssm_out"], "ssm_post")

    merged, mix, x2 = _merge_out(ya, yb, p3, wfull["w_out"], x1, gt2, "merge_out")
    unpack_group(_exchange_end(ffn2_handle, x2, False, "gather_w_ffn2_wait"), groups[2])
    x3, ffn2_saved, _ = _ffn_forward(x2, g_ffn2, sh3, sc3, gt3, wfull["w1_b"], wfull["w3_b"], wfull["w2_b"], "ffn2")

    dx3, lossvec, dgfin = _final_loss(x3, g_final.reshape(1, D), loss_target, "final_loss")
    loss = lax.psum(jnp.sum(lossvec), ("x", "y", "c"))

    gfull = {}
    dx2, (dsh3, dsc3, dgt3, dg3), (gfull["w1_b"], gfull["w3_b"], gfull["w2_b"]) = _ffn_backward(
        dx3, ffn2_saved, g_ffn2, sc3, gt3, wfull["w1_b"], wfull["w3_b"], wfull["w2_b"], "ffn2")

    def stack_group(grp):
        return [gfull[n] if gfull[n].ndim == 3 else _stacked_from_full(gfull[n], rows, cols, split)
                for n, rows, cols, split in grp]

    st_ffn2 = stack_group(groups[2])
    h_ffn2 = _exchange_begin(st_ffn2, [_own_slab(s, chip) for s in st_ffn2], True, "scatter_ffn2_start")

    dmix, dya, dyb, dp3, dgt2 = _merge_bwd(dx2, gt2 + h_ffn2[4][0, 0], mix, ya, yb, p3, wfull["w_out"], "merge_bwd")
    gfull["w_out"] = _mm(_flat(merged), _flat(dmix), ta=True, out_dtype=BF16, name="gw_out")
    dp1, dconvw8 = _conv_bwd(dya, wfull["w_conv_out"], p1, convw8, "conv_bwd")
    gfull["w_conv_out"] = _mm(_flat(ya_in), _flat(dya), ta=True, out_dtype=BF16, name="gw_conv_out")
    ds0, dz, ddskip = _ssm_post_bwd(dyb, s0, z, u, d_skip, wfull["w_glu"], wfull["w_ssm_out"], "ssm_post_bwd")
    gfull["w_ssm_out"] = _mm(_flat(s2), _flat(dyb), ta=True, out_dtype=BF16, name="gw_ssm_out")
    gfull["w_glu"] = _mm(_flat(s1), _flat(dz), ta=True, out_dtype=BF16, name="gw_glu")
    du, dwb, dwct, dar8, dai8 = _ssm_bwd(ds0, u, xs, wb, wct, ar8, ai8, d_skip, "ssm_bwd")
    dx1, dsh2, dsc2, dgmix = _dh_norm_bwd([dp1, du, dp3], [win1, winu, win3], x1, g_mix, sc2, dx2, "mix_bwd_dh")
    gfull["w_in"] = _stack_cols([
        _mm(h2f, _flat(dp1), ta=True, out_dtype=BF16, name="gw_in1"), _mm(h2f, _flat(du), ta=True, out_dtype=BF16, name="gw_inu"),
        _mm(h2f, _flat(dp3), ta=True, out_dtype=BF16, name="gw_in3")], BIG[3][2])

    st_mix = stack_group(groups[1])
    h_mix = _exchange_begin(st_mix, [_own_slab(s, chip) for s in st_mix], True, "scatter_mix_start")

    def swap_begin(recv, name):
        return _exchange_begin(recv, [lax.empty(v.shape, v.dtype) for v in recv], SIBLING, name)

    recv_ffn2 = _exchange_end(h_ffn2, dx1, True, "scatter_ffn2_wait")
    sw_ffn2 = swap_begin(recv_ffn2, "swap_ffn2_start")

    ffn1_names = {"w1": BIG[0], "w3": BIG[1], "w2": BIG[2]}
    ffn1_handles = {}

    def emit_ffn1(key, gw):
        n, rows, cols, split = ffn1_names[key]
        st = _stacked_from_full(gw, rows, cols, split)
        ffn1_handles[n] = _exchange_begin([st], [_own_slab(st, chip)], True, f"scatter_ffn1_{key}_start")
        return ffn1_handles[n][4][0, 0]

    grad_x, (dsh1, dsc1, dgt1, dg1), _ = _ffn_backward(
        dx1, ffn1_saved, g_ffn1, sc1, gt1 + (h_mix[4][0, 0] + sw_ffn2[4][0, 0]), wfull["w1_a"], wfull["w3_a"],
        wfull["w2_a"], "ffn1", emit=emit_ffn1)
    recv_mix = _exchange_end(h_mix, grad_x, True, "scatter_mix_wait")
    sw_mix = swap_begin(recv_mix, "swap_mix_start")
    dg1 = dg1 + sw_mix[4][0, 0]

    sbw = GP // SSM_SUPER
    d_are, d_aim, d_ldt, d_bre_t, d_bim_t = _ssm_disc_bwd(
        are, aim, ldt, bre_t, bim_t, jnp.sum(dar8, axis=0, keepdims=True), jnp.sum(dai8, axis=0, keepdims=True),
        _blockdiag_extract(dwb[:, :sbw]), _blockdiag_extract(dwb[:, sbw:]))
    d_cre = _c_from_t(_blockdiag_extract(dwct[:, :sbw]))
    d_cim = -_c_from_t(_blockdiag_extract(dwct[:, sbw:]))

    small_parts = [dg1, dgmix, dg3, dgfin, dconvw8[:3], d_are, d_aim, _from_t(d_bre_t), _from_t(d_bim_t), d_cre, d_cim,
                   jnp.sum(d_ldt.reshape(NG, NP), axis=1), ddskip]
    small_sizes = [int(p.size) for p in small_parts]
    n_small = sum(small_sizes)
    n_small_pad = -(-n_small // (SUB * PACK_COLS)) * (SUB * PACK_COLS)
    dmod = jnp.concatenate([dsh1, dsc1, dgt1, dsh2, dsc2, dgt2, dsh3, dsc3, dgt3], axis=2).reshape(bsz * NMOD * D)
    flat = jnp.concatenate([p.reshape(-1) for p in small_parts] + [jnp.zeros((n_small_pad - n_small,), F32), dmod])
    grads, deltas, new_m, new_v = {}, {}, {}, {}

    def adamw_group(grp, recv_own, recv_sib, token):
        for (n, _, _, _), r_own, r_sib in zip(grp, recv_own, recv_sib):
            wmv, _ = lax.optimization_barrier(
                ((_view(args[n], n), _view(args["m_" + n], n), _view(args["v_" + n], n)), token))
            res = _adamw_big(*wmv, r_own, r_sib, f"adamw_{n}")
            grads[n], deltas[n], new_m[n], new_v[n] = [_view(r, n) for r in res]
        return deltas[grp[-1][0]]

    flat2d = flat.reshape(-1, PACK_COLS)
    h_small = _exchange_begin(None, [_own_slab(jnp.broadcast_to(flat2d[None], (N_CHIPS,) + flat2d.shape), chip)],
                              False, "gather_small_chips_start")
    recv_ffn2, sib_ffn2 = _exchange_end(sw_ffn2, flat2d, SIBLING, "swap_ffn2_wait", with_srcs=True)
    done = adamw_group(groups[2], recv_ffn2, sib_ffn2, h_small[4])
    recv_mix, sib_mix = _exchange_end(sw_mix, done, SIBLING, "swap_mix_wait", with_srcs=True)
    done = adamw_group(groups[1], recv_mix, sib_mix, h_small[4])
    own_c = _exchange_end(h_small, done, False, "gather_small_chips_wait")[0]
    sib_c = _swap_sibling([own_c], "gather_small_sib")[0]

    recv_ffn1 = [_exchange_end(ffn1_handles[n], sib_c, True, f"scatter_ffn1_{n}_wait")[0] for n, _, _, _ in groups[0]]
    sw_ffn1 = swap_begin(recv_ffn1, "swap_ffn1_start")
    (own_c, sib_c), _ = lax.optimization_barrier(((own_c, sib_c), sw_ffn1[4]))
    core0 = jnp.where(mc == 0, own_c, sib_c)
    core1 = jnp.where(mc == 0, sib_c, own_c)
    allg = jnp.stack([core0, core1], axis=1).reshape(N_DEV, -1)
    small = _sum_slabs(allg[:, :n_small_pad].reshape(N_DEV, -1, PACK_COLS), "sum_small").reshape(-1)
    sg, o = [], 0
    for p, sz in zip(small_parts, small_sizes):
        sg.append(small[o:o + sz].reshape(p.shape))
        o += sz
    (g_g1, g_gmix, g_g3, g_gfin, g_convw, g_are, g_aim, g_bre, g_bim, g_cre, g_cim, g_ldt, g_dskip) = sg

    dmod_all = allg[:, n_small_pad:].reshape(nb, NMOD * D)
    dmod_shard = _select(dmod_all.reshape(nb, N_CHIPS, nmod_shard).transpose(1, 0, 2), chip)
    g_wada, g_bada = _ada_bwd(c_all, dmod_shard, dmod_all)

    grads["w_ada"] = g_wada[None]
    deltas["w_ada"], new_m["w_ada"], new_v["w_ada"] = _adamw_rows(w_ada, m_w_ada, v_w_ada, g_wada, "adamw_w_ada")

    g_convw_shard = _select(g_convw.reshape(3, N_CHIPS, CW // N_CHIPS).transpose(1, 0, 2), chip)
    small_g = {"b_ada": g_bada, "g_ffn1": g_g1, "g_mix": g_gmix, "g_ffn2": g_g3, "g_final": g_gfin,
               "conv_w": g_convw_shard, "a_re": g_are, "a_im": g_aim, "b_re": g_bre, "b_im": g_bim,
               "c_re": g_cre, "c_im": g_cim, "log_dt": g_ldt, "d_skip": g_dskip}
    small_names = list(small_g)
    sizes = [int(args[n].size) for n in small_names]
    tot = sum(sizes)
    tot_pad = -(-tot // (SUB * PACK_COLS)) * (SUB * PACK_COLS)

    def pack(get):
        return jnp.concatenate([get(n).reshape(-1) for n in small_names] + [jnp.zeros((tot_pad - tot,), F32)]).reshape(
            -1, PACK_COLS)

    res = _adamw_plain(pack(lambda n: args[n]), pack(lambda n: args["m_" + n]), pack(lambda n: args["v_" + n]),
                       pack(lambda n: small_g[n]), "adamw_small")
    o = 0
    for n, sz in zip(small_names, sizes):
        shp = args[n].shape
        grads[n] = small_g[n].reshape(shp)
        deltas[n], new_m[n], new_v[n] = [r.reshape(-1)[o:o + sz].reshape(shp) for r in res]
        o += sz

    recv_ffn1, sib_ffn1 = _exchange_end(sw_ffn1, res[0], SIBLING, "swap_ffn1_wait", with_srcs=True)
    adamw_group(groups[0], recv_ffn1, sib_ffn1, jnp.zeros((), F32))

    return (loss, grad_x, *[grads[n] for n in names], *[deltas[n] for n in names],
            *[new_m[n] for n in names], *[new_v[n] for n in names])
```
